```python
import math
import jax, jax.numpy as jnp
from jax import lax
import numpy as np

D_MODEL = 1024
BATCH = 8
SEQ = 8192
DEPTH = 1

D_MIX = D_MODEL
DIL_HEAD_DIM = 64
DIL_HEADS = (D_MIX // 2) // DIL_HEAD_DIM
DIL_WIDTH = DIL_HEADS * DIL_HEAD_DIM
DIL_BRANCHES = ((128, 1), (512, 4), (2048, 16))
MLA_NOPE = 128
MLA_ROPE = 64
MLA_QK_DIM = MLA_NOPE + MLA_ROPE
MLA_V_DIM = 128
MLA_HEADS = (D_MIX - DIL_WIDTH) // MLA_V_DIM
MLA_WIDTH = MLA_HEADS * MLA_V_DIM
MLA_Q_RANK = 256
MLA_KV_RANK = 128
ROPE_BASE = 10000.0
REL_BUCKETS = 32
REL_MAX_DIST = 2048
D_FF = 2816
FFN_RESID = 0.5
Q_BLOCK = 128
EPS = 1e-6
IN_COLS = 3 * DIL_WIDTH + MLA_Q_RANK + MLA_KV_RANK + MLA_ROPE

kernel_name = "hymba_dilated_mla_macaron"


def _rms(x, g):
    xf = x.astype(jnp.float32)
    y = xf * lax.rsqrt(jnp.mean(xf * xf, axis=-1, keepdims=True) + EPS)
    return (y * g.astype(jnp.float32)).astype(x.dtype)


def _swiglu(x, g, w_gate, w_up, w_down):
    h = _rms(x, g)
    return (jax.nn.silu(h @ w_gate) * (h @ w_up)) @ w_down


def _t5_bucket(dist):
    max_exact = REL_BUCKETS // 2
    d = np.maximum(dist, 1).astype(np.float32)
    large = max_exact + (np.log(d / max_exact) / np.log(REL_MAX_DIST / max_exact)
                         * (REL_BUCKETS - max_exact)).astype(np.int32)
    large = np.minimum(large, REL_BUCKETS - 1)
    return np.where(dist < max_exact, dist, large).astype(np.int32)


def _dilated_branch(q, k, v, rel_bias, window, dilation):
    B, S, H, hd = q.shape
    L = S // dilation
    W = window // dilation
    Bq = math.gcd(L, Q_BLOCK)
    nb = L // Bq

    def to_sub(t):
        return t.reshape(B, L, dilation, H, hd).transpose(0, 2, 3, 1, 4)

    qs = to_sub(q).reshape(B, dilation, H, nb, Bq, hd)
    pad = ((0, 0), (0, 0), (0, 0), (W, 0), (0, 0))
    ks = jnp.pad(to_sub(k), pad)
    vs = jnp.pad(to_sub(v), pad)
    idx = np.arange(nb)[:, None] * Bq + np.arange(Bq + W)[None, :]
    kb = ks[:, :, :, idx]
    vb = vs[:, :, :, idx]
    logits = jnp.einsum('brhnqc,brhnkc->brhnqk', qs, kb) * (hd ** -0.5)

    i = np.arange(Bq)[:, None]
    j = np.arange(Bq + W)[None, :]
    delta = i + W - j
    key_sub = idx[:, None, :] - W
    valid = (delta >= 0) & (delta <= W) & (key_sub >= 0)
    bucket = _t5_bucket(np.clip(delta, 0, None) * dilation)
    bias = jnp.take(rel_bias.astype(jnp.float32), jnp.asarray(bucket), axis=1)
    logits = logits + bias[None, None, :, None]
    logits = jnp.where(jnp.asarray(valid)[None, None, None], logits, -jnp.inf)
    lse = jax.nn.logsumexp(logits, axis=-1)
    p = jnp.exp(logits - lse[..., None])
    o = jnp.einsum('brhnqk,brhnkc->brhnqc', p, vb)
    o = o.reshape(B, dilation, H, L, hd).transpose(0, 3, 1, 2, 4).reshape(B, S, H, hd)
    lse = lse.reshape(B, dilation, H, L).transpose(0, 3, 1, 2).reshape(B, S, H)
    return o, lse


def _dilated_attention(q, k, v, q_g, k_g, rel_bias):
    B, S = q.shape[:2]
    sh = (B, S, DIL_HEADS, DIL_HEAD_DIM)
    q = _rms(q.reshape(sh).astype(jnp.float32), q_g)
    k = _rms(k.reshape(sh).astype(jnp.float32), k_g)
    v = v.reshape(sh).astype(jnp.float32)
    outs, lses = [], []
    for window, dilation in DIL_BRANCHES:
        o, lse = _dilated_branch(q, k, v, rel_bias, window, dilation)
        outs.append(o)
        lses.append(lse)
    alpha = jax.nn.softmax(jnp.stack(lses, 0), axis=0)
    o = jnp.sum(alpha[..., None] * jnp.stack(outs, 0), axis=0)
    return o.reshape(B, S, DIL_WIDTH)


def _rope(x, pos):
    dim = x.shape[-1]
    inv_freq = ROPE_BASE ** (-jnp.arange(0, dim, 2, dtype=jnp.float32) / dim)
    ang = pos[:, None] * inv_freq[None, :]
    cos = jnp.cos(ang)[None, :, None, :]
    sin = jnp.sin(ang)[None, :, None, :]
    x1, x2 = x[..., : dim // 2], x[..., dim // 2:]
    return jnp.concatenate([x1 * cos - x2 * sin, x2 * cos + x1 * sin], axis=-1)


def _mla(cq, ckv, k_pe, q_a_norm, w_q_b, kv_a_norm, w_kv_b, q_g, k_g):
    B, S = cq.shape[:2]
    H = MLA_HEADS
    pos = jnp.arange(S, dtype=jnp.float32)
    q = (_rms(cq, q_a_norm) @ w_q_b).reshape(B, S, H, MLA_QK_DIM).astype(jnp.float32)
    kv = (_rms(ckv, kv_a_norm) @ w_kv_b).reshape(B, S, H, MLA_NOPE + MLA_V_DIM).astype(jnp.float32)
    k_nope, v = kv[..., :MLA_NOPE], kv[..., MLA_NOPE:]
    k_pe = jnp.broadcast_to(k_pe.astype(jnp.float32)[:, :, None, :], (B, S, H, MLA_ROPE))
    k = jnp.concatenate([k_nope, k_pe], axis=-1)
    q = _rms(q, q_g)
    k = _rms(k, k_g)
    q = jnp.concatenate([q[..., :MLA_NOPE], _rope(q[..., MLA_NOPE:], pos)], axis=-1)
    k = jnp.concatenate([k[..., :MLA_NOPE], _rope(k[..., MLA_NOPE:], pos)], axis=-1)

    nb = S // Q_BLOCK
    qb = (q * MLA_QK_DIM ** -0.5).transpose(0, 2, 1, 3).reshape(B, H, nb, Q_BLOCK, MLA_QK_DIM)
    qb = qb.transpose(2, 0, 1, 3, 4)
    kt = k.transpose(0, 2, 1, 3)
    vt = v.transpose(0, 2, 1, 3)
    key_pos = jnp.arange(S)

    def block(args):
        q_blk, n = args
        logits = jnp.einsum('bhqc,bhkc->bhqk', q_blk, kt)
        q_pos = n * Q_BLOCK + jnp.arange(Q_BLOCK)
        mask = key_pos[None, :] <= q_pos[:, None]
        p = jax.nn.softmax(jnp.where(mask, logits, -jnp.inf), axis=-1)
        return jnp.einsum('bhqk,bhkc->bhqc', p, vt)

    o = lax.map(block, (qb, jnp.arange(nb)))
    return o.transpose(1, 0, 3, 2, 4).reshape(B, S, MLA_WIDTH)


def _fwd_setup_inputs(seed: int = 0) -> dict:
    key = jax.random.key(seed)
    ks = jax.random.split(key, 24)
    f32 = jnp.float32

    def w(k, shape, fan_in):
        return jax.random.normal(k, (DEPTH,) + shape, f32) * fan_in ** -0.5

    def g(k, dim):
        return 1.0 + 0.02 * jax.random.normal(k, (DEPTH, dim), f32)

    return {
        "x": jax.random.normal(ks[0], (BATCH, SEQ, D_MODEL), f32),
        "ffn1_norm": g(ks[1], D_MODEL),
        "ffn1_w_gate": w(ks[2], (D_MODEL, D_FF), D_MODEL),
        "ffn1_w_up": w(ks[3], (D_MODEL, D_FF), D_MODEL),
        "ffn1_w_down": w(ks[4], (D_FF, D_MODEL), D_FF),
        "mix_norm": g(ks[5], D_MODEL),
        "w_in": w(ks[6], (D_MODEL, IN_COLS), D_MODEL),
        "dil_q_norm": g(ks[7], DIL_HEAD_DIM),
        "dil_k_norm": g(ks[8], DIL_HEAD_DIM),
        "rel_bias": 0.2 * jax.random.normal(ks[9], (DIL_HEADS, REL_BUCKETS), f32),
        "mla_q_a_norm": g(ks[10], MLA_Q_RANK),
        "mla_w_q_b": w(ks[11], (MLA_Q_RANK, MLA_HEADS * MLA_QK_DIM), MLA_Q_RANK),
        "mla_kv_a_norm": g(ks[12], MLA_KV_RANK),
        "mla_w_kv_b": w(ks[13], (MLA_KV_RANK, MLA_HEADS * (MLA_NOPE + MLA_V_DIM)), MLA_KV_RANK),
        "mla_q_norm": g(ks[14], MLA_QK_DIM),
        "mla_k_norm": g(ks[15], MLA_QK_DIM),
        "out_norm_dil": g(ks[16], DIL_WIDTH),
        "out_norm_mla": g(ks[17], MLA_WIDTH),
        "w_out": w(ks[18], (D_MIX, D_MODEL), D_MIX),
        "ffn2_norm": g(ks[19], D_MODEL),
        "ffn2_w_gate": w(ks[20], (D_MODEL, D_FF), D_MODEL),
        "ffn2_w_up": w(ks[21], (D_MODEL, D_FF), D_MODEL),
        "ffn2_w_down": w(ks[22], (D_FF, D_MODEL), D_FF),
    }


def _fwd_reference(x, ffn1_norm, ffn1_w_gate, ffn1_w_up, ffn1_w_down, mix_norm, w_in,
              dil_q_norm, dil_k_norm, rel_bias, mla_q_a_norm, mla_w_q_b, mla_kv_a_norm,
              mla_w_kv_b, mla_q_norm, mla_k_norm, out_norm_dil, out_norm_mla, w_out,
              ffn2_norm, ffn2_w_gate, ffn2_w_up, ffn2_w_down):
    splits = np.cumsum([DIL_WIDTH, DIL_WIDTH, DIL_WIDTH, MLA_Q_RANK, MLA_KV_RANK])
    for l in range(DEPTH):
        x = x + FFN_RESID * _swiglu(x, ffn1_norm[l], ffn1_w_gate[l], ffn1_w_up[l], ffn1_w_down[l])
        h = _rms(x, mix_norm[l])
        proj = h @ w_in[l]
        q_a, k_a, v_a, cq, ckv, k_pe = jnp.split(proj, splits, axis=-1)
        o_dil = _dilated_attention(q_a, k_a, v_a, dil_q_norm[l], dil_k_norm[l], rel_bias)
        o_mla = _mla(cq, ckv, k_pe, mla_q_a_norm[l], mla_w_q_b[l], mla_kv_a_norm[l],
                     mla_w_kv_b[l], mla_q_norm[l], mla_k_norm[l])
        o = jnp.concatenate([_rms(o_dil, out_norm_dil[l]), _rms(o_mla, out_norm_mla[l])], axis=-1)
        x = x + o.astype(x.dtype) @ w_out[l]
        x = x + FFN_RESID * _swiglu(x, ffn2_norm[l], ffn2_w_gate[l], ffn2_w_up[l], ffn2_w_down[l])
    return x


import jax as _jax
import jax.numpy as _jnp

TWIN_FORMAT = 'train_step'
FWD_PARAMS = ['x', 'ffn1_norm', 'ffn1_w_gate', 'ffn1_w_up', 'ffn1_w_down', 'mix_norm', 'w_in', 'dil_q_norm', 'dil_k_norm', 'rel_bias', 'mla_q_a_norm', 'mla_w_q_b', 'mla_kv_a_norm', 'mla_w_kv_b', 'mla_q_norm', 'mla_k_norm', 'out_norm_dil', 'out_norm_mla', 'w_out', 'ffn2_norm', 'ffn2_w_gate', 'ffn2_w_up', 'ffn2_w_down']
TWIN_WEIGHTS = ['ffn1_norm', 'ffn1_w_gate', 'ffn1_w_up', 'ffn1_w_down', 'mix_norm', 'w_in', 'dil_q_norm', 'dil_k_norm', 'rel_bias', 'mla_q_a_norm', 'mla_w_q_b', 'mla_kv_a_norm', 'mla_w_kv_b', 'mla_q_norm', 'mla_k_norm', 'out_norm_dil', 'out_norm_mla', 'w_out', 'ffn2_norm', 'ffn2_w_gate', 'ffn2_w_up', 'ffn2_w_down']
TWIN_DIFF_INPUT = 'x'
TWIN_INPUTS = ['x', 'ffn1_norm', 'ffn1_w_gate', 'ffn1_w_up', 'ffn1_w_down', 'mix_norm', 'w_in', 'dil_q_norm', 'dil_k_norm', 'rel_bias', 'mla_q_a_norm', 'mla_w_q_b', 'mla_kv_a_norm', 'mla_w_kv_b', 'mla_q_norm', 'mla_k_norm', 'out_norm_dil', 'out_norm_mla', 'w_out', 'ffn2_norm', 'ffn2_w_gate', 'ffn2_w_up', 'ffn2_w_down', 'loss_target', 'm_ffn1_norm', 'm_ffn1_w_gate', 'm_ffn1_w_up', 'm_ffn1_w_down', 'm_mix_norm', 'm_w_in', 'm_dil_q_norm', 'm_dil_k_norm', 'm_rel_bias', 'm_mla_q_a_norm', 'm_mla_w_q_b', 'm_mla_kv_a_norm', 'm_mla_w_kv_b', 'm_mla_q_norm', 'm_mla_k_norm', 'm_out_norm_dil', 'm_out_norm_mla', 'm_w_out', 'm_ffn2_norm', 'm_ffn2_w_gate', 'm_ffn2_w_up', 'm_ffn2_w_down', 'v_ffn1_norm', 'v_ffn1_w_gate', 'v_ffn1_w_up', 'v_ffn1_w_down', 'v_mix_norm', 'v_w_in', 'v_dil_q_norm', 'v_dil_k_norm', 'v_rel_bias', 'v_mla_q_a_norm', 'v_mla_w_q_b', 'v_mla_kv_a_norm', 'v_mla_w_kv_b', 'v_mla_q_norm', 'v_mla_k_norm', 'v_out_norm_dil', 'v_out_norm_mla', 'v_w_out', 'v_ffn2_norm', 'v_ffn2_w_gate', 'v_ffn2_w_up', 'v_ffn2_w_down']
TWIN_OUTPUTS = ['loss', 'grad_x', 'grad_ffn1_norm', 'grad_ffn1_w_gate', 'grad_ffn1_w_up', 'grad_ffn1_w_down', 'grad_mix_norm', 'grad_w_in', 'grad_dil_q_norm', 'grad_dil_k_norm', 'grad_rel_bias', 'grad_mla_q_a_norm', 'grad_mla_w_q_b', 'grad_mla_kv_a_norm', 'grad_mla_w_kv_b', 'grad_mla_q_norm', 'grad_mla_k_norm', 'grad_out_norm_dil', 'grad_out_norm_mla', 'grad_w_out', 'grad_ffn2_norm', 'grad_ffn2_w_gate', 'grad_ffn2_w_up', 'grad_ffn2_w_down', 'delta_ffn1_norm', 'delta_ffn1_w_gate', 'delta_ffn1_w_up', 'delta_ffn1_w_down', 'delta_mix_norm', 'delta_w_in', 'delta_dil_q_norm', 'delta_dil_k_norm', 'delta_rel_bias', 'delta_mla_q_a_norm', 'delta_mla_w_q_b', 'delta_mla_kv_a_norm', 'delta_mla_w_kv_b', 'delta_mla_q_norm', 'delta_mla_k_norm', 'delta_out_norm_dil', 'delta_out_norm_mla', 'delta_w_out', 'delta_ffn2_norm', 'delta_ffn2_w_gate', 'delta_ffn2_w_up', 'delta_ffn2_w_down', 'new_m_ffn1_norm', 'new_m_ffn1_w_gate', 'new_m_ffn1_w_up', 'new_m_ffn1_w_down', 'new_m_mix_norm', 'new_m_w_in', 'new_m_dil_q_norm', 'new_m_dil_k_norm', 'new_m_rel_bias', 'new_m_mla_q_a_norm', 'new_m_mla_w_q_b', 'new_m_mla_kv_a_norm', 'new_m_mla_w_kv_b', 'new_m_mla_q_norm', 'new_m_mla_k_norm', 'new_m_out_norm_dil', 'new_m_out_norm_mla', 'new_m_w_out', 'new_m_ffn2_norm', 'new_m_ffn2_w_gate', 'new_m_ffn2_w_up', 'new_m_ffn2_w_down', 'new_v_ffn1_norm', 'new_v_ffn1_w_gate', 'new_v_ffn1_w_up', 'new_v_ffn1_w_down', 'new_v_mix_norm', 'new_v_w_in', 'new_v_dil_q_norm', 'new_v_dil_k_norm', 'new_v_rel_bias', 'new_v_mla_q_a_norm', 'new_v_mla_w_q_b', 'new_v_mla_kv_a_norm', 'new_v_mla_w_kv_b', 'new_v_mla_q_norm', 'new_v_mla_k_norm', 'new_v_out_norm_dil', 'new_v_out_norm_mla', 'new_v_w_out', 'new_v_ffn2_norm', 'new_v_ffn2_w_gate', 'new_v_ffn2_w_up', 'new_v_ffn2_w_down']
TWIN_LEAF_KINDS = {'loss': 'loss', 'grad_x': 'grad_x', 'grad_ffn1_norm': 'grad_w', 'grad_ffn1_w_gate': 'grad_w', 'grad_ffn1_w_up': 'grad_w', 'grad_ffn1_w_down': 'grad_w', 'grad_mix_norm': 'grad_w', 'grad_w_in': 'grad_w', 'grad_dil_q_norm': 'grad_w', 'grad_dil_k_norm': 'grad_w', 'grad_rel_bias': 'grad_w', 'grad_mla_q_a_norm': 'grad_w', 'grad_mla_w_q_b': 'grad_w', 'grad_mla_kv_a_norm': 'grad_w', 'grad_mla_w_kv_b': 'grad_w', 'grad_mla_q_norm': 'grad_w', 'grad_mla_k_norm': 'grad_w', 'grad_out_norm_dil': 'grad_w', 'grad_out_norm_mla': 'grad_w', 'grad_w_out': 'grad_w', 'grad_ffn2_norm': 'grad_w', 'grad_ffn2_w_gate': 'grad_w', 'grad_ffn2_w_up': 'grad_w', 'grad_ffn2_w_down': 'grad_w', 'delta_ffn1_norm': 'delta_w', 'delta_ffn1_w_gate': 'delta_w', 'delta_ffn1_w_up': 'delta_w', 'delta_ffn1_w_down': 'delta_w', 'delta_mix_norm': 'delta_w', 'delta_w_in': 'delta_w', 'delta_dil_q_norm': 'delta_w', 'delta_dil_k_norm': 'delta_w', 'delta_rel_bias': 'delta_w', 'delta_mla_q_a_norm': 'delta_w', 'delta_mla_w_q_b': 'delta_w', 'delta_mla_kv_a_norm': 'delta_w', 'delta_mla_w_kv_b': 'delta_w', 'delta_mla_q_norm': 'delta_w', 'delta_mla_k_norm': 'delta_w', 'delta_out_norm_dil': 'delta_w', 'delta_out_norm_mla': 'delta_w', 'delta_w_out': 'delta_w', 'delta_ffn2_norm': 'delta_w', 'delta_ffn2_w_gate': 'delta_w', 'delta_ffn2_w_up': 'delta_w', 'delta_ffn2_w_down': 'delta_w', 'new_m_ffn1_norm': 'new_m', 'new_m_ffn1_w_gate': 'new_m', 'new_m_ffn1_w_up': 'new_m', 'new_m_ffn1_w_down': 'new_m', 'new_m_mix_norm': 'new_m', 'new_m_w_in': 'new_m', 'new_m_dil_q_norm': 'new_m', 'new_m_dil_k_norm': 'new_m', 'new_m_rel_bias': 'new_m', 'new_m_mla_q_a_norm': 'new_m', 'new_m_mla_w_q_b': 'new_m', 'new_m_mla_kv_a_norm': 'new_m', 'new_m_mla_w_kv_b': 'new_m', 'new_m_mla_q_norm': 'new_m', 'new_m_mla_k_norm': 'new_m', 'new_m_out_norm_dil': 'new_m', 'new_m_out_norm_mla': 'new_m', 'new_m_w_out': 'new_m', 'new_m_ffn2_norm': 'new_m', 'new_m_ffn2_w_gate': 'new_m', 'new_m_ffn2_w_up': 'new_m', 'new_m_ffn2_w_down': 'new_m', 'new_v_ffn1_norm': 'new_v', 'new_v_ffn1_w_gate': 'new_v', 'new_v_ffn1_w_up': 'new_v', 'new_v_ffn1_w_down': 'new_v', 'new_v_mix_norm': 'new_v', 'new_v_w_in': 'new_v', 'new_v_dil_q_norm': 'new_v', 'new_v_dil_k_norm': 'new_v', 'new_v_rel_bias': 'new_v', 'new_v_mla_q_a_norm': 'new_v', 'new_v_mla_w_q_b': 'new_v', 'new_v_mla_kv_a_norm': 'new_v', 'new_v_mla_w_kv_b': 'new_v', 'new_v_mla_q_norm': 'new_v', 'new_v_mla_k_norm': 'new_v', 'new_v_out_norm_dil': 'new_v', 'new_v_out_norm_mla': 'new_v', 'new_v_w_out': 'new_v', 'new_v_ffn2_norm': 'new_v', 'new_v_ffn2_w_gate': 'new_v', 'new_v_ffn2_w_up': 'new_v', 'new_v_ffn2_w_down': 'new_v'}


def _forward(args):
    return _fwd_reference(*[args[k] for k in FWD_PARAMS])


def _output_shape():
    def fwd():
        inp = _fwd_setup_inputs(0)
        return _fwd_reference(*[inp[k] for k in FWD_PARAMS])
    out = _jax.eval_shape(fwd)
    return out.shape, out.dtype

N_MICROBATCH = 1
ADAM_LR = 0.001
ADAM_B1 = 0.9
ADAM_B2 = 0.999
ADAM_EPS = 1e-08
ADAM_WD = 0.01
ADAM_STEP = 10
PER_EXAMPLE_BATCH_AXIS = {'x': 0, 'loss_target': 0}
SHARED_INPUTS = []
_WEIGHT_DTYPES = {'ffn1_norm': _jnp.float32, 'ffn1_w_gate': _jnp.float32, 'ffn1_w_up': _jnp.float32, 'ffn1_w_down': _jnp.float32, 'mix_norm': _jnp.float32, 'w_in': _jnp.float32, 'dil_q_norm': _jnp.float32, 'dil_k_norm': _jnp.float32, 'rel_bias': _jnp.float32, 'mla_q_a_norm': _jnp.float32, 'mla_w_q_b': _jnp.float32, 'mla_kv_a_norm': _jnp.float32, 'mla_w_kv_b': _jnp.float32, 'mla_q_norm': _jnp.float32, 'mla_k_norm': _jnp.float32, 'out_norm_dil': _jnp.float32, 'out_norm_mla': _jnp.float32, 'w_out': _jnp.float32, 'ffn2_norm': _jnp.float32, 'ffn2_w_gate': _jnp.float32, 'ffn2_w_up': _jnp.float32, 'ffn2_w_down': _jnp.float32}
MOMENT_SCALE = {'ffn1_norm': 1.194096e+01, 'ffn1_w_gate': 3.076626e-01, 'ffn1_w_up': 3.216342e-01, 'ffn1_w_down': 5.393547e-01, 'mix_norm': 2.174245e+00, 'w_in': 1.200745e+00, 'dil_q_norm': 1.314561e+00, 'dil_k_norm': 1.324048e+00, 'rel_bias': 5.817679e-01, 'mla_q_a_norm': 3.651625e+00, 'mla_w_q_b': 1.731065e+00, 'mla_kv_a_norm': 1.054678e+01, 'mla_w_kv_b': 2.410910e+00, 'mla_q_norm': 3.760533e+00, 'mla_k_norm': 3.798581e+00, 'out_norm_dil': 6.338797e+01, 'out_norm_mla': 6.507770e+01, 'w_out': 2.395477e+00, 'ffn2_norm': 1.208792e+01, 'ffn2_w_gate': 1.309764e-01, 'ffn2_w_up': 1.843364e-01, 'ffn2_w_down': 2.945457e-01}


def _to_microbatches(a, axis):
    t = _jnp.moveaxis(a, axis, 0)
    t = t.reshape((N_MICROBATCH, t.shape[0] // N_MICROBATCH) + t.shape[1:])
    return _jnp.moveaxis(t, 1, axis + 1)


def setup_inputs(seed: int = 0) -> dict:
    inp = _fwd_setup_inputs(seed)
    key = _jax.random.fold_in(_jax.random.key(seed), 7919)
    shape, _ = _output_shape()
    out = dict(inp)
    out["loss_target"] = _jax.random.normal(_jax.random.fold_in(key, 0), shape, _jnp.float32)
    for i, name in enumerate(TWIN_WEIGHTS):
        w = inp[name].astype(_jnp.float32)
        if MOMENT_SCALE is None:
            s = _jnp.sqrt(_jnp.mean(_jnp.square(w)) + 1e-30)
        else:
            s = MOMENT_SCALE[name]
        km, kv = _jax.random.split(_jax.random.fold_in(key, i + 1))
        out[name] = w
        out["m_" + name] = s * _jax.random.normal(km, w.shape, _jnp.float32)
        out["v_" + name] = (s * s) * _jax.random.uniform(kv, w.shape, _jnp.float32, 0.5, 1.5)
    if N_MICROBATCH > 1:
        for name, axis in PER_EXAMPLE_BATCH_AXIS.items():
            out[name] = _to_microbatches(out[name], axis)
    return {'x': out['x'], 'ffn1_norm': out['ffn1_norm'], 'ffn1_w_gate': out['ffn1_w_gate'], 'ffn1_w_up': out['ffn1_w_up'], 'ffn1_w_down': out['ffn1_w_down'], 'mix_norm': out['mix_norm'], 'w_in': out['w_in'], 'dil_q_norm': out['dil_q_norm'], 'dil_k_norm': out['dil_k_norm'], 'rel_bias': out['rel_bias'], 'mla_q_a_norm': out['mla_q_a_norm'], 'mla_w_q_b': out['mla_w_q_b'], 'mla_kv_a_norm': out['mla_kv_a_norm'], 'mla_w_kv_b': out['mla_w_kv_b'], 'mla_q_norm': out['mla_q_norm'], 'mla_k_norm': out['mla_k_norm'], 'out_norm_dil': out['out_norm_dil'], 'out_norm_mla': out['out_norm_mla'], 'w_out': out['w_out'], 'ffn2_norm': out['ffn2_norm'], 'ffn2_w_gate': out['ffn2_w_gate'], 'ffn2_w_up': out['ffn2_w_up'], 'ffn2_w_down': out['ffn2_w_down'], 'loss_target': out['loss_target'], 'm_ffn1_norm': out['m_ffn1_norm'], 'm_ffn1_w_gate': out['m_ffn1_w_gate'], 'm_ffn1_w_up': out['m_ffn1_w_up'], 'm_ffn1_w_down': out['m_ffn1_w_down'], 'm_mix_norm': out['m_mix_norm'], 'm_w_in': out['m_w_in'], 'm_dil_q_norm': out['m_dil_q_norm'], 'm_dil_k_norm': out['m_dil_k_norm'], 'm_rel_bias': out['m_rel_bias'], 'm_mla_q_a_norm': out['m_mla_q_a_norm'], 'm_mla_w_q_b': out['m_mla_w_q_b'], 'm_mla_kv_a_norm': out['m_mla_kv_a_norm'], 'm_mla_w_kv_b': out['m_mla_w_kv_b'], 'm_mla_q_norm': out['m_mla_q_norm'], 'm_mla_k_norm': out['m_mla_k_norm'], 'm_out_norm_dil': out['m_out_norm_dil'], 'm_out_norm_mla': out['m_out_norm_mla'], 'm_w_out': out['m_w_out'], 'm_ffn2_norm': out['m_ffn2_norm'], 'm_ffn2_w_gate': out['m_ffn2_w_gate'], 'm_ffn2_w_up': out['m_ffn2_w_up'], 'm_ffn2_w_down': out['m_ffn2_w_down'], 'v_ffn1_norm': out['v_ffn1_norm'], 'v_ffn1_w_gate': out['v_ffn1_w_gate'], 'v_ffn1_w_up': out['v_ffn1_w_up'], 'v_ffn1_w_down': out['v_ffn1_w_down'], 'v_mix_norm': out['v_mix_norm'], 'v_w_in': out['v_w_in'], 'v_dil_q_norm': out['v_dil_q_norm'], 'v_dil_k_norm': out['v_dil_k_norm'], 'v_rel_bias': out['v_rel_bias'], 'v_mla_q_a_norm': out['v_mla_q_a_norm'], 'v_mla_w_q_b': out['v_mla_w_q_b'], 'v_mla_kv_a_norm': out['v_mla_kv_a_norm'], 'v_mla_w_kv_b': out['v_mla_w_kv_b'], 'v_mla_q_norm': out['v_mla_q_norm'], 'v_mla_k_norm': out['v_mla_k_norm'], 'v_out_norm_dil': out['v_out_norm_dil'], 'v_out_norm_mla': out['v_out_norm_mla'], 'v_w_out': out['v_w_out'], 'v_ffn2_norm': out['v_ffn2_norm'], 'v_ffn2_w_gate': out['v_ffn2_w_gate'], 'v_ffn2_w_up': out['v_ffn2_w_up'], 'v_ffn2_w_down': out['v_ffn2_w_down']}


def _loss(weights, diff, rest, loss_target):
    with _jax.named_scope("forward"):
        args = {**rest, TWIN_DIFF_INPUT: diff, **{k: w.astype(_WEIGHT_DTYPES[k]) for k, w in weights.items()}}
        y = _forward(args)
    with _jax.named_scope("loss_head"):
        err = _jnp.square(y.astype(_jnp.float32) - loss_target)
        return 0.5 * _jnp.sum(_jnp.mean(err, axis=-1)) if err.ndim else 0.5 * err


def _adamw(w, g, m, v):
    m = ADAM_B1 * m + (1.0 - ADAM_B1) * g
    v = ADAM_B2 * v + (1.0 - ADAM_B2) * _jnp.square(g)
    m_hat = m / (1.0 - ADAM_B1 ** ADAM_STEP)
    v_hat = v / (1.0 - ADAM_B2 ** ADAM_STEP)
    delta = -ADAM_LR * (m_hat / (_jnp.sqrt(v_hat) + ADAM_EPS) + ADAM_WD * w)
    return delta, m, v


def reference(x, ffn1_norm, ffn1_w_gate, ffn1_w_up, ffn1_w_down, mix_norm, w_in, dil_q_norm, dil_k_norm, rel_bias, mla_q_a_norm, mla_w_q_b, mla_kv_a_norm, mla_w_kv_b, mla_q_norm, mla_k_norm, out_norm_dil, out_norm_mla, w_out, ffn2_norm, ffn2_w_gate, ffn2_w_up, ffn2_w_down, loss_target, m_ffn1_norm, m_ffn1_w_gate, m_ffn1_w_up, m_ffn1_w_down, m_mix_norm, m_w_in, m_dil_q_norm, m_dil_k_norm, m_rel_bias, m_mla_q_a_norm, m_mla_w_q_b, m_mla_kv_a_norm, m_mla_w_kv_b, m_mla_q_norm, m_mla_k_norm, m_out_norm_dil, m_out_norm_mla, m_w_out, m_ffn2_norm, m_ffn2_w_gate, m_ffn2_w_up, m_ffn2_w_down, v_ffn1_norm, v_ffn1_w_gate, v_ffn1_w_up, v_ffn1_w_down, v_mix_norm, v_w_in, v_dil_q_norm, v_dil_k_norm, v_rel_bias, v_mla_q_a_norm, v_mla_w_q_b, v_mla_kv_a_norm, v_mla_w_kv_b, v_mla_q_norm, v_mla_k_norm, v_out_norm_dil, v_out_norm_mla, v_w_out, v_ffn2_norm, v_ffn2_w_gate, v_ffn2_w_up, v_ffn2_w_down):
    given = dict(x=x, ffn1_norm=ffn1_norm, ffn1_w_gate=ffn1_w_gate, ffn1_w_up=ffn1_w_up, ffn1_w_down=ffn1_w_down, mix_norm=mix_norm, w_in=w_in, dil_q_norm=dil_q_norm, dil_k_norm=dil_k_norm, rel_bias=rel_bias, mla_q_a_norm=mla_q_a_norm, mla_w_q_b=mla_w_q_b, mla_kv_a_norm=mla_kv_a_norm, mla_w_kv_b=mla_w_kv_b, mla_q_norm=mla_q_norm, mla_k_norm=mla_k_norm, out_norm_dil=out_norm_dil, out_norm_mla=out_norm_mla, w_out=w_out, ffn2_norm=ffn2_norm, ffn2_w_gate=ffn2_w_gate, ffn2_w_up=ffn2_w_up, ffn2_w_down=ffn2_w_down, loss_target=loss_target, m_ffn1_norm=m_ffn1_norm, m_ffn1_w_gate=m_ffn1_w_gate, m_ffn1_w_up=m_ffn1_w_up, m_ffn1_w_down=m_ffn1_w_down, m_mix_norm=m_mix_norm, m_w_in=m_w_in, m_dil_q_norm=m_dil_q_norm, m_dil_k_norm=m_dil_k_norm, m_rel_bias=m_rel_bias, m_mla_q_a_norm=m_mla_q_a_norm, m_mla_w_q_b=m_mla_w_q_b, m_mla_kv_a_norm=m_mla_kv_a_norm, m_mla_w_kv_b=m_mla_w_kv_b, m_mla_q_norm=m_mla_q_norm, m_mla_k_norm=m_mla_k_norm, m_out_norm_dil=m_out_norm_dil, m_out_norm_mla=m_out_norm_mla, m_w_out=m_w_out, m_ffn2_norm=m_ffn2_norm, m_ffn2_w_gate=m_ffn2_w_gate, m_ffn2_w_up=m_ffn2_w_up, m_ffn2_w_down=m_ffn2_w_down, v_ffn1_norm=v_ffn1_norm, v_ffn1_w_gate=v_ffn1_w_gate, v_ffn1_w_up=v_ffn1_w_up, v_ffn1_w_down=v_ffn1_w_down, v_mix_norm=v_mix_norm, v_w_in=v_w_in, v_dil_q_norm=v_dil_q_norm, v_dil_k_norm=v_dil_k_norm, v_rel_bias=v_rel_bias, v_mla_q_a_norm=v_mla_q_a_norm, v_mla_w_q_b=v_mla_w_q_b, v_mla_kv_a_norm=v_mla_kv_a_norm, v_mla_w_kv_b=v_mla_w_kv_b, v_mla_q_norm=v_mla_q_norm, v_mla_k_norm=v_mla_k_norm, v_out_norm_dil=v_out_norm_dil, v_out_norm_mla=v_out_norm_mla, v_w_out=v_w_out, v_ffn2_norm=v_ffn2_norm, v_ffn2_w_gate=v_ffn2_w_gate, v_ffn2_w_up=v_ffn2_w_up, v_ffn2_w_down=v_ffn2_w_down)
    weights = {n: given[n] for n in TWIN_WEIGHTS}
    shared = {n: given[n] for n in SHARED_INPUTS}
    per_example = {n: given[n] for n in ['x']}
    grad_fn = _jax.value_and_grad(_loss, argnums=(0, 1))

    def one_microbatch(ex, loss_target):
        ex = dict(ex)
        diff = ex.pop(TWIN_DIFF_INPUT)
        return grad_fn(weights, diff, {**shared, **ex}, loss_target)

    if N_MICROBATCH == 1:
        loss, (grad_w, grad_x) = one_microbatch(per_example, given["loss_target"])
    else:
        def body(carry, xs):
            loss_sum, grad_sum = carry
            l_k, (gw_k, gx_k) = one_microbatch(xs[0], xs[1])
            with _jax.named_scope("update"):
                return (loss_sum + l_k, _jax.tree.map(_jnp.add, grad_sum, gw_k)), gx_k

        init = (_jnp.zeros((), _jnp.float32), _jax.tree.map(_jnp.zeros_like, weights))
        (loss, grad_w), grad_x = _jax.lax.scan(body, init, (per_example, given["loss_target"]))
    with _jax.named_scope("update"):
        delta_w, new_m, new_v = {}, {}, {}
        for n in TWIN_WEIGHTS:
            delta_w[n], new_m[n], new_v[n] = _adamw(weights[n], grad_w[n], given["m_" + n], given["v_" + n])
    return (loss, grad_x, *[grad_w[n] for n in TWIN_WEIGHTS], *[delta_w[n] for n in TWIN_WEIGHTS],
            *[new_m[n] for n in TWIN_WEIGHTS], *[new_v[n] for n in TWIN_WEIGHTS])
```

```python
import math

import numpy as np
import jax
import jax.numpy as jnp
from jax import lax
from jax.experimental import pallas as pl
from jax.experimental.pallas import tpu as pltpu

F32, BF16 = jnp.float32, jnp.bfloat16
EPS = 1e-6
NEG = -1e30
N_DEV = 8

DIL_HEADS, DIL_HD = 8, 64
DIL_BRANCHES = ((128, 1), (512, 4), (2048, 16))
DIL_BLOCK = 128
MLA_HEADS, MLA_NOPE, MLA_ROPE, MLA_V = 4, 128, 64, 128
MLA_QK = MLA_NOPE + MLA_ROPE
MLA_PAD = 256
ROPE_BASE = 10000.0
REL_BUCKETS, REL_MAX_DIST = 32, 2048
PROJ_COLS, PROJ_PAD = 1984, 2048
FFN_RESID = 0.5
ADAM_LR, ADAM_B1, ADAM_B2, ADAM_EPS, ADAM_WD, ADAM_STEP = 0.001, 0.9, 0.999, 1e-08, 0.01, 10
VMEM_LIMIT = 56 * 1024 * 1024

_NT = (((1,), (1,)), ((), ()))
_TN = (((0,), (0,)), ((), ()))


def _dot(a, b):
    return jnp.dot(a, b, preferred_element_type=F32)


def _dot_nt(a, b):
    return lax.dot_general(a, b, _NT, preferred_element_type=F32)


def _dot_tn(a, b):
    return lax.dot_general(a, b, _TN, preferred_element_type=F32)


def _params(n_axes):
    return pltpu.CompilerParams(dimension_semantics=("arbitrary",) * n_axes, vmem_limit_bytes=VMEM_LIMIT)


def _rstd(x, n=None):
    n = x.shape[-1] if n is None else n
    return lax.rsqrt(jnp.sum(x * x, axis=-1, keepdims=True) / n + EPS)


def _rms_bwd(dy, x, g, r, n=None):
    n = x.shape[-1] if n is None else n
    u = dy * g
    dx = r * u - x * (r * r * r) * (jnp.sum(u * x, axis=-1, keepdims=True) / n)
    return dx, dy * x * r


def _sigmoid(x):
    return 1.0 / (1.0 + jnp.exp(-x))


def _ffn_fwd(x, gain, wg, wu, wd, tm=512, tf=256):
    T, D = x.shape
    F = wg.shape[1]
    nj = F // tf

    def body(x_ref, g_ref, wg_ref, wu_ref, wd_ref, xo_ref, h_ref, gate_ref, up_ref, acc):
        j = pl.program_id(1)

        @pl.when(j == 0)
        def _():
            xv = x_ref[...]
            h_ref[...] = (xv * _rstd(xv) * g_ref[...]).astype(BF16)
            acc[...] = jnp.zeros_like(acc)

        h = h_ref[...]
        g = _dot(h, wg_ref[...])
        u = _dot(h, wu_ref[...])
        gate_ref[...] = g.astype(BF16)
        up_ref[...] = u.astype(BF16)
        a = (g * _sigmoid(g) * u).astype(BF16)
        acc[...] += _dot(a, wd_ref[...])

        @pl.when(j == nj - 1)
        def _():
            xo_ref[...] = x_ref[...] + FFN_RESID * acc[...]

    return pl.pallas_call(
        body, name="ffn_fwd", grid=(T // tm, nj),
        in_specs=[pl.BlockSpec((tm, D), lambda i, j: (i, 0)), pl.BlockSpec((1, D), lambda i, j: (0, 0)),
                  pl.BlockSpec((D, tf), lambda i, j: (0, j)), pl.BlockSpec((D, tf), lambda i, j: (0, j)),
                  pl.BlockSpec((tf, D), lambda i, j: (j, 0))],
        out_specs=[pl.BlockSpec((tm, D), lambda i, j: (i, 0)), pl.BlockSpec((tm, D), lambda i, j: (i, 0)),
                   pl.BlockSpec((tm, tf), lambda i, j: (i, j)), pl.BlockSpec((tm, tf), lambda i, j: (i, j))],
        out_shape=[jax.ShapeDtypeStruct((T, D), F32), jax.ShapeDtypeStruct((T, D), BF16),
                   jax.ShapeDtypeStruct((T, F), BF16), jax.ShapeDtypeStruct((T, F), BF16)],
        scratch_shapes=[pltpu.VMEM((tm, D), F32)],
        compiler_params=_params(2),
    )(x, gain, wg, wu, wd)


def _ffn_bwd(dy, x, gain, gate, up, wg, wu, wd, tm=512, tf=256):
    T, D = x.shape
    F = wg.shape[1]
    nj = F // tf

    def body(dy_ref, x_ref, g_ref, gate_ref, up_ref, wg_ref, wu_ref, wd_ref,
             dx_ref, a_ref, dg_ref, du_ref, dyh_ref, dgain_ref, acc):
        i, j = pl.program_id(0), pl.program_id(1)

        @pl.when((i == 0) & (j == 0))
        def _():
            dgain_ref[...] = jnp.zeros_like(dgain_ref)

        @pl.when(j == 0)
        def _():
            dyh_ref[...] = (FFN_RESID * dy_ref[...]).astype(BF16)
            acc[...] = jnp.zeros_like(acc)

        da = _dot_nt(dyh_ref[...], wd_ref[...])
        g = gate_ref[...].astype(F32)
        u = up_ref[...].astype(F32)
        sig = _sigmoid(g)
        s = g * sig
        a_ref[...] = (s * u).astype(BF16)
        dg = (da * u * (sig * (1.0 + g * (1.0 - sig)))).astype(BF16)
        du = (da * s).astype(BF16)
        dg_ref[...] = dg
        du_ref[...] = du
        acc[...] += _dot_nt(dg, wg_ref[...]) + _dot_nt(du, wu_ref[...])

        @pl.when(j == nj - 1)
        def _():
            xv = x_ref[...]
            dxn, dgc = _rms_bwd(acc[...], xv, g_ref[...], _rstd(xv))
            dx_ref[...] = dy_ref[...] + dxn
            dgain_ref[...] += jnp.sum(dgc, axis=0, keepdims=True)

    return pl.pallas_call(
        body, name="ffn_bwd", grid=(T // tm, nj),
        in_specs=[pl.BlockSpec((tm, D), lambda i, j: (i, 0)), pl.BlockSpec((tm, D), lambda i, j: (i, 0)),
                  pl.BlockSpec((1, D), lambda i, j: (0, 0)),
                  pl.BlockSpec((tm, tf), lambda i, j: (i, j)), pl.BlockSpec((tm, tf), lambda i, j: (i, j)),
                  pl.BlockSpec((D, tf), lambda i, j: (0, j)), pl.BlockSpec((D, tf), lambda i, j: (0, j)),
                  pl.BlockSpec((tf, D), lambda i, j: (j, 0))],
        out_specs=[pl.BlockSpec((tm, D), lambda i, j: (i, 0)),
                   pl.BlockSpec((tm, tf), lambda i, j: (i, j)), pl.BlockSpec((tm, tf), lambda i, j: (i, j)),
                   pl.BlockSpec((tm, tf), lambda i, j: (i, j)),
                   pl.BlockSpec((tm, D), lambda i, j: (i, 0)), pl.BlockSpec((1, D), lambda i, j: (0, 0))],
        out_shape=[jax.ShapeDtypeStruct((T, D), F32), jax.ShapeDtypeStruct((T, F), BF16),
                   jax.ShapeDtypeStruct((T, F), BF16), jax.ShapeDtypeStruct((T, F), BF16),
                   jax.ShapeDtypeStruct((T, D), BF16), jax.ShapeDtypeStruct((1, D), F32)],
        scratch_shapes=[pltpu.VMEM((tm, D), F32)],
        compiler_params=_params(2),
    )(dy, x, gain, gate, up, wg, wu, wd)


def _matmul_tn(a, b, tk, tn, tt=512):
    T, K = a.shape
    N = b.shape[1]
    tk, tn = min(tk, K), min(tn, N)

    def body(a_ref, b_ref, o_ref):
        @pl.when(pl.program_id(2) == 0)
        def _():
            o_ref[...] = jnp.zeros_like(o_ref)

        o_ref[...] += _dot_tn(a_ref[...].astype(BF16), b_ref[...].astype(BF16))

    return pl.pallas_call(
        body, name="matmul_tn", grid=(K // tk, N // tn, T // tt),
        in_specs=[pl.BlockSpec((tt, tk), lambda k, n, t: (t, k)), pl.BlockSpec((tt, tn), lambda k, n, t: (t, n))],
        out_specs=pl.BlockSpec((tk, tn), lambda k, n, t: (k, n)),
        out_shape=jax.ShapeDtypeStruct((K, N), F32),
        compiler_params=_params(3),
    )(a, b)


def _loss_grad(y, target, tm=512):
    T, D = y.shape

    def body(y_ref, t_ref, dy_ref, loss_ref):
        @pl.when(pl.program_id(0) == 0)
        def _():
            loss_ref[...] = jnp.zeros_like(loss_ref)

        e = y_ref[...] - t_ref[...]
        dy_ref[...] = e * (1.0 / D)
        loss_ref[...] += (0.5 / D) * jnp.sum(e * e)

    return pl.pallas_call(
        body, name="loss_grad", grid=(T // tm,),
        in_specs=[pl.BlockSpec((tm, D), lambda i: (i, 0)), pl.BlockSpec((tm, D), lambda i: (i, 0))],
        out_specs=[pl.BlockSpec((tm, D), lambda i: (i, 0)), pl.BlockSpec((1, 128), lambda i: (0, 0))],
        out_shape=[jax.ShapeDtypeStruct((T, D), F32), jax.ShapeDtypeStruct((1, 128), F32)],
        compiler_params=_params(1),
    )(y, target)


def _in_proj(x, gain, w, tm=512):
    T, D = x.shape
    N = w.shape[1]

    def body(x_ref, g_ref, w_ref, h_ref, p_ref):
        xv = x_ref[...]
        h = (xv * _rstd(xv) * g_ref[...]).astype(BF16)
        h_ref[...] = h
        p_ref[...] = _dot(h, w_ref[...])

    return pl.pallas_call(
        body, name="in_proj", grid=(T // tm,),
        in_specs=[pl.BlockSpec((tm, D), lambda i: (i, 0)), pl.BlockSpec((1, D), lambda i: (0, 0)),
                  pl.BlockSpec((D, N), lambda i: (0, 0))],
        out_specs=[pl.BlockSpec((tm, D), lambda i: (i, 0)), pl.BlockSpec((tm, N), lambda i: (i, 0))],
        out_shape=[jax.ShapeDtypeStruct((T, D), BF16), jax.ShapeDtypeStruct((T, N), F32)],
        compiler_params=_params(1),
    )(x, gain, w)


def _in_proj_bwd(dx_up, x, gain, w, dq, dk, dv, dcq, dckv, dkpe, tm=512):
    T, D = x.shape
    N = w.shape[1]
    W = DIL_HEADS * DIL_HD

    def body(dxu_ref, x_ref, g_ref, w_ref, dq_ref, dk_ref, dv_ref, dcq_ref, dckv_ref, dkpe_ref,
             dx_ref, dp_ref, dgain_ref):
        @pl.when(pl.program_id(0) == 0)
        def _():
            dgain_ref[...] = jnp.zeros_like(dgain_ref)

        dp_ref[:, 0:W] = dq_ref[...].astype(BF16)
        dp_ref[:, W:2 * W] = dk_ref[...].astype(BF16)
        dp_ref[:, 2 * W:3 * W] = dv_ref[...].astype(BF16)
        dp_ref[:, 3 * W:3 * W + 256] = dcq_ref[...].astype(BF16)
        dp_ref[:, 3 * W + 256:3 * W + 384] = dckv_ref[...].astype(BF16)
        dp_ref[:, 3 * W + 384:N] = dkpe_ref[...].astype(BF16)
        dh = _dot_nt(dp_ref[...], w_ref[...])
        xv = x_ref[...]
        dxn, dgc = _rms_bwd(dh, xv, g_ref[...], _rstd(xv))
        dx_ref[...] = dxu_ref[...] + dxn
        dgain_ref[...] += jnp.sum(dgc, axis=0, keepdims=True)

    row = lambda i: (i, 0)
    return pl.pallas_call(
        body, name="in_proj_bwd", grid=(T // tm,),
        in_specs=[pl.BlockSpec((tm, D), row), pl.BlockSpec((tm, D), row), pl.BlockSpec((1, D), lambda i: (0, 0)),
                  pl.BlockSpec((D, N), lambda i: (0, 0)),
                  pl.BlockSpec((tm, W), row), pl.BlockSpec((tm, W), row), pl.BlockSpec((tm, W), row),
                  pl.BlockSpec((tm, 256), row), pl.BlockSpec((tm, 128), row), pl.BlockSpec((tm, 128), row)],
        out_specs=[pl.BlockSpec((tm, D), row), pl.BlockSpec((tm, N), row), pl.BlockSpec((1, D), lambda i: (0, 0))],
        out_shape=[jax.ShapeDtypeStruct((T, D), F32), jax.ShapeDtypeStruct((T, N), BF16),
                   jax.ShapeDtypeStruct((1, D), F32)],
        compiler_params=_params(1),
    )(dx_up, x, gain, w, dq, dk, dv, dcq, dckv, dkpe)


def _out_proj(x, o_dil, o_mla, g_dil, g_mla, w, tm=512):
    T, D = x.shape
    W = o_dil.shape[1]

    def body(x_ref, od_ref, om_ref, gd_ref, gm_ref, w_ref, xo_ref, oc_ref):
        od, om = od_ref[...], om_ref[...]
        oc_ref[:, 0:W] = (od * _rstd(od) * gd_ref[...]).astype(BF16)
        oc_ref[:, W:2 * W] = (om * _rstd(om) * gm_ref[...]).astype(BF16)
        xo_ref[...] = x_ref[...] + _dot(oc_ref[...], w_ref[...])

    row = lambda i: (i, 0)
    fix = lambda i: (0, 0)
    return pl.pallas_call(
        body, name="out_proj", grid=(T // tm,),
        in_specs=[pl.BlockSpec((tm, D), row), pl.BlockSpec((tm, W), row), pl.BlockSpec((tm, W), row),
                  pl.BlockSpec((1, W), fix), pl.BlockSpec((1, W), fix), pl.BlockSpec((2 * W, D), fix)],
        out_specs=[pl.BlockSpec((tm, D), row), pl.BlockSpec((tm, 2 * W), row)],
        out_shape=[jax.ShapeDtypeStruct((T, D), F32), jax.ShapeDtypeStruct((T, 2 * W), BF16)],
        compiler_params=_params(1),
    )(x, o_dil, o_mla, g_dil, g_mla, w)


def _out_proj_bwd(dx, o_dil, o_mla, g_dil, g_mla, w, tm=512):
    T, D = dx.shape
    W = o_dil.shape[1]

    def body(dx_ref, od_ref, om_ref, gd_ref, gm_ref, w_ref, dod_ref, dom_ref, dgd_ref, dgm_ref):
        @pl.when(pl.program_id(0) == 0)
        def _():
            dgd_ref[...] = jnp.zeros_like(dgd_ref)
            dgm_ref[...] = jnp.zeros_like(dgm_ref)

        doc = _dot_nt(dx_ref[...].astype(BF16), w_ref[...])
        od, om = od_ref[...], om_ref[...]
        dod, dgd = _rms_bwd(doc[:, 0:W], od, gd_ref[...], _rstd(od))
        dom, dgm = _rms_bwd(doc[:, W:2 * W], om, gm_ref[...], _rstd(om))
        dod_ref[...] = dod
        dom_ref[...] = dom
        dgd_ref[...] += jnp.sum(dgd, axis=0, keepdims=True)
        dgm_ref[...] += jnp.sum(dgm, axis=0, keepdims=True)

    row = lambda i: (i, 0)
    fix = lambda i: (0, 0)
    return pl.pallas_call(
        body, name="out_proj_bwd", grid=(T // tm,),
        in_specs=[pl.BlockSpec((tm, D), row), pl.BlockSpec((tm, W), row), pl.BlockSpec((tm, W), row),
                  pl.BlockSpec((1, W), fix), pl.BlockSpec((1, W), fix), pl.BlockSpec((2 * W, D), fix)],
        out_specs=[pl.BlockSpec((tm, W), row), pl.BlockSpec((tm, W), row),
                   pl.BlockSpec((1, W), fix), pl.BlockSpec((1, W), fix)],
        out_shape=[jax.ShapeDtypeStruct((T, W), F32), jax.ShapeDtypeStruct((T, W), F32),
                   jax.ShapeDtypeStruct((1, W), F32), jax.ShapeDtypeStruct((1, W), F32)],
        compiler_params=_params(1),
    )(dx, o_dil, o_mla, g_dil, g_mla, w)


def _pair_rstd(x, lo):
    sq = x * x
    s0 = jnp.sum(jnp.where(lo, sq, 0.0), axis=-1, keepdims=True)
    s1 = jnp.sum(jnp.where(lo, 0.0, sq), axis=-1, keepdims=True)
    return jnp.where(lo, lax.rsqrt(s0 / DIL_HD + EPS), lax.rsqrt(s1 / DIL_HD + EPS))


def _pair_rms_bwd(dn, x, r, g, lo):
    u = dn * g
    t = u * x
    d0 = jnp.sum(jnp.where(lo, t, 0.0), axis=-1, keepdims=True)
    d1 = jnp.sum(jnp.where(lo, 0.0, t), axis=-1, keepdims=True)
    dx = r * u - x * (r * r * r) * (jnp.where(lo, d0, d1) / DIL_HD)
    return dx, jnp.sum(dn * x * r, axis=0, keepdims=True)


def _pair_col(x, lo, e):
    sel = lo if e == 0 else jnp.logical_not(lo)
    return jnp.max(jnp.where(sel, x, NEG), axis=-1, keepdims=True)


def _dil_masks(n):
    row = lax.broadcasted_iota(jnp.int32, (DIL_BLOCK, DIL_BLOCK), 0)
    col = lax.broadcasted_iota(jnp.int32, (DIL_BLOCK, DIL_BLOCK), 1)
    return col < DIL_HD, jnp.logical_and(col >= row, n > 0), col <= row


def _dil_fwd(proj, bias_p, bias_c, gq, gk, d, prev):
    T = proj.shape[0]
    L = T // d
    nb = L // DIL_BLOCK
    W = DIL_HEADS * DIL_HD
    has_prev = prev is not None

    def body(*refs):
        q_ref, kp_ref, kc_ref, vp_ref, vc_ref, bp_ref, bc_ref, gq_ref, gk_ref = refs[:9]
        if has_prev:
            oin_ref, lin_ref, o_ref, l_ref = refs[9:]
        else:
            o_ref, l_ref = refs[9:]
        lo, mask_p, mask_c = _dil_masks(pl.program_id(1))
        gqv, gkv = gq_ref[...], gk_ref[...]
        for hp in range(DIL_HEADS // 2):
            sl = slice(128 * hp, 128 * hp + 128)
            q = q_ref[:, sl]
            qn = q * _pair_rstd(q, lo) * gqv
            kp, kc = kp_ref[:, sl], kc_ref[:, sl]
            kpn = (kp * _pair_rstd(kp, lo) * gkv).astype(BF16)
            kcn = (kc * _pair_rstd(kc, lo) * gkv).astype(BF16)
            vp, vc = vp_ref[:, sl].astype(BF16), vc_ref[:, sl].astype(BF16)
            o_e, l_e = [], []
            for e in range(2):
                sel = lo if e == 0 else jnp.logical_not(lo)
                qe = jnp.where(sel, qn, 0.0).astype(BF16)
                sp = jnp.where(mask_p, _dot_nt(qe, kpn) * DIL_HD ** -0.5 + bp_ref[2 * hp + e], NEG)
                sc = jnp.where(mask_c, _dot_nt(qe, kcn) * DIL_HD ** -0.5 + bc_ref[2 * hp + e], NEG)
                m = jnp.maximum(jnp.max(sp, axis=-1, keepdims=True), jnp.max(sc, axis=-1, keepdims=True))
                pp, pc = jnp.exp(sp - m), jnp.exp(sc - m)
                l = jnp.sum(pp, axis=-1, keepdims=True) + jnp.sum(pc, axis=-1, keepdims=True)
                o_e.append((_dot(pp.astype(BF16), vp) + _dot(pc.astype(BF16), vc)) / l)
                l_e.append(m + jnp.log(l))
            o = jnp.where(lo, o_e[0], o_e[1])
            lse = jnp.where(lo, l_e[0], l_e[1])
            if has_prev:
                lin = lin_ref[:, sl]
                mx = jnp.maximum(lin, lse)
                lnew = mx + jnp.log(jnp.exp(lin - mx) + jnp.exp(lse - mx))
                o = oin_ref[:, sl] * jnp.exp(lin - lnew) + o * jnp.exp(lse - lnew)
                lse = lnew
            o_ref[:, sl] = o
            l_ref[:, sl] = lse

    blk = (DIL_BLOCK, W)
    cpb = PROJ_PAD // W
    prev_n = lambda n: jnp.maximum(n - 1, 0)
    fix3 = lambda r, n: (0, 0, 0)
    fix2 = lambda r, n: (0, 0)
    in_specs = [pl.BlockSpec(blk, lambda r, n: (n, cpb * r)),
                pl.BlockSpec(blk, lambda r, n: (prev_n(n), cpb * r + 1)), pl.BlockSpec(blk, lambda r, n: (n, cpb * r + 1)),
                pl.BlockSpec(blk, lambda r, n: (prev_n(n), cpb * r + 2)), pl.BlockSpec(blk, lambda r, n: (n, cpb * r + 2)),
                pl.BlockSpec((DIL_HEADS, DIL_BLOCK, DIL_BLOCK), fix3), pl.BlockSpec((DIL_HEADS, DIL_BLOCK, DIL_BLOCK), fix3),
                pl.BlockSpec((1, 128), fix2), pl.BlockSpec((1, 128), fix2)]
    pv = proj.reshape(L, d * PROJ_PAD)
    args = [pv, pv, pv, pv, pv, bias_p, bias_c, gq, gk]
    tok = pl.BlockSpec(blk, lambda r, n: (n, r))
    if has_prev:
        in_specs += [tok, tok]
        args += [prev[0].reshape(L, d * W), prev[1].reshape(L, d * W)]
    o, lse = pl.pallas_call(
        body, name=f"dil_fwd_d{d}", grid=(d, nb), in_specs=in_specs, out_specs=[tok, tok],
        out_shape=[jax.ShapeDtypeStruct((L, d * W), F32), jax.ShapeDtypeStruct((L, d * W), F32)],
        compiler_params=_params(2),
    )(*args)
    return o.reshape(T, W), lse.reshape(T, W)


def _dil_bwd(proj, o, lse, do, bias_p, bias_c, gq, gk, d, prev):
    T = proj.shape[0]
    L = T // d
    nb = L // DIL_BLOCK
    W = DIL_HEADS * DIL_HD
    has_prev = prev is not None
    scale = DIL_HD ** -0.5

    def body(*refs):
        (q_ref, kp_ref, kc_ref, vp_ref, vc_ref, o_ref, l_ref, do_ref, bp_ref, bc_ref, gq_ref, gk_ref) = refs[:12]
        refs = refs[12:]
        if has_prev:
            dqi_ref, dki_ref, dvi_ref = refs[:3]
            refs = refs[3:]
        dq_ref, dk_ref, dv_ref, dbp_ref, dbc_ref, dgq_ref, dgk_ref, ck, cv = refs
        r_id, n = pl.program_id(0), pl.program_id(1)
        lo, mask_p, mask_c = _dil_masks(n)
        gqv, gkv = gq_ref[...], gk_ref[...]

        @pl.when((r_id == 0) & (n == 0))
        def _():
            dbp_ref[...] = jnp.zeros_like(dbp_ref)
            dbc_ref[...] = jnp.zeros_like(dbc_ref)
            dgq_ref[...] = jnp.zeros_like(dgq_ref)
            dgk_ref[...] = jnp.zeros_like(dgk_ref)

        @pl.when(n == 0)
        def _():
            ck[...] = jnp.zeros_like(ck)
            cv[...] = jnp.zeros_like(cv)

        def finish_prev(sl, dkn_p, dv_p):
            kp = kp_ref[:, sl]
            dk, dgk = _pair_rms_bwd(ck[:, sl] + dkn_p, kp, _pair_rstd(kp, lo), gkv, lo)
            dv = cv[:, sl] + dv_p
            if has_prev:
                dk = dk + dki_ref[:, sl]
                dv = dv + dvi_ref[:, sl]
            dk_ref[:, sl] = dk
            dv_ref[:, sl] = dv
            dgk_ref[...] += dgk

        @pl.when(n < nb)
        def _():
            for hp in range(DIL_HEADS // 2):
                sl = slice(128 * hp, 128 * hp + 128)
                q = q_ref[:, sl]
                rq = _pair_rstd(q, lo)
                qn = q * rq * gqv
                qb = qn.astype(BF16)
                kp, kc = kp_ref[:, sl], kc_ref[:, sl]
                kpn = (kp * _pair_rstd(kp, lo) * gkv).astype(BF16)
                kcn = (kc * _pair_rstd(kc, lo) * gkv).astype(BF16)
                vp, vc = vp_ref[:, sl].astype(BF16), vc_ref[:, sl].astype(BF16)
                dov = do_ref[:, sl]
                dob = dov.astype(BF16)
                dot_o = dov * o_ref[:, sl]
                lse_pair = l_ref[:, sl]
                res = []
                for e in range(2):
                    sel = lo if e == 0 else jnp.logical_not(lo)
                    h = 2 * hp + e
                    qe = jnp.where(sel, qn, 0.0).astype(BF16)
                    doe = jnp.where(sel, dov, 0.0).astype(BF16)
                    delta = jnp.sum(jnp.where(sel, dot_o, 0.0), axis=-1, keepdims=True)
                    lse_e = _pair_col(lse_pair, lo, e)
                    sp = jnp.where(mask_p, _dot_nt(qe, kpn) * scale + bp_ref[h], NEG)
                    sc = jnp.where(mask_c, _dot_nt(qe, kcn) * scale + bc_ref[h], NEG)
                    pp, pc = jnp.exp(sp - lse_e), jnp.exp(sc - lse_e)
                    dsp = pp * (_dot_nt(doe, vp) - delta)
                    dsc = pc * (_dot_nt(doe, vc) - delta)
                    dbp_ref[h] += dsp
                    dbc_ref[h] += dsc
                    dspb, dscb = dsp.astype(BF16), dsc.astype(BF16)
                    res.append(((_dot(dspb, kpn) + _dot(dscb, kcn)) * scale,
                                _dot_tn(dspb, qb) * scale, _dot_tn(dscb, qb) * scale,
                                _dot_tn(pp.astype(BF16), dob), _dot_tn(pc.astype(BF16), dob)))
                dqn, dkn_p, dkn_c, dv_p, dv_c = (jnp.where(lo, a, b) for a, b in zip(res[0], res[1]))
                dq, dgq = _pair_rms_bwd(dqn, q, rq, gqv, lo)
                if has_prev:
                    dq = dq + dqi_ref[:, sl]
                dq_ref[:, sl] = dq
                dgq_ref[...] += dgq
                finish_prev(sl, dkn_p, dv_p)
                ck[:, sl] = dkn_c
                cv[:, sl] = dv_c

        @pl.when(n == nb)
        def _():
            for hp in range(DIL_HEADS // 2):
                sl = slice(128 * hp, 128 * hp + 128)
                finish_prev(sl, 0.0, 0.0)

    blk = (DIL_BLOCK, W)
    cpb = PROJ_PAD // W
    qn_ = lambda n: jnp.minimum(n, nb - 1)
    pn_ = lambda n: jnp.maximum(n - 1, 0)
    fix3 = lambda r, n: (0, 0, 0)
    fix2 = lambda r, n: (0, 0)
    tok_q = pl.BlockSpec(blk, lambda r, n: (qn_(n), r))
    tok_p = pl.BlockSpec(blk, lambda r, n: (pn_(n), r))
    bias_spec = pl.BlockSpec((DIL_HEADS, DIL_BLOCK, DIL_BLOCK), fix3)
    gain_spec = pl.BlockSpec((1, 128), fix2)
    in_specs = [pl.BlockSpec(blk, lambda r, n: (qn_(n), cpb * r)),
                pl.BlockSpec(blk, lambda r, n: (pn_(n), cpb * r + 1)), pl.BlockSpec(blk, lambda r, n: (qn_(n), cpb * r + 1)),
                pl.BlockSpec(blk, lambda r, n: (pn_(n), cpb * r + 2)), pl.BlockSpec(blk, lambda r, n: (qn_(n), cpb * r + 2)),
                tok_q, tok_q, tok_q, bias_spec, bias_spec, gain_spec, gain_spec]
    pv = proj.reshape(L, d * PROJ_PAD)
    view = lambda a: a.reshape(L, d * W)
    args = [pv, pv, pv, pv, pv, view(o), view(lse), view(do), bias_p, bias_c, gq, gk]
    if has_prev:
        in_specs += [tok_q, tok_p, tok_p]
        args += [view(prev[0]), view(prev[1]), view(prev[2])]
    tok_shape = jax.ShapeDtypeStruct((L, d * W), F32)
    bias_shape = jax.ShapeDtypeStruct((DIL_HEADS, DIL_BLOCK, DIL_BLOCK), F32)
    dq, dk, dv, dbp, dbc, dgq, dgk = pl.pallas_call(
        body, name=f"dil_bwd_d{d}", grid=(d, nb + 1), in_specs=in_specs,
        out_specs=[tok_q, tok_p, tok_p, bias_spec, bias_spec, gain_spec, gain_spec],
        out_shape=[tok_shape, tok_shape, tok_shape, bias_shape, bias_shape,
                   jax.ShapeDtypeStruct((1, 128), F32), jax.ShapeDtypeStruct((1, 128), F32)],
        scratch_shapes=[pltpu.VMEM(blk, F32), pltpu.VMEM(blk, F32)],
        compiler_params=_params(2),
    )(*args)
    return (dq.reshape(T, W), dk.reshape(T, W), dv.reshape(T, W)), dbp, dbc, dgq, dgk


def _t5_bucket(dist):
    max_exact = REL_BUCKETS // 2
    dd = np.maximum(dist, 1).astype(np.float32)
    large = max_exact + (np.log(dd / max_exact) / np.log(REL_MAX_DIST / max_exact)
                         * (REL_BUCKETS - max_exact)).astype(np.int32)
    large = np.minimum(large, REL_BUCKETS - 1)
    return np.where(dist < max_exact, dist, large).astype(np.int32)


def _bias_buckets(d):
    i = np.arange(DIL_BLOCK)[:, None]
    j = np.arange(DIL_BLOCK)[None, :]
    prev = _t5_bucket(np.clip(DIL_BLOCK + i - j, 0, None) * d)
    cur = _t5_bucket(np.clip(i - j, 0, None) * d)
    return prev, cur


def _bias_grad(dbs, buckets):
    n = len(dbs)
    K = DIL_BLOCK * DIL_BLOCK
    onehots = [jnp.asarray(np.eye(REL_BUCKETS, dtype=np.float32)[b.reshape(-1)], BF16) for b in buckets]

    def body(*refs):
        o_ref = refs[-1]
        acc = jnp.zeros((DIL_HEADS, REL_BUCKETS), F32)
        for k in range(n):
            x = refs[k][...]
            oh = refs[n + k][...]
            for _ in range(3):
                xb = x.astype(BF16)
                acc = acc + _dot(xb, oh)
                x = x - xb.astype(F32)
        o_ref[...] = acc

    return pl.pallas_call(
        body, name="bias_grad",
        out_shape=jax.ShapeDtypeStruct((DIL_HEADS, REL_BUCKETS), F32),
        compiler_params=pltpu.CompilerParams(vmem_limit_bytes=VMEM_LIMIT),
    )(*[x.reshape(DIL_HEADS, K) for x in dbs], *onehots)


def _swap_halves(x):
    lane = lax.broadcasted_iota(jnp.int32, x.shape, 1)
    first = (lane % 64) < 32
    return jnp.where(first, pltpu.roll(x, 96, 1), pltpu.roll(x, 32, 1))


def _rope_tables(T):
    pos = jnp.arange(T, dtype=F32)
    inv_freq = ROPE_BASE ** (-jnp.arange(0, MLA_ROPE, 2, dtype=F32) / MLA_ROPE)
    ang = pos[:, None] * inv_freq[None, :]
    z = jnp.zeros((T, 128 - MLA_ROPE), F32)
    cos = jnp.concatenate([jnp.cos(ang), jnp.cos(ang), z], axis=-1)
    sin = jnp.concatenate([-jnp.sin(ang), jnp.sin(ang), z], axis=-1)
    return cos, sin


def _mla_prep(proj, cos, sin, g_qa, g_kva, g_q, g_k, wq, wkv, tm=512):
    T = proj.shape[0]
    H = MLA_HEADS
    scale = MLA_QK ** -0.5

    def body(cq_ref, ckv_ref, kpe_ref, cos_ref, sin_ref, gqa_ref, gkva_ref, gq_ref, gk_ref, wq_ref, wkv_ref,
             q_ref, k_ref, v_ref):
        cosv, sinv = cos_ref[...], sin_ref[...]

        def rope(x):
            return x * cosv + _swap_halves(x) * sinv

        cq = cq_ref[...]
        qp = _dot((cq * _rstd(cq) * gqa_ref[...]).astype(BF16), wq_ref[...])
        ckv = ckv_ref[...]
        kvp = _dot((ckv * _rstd(ckv) * gkva_ref[...]).astype(BF16), wkv_ref[...])
        kpe = kpe_ref[...]
        for h in range(H):
            a = qp[:, MLA_PAD * h:MLA_PAD * (h + 1)]
            qn = a * _rstd(a, MLA_QK) * gq_ref[...]
            q_ref[h, :, 0:128] = (qn[:, 0:128] * scale).astype(BF16)
            q_ref[h, :, 128:256] = (rope(qn[:, 128:256]) * scale).astype(BF16)
            kn = kvp[:, MLA_PAD * h:MLA_PAD * h + 128]
            r = lax.rsqrt((jnp.sum(kn * kn, axis=-1, keepdims=True)
                           + jnp.sum(kpe * kpe, axis=-1, keepdims=True)) / MLA_QK + EPS)
            k_ref[h, :, 0:128] = (kn * r * gk_ref[:, 0:128]).astype(BF16)
            k_ref[h, :, 128:256] = rope(kpe * r * gk_ref[:, 128:256]).astype(BF16)
            v_ref[h] = kvp[:, MLA_PAD * h + 128:MLA_PAD * (h + 1)].astype(BF16)

    fix = lambda i: (0, 0)
    return pl.pallas_call(
        body, name="mla_prep", grid=(T // tm,),
        in_specs=[pl.BlockSpec((tm, 256), lambda i: (i, 6)), pl.BlockSpec((tm, 128), lambda i: (i, 14)),
                  pl.BlockSpec((tm, 128), lambda i: (i, 15)),
                  pl.BlockSpec((tm, 128), lambda i: (i, 0)), pl.BlockSpec((tm, 128), lambda i: (i, 0)),
                  pl.BlockSpec((1, 256), fix), pl.BlockSpec((1, 128), fix),
                  pl.BlockSpec((1, 256), fix), pl.BlockSpec((1, 256), fix),
                  pl.BlockSpec((256, H * MLA_PAD), fix), pl.BlockSpec((128, H * MLA_PAD), fix)],
        out_specs=[pl.BlockSpec((H, tm, MLA_PAD), lambda i: (0, i, 0)), pl.BlockSpec((H, tm, MLA_PAD), lambda i: (0, i, 0)),
                   pl.BlockSpec((H, tm, MLA_V), lambda i: (0, i, 0))],
        out_shape=[jax.ShapeDtypeStruct((H, T, MLA_PAD), BF16), jax.ShapeDtypeStruct((H, T, MLA_PAD), BF16),
                   jax.ShapeDtypeStruct((H, T, MLA_V), BF16)],
        compiler_params=_params(1),
    )(proj, proj, proj, cos, sin, g_qa, g_kva, g_q, g_k, wq, wkv)


def _mla_prep_bwd(proj, cos, sin, g_qa, g_kva, g_q, g_k, wq, wkv, dq, dk, dv, tm=512):
    T = proj.shape[0]
    H = MLA_HEADS
    scale = MLA_QK ** -0.5

    def body(cq_ref, ckv_ref, kpe_ref, cos_ref, sin_ref, gqa_ref, gkva_ref, gq_ref, gk_ref, wq_ref, wkv_ref,
             dq_ref, dk_ref, dv_ref,
             dcq_ref, dckv_ref, dkpe_ref, cqn_ref, ckvn_ref, dqp_ref, dkvp_ref,
             dgqa_ref, dgkva_ref, dgq_ref, dgk_ref):
        @pl.when(pl.program_id(0) == 0)
        def _():
            for ref in (dgqa_ref, dgkva_ref, dgq_ref, dgk_ref):
                ref[...] = jnp.zeros_like(ref)

        cosv, sinv = cos_ref[...], sin_ref[...]

        def rope_bwd(dy):
            return dy * cosv + _swap_halves(dy * sinv)

        cq = cq_ref[...]
        rcq = _rstd(cq)
        cqn = (cq * rcq * gqa_ref[...]).astype(BF16)
        cqn_ref[...] = cqn
        qp = _dot(cqn, wq_ref[...])
        ckv = ckv_ref[...]
        rckv = _rstd(ckv)
        ckvn = (ckv * rckv * gkva_ref[...]).astype(BF16)
        ckvn_ref[...] = ckvn
        kvp = _dot(ckvn, wkv_ref[...])
        kpe = kpe_ref[...]
        dkpe = jnp.zeros_like(kpe)
        dgq = jnp.zeros((1, MLA_PAD), F32)
        dgk = jnp.zeros((1, MLA_PAD), F32)
        for h in range(H):
            a = qp[:, MLA_PAD * h:MLA_PAD * (h + 1)]
            dqh = dq_ref[h]
            dn = jnp.concatenate([dqh[:, 0:128], rope_bwd(dqh[:, 128:256])], axis=-1) * scale
            da, dg = _rms_bwd(dn, a, gq_ref[...], _rstd(a, MLA_QK), MLA_QK)
            dgq = dgq + jnp.sum(dg, axis=0, keepdims=True)
            dqp_ref[:, MLA_PAD * h:MLA_PAD * (h + 1)] = da.astype(BF16)

            ak = jnp.concatenate([kvp[:, MLA_PAD * h:MLA_PAD * h + 128], kpe], axis=-1)
            dkh = dk_ref[h]
            dnk = jnp.concatenate([dkh[:, 0:128], rope_bwd(dkh[:, 128:256])], axis=-1)
            dak, dg = _rms_bwd(dnk, ak, gk_ref[...], _rstd(ak, MLA_QK), MLA_QK)
            dgk = dgk + jnp.sum(dg, axis=0, keepdims=True)
            dkpe = dkpe + dak[:, 128:256]
            dkvp_ref[:, MLA_PAD * h:MLA_PAD * h + 128] = dak[:, 0:128].astype(BF16)
            dkvp_ref[:, MLA_PAD * h + 128:MLA_PAD * (h + 1)] = dv_ref[h].astype(BF16)
        dkpe_ref[...] = dkpe
        dgq_ref[...] += dgq
        dgk_ref[...] += dgk
        dcq, dg = _rms_bwd(_dot_nt(dqp_ref[...], wq_ref[...]), cq, gqa_ref[...], rcq)
        dcq_ref[...] = dcq
        dgqa_ref[...] += jnp.sum(dg, axis=0, keepdims=True)
        dckv, dg = _rms_bwd(_dot_nt(dkvp_ref[...], wkv_ref[...]), ckv, gkva_ref[...], rckv)
        dckv_ref[...] = dckv
        dgkva_ref[...] += jnp.sum(dg, axis=0, keepdims=True)

    fix = lambda i: (0, 0)
    row = lambda i: (i, 0)
    head = lambda i: (0, i, 0)
    return pl.pallas_call(
        body, name="mla_prep_bwd", grid=(T // tm,),
        in_specs=[pl.BlockSpec((tm, 256), lambda i: (i, 6)), pl.BlockSpec((tm, 128), lambda i: (i, 14)),
                  pl.BlockSpec((tm, 128), lambda i: (i, 15)),
                  pl.BlockSpec((tm, 128), row), pl.BlockSpec((tm, 128), row),
                  pl.BlockSpec((1, 256), fix), pl.BlockSpec((1, 128), fix),
                  pl.BlockSpec((1, 256), fix), pl.BlockSpec((1, 256), fix),
                  pl.BlockSpec((256, H * MLA_PAD), fix), pl.BlockSpec((128, H * MLA_PAD), fix),
                  pl.BlockSpec((H, tm, MLA_PAD), head), pl.BlockSpec((H, tm, MLA_PAD), head),
                  pl.BlockSpec((H, tm, MLA_V), head)],
        out_specs=[pl.BlockSpec((tm, 256), row), pl.BlockSpec((tm, 128), row), pl.BlockSpec((tm, 128), row),
                   pl.BlockSpec((tm, 256), row), pl.BlockSpec((tm, 128), row),
                   pl.BlockSpec((tm, H * MLA_PAD), row), pl.BlockSpec((tm, H * MLA_PAD), row),
                   pl.BlockSpec((1, 256), fix), pl.BlockSpec((1, 128), fix),
                   pl.BlockSpec((1, 256), fix), pl.BlockSpec((1, 256), fix)],
        out_shape=[jax.ShapeDtypeStruct((T, 256), F32), jax.ShapeDtypeStruct((T, 128), F32),
                   jax.ShapeDtypeStruct((T, 128), F32),
                   jax.ShapeDtypeStruct((T, 256), BF16), jax.ShapeDtypeStruct((T, 128), BF16),
                   jax.ShapeDtypeStruct((T, H * MLA_PAD), BF16), jax.ShapeDtypeStruct((T, H * MLA_PAD), BF16),
                   jax.ShapeDtypeStruct((1, 256), F32), jax.ShapeDtypeStruct((1, 128), F32),
                   jax.ShapeDtypeStruct((1, 256), F32), jax.ShapeDtypeStruct((1, 256), F32)],
        compiler_params=_params(1),
    )(proj, proj, proj, cos, sin, g_qa, g_kva, g_q, g_k, wq, wkv, dq, dk, dv)


def _causal(s, qi, ki, tq, tk):
    row = lax.broadcasted_iota(jnp.int32, (tq, tk), 0) + qi * tq
    col = lax.broadcasted_iota(jnp.int32, (tq, tk), 1) + ki * tk
    return jnp.where(col <= row, s, NEG)


def _mla_attn(q, k, v, tq=512):
    H, T, _ = q.shape
    nq = T // tq

    def body(q_ref, k_ref, v_ref, o_ref, lse_ref, m_s, l_s, acc):
        qi, ki = pl.program_id(1), pl.program_id(2)

        @pl.when(ki == 0)
        def _():
            m_s[...] = jnp.full_like(m_s, NEG)
            l_s[...] = jnp.zeros_like(l_s)
            acc[...] = jnp.zeros_like(acc)

        @pl.when(ki <= qi)
        def _():
            s = _causal(_dot_nt(q_ref[...], k_ref[...]), qi, ki, tq, tq)
            m_new = jnp.maximum(m_s[...], jnp.max(s, axis=-1, keepdims=True))
            alpha = jnp.exp(m_s[...] - m_new)
            p = jnp.exp(s - m_new)
            l_s[...] = alpha * l_s[...] + jnp.sum(p, axis=-1, keepdims=True)
            acc[...] = alpha * acc[...] + _dot(p.astype(BF16), v_ref[...])
            m_s[...] = m_new

        @pl.when(ki == qi)
        def _():
            o_ref[...] = acc[...] / l_s[...]
            lse_ref[...] = jnp.broadcast_to(m_s[...] + jnp.log(l_s[...]), lse_ref.shape)

    kv = lambda h, qi, ki: (h, jnp.minimum(ki, qi), 0)
    return pl.pallas_call(
        body, name="mla_attn", grid=(H, nq, nq),
        in_specs=[pl.BlockSpec((None, tq, MLA_PAD), lambda h, qi, ki: (h, qi, 0)),
                  pl.BlockSpec((None, tq, MLA_PAD), kv), pl.BlockSpec((None, tq, MLA_V), kv)],
        out_specs=[pl.BlockSpec((tq, MLA_V), lambda h, qi, ki: (qi, h)),
                   pl.BlockSpec((None, tq, 128), lambda h, qi, ki: (h, qi, 0))],
        out_shape=[jax.ShapeDtypeStruct((T, H * MLA_V), F32), jax.ShapeDtypeStruct((H, T, 128), F32)],
        scratch_shapes=[pltpu.VMEM((tq, 1), F32), pltpu.VMEM((tq, 1), F32), pltpu.VMEM((tq, MLA_V), F32)],
        compiler_params=_params(3),
    )(q, k, v)


def _mla_attn_bwd(q, k, v, o, lse, do, tq=512):
    H, T, _ = q.shape
    nq = T // tq

    def body(q_ref, k_ref, v_ref, o_ref, lse_ref, do_ref, dq_ref, dk_ref, dv_ref, dk_s, dv_s):
        ki, qi = pl.program_id(1), pl.program_id(2)

        @pl.when((ki == 0) & (qi == 0))
        def _():
            dq_ref[...] = jnp.zeros_like(dq_ref)

        @pl.when(qi == 0)
        def _():
            dk_s[...] = jnp.zeros_like(dk_s)
            dv_s[...] = jnp.zeros_like(dv_s)

        @pl.when(qi >= ki)
        def _():
            qv, kv_, dov = q_ref[...], k_ref[...], do_ref[...]
            delta = jnp.sum(dov * o_ref[...], axis=-1, keepdims=True)
            lse_v = jnp.max(lse_ref[...], axis=-1, keepdims=True)
            p = jnp.exp(_causal(_dot_nt(qv, kv_), qi, ki, tq, tq) - lse_v)
            dob = dov.astype(BF16)
            dv_s[...] += _dot_tn(p.astype(BF16), dob)
            ds = (p * (_dot_nt(dob, v_ref[...]) - delta)).astype(BF16)
            dk_s[...] += _dot_tn(ds, qv)
            rows = pl.ds(pl.multiple_of(qi * tq, tq), tq)
            dq_ref[rows, :] += _dot(ds, kv_)

        @pl.when(qi == nq - 1)
        def _():
            dk_ref[...] = dk_s[...]
            dv_ref[...] = dv_s[...]

    qrow = lambda h, ki, qi: (h, jnp.maximum(qi, ki), 0)
    krow = lambda h, ki, qi: (h, ki, 0)
    return pl.pallas_call(
        body, name="mla_attn_bwd", grid=(H, nq, nq),
        in_specs=[pl.BlockSpec((None, tq, MLA_PAD), qrow), pl.BlockSpec((None, tq, MLA_PAD), krow),
                  pl.BlockSpec((None, tq, MLA_V), krow),
                  pl.BlockSpec((tq, MLA_V), lambda h, ki, qi: (jnp.maximum(qi, ki), h)),
                  pl.BlockSpec((None, tq, 128), qrow),
                  pl.BlockSpec((tq, MLA_V), lambda h, ki, qi: (jnp.maximum(qi, ki), h))],
        out_specs=[pl.BlockSpec((None, T, MLA_PAD), lambda h, ki, qi: (h, 0, 0)),
                   pl.BlockSpec((None, tq, MLA_PAD), krow), pl.BlockSpec((None, tq, MLA_V), krow)],
        out_shape=[jax.ShapeDtypeStruct((H, T, MLA_PAD), F32), jax.ShapeDtypeStruct((H, T, MLA_PAD), F32),
                   jax.ShapeDtypeStruct((H, T, MLA_V), F32)],
        scratch_shapes=[pltpu.VMEM((tq, MLA_PAD), F32), pltpu.VMEM((tq, MLA_V), F32)],
        compiler_params=_params(3),
    )(q, k, v, o, lse, do)


def _pair_gain(g):
    return jnp.tile(g.reshape(1, DIL_HD), (1, 2))


def _pad_gain(g):
    return jnp.pad(g.reshape(1, MLA_QK), ((0, 0), (0, MLA_PAD - MLA_QK)))


def _local_step(x, target, w, s):
    T = x.shape[0]
    gq, gk = _pair_gain(s["dil_q_norm"]), _pair_gain(s["dil_k_norm"])
    g_q, g_k = _pad_gain(s["mla_q_norm"]), _pad_gain(s["mla_k_norm"])
    cos, sin = _rope_tables(T)
    buckets = [_bias_buckets(d) for _, d in DIL_BRANCHES]
    biases = [(jnp.take(s["rel_bias"], jnp.asarray(bp), axis=1), jnp.take(s["rel_bias"], jnp.asarray(bc), axis=1))
              for bp, bc in buckets]

    x1, h1, gate1, up1 = _ffn_fwd(x, s["ffn1_norm"], w["ffn1_w_gate"], w["ffn1_w_up"], w["ffn1_w_down"])
    hm, proj = _in_proj(x1, s["mix_norm"], w["w_in"])
    dil = None
    for (_, d), (bp, bc) in zip(DIL_BRANCHES, biases):
        dil = _dil_fwd(proj, bp, bc, gq, gk, d, dil)
    o_dil, lse_dil = dil
    q, k, v = _mla_prep(proj, cos, sin, s["mla_q_a_norm"], s["mla_kv_a_norm"], g_q, g_k, w["mla_w_q_b"], w["mla_w_kv_b"])
    o_mla, lse_mla = _mla_attn(q, k, v)
    x2, oc = _out_proj(x1, o_dil, o_mla, s["out_norm_dil"], s["out_norm_mla"], w["w_out"])
    y, h2, gate2, up2 = _ffn_fwd(x2, s["ffn2_norm"], w["ffn2_w_gate"], w["ffn2_w_up"], w["ffn2_w_down"])
    dy, loss = _loss_grad(y, target)

    gw, gs = {}, {}

    def ffn_grads(name, dy_in, x_in, h, gate, up):
        dx, a, dg, du, dyh, dgain = _ffn_bwd(dy_in, x_in, s[name + "_norm"], gate, up,
                                             w[name + "_w_gate"], w[name + "_w_up"], w[name + "_w_down"])
        gs[name + "_norm"] = dgain
        gw[name + "_w_gate"] = _matmul_tn(h, dg, 1024, 1408)
        gw[name + "_w_up"] = _matmul_tn(h, du, 1024, 1408)
        gw[name + "_w_down"] = _matmul_tn(a, dyh, 1408, 1024)
        return dx

    dx2 = ffn_grads("ffn2", dy, x2, h2, gate2, up2)
    gw["w_out"] = _matmul_tn(oc, dx2, 1024, 1024)
    do_dil, do_mla, gs["out_norm_dil"], gs["out_norm_mla"] = _out_proj_bwd(
        dx2, o_dil, o_mla, s["out_norm_dil"], s["out_norm_mla"], w["w_out"])

    dq, dk, dv = _mla_attn_bwd(q, k, v, o_mla, lse_mla, do_mla)
    (dcq, dckv, dkpe, cqn, ckvn, dqp, dkvp, gs["mla_q_a_norm"], gs["mla_kv_a_norm"], dg_q, dg_k) = _mla_prep_bwd(
        proj, cos, sin, s["mla_q_a_norm"], s["mla_kv_a_norm"], g_q, g_k, w["mla_w_q_b"], w["mla_w_kv_b"], dq, dk, dv)
    gs["mla_q_norm"], gs["mla_k_norm"] = dg_q[:, :MLA_QK], dg_k[:, :MLA_QK]
    gw["mla_w_q_b"] = _matmul_tn(cqn, dqp, 256, 1024)
    gw["mla_w_kv_b"] = _matmul_tn(ckvn, dkvp, 128, 1024)

    dqkv, dbs, dgq, dgk = None, [], 0.0, 0.0
    for (_, d), (bp, bc) in zip(DIL_BRANCHES, biases):
        dqkv, dbp, dbc, dgq_b, dgk_b = _dil_bwd(proj, o_dil, lse_dil, do_dil, bp, bc, gq, gk, d, dqkv)
        dbs += [dbp, dbc]
        dgq, dgk = dgq + dgq_b, dgk + dgk_b
    gs["dil_q_norm"] = dgq[:, :DIL_HD] + dgq[:, DIL_HD:]
    gs["dil_k_norm"] = dgk[:, :DIL_HD] + dgk[:, DIL_HD:]
    gs["rel_bias"] = _bias_grad(dbs, [b for pair in buckets for b in pair])

    dx1, dproj, gs["mix_norm"] = _in_proj_bwd(dx2, x1, s["mix_norm"], w["w_in"], *dqkv, dcq, dckv, dkpe)
    gw["w_in"] = _matmul_tn(hm, dproj, 1024, 1024)
    grad_x = ffn_grads("ffn1", dx1, x, h1, gate1, up1)
    return loss, grad_x, gw, gs


def _position():
    x, y, c = lax.axis_index("x"), lax.axis_index("y"), lax.axis_index("c")
    return x, y, c, 4 * x + 2 * y + c


def _peer(x, y, c, k):
    px = 1 - x if k & 4 else x
    py = 1 - y if k & 2 else y
    pc = 1 - c if k & 1 else c
    return (px, py, pc), 4 * px + 2 * py + pc


def _exchange(arrays, scatter):
    n = len(arrays)
    out_shapes = [jax.ShapeDtypeStruct(a.shape if sc else (N_DEV,) + a.shape, a.dtype) for a, sc in zip(arrays, scatter)]

    def body(*refs):
        ins, outs = refs[:n], refs[n:2 * n]
        send_sems, recv_sems, local_sems = refs[2 * n:]
        x, y, c, me = _position()
        local = []
        for a in range(n):
            src = ins[a].at[me] if scatter[a] else ins[a]
            local.append(pltpu.make_async_copy(src, outs[a].at[me], local_sems.at[a]))
            local[-1].start()
        copies = []
        for k in range(1, N_DEV):
            peer, peer_idx = _peer(x, y, c, k)
            for a in range(n):
                src = ins[a].at[peer_idx] if scatter[a] else ins[a]
                copies.append(pltpu.make_async_remote_copy(
                    src_ref=src, dst_ref=outs[a].at[me], send_sem=send_sems.at[a, k - 1], recv_sem=recv_sems.at[a, k - 1],
                    device_id=peer, device_id_type=pl.DeviceIdType.MESH))
                copies[-1].start()
        for cp in copies:
            cp.wait()
        for cp in local:
            cp.wait()

    return pl.pallas_call(
        body, name="exchange_scatter" if any(scatter) else "exchange_gather",
        in_specs=[pl.BlockSpec(memory_space=pl.ANY)] * n, out_specs=[pl.BlockSpec(memory_space=pl.ANY)] * n,
        out_shape=out_shapes,
        scratch_shapes=[pltpu.SemaphoreType.DMA((n, N_DEV - 1)), pltpu.SemaphoreType.DMA((n, N_DEV - 1)),
                        pltpu.SemaphoreType.DMA((n,))],
    )(*arrays)


def _adamw_math(wv, g, m, v):
    m = ADAM_B1 * m + (1.0 - ADAM_B1) * g
    v = ADAM_B2 * v + (1.0 - ADAM_B2) * (g * g)
    m_hat = m / (1.0 - ADAM_B1 ** ADAM_STEP)
    v_hat = v / (1.0 - ADAM_B2 ** ADAM_STEP)
    delta = -ADAM_LR * (m_hat / (jnp.sqrt(v_hat) + ADAM_EPS) + ADAM_WD * wv)
    return delta, m, v


def _adamw(parts, slot, wv, m, v):
    R, C = wv.shape
    tr = max(t for t in range(16, 257, 16) if R % t == 0)

    def body(p_ref, w_ref, m_ref, v_ref, g_ref, d_ref, mo_ref, vo_ref):
        g = p_ref[0].astype(F32)
        for j in range(1, N_DEV):
            g = g + p_ref[j].astype(F32)
        d, mn, vn = _adamw_math(w_ref[...], g, m_ref[...], v_ref[...])
        g_ref[...] = g
        d_ref[...] = d
        mo_ref[...] = mn
        vo_ref[...] = vn

    row = lambda i: (i, 0)
    if slot is None:
        p_spec = pl.BlockSpec((N_DEV, tr, C), lambda i: (0, i, 0))
    else:
        p_spec = pl.BlockSpec((N_DEV, None, tr, C), lambda i: (0, slot, i, 0))
    out = jax.ShapeDtypeStruct((R, C), F32)
    return pl.pallas_call(
        body, name="adamw", grid=(R // tr,),
        in_specs=[p_spec, pl.BlockSpec((tr, C), row), pl.BlockSpec((tr, C), row), pl.BlockSpec((tr, C), row)],
        out_specs=[pl.BlockSpec((tr, C), row)] * 4, out_shape=[out] * 4,
        compiler_params=_params(1),
    )(parts, wv, m, v)


_COL_SHARDED = ("ffn1_w_gate", "ffn1_w_up", "ffn2_w_gate", "ffn2_w_up")
_ROW_SHARDED = ("ffn1_w_down", "ffn2_w_down")
_SMALL = ("ffn1_norm", "mix_norm", "ffn2_norm", "out_norm_dil", "out_norm_mla", "mla_q_a_norm", "rel_bias",
          "mla_q_norm", "mla_k_norm", "mla_kv_a_norm", "dil_q_norm", "dil_k_norm")
_SMALL_ROWS = 48


def _cols_to_full(g):
    return g.transpose(1, 0, 2).reshape(g.shape[1], N_DEV * g.shape[2])


def _full_to_cols(f):
    return f.reshape(f.shape[0], N_DEV, f.shape[1] // N_DEV).transpose(1, 0, 2)


def _pack_small(parts, extra):
    flat = jnp.concatenate([parts[n].reshape(-1) for n in _SMALL] + [extra.reshape(-1)])
    return jnp.pad(flat, (0, _SMALL_ROWS * 128 - flat.shape[0])).reshape(_SMALL_ROWS, 128)


def _unpack_small(packed, shapes):
    flat, out, off = packed.reshape(-1), {}, 0
    for n in _SMALL:
        size = math.prod(shapes[n])
        out[n] = flat[off:off + size].reshape(shapes[n])
        off += size
    return out, flat[off]


_NAMES = ("ffn1_norm", "ffn1_w_gate", "ffn1_w_up", "ffn1_w_down", "mix_norm", "w_in", "dil_q_norm", "dil_k_norm",
          "rel_bias", "mla_q_a_norm", "mla_w_q_b", "mla_kv_a_norm", "mla_w_kv_b", "mla_q_norm", "mla_k_norm",
          "out_norm_dil", "out_norm_mla", "w_out", "ffn2_norm", "ffn2_w_gate", "ffn2_w_up", "ffn2_w_down")


def kernel(x, ffn1_norm, ffn1_w_gate, ffn1_w_up, ffn1_w_down, mix_norm, w_in, dil_q_norm, dil_k_norm, rel_bias, mla_q_a_norm, mla_w_q_b, mla_kv_a_norm, mla_w_kv_b, mla_q_norm, mla_k_norm, out_norm_dil, out_norm_mla, w_out, ffn2_norm, ffn2_w_gate, ffn2_w_up, ffn2_w_down, loss_target, m_ffn1_norm, m_ffn1_w_gate, m_ffn1_w_up, m_ffn1_w_down, m_mix_norm, m_w_in, m_dil_q_norm, m_dil_k_norm, m_rel_bias, m_mla_q_a_norm, m_mla_w_q_b, m_mla_kv_a_norm, m_mla_w_kv_b, m_mla_q_norm, m_mla_k_norm, m_out_norm_dil, m_out_norm_mla, m_w_out, m_ffn2_norm, m_ffn2_w_gate, m_ffn2_w_up, m_ffn2_w_down, v_ffn1_norm, v_ffn1_w_gate, v_ffn1_w_up, v_ffn1_w_down, v_mix_norm, v_w_in, v_dil_q_norm, v_dil_k_norm, v_rel_bias, v_mla_q_a_norm, v_mla_w_q_b, v_mla_kv_a_norm, v_mla_w_kv_b, v_mla_q_norm, v_mla_k_norm, v_out_norm_dil, v_out_norm_mla, v_w_out, v_ffn2_norm, v_ffn2_w_gate, v_ffn2_w_up, v_ffn2_w_down):
    args = locals()
    wts = {n: args[n] for n in _NAMES}
    mom = {n: args["m_" + n] for n in _NAMES}
    var = {n: args["v_" + n] for n in _NAMES}

    shard = lambda n: wts[n][0].astype(BF16)
    cols = jnp.stack([shard(n) for n in _COL_SHARDED])
    rows = jnp.stack([shard(n) for n in _ROW_SHARDED])
    g_cols, g_rows, g_in, g_qb, g_kvb, g_out = _exchange(
        [cols, rows, shard("w_in"), shard("mla_w_q_b"), shard("mla_w_kv_b"), shard("w_out")], [False] * 6)
    w = {n: _cols_to_full(g_cols[:, i]) for i, n in enumerate(_COL_SHARDED)}
    w.update({n: g_rows[:, i].reshape(-1, g_rows.shape[-1]) for i, n in enumerate(_ROW_SHARDED)})
    w["w_in"] = jnp.pad(_cols_to_full(g_in), ((0, 0), (0, PROJ_PAD - PROJ_COLS)))
    wq = _cols_to_full(g_qb).reshape(-1, MLA_HEADS, MLA_QK)
    w["mla_w_q_b"] = jnp.pad(wq, ((0, 0), (0, 0), (0, MLA_PAD - MLA_QK))).reshape(-1, MLA_HEADS * MLA_PAD)
    w["mla_w_kv_b"] = _cols_to_full(g_kvb)
    w["w_out"] = g_out.reshape(-1, g_out.shape[-1])
    small = {n: wts[n].reshape(1, -1) if n != "rel_bias" else wts[n] for n in _SMALL}

    loss, grad_x, gw, gs = _local_step(x[0], loss_target[0], w, small)

    part = lambda f: f.astype(BF16)
    p_cols = jnp.stack([part(_full_to_cols(gw[n])) for n in _COL_SHARDED], axis=1)
    p_rows = jnp.stack([part(gw[n].reshape(N_DEV, -1, gw[n].shape[-1])) for n in _ROW_SHARDED], axis=1)
    p_in = part(_full_to_cols(gw["w_in"][:, :PROJ_COLS]))
    gq_full = gw["mla_w_q_b"].reshape(-1, MLA_HEADS, MLA_PAD)[:, :, :MLA_QK].reshape(-1, MLA_HEADS * MLA_QK)
    p_qb = part(_full_to_cols(gq_full))
    p_kvb = part(_full_to_cols(gw["mla_w_kv_b"]))
    p_out = part(gw["w_out"].reshape(N_DEV, -1, gw["w_out"].shape[-1]))
    p_small = _pack_small(gs, loss[0, 0])
    r_cols, r_rows, r_in, r_qb, r_kvb, r_out, r_small = _exchange(
        [p_cols, p_rows, p_in, p_qb, p_kvb, p_out, p_small], [True] * 6 + [False])

    res = {}
    for i, n in enumerate(_COL_SHARDED):
        res[n] = _adamw(r_cols, i, wts[n][0], mom[n][0], var[n][0])
    for i, n in enumerate(_ROW_SHARDED):
        res[n] = _adamw(r_rows, i, wts[n][0], mom[n][0], var[n][0])
    for n, r in (("w_in", r_in), ("mla_w_q_b", r_qb), ("mla_w_kv_b", r_kvb), ("w_out", r_out)):
        res[n] = _adamw(r, None, wts[n][0], mom[n][0], var[n][0])
    shapes = {n: wts[n].shape for n in _SMALL}
    zero = jnp.zeros((), F32)
    packed = _adamw(r_small, None, _pack_small(wts, zero), _pack_small(mom, zero), _pack_small(var, zero))
    loss_total = None
    for slot, q in enumerate(packed):
        vals, extra = _unpack_small(q, shapes)
        if slot == 0:
            loss_total = extra
        for n in _SMALL:
            res.setdefault(n, [None] * 4)[slot] = vals[n]
    outs = [loss_total, grad_x[None]]
    for slot in range(4):
        outs += [res[n][slot].reshape(wts[n].shape) for n in _NAMES]
    return tuple(outs)
```

```python
import math

import numpy as np
import jax
import jax.numpy as jnp
from jax import lax
from jax.experimental import pallas as pl
from jax.experimental.pallas import tpu as pltpu

F32, BF16 = jnp.float32, jnp.bfloat16
EPS = 1e-6
NEG = -1e30
N_DEV = 8

DIL_HEADS, DIL_HD = 8, 64
DIL_WIDTH = DIL_HEADS * DIL_HD
DIL_BRANCHES = ((128, 1), (512, 4), (2048, 16))
DIL_BLOCK = 128
MLA_HEADS, MLA_NOPE, MLA_ROPE, MLA_V = 4, 128, 64, 128
MLA_QK = MLA_NOPE + MLA_ROPE
MLA_PAD = 256
ROPE_BASE = 10000.0
REL_BUCKETS, REL_MAX_DIST = 32, 2048
PROJ_COLS, PROJ_PAD = 1984, 2048
FFN_RESID = 0.5
ADAM_LR, ADAM_B1, ADAM_B2, ADAM_EPS, ADAM_WD, ADAM_STEP = 0.001, 0.9, 0.999, 1e-08, 0.01, 10
VMEM_LIMIT = 56 * 1024 * 1024

_NT = (((1,), (1,)), ((), ()))
_TN = (((0,), (0,)), ((), ()))


def _dot(a, b):
    return jnp.dot(a, b, preferred_element_type=F32)


def _dot_nt(a, b):
    return lax.dot_general(a, b, _NT, preferred_element_type=F32)


def _dot_tn(a, b):
    return lax.dot_general(a, b, _TN, preferred_element_type=F32)


def _params(n_axes):
    return pltpu.CompilerParams(dimension_semantics=("arbitrary",) * n_axes, vmem_limit_bytes=VMEM_LIMIT)


def _rstd(x, n=None):
    n = x.shape[-1] if n is None else n
    return lax.rsqrt(jnp.sum(x * x, axis=-1, keepdims=True) / n + EPS)


def _rms_bwd(dy, x, g, r, n=None):
    n = x.shape[-1] if n is None else n
    u = dy * g
    dx = r * u - x * (r * r * r) * (jnp.sum(u * x, axis=-1, keepdims=True) / n)
    return dx, dy * x * r


def _sigmoid(x):
    return 1.0 / (1.0 + jnp.exp(-x))


def _split3(x):
    parts = []
    for _ in range(3):
        xb = x.astype(BF16)
        parts.append(xb)
        x = x - xb.astype(F32)
    return parts


def _ffn_fwd(x, gain, wg, wu, wd, tm=512, tf=256):
    T, D = x.shape
    F = wg.shape[1]
    nj = F // tf

    def body(x_ref, g_ref, wg_ref, wu_ref, wd_ref, xo_ref, h_ref, gate_ref, up_ref, acc):
        j = pl.program_id(1)

        @pl.when(j == 0)
        def _():
            xv = x_ref[...]
            h_ref[...] = (xv * _rstd(xv) * g_ref[...]).astype(BF16)
            acc[...] = jnp.zeros_like(acc)

        h = h_ref[...]
        g = _dot(h, wg_ref[...])
        u = _dot(h, wu_ref[...])
        gate_ref[...] = g.astype(BF16)
        up_ref[...] = u.astype(BF16)
        a = (g * _sigmoid(g) * u).astype(BF16)
        acc[...] += _dot(a, wd_ref[...])

        @pl.when(j == nj - 1)
        def _():
            xo_ref[...] = x_ref[...] + FFN_RESID * acc[...]

    return pl.pallas_call(
        body, name="ffn_fwd", grid=(T // tm, nj),
        in_specs=[pl.BlockSpec((tm, D), lambda i, j: (i, 0)), pl.BlockSpec((1, D), lambda i, j: (0, 0)),
                  pl.BlockSpec((D, tf), lambda i, j: (0, j)), pl.BlockSpec((D, tf), lambda i, j: (0, j)),
                  pl.BlockSpec((tf, D), lambda i, j: (j, 0))],
        out_specs=[pl.BlockSpec((tm, D), lambda i, j: (i, 0)), pl.BlockSpec((tm, D), lambda i, j: (i, 0)),
                   pl.BlockSpec((tm, tf), lambda i, j: (i, j)), pl.BlockSpec((tm, tf), lambda i, j: (i, j))],
        out_shape=[jax.ShapeDtypeStruct((T, D), F32), jax.ShapeDtypeStruct((T, D), BF16),
                   jax.ShapeDtypeStruct((T, F), BF16), jax.ShapeDtypeStruct((T, F), BF16)],
        scratch_shapes=[pltpu.VMEM((tm, D), F32)],
        compiler_params=_params(2),
    )(x, gain, wg, wu, wd)


def _ffn_bwd(dy, x, gain, gate, up, wg, wu, wd, tm=512, tf=256):
    T, D = x.shape
    F = wg.shape[1]
    nj = F // tf

    def body(dy_ref, x_ref, g_ref, gate_ref, up_ref, wg_ref, wu_ref, wd_ref,
             dx_ref, a_ref, dg_ref, du_ref, dyh_ref, dgain_ref, acc):
        i, j = pl.program_id(0), pl.program_id(1)

        @pl.when((i == 0) & (j == 0))
        def _():
            dgain_ref[...] = jnp.zeros_like(dgain_ref)

        @pl.when(j == 0)
        def _():
            dyh_ref[...] = (FFN_RESID * dy_ref[...]).astype(BF16)
            acc[...] = jnp.zeros_like(acc)

        da = _dot_nt(dyh_ref[...], wd_ref[...])
        g = gate_ref[...].astype(F32)
        u = up_ref[...].astype(F32)
        sig = _sigmoid(g)
        s = g * sig
        a_ref[...] = (s * u).astype(BF16)
        dg = (da * u * (sig * (1.0 + g * (1.0 - sig)))).astype(BF16)
        du = (da * s).astype(BF16)
        dg_ref[...] = dg
        du_ref[...] = du
        acc[...] += _dot_nt(dg, wg_ref[...]) + _dot_nt(du, wu_ref[...])

        @pl.when(j == nj - 1)
        def _():
            xv = x_ref[...]
            dxn, dgc = _rms_bwd(acc[...], xv, g_ref[...], _rstd(xv))
            dx_ref[...] = dy_ref[...] + dxn
            dgain_ref[...] += jnp.sum(dgc, axis=0, keepdims=True)

    return pl.pallas_call(
        body, name="ffn_bwd", grid=(T // tm, nj),
        in_specs=[pl.BlockSpec((tm, D), lambda i, j: (i, 0)), pl.BlockSpec((tm, D), lambda i, j: (i, 0)),
                  pl.BlockSpec((1, D), lambda i, j: (0, 0)),
                  pl.BlockSpec((tm, tf), lambda i, j: (i, j)), pl.BlockSpec((tm, tf), lambda i, j: (i, j)),
                  pl.BlockSpec((D, tf), lambda i, j: (0, j)), pl.BlockSpec((D, tf), lambda i, j: (0, j)),
                  pl.BlockSpec((tf, D), lambda i, j: (j, 0))],
        out_specs=[pl.BlockSpec((tm, D), lambda i, j: (i, 0)),
                   pl.BlockSpec((tm, tf), lambda i, j: (i, j)), pl.BlockSpec((tm, tf), lambda i, j: (i, j)),
                   pl.BlockSpec((tm, tf), lambda i, j: (i, j)),
                   pl.BlockSpec((tm, D), lambda i, j: (i, 0)), pl.BlockSpec((1, D), lambda i, j: (0, 0))],
        out_shape=[jax.ShapeDtypeStruct((T, D), F32), jax.ShapeDtypeStruct((T, F), BF16),
                   jax.ShapeDtypeStruct((T, F), BF16), jax.ShapeDtypeStruct((T, F), BF16),
                   jax.ShapeDtypeStruct((T, D), BF16), jax.ShapeDtypeStruct((1, D), F32)],
        scratch_shapes=[pltpu.VMEM((tm, D), F32)],
        compiler_params=_params(2),
    )(dy, x, gain, gate, up, wg, wu, wd)


def _matmul_tn(a, b, tk, tn, tt=512):
    T, K = a.shape
    N = b.shape[1]
    tk, tn = min(tk, K), min(tn, N)

    def body(a_ref, b_ref, o_ref):
        @pl.when(pl.program_id(2) == 0)
        def _():
            o_ref[...] = jnp.zeros_like(o_ref)

        o_ref[...] += _dot_tn(a_ref[...].astype(BF16), b_ref[...].astype(BF16))

    return pl.pallas_call(
        body, name="matmul_tn", grid=(K // tk, N // tn, T // tt),
        in_specs=[pl.BlockSpec((tt, tk), lambda k, n, t: (t, k)), pl.BlockSpec((tt, tn), lambda k, n, t: (t, n))],
        out_specs=pl.BlockSpec((tk, tn), lambda k, n, t: (k, n)),
        out_shape=jax.ShapeDtypeStruct((K, N), F32),
        compiler_params=_params(3),
    )(a, b)


def _loss_grad(y, target, tm=512):
    T, D = y.shape

    def body(y_ref, t_ref, dy_ref, loss_ref):
        @pl.when(pl.program_id(0) == 0)
        def _():
            loss_ref[...] = jnp.zeros_like(loss_ref)

        e = y_ref[...] - t_ref[...]
        dy_ref[...] = e * (1.0 / D)
        loss_ref[...] += (0.5 / D) * jnp.sum(e * e)

    return pl.pallas_call(
        body, name="loss_grad", grid=(T // tm,),
        in_specs=[pl.BlockSpec((tm, D), lambda i: (i, 0)), pl.BlockSpec((tm, D), lambda i: (i, 0))],
        out_specs=[pl.BlockSpec((tm, D), lambda i: (i, 0)), pl.BlockSpec((1, 128), lambda i: (0, 0))],
        out_shape=[jax.ShapeDtypeStruct((T, D), F32), jax.ShapeDtypeStruct((1, 128), F32)],
        compiler_params=_params(1),
    )(y, target)


def _in_proj(x, gain, w, tm=512):
    T, D = x.shape
    N = w.shape[1]

    def body(x_ref, g_ref, w_ref, h_ref, p_ref):
        xv = x_ref[...]
        h = (xv * _rstd(xv) * g_ref[...]).astype(BF16)
        h_ref[...] = h
        p_ref[...] = _dot(h, w_ref[...])

    return pl.pallas_call(
        body, name="in_proj", grid=(T // tm,),
        in_specs=[pl.BlockSpec((tm, D), lambda i: (i, 0)), pl.BlockSpec((1, D), lambda i: (0, 0)),
                  pl.BlockSpec((D, N), lambda i: (0, 0))],
        out_specs=[pl.BlockSpec((tm, D), lambda i: (i, 0)), pl.BlockSpec((tm, N), lambda i: (i, 0))],
        out_shape=[jax.ShapeDtypeStruct((T, D), BF16), jax.ShapeDtypeStruct((T, N), F32)],
        compiler_params=_params(1),
    )(x, gain, w)


def _in_proj_bwd(dx_up, x, gain, w, dqkv, dcq, dckv, dkpe, tm=512):
    T, D = x.shape
    N = w.shape[1]
    W = DIL_WIDTH
    nb = len(dqkv)

    def body(*refs):
        dxu_ref, x_ref, g_ref, w_ref = refs[:4]
        dil_refs = refs[4:4 + 3 * nb]
        dcq_ref, dckv_ref, dkpe_ref, dx_ref, dp_ref, dgain_ref = refs[4 + 3 * nb:]

        @pl.when(pl.program_id(0) == 0)
        def _():
            dgain_ref[...] = jnp.zeros_like(dgain_ref)

        for part in range(3):
            acc = dil_refs[part][...]
            for b in range(1, nb):
                acc = acc + dil_refs[3 * b + part][...]
            dp_ref[:, part * W:(part + 1) * W] = acc.astype(BF16)
        dp_ref[:, 3 * W:3 * W + 256] = dcq_ref[...].astype(BF16)
        dp_ref[:, 3 * W + 256:3 * W + 384] = dckv_ref[...].astype(BF16)
        dp_ref[:, 3 * W + 384:N] = dkpe_ref[...].astype(BF16)
        dh = _dot_nt(dp_ref[...], w_ref[...])
        xv = x_ref[...]
        dxn, dgc = _rms_bwd(dh, xv, g_ref[...], _rstd(xv))
        dx_ref[...] = dxu_ref[...] + dxn
        dgain_ref[...] += jnp.sum(dgc, axis=0, keepdims=True)

    row = lambda i: (i, 0)
    return pl.pallas_call(
        body, name="in_proj_bwd", grid=(T // tm,),
        in_specs=[pl.BlockSpec((tm, D), row), pl.BlockSpec((tm, D), row), pl.BlockSpec((1, D), lambda i: (0, 0)),
                  pl.BlockSpec((D, N), lambda i: (0, 0))] + [pl.BlockSpec((tm, W), row)] * (3 * nb)
                 + [pl.BlockSpec((tm, 256), row), pl.BlockSpec((tm, 128), row), pl.BlockSpec((tm, 128), row)],
        out_specs=[pl.BlockSpec((tm, D), row), pl.BlockSpec((tm, N), row), pl.BlockSpec((1, D), lambda i: (0, 0))],
        out_shape=[jax.ShapeDtypeStruct((T, D), F32), jax.ShapeDtypeStruct((T, N), BF16),
                   jax.ShapeDtypeStruct((1, D), F32)],
        compiler_params=_params(1),
    )(dx_up, x, gain, w, *[a for triple in dqkv for a in triple], dcq, dckv, dkpe)


def _out_proj(x, o_dil, o_mla, g_dil, g_mla, w, tm=512):
    T, D = x.shape
    W = o_dil.shape[1]

    def body(x_ref, od_ref, om_ref, gd_ref, gm_ref, w_ref, xo_ref, oc_ref):
        od, om = od_ref[...], om_ref[...]
        oc_ref[:, 0:W] = (od * _rstd(od) * gd_ref[...]).astype(BF16)
        oc_ref[:, W:2 * W] = (om * _rstd(om) * gm_ref[...]).astype(BF16)
        xo_ref[...] = x_ref[...] + _dot(oc_ref[...], w_ref[...])

    row = lambda i: (i, 0)
    fix = lambda i: (0, 0)
    return pl.pallas_call(
        body, name="out_proj", grid=(T // tm,),
        in_specs=[pl.BlockSpec((tm, D), row), pl.BlockSpec((tm, W), row), pl.BlockSpec((tm, W), row),
                  pl.BlockSpec((1, W), fix), pl.BlockSpec((1, W), fix), pl.BlockSpec((2 * W, D), fix)],
        out_specs=[pl.BlockSpec((tm, D), row), pl.BlockSpec((tm, 2 * W), row)],
        out_shape=[jax.ShapeDtypeStruct((T, D), F32), jax.ShapeDtypeStruct((T, 2 * W), BF16)],
        compiler_params=_params(1),
    )(x, o_dil, o_mla, g_dil, g_mla, w)


def _out_proj_bwd(dx, o_dil, o_mla, g_dil, g_mla, w, tm=512):
    T, D = dx.shape
    W = o_dil.shape[1]

    def body(dx_ref, od_ref, om_ref, gd_ref, gm_ref, w_ref, dod_ref, dom_ref, dgd_ref, dgm_ref):
        @pl.when(pl.program_id(0) == 0)
        def _():
            dgd_ref[...] = jnp.zeros_like(dgd_ref)
            dgm_ref[...] = jnp.zeros_like(dgm_ref)

        doc = _dot_nt(dx_ref[...].astype(BF16), w_ref[...])
        od, om = od_ref[...], om_ref[...]
        dod, dgd = _rms_bwd(doc[:, 0:W], od, gd_ref[...], _rstd(od))
        dom, dgm = _rms_bwd(doc[:, W:2 * W], om, gm_ref[...], _rstd(om))
        dod_ref[...] = dod
        dom_ref[...] = dom
        dgd_ref[...] += jnp.sum(dgd, axis=0, keepdims=True)
        dgm_ref[...] += jnp.sum(dgm, axis=0, keepdims=True)

    row = lambda i: (i, 0)
    fix = lambda i: (0, 0)
    return pl.pallas_call(
        body, name="out_proj_bwd", grid=(T // tm,),
        in_specs=[pl.BlockSpec((tm, D), row), pl.BlockSpec((tm, W), row), pl.BlockSpec((tm, W), row),
                  pl.BlockSpec((1, W), fix), pl.BlockSpec((1, W), fix), pl.BlockSpec((2 * W, D), fix)],
        out_specs=[pl.BlockSpec((tm, W), row), pl.BlockSpec((tm, W), row),
                   pl.BlockSpec((1, W), fix), pl.BlockSpec((1, W), fix)],
        out_shape=[jax.ShapeDtypeStruct((T, W), F32), jax.ShapeDtypeStruct((T, W), F32),
                   jax.ShapeDtypeStruct((1, W), F32), jax.ShapeDtypeStruct((1, W), F32)],
        compiler_params=_params(1),
    )(dx, o_dil, o_mla, g_dil, g_mla, w)


def _pair_rstd(x, lo):
    sq = x * x
    s0 = jnp.sum(jnp.where(lo, sq, 0.0), axis=-1, keepdims=True)
    s1 = jnp.sum(jnp.where(lo, 0.0, sq), axis=-1, keepdims=True)
    return jnp.where(lo, lax.rsqrt(s0 / DIL_HD + EPS), lax.rsqrt(s1 / DIL_HD + EPS))


def _pair_rms_bwd(dn, x, r, g, lo):
    u = dn * g
    t = u * x
    d0 = jnp.sum(jnp.where(lo, t, 0.0), axis=-1, keepdims=True)
    d1 = jnp.sum(jnp.where(lo, 0.0, t), axis=-1, keepdims=True)
    dx = r * u - x * (r * r * r) * (jnp.where(lo, d0, d1) / DIL_HD)
    return dx, jnp.sum(dn * x * r, axis=0, keepdims=True)


def _pair_col(x, lo, e):
    sel = lo if e == 0 else jnp.logical_not(lo)
    return jnp.max(jnp.where(sel, x, NEG), axis=-1, keepdims=True)


def _dil_masks(n):
    row = lax.broadcasted_iota(jnp.int32, (DIL_BLOCK, DIL_BLOCK), 0)
    col = lax.broadcasted_iota(jnp.int32, (DIL_BLOCK, DIL_BLOCK), 1)
    return col < DIL_HD, jnp.logical_and(col >= row, n > 0), col <= row


def _dil_pairs(d):
    return 4 if d == 1 else 1


def _sub_rows(r, d):
    return pl.ds(r, DIL_BLOCK, stride=d) if d > 1 else pl.ds(0, DIL_BLOCK)


def _split_subsequences(pairs, d, P):
    for r in range(d):
        for p in range(P):
            for block, scratch in pairs:
                scratch[r * P + p] = block[_sub_rows(r, d), pl.ds(128 * p, 128)]


def _merge_subsequences(pairs, d, P):
    for r in range(d):
        for p in range(P):
            for block, scratch in pairs:
                block[_sub_rows(r, d), pl.ds(128 * p, 128)] = scratch[r * P + p]


def _dil_fwd(proj, bias_p, bias_c, gq, gk, d, prev):
    T = proj.shape[0]
    P = _dil_pairs(d)
    rows, cw, n_it = DIL_BLOCK * d, 128 * P, d * P
    nblk = T // rows
    has_prev = prev is not None
    scale = DIL_HD ** -0.5

    def body(*refs):
        q_ref, kp_ref, kc_ref, vp_ref, vc_ref, bp_ref, bc_ref, gq_ref, gk_ref = refs[:9]
        refs = refs[9:]
        if has_prev:
            oin_ref, lin_ref = refs[:2]
            refs = refs[2:]
        o_ref, l_ref, qs, kps, kcs, vps, vcs, os_, ls_ = refs[:9]
        pb, n = pl.program_id(0), pl.program_id(1)
        lo, mask_p, mask_c = _dil_masks(n)
        gqv, gkv = gq_ref[...], gk_ref[...]
        loads = [(q_ref, qs), (kp_ref, kps), (kc_ref, kcs), (vp_ref, vps), (vc_ref, vcs)]
        if has_prev:
            ois, lis = refs[9:]
            loads += [(oin_ref, ois), (lin_ref, lis)]
        _split_subsequences(loads, d, P)

        def step(i, carry):
            h0 = 2 * (pb * P + i % P)
            q = qs[i]
            qn = q * _pair_rstd(q, lo) * gqv
            kp, kc = kps[i], kcs[i]
            kpn = (kp * _pair_rstd(kp, lo) * gkv).astype(BF16)
            kcn = (kc * _pair_rstd(kc, lo) * gkv).astype(BF16)
            vp, vc = vps[i].astype(BF16), vcs[i].astype(BF16)
            o_e, l_e = [], []
            for e in range(2):
                sel = lo if e == 0 else jnp.logical_not(lo)
                qe = jnp.where(sel, qn, 0.0).astype(BF16)
                sp = jnp.where(mask_p, _dot_nt(qe, kpn) * scale + bp_ref[h0 + e], NEG)
                sc = jnp.where(mask_c, _dot_nt(qe, kcn) * scale + bc_ref[h0 + e], NEG)
                m = jnp.maximum(jnp.max(sp, axis=-1, keepdims=True), jnp.max(sc, axis=-1, keepdims=True))
                pp, pc = jnp.exp(sp - m), jnp.exp(sc - m)
                l = jnp.sum(pp, axis=-1, keepdims=True) + jnp.sum(pc, axis=-1, keepdims=True)
                o_e.append((_dot(pp.astype(BF16), vp) + _dot(pc.astype(BF16), vc)) / l)
                l_e.append(m + jnp.log(l))
            o = jnp.where(lo, o_e[0], o_e[1])
            lse = jnp.where(lo, l_e[0], l_e[1])
            if has_prev:
                lin = lis[i]
                mx = jnp.maximum(lin, lse)
                lnew = mx + jnp.log(jnp.exp(lin - mx) + jnp.exp(lse - mx))
                o = ois[i] * jnp.exp(lin - lnew) + o * jnp.exp(lse - lnew)
                lse = lnew
            os_[i] = o
            ls_[i] = lse
            return carry

        lax.fori_loop(0, n_it, step, 0)
        _merge_subsequences([(o_ref, os_), (l_ref, ls_)], d, P)

    blk = (rows, cw)
    kcol, vcol = DIL_WIDTH // cw, 2 * DIL_WIDTH // cw
    prev_n = lambda n: jnp.maximum(n - 1, 0)
    fix3 = lambda pb, n: (0, 0, 0)
    fix2 = lambda pb, n: (0, 0)
    tok = pl.BlockSpec(blk, lambda pb, n: (n, pb))
    bias_spec = pl.BlockSpec((DIL_HEADS, DIL_BLOCK, DIL_BLOCK), fix3)
    in_specs = [tok,
                pl.BlockSpec(blk, lambda pb, n: (prev_n(n), kcol + pb)), pl.BlockSpec(blk, lambda pb, n: (n, kcol + pb)),
                pl.BlockSpec(blk, lambda pb, n: (prev_n(n), vcol + pb)), pl.BlockSpec(blk, lambda pb, n: (n, vcol + pb)),
                bias_spec, bias_spec, pl.BlockSpec((1, 128), fix2), pl.BlockSpec((1, 128), fix2)]
    args = [proj, proj, proj, proj, proj, bias_p, bias_c, gq, gk]
    n_scratch = 7
    if has_prev:
        in_specs += [tok, tok]
        args += list(prev)
        n_scratch += 2
    out = jax.ShapeDtypeStruct((T, DIL_WIDTH), F32)
    return pl.pallas_call(
        body, name=f"dil_fwd_d{d}", grid=(DIL_HEADS // 2 // P, nblk), in_specs=in_specs, out_specs=[tok, tok],
        out_shape=[out, out],
        scratch_shapes=[pltpu.VMEM((n_it, DIL_BLOCK, 128), F32)] * n_scratch,
        compiler_params=_params(2),
    )(*args)


def _dil_bwd(proj, o, lse, do, bias_p, bias_c, gq, gk, d):
    T = proj.shape[0]
    P = _dil_pairs(d)
    rows, cw, n_it = DIL_BLOCK * d, 128 * P, d * P
    nblk = T // rows
    scale = DIL_HD ** -0.5

    def body(q_ref, kp_ref, kc_ref, vp_ref, vc_ref, o_ref, l_ref, do_ref, bp_ref, bc_ref, gq_ref, gk_ref,
             dq_ref, dk_ref, dv_ref, dbp_ref, dbc_ref, dgq_ref, dgk_ref,
             qs, kps, kcs, vps, vcs, os_, ls_, dos, dqs, dks, dvs, ck, cv):
        pb, n = pl.program_id(0), pl.program_id(1)
        lo, mask_p, mask_c = _dil_masks(n)
        gqv, gkv = gq_ref[...], gk_ref[...]

        @pl.when((pb == 0) & (n == 0))
        def _():
            dbp_ref[...] = jnp.zeros_like(dbp_ref)
            dbc_ref[...] = jnp.zeros_like(dbc_ref)
            dgq_ref[...] = jnp.zeros_like(dgq_ref)
            dgk_ref[...] = jnp.zeros_like(dgk_ref)

        @pl.when(n == 0)
        def _():
            ck[...] = jnp.zeros_like(ck)
            cv[...] = jnp.zeros_like(cv)

        _split_subsequences([(q_ref, qs), (kp_ref, kps), (kc_ref, kcs), (vp_ref, vps), (vc_ref, vcs),
                             (o_ref, os_), (l_ref, ls_), (do_ref, dos)], d, P)

        def finish_prev(i, kp, rkp, dkn_p, dv_p):
            dk, dgk = _pair_rms_bwd(ck[i] + dkn_p, kp, rkp, gkv, lo)
            dks[i] = dk
            dvs[i] = cv[i] + dv_p
            dgk_ref[...] += dgk

        def step(i, carry):
            h0 = 2 * (pb * P + i % P)
            q = qs[i]
            rq = _pair_rstd(q, lo)
            qn = q * rq * gqv
            qb = qn.astype(BF16)
            kp, kc = kps[i], kcs[i]
            rkp = _pair_rstd(kp, lo)
            kpn = (kp * rkp * gkv).astype(BF16)
            kcn = (kc * _pair_rstd(kc, lo) * gkv).astype(BF16)
            vp, vc = vps[i].astype(BF16), vcs[i].astype(BF16)
            dov = dos[i]
            dob = dov.astype(BF16)
            dot_o = dov * os_[i]
            lse_pair = ls_[i]
            res = []
            for e in range(2):
                sel = lo if e == 0 else jnp.logical_not(lo)
                h = h0 + e
                qe = jnp.where(sel, qn, 0.0).astype(BF16)
                doe = jnp.where(sel, dov, 0.0).astype(BF16)
                delta = jnp.sum(jnp.where(sel, dot_o, 0.0), axis=-1, keepdims=True)
                lse_e = _pair_col(lse_pair, lo, e)
                sp = jnp.where(mask_p, _dot_nt(qe, kpn) * scale + bp_ref[h], NEG)
                sc = jnp.where(mask_c, _dot_nt(qe, kcn) * scale + bc_ref[h], NEG)
                pp, pc = jnp.exp(sp - lse_e), jnp.exp(sc - lse_e)
                dsp = pp * (_dot_nt(doe, vp) - delta)
                dsc = pc * (_dot_nt(doe, vc) - delta)
                dbp_ref[h] += dsp
                dbc_ref[h] += dsc
                dspb, dscb = dsp.astype(BF16), dsc.astype(BF16)
                res.append(((_dot(dspb, kpn) + _dot(dscb, kcn)) * scale,
                            _dot_tn(dspb, qb) * scale, _dot_tn(dscb, qb) * scale,
                            _dot_tn(pp.astype(BF16), dob), _dot_tn(pc.astype(BF16), dob)))
            dqn, dkn_p, dkn_c, dv_p, dv_c = (jnp.where(lo, a, b) for a, b in zip(res[0], res[1]))
            dq, dgq = _pair_rms_bwd(dqn, q, rq, gqv, lo)
            dqs[i] = dq
            dgq_ref[...] += dgq
            finish_prev(i, kp, rkp, dkn_p, dv_p)
            ck[i] = dkn_c
            cv[i] = dv_c
            return carry

        def flush(i, carry):
            kp = kps[i]
            finish_prev(i, kp, _pair_rstd(kp, lo), 0.0, 0.0)
            return carry

        @pl.when(n < nblk)
        def _():
            lax.fori_loop(0, n_it, step, 0)
            _merge_subsequences([(dq_ref, dqs)], d, P)

        @pl.when(n == nblk)
        def _():
            lax.fori_loop(0, n_it, flush, 0)

        _merge_subsequences([(dk_ref, dks), (dv_ref, dvs)], d, P)

    blk = (rows, cw)
    kcol, vcol = DIL_WIDTH // cw, 2 * DIL_WIDTH // cw
    qn_ = lambda n: jnp.minimum(n, nblk - 1)
    pn_ = lambda n: jnp.maximum(n - 1, 0)
    fix3 = lambda pb, n: (0, 0, 0)
    fix2 = lambda pb, n: (0, 0)
    tok_q = pl.BlockSpec(blk, lambda pb, n: (qn_(n), pb))
    tok_p = pl.BlockSpec(blk, lambda pb, n: (pn_(n), pb))
    bias_spec = pl.BlockSpec((DIL_HEADS, DIL_BLOCK, DIL_BLOCK), fix3)
    gain_spec = pl.BlockSpec((1, 128), fix2)
    in_specs = [tok_q,
                pl.BlockSpec(blk, lambda pb, n: (pn_(n), kcol + pb)), pl.BlockSpec(blk, lambda pb, n: (qn_(n), kcol + pb)),
                pl.BlockSpec(blk, lambda pb, n: (pn_(n), vcol + pb)), pl.BlockSpec(blk, lambda pb, n: (qn_(n), vcol + pb)),
                tok_q, tok_q, tok_q, bias_spec, bias_spec, gain_spec, gain_spec]
    tok_shape = jax.ShapeDtypeStruct((T, DIL_WIDTH), F32)
    bias_shape = jax.ShapeDtypeStruct((DIL_HEADS, DIL_BLOCK, DIL_BLOCK), F32)
    dq, dk, dv, dbp, dbc, dgq, dgk = pl.pallas_call(
        body, name=f"dil_bwd_d{d}", grid=(DIL_HEADS // 2 // P, nblk + 1), in_specs=in_specs,
        out_specs=[tok_q, tok_p, tok_p, bias_spec, bias_spec, gain_spec, gain_spec],
        out_shape=[tok_shape, tok_shape, tok_shape, bias_shape, bias_shape,
                   jax.ShapeDtypeStruct((1, 128), F32), jax.ShapeDtypeStruct((1, 128), F32)],
        scratch_shapes=[pltpu.VMEM((n_it, DIL_BLOCK, 128), F32)] * 13,
        compiler_params=_params(2),
    )(proj, proj, proj, proj, proj, o, lse, do, bias_p, bias_c, gq, gk)
    return (dq, dk, dv), dbp, dbc, dgq, dgk


def _t5_bucket(dist):
    max_exact = REL_BUCKETS // 2
    dd = np.maximum(dist, 1).astype(np.float32)
    large = max_exact + (np.log(dd / max_exact) / np.log(REL_MAX_DIST / max_exact)
                         * (REL_BUCKETS - max_exact)).astype(np.int32)
    large = np.minimum(large, REL_BUCKETS - 1)
    return np.where(dist < max_exact, dist, large).astype(np.int32)


def _bucket_onehots():
    i = np.arange(DIL_BLOCK)[:, None]
    j = np.arange(DIL_BLOCK)[None, :]
    out = []
    for _, d in DIL_BRANCHES:
        for dist in (DIL_BLOCK + i - j, i - j):
            bucket = _t5_bucket(np.clip(dist, 0, None) * d).reshape(-1)
            out.append(jnp.asarray(np.eye(REL_BUCKETS, dtype=np.float32)[:, bucket], BF16))
    return out


def _bias_tables(rel_bias, onehots):
    n = len(onehots)

    def body(rb_ref, *refs):
        parts = _split3(rb_ref[...])
        for k in range(n):
            oh = refs[k][...]
            refs[n + k][...] = _dot(parts[0], oh) + _dot(parts[1], oh) + _dot(parts[2], oh)

    return pl.pallas_call(
        body, name="bias_tables",
        out_shape=[jax.ShapeDtypeStruct((DIL_HEADS, DIL_BLOCK * DIL_BLOCK), F32)] * n,
        compiler_params=pltpu.CompilerParams(vmem_limit_bytes=VMEM_LIMIT),
    )(rel_bias, *onehots)


def _bias_grad(dbs, onehots):
    n = len(dbs)

    def body(*refs):
        acc = jnp.zeros((DIL_HEADS, REL_BUCKETS), F32)
        for k in range(n):
            oh = refs[n + k][...]
            for part in _split3(refs[k][...]):
                acc = acc + _dot_nt(part, oh)
        refs[-1][...] = acc

    return pl.pallas_call(
        body, name="bias_grad",
        out_shape=jax.ShapeDtypeStruct((DIL_HEADS, REL_BUCKETS), F32),
        compiler_params=pltpu.CompilerParams(vmem_limit_bytes=VMEM_LIMIT),
    )(*dbs, *onehots)


def _swap_halves(x):
    lane = lax.broadcasted_iota(jnp.int32, x.shape, 1)
    first = (lane % 64) < 32
    return jnp.where(first, pltpu.roll(x, 96, 1), pltpu.roll(x, 32, 1))


def _rope_tables(T):
    pos = jnp.arange(T, dtype=F32)
    inv_freq = ROPE_BASE ** (-jnp.arange(0, MLA_ROPE, 2, dtype=F32) / MLA_ROPE)
    ang = pos[:, None] * inv_freq[None, :]
    z = jnp.zeros((T, 128 - MLA_ROPE), F32)
    cos = jnp.concatenate([jnp.cos(ang), jnp.cos(ang), z], axis=-1)
    sin = jnp.concatenate([-jnp.sin(ang), jnp.sin(ang), z], axis=-1)
    return cos, sin


def _mla_prep(proj, cos, sin, g_qa, g_kva, g_q, g_k, wq, wkv, tm=512):
    T = proj.shape[0]
    H = MLA_HEADS
    scale = MLA_QK ** -0.5

    def body(cq_ref, ckv_ref, kpe_ref, cos_ref, sin_ref, gqa_ref, gkva_ref, gq_ref, gk_ref, wq_ref, wkv_ref,
             q_ref, k_ref, v_ref):
        cosv, sinv = cos_ref[...], sin_ref[...]

        def rope(x):
            return x * cosv + _swap_halves(x) * sinv

        cq = cq_ref[...]
        qp = _dot((cq * _rstd(cq) * gqa_ref[...]).astype(BF16), wq_ref[...])
        ckv = ckv_ref[...]
        kvp = _dot((ckv * _rstd(ckv) * gkva_ref[...]).astype(BF16), wkv_ref[...])
        kpe = kpe_ref[...]
        for h in range(H):
            a = qp[:, MLA_PAD * h:MLA_PAD * (h + 1)]
            qn = a * _rstd(a, MLA_QK) * gq_ref[...]
            q_ref[h, :, 0:128] = (qn[:, 0:128] * scale).astype(BF16)
            q_ref[h, :, 128:256] = (rope(qn[:, 128:256]) * scale).astype(BF16)
            kn = kvp[:, MLA_PAD * h:MLA_PAD * h + 128]
            r = lax.rsqrt((jnp.sum(kn * kn, axis=-1, keepdims=True)
                           + jnp.sum(kpe * kpe, axis=-1, keepdims=True)) / MLA_QK + EPS)
            k_ref[h, :, 0:128] = (kn * r * gk_ref[:, 0:128]).astype(BF16)
            k_ref[h, :, 128:256] = rope(kpe * r * gk_ref[:, 128:256]).astype(BF16)
            v_ref[h] = kvp[:, MLA_PAD * h + 128:MLA_PAD * (h + 1)].astype(BF16)

    fix = lambda i: (0, 0)
    return pl.pallas_call(
        body, name="mla_prep", grid=(T // tm,),
        in_specs=[pl.BlockSpec((tm, 256), lambda i: (i, 6)), pl.BlockSpec((tm, 128), lambda i: (i, 14)),
                  pl.BlockSpec((tm, 128), lambda i: (i, 15)),
                  pl.BlockSpec((tm, 128), lambda i: (i, 0)), pl.BlockSpec((tm, 128), lambda i: (i, 0)),
                  pl.BlockSpec((1, 256), fix), pl.BlockSpec((1, 128), fix),
                  pl.BlockSpec((1, 256), fix), pl.BlockSpec((1, 256), fix),
                  pl.BlockSpec((256, H * MLA_PAD), fix), pl.BlockSpec((128, H * MLA_PAD), fix)],
        out_specs=[pl.BlockSpec((H, tm, MLA_PAD), lambda i: (0, i, 0)), pl.BlockSpec((H, tm, MLA_PAD), lambda i: (0, i, 0)),
                   pl.BlockSpec((H, tm, MLA_V), lambda i: (0, i, 0))],
        out_shape=[jax.ShapeDtypeStruct((H, T, MLA_PAD), BF16), jax.ShapeDtypeStruct((H, T, MLA_PAD), BF16),
                   jax.ShapeDtypeStruct((H, T, MLA_V), BF16)],
        compiler_params=_params(1),
    )(proj, proj, proj, cos, sin, g_qa, g_kva, g_q, g_k, wq, wkv)


def _mla_prep_bwd(proj, cos, sin, g_qa, g_kva, g_q, g_k, wq, wkv, dq, dk, dv, tm=512):
    T = proj.shape[0]
    H = MLA_HEADS
    scale = MLA_QK ** -0.5

    def body(cq_ref, ckv_ref, kpe_ref, cos_ref, sin_ref, gqa_ref, gkva_ref, gq_ref, gk_ref, wq_ref, wkv_ref,
             dq_ref, dk_ref, dv_ref,
             dcq_ref, dckv_ref, dkpe_ref, cqn_ref, ckvn_ref, dqp_ref, dkvp_ref,
             dgqa_ref, dgkva_ref, dgq_ref, dgk_ref):
        @pl.when(pl.program_id(0) == 0)
        def _():
            for ref in (dgqa_ref, dgkva_ref, dgq_ref, dgk_ref):
                ref[...] = jnp.zeros_like(ref)

        cosv, sinv = cos_ref[...], sin_ref[...]

        def rope_bwd(dy):
            return dy * cosv + _swap_halves(dy * sinv)

        cq = cq_ref[...]
        rcq = _rstd(cq)
        cqn = (cq * rcq * gqa_ref[...]).astype(BF16)
        cqn_ref[...] = cqn
        qp = _dot(cqn, wq_ref[...])
        ckv = ckv_ref[...]
        rckv = _rstd(ckv)
        ckvn = (ckv * rckv * gkva_ref[...]).astype(BF16)
        ckvn_ref[...] = ckvn
        kvp = _dot(ckvn, wkv_ref[...])
        kpe = kpe_ref[...]
        dkpe = jnp.zeros_like(kpe)
        dgq = jnp.zeros((1, MLA_PAD), F32)
        dgk = jnp.zeros((1, MLA_PAD), F32)
        for h in range(H):
            a = qp[:, MLA_PAD * h:MLA_PAD * (h + 1)]
            dqh = dq_ref[h]
            dn = jnp.concatenate([dqh[:, 0:128], rope_bwd(dqh[:, 128:256])], axis=-1) * scale
            da, dg = _rms_bwd(dn, a, gq_ref[...], _rstd(a, MLA_QK), MLA_QK)
            dgq = dgq + jnp.sum(dg, axis=0, keepdims=True)
            dqp_ref[:, MLA_PAD * h:MLA_PAD * (h + 1)] = da.astype(BF16)

            ak = jnp.concatenate([kvp[:, MLA_PAD * h:MLA_PAD * h + 128], kpe], axis=-1)
            dkh = dk_ref[h]
            dnk = jnp.concatenate([dkh[:, 0:128], rope_bwd(dkh[:, 128:256])], axis=-1)
            dak, dg = _rms_bwd(dnk, ak, gk_ref[...], _rstd(ak, MLA_QK), MLA_QK)
            dgk = dgk + jnp.sum(dg, axis=0, keepdims=True)
            dkpe = dkpe + dak[:, 128:256]
            dkvp_ref[:, MLA_PAD * h:MLA_PAD * h + 128] = dak[:, 0:128].astype(BF16)
            dkvp_ref[:, MLA_PAD * h + 128:MLA_PAD * (h + 1)] = dv_ref[h].astype(BF16)
        dkpe_ref[...] = dkpe
        dgq_ref[...] += dgq
        dgk_ref[...] += dgk
        dcq, dg = _rms_bwd(_dot_nt(dqp_ref[...], wq_ref[...]), cq, gqa_ref[...], rcq)
        dcq_ref[...] = dcq
        dgqa_ref[...] += jnp.sum(dg, axis=0, keepdims=True)
        dckv, dg = _rms_bwd(_dot_nt(dkvp_ref[...], wkv_ref[...]), ckv, gkva_ref[...], rckv)
        dckv_ref[...] = dckv
        dgkva_ref[...] += jnp.sum(dg, axis=0, keepdims=True)

    fix = lambda i: (0, 0)
    row = lambda i: (i, 0)
    head = lambda i: (0, i, 0)
    return pl.pallas_call(
        body, name="mla_prep_bwd", grid=(T // tm,),
        in_specs=[pl.BlockSpec((tm, 256), lambda i: (i, 6)), pl.BlockSpec((tm, 128), lambda i: (i, 14)),
                  pl.BlockSpec((tm, 128), lambda i: (i, 15)),
                  pl.BlockSpec((tm, 128), row), pl.BlockSpec((tm, 128), row),
                  pl.BlockSpec((1, 256), fix), pl.BlockSpec((1, 128), fix),
                  pl.BlockSpec((1, 256), fix), pl.BlockSpec((1, 256), fix),
                  pl.BlockSpec((256, H * MLA_PAD), fix), pl.BlockSpec((128, H * MLA_PAD), fix),
                  pl.BlockSpec((H, tm, MLA_PAD), head), pl.BlockSpec((H, tm, MLA_PAD), head),
                  pl.BlockSpec((H, tm, MLA_V), head)],
        out_specs=[pl.BlockSpec((tm, 256), row), pl.BlockSpec((tm, 128), row), pl.BlockSpec((tm, 128), row),
                   pl.BlockSpec((tm, 256), row), pl.BlockSpec((tm, 128), row),
                   pl.BlockSpec((tm, H * MLA_PAD), row), pl.BlockSpec((tm, H * MLA_PAD), row),
                   pl.BlockSpec((1, 256), fix), pl.BlockSpec((1, 128), fix),
                   pl.BlockSpec((1, 256), fix), pl.BlockSpec((1, 256), fix)],
        out_shape=[jax.ShapeDtypeStruct((T, 256), F32), jax.ShapeDtypeStruct((T, 128), F32),
                   jax.ShapeDtypeStruct((T, 128), F32),
                   jax.ShapeDtypeStruct((T, 256), BF16), jax.ShapeDtypeStruct((T, 128), BF16),
                   jax.ShapeDtypeStruct((T, H * MLA_PAD), BF16), jax.ShapeDtypeStruct((T, H * MLA_PAD), BF16),
                   jax.ShapeDtypeStruct((1, 256), F32), jax.ShapeDtypeStruct((1, 128), F32),
                   jax.ShapeDtypeStruct((1, 256), F32), jax.ShapeDtypeStruct((1, 256), F32)],
        compiler_params=_params(1),
    )(proj, proj, proj, cos, sin, g_qa, g_kva, g_q, g_k, wq, wkv, dq, dk, dv)


def _causal_pairs(T, tq, tk, key_major):
    pairs = [(i, j) for i in range(T // tq) for j in range(T // tk) if j * tk <= i * tq + tq - 1]
    if key_major:
        pairs.sort(key=lambda p: (p[1], p[0]))
    outer = [p[1] if key_major else p[0] for p in pairs]
    first = [int(t == 0 or outer[t] != outer[t - 1]) for t in range(len(pairs))]
    last = [int(t == len(pairs) - 1 or outer[t] != outer[t + 1]) for t in range(len(pairs))]
    tab = lambda v: jnp.asarray(np.array(v, np.int32))
    return tab([p[0] for p in pairs]), tab([p[1] for p in pairs]), tab(first), tab(last)


def _causal_scores(qv, kv, qi, ki, row0, tq, tk, masked):
    s = _dot_nt(qv, kv)
    if masked:
        row = lax.broadcasted_iota(jnp.int32, s.shape, 0) + (qi * tq + row0)
        col = lax.broadcasted_iota(jnp.int32, s.shape, 1) + ki * tk
        s = jnp.where(col <= row, s, NEG)
    return s


def _mla_attn(q, k, v, tq=512, tk=1024, rc=256):
    H, T, _ = q.shape
    tables = _causal_pairs(T, tq, tk, key_major=False)

    def body(qt, kt, ft, lt, q_ref, k_ref, v_ref, o_ref, lse_ref, m_s, l_s, acc):
        t = pl.program_id(1)
        qi, ki = qt[t], kt[t]

        @pl.when(ft[t] == 1)
        def _():
            m_s[...] = jnp.full_like(m_s, NEG)
            l_s[...] = jnp.zeros_like(l_s)
            acc[...] = jnp.zeros_like(acc)

        def update(masked):
            kk, vv = k_ref[...], v_ref[...]
            for c in range(tq // rc):
                rows = pl.ds(c * rc, rc)
                s = _causal_scores(q_ref[rows, :], kk, qi, ki, c * rc, tq, tk, masked)
                m_old = m_s[rows, :]
                m_new = jnp.maximum(m_old, jnp.max(s, axis=-1, keepdims=True))
                alpha = jnp.exp(m_old - m_new)
                p = jnp.exp(s - m_new)
                l_s[rows, :] = alpha * l_s[rows, :] + jnp.sum(p, axis=-1, keepdims=True)
                acc[rows, :] = alpha * acc[rows, :] + _dot(p.astype(BF16), vv)
                m_s[rows, :] = m_new

        diagonal = (ki + 1) * tk - 1 > qi * tq

        @pl.when(diagonal)
        def _():
            update(True)

        @pl.when(jnp.logical_not(diagonal))
        def _():
            update(False)

        @pl.when(lt[t] == 1)
        def _():
            o_ref[...] = acc[...] / l_s[...]
            lse_ref[...] = jnp.broadcast_to(m_s[...] + jnp.log(l_s[...]), lse_ref.shape)

    qrow = lambda h, t, qt, kt, ft, lt: (h, qt[t], 0)
    krow = lambda h, t, qt, kt, ft, lt: (h, kt[t], 0)
    return pl.pallas_call(
        body, name="mla_attn",
        grid_spec=pltpu.PrefetchScalarGridSpec(
            num_scalar_prefetch=4, grid=(H, int(tables[0].shape[0])),
            in_specs=[pl.BlockSpec((None, tq, MLA_PAD), qrow), pl.BlockSpec((None, tk, MLA_PAD), krow),
                      pl.BlockSpec((None, tk, MLA_V), krow)],
            out_specs=[pl.BlockSpec((tq, MLA_V), lambda h, t, qt, kt, ft, lt: (qt[t], h)),
                       pl.BlockSpec((None, tq, 128), qrow)],
            scratch_shapes=[pltpu.VMEM((tq, 1), F32), pltpu.VMEM((tq, 1), F32), pltpu.VMEM((tq, MLA_V), F32)]),
        out_shape=[jax.ShapeDtypeStruct((T, H * MLA_V), F32), jax.ShapeDtypeStruct((H, T, 128), F32)],
        compiler_params=_params(2),
    )(*tables, q, k, v)


def _mla_attn_bwd(q, k, v, o, lse, do, tq=512, tk=512, rc=256):
    H, T, _ = q.shape
    tables = _causal_pairs(T, tq, tk, key_major=True)

    def body(qt, kt, ft, lt, q_ref, k_ref, v_ref, o_ref, lse_ref, do_ref, dq_ref, dk_ref, dv_ref, dk_s, dv_s):
        t = pl.program_id(1)
        qi, ki = qt[t], kt[t]

        @pl.when(t == 0)
        def _():
            dq_ref[...] = jnp.zeros_like(dq_ref)

        @pl.when(ft[t] == 1)
        def _():
            dk_s[...] = jnp.zeros_like(dk_s)
            dv_s[...] = jnp.zeros_like(dv_s)

        def update(masked):
            kk, vv = k_ref[...], v_ref[...]
            for c in range(tq // rc):
                rows = pl.ds(c * rc, rc)
                qv, dov = q_ref[rows, :], do_ref[rows, :]
                delta = jnp.sum(dov * o_ref[rows, :], axis=-1, keepdims=True)
                lse_v = jnp.max(lse_ref[rows, :], axis=-1, keepdims=True)
                p = jnp.exp(_causal_scores(qv, kk, qi, ki, c * rc, tq, tk, masked) - lse_v)
                dob = dov.astype(BF16)
                dv_s[...] += _dot_tn(p.astype(BF16), dob)
                ds = (p * (_dot_nt(dob, vv) - delta)).astype(BF16)
                dk_s[...] += _dot_tn(ds, qv)
                out_rows = pl.ds(pl.multiple_of(qi * tq + c * rc, rc), rc)
                dq_ref[out_rows, :] += _dot(ds, kk)

        diagonal = (ki + 1) * tk - 1 > qi * tq

        @pl.when(diagonal)
        def _():
            update(True)

        @pl.when(jnp.logical_not(diagonal))
        def _():
            update(False)

        @pl.when(lt[t] == 1)
        def _():
            dk_ref[...] = dk_s[...]
            dv_ref[...] = dv_s[...]

    qrow = lambda h, t, qt, kt, ft, lt: (h, qt[t], 0)
    krow = lambda h, t, qt, kt, ft, lt: (h, kt[t], 0)
    qcol = lambda h, t, qt, kt, ft, lt: (qt[t], h)
    return pl.pallas_call(
        body, name="mla_attn_bwd",
        grid_spec=pltpu.PrefetchScalarGridSpec(
            num_scalar_prefetch=4, grid=(H, int(tables[0].shape[0])),
            in_specs=[pl.BlockSpec((None, tq, MLA_PAD), qrow), pl.BlockSpec((None, tk, MLA_PAD), krow),
                      pl.BlockSpec((None, tk, MLA_V), krow), pl.BlockSpec((tq, MLA_V), qcol),
                      pl.BlockSpec((None, tq, 128), qrow), pl.BlockSpec((tq, MLA_V), qcol)],
            out_specs=[pl.BlockSpec((None, T, MLA_PAD), lambda h, t, qt, kt, ft, lt: (h, 0, 0)),
                       pl.BlockSpec((None, tk, MLA_PAD), krow), pl.BlockSpec((None, tk, MLA_V), krow)],
            scratch_shapes=[pltpu.VMEM((tk, MLA_PAD), F32), pltpu.VMEM((tk, MLA_V), F32)]),
        out_shape=[jax.ShapeDtypeStruct((H, T, MLA_PAD), F32), jax.ShapeDtypeStruct((H, T, MLA_PAD), F32),
                   jax.ShapeDtypeStruct((H, T, MLA_V), F32)],
        compiler_params=_params(2),
    )(*tables, q, k, v, o, lse, do)


def _pair_gain(g):
    return jnp.tile(g.reshape(1, DIL_HD), (1, 2))


def _pad_gain(g):
    return jnp.pad(g.reshape(1, MLA_QK), ((0, 0), (0, MLA_PAD - MLA_QK)))


def _local_step(x, target, w, s):
    T = x.shape[0]
    gq, gk = _pair_gain(s["dil_q_norm"]), _pair_gain(s["dil_k_norm"])
    g_q, g_k = _pad_gain(s["mla_q_norm"]), _pad_gain(s["mla_k_norm"])
    cos, sin = _rope_tables(T)
    onehots = _bucket_onehots()
    tables = [t.reshape(DIL_HEADS, DIL_BLOCK, DIL_BLOCK) for t in _bias_tables(s["rel_bias"], onehots)]
    biases = list(zip(tables[0::2], tables[1::2]))

    x1, h1, gate1, up1 = _ffn_fwd(x, s["ffn1_norm"], w["ffn1_w_gate"], w["ffn1_w_up"], w["ffn1_w_down"])
    hm, proj = _in_proj(x1, s["mix_norm"], w["w_in"])
    dil = None
    for (_, d), (bp, bc) in zip(DIL_BRANCHES, biases):
        dil = _dil_fwd(proj, bp, bc, gq, gk, d, dil)
    o_dil, lse_dil = dil
    q, k, v = _mla_prep(proj, cos, sin, s["mla_q_a_norm"], s["mla_kv_a_norm"], g_q, g_k, w["mla_w_q_b"], w["mla_w_kv_b"])
    o_mla, lse_mla = _mla_attn(q, k, v)
    x2, oc = _out_proj(x1, o_dil, o_mla, s["out_norm_dil"], s["out_norm_mla"], w["w_out"])
    y, h2, gate2, up2 = _ffn_fwd(x2, s["ffn2_norm"], w["ffn2_w_gate"], w["ffn2_w_up"], w["ffn2_w_down"])
    dy, loss = _loss_grad(y, target)

    gw, gs = {}, {}

    def ffn_grads(name, dy_in, x_in, h, gate, up):
        dx, a, dg, du, dyh, dgain = _ffn_bwd(dy_in, x_in, s[name + "_norm"], gate, up,
                                             w[name + "_w_gate"], w[name + "_w_up"], w[name + "_w_down"])
        gs[name + "_norm"] = dgain
        gw[name + "_w_gate"] = _matmul_tn(h, dg, 1024, 1408)
        gw[name + "_w_up"] = _matmul_tn(h, du, 1024, 1408)
        gw[name + "_w_down"] = _matmul_tn(a, dyh, 1408, 1024)
        return dx

    dx2 = ffn_grads("ffn2", dy, x2, h2, gate2, up2)
    gw["w_out"] = _matmul_tn(oc, dx2, 1024, 1024)
    do_dil, do_mla, gs["out_norm_dil"], gs["out_norm_mla"] = _out_proj_bwd(
        dx2, o_dil, o_mla, s["out_norm_dil"], s["out_norm_mla"], w["w_out"])

    dq, dk, dv = _mla_attn_bwd(q, k, v, o_mla, lse_mla, do_mla)
    (dcq, dckv, dkpe, cqn, ckvn, dqp, dkvp, gs["mla_q_a_norm"], gs["mla_kv_a_norm"], dg_q, dg_k) = _mla_prep_bwd(
        proj, cos, sin, s["mla_q_a_norm"], s["mla_kv_a_norm"], g_q, g_k, w["mla_w_q_b"], w["mla_w_kv_b"], dq, dk, dv)
    gs["mla_q_norm"], gs["mla_k_norm"] = dg_q[:, :MLA_QK], dg_k[:, :MLA_QK]
    gw["mla_w_q_b"] = _matmul_tn(cqn, dqp, 256, 1024)
    gw["mla_w_kv_b"] = _matmul_tn(ckvn, dkvp, 128, 1024)

    dqkv, dbs, dgq, dgk = [], [], 0.0, 0.0
    for (_, d), (bp, bc) in zip(DIL_BRANCHES, biases):
        triple, dbp, dbc, dgq_b, dgk_b = _dil_bwd(proj, o_dil, lse_dil, do_dil, bp, bc, gq, gk, d)
        dqkv.append(triple)
        dbs += [dbp.reshape(DIL_HEADS, -1), dbc.reshape(DIL_HEADS, -1)]
        dgq, dgk = dgq + dgq_b, dgk + dgk_b
    gs["dil_q_norm"] = dgq[:, :DIL_HD] + dgq[:, DIL_HD:]
    gs["dil_k_norm"] = dgk[:, :DIL_HD] + dgk[:, DIL_HD:]
    gs["rel_bias"] = _bias_grad(dbs, onehots)

    dx1, dproj, gs["mix_norm"] = _in_proj_bwd(dx2, x1, s["mix_norm"], w["w_in"], dqkv, dcq, dckv, dkpe)
    gw["w_in"] = _matmul_tn(hm, dproj, 1024, 1024)
    grad_x = ffn_grads("ffn1", dx1, x, h1, gate1, up1)
    return loss, grad_x, gw, gs


def _position():
    x, y, c = lax.axis_index("x"), lax.axis_index("y"), lax.axis_index("c")
    return x, y, c, 4 * x + 2 * y + c


def _peer(x, y, c, k):
    px = 1 - x if k & 4 else x
    py = 1 - y if k & 2 else y
    pc = 1 - c if k & 1 else c
    return (px, py, pc), 4 * px + 2 * py + pc


def _exchange(arrays, scatter):
    n = len(arrays)
    out_shapes = [jax.ShapeDtypeStruct(a.shape if sc else (N_DEV,) + a.shape, a.dtype) for a, sc in zip(arrays, scatter)]

    def body(*refs):
        ins, outs = refs[:n], refs[n:2 * n]
        send_sems, recv_sems, local_sems = refs[2 * n:]
        x, y, c, me = _position()
        local = []
        for a in range(n):
            src = ins[a].at[me] if scatter[a] else ins[a]
            local.append(pltpu.make_async_copy(src, outs[a].at[me], local_sems.at[a]))
            local[-1].start()
        copies = []
        for k in range(1, N_DEV):
            peer, peer_idx = _peer(x, y, c, k)
            for a in range(n):
                src = ins[a].at[peer_idx] if scatter[a] else ins[a]
                copies.append(pltpu.make_async_remote_copy(
                    src_ref=src, dst_ref=outs[a].at[me], send_sem=send_sems.at[a, k - 1], recv_sem=recv_sems.at[a, k - 1],
                    device_id=peer, device_id_type=pl.DeviceIdType.MESH))
                copies[-1].start()
        for cp in copies:
            cp.wait()
        for cp in local:
            cp.wait()

    return pl.pallas_call(
        body, name="exchange_scatter" if any(scatter) else "exchange_gather",
        in_specs=[pl.BlockSpec(memory_space=pl.ANY)] * n, out_specs=[pl.BlockSpec(memory_space=pl.ANY)] * n,
        out_shape=out_shapes,
        scratch_shapes=[pltpu.SemaphoreType.DMA((n, N_DEV - 1)), pltpu.SemaphoreType.DMA((n, N_DEV - 1)),
                        pltpu.SemaphoreType.DMA((n,))],
    )(*arrays)


def _adamw_math(wv, g, m, v):
    m = ADAM_B1 * m + (1.0 - ADAM_B1) * g
    v = ADAM_B2 * v + (1.0 - ADAM_B2) * (g * g)
    m_hat = m / (1.0 - ADAM_B1 ** ADAM_STEP)
    v_hat = v / (1.0 - ADAM_B2 ** ADAM_STEP)
    delta = -ADAM_LR * (m_hat / (jnp.sqrt(v_hat) + ADAM_EPS) + ADAM_WD * wv)
    return delta, m, v


def _adamw(parts, slot, wv, m, v):
    R, C = wv.shape
    tr = max(t for t in range(16, 257, 16) if R % t == 0)

    def body(p_ref, w_ref, m_ref, v_ref, g_ref, d_ref, mo_ref, vo_ref):
        g = p_ref[0].astype(F32)
        for j in range(1, N_DEV):
            g = g + p_ref[j].astype(F32)
        d, mn, vn = _adamw_math(w_ref[...], g, m_ref[...], v_ref[...])
        g_ref[...] = g
        d_ref[...] = d
        mo_ref[...] = mn
        vo_ref[...] = vn

    row = lambda i: (i, 0)
    if slot is None:
        p_spec = pl.BlockSpec((N_DEV, tr, C), lambda i: (0, i, 0))
    else:
        p_spec = pl.BlockSpec((N_DEV, None, tr, C), lambda i: (0, slot, i, 0))
    out = jax.ShapeDtypeStruct((R, C), F32)
    return pl.pallas_call(
        body, name="adamw", grid=(R // tr,),
        in_specs=[p_spec, pl.BlockSpec((tr, C), row), pl.BlockSpec((tr, C), row), pl.BlockSpec((tr, C), row)],
        out_specs=[pl.BlockSpec((tr, C), row)] * 4, out_shape=[out] * 4,
        compiler_params=_params(1),
    )(parts, wv, m, v)


_COL_SHARDED = ("ffn1_w_gate", "ffn1_w_up", "ffn2_w_gate", "ffn2_w_up")
_ROW_SHARDED = ("ffn1_w_down", "ffn2_w_down")
_SMALL = ("ffn1_norm", "mix_norm", "ffn2_norm", "out_norm_dil", "out_norm_mla", "mla_q_a_norm", "rel_bias",
          "mla_q_norm", "mla_k_norm", "mla_kv_a_norm", "dil_q_norm", "dil_k_norm")
_SMALL_ROWS = 48


def _cols_to_full(g):
    return g.transpose(1, 0, 2).reshape(g.shape[1], N_DEV * g.shape[2])


def _full_to_cols(f):
    return f.reshape(f.shape[0], N_DEV, f.shape[1] // N_DEV).transpose(1, 0, 2)


def _pack_small(parts, extra):
    flat = jnp.concatenate([parts[n].reshape(-1) for n in _SMALL] + [extra.reshape(-1)])
    return jnp.pad(flat, (0, _SMALL_ROWS * 128 - flat.shape[0])).reshape(_SMALL_ROWS, 128)


def _unpack_small(packed, shapes):
    flat, out, off = packed.reshape(-1), {}, 0
    for n in _SMALL:
        size = math.prod(shapes[n])
        out[n] = flat[off:off + size].reshape(shapes[n])
        off += size
    return out, flat[off]


_NAMES = ("ffn1_norm", "ffn1_w_gate", "ffn1_w_up", "ffn1_w_down", "mix_norm", "w_in", "dil_q_norm", "dil_k_norm",
          "rel_bias", "mla_q_a_norm", "mla_w_q_b", "mla_kv_a_norm", "mla_w_kv_b", "mla_q_norm", "mla_k_norm",
          "out_norm_dil", "out_norm_mla", "w_out", "ffn2_norm", "ffn2_w_gate", "ffn2_w_up", "ffn2_w_down")


def kernel(x, ffn1_norm, ffn1_w_gate, ffn1_w_up, ffn1_w_down, mix_norm, w_in, dil_q_norm, dil_k_norm, rel_bias, mla_q_a_norm, mla_w_q_b, mla_kv_a_norm, mla_w_kv_b, mla_q_norm, mla_k_norm, out_norm_dil, out_norm_mla, w_out, ffn2_norm, ffn2_w_gate, ffn2_w_up, ffn2_w_down, loss_target, m_ffn1_norm, m_ffn1_w_gate, m_ffn1_w_up, m_ffn1_w_down, m_mix_norm, m_w_in, m_dil_q_norm, m_dil_k_norm, m_rel_bias, m_mla_q_a_norm, m_mla_w_q_b, m_mla_kv_a_norm, m_mla_w_kv_b, m_mla_q_norm, m_mla_k_norm, m_out_norm_dil, m_out_norm_mla, m_w_out, m_ffn2_norm, m_ffn2_w_gate, m_ffn2_w_up, m_ffn2_w_down, v_ffn1_norm, v_ffn1_w_gate, v_ffn1_w_up, v_ffn1_w_down, v_mix_norm, v_w_in, v_dil_q_norm, v_dil_k_norm, v_rel_bias, v_mla_q_a_norm, v_mla_w_q_b, v_mla_kv_a_norm, v_mla_w_kv_b, v_mla_q_norm, v_mla_k_norm, v_out_norm_dil, v_out_norm_mla, v_w_out, v_ffn2_norm, v_ffn2_w_gate, v_ffn2_w_up, v_ffn2_w_down):
    args = locals()
    wts = {n: args[n] for n in _NAMES}
    mom = {n: args["m_" + n] for n in _NAMES}
    var = {n: args["v_" + n] for n in _NAMES}

    shard = lambda n: wts[n][0].astype(BF16)
    cols = jnp.stack([shard(n) for n in _COL_SHARDED])
    rows = jnp.stack([shard(n) for n in _ROW_SHARDED])
    g_cols, g_rows, g_in, g_qb, g_kvb, g_out = _exchange(
        [cols, rows, shard("w_in"), shard("mla_w_q_b"), shard("mla_w_kv_b"), shard("w_out")], [False] * 6)
    w = {n: _cols_to_full(g_cols[:, i]) for i, n in enumerate(_COL_SHARDED)}
    w.update({n: g_rows[:, i].reshape(-1, g_rows.shape[-1]) for i, n in enumerate(_ROW_SHARDED)})
    w["w_in"] = jnp.pad(_cols_to_full(g_in), ((0, 0), (0, PROJ_PAD - PROJ_COLS)))
    wq = _cols_to_full(g_qb).reshape(-1, MLA_HEADS, MLA_QK)
    w["mla_w_q_b"] = jnp.pad(wq, ((0, 0), (0, 0), (0, MLA_PAD - MLA_QK))).reshape(-1, MLA_HEADS * MLA_PAD)
    w["mla_w_kv_b"] = _cols_to_full(g_kvb)
    w["w_out"] = g_out.reshape(-1, g_out.shape[-1])
    small = {n: wts[n].reshape(1, -1) if n != "rel_bias" else wts[n] for n in _SMALL}

    loss, grad_x, gw, gs = _local_step(x[0], loss_target[0], w, small)

    part = lambda f: f.astype(BF16)
    p_cols = jnp.stack([part(_full_to_cols(gw[n])) for n in _COL_SHARDED], axis=1)
    p_rows = jnp.stack([part(gw[n].reshape(N_DEV, -1, gw[n].shape[-1])) for n in _ROW_SHARDED], axis=1)
    p_in = part(_full_to_cols(gw["w_in"][:, :PROJ_COLS]))
    gq_full = gw["mla_w_q_b"].reshape(-1, MLA_HEADS, MLA_PAD)[:, :, :MLA_QK].reshape(-1, MLA_HEADS * MLA_QK)
    p_qb = part(_full_to_cols(gq_full))
    p_kvb = part(_full_to_cols(gw["mla_w_kv_b"]))
    p_out = part(gw["w_out"].reshape(N_DEV, -1, gw["w_out"].shape[-1]))
    p_small = _pack_small(gs, loss[0, 0])
    r_cols, r_rows, r_in, r_qb, r_kvb, r_out, r_small = _exchange(
        [p_cols, p_rows, p_in, p_qb, p_kvb, p_out, p_small], [True] * 6 + [False])

    res = {}
    for i, n in enumerate(_COL_SHARDED):
        res[n] = _adamw(r_cols, i, wts[n][0], mom[n][0], var[n][0])
    for i, n in enumerate(_ROW_SHARDED):
        res[n] = _adamw(r_rows, i, wts[n][0], mom[n][0], var[n][0])
    for n, r in (("w_in", r_in), ("mla_w_q_b", r_qb), ("mla_w_kv_b", r_kvb), ("w_out", r_out)):
        res[n] = _adamw(r, None, wts[n][0], mom[n][0], var[n][0])
    shapes = {n: wts[n].shape for n in _SMALL}
    zero = jnp.zeros((), F32)
    packed = _adamw(r_small, None, _pack_small(wts, zero), _pack_small(mom, zero), _pack_small(var, zero))
    loss_total = None
    for slot, q in enumerate(packed):
        vals, extra = _unpack_small(q, shapes)
        if slot == 0:
            loss_total = extra
        for n in _SMALL:
            res.setdefault(n, [None] * 4)[slot] = vals[n]
    outs = [loss_total, grad_x[None]]
    for slot in range(4):
        outs += [res[n][slot].reshape(wts[n].shape) for n in _NAMES]
    return tuple(outs)
```

```python
import math

import numpy as np
import jax
import jax.numpy as jnp
from jax import lax
from jax.experimental import pallas as pl
from jax.experimental.pallas import tpu as pltpu

F32, BF16 = jnp.float32, jnp.bfloat16
EPS = 1e-6
NEG = -1e30
N_DEV = 8

DIL_HEADS, DIL_HD = 8, 64
DIL_WIDTH = DIL_HEADS * DIL_HD
DIL_BRANCHES = ((128, 1), (512, 4), (2048, 16))
DIL_BLOCK = 128
MLA_HEADS, MLA_NOPE, MLA_ROPE, MLA_V = 4, 128, 64, 128
MLA_QK = MLA_NOPE + MLA_ROPE
MLA_PAD = 256
ROPE_BASE = 10000.0
REL_BUCKETS, REL_MAX_DIST = 32, 2048
PROJ_COLS, PROJ_PAD = 1984, 2048
FFN_RESID = 0.5
ADAM_LR, ADAM_B1, ADAM_B2, ADAM_EPS, ADAM_WD, ADAM_STEP = 0.001, 0.9, 0.999, 1e-08, 0.01, 10
VMEM_LIMIT = 56 * 1024 * 1024

_NT = (((1,), (1,)), ((), ()))
_TN = (((0,), (0,)), ((), ()))


def _dot(a, b):
    return jnp.dot(a, b, preferred_element_type=F32)


def _dot_nt(a, b):
    return lax.dot_general(a, b, _NT, preferred_element_type=F32)


def _dot_tn(a, b):
    return lax.dot_general(a, b, _TN, preferred_element_type=F32)


def _params(n_axes):
    return pltpu.CompilerParams(dimension_semantics=("arbitrary",) * n_axes, vmem_limit_bytes=VMEM_LIMIT)


def _rstd(x, n=None):
    n = x.shape[-1] if n is None else n
    return lax.rsqrt(jnp.sum(x * x, axis=-1, keepdims=True) / n + EPS)


def _rms_bwd(dy, x, g, r, n=None):
    n = x.shape[-1] if n is None else n
    u = dy * g
    dx = r * u - x * (r * r * r) * (jnp.sum(u * x, axis=-1, keepdims=True) / n)
    return dx, dy * x * r


def _sigmoid(x):
    return 1.0 / (1.0 + jnp.exp(-x))


def _split3(x):
    parts = []
    for _ in range(3):
        xb = x.astype(BF16)
        parts.append(xb)
        x = x - xb.astype(F32)
    return parts


def _ffn_fwd(x, gain, wg, wu, wd, tm=1024, tf=256):
    T, D = x.shape
    F = wg.shape[1]
    nj = F // tf

    def body(x_ref, g_ref, wg_ref, wu_ref, wd_ref, xo_ref, h_ref, gate_ref, up_ref, acc):
        j = pl.program_id(1)

        @pl.when(j == 0)
        def _():
            xv = x_ref[...]
            h_ref[...] = (xv * _rstd(xv) * g_ref[...]).astype(BF16)
            acc[...] = jnp.zeros_like(acc)

        h = h_ref[...]
        g = _dot(h, wg_ref[...])
        u = _dot(h, wu_ref[...])
        gate_ref[...] = g.astype(BF16)
        up_ref[...] = u.astype(BF16)
        a = (g * _sigmoid(g) * u).astype(BF16)
        acc[...] += _dot(a, wd_ref[...])

        @pl.when(j == nj - 1)
        def _():
            xo_ref[...] = x_ref[...] + FFN_RESID * acc[...]

    return pl.pallas_call(
        body, name="ffn_fwd", grid=(T // tm, nj),
        in_specs=[pl.BlockSpec((tm, D), lambda i, j: (i, 0)), pl.BlockSpec((1, D), lambda i, j: (0, 0)),
                  pl.BlockSpec((D, tf), lambda i, j: (0, j)), pl.BlockSpec((D, tf), lambda i, j: (0, j)),
                  pl.BlockSpec((tf, D), lambda i, j: (j, 0))],
        out_specs=[pl.BlockSpec((tm, D), lambda i, j: (i, 0)), pl.BlockSpec((tm, D), lambda i, j: (i, 0)),
                   pl.BlockSpec((tm, tf), lambda i, j: (i, j)), pl.BlockSpec((tm, tf), lambda i, j: (i, j))],
        out_shape=[jax.ShapeDtypeStruct((T, D), F32), jax.ShapeDtypeStruct((T, D), BF16),
                   jax.ShapeDtypeStruct((T, F), BF16), jax.ShapeDtypeStruct((T, F), BF16)],
        scratch_shapes=[pltpu.VMEM((tm, D), F32)],
        compiler_params=_params(2),
    )(x, gain, wg, wu, wd)


def _ffn_bwd(dy, x, gain, gate, up, wg, wu, wd, tm=1024, tf=256):
    T, D = x.shape
    F = wg.shape[1]
    nj = F // tf

    def body(dy_ref, x_ref, g_ref, gate_ref, up_ref, wg_ref, wu_ref, wd_ref,
             dx_ref, a_ref, dg_ref, du_ref, dyh_ref, dgain_ref, acc):
        i, j = pl.program_id(0), pl.program_id(1)

        @pl.when((i == 0) & (j == 0))
        def _():
            dgain_ref[...] = jnp.zeros_like(dgain_ref)

        @pl.when(j == 0)
        def _():
            dyh_ref[...] = (FFN_RESID * dy_ref[...]).astype(BF16)
            acc[...] = jnp.zeros_like(acc)

        da = _dot_nt(dyh_ref[...], wd_ref[...])
        g = gate_ref[...].astype(F32)
        u = up_ref[...].astype(F32)
        sig = _sigmoid(g)
        s = g * sig
        a_ref[...] = (s * u).astype(BF16)
        dg = (da * u * (sig * (1.0 + g * (1.0 - sig)))).astype(BF16)
        du = (da * s).astype(BF16)
        dg_ref[...] = dg
        du_ref[...] = du
        acc[...] += _dot_nt(dg, wg_ref[...]) + _dot_nt(du, wu_ref[...])

        @pl.when(j == nj - 1)
        def _():
            xv = x_ref[...]
            dxn, dgc = _rms_bwd(acc[...], xv, g_ref[...], _rstd(xv))
            dx_ref[...] = dy_ref[...] + dxn
            dgain_ref[...] += jnp.sum(dgc, axis=0, keepdims=True)

    return pl.pallas_call(
        body, name="ffn_bwd", grid=(T // tm, nj),
        in_specs=[pl.BlockSpec((tm, D), lambda i, j: (i, 0)), pl.BlockSpec((tm, D), lambda i, j: (i, 0)),
                  pl.BlockSpec((1, D), lambda i, j: (0, 0)),
                  pl.BlockSpec((tm, tf), lambda i, j: (i, j)), pl.BlockSpec((tm, tf), lambda i, j: (i, j)),
                  pl.BlockSpec((D, tf), lambda i, j: (0, j)), pl.BlockSpec((D, tf), lambda i, j: (0, j)),
                  pl.BlockSpec((tf, D), lambda i, j: (j, 0))],
        out_specs=[pl.BlockSpec((tm, D), lambda i, j: (i, 0)),
                   pl.BlockSpec((tm, tf), lambda i, j: (i, j)), pl.BlockSpec((tm, tf), lambda i, j: (i, j)),
                   pl.BlockSpec((tm, tf), lambda i, j: (i, j)),
                   pl.BlockSpec((tm, D), lambda i, j: (i, 0)), pl.BlockSpec((1, D), lambda i, j: (0, 0))],
        out_shape=[jax.ShapeDtypeStruct((T, D), F32), jax.ShapeDtypeStruct((T, F), BF16),
                   jax.ShapeDtypeStruct((T, F), BF16), jax.ShapeDtypeStruct((T, F), BF16),
                   jax.ShapeDtypeStruct((T, D), BF16), jax.ShapeDtypeStruct((1, D), F32)],
        scratch_shapes=[pltpu.VMEM((tm, D), F32)],
        compiler_params=_params(2),
    )(dy, x, gain, gate, up, wg, wu, wd)


def _matmul_tn(a, b, tk, tn, tt=512):
    T, K = a.shape
    N = b.shape[1]
    tk, tn = min(tk, K), min(tn, N)

    def body(a_ref, b_ref, o_ref):
        @pl.when(pl.program_id(2) == 0)
        def _():
            o_ref[...] = jnp.zeros_like(o_ref)

        o_ref[...] += _dot_tn(a_ref[...].astype(BF16), b_ref[...].astype(BF16))

    return pl.pallas_call(
        body, name="matmul_tn", grid=(K // tk, N // tn, T // tt),
        in_specs=[pl.BlockSpec((tt, tk), lambda k, n, t: (t, k)), pl.BlockSpec((tt, tn), lambda k, n, t: (t, n))],
        out_specs=pl.BlockSpec((tk, tn), lambda k, n, t: (k, n)),
        out_shape=jax.ShapeDtypeStruct((K, N), F32),
        compiler_params=_params(3),
    )(a, b)


def _loss_grad(y, target, tm=512):
    T, D = y.shape

    def body(y_ref, t_ref, dy_ref, loss_ref):
        @pl.when(pl.program_id(0) == 0)
        def _():
            loss_ref[...] = jnp.zeros_like(loss_ref)

        e = y_ref[...] - t_ref[...]
        dy_ref[...] = e * (1.0 / D)
        loss_ref[...] += (0.5 / D) * jnp.sum(e * e)

    return pl.pallas_call(
        body, name="loss_grad", grid=(T // tm,),
        in_specs=[pl.BlockSpec((tm, D), lambda i: (i, 0)), pl.BlockSpec((tm, D), lambda i: (i, 0))],
        out_specs=[pl.BlockSpec((tm, D), lambda i: (i, 0)), pl.BlockSpec((1, 128), lambda i: (0, 0))],
        out_shape=[jax.ShapeDtypeStruct((T, D), F32), jax.ShapeDtypeStruct((1, 128), F32)],
        compiler_params=_params(1),
    )(y, target)


def _in_proj(x, gain, w, tm=512):
    T, D = x.shape
    N = w.shape[1]

    def body(x_ref, g_ref, w_ref, h_ref, p_ref):
        xv = x_ref[...]
        h = (xv * _rstd(xv) * g_ref[...]).astype(BF16)
        h_ref[...] = h
        p_ref[...] = _dot(h, w_ref[...])

    return pl.pallas_call(
        body, name="in_proj", grid=(T // tm,),
        in_specs=[pl.BlockSpec((tm, D), lambda i: (i, 0)), pl.BlockSpec((1, D), lambda i: (0, 0)),
                  pl.BlockSpec((D, N), lambda i: (0, 0))],
        out_specs=[pl.BlockSpec((tm, D), lambda i: (i, 0)), pl.BlockSpec((tm, N), lambda i: (i, 0))],
        out_shape=[jax.ShapeDtypeStruct((T, D), BF16), jax.ShapeDtypeStruct((T, N), F32)],
        compiler_params=_params(1),
    )(x, gain, w)


def _in_proj_bwd(dx_up, x, gain, w, dqkv, dcq, dckv, dkpe, tm=512):
    T, D = x.shape
    N = w.shape[1]
    W = DIL_WIDTH
    nb = len(dqkv)

    def body(*refs):
        dxu_ref, x_ref, g_ref, w_ref = refs[:4]
        dil_refs = refs[4:4 + 3 * nb]
        dcq_ref, dckv_ref, dkpe_ref, dx_ref, dp_ref, dgain_ref = refs[4 + 3 * nb:]

        @pl.when(pl.program_id(0) == 0)
        def _():
            dgain_ref[...] = jnp.zeros_like(dgain_ref)

        for part in range(3):
            acc = dil_refs[part][...]
            for b in range(1, nb):
                acc = acc + dil_refs[3 * b + part][...]
            dp_ref[:, part * W:(part + 1) * W] = acc.astype(BF16)
        dp_ref[:, 3 * W:3 * W + 256] = dcq_ref[...].astype(BF16)
        dp_ref[:, 3 * W + 256:3 * W + 384] = dckv_ref[...].astype(BF16)
        dp_ref[:, 3 * W + 384:N] = dkpe_ref[...].astype(BF16)
        dh = _dot_nt(dp_ref[...], w_ref[...])
        xv = x_ref[...]
        dxn, dgc = _rms_bwd(dh, xv, g_ref[...], _rstd(xv))
        dx_ref[...] = dxu_ref[...] + dxn
        dgain_ref[...] += jnp.sum(dgc, axis=0, keepdims=True)

    row = lambda i: (i, 0)
    return pl.pallas_call(
        body, name="in_proj_bwd", grid=(T // tm,),
        in_specs=[pl.BlockSpec((tm, D), row), pl.BlockSpec((tm, D), row), pl.BlockSpec((1, D), lambda i: (0, 0)),
                  pl.BlockSpec((D, N), lambda i: (0, 0))] + [pl.BlockSpec((tm, W), row)] * (3 * nb)
                 + [pl.BlockSpec((tm, 256), row), pl.BlockSpec((tm, 128), row), pl.BlockSpec((tm, 128), row)],
        out_specs=[pl.BlockSpec((tm, D), row), pl.BlockSpec((tm, N), row), pl.BlockSpec((1, D), lambda i: (0, 0))],
        out_shape=[jax.ShapeDtypeStruct((T, D), F32), jax.ShapeDtypeStruct((T, N), BF16),
                   jax.ShapeDtypeStruct((1, D), F32)],
        compiler_params=_params(1),
    )(dx_up, x, gain, w, *[a for triple in dqkv for a in triple], dcq, dckv, dkpe)


def _out_proj(x, o_dil, o_mla, g_dil, g_mla, w, tm=512):
    T, D = x.shape
    W = o_dil.shape[1]

    def body(x_ref, od_ref, om_ref, gd_ref, gm_ref, w_ref, xo_ref, oc_ref):
        od, om = od_ref[...], om_ref[...]
        oc_ref[:, 0:W] = (od * _rstd(od) * gd_ref[...]).astype(BF16)
        oc_ref[:, W:2 * W] = (om * _rstd(om) * gm_ref[...]).astype(BF16)
        xo_ref[...] = x_ref[...] + _dot(oc_ref[...], w_ref[...])

    row = lambda i: (i, 0)
    fix = lambda i: (0, 0)
    return pl.pallas_call(
        body, name="out_proj", grid=(T // tm,),
        in_specs=[pl.BlockSpec((tm, D), row), pl.BlockSpec((tm, W), row), pl.BlockSpec((tm, W), row),
                  pl.BlockSpec((1, W), fix), pl.BlockSpec((1, W), fix), pl.BlockSpec((2 * W, D), fix)],
        out_specs=[pl.BlockSpec((tm, D), row), pl.BlockSpec((tm, 2 * W), row)],
        out_shape=[jax.ShapeDtypeStruct((T, D), F32), jax.ShapeDtypeStruct((T, 2 * W), BF16)],
        compiler_params=_params(1),
    )(x, o_dil, o_mla, g_dil, g_mla, w)


def _out_proj_bwd(dx, o_dil, o_mla, g_dil, g_mla, w, tm=512):
    T, D = dx.shape
    W = o_dil.shape[1]

    def body(dx_ref, od_ref, om_ref, gd_ref, gm_ref, w_ref, dod_ref, dom_ref, dgd_ref, dgm_ref):
        @pl.when(pl.program_id(0) == 0)
        def _():
            dgd_ref[...] = jnp.zeros_like(dgd_ref)
            dgm_ref[...] = jnp.zeros_like(dgm_ref)

        doc = _dot_nt(dx_ref[...].astype(BF16), w_ref[...])
        od, om = od_ref[...], om_ref[...]
        dod, dgd = _rms_bwd(doc[:, 0:W], od, gd_ref[...], _rstd(od))
        dom, dgm = _rms_bwd(doc[:, W:2 * W], om, gm_ref[...], _rstd(om))
        dod_ref[...] = dod
        dom_ref[...] = dom
        dgd_ref[...] += jnp.sum(dgd, axis=0, keepdims=True)
        dgm_ref[...] += jnp.sum(dgm, axis=0, keepdims=True)

    row = lambda i: (i, 0)
    fix = lambda i: (0, 0)
    return pl.pallas_call(
        body, name="out_proj_bwd", grid=(T // tm,),
        in_specs=[pl.BlockSpec((tm, D), row), pl.BlockSpec((tm, W), row), pl.BlockSpec((tm, W), row),
                  pl.BlockSpec((1, W), fix), pl.BlockSpec((1, W), fix), pl.BlockSpec((2 * W, D), fix)],
        out_specs=[pl.BlockSpec((tm, W), row), pl.BlockSpec((tm, W), row),
                   pl.BlockSpec((1, W), fix), pl.BlockSpec((1, W), fix)],
        out_shape=[jax.ShapeDtypeStruct((T, W), F32), jax.ShapeDtypeStruct((T, W), F32),
                   jax.ShapeDtypeStruct((1, W), F32), jax.ShapeDtypeStruct((1, W), F32)],
        compiler_params=_params(1),
    )(dx, o_dil, o_mla, g_dil, g_mla, w)


def _pair_rstd(x, lo):
    sq = x * x
    s0 = jnp.sum(jnp.where(lo, sq, 0.0), axis=-1, keepdims=True)
    s1 = jnp.sum(jnp.where(lo, 0.0, sq), axis=-1, keepdims=True)
    return jnp.where(lo, lax.rsqrt(s0 / DIL_HD + EPS), lax.rsqrt(s1 / DIL_HD + EPS))


def _pair_rms_bwd(dn, x, r, g, lo):
    u = dn * g
    t = u * x
    d0 = jnp.sum(jnp.where(lo, t, 0.0), axis=-1, keepdims=True)
    d1 = jnp.sum(jnp.where(lo, 0.0, t), axis=-1, keepdims=True)
    dx = r * u - x * (r * r * r) * (jnp.where(lo, d0, d1) / DIL_HD)
    return dx, jnp.sum(dn * x * r, axis=0, keepdims=True)


def _pair_col(x, lo, e):
    sel = lo if e == 0 else jnp.logical_not(lo)
    return jnp.max(jnp.where(sel, x, NEG), axis=-1, keepdims=True)


def _dil_masks(n):
    row = lax.broadcasted_iota(jnp.int32, (DIL_BLOCK, DIL_BLOCK), 0)
    col = lax.broadcasted_iota(jnp.int32, (DIL_BLOCK, DIL_BLOCK), 1)
    return col < DIL_HD, jnp.logical_and(col >= row, n > 0), col <= row


def _dil_pairs(d):
    return 4 if d == 1 else 1


def _sub_rows(r, d):
    return pl.ds(r, DIL_BLOCK, stride=d) if d > 1 else pl.ds(0, DIL_BLOCK)


def _split_subsequences(pairs, d, P):
    for r in range(d):
        for p in range(P):
            for block, scratch in pairs:
                scratch[r * P + p] = block[_sub_rows(r, d), pl.ds(128 * p, 128)]


def _merge_subsequences(pairs, d, P):
    for r in range(d):
        for p in range(P):
            for block, scratch in pairs:
                block[_sub_rows(r, d), pl.ds(128 * p, 128)] = scratch[r * P + p]


def _dil_fwd(proj, bias_p, bias_c, gq, gk, d, prev):
    T = proj.shape[0]
    P = _dil_pairs(d)
    rows, cw, n_it = DIL_BLOCK * d, 128 * P, d * P
    nblk = T // rows
    has_prev = prev is not None
    scale = DIL_HD ** -0.5

    def body(*refs):
        q_ref, kp_ref, kc_ref, vp_ref, vc_ref, bp_ref, bc_ref, gq_ref, gk_ref = refs[:9]
        refs = refs[9:]
        if has_prev:
            oin_ref, lin_ref = refs[:2]
            refs = refs[2:]
        o_ref, l_ref, qs, kps, kcs, vps, vcs, os_, ls_ = refs[:9]
        pb, n = pl.program_id(0), pl.program_id(1)
        lo, mask_p, mask_c = _dil_masks(n)
        gqv, gkv = gq_ref[...], gk_ref[...]
        loads = [(q_ref, qs), (kp_ref, kps), (kc_ref, kcs), (vp_ref, vps), (vc_ref, vcs)]
        if has_prev:
            ois, lis = refs[9:]
            loads += [(oin_ref, ois), (lin_ref, lis)]
        _split_subsequences(loads, d, P)

        def step(i, carry):
            h0 = 2 * (pb * P + i % P)
            q = qs[i]
            qn = q * _pair_rstd(q, lo) * gqv
            kp, kc = kps[i], kcs[i]
            kpn = (kp * _pair_rstd(kp, lo) * gkv).astype(BF16)
            kcn = (kc * _pair_rstd(kc, lo) * gkv).astype(BF16)
            vp, vc = vps[i].astype(BF16), vcs[i].astype(BF16)
            o_e, l_e = [], []
            for e in range(2):
                sel = lo if e == 0 else jnp.logical_not(lo)
                qe = jnp.where(sel, qn, 0.0).astype(BF16)
                sp = jnp.where(mask_p, _dot_nt(qe, kpn) * scale + bp_ref[h0 + e], NEG)
                sc = jnp.where(mask_c, _dot_nt(qe, kcn) * scale + bc_ref[h0 + e], NEG)
                m = jnp.maximum(jnp.max(sp, axis=-1, keepdims=True), jnp.max(sc, axis=-1, keepdims=True))
                pp, pc = jnp.exp(sp - m), jnp.exp(sc - m)
                l = jnp.sum(pp, axis=-1, keepdims=True) + jnp.sum(pc, axis=-1, keepdims=True)
                o_e.append((_dot(pp.astype(BF16), vp) + _dot(pc.astype(BF16), vc)) / l)
                l_e.append(m + jnp.log(l))
            o = jnp.where(lo, o_e[0], o_e[1])
            lse = jnp.where(lo, l_e[0], l_e[1])
            if has_prev:
                lin = lis[i]
                mx = jnp.maximum(lin, lse)
                lnew = mx + jnp.log(jnp.exp(lin - mx) + jnp.exp(lse - mx))
                o = ois[i] * jnp.exp(lin - lnew) + o * jnp.exp(lse - lnew)
                lse = lnew
            os_[i] = o
            ls_[i] = lse
            return carry

        lax.fori_loop(0, n_it, step, 0, unroll=4)
        _merge_subsequences([(o_ref, os_), (l_ref, ls_)], d, P)

    blk = (rows, cw)
    kcol, vcol = DIL_WIDTH // cw, 2 * DIL_WIDTH // cw
    prev_n = lambda n: jnp.maximum(n - 1, 0)
    fix3 = lambda pb, n: (0, 0, 0)
    fix2 = lambda pb, n: (0, 0)
    tok = pl.BlockSpec(blk, lambda pb, n: (n, pb))
    bias_spec = pl.BlockSpec((DIL_HEADS, DIL_BLOCK, DIL_BLOCK), fix3)
    in_specs = [tok,
                pl.BlockSpec(blk, lambda pb, n: (prev_n(n), kcol + pb)), pl.BlockSpec(blk, lambda pb, n: (n, kcol + pb)),
                pl.BlockSpec(blk, lambda pb, n: (prev_n(n), vcol + pb)), pl.BlockSpec(blk, lambda pb, n: (n, vcol + pb)),
                bias_spec, bias_spec, pl.BlockSpec((1, 128), fix2), pl.BlockSpec((1, 128), fix2)]
    args = [proj, proj, proj, proj, proj, bias_p, bias_c, gq, gk]
    n_scratch = 7
    if has_prev:
        in_specs += [tok, tok]
        args += list(prev)
        n_scratch += 2
    out = jax.ShapeDtypeStruct((T, DIL_WIDTH), F32)
    return pl.pallas_call(
        body, name=f"dil_fwd_d{d}", grid=(DIL_HEADS // 2 // P, nblk), in_specs=in_specs, out_specs=[tok, tok],
        out_shape=[out, out],
        scratch_shapes=[pltpu.VMEM((n_it, DIL_BLOCK, 128), F32)] * n_scratch,
        compiler_params=_params(2),
    )(*args)


def _dil_bwd(proj, o, lse, do, bias_p, bias_c, gq, gk, d):
    T = proj.shape[0]
    P = _dil_pairs(d)
    rows, cw, n_it = DIL_BLOCK * d, 128 * P, d * P
    nblk = T // rows
    scale = DIL_HD ** -0.5

    def body(q_ref, kp_ref, kc_ref, vp_ref, vc_ref, o_ref, l_ref, do_ref, bp_ref, bc_ref, gq_ref, gk_ref,
             dq_ref, dk_ref, dv_ref, dbp_ref, dbc_ref, dgq_ref, dgk_ref,
             qs, kps, kcs, vps, vcs, os_, ls_, dos, dqs, dks, dvs, ck, cv):
        pb, n = pl.program_id(0), pl.program_id(1)
        lo, mask_p, mask_c = _dil_masks(n)
        gqv, gkv = gq_ref[...], gk_ref[...]

        @pl.when((pb == 0) & (n == 0))
        def _():
            dbp_ref[...] = jnp.zeros_like(dbp_ref)
            dbc_ref[...] = jnp.zeros_like(dbc_ref)
            dgq_ref[...] = jnp.zeros_like(dgq_ref)
            dgk_ref[...] = jnp.zeros_like(dgk_ref)

        @pl.when(n == 0)
        def _():
            ck[...] = jnp.zeros_like(ck)
            cv[...] = jnp.zeros_like(cv)

        _split_subsequences([(q_ref, qs), (kp_ref, kps), (kc_ref, kcs), (vp_ref, vps), (vc_ref, vcs),
                             (o_ref, os_), (l_ref, ls_), (do_ref, dos)], d, P)

        def finish_prev(i, kp, rkp, dkn_p, dv_p):
            dk, dgk = _pair_rms_bwd(ck[i] + dkn_p, kp, rkp, gkv, lo)
            dks[i] = dk
            dvs[i] = cv[i] + dv_p
            dgk_ref[...] += dgk

        def step(i, carry):
            h0 = 2 * (pb * P + i % P)
            q = qs[i]
            rq = _pair_rstd(q, lo)
            qn = q * rq * gqv
            qb = qn.astype(BF16)
            kp, kc = kps[i], kcs[i]
            rkp = _pair_rstd(kp, lo)
            kpn = (kp * rkp * gkv).astype(BF16)
            kcn = (kc * _pair_rstd(kc, lo) * gkv).astype(BF16)
            vp, vc = vps[i].astype(BF16), vcs[i].astype(BF16)
            dov = dos[i]
            dob = dov.astype(BF16)
            dot_o = dov * os_[i]
            lse_pair = ls_[i]
            res = []
            for e in range(2):
                sel = lo if e == 0 else jnp.logical_not(lo)
                h = h0 + e
                qe = jnp.where(sel, qn, 0.0).astype(BF16)
                doe = jnp.where(sel, dov, 0.0).astype(BF16)
                delta = jnp.sum(jnp.where(sel, dot_o, 0.0), axis=-1, keepdims=True)
                lse_e = _pair_col(lse_pair, lo, e)
                sp = jnp.where(mask_p, _dot_nt(qe, kpn) * scale + bp_ref[h], NEG)
                sc = jnp.where(mask_c, _dot_nt(qe, kcn) * scale + bc_ref[h], NEG)
                pp, pc = jnp.exp(sp - lse_e), jnp.exp(sc - lse_e)
                dsp = pp * (_dot_nt(doe, vp) - delta)
                dsc = pc * (_dot_nt(doe, vc) - delta)
                dbp_ref[h] += dsp
                dbc_ref[h] += dsc
                dspb, dscb = dsp.astype(BF16), dsc.astype(BF16)
                res.append(((_dot(dspb, kpn) + _dot(dscb, kcn)) * scale,
                            _dot_tn(dspb, qb) * scale, _dot_tn(dscb, qb) * scale,
                            _dot_tn(pp.astype(BF16), dob), _dot_tn(pc.astype(BF16), dob)))
            dqn, dkn_p, dkn_c, dv_p, dv_c = (jnp.where(lo, a, b) for a, b in zip(res[0], res[1]))
            dq, dgq = _pair_rms_bwd(dqn, q, rq, gqv, lo)
            dqs[i] = dq
            dgq_ref[...] += dgq
            finish_prev(i, kp, rkp, dkn_p, dv_p)
            ck[i] = dkn_c
            cv[i] = dv_c
            return carry

        def flush(i, carry):
            kp = kps[i]
            finish_prev(i, kp, _pair_rstd(kp, lo), 0.0, 0.0)
            return carry

        @pl.when(n < nblk)
        def _():
            lax.fori_loop(0, n_it, step, 0, unroll=2)
            _merge_subsequences([(dq_ref, dqs)], d, P)

        @pl.when(n == nblk)
        def _():
            lax.fori_loop(0, n_it, flush, 0)

        _merge_subsequences([(dk_ref, dks), (dv_ref, dvs)], d, P)

    blk = (rows, cw)
    kcol, vcol = DIL_WIDTH // cw, 2 * DIL_WIDTH // cw
    qn_ = lambda n: jnp.minimum(n, nblk - 1)
    pn_ = lambda n: jnp.maximum(n - 1, 0)
    fix3 = lambda pb, n: (0, 0, 0)
    fix2 = lambda pb, n: (0, 0)
    tok_q = pl.BlockSpec(blk, lambda pb, n: (qn_(n), pb))
    tok_p = pl.BlockSpec(blk, lambda pb, n: (pn_(n), pb))
    bias_spec = pl.BlockSpec((DIL_HEADS, DIL_BLOCK, DIL_BLOCK), fix3)
    gain_spec = pl.BlockSpec((1, 128), fix2)
    in_specs = [tok_q,
                pl.BlockSpec(blk, lambda pb, n: (pn_(n), kcol + pb)), pl.BlockSpec(blk, lambda pb, n: (qn_(n), kcol + pb)),
                pl.BlockSpec(blk, lambda pb, n: (pn_(n), vcol + pb)), pl.BlockSpec(blk, lambda pb, n: (qn_(n), vcol + pb)),
                tok_q, tok_q, tok_q, bias_spec, bias_spec, gain_spec, gain_spec]
    tok_shape = jax.ShapeDtypeStruct((T, DIL_WIDTH), F32)
    bias_shape = jax.ShapeDtypeStruct((DIL_HEADS, DIL_BLOCK, DIL_BLOCK), F32)
    dq, dk, dv, dbp, dbc, dgq, dgk = pl.pallas_call(
        body, name=f"dil_bwd_d{d}", grid=(DIL_HEADS // 2 // P, nblk + 1), in_specs=in_specs,
        out_specs=[tok_q, tok_p, tok_p, bias_spec, bias_spec, gain_spec, gain_spec],
        out_shape=[tok_shape, tok_shape, tok_shape, bias_shape, bias_shape,
                   jax.ShapeDtypeStruct((1, 128), F32), jax.ShapeDtypeStruct((1, 128), F32)],
        scratch_shapes=[pltpu.VMEM((n_it, DIL_BLOCK, 128), F32)] * 13,
        compiler_params=_params(2),
    )(proj, proj, proj, proj, proj, o, lse, do, bias_p, bias_c, gq, gk)
    return (dq, dk, dv), dbp, dbc, dgq, dgk


def _t5_bucket(dist):
    max_exact = REL_BUCKETS // 2
    dd = np.maximum(dist, 1).astype(np.float32)
    large = max_exact + (np.log(dd / max_exact) / np.log(REL_MAX_DIST / max_exact)
                         * (REL_BUCKETS - max_exact)).astype(np.int32)
    large = np.minimum(large, REL_BUCKETS - 1)
    return np.where(dist < max_exact, dist, large).astype(np.int32)


def _bucket_onehots():
    i = np.arange(DIL_BLOCK)[:, None]
    j = np.arange(DIL_BLOCK)[None, :]
    out = []
    for _, d in DIL_BRANCHES:
        for dist in (DIL_BLOCK + i - j, i - j):
            bucket = _t5_bucket(np.clip(dist, 0, None) * d).reshape(-1)
            out.append(jnp.asarray(np.eye(REL_BUCKETS, dtype=np.float32)[:, bucket], BF16))
    return out


def _bias_tables(rel_bias, onehots):
    n = len(onehots)

    def body(rb_ref, *refs):
        parts = _split3(rb_ref[...])
        for k in range(n):
            oh = refs[k][...]
            refs[n + k][...] = _dot(parts[0], oh) + _dot(parts[1], oh) + _dot(parts[2], oh)

    return pl.pallas_call(
        body, name="bias_tables",
        out_shape=[jax.ShapeDtypeStruct((DIL_HEADS, DIL_BLOCK * DIL_BLOCK), F32)] * n,
        compiler_params=pltpu.CompilerParams(vmem_limit_bytes=VMEM_LIMIT),
    )(rel_bias, *onehots)


def _bias_grad(dbs, onehots):
    n = len(dbs)

    def body(*refs):
        acc = jnp.zeros((DIL_HEADS, REL_BUCKETS), F32)
        for k in range(n):
            oh = refs[n + k][...]
            for part in _split3(refs[k][...]):
                acc = acc + _dot_nt(part, oh)
        refs[-1][...] = acc

    return pl.pallas_call(
        body, name="bias_grad",
        out_shape=jax.ShapeDtypeStruct((DIL_HEADS, REL_BUCKETS), F32),
        compiler_params=pltpu.CompilerParams(vmem_limit_bytes=VMEM_LIMIT),
    )(*dbs, *onehots)


def _swap_halves(x):
    lane = lax.broadcasted_iota(jnp.int32, x.shape, 1)
    first = (lane % 64) < 32
    return jnp.where(first, pltpu.roll(x, 96, 1), pltpu.roll(x, 32, 1))


def _rope_tables(T):
    pos = jnp.arange(T, dtype=F32)
    inv_freq = ROPE_BASE ** (-jnp.arange(0, MLA_ROPE, 2, dtype=F32) / MLA_ROPE)
    ang = pos[:, None] * inv_freq[None, :]
    z = jnp.zeros((T, 128 - MLA_ROPE), F32)
    cos = jnp.concatenate([jnp.cos(ang), jnp.cos(ang), z], axis=-1)
    sin = jnp.concatenate([-jnp.sin(ang), jnp.sin(ang), z], axis=-1)
    return cos, sin


def _mla_prep(proj, cos, sin, g_qa, g_kva, g_q, g_k, wq, wkv, tm=512):
    T = proj.shape[0]
    H = MLA_HEADS
    scale = MLA_QK ** -0.5

    def body(cq_ref, ckv_ref, kpe_ref, cos_ref, sin_ref, gqa_ref, gkva_ref, gq_ref, gk_ref, wq_ref, wkv_ref,
             q_ref, k_ref, v_ref):
        cosv, sinv = cos_ref[...], sin_ref[...]

        def rope(x):
            return x * cosv + _swap_halves(x) * sinv

        cq = cq_ref[...]
        qp = _dot((cq * _rstd(cq) * gqa_ref[...]).astype(BF16), wq_ref[...])
        ckv = ckv_ref[...]
        kvp = _dot((ckv * _rstd(ckv) * gkva_ref[...]).astype(BF16), wkv_ref[...])
        kpe = kpe_ref[...]
        for h in range(H):
            a = qp[:, MLA_PAD * h:MLA_PAD * (h + 1)]
            qn = a * _rstd(a, MLA_QK) * gq_ref[...]
            q_ref[h, :, 0:128] = (qn[:, 0:128] * scale).astype(BF16)
            q_ref[h, :, 128:256] = (rope(qn[:, 128:256]) * scale).astype(BF16)
            kn = kvp[:, MLA_PAD * h:MLA_PAD * h + 128]
            r = lax.rsqrt((jnp.sum(kn * kn, axis=-1, keepdims=True)
                           + jnp.sum(kpe * kpe, axis=-1, keepdims=True)) / MLA_QK + EPS)
            k_ref[h, :, 0:128] = (kn * r * gk_ref[:, 0:128]).astype(BF16)
            k_ref[h, :, 128:256] = rope(kpe * r * gk_ref[:, 128:256]).astype(BF16)
            v_ref[h] = kvp[:, MLA_PAD * h + 128:MLA_PAD * (h + 1)].astype(BF16)

    fix = lambda i: (0, 0)
    return pl.pallas_call(
        body, name="mla_prep", grid=(T // tm,),
        in_specs=[pl.BlockSpec((tm, 256), lambda i: (i, 6)), pl.BlockSpec((tm, 128), lambda i: (i, 14)),
                  pl.BlockSpec((tm, 128), lambda i: (i, 15)),
                  pl.BlockSpec((tm, 128), lambda i: (i, 0)), pl.BlockSpec((tm, 128), lambda i: (i, 0)),
                  pl.BlockSpec((1, 256), fix), pl.BlockSpec((1, 128), fix),
                  pl.BlockSpec((1, 256), fix), pl.BlockSpec((1, 256), fix),
                  pl.BlockSpec((256, H * MLA_PAD), fix), pl.BlockSpec((128, H * MLA_PAD), fix)],
        out_specs=[pl.BlockSpec((H, tm, MLA_PAD), lambda i: (0, i, 0)), pl.BlockSpec((H, tm, MLA_PAD), lambda i: (0, i, 0)),
                   pl.BlockSpec((H, tm, MLA_V), lambda i: (0, i, 0))],
        out_shape=[jax.ShapeDtypeStruct((H, T, MLA_PAD), BF16), jax.ShapeDtypeStruct((H, T, MLA_PAD), BF16),
                   jax.ShapeDtypeStruct((H, T, MLA_V), BF16)],
        compiler_params=_params(1),
    )(proj, proj, proj, cos, sin, g_qa, g_kva, g_q, g_k, wq, wkv)


def _mla_prep_bwd(proj, cos, sin, g_qa, g_kva, g_q, g_k, wq, wkv, dq, dk, dv, tm=512):
    T = proj.shape[0]
    H = MLA_HEADS
    scale = MLA_QK ** -0.5

    def body(cq_ref, ckv_ref, kpe_ref, cos_ref, sin_ref, gqa_ref, gkva_ref, gq_ref, gk_ref, wq_ref, wkv_ref,
             dq_ref, dk_ref, dv_ref,
             dcq_ref, dckv_ref, dkpe_ref, cqn_ref, ckvn_ref, dqp_ref, dkvp_ref,
             dgqa_ref, dgkva_ref, dgq_ref, dgk_ref):
        @pl.when(pl.program_id(0) == 0)
        def _():
            for ref in (dgqa_ref, dgkva_ref, dgq_ref, dgk_ref):
                ref[...] = jnp.zeros_like(ref)

        cosv, sinv = cos_ref[...], sin_ref[...]

        def rope_bwd(dy):
            return dy * cosv + _swap_halves(dy * sinv)

        cq = cq_ref[...]
        rcq = _rstd(cq)
        cqn = (cq * rcq * gqa_ref[...]).astype(BF16)
        cqn_ref[...] = cqn
        qp = _dot(cqn, wq_ref[...])
        ckv = ckv_ref[...]
        rckv = _rstd(ckv)
        ckvn = (ckv * rckv * gkva_ref[...]).astype(BF16)
        ckvn_ref[...] = ckvn
        kvp = _dot(ckvn, wkv_ref[...])
        kpe = kpe_ref[...]
        dkpe = jnp.zeros_like(kpe)
        dgq = jnp.zeros((1, MLA_PAD), F32)
        dgk = jnp.zeros((1, MLA_PAD), F32)
        for h in range(H):
            a = qp[:, MLA_PAD * h:MLA_PAD * (h + 1)]
            dqh = dq_ref[h]
            dn = jnp.concatenate([dqh[:, 0:128], rope_bwd(dqh[:, 128:256])], axis=-1) * scale
            da, dg = _rms_bwd(dn, a, gq_ref[...], _rstd(a, MLA_QK), MLA_QK)
            dgq = dgq + jnp.sum(dg, axis=0, keepdims=True)
            dqp_ref[:, MLA_PAD * h:MLA_PAD * (h + 1)] = da.astype(BF16)

            ak = jnp.concatenate([kvp[:, MLA_PAD * h:MLA_PAD * h + 128], kpe], axis=-1)
            dkh = dk_ref[h]
            dnk = jnp.concatenate([dkh[:, 0:128], rope_bwd(dkh[:, 128:256])], axis=-1)
            dak, dg = _rms_bwd(dnk, ak, gk_ref[...], _rstd(ak, MLA_QK), MLA_QK)
            dgk = dgk + jnp.sum(dg, axis=0, keepdims=True)
            dkpe = dkpe + dak[:, 128:256]
            dkvp_ref[:, MLA_PAD * h:MLA_PAD * h + 128] = dak[:, 0:128].astype(BF16)
            dkvp_ref[:, MLA_PAD * h + 128:MLA_PAD * (h + 1)] = dv_ref[h].astype(BF16)
        dkpe_ref[...] = dkpe
        dgq_ref[...] += dgq
        dgk_ref[...] += dgk
        dcq, dg = _rms_bwd(_dot_nt(dqp_ref[...], wq_ref[...]), cq, gqa_ref[...], rcq)
        dcq_ref[...] = dcq
        dgqa_ref[...] += jnp.sum(dg, axis=0, keepdims=True)
        dckv, dg = _rms_bwd(_dot_nt(dkvp_ref[...], wkv_ref[...]), ckv, gkva_ref[...], rckv)
        dckv_ref[...] = dckv
        dgkva_ref[...] += jnp.sum(dg, axis=0, keepdims=True)

    fix = lambda i: (0, 0)
    row = lambda i: (i, 0)
    head = lambda i: (0, i, 0)
    return pl.pallas_call(
        body, name="mla_prep_bwd", grid=(T // tm,),
        in_specs=[pl.BlockSpec((tm, 256), lambda i: (i, 6)), pl.BlockSpec((tm, 128), lambda i: (i, 14)),
                  pl.BlockSpec((tm, 128), lambda i: (i, 15)),
                  pl.BlockSpec((tm, 128), row), pl.BlockSpec((tm, 128), row),
                  pl.BlockSpec((1, 256), fix), pl.BlockSpec((1, 128), fix),
                  pl.BlockSpec((1, 256), fix), pl.BlockSpec((1, 256), fix),
                  pl.BlockSpec((256, H * MLA_PAD), fix), pl.BlockSpec((128, H * MLA_PAD), fix),
                  pl.BlockSpec((H, tm, MLA_PAD), head), pl.BlockSpec((H, tm, MLA_PAD), head),
                  pl.BlockSpec((H, tm, MLA_V), head)],
        out_specs=[pl.BlockSpec((tm, 256), row), pl.BlockSpec((tm, 128), row), pl.BlockSpec((tm, 128), row),
                   pl.BlockSpec((tm, 256), row), pl.BlockSpec((tm, 128), row),
                   pl.BlockSpec((tm, H * MLA_PAD), row), pl.BlockSpec((tm, H * MLA_PAD), row),
                   pl.BlockSpec((1, 256), fix), pl.BlockSpec((1, 128), fix),
                   pl.BlockSpec((1, 256), fix), pl.BlockSpec((1, 256), fix)],
        out_shape=[jax.ShapeDtypeStruct((T, 256), F32), jax.ShapeDtypeStruct((T, 128), F32),
                   jax.ShapeDtypeStruct((T, 128), F32),
                   jax.ShapeDtypeStruct((T, 256), BF16), jax.ShapeDtypeStruct((T, 128), BF16),
                   jax.ShapeDtypeStruct((T, H * MLA_PAD), BF16), jax.ShapeDtypeStruct((T, H * MLA_PAD), BF16),
                   jax.ShapeDtypeStruct((1, 256), F32), jax.ShapeDtypeStruct((1, 128), F32),
                   jax.ShapeDtypeStruct((1, 256), F32), jax.ShapeDtypeStruct((1, 256), F32)],
        compiler_params=_params(1),
    )(proj, proj, proj, cos, sin, g_qa, g_kva, g_q, g_k, wq, wkv, dq, dk, dv)


def _causal_pairs(T, tq, tk, key_major):
    pairs = [(i, j) for i in range(T // tq) for j in range(T // tk) if j * tk <= i * tq + tq - 1]
    if key_major:
        pairs.sort(key=lambda p: (p[1], p[0]))
    outer = [p[1] if key_major else p[0] for p in pairs]
    first = [int(t == 0 or outer[t] != outer[t - 1]) for t in range(len(pairs))]
    last = [int(t == len(pairs) - 1 or outer[t] != outer[t + 1]) for t in range(len(pairs))]
    tab = lambda v: jnp.asarray(np.array(v, np.int32))
    return tab([p[0] for p in pairs]), tab([p[1] for p in pairs]), tab(first), tab(last)


def _causal_scores(qv, kv, qi, ki, row0, tq, tk, masked):
    s = _dot_nt(qv, kv)
    if masked:
        row = lax.broadcasted_iota(jnp.int32, s.shape, 0) + (qi * tq + row0)
        col = lax.broadcasted_iota(jnp.int32, s.shape, 1) + ki * tk
        s = jnp.where(col <= row, s, NEG)
    return s


def _mla_attn(q, k, v, tq=512, tk=1024, rc=256):
    H, T, _ = q.shape
    tables = _causal_pairs(T, tq, tk, key_major=False)

    def body(qt, kt, ft, lt, q_ref, k_ref, v_ref, o_ref, lse_ref, m_s, l_s, acc):
        t = pl.program_id(1)
        qi, ki = qt[t], kt[t]

        @pl.when(ft[t] == 1)
        def _():
            m_s[...] = jnp.full_like(m_s, NEG)
            l_s[...] = jnp.zeros_like(l_s)
            acc[...] = jnp.zeros_like(acc)

        def update(masked):
            kk, vv = k_ref[...], v_ref[...]
            for c in range(tq // rc):
                rows = pl.ds(c * rc, rc)
                s = _causal_scores(q_ref[rows, :], kk, qi, ki, c * rc, tq, tk, masked)
                m_old = m_s[rows, :]
                m_new = jnp.maximum(m_old, jnp.max(s, axis=-1, keepdims=True))
                alpha = jnp.exp(m_old - m_new)
                p = jnp.exp(s - m_new)
                l_s[rows, :] = alpha * l_s[rows, :] + jnp.sum(p, axis=-1, keepdims=True)
                acc[rows, :] = alpha * acc[rows, :] + _dot(p.astype(BF16), vv)
                m_s[rows, :] = m_new

        diagonal = (ki + 1) * tk - 1 > qi * tq

        @pl.when(diagonal)
        def _():
            update(True)

        @pl.when(jnp.logical_not(diagonal))
        def _():
            update(False)

        @pl.when(lt[t] == 1)
        def _():
            o_ref[...] = acc[...] / l_s[...]
            lse_ref[...] = jnp.broadcast_to(m_s[...] + jnp.log(l_s[...]), lse_ref.shape)

    qrow = lambda h, t, qt, kt, ft, lt: (h, qt[t], 0)
    krow = lambda h, t, qt, kt, ft, lt: (h, kt[t], 0)
    return pl.pallas_call(
        body, name="mla_attn",
        grid_spec=pltpu.PrefetchScalarGridSpec(
            num_scalar_prefetch=4, grid=(H, int(tables[0].shape[0])),
            in_specs=[pl.BlockSpec((None, tq, MLA_PAD), qrow), pl.BlockSpec((None, tk, MLA_PAD), krow),
                      pl.BlockSpec((None, tk, MLA_V), krow)],
            out_specs=[pl.BlockSpec((tq, MLA_V), lambda h, t, qt, kt, ft, lt: (qt[t], h)),
                       pl.BlockSpec((None, tq, 128), qrow)],
            scratch_shapes=[pltpu.VMEM((tq, 1), F32), pltpu.VMEM((tq, 1), F32), pltpu.VMEM((tq, MLA_V), F32)]),
        out_shape=[jax.ShapeDtypeStruct((T, H * MLA_V), F32), jax.ShapeDtypeStruct((H, T, 128), F32)],
        compiler_params=_params(2),
    )(*tables, q, k, v)


def _mla_attn_bwd(q, k, v, o, lse, do, tq=512, tk=1024, rc=512):
    H, T, _ = q.shape
    tables = _causal_pairs(T, tq, tk, key_major=True)

    def body(qt, kt, ft, lt, q_ref, k_ref, v_ref, o_ref, lse_ref, do_ref, dq_ref, dk_ref, dv_ref, dk_s, dv_s):
        t = pl.program_id(1)
        qi, ki = qt[t], kt[t]

        @pl.when(t == 0)
        def _():
            dq_ref[...] = jnp.zeros_like(dq_ref)

        @pl.when(ft[t] == 1)
        def _():
            dk_s[...] = jnp.zeros_like(dk_s)
            dv_s[...] = jnp.zeros_like(dv_s)

        def update(masked):
            kk, vv = k_ref[...], v_ref[...]
            for c in range(tq // rc):
                rows = pl.ds(c * rc, rc)
                qv, dov = q_ref[rows, :], do_ref[rows, :]
                delta = jnp.sum(dov * o_ref[rows, :], axis=-1, keepdims=True)
                lse_v = jnp.max(lse_ref[rows, :], axis=-1, keepdims=True)
                p = jnp.exp(_causal_scores(qv, kk, qi, ki, c * rc, tq, tk, masked) - lse_v)
                dob = dov.astype(BF16)
                dv_s[...] += _dot_tn(p.astype(BF16), dob)
                ds = (p * (_dot_nt(dob, vv) - delta)).astype(BF16)
                dk_s[...] += _dot_tn(ds, qv)
                out_rows = pl.ds(pl.multiple_of(qi * tq + c * rc, rc), rc)
                dq_ref[out_rows, :] += _dot(ds, kk)

        diagonal = (ki + 1) * tk - 1 > qi * tq

        @pl.when(diagonal)
        def _():
            update(True)

        @pl.when(jnp.logical_not(diagonal))
        def _():
            update(False)

        @pl.when(lt[t] == 1)
        def _():
            dk_ref[...] = dk_s[...]
            dv_ref[...] = dv_s[...]

    qrow = lambda h, t, qt, kt, ft, lt: (h, qt[t], 0)
    krow = lambda h, t, qt, kt, ft, lt: (h, kt[t], 0)
    qcol = lambda h, t, qt, kt, ft, lt: (qt[t], h)
    return pl.pallas_call(
        body, name="mla_attn_bwd",
        grid_spec=pltpu.PrefetchScalarGridSpec(
            num_scalar_prefetch=4, grid=(H, int(tables[0].shape[0])),
            in_specs=[pl.BlockSpec((None, tq, MLA_PAD), qrow), pl.BlockSpec((None, tk, MLA_PAD), krow),
                      pl.BlockSpec((None, tk, MLA_V), krow), pl.BlockSpec((tq, MLA_V), qcol),
                      pl.BlockSpec((None, tq, 128), qrow), pl.BlockSpec((tq, MLA_V), qcol)],
            out_specs=[pl.BlockSpec((None, T, MLA_PAD), lambda h, t, qt, kt, ft, lt: (h, 0, 0)),
                       pl.BlockSpec((None, tk, MLA_PAD), krow), pl.BlockSpec((None, tk, MLA_V), krow)],
            scratch_shapes=[pltpu.VMEM((tk, MLA_PAD), F32), pltpu.VMEM((tk, MLA_V), F32)]),
        out_shape=[jax.ShapeDtypeStruct((H, T, MLA_PAD), F32), jax.ShapeDtypeStruct((H, T, MLA_PAD), F32),
                   jax.ShapeDtypeStruct((H, T, MLA_V), F32)],
        compiler_params=_params(2),
    )(*tables, q, k, v, o, lse, do)


def _pair_gain(g):
    return jnp.tile(g.reshape(1, DIL_HD), (1, 2))


def _pad_gain(g):
    return jnp.pad(g.reshape(1, MLA_QK), ((0, 0), (0, MLA_PAD - MLA_QK)))


def _local_step(x, target, w, s):
    T = x.shape[0]
    gq, gk = _pair_gain(s["dil_q_norm"]), _pair_gain(s["dil_k_norm"])
    g_q, g_k = _pad_gain(s["mla_q_norm"]), _pad_gain(s["mla_k_norm"])
    cos, sin = _rope_tables(T)
    onehots = _bucket_onehots()
    tables = [t.reshape(DIL_HEADS, DIL_BLOCK, DIL_BLOCK) for t in _bias_tables(s["rel_bias"], onehots)]
    biases = list(zip(tables[0::2], tables[1::2]))

    x1, h1, gate1, up1 = _ffn_fwd(x, s["ffn1_norm"], w["ffn1_w_gate"], w["ffn1_w_up"], w["ffn1_w_down"])
    hm, proj = _in_proj(x1, s["mix_norm"], w["w_in"])
    dil = None
    for (_, d), (bp, bc) in zip(DIL_BRANCHES, biases):
        dil = _dil_fwd(proj, bp, bc, gq, gk, d, dil)
    o_dil, lse_dil = dil
    q, k, v = _mla_prep(proj, cos, sin, s["mla_q_a_norm"], s["mla_kv_a_norm"], g_q, g_k, w["mla_w_q_b"], w["mla_w_kv_b"])
    o_mla, lse_mla = _mla_attn(q, k, v)
    x2, oc = _out_proj(x1, o_dil, o_mla, s["out_norm_dil"], s["out_norm_mla"], w["w_out"])
    y, h2, gate2, up2 = _ffn_fwd(x2, s["ffn2_norm"], w["ffn2_w_gate"], w["ffn2_w_up"], w["ffn2_w_down"])
    dy, loss = _loss_grad(y, target)

    gw, gs = {}, {}

    def ffn_grads(name, dy_in, x_in, h, gate, up):
        dx, a, dg, du, dyh, dgain = _ffn_bwd(dy_in, x_in, s[name + "_norm"], gate, up,
                                             w[name + "_w_gate"], w[name + "_w_up"], w[name + "_w_down"])
        gs[name + "_norm"] = dgain
        gw[name + "_w_gate"] = _matmul_tn(h, dg, 1024, 1408)
        gw[name + "_w_up"] = _matmul_tn(h, du, 1024, 1408)
        gw[name + "_w_down"] = _matmul_tn(a, dyh, 1408, 1024)
        return dx

    dx2 = ffn_grads("ffn2", dy, x2, h2, gate2, up2)
    gw["w_out"] = _matmul_tn(oc, dx2, 1024, 1024)
    do_dil, do_mla, gs["out_norm_dil"], gs["out_norm_mla"] = _out_proj_bwd(
        dx2, o_dil, o_mla, s["out_norm_dil"], s["out_norm_mla"], w["w_out"])

    dq, dk, dv = _mla_attn_bwd(q, k, v, o_mla, lse_mla, do_mla)
    (dcq, dckv, dkpe, cqn, ckvn, dqp, dkvp, gs["mla_q_a_norm"], gs["mla_kv_a_norm"], dg_q, dg_k) = _mla_prep_bwd(
        proj, cos, sin, s["mla_q_a_norm"], s["mla_kv_a_norm"], g_q, g_k, w["mla_w_q_b"], w["mla_w_kv_b"], dq, dk, dv)
    gs["mla_q_norm"], gs["mla_k_norm"] = dg_q[:, :MLA_QK], dg_k[:, :MLA_QK]
    gw["mla_w_q_b"] = _matmul_tn(cqn, dqp, 256, 1024)
    gw["mla_w_kv_b"] = _matmul_tn(ckvn, dkvp, 128, 1024)

    dqkv, dbs, dgq, dgk = [], [], 0.0, 0.0
    for (_, d), (bp, bc) in zip(DIL_BRANCHES, biases):
        triple, dbp, dbc, dgq_b, dgk_b = _dil_bwd(proj, o_dil, lse_dil, do_dil, bp, bc, gq, gk, d)
        dqkv.append(triple)
        dbs += [dbp.reshape(DIL_HEADS, -1), dbc.reshape(DIL_HEADS, -1)]
        dgq, dgk = dgq + dgq_b, dgk + dgk_b
    gs["dil_q_norm"] = dgq[:, :DIL_HD] + dgq[:, DIL_HD:]
    gs["dil_k_norm"] = dgk[:, :DIL_HD] + dgk[:, DIL_HD:]
    gs["rel_bias"] = _bias_grad(dbs, onehots)

    dx1, dproj, gs["mix_norm"] = _in_proj_bwd(dx2, x1, s["mix_norm"], w["w_in"], dqkv, dcq, dckv, dkpe)
    gw["w_in"] = _matmul_tn(hm, dproj, 1024, 1024)
    grad_x = ffn_grads("ffn1", dx1, x, h1, gate1, up1)
    return loss, grad_x, gw, gs


def _position():
    x, y, c = lax.axis_index("x"), lax.axis_index("y"), lax.axis_index("c")
    return x, y, c, 4 * x + 2 * y + c


def _peer(x, y, c, k):
    px = 1 - x if k & 4 else x
    py = 1 - y if k & 2 else y
    pc = 1 - c if k & 1 else c
    return (px, py, pc), 4 * px + 2 * py + pc


def _exchange(arrays, scatter):
    n = len(arrays)
    out_shapes = [jax.ShapeDtypeStruct(a.shape if sc else (N_DEV,) + a.shape, a.dtype) for a, sc in zip(arrays, scatter)]

    def body(*refs):
        ins, outs = refs[:n], refs[n:2 * n]
        send_sems, recv_sems, local_sems = refs[2 * n:]
        x, y, c, me = _position()
        local = []
        for a in range(n):
            src = ins[a].at[me] if scatter[a] else ins[a]
            local.append(pltpu.make_async_copy(src, outs[a].at[me], local_sems.at[a]))
            local[-1].start()
        copies = []
        for k in range(1, N_DEV):
            peer, peer_idx = _peer(x, y, c, k)
            for a in range(n):
                src = ins[a].at[peer_idx] if scatter[a] else ins[a]
                copies.append(pltpu.make_async_remote_copy(
                    src_ref=src, dst_ref=outs[a].at[me], send_sem=send_sems.at[a, k - 1], recv_sem=recv_sems.at[a, k - 1],
                    device_id=peer, device_id_type=pl.DeviceIdType.MESH))
                copies[-1].start()
        for cp in copies:
            cp.wait()
        for cp in local:
            cp.wait()

    return pl.pallas_call(
        body, name="exchange_scatter" if any(scatter) else "exchange_gather",
        in_specs=[pl.BlockSpec(memory_space=pl.ANY)] * n, out_specs=[pl.BlockSpec(memory_space=pl.ANY)] * n,
        out_shape=out_shapes,
        scratch_shapes=[pltpu.SemaphoreType.DMA((n, N_DEV - 1)), pltpu.SemaphoreType.DMA((n, N_DEV - 1)),
                        pltpu.SemaphoreType.DMA((n,))],
    )(*arrays)


def _adamw_math(wv, g, m, v):
    m = ADAM_B1 * m + (1.0 - ADAM_B1) * g
    v = ADAM_B2 * v + (1.0 - ADAM_B2) * (g * g)
    m_hat = m / (1.0 - ADAM_B1 ** ADAM_STEP)
    v_hat = v / (1.0 - ADAM_B2 ** ADAM_STEP)
    delta = -ADAM_LR * (m_hat / (jnp.sqrt(v_hat) + ADAM_EPS) + ADAM_WD * wv)
    return delta, m, v


def _adamw(parts, slot, wv, m, v):
    R, C = wv.shape
    tr = max(t for t in range(16, 257, 16) if R % t == 0)

    def body(p_ref, w_ref, m_ref, v_ref, g_ref, d_ref, mo_ref, vo_ref):
        g = p_ref[0].astype(F32)
        for j in range(1, N_DEV):
            g = g + p_ref[j].astype(F32)
        d, mn, vn = _adamw_math(w_ref[...], g, m_ref[...], v_ref[...])
        g_ref[...] = g
        d_ref[...] = d
        mo_ref[...] = mn
        vo_ref[...] = vn

    row = lambda i: (i, 0)
    if slot is None:
        p_spec = pl.BlockSpec((N_DEV, tr, C), lambda i: (0, i, 0))
    else:
        p_spec = pl.BlockSpec((N_DEV, None, tr, C), lambda i: (0, slot, i, 0))
    out = jax.ShapeDtypeStruct((R, C), F32)
    return pl.pallas_call(
        body, name="adamw", grid=(R // tr,),
        in_specs=[p_spec, pl.BlockSpec((tr, C), row), pl.BlockSpec((tr, C), row), pl.BlockSpec((tr, C), row)],
        out_specs=[pl.BlockSpec((tr, C), row)] * 4, out_shape=[out] * 4,
        compiler_params=_params(1),
    )(parts, wv, m, v)


_COL_SHARDED = ("ffn1_w_gate", "ffn1_w_up", "ffn2_w_gate", "ffn2_w_up")
_ROW_SHARDED = ("ffn1_w_down", "ffn2_w_down")
_SMALL = ("ffn1_norm", "mix_norm", "ffn2_norm", "out_norm_dil", "out_norm_mla", "mla_q_a_norm", "rel_bias",
          "mla_q_norm", "mla_k_norm", "mla_kv_a_norm", "dil_q_norm", "dil_k_norm")
_SMALL_ROWS = 48


def _cols_to_full(g):
    return g.transpose(1, 0, 2).reshape(g.shape[1], N_DEV * g.shape[2])


def _full_to_cols(f):
    return f.reshape(f.shape[0], N_DEV, f.shape[1] // N_DEV).transpose(1, 0, 2)


def _pack_small(parts, extra):
    flat = jnp.concatenate([parts[n].reshape(-1) for n in _SMALL] + [extra.reshape(-1)])
    return jnp.pad(flat, (0, _SMALL_ROWS * 128 - flat.shape[0])).reshape(_SMALL_ROWS, 128)


def _unpack_small(packed, shapes):
    flat, out, off = packed.reshape(-1), {}, 0
    for n in _SMALL:
        size = math.prod(shapes[n])
        out[n] = flat[off:off + size].reshape(shapes[n])
        off += size
    return out, flat[off]


_NAMES = ("ffn1_norm", "ffn1_w_gate", "ffn1_w_up", "ffn1_w_down", "mix_norm", "w_in", "dil_q_norm", "dil_k_norm",
          "rel_bias", "mla_q_a_norm", "mla_w_q_b", "mla_kv_a_norm", "mla_w_kv_b", "mla_q_norm", "mla_k_norm",
          "out_norm_dil", "out_norm_mla", "w_out", "ffn2_norm", "ffn2_w_gate", "ffn2_w_up", "ffn2_w_down")


def kernel(x, ffn1_norm, ffn1_w_gate, ffn1_w_up, ffn1_w_down, mix_norm, w_in, dil_q_norm, dil_k_norm, rel_bias, mla_q_a_norm, mla_w_q_b, mla_kv_a_norm, mla_w_kv_b, mla_q_norm, mla_k_norm, out_norm_dil, out_norm_mla, w_out, ffn2_norm, ffn2_w_gate, ffn2_w_up, ffn2_w_down, loss_target, m_ffn1_norm, m_ffn1_w_gate, m_ffn1_w_up, m_ffn1_w_down, m_mix_norm, m_w_in, m_dil_q_norm, m_dil_k_norm, m_rel_bias, m_mla_q_a_norm, m_mla_w_q_b, m_mla_kv_a_norm, m_mla_w_kv_b, m_mla_q_norm, m_mla_k_norm, m_out_norm_dil, m_out_norm_mla, m_w_out, m_ffn2_norm, m_ffn2_w_gate, m_ffn2_w_up, m_ffn2_w_down, v_ffn1_norm, v_ffn1_w_gate, v_ffn1_w_up, v_ffn1_w_down, v_mix_norm, v_w_in, v_dil_q_norm, v_dil_k_norm, v_rel_bias, v_mla_q_a_norm, v_mla_w_q_b, v_mla_kv_a_norm, v_mla_w_kv_b, v_mla_q_norm, v_mla_k_norm, v_out_norm_dil, v_out_norm_mla, v_w_out, v_ffn2_norm, v_ffn2_w_gate, v_ffn2_w_up, v_ffn2_w_down):
    args = locals()
    wts = {n: args[n] for n in _NAMES}
    mom = {n: args["m_" + n] for n in _NAMES}
    var = {n: args["v_" + n] for n in _NAMES}

    shard = lambda n: wts[n][0].astype(BF16)
    cols = jnp.stack([shard(n) for n in _COL_SHARDED])
    rows = jnp.stack([shard(n) for n in _ROW_SHARDED])
    g_cols, g_rows, g_in, g_qb, g_kvb, g_out = _exchange(
        [cols, rows, shard("w_in"), shard("mla_w_q_b"), shard("mla_w_kv_b"), shard("w_out")], [False] * 6)
    w = {n: _cols_to_full(g_cols[:, i]) for i, n in enumerate(_COL_SHARDED)}
    w.update({n: g_rows[:, i].reshape(-1, g_rows.shape[-1]) for i, n in enumerate(_ROW_SHARDED)})
    w["w_in"] = jnp.pad(_cols_to_full(g_in), ((0, 0), (0, PROJ_PAD - PROJ_COLS)))
    wq = _cols_to_full(g_qb).reshape(-1, MLA_HEADS, MLA_QK)
    w["mla_w_q_b"] = jnp.pad(wq, ((0, 0), (0, 0), (0, MLA_PAD - MLA_QK))).reshape(-1, MLA_HEADS * MLA_PAD)
    w["mla_w_kv_b"] = _cols_to_full(g_kvb)
    w["w_out"] = g_out.reshape(-1, g_out.shape[-1])
    small = {n: wts[n].reshape(1, -1) if n != "rel_bias" else wts[n] for n in _SMALL}

    loss, grad_x, gw, gs = _local_step(x[0], loss_target[0], w, small)

    part = lambda f: f.astype(BF16)
    p_cols = jnp.stack([part(_full_to_cols(gw[n])) for n in _COL_SHARDED], axis=1)
    p_rows = jnp.stack([part(gw[n].reshape(N_DEV, -1, gw[n].shape[-1])) for n in _ROW_SHARDED], axis=1)
    p_in = part(_full_to_cols(gw["w_in"][:, :PROJ_COLS]))
    gq_full = gw["mla_w_q_b"].reshape(-1, MLA_HEADS, MLA_PAD)[:, :, :MLA_QK].reshape(-1, MLA_HEADS * MLA_QK)
    p_qb = part(_full_to_cols(gq_full))
    p_kvb = part(_full_to_cols(gw["mla_w_kv_b"]))
    p_out = part(gw["w_out"].reshape(N_DEV, -1, gw["w_out"].shape[-1]))
    p_small = _pack_small(gs, loss[0, 0])
    r_cols, r_rows, r_in, r_qb, r_kvb, r_out, r_small = _exchange(
        [p_cols, p_rows, p_in, p_qb, p_kvb, p_out, p_small], [True] * 6 + [False])

    res = {}
    for i, n in enumerate(_COL_SHARDED):
        res[n] = _adamw(r_cols, i, wts[n][0], mom[n][0], var[n][0])
    for i, n in enumerate(_ROW_SHARDED):
        res[n] = _adamw(r_rows, i, wts[n][0], mom[n][0], var[n][0])
    for n, r in (("w_in", r_in), ("mla_w_q_b", r_qb), ("mla_w_kv_b", r_kvb), ("w_out", r_out)):
        res[n] = _adamw(r, None, wts[n][0], mom[n][0], var[n][0])
    shapes = {n: wts[n].shape for n in _SMALL}
    zero = jnp.zeros((), F32)
    packed = _adamw(r_small, None, _pack_small(wts, zero), _pack_small(mom, zero), _pack_small(var, zero))
    loss_total = None
    for slot, q in enumerate(packed):
        vals, extra = _unpack_small(q, shapes)
        if slot == 0:
            loss_total = extra
        for n in _SMALL:
            res.setdefault(n, [None] * 4)[slot] = vals[n]
    outs = [loss_total, grad_x[None]]
    for slot in range(4):
        outs += [res[n][slot].reshape(wts[n].shape) for n in _NAMES]
    return tuple(outs)
```

```python
import math

import numpy as np
import jax
import jax.numpy as jnp
from jax import lax
from jax.experimental import pallas as pl
from jax.experimental.pallas import tpu as pltpu

F32, BF16 = jnp.float32, jnp.bfloat16
EPS = 1e-6
NEG = -1e30
N_DEV = 8

DIL_HEADS, DIL_HD = 8, 64
DIL_WIDTH = DIL_HEADS * DIL_HD
DIL_BRANCHES = ((128, 1), (512, 4), (2048, 16))
DIL_BLOCK = 128
MLA_HEADS, MLA_NOPE, MLA_ROPE, MLA_V = 4, 128, 64, 128
MLA_QK = MLA_NOPE + MLA_ROPE
MLA_PAD = 256
ROPE_BASE = 10000.0
REL_BUCKETS, REL_MAX_DIST = 32, 2048
PROJ_COLS, PROJ_PAD = 1984, 2048
FFN_RESID = 0.5
ADAM_LR, ADAM_B1, ADAM_B2, ADAM_EPS, ADAM_WD, ADAM_STEP = 0.001, 0.9, 0.999, 1e-08, 0.01, 10
VMEM_LIMIT = 56 * 1024 * 1024

_NT = (((1,), (1,)), ((), ()))
_TN = (((0,), (0,)), ((), ()))


def _dot(a, b):
    return jnp.dot(a, b, preferred_element_type=F32)


def _dot_nt(a, b):
    return lax.dot_general(a, b, _NT, preferred_element_type=F32)


def _dot_tn(a, b):
    return lax.dot_general(a, b, _TN, preferred_element_type=F32)


def _params(n_axes):
    return pltpu.CompilerParams(dimension_semantics=("arbitrary",) * n_axes, vmem_limit_bytes=VMEM_LIMIT)


def _rstd(x, n=None):
    n = x.shape[-1] if n is None else n
    return lax.rsqrt(jnp.sum(x * x, axis=-1, keepdims=True) / n + EPS)


def _rms_bwd(dy, x, g, r, n=None):
    n = x.shape[-1] if n is None else n
    u = dy * g
    dx = r * u - x * (r * r * r) * (jnp.sum(u * x, axis=-1, keepdims=True) / n)
    return dx, dy * x * r


def _sigmoid(x):
    return 1.0 / (1.0 + jnp.exp(-x))


def _split3(x):
    parts = []
    for _ in range(3):
        xb = x.astype(BF16)
        parts.append(xb)
        x = x - xb.astype(F32)
    return parts


def _ffn_fwd(x, gain, wg, wu, wd, ride=None, tm=1024, tf=256):
    T, D = x.shape
    F = wg.shape[1]
    ni, nj = T // tm, F // tf
    r_args, r_in, r_shape, r_out, r_scratch = _ride_parts(ride)

    def body(x_ref, g_ref, wg_ref, wu_ref, wd_ref, xo_ref, h_ref, gate_ref, up_ref, acc):
        j = pl.program_id(1)

        @pl.when(j == 0)
        def _():
            xv = x_ref[...]
            h_ref[...] = (xv * _rstd(xv) * g_ref[...]).astype(BF16)
            acc[...] = jnp.zeros_like(acc)

        h = h_ref[...]
        g = _dot(h, wg_ref[...])
        u = _dot(h, wu_ref[...])
        gate_ref[...] = g.astype(BF16)
        up_ref[...] = u.astype(BF16)
        a = (g * _sigmoid(g) * u).astype(BF16)
        acc[...] += _dot(a, wd_ref[...])

        @pl.when(j == nj - 1)
        def _():
            xo_ref[...] = x_ref[...] + FFN_RESID * acc[...]

    first = lambda: (pl.program_id(0) == 0) & (pl.program_id(1) == 0)
    last = lambda: (pl.program_id(0) == ni - 1) & (pl.program_id(1) == nj - 1)
    outs = pl.pallas_call(
        _riding(body, 5, 4, 1, ride, first, last), name="ffn_fwd", grid=(ni, nj),
        in_specs=[pl.BlockSpec((tm, D), lambda i, j: (i, 0)), pl.BlockSpec((1, D), lambda i, j: (0, 0)),
                  pl.BlockSpec((D, tf), lambda i, j: (0, j)), pl.BlockSpec((D, tf), lambda i, j: (0, j)),
                  pl.BlockSpec((tf, D), lambda i, j: (j, 0))] + r_in,
        out_specs=[pl.BlockSpec((tm, D), lambda i, j: (i, 0)), pl.BlockSpec((tm, D), lambda i, j: (i, 0)),
                   pl.BlockSpec((tm, tf), lambda i, j: (i, j)), pl.BlockSpec((tm, tf), lambda i, j: (i, j))] + r_out,
        out_shape=[jax.ShapeDtypeStruct((T, D), F32), jax.ShapeDtypeStruct((T, D), BF16),
                   jax.ShapeDtypeStruct((T, F), BF16), jax.ShapeDtypeStruct((T, F), BF16)] + r_shape,
        scratch_shapes=[pltpu.VMEM((tm, D), F32)] + r_scratch,
        compiler_params=_params(2),
    )(x, gain, wg, wu, wd, *r_args)
    return outs[:4], outs[4:]


def _ffn_bwd(dy, x, gain, gate, up, wg, wu, wd, ride=None, tm=1024, tf=256):
    T, D = x.shape
    F = wg.shape[1]
    ni, nj = T // tm, F // tf
    r_args, r_in, r_shape, r_out, r_scratch = _ride_parts(ride)

    def body(dy_ref, x_ref, g_ref, gate_ref, up_ref, wg_ref, wu_ref, wd_ref,
             dx_ref, a_ref, dg_ref, du_ref, dyh_ref, dgain_ref, acc):
        i, j = pl.program_id(0), pl.program_id(1)

        @pl.when((i == 0) & (j == 0))
        def _():
            dgain_ref[...] = jnp.zeros_like(dgain_ref)

        @pl.when(j == 0)
        def _():
            dyh_ref[...] = (FFN_RESID * dy_ref[...]).astype(BF16)
            acc[...] = jnp.zeros_like(acc)

        da = _dot_nt(dyh_ref[...], wd_ref[...])
        g = gate_ref[...].astype(F32)
        u = up_ref[...].astype(F32)
        sig = _sigmoid(g)
        s = g * sig
        a_ref[...] = (s * u).astype(BF16)
        dg = (da * u * (sig * (1.0 + g * (1.0 - sig)))).astype(BF16)
        du = (da * s).astype(BF16)
        dg_ref[...] = dg
        du_ref[...] = du
        acc[...] += _dot_nt(dg, wg_ref[...]) + _dot_nt(du, wu_ref[...])

        @pl.when(j == nj - 1)
        def _():
            xv = x_ref[...]
            dxn, dgc = _rms_bwd(acc[...], xv, g_ref[...], _rstd(xv))
            dx_ref[...] = dy_ref[...] + dxn
            dgain_ref[...] += jnp.sum(dgc, axis=0, keepdims=True)

    first = lambda: (pl.program_id(0) == 0) & (pl.program_id(1) == 0)
    last = lambda: (pl.program_id(0) == ni - 1) & (pl.program_id(1) == nj - 1)
    outs = pl.pallas_call(
        _riding(body, 8, 6, 1, ride, first, last), name="ffn_bwd", grid=(ni, nj),
        in_specs=[pl.BlockSpec((tm, D), lambda i, j: (i, 0)), pl.BlockSpec((tm, D), lambda i, j: (i, 0)),
                  pl.BlockSpec((1, D), lambda i, j: (0, 0)),
                  pl.BlockSpec((tm, tf), lambda i, j: (i, j)), pl.BlockSpec((tm, tf), lambda i, j: (i, j)),
                  pl.BlockSpec((D, tf), lambda i, j: (0, j)), pl.BlockSpec((D, tf), lambda i, j: (0, j)),
                  pl.BlockSpec((tf, D), lambda i, j: (j, 0))] + r_in,
        out_specs=[pl.BlockSpec((tm, D), lambda i, j: (i, 0)),
                   pl.BlockSpec((tm, tf), lambda i, j: (i, j)), pl.BlockSpec((tm, tf), lambda i, j: (i, j)),
                   pl.BlockSpec((tm, tf), lambda i, j: (i, j)),
                   pl.BlockSpec((tm, D), lambda i, j: (i, 0)), pl.BlockSpec((1, D), lambda i, j: (0, 0))] + r_out,
        out_shape=[jax.ShapeDtypeStruct((T, D), F32), jax.ShapeDtypeStruct((T, F), BF16),
                   jax.ShapeDtypeStruct((T, F), BF16), jax.ShapeDtypeStruct((T, F), BF16),
                   jax.ShapeDtypeStruct((T, D), BF16), jax.ShapeDtypeStruct((1, D), F32)] + r_shape,
        scratch_shapes=[pltpu.VMEM((tm, D), F32)] + r_scratch,
        compiler_params=_params(2),
    )(dy, x, gain, gate, up, wg, wu, wd, *r_args)
    return outs[:6], outs[6:]


def _matmul_tn(a, b, tk, tn, tt=512):
    T, K = a.shape
    N = b.shape[1]
    tk, tn = min(tk, K), min(tn, N)

    def body(a_ref, b_ref, o_ref):
        @pl.when(pl.program_id(2) == 0)
        def _():
            o_ref[...] = jnp.zeros_like(o_ref)

        o_ref[...] += _dot_tn(a_ref[...].astype(BF16), b_ref[...].astype(BF16))

    return pl.pallas_call(
        body, name="matmul_tn", grid=(K // tk, N // tn, T // tt),
        in_specs=[pl.BlockSpec((tt, tk), lambda k, n, t: (t, k)), pl.BlockSpec((tt, tn), lambda k, n, t: (t, n))],
        out_specs=pl.BlockSpec((tk, tn), lambda k, n, t: (k, n)),
        out_shape=jax.ShapeDtypeStruct((K, N), F32),
        compiler_params=_params(3),
    )(a, b)


def _loss_grad(y, target, tm=512):
    T, D = y.shape

    def body(y_ref, t_ref, dy_ref, loss_ref):
        @pl.when(pl.program_id(0) == 0)
        def _():
            loss_ref[...] = jnp.zeros_like(loss_ref)

        e = y_ref[...] - t_ref[...]
        dy_ref[...] = e * (1.0 / D)
        loss_ref[...] += (0.5 / D) * jnp.sum(e * e)

    return pl.pallas_call(
        body, name="loss_grad", grid=(T // tm,),
        in_specs=[pl.BlockSpec((tm, D), lambda i: (i, 0)), pl.BlockSpec((tm, D), lambda i: (i, 0))],
        out_specs=[pl.BlockSpec((tm, D), lambda i: (i, 0)), pl.BlockSpec((1, 128), lambda i: (0, 0))],
        out_shape=[jax.ShapeDtypeStruct((T, D), F32), jax.ShapeDtypeStruct((1, 128), F32)],
        compiler_params=_params(1),
    )(y, target)


def _in_proj(x, gain, w, tm=512):
    T, D = x.shape
    N = w.shape[1]

    def body(x_ref, g_ref, w_ref, h_ref, p_ref):
        xv = x_ref[...]
        h = (xv * _rstd(xv) * g_ref[...]).astype(BF16)
        h_ref[...] = h
        p_ref[...] = _dot(h, w_ref[...])

    return pl.pallas_call(
        body, name="in_proj", grid=(T // tm,),
        in_specs=[pl.BlockSpec((tm, D), lambda i: (i, 0)), pl.BlockSpec((1, D), lambda i: (0, 0)),
                  pl.BlockSpec((D, N), lambda i: (0, 0))],
        out_specs=[pl.BlockSpec((tm, D), lambda i: (i, 0)), pl.BlockSpec((tm, N), lambda i: (i, 0))],
        out_shape=[jax.ShapeDtypeStruct((T, D), BF16), jax.ShapeDtypeStruct((T, N), F32)],
        compiler_params=_params(1),
    )(x, gain, w)


def _in_proj_bwd(dx_up, x, gain, w, dqkv, dcq, dckv, dkpe, tm=512):
    T, D = x.shape
    N = w.shape[1]
    W = DIL_WIDTH
    nb = len(dqkv)

    def body(*refs):
        dxu_ref, x_ref, g_ref, w_ref = refs[:4]
        dil_refs = refs[4:4 + 3 * nb]
        dcq_ref, dckv_ref, dkpe_ref, dx_ref, dp_ref, dgain_ref = refs[4 + 3 * nb:]

        @pl.when(pl.program_id(0) == 0)
        def _():
            dgain_ref[...] = jnp.zeros_like(dgain_ref)

        for part in range(3):
            acc = dil_refs[part][...]
            for b in range(1, nb):
                acc = acc + dil_refs[3 * b + part][...]
            dp_ref[:, part * W:(part + 1) * W] = acc.astype(BF16)
        dp_ref[:, 3 * W:3 * W + 256] = dcq_ref[...].astype(BF16)
        dp_ref[:, 3 * W + 256:3 * W + 384] = dckv_ref[...].astype(BF16)
        dp_ref[:, 3 * W + 384:N] = dkpe_ref[...].astype(BF16)
        dh = _dot_nt(dp_ref[...], w_ref[...])
        xv = x_ref[...]
        dxn, dgc = _rms_bwd(dh, xv, g_ref[...], _rstd(xv))
        dx_ref[...] = dxu_ref[...] + dxn
        dgain_ref[...] += jnp.sum(dgc, axis=0, keepdims=True)

    row = lambda i: (i, 0)
    return pl.pallas_call(
        body, name="in_proj_bwd", grid=(T // tm,),
        in_specs=[pl.BlockSpec((tm, D), row), pl.BlockSpec((tm, D), row), pl.BlockSpec((1, D), lambda i: (0, 0)),
                  pl.BlockSpec((D, N), lambda i: (0, 0))] + [pl.BlockSpec((tm, W), row)] * (3 * nb)
                 + [pl.BlockSpec((tm, 256), row), pl.BlockSpec((tm, 128), row), pl.BlockSpec((tm, 128), row)],
        out_specs=[pl.BlockSpec((tm, D), row), pl.BlockSpec((tm, N), row), pl.BlockSpec((1, D), lambda i: (0, 0))],
        out_shape=[jax.ShapeDtypeStruct((T, D), F32), jax.ShapeDtypeStruct((T, N), BF16),
                   jax.ShapeDtypeStruct((1, D), F32)],
        compiler_params=_params(1),
    )(dx_up, x, gain, w, *[a for triple in dqkv for a in triple], dcq, dckv, dkpe)


def _out_proj(x, o_dil, o_mla, g_dil, g_mla, w, tm=512):
    T, D = x.shape
    W = o_dil.shape[1]

    def body(x_ref, od_ref, om_ref, gd_ref, gm_ref, w_ref, xo_ref, oc_ref):
        od, om = od_ref[...], om_ref[...]
        oc_ref[:, 0:W] = (od * _rstd(od) * gd_ref[...]).astype(BF16)
        oc_ref[:, W:2 * W] = (om * _rstd(om) * gm_ref[...]).astype(BF16)
        xo_ref[...] = x_ref[...] + _dot(oc_ref[...], w_ref[...])

    row = lambda i: (i, 0)
    fix = lambda i: (0, 0)
    return pl.pallas_call(
        body, name="out_proj", grid=(T // tm,),
        in_specs=[pl.BlockSpec((tm, D), row), pl.BlockSpec((tm, W), row), pl.BlockSpec((tm, W), row),
                  pl.BlockSpec((1, W), fix), pl.BlockSpec((1, W), fix), pl.BlockSpec((2 * W, D), fix)],
        out_specs=[pl.BlockSpec((tm, D), row), pl.BlockSpec((tm, 2 * W), row)],
        out_shape=[jax.ShapeDtypeStruct((T, D), F32), jax.ShapeDtypeStruct((T, 2 * W), BF16)],
        compiler_params=_params(1),
    )(x, o_dil, o_mla, g_dil, g_mla, w)


def _out_proj_bwd(dx, o_dil, o_mla, g_dil, g_mla, w, tm=512):
    T, D = dx.shape
    W = o_dil.shape[1]

    def body(dx_ref, od_ref, om_ref, gd_ref, gm_ref, w_ref, dod_ref, dom_ref, dgd_ref, dgm_ref):
        @pl.when(pl.program_id(0) == 0)
        def _():
            dgd_ref[...] = jnp.zeros_like(dgd_ref)
            dgm_ref[...] = jnp.zeros_like(dgm_ref)

        doc = _dot_nt(dx_ref[...].astype(BF16), w_ref[...])
        od, om = od_ref[...], om_ref[...]
        dod, dgd = _rms_bwd(doc[:, 0:W], od, gd_ref[...], _rstd(od))
        dom, dgm = _rms_bwd(doc[:, W:2 * W], om, gm_ref[...], _rstd(om))
        dod_ref[...] = dod
        dom_ref[...] = dom
        dgd_ref[...] += jnp.sum(dgd, axis=0, keepdims=True)
        dgm_ref[...] += jnp.sum(dgm, axis=0, keepdims=True)

    row = lambda i: (i, 0)
    fix = lambda i: (0, 0)
    return pl.pallas_call(
        body, name="out_proj_bwd", grid=(T // tm,),
        in_specs=[pl.BlockSpec((tm, D), row), pl.BlockSpec((tm, W), row), pl.BlockSpec((tm, W), row),
                  pl.BlockSpec((1, W), fix), pl.BlockSpec((1, W), fix), pl.BlockSpec((2 * W, D), fix)],
        out_specs=[pl.BlockSpec((tm, W), row), pl.BlockSpec((tm, W), row),
                   pl.BlockSpec((1, W), fix), pl.BlockSpec((1, W), fix)],
        out_shape=[jax.ShapeDtypeStruct((T, W), F32), jax.ShapeDtypeStruct((T, W), F32),
                   jax.ShapeDtypeStruct((1, W), F32), jax.ShapeDtypeStruct((1, W), F32)],
        compiler_params=_params(1),
    )(dx, o_dil, o_mla, g_dil, g_mla, w)


def _pair_rstd(x, lo):
    sq = x * x
    s0 = jnp.sum(jnp.where(lo, sq, 0.0), axis=-1, keepdims=True)
    s1 = jnp.sum(jnp.where(lo, 0.0, sq), axis=-1, keepdims=True)
    return jnp.where(lo, lax.rsqrt(s0 / DIL_HD + EPS), lax.rsqrt(s1 / DIL_HD + EPS))


def _pair_rms_bwd(dn, x, r, g, lo):
    u = dn * g
    t = u * x
    d0 = jnp.sum(jnp.where(lo, t, 0.0), axis=-1, keepdims=True)
    d1 = jnp.sum(jnp.where(lo, 0.0, t), axis=-1, keepdims=True)
    dx = r * u - x * (r * r * r) * (jnp.where(lo, d0, d1) / DIL_HD)
    return dx, jnp.sum(dn * x * r, axis=0, keepdims=True)


def _pair_col(x, lo, e):
    sel = lo if e == 0 else jnp.logical_not(lo)
    return jnp.max(jnp.where(sel, x, NEG), axis=-1, keepdims=True)


def _dil_masks(n):
    row = lax.broadcasted_iota(jnp.int32, (DIL_BLOCK, DIL_BLOCK), 0)
    col = lax.broadcasted_iota(jnp.int32, (DIL_BLOCK, DIL_BLOCK), 1)
    return col < DIL_HD, jnp.logical_and(col >= row, n > 0), col <= row


def _dil_pairs(d):
    return 4 if d == 1 else 1


def _sub_rows(r, d):
    return pl.ds(r, DIL_BLOCK, stride=d) if d > 1 else pl.ds(0, DIL_BLOCK)


def _split_subsequences(pairs, d, P):
    for r in range(d):
        for p in range(P):
            for block, scratch in pairs:
                scratch[r * P + p] = block[_sub_rows(r, d), pl.ds(128 * p, 128)]


def _merge_subsequences(pairs, d, P):
    for r in range(d):
        for p in range(P):
            for block, scratch in pairs:
                block[_sub_rows(r, d), pl.ds(128 * p, 128)] = scratch[r * P + p]


def _dil_fwd(proj, bias_p, bias_c, gq, gk, d, prev):
    T = proj.shape[0]
    P = _dil_pairs(d)
    rows, cw, n_it = DIL_BLOCK * d, 128 * P, d * P
    nblk = T // rows
    has_prev = prev is not None
    scale = DIL_HD ** -0.5

    def body(*refs):
        q_ref, kp_ref, kc_ref, vp_ref, vc_ref, bp_ref, bc_ref, gq_ref, gk_ref = refs[:9]
        refs = refs[9:]
        if has_prev:
            oin_ref, lin_ref = refs[:2]
            refs = refs[2:]
        o_ref, l_ref, qs, kps, kcs, vps, vcs, os_, ls_ = refs[:9]
        pb, n = pl.program_id(0), pl.program_id(1)
        lo, mask_p, mask_c = _dil_masks(n)
        gqv, gkv = gq_ref[...], gk_ref[...]
        loads = [(q_ref, qs), (kp_ref, kps), (kc_ref, kcs), (vp_ref, vps), (vc_ref, vcs)]
        if has_prev:
            ois, lis = refs[9:]
            loads += [(oin_ref, ois), (lin_ref, lis)]
        _split_subsequences(loads, d, P)

        def step(i, carry):
            h0 = 2 * (pb * P + i % P)
            q = qs[i]
            qn = q * _pair_rstd(q, lo) * gqv
            kp, kc = kps[i], kcs[i]
            kpn = (kp * _pair_rstd(kp, lo) * gkv).astype(BF16)
            kcn = (kc * _pair_rstd(kc, lo) * gkv).astype(BF16)
            vp, vc = vps[i].astype(BF16), vcs[i].astype(BF16)
            o_e, l_e = [], []
            for e in range(2):
                sel = lo if e == 0 else jnp.logical_not(lo)
                qe = jnp.where(sel, qn, 0.0).astype(BF16)
                sp = jnp.where(mask_p, _dot_nt(qe, kpn) * scale + bp_ref[h0 + e], NEG)
                sc = jnp.where(mask_c, _dot_nt(qe, kcn) * scale + bc_ref[h0 + e], NEG)
                m = jnp.maximum(jnp.max(sp, axis=-1, keepdims=True), jnp.max(sc, axis=-1, keepdims=True))
                pp, pc = jnp.exp(sp - m), jnp.exp(sc - m)
                l = jnp.sum(pp, axis=-1, keepdims=True) + jnp.sum(pc, axis=-1, keepdims=True)
                o_e.append((_dot(pp.astype(BF16), vp) + _dot(pc.astype(BF16), vc)) / l)
                l_e.append(m + jnp.log(l))
            o = jnp.where(lo, o_e[0], o_e[1])
            lse = jnp.where(lo, l_e[0], l_e[1])
            if has_prev:
                lin = lis[i]
                mx = jnp.maximum(lin, lse)
                lnew = mx + jnp.log(jnp.exp(lin - mx) + jnp.exp(lse - mx))
                o = ois[i] * jnp.exp(lin - lnew) + o * jnp.exp(lse - lnew)
                lse = lnew
            os_[i] = o
            ls_[i] = lse
            return carry

        lax.fori_loop(0, n_it, step, 0, unroll=4)
        _merge_subsequences([(o_ref, os_), (l_ref, ls_)], d, P)

    blk = (rows, cw)
    kcol, vcol = DIL_WIDTH // cw, 2 * DIL_WIDTH // cw
    prev_n = lambda n: jnp.maximum(n - 1, 0)
    fix3 = lambda pb, n: (0, 0, 0)
    fix2 = lambda pb, n: (0, 0)
    tok = pl.BlockSpec(blk, lambda pb, n: (n, pb))
    bias_spec = pl.BlockSpec((DIL_HEADS, DIL_BLOCK, DIL_BLOCK), fix3)
    in_specs = [tok,
                pl.BlockSpec(blk, lambda pb, n: (prev_n(n), kcol + pb)), pl.BlockSpec(blk, lambda pb, n: (n, kcol + pb)),
                pl.BlockSpec(blk, lambda pb, n: (prev_n(n), vcol + pb)), pl.BlockSpec(blk, lambda pb, n: (n, vcol + pb)),
                bias_spec, bias_spec, pl.BlockSpec((1, 128), fix2), pl.BlockSpec((1, 128), fix2)]
    args = [proj, proj, proj, proj, proj, bias_p, bias_c, gq, gk]
    n_scratch = 7
    if has_prev:
        in_specs += [tok, tok]
        args += list(prev)
        n_scratch += 2
    out = jax.ShapeDtypeStruct((T, DIL_WIDTH), F32)
    return pl.pallas_call(
        body, name=f"dil_fwd_d{d}", grid=(DIL_HEADS // 2 // P, nblk), in_specs=in_specs, out_specs=[tok, tok],
        out_shape=[out, out],
        scratch_shapes=[pltpu.VMEM((n_it, DIL_BLOCK, 128), F32)] * n_scratch,
        compiler_params=_params(2),
    )(*args)


def _dil_bwd(proj, o, lse, do, bias_p, bias_c, gq, gk, d):
    T = proj.shape[0]
    P = _dil_pairs(d)
    rows, cw, n_it = DIL_BLOCK * d, 128 * P, d * P
    nblk = T // rows
    scale = DIL_HD ** -0.5

    def body(q_ref, kp_ref, kc_ref, vp_ref, vc_ref, o_ref, l_ref, do_ref, bp_ref, bc_ref, gq_ref, gk_ref,
             dq_ref, dk_ref, dv_ref, dbp_ref, dbc_ref, dgq_ref, dgk_ref,
             qs, kps, kcs, vps, vcs, os_, ls_, dos, dqs, dks, dvs, ck, cv):
        pb, n = pl.program_id(0), pl.program_id(1)
        lo, mask_p, mask_c = _dil_masks(n)
        gqv, gkv = gq_ref[...], gk_ref[...]

        @pl.when((pb == 0) & (n == 0))
        def _():
            dbp_ref[...] = jnp.zeros_like(dbp_ref)
            dbc_ref[...] = jnp.zeros_like(dbc_ref)
            dgq_ref[...] = jnp.zeros_like(dgq_ref)
            dgk_ref[...] = jnp.zeros_like(dgk_ref)

        @pl.when(n == 0)
        def _():
            ck[...] = jnp.zeros_like(ck)
            cv[...] = jnp.zeros_like(cv)

        _split_subsequences([(q_ref, qs), (kp_ref, kps), (kc_ref, kcs), (vp_ref, vps), (vc_ref, vcs),
                             (o_ref, os_), (l_ref, ls_), (do_ref, dos)], d, P)

        def finish_prev(i, kp, rkp, dkn_p, dv_p):
            dk, dgk = _pair_rms_bwd(ck[i] + dkn_p, kp, rkp, gkv, lo)
            dks[i] = dk
            dvs[i] = cv[i] + dv_p
            dgk_ref[...] += dgk

        def step(i, carry):
            h0 = 2 * (pb * P + i % P)
            q = qs[i]
            rq = _pair_rstd(q, lo)
            qn = q * rq * gqv
            qb = qn.astype(BF16)
            kp, kc = kps[i], kcs[i]
            rkp = _pair_rstd(kp, lo)
            kpn = (kp * rkp * gkv).astype(BF16)
            kcn = (kc * _pair_rstd(kc, lo) * gkv).astype(BF16)
            vp, vc = vps[i].astype(BF16), vcs[i].astype(BF16)
            dov = dos[i]
            dob = dov.astype(BF16)
            dot_o = dov * os_[i]
            lse_pair = ls_[i]
            res = []
            for e in range(2):
                sel = lo if e == 0 else jnp.logical_not(lo)
                h = h0 + e
                qe = jnp.where(sel, qn, 0.0).astype(BF16)
                doe = jnp.where(sel, dov, 0.0).astype(BF16)
                delta = jnp.sum(jnp.where(sel, dot_o, 0.0), axis=-1, keepdims=True)
                lse_e = _pair_col(lse_pair, lo, e)
                sp = jnp.where(mask_p, _dot_nt(qe, kpn) * scale + bp_ref[h], NEG)
                sc = jnp.where(mask_c, _dot_nt(qe, kcn) * scale + bc_ref[h], NEG)
                pp, pc = jnp.exp(sp - lse_e), jnp.exp(sc - lse_e)
                dsp = pp * (_dot_nt(doe, vp) - delta)
                dsc = pc * (_dot_nt(doe, vc) - delta)
                dbp_ref[h] += dsp
                dbc_ref[h] += dsc
                dspb, dscb = dsp.astype(BF16), dsc.astype(BF16)
                res.append(((_dot(dspb, kpn) + _dot(dscb, kcn)) * scale,
                            _dot_tn(dspb, qb) * scale, _dot_tn(dscb, qb) * scale,
                            _dot_tn(pp.astype(BF16), dob), _dot_tn(pc.astype(BF16), dob)))
            dqn, dkn_p, dkn_c, dv_p, dv_c = (jnp.where(lo, a, b) for a, b in zip(res[0], res[1]))
            dq, dgq = _pair_rms_bwd(dqn, q, rq, gqv, lo)
            dqs[i] = dq
            dgq_ref[...] += dgq
            finish_prev(i, kp, rkp, dkn_p, dv_p)
            ck[i] = dkn_c
            cv[i] = dv_c
            return carry

        def flush(i, carry):
            kp = kps[i]
            finish_prev(i, kp, _pair_rstd(kp, lo), 0.0, 0.0)
            return carry

        @pl.when(n < nblk)
        def _():
            lax.fori_loop(0, n_it, step, 0, unroll=2)
            _merge_subsequences([(dq_ref, dqs)], d, P)

        @pl.when(n == nblk)
        def _():
            lax.fori_loop(0, n_it, flush, 0)

        _merge_subsequences([(dk_ref, dks), (dv_ref, dvs)], d, P)

    blk = (rows, cw)
    kcol, vcol = DIL_WIDTH // cw, 2 * DIL_WIDTH // cw
    qn_ = lambda n: jnp.minimum(n, nblk - 1)
    pn_ = lambda n: jnp.maximum(n - 1, 0)
    fix3 = lambda pb, n: (0, 0, 0)
    fix2 = lambda pb, n: (0, 0)
    tok_q = pl.BlockSpec(blk, lambda pb, n: (qn_(n), pb))
    tok_p = pl.BlockSpec(blk, lambda pb, n: (pn_(n), pb))
    bias_spec = pl.BlockSpec((DIL_HEADS, DIL_BLOCK, DIL_BLOCK), fix3)
    gain_spec = pl.BlockSpec((1, 128), fix2)
    in_specs = [tok_q,
                pl.BlockSpec(blk, lambda pb, n: (pn_(n), kcol + pb)), pl.BlockSpec(blk, lambda pb, n: (qn_(n), kcol + pb)),
                pl.BlockSpec(blk, lambda pb, n: (pn_(n), vcol + pb)), pl.BlockSpec(blk, lambda pb, n: (qn_(n), vcol + pb)),
                tok_q, tok_q, tok_q, bias_spec, bias_spec, gain_spec, gain_spec]
    tok_shape = jax.ShapeDtypeStruct((T, DIL_WIDTH), F32)
    bias_shape = jax.ShapeDtypeStruct((DIL_HEADS, DIL_BLOCK, DIL_BLOCK), F32)
    dq, dk, dv, dbp, dbc, dgq, dgk = pl.pallas_call(
        body, name=f"dil_bwd_d{d}", grid=(DIL_HEADS // 2 // P, nblk + 1), in_specs=in_specs,
        out_specs=[tok_q, tok_p, tok_p, bias_spec, bias_spec, gain_spec, gain_spec],
        out_shape=[tok_shape, tok_shape, tok_shape, bias_shape, bias_shape,
                   jax.ShapeDtypeStruct((1, 128), F32), jax.ShapeDtypeStruct((1, 128), F32)],
        scratch_shapes=[pltpu.VMEM((n_it, DIL_BLOCK, 128), F32)] * 13,
        compiler_params=_params(2),
    )(proj, proj, proj, proj, proj, o, lse, do, bias_p, bias_c, gq, gk)
    return (dq, dk, dv), dbp, dbc, dgq, dgk


def _t5_bucket(dist):
    max_exact = REL_BUCKETS // 2
    dd = np.maximum(dist, 1).astype(np.float32)
    large = max_exact + (np.log(dd / max_exact) / np.log(REL_MAX_DIST / max_exact)
                         * (REL_BUCKETS - max_exact)).astype(np.int32)
    large = np.minimum(large, REL_BUCKETS - 1)
    return np.where(dist < max_exact, dist, large).astype(np.int32)


def _bucket_onehots():
    i = np.arange(DIL_BLOCK)[:, None]
    j = np.arange(DIL_BLOCK)[None, :]
    out = []
    for _, d in DIL_BRANCHES:
        for dist in (DIL_BLOCK + i - j, i - j):
            bucket = _t5_bucket(np.clip(dist, 0, None) * d).reshape(-1)
            out.append(jnp.asarray(np.eye(REL_BUCKETS, dtype=np.float32)[:, bucket], BF16))
    return out


def _bias_tables(rel_bias, onehots):
    n = len(onehots)

    def body(rb_ref, *refs):
        parts = _split3(rb_ref[...])
        for k in range(n):
            oh = refs[k][...]
            refs[n + k][...] = _dot(parts[0], oh) + _dot(parts[1], oh) + _dot(parts[2], oh)

    return pl.pallas_call(
        body, name="bias_tables",
        out_shape=[jax.ShapeDtypeStruct((DIL_HEADS, DIL_BLOCK * DIL_BLOCK), F32)] * n,
        compiler_params=pltpu.CompilerParams(vmem_limit_bytes=VMEM_LIMIT),
    )(rel_bias, *onehots)


def _bias_grad(dbs, onehots):
    n = len(dbs)

    def body(*refs):
        acc = jnp.zeros((DIL_HEADS, REL_BUCKETS), F32)
        for k in range(n):
            oh = refs[n + k][...]
            for part in _split3(refs[k][...]):
                acc = acc + _dot_nt(part, oh)
        refs[-1][...] = acc

    return pl.pallas_call(
        body, name="bias_grad",
        out_shape=jax.ShapeDtypeStruct((DIL_HEADS, REL_BUCKETS), F32),
        compiler_params=pltpu.CompilerParams(vmem_limit_bytes=VMEM_LIMIT),
    )(*dbs, *onehots)


def _swap_halves(x):
    lane = lax.broadcasted_iota(jnp.int32, x.shape, 1)
    first = (lane % 64) < 32
    return jnp.where(first, pltpu.roll(x, 96, 1), pltpu.roll(x, 32, 1))


def _rope_tables(T):
    pos = jnp.arange(T, dtype=F32)
    inv_freq = ROPE_BASE ** (-jnp.arange(0, MLA_ROPE, 2, dtype=F32) / MLA_ROPE)
    ang = pos[:, None] * inv_freq[None, :]
    z = jnp.zeros((T, 128 - MLA_ROPE), F32)
    cos = jnp.concatenate([jnp.cos(ang), jnp.cos(ang), z], axis=-1)
    sin = jnp.concatenate([-jnp.sin(ang), jnp.sin(ang), z], axis=-1)
    return cos, sin


def _mla_prep(proj, cos, sin, g_qa, g_kva, g_q, g_k, wq, wkv, tm=512):
    T = proj.shape[0]
    H = MLA_HEADS
    scale = MLA_QK ** -0.5

    def body(cq_ref, ckv_ref, kpe_ref, cos_ref, sin_ref, gqa_ref, gkva_ref, gq_ref, gk_ref, wq_ref, wkv_ref,
             q_ref, k_ref, v_ref):
        cosv, sinv = cos_ref[...], sin_ref[...]

        def rope(x):
            return x * cosv + _swap_halves(x) * sinv

        cq = cq_ref[...]
        qp = _dot((cq * _rstd(cq) * gqa_ref[...]).astype(BF16), wq_ref[...])
        ckv = ckv_ref[...]
        kvp = _dot((ckv * _rstd(ckv) * gkva_ref[...]).astype(BF16), wkv_ref[...])
        kpe = kpe_ref[...]
        for h in range(H):
            a = qp[:, MLA_PAD * h:MLA_PAD * (h + 1)]
            qn = a * _rstd(a, MLA_QK) * gq_ref[...]
            q_ref[h, :, 0:128] = (qn[:, 0:128] * scale).astype(BF16)
            q_ref[h, :, 128:256] = (rope(qn[:, 128:256]) * scale).astype(BF16)
            kn = kvp[:, MLA_PAD * h:MLA_PAD * h + 128]
            r = lax.rsqrt((jnp.sum(kn * kn, axis=-1, keepdims=True)
                           + jnp.sum(kpe * kpe, axis=-1, keepdims=True)) / MLA_QK + EPS)
            k_ref[h, :, 0:128] = (kn * r * gk_ref[:, 0:128]).astype(BF16)
            k_ref[h, :, 128:256] = rope(kpe * r * gk_ref[:, 128:256]).astype(BF16)
            v_ref[h] = kvp[:, MLA_PAD * h + 128:MLA_PAD * (h + 1)].astype(BF16)

    fix = lambda i: (0, 0)
    return pl.pallas_call(
        body, name="mla_prep", grid=(T // tm,),
        in_specs=[pl.BlockSpec((tm, 256), lambda i: (i, 6)), pl.BlockSpec((tm, 128), lambda i: (i, 14)),
                  pl.BlockSpec((tm, 128), lambda i: (i, 15)),
                  pl.BlockSpec((tm, 128), lambda i: (i, 0)), pl.BlockSpec((tm, 128), lambda i: (i, 0)),
                  pl.BlockSpec((1, 256), fix), pl.BlockSpec((1, 128), fix),
                  pl.BlockSpec((1, 256), fix), pl.BlockSpec((1, 256), fix),
                  pl.BlockSpec((256, H * MLA_PAD), fix), pl.BlockSpec((128, H * MLA_PAD), fix)],
        out_specs=[pl.BlockSpec((H, tm, MLA_PAD), lambda i: (0, i, 0)), pl.BlockSpec((H, tm, MLA_PAD), lambda i: (0, i, 0)),
                   pl.BlockSpec((H, tm, MLA_V), lambda i: (0, i, 0))],
        out_shape=[jax.ShapeDtypeStruct((H, T, MLA_PAD), BF16), jax.ShapeDtypeStruct((H, T, MLA_PAD), BF16),
                   jax.ShapeDtypeStruct((H, T, MLA_V), BF16)],
        compiler_params=_params(1),
    )(proj, proj, proj, cos, sin, g_qa, g_kva, g_q, g_k, wq, wkv)


def _mla_prep_bwd(proj, cos, sin, g_qa, g_kva, g_q, g_k, wq, wkv, dq, dk, dv, tm=512):
    T = proj.shape[0]
    H = MLA_HEADS
    scale = MLA_QK ** -0.5

    def body(cq_ref, ckv_ref, kpe_ref, cos_ref, sin_ref, gqa_ref, gkva_ref, gq_ref, gk_ref, wq_ref, wkv_ref,
             dq_ref, dk_ref, dv_ref,
             dcq_ref, dckv_ref, dkpe_ref, cqn_ref, ckvn_ref, dqp_ref, dkvp_ref,
             dgqa_ref, dgkva_ref, dgq_ref, dgk_ref):
        @pl.when(pl.program_id(0) == 0)
        def _():
            for ref in (dgqa_ref, dgkva_ref, dgq_ref, dgk_ref):
                ref[...] = jnp.zeros_like(ref)

        cosv, sinv = cos_ref[...], sin_ref[...]

        def rope_bwd(dy):
            return dy * cosv + _swap_halves(dy * sinv)

        cq = cq_ref[...]
        rcq = _rstd(cq)
        cqn = (cq * rcq * gqa_ref[...]).astype(BF16)
        cqn_ref[...] = cqn
        qp = _dot(cqn, wq_ref[...])
        ckv = ckv_ref[...]
        rckv = _rstd(ckv)
        ckvn = (ckv * rckv * gkva_ref[...]).astype(BF16)
        ckvn_ref[...] = ckvn
        kvp = _dot(ckvn, wkv_ref[...])
        kpe = kpe_ref[...]
        dkpe = jnp.zeros_like(kpe)
        dgq = jnp.zeros((1, MLA_PAD), F32)
        dgk = jnp.zeros((1, MLA_PAD), F32)
        for h in range(H):
            a = qp[:, MLA_PAD * h:MLA_PAD * (h + 1)]
            dqh = dq_ref[h]
            dn = jnp.concatenate([dqh[:, 0:128], rope_bwd(dqh[:, 128:256])], axis=-1) * scale
            da, dg = _rms_bwd(dn, a, gq_ref[...], _rstd(a, MLA_QK), MLA_QK)
            dgq = dgq + jnp.sum(dg, axis=0, keepdims=True)
            dqp_ref[:, MLA_PAD * h:MLA_PAD * (h + 1)] = da.astype(BF16)

            ak = jnp.concatenate([kvp[:, MLA_PAD * h:MLA_PAD * h + 128], kpe], axis=-1)
            dkh = dk_ref[h]
            dnk = jnp.concatenate([dkh[:, 0:128], rope_bwd(dkh[:, 128:256])], axis=-1)
            dak, dg = _rms_bwd(dnk, ak, gk_ref[...], _rstd(ak, MLA_QK), MLA_QK)
            dgk = dgk + jnp.sum(dg, axis=0, keepdims=True)
            dkpe = dkpe + dak[:, 128:256]
            dkvp_ref[:, MLA_PAD * h:MLA_PAD * h + 128] = dak[:, 0:128].astype(BF16)
            dkvp_ref[:, MLA_PAD * h + 128:MLA_PAD * (h + 1)] = dv_ref[h].astype(BF16)
        dkpe_ref[...] = dkpe
        dgq_ref[...] += dgq
        dgk_ref[...] += dgk
        dcq, dg = _rms_bwd(_dot_nt(dqp_ref[...], wq_ref[...]), cq, gqa_ref[...], rcq)
        dcq_ref[...] = dcq
        dgqa_ref[...] += jnp.sum(dg, axis=0, keepdims=True)
        dckv, dg = _rms_bwd(_dot_nt(dkvp_ref[...], wkv_ref[...]), ckv, gkva_ref[...], rckv)
        dckv_ref[...] = dckv
        dgkva_ref[...] += jnp.sum(dg, axis=0, keepdims=True)

    fix = lambda i: (0, 0)
    row = lambda i: (i, 0)
    head = lambda i: (0, i, 0)
    return pl.pallas_call(
        body, name="mla_prep_bwd", grid=(T // tm,),
        in_specs=[pl.BlockSpec((tm, 256), lambda i: (i, 6)), pl.BlockSpec((tm, 128), lambda i: (i, 14)),
                  pl.BlockSpec((tm, 128), lambda i: (i, 15)),
                  pl.BlockSpec((tm, 128), row), pl.BlockSpec((tm, 128), row),
                  pl.BlockSpec((1, 256), fix), pl.BlockSpec((1, 128), fix),
                  pl.BlockSpec((1, 256), fix), pl.BlockSpec((1, 256), fix),
                  pl.BlockSpec((256, H * MLA_PAD), fix), pl.BlockSpec((128, H * MLA_PAD), fix),
                  pl.BlockSpec((H, tm, MLA_PAD), head), pl.BlockSpec((H, tm, MLA_PAD), head),
                  pl.BlockSpec((H, tm, MLA_V), head)],
        out_specs=[pl.BlockSpec((tm, 256), row), pl.BlockSpec((tm, 128), row), pl.BlockSpec((tm, 128), row),
                   pl.BlockSpec((tm, 256), row), pl.BlockSpec((tm, 128), row),
                   pl.BlockSpec((tm, H * MLA_PAD), row), pl.BlockSpec((tm, H * MLA_PAD), row),
                   pl.BlockSpec((1, 256), fix), pl.BlockSpec((1, 128), fix),
                   pl.BlockSpec((1, 256), fix), pl.BlockSpec((1, 256), fix)],
        out_shape=[jax.ShapeDtypeStruct((T, 256), F32), jax.ShapeDtypeStruct((T, 128), F32),
                   jax.ShapeDtypeStruct((T, 128), F32),
                   jax.ShapeDtypeStruct((T, 256), BF16), jax.ShapeDtypeStruct((T, 128), BF16),
                   jax.ShapeDtypeStruct((T, H * MLA_PAD), BF16), jax.ShapeDtypeStruct((T, H * MLA_PAD), BF16),
                   jax.ShapeDtypeStruct((1, 256), F32), jax.ShapeDtypeStruct((1, 128), F32),
                   jax.ShapeDtypeStruct((1, 256), F32), jax.ShapeDtypeStruct((1, 256), F32)],
        compiler_params=_params(1),
    )(proj, proj, proj, cos, sin, g_qa, g_kva, g_q, g_k, wq, wkv, dq, dk, dv)


def _causal_pairs(T, tq, tk, key_major):
    pairs = [(i, j) for i in range(T // tq) for j in range(T // tk) if j * tk <= i * tq + tq - 1]
    if key_major:
        pairs.sort(key=lambda p: (p[1], p[0]))
    outer = [p[1] if key_major else p[0] for p in pairs]
    first = [int(t == 0 or outer[t] != outer[t - 1]) for t in range(len(pairs))]
    last = [int(t == len(pairs) - 1 or outer[t] != outer[t + 1]) for t in range(len(pairs))]
    tab = lambda v: jnp.asarray(np.array(v, np.int32))
    return tab([p[0] for p in pairs]), tab([p[1] for p in pairs]), tab(first), tab(last)


def _causal_scores(qv, kv, qi, ki, row0, tq, tk, masked):
    s = _dot_nt(qv, kv)
    if masked:
        row = lax.broadcasted_iota(jnp.int32, s.shape, 0) + (qi * tq + row0)
        col = lax.broadcasted_iota(jnp.int32, s.shape, 1) + ki * tk
        s = jnp.where(col <= row, s, NEG)
    return s


def _mla_attn(q, k, v, ride=None, tq=512, tk=1024, rc=256):
    H, T, _ = q.shape
    tables = _causal_pairs(T, tq, tk, key_major=False)
    n_pairs = int(tables[0].shape[0])
    r_args, r_in, r_shape, r_out, r_scratch = _ride_parts(ride)

    def body(qt, kt, ft, lt, q_ref, k_ref, v_ref, o_ref, lse_ref, m_s, l_s, acc):
        t = pl.program_id(1)
        qi, ki = qt[t], kt[t]

        @pl.when(ft[t] == 1)
        def _():
            m_s[...] = jnp.full_like(m_s, NEG)
            l_s[...] = jnp.zeros_like(l_s)
            acc[...] = jnp.zeros_like(acc)

        def update(masked):
            kk, vv = k_ref[...], v_ref[...]
            for c in range(tq // rc):
                rows = pl.ds(c * rc, rc)
                s = _causal_scores(q_ref[rows, :], kk, qi, ki, c * rc, tq, tk, masked)
                m_old = m_s[rows, :]
                m_new = jnp.maximum(m_old, jnp.max(s, axis=-1, keepdims=True))
                alpha = jnp.exp(m_old - m_new)
                p = jnp.exp(s - m_new)
                l_s[rows, :] = alpha * l_s[rows, :] + jnp.sum(p, axis=-1, keepdims=True)
                acc[rows, :] = alpha * acc[rows, :] + _dot(p.astype(BF16), vv)
                m_s[rows, :] = m_new

        diagonal = (ki + 1) * tk - 1 > qi * tq

        @pl.when(diagonal)
        def _():
            update(True)

        @pl.when(jnp.logical_not(diagonal))
        def _():
            update(False)

        @pl.when(lt[t] == 1)
        def _():
            o_ref[...] = acc[...] / l_s[...]
            lse_ref[...] = jnp.broadcast_to(m_s[...] + jnp.log(l_s[...]), lse_ref.shape)

    qrow = lambda h, t, qt, kt, ft, lt: (h, qt[t], 0)
    krow = lambda h, t, qt, kt, ft, lt: (h, kt[t], 0)
    first = lambda: (pl.program_id(0) == 0) & (pl.program_id(1) == 0)
    last = lambda: (pl.program_id(0) == H - 1) & (pl.program_id(1) == n_pairs - 1)
    outs = pl.pallas_call(
        _riding(body, 7, 2, 3, ride, first, last), name="mla_attn",
        grid_spec=pltpu.PrefetchScalarGridSpec(
            num_scalar_prefetch=4, grid=(H, n_pairs),
            in_specs=[pl.BlockSpec((None, tq, MLA_PAD), qrow), pl.BlockSpec((None, tk, MLA_PAD), krow),
                      pl.BlockSpec((None, tk, MLA_V), krow)] + r_in,
            out_specs=[pl.BlockSpec((tq, MLA_V), lambda h, t, qt, kt, ft, lt: (qt[t], h)),
                       pl.BlockSpec((None, tq, 128), qrow)] + r_out,
            scratch_shapes=[pltpu.VMEM((tq, 1), F32), pltpu.VMEM((tq, 1), F32), pltpu.VMEM((tq, MLA_V), F32)]
            + r_scratch),
        out_shape=[jax.ShapeDtypeStruct((T, H * MLA_V), F32), jax.ShapeDtypeStruct((H, T, 128), F32)] + r_shape,
        compiler_params=_params(2),
    )(*tables, q, k, v, *r_args)
    return outs[:2], outs[2:]


def _mla_attn_bwd(q, k, v, o, lse, do, ride=None, tq=512, tk=1024, rc=512):
    H, T, _ = q.shape
    tables = _causal_pairs(T, tq, tk, key_major=True)
    n_pairs = int(tables[0].shape[0])
    r_args, r_in, r_shape, r_out, r_scratch = _ride_parts(ride)

    def body(qt, kt, ft, lt, q_ref, k_ref, v_ref, o_ref, lse_ref, do_ref, dq_ref, dk_ref, dv_ref, dk_s, dv_s):
        t = pl.program_id(1)
        qi, ki = qt[t], kt[t]

        @pl.when(t == 0)
        def _():
            dq_ref[...] = jnp.zeros_like(dq_ref)

        @pl.when(ft[t] == 1)
        def _():
            dk_s[...] = jnp.zeros_like(dk_s)
            dv_s[...] = jnp.zeros_like(dv_s)

        def update(masked):
            kk, vv = k_ref[...], v_ref[...]
            for c in range(tq // rc):
                rows = pl.ds(c * rc, rc)
                qv, dov = q_ref[rows, :], do_ref[rows, :]
                delta = jnp.sum(dov * o_ref[rows, :], axis=-1, keepdims=True)
                lse_v = jnp.max(lse_ref[rows, :], axis=-1, keepdims=True)
                p = jnp.exp(_causal_scores(qv, kk, qi, ki, c * rc, tq, tk, masked) - lse_v)
                dob = dov.astype(BF16)
                dv_s[...] += _dot_tn(p.astype(BF16), dob)
                ds = (p * (_dot_nt(dob, vv) - delta)).astype(BF16)
                dk_s[...] += _dot_tn(ds, qv)
                out_rows = pl.ds(pl.multiple_of(qi * tq + c * rc, rc), rc)
                dq_ref[out_rows, :] += _dot(ds, kk)

        diagonal = (ki + 1) * tk - 1 > qi * tq

        @pl.when(diagonal)
        def _():
            update(True)

        @pl.when(jnp.logical_not(diagonal))
        def _():
            update(False)

        @pl.when(lt[t] == 1)
        def _():
            dk_ref[...] = dk_s[...]
            dv_ref[...] = dv_s[...]

    qrow = lambda h, t, qt, kt, ft, lt: (h, qt[t], 0)
    krow = lambda h, t, qt, kt, ft, lt: (h, kt[t], 0)
    qcol = lambda h, t, qt, kt, ft, lt: (qt[t], h)
    first = lambda: (pl.program_id(0) == 0) & (pl.program_id(1) == 0)
    last = lambda: (pl.program_id(0) == H - 1) & (pl.program_id(1) == n_pairs - 1)
    outs = pl.pallas_call(
        _riding(body, 10, 3, 2, ride, first, last), name="mla_attn_bwd",
        grid_spec=pltpu.PrefetchScalarGridSpec(
            num_scalar_prefetch=4, grid=(H, n_pairs),
            in_specs=[pl.BlockSpec((None, tq, MLA_PAD), qrow), pl.BlockSpec((None, tk, MLA_PAD), krow),
                      pl.BlockSpec((None, tk, MLA_V), krow), pl.BlockSpec((tq, MLA_V), qcol),
                      pl.BlockSpec((None, tq, 128), qrow), pl.BlockSpec((tq, MLA_V), qcol)] + r_in,
            out_specs=[pl.BlockSpec((None, T, MLA_PAD), lambda h, t, qt, kt, ft, lt: (h, 0, 0)),
                       pl.BlockSpec((None, tk, MLA_PAD), krow), pl.BlockSpec((None, tk, MLA_V), krow)] + r_out,
            scratch_shapes=[pltpu.VMEM((tk, MLA_PAD), F32), pltpu.VMEM((tk, MLA_V), F32)] + r_scratch),
        out_shape=[jax.ShapeDtypeStruct((H, T, MLA_PAD), F32), jax.ShapeDtypeStruct((H, T, MLA_PAD), F32),
                   jax.ShapeDtypeStruct((H, T, MLA_V), F32)] + r_shape,
        compiler_params=_params(2),
    )(*tables, q, k, v, o, lse, do, *r_args)
    return outs[:3], outs[3:]


def _pair_gain(g):
    return jnp.tile(g.reshape(1, DIL_HD), (1, 2))


def _pad_gain(g):
    return jnp.pad(g.reshape(1, MLA_QK), ((0, 0), (0, MLA_PAD - MLA_QK)))


def _local_step(x, target, s, comm):
    T = x.shape[0]
    w = comm.w
    gq, gk = _pair_gain(s["dil_q_norm"]), _pair_gain(s["dil_k_norm"])
    g_q, g_k = _pad_gain(s["mla_q_norm"]), _pad_gain(s["mla_k_norm"])
    cos, sin = _rope_tables(T)
    onehots = _bucket_onehots()
    tables = [t.reshape(DIL_HEADS, DIL_BLOCK, DIL_BLOCK) for t in _bias_tables(s["rel_bias"], onehots)]
    biases = list(zip(tables[0::2], tables[1::2]))

    (x1, h1, gate1, up1), got = _ffn_fwd(x, s["ffn1_norm"], w["ffn1_w_gate"], w["ffn1_w_up"], w["ffn1_w_down"],
                                         ride=comm.gather("attn"))
    comm.weights_landed("attn", got)
    hm, proj = _in_proj(x1, s["mix_norm"], w["w_in"])
    dil = None
    for (_, d), (bp, bc) in zip(DIL_BRANCHES, biases):
        dil = _dil_fwd(proj, bp, bc, gq, gk, d, dil)
    o_dil, lse_dil = dil
    q, k, v = _mla_prep(proj, cos, sin, s["mla_q_a_norm"], s["mla_kv_a_norm"], g_q, g_k, w["mla_w_q_b"], w["mla_w_kv_b"])
    (o_mla, lse_mla), got = _mla_attn(q, k, v, ride=comm.gather("ffn2"))
    comm.weights_landed("ffn2", got)
    x2, oc = _out_proj(x1, o_dil, o_mla, s["out_norm_dil"], s["out_norm_mla"], w["w_out"])
    (y, h2, gate2, up2), _ = _ffn_fwd(x2, s["ffn2_norm"], w["ffn2_w_gate"], w["ffn2_w_up"], w["ffn2_w_down"])
    dy, loss = _loss_grad(y, target)

    gw, gs = {}, {}

    def ffn_grads(name, dy_in, x_in, h, gate, up, ride=None):
        (dx, a, dg, du, dyh, dgain), got = _ffn_bwd(dy_in, x_in, s[name + "_norm"], gate, up,
                                                    w[name + "_w_gate"], w[name + "_w_up"], w[name + "_w_down"], ride=ride)
        gs[name + "_norm"] = dgain
        gw[name + "_w_gate"] = _matmul_tn(h, dg, 1024, 1408)
        gw[name + "_w_up"] = _matmul_tn(h, du, 1024, 1408)
        gw[name + "_w_down"] = _matmul_tn(a, dyh, 1408, 1024)
        return dx, got

    dx2, _ = ffn_grads("ffn2", dy, x2, h2, gate2, up2)
    gw["w_out"] = _matmul_tn(oc, dx2, 1024, 1024)
    do_dil, do_mla, gs["out_norm_dil"], gs["out_norm_mla"] = _out_proj_bwd(
        dx2, o_dil, o_mla, s["out_norm_dil"], s["out_norm_mla"], w["w_out"])

    (dq, dk, dv), got = _mla_attn_bwd(q, k, v, o_mla, lse_mla, do_mla, ride=comm.scatter("ffn2", gw))
    comm.grads_landed("ffn2", got)
    (dcq, dckv, dkpe, cqn, ckvn, dqp, dkvp, gs["mla_q_a_norm"], gs["mla_kv_a_norm"], dg_q, dg_k) = _mla_prep_bwd(
        proj, cos, sin, s["mla_q_a_norm"], s["mla_kv_a_norm"], g_q, g_k, w["mla_w_q_b"], w["mla_w_kv_b"], dq, dk, dv)
    gs["mla_q_norm"], gs["mla_k_norm"] = dg_q[:, :MLA_QK], dg_k[:, :MLA_QK]
    gw["mla_w_q_b"] = _matmul_tn(cqn, dqp, 256, 1024)
    gw["mla_w_kv_b"] = _matmul_tn(ckvn, dkvp, 128, 1024)

    dqkv, dbs, dgq, dgk = [], [], 0.0, 0.0
    for (_, d), (bp, bc) in zip(DIL_BRANCHES, biases):
        triple, dbp, dbc, dgq_b, dgk_b = _dil_bwd(proj, o_dil, lse_dil, do_dil, bp, bc, gq, gk, d)
        dqkv.append(triple)
        dbs += [dbp.reshape(DIL_HEADS, -1), dbc.reshape(DIL_HEADS, -1)]
        dgq, dgk = dgq + dgq_b, dgk + dgk_b
    gs["dil_q_norm"] = dgq[:, :DIL_HD] + dgq[:, DIL_HD:]
    gs["dil_k_norm"] = dgk[:, :DIL_HD] + dgk[:, DIL_HD:]
    gs["rel_bias"] = _bias_grad(dbs, onehots)

    dx1, dproj, gs["mix_norm"] = _in_proj_bwd(dx2, x1, s["mix_norm"], w["w_in"], dqkv, dcq, dckv, dkpe)
    gw["w_in"] = _matmul_tn(hm, dproj, 1024, 1024)
    grad_x, got = ffn_grads("ffn1", dx1, x, h1, gate1, up1, ride=comm.scatter("attn", gw))
    comm.grads_landed("attn", got)
    return loss, grad_x, gw, gs


def _position():
    x, y, c = lax.axis_index("x"), lax.axis_index("y"), lax.axis_index("c")
    return x, y, c, 4 * x + 2 * y + c


def _peer(x, y, c, k):
    px = 1 - x if k & 4 else x
    py = 1 - y if k & 2 else y
    pc = 1 - c if k & 1 else c
    return (px, py, pc), 4 * px + 2 * py + pc


class _Ride:
    def __init__(self, arrays, scatter):
        self.arrays, self.scatter = list(arrays), list(scatter)
        self.n = n = len(self.arrays)
        self.specs = [pl.BlockSpec(memory_space=pl.ANY)] * n
        self.out_shape = [jax.ShapeDtypeStruct(a.shape if sc else (N_DEV,) + a.shape, a.dtype)
                          for a, sc in zip(self.arrays, self.scatter)]
        self.scratch = [pltpu.SemaphoreType.DMA((n, N_DEV - 1)), pltpu.SemaphoreType.DMA((n, N_DEV - 1)),
                        pltpu.SemaphoreType.DMA((n,))]

    def _copies(self, ins, outs, sems):
        send_sems, recv_sems, local_sems = sems
        x, y, c, me = _position()
        copies = []
        for a in range(self.n):
            src = ins[a].at[me] if self.scatter[a] else ins[a]
            copies.append(pltpu.make_async_copy(src, outs[a].at[me], local_sems.at[a]))
        for k in range(1, N_DEV):
            peer, peer_idx = _peer(x, y, c, k)
            for a in range(self.n):
                src = ins[a].at[peer_idx] if self.scatter[a] else ins[a]
                copies.append(pltpu.make_async_remote_copy(
                    src_ref=src, dst_ref=outs[a].at[me], send_sem=send_sems.at[a, k - 1], recv_sem=recv_sems.at[a, k - 1],
                    device_id=peer, device_id_type=pl.DeviceIdType.MESH))
        return copies

    def start(self, ins, outs, sems):
        for cp in self._copies(ins, outs, sems):
            cp.start()

    def wait(self, ins, outs, sems):
        for cp in self._copies(ins, outs, sems):
            cp.wait()


def _ride_parts(ride):
    if ride is None:
        return [], [], [], [], []
    return ride.arrays, ride.specs, ride.out_shape, ride.specs, ride.scratch


def _riding(body, n_in, n_out, n_scratch, ride, first, last):
    if ride is None:
        return body
    n = ride.n
    i1, i2 = n_in + n, n_in + n + n_out
    i3, i4 = i2 + n, i2 + n + n_scratch

    def wrapped(*refs):
        ins, outs, sems = refs[n_in:i1], refs[i2:i3], refs[i4:]

        @pl.when(first())
        def _():
            ride.start(ins, outs, sems)

        body(*refs[:n_in], *refs[i1:i2], *refs[i3:i4])

        @pl.when(last())
        def _():
            ride.wait(ins, outs, sems)

    return wrapped


def _exchange(ride, name):
    def body(*refs):
        parts = refs[:ride.n], refs[ride.n:2 * ride.n], refs[2 * ride.n:]
        ride.start(*parts)
        ride.wait(*parts)

    return pl.pallas_call(body, name=name, in_specs=ride.specs, out_specs=ride.specs, out_shape=ride.out_shape,
                          scratch_shapes=ride.scratch)(*ride.arrays)


def _adamw_math(wv, g, m, v):
    m = ADAM_B1 * m + (1.0 - ADAM_B1) * g
    v = ADAM_B2 * v + (1.0 - ADAM_B2) * (g * g)
    m_hat = m / (1.0 - ADAM_B1 ** ADAM_STEP)
    v_hat = v / (1.0 - ADAM_B2 ** ADAM_STEP)
    delta = -ADAM_LR * (m_hat / (jnp.sqrt(v_hat) + ADAM_EPS) + ADAM_WD * wv)
    return delta, m, v


def _adamw(parts, wv, m, v):
    R, C = wv.shape
    tr = max(t for t in range(16, 257, 16) if R % t == 0)

    def body(p_ref, w_ref, m_ref, v_ref, g_ref, d_ref, mo_ref, vo_ref):
        g = p_ref[0].astype(F32)
        for j in range(1, N_DEV):
            g = g + p_ref[j].astype(F32)
        d, mn, vn = _adamw_math(w_ref[...], g, m_ref[...], v_ref[...])
        g_ref[...] = g
        d_ref[...] = d
        mo_ref[...] = mn
        vo_ref[...] = vn

    row = lambda i: (i, 0)
    p_spec = pl.BlockSpec((N_DEV, tr, C), lambda i: (0, i, 0))
    out = jax.ShapeDtypeStruct((R, C), F32)
    return pl.pallas_call(
        body, name="adamw", grid=(R // tr,),
        in_specs=[p_spec, pl.BlockSpec((tr, C), row), pl.BlockSpec((tr, C), row), pl.BlockSpec((tr, C), row)],
        out_specs=[pl.BlockSpec((tr, C), row)] * 4, out_shape=[out] * 4,
        compiler_params=_params(1),
    )(parts, wv, m, v)


_ROW_SHARDED = ("ffn1_w_down", "ffn2_w_down", "w_out")
_GROUPS = {"ffn1": ("ffn1_w_gate", "ffn1_w_up", "ffn1_w_down"),
           "ffn2": ("ffn2_w_gate", "ffn2_w_up", "ffn2_w_down"),
           "attn": ("w_in", "mla_w_q_b", "mla_w_kv_b", "w_out")}
_SMALL = ("ffn1_norm", "mix_norm", "ffn2_norm", "out_norm_dil", "out_norm_mla", "mla_q_a_norm", "rel_bias",
          "mla_q_norm", "mla_k_norm", "mla_kv_a_norm", "dil_q_norm", "dil_k_norm")
_SMALL_ROWS = 48


def _cols_to_full(g):
    return g.transpose(1, 0, 2).reshape(g.shape[1], N_DEV * g.shape[2])


def _full_to_cols(f):
    return f.reshape(f.shape[0], N_DEV, f.shape[1] // N_DEV).transpose(1, 0, 2)


def _to_full(name, g):
    if name in _ROW_SHARDED:
        return g.reshape(-1, g.shape[-1])
    f = _cols_to_full(g)
    if name == "w_in":
        f = jnp.pad(f, ((0, 0), (0, PROJ_PAD - PROJ_COLS)))
    if name == "mla_w_q_b":
        f = jnp.pad(f.reshape(-1, MLA_HEADS, MLA_QK), ((0, 0), (0, 0), (0, MLA_PAD - MLA_QK)))
        f = f.reshape(-1, MLA_HEADS * MLA_PAD)
    return f


def _to_parts(name, f):
    if name in _ROW_SHARDED:
        return f.reshape(N_DEV, -1, f.shape[-1]).astype(BF16)
    if name == "w_in":
        f = f[:, :PROJ_COLS]
    if name == "mla_w_q_b":
        f = f.reshape(-1, MLA_HEADS, MLA_PAD)[:, :, :MLA_QK].reshape(-1, MLA_HEADS * MLA_QK)
    return _full_to_cols(f).astype(BF16)


class _Comm:
    def __init__(self, shards):
        self.shards, self.w, self.recv = shards, {}, {}

    def gather(self, group):
        return _Ride([self.shards[n] for n in _GROUPS[group]], [False] * len(_GROUPS[group]))

    def scatter(self, group, grads):
        return _Ride([_to_parts(n, grads[n]) for n in _GROUPS[group]], [True] * len(_GROUPS[group]))

    def weights_landed(self, group, got):
        self.w.update({n: _to_full(n, g) for n, g in zip(_GROUPS[group], got)})

    def grads_landed(self, group, got):
        self.recv.update(zip(_GROUPS[group], got))


def _pack_small(parts, extra):
    flat = jnp.concatenate([parts[n].reshape(-1) for n in _SMALL] + [extra.reshape(-1)])
    return jnp.pad(flat, (0, _SMALL_ROWS * 128 - flat.shape[0])).reshape(_SMALL_ROWS, 128)


def _unpack_small(packed, shapes):
    flat, out, off = packed.reshape(-1), {}, 0
    for n in _SMALL:
        size = math.prod(shapes[n])
        out[n] = flat[off:off + size].reshape(shapes[n])
        off += size
    return out, flat[off]


_NAMES = ("ffn1_norm", "ffn1_w_gate", "ffn1_w_up", "ffn1_w_down", "mix_norm", "w_in", "dil_q_norm", "dil_k_norm",
          "rel_bias", "mla_q_a_norm", "mla_w_q_b", "mla_kv_a_norm", "mla_w_kv_b", "mla_q_norm", "mla_k_norm",
          "out_norm_dil", "out_norm_mla", "w_out", "ffn2_norm", "ffn2_w_gate", "ffn2_w_up", "ffn2_w_down")


def kernel(x, ffn1_norm, ffn1_w_gate, ffn1_w_up, ffn1_w_down, mix_norm, w_in, dil_q_norm, dil_k_norm, rel_bias, mla_q_a_norm, mla_w_q_b, mla_kv_a_norm, mla_w_kv_b, mla_q_norm, mla_k_norm, out_norm_dil, out_norm_mla, w_out, ffn2_norm, ffn2_w_gate, ffn2_w_up, ffn2_w_down, loss_target, m_ffn1_norm, m_ffn1_w_gate, m_ffn1_w_up, m_ffn1_w_down, m_mix_norm, m_w_in, m_dil_q_norm, m_dil_k_norm, m_rel_bias, m_mla_q_a_norm, m_mla_w_q_b, m_mla_kv_a_norm, m_mla_w_kv_b, m_mla_q_norm, m_mla_k_norm, m_out_norm_dil, m_out_norm_mla, m_w_out, m_ffn2_norm, m_ffn2_w_gate, m_ffn2_w_up, m_ffn2_w_down, v_ffn1_norm, v_ffn1_w_gate, v_ffn1_w_up, v_ffn1_w_down, v_mix_norm, v_w_in, v_dil_q_norm, v_dil_k_norm, v_rel_bias, v_mla_q_a_norm, v_mla_w_q_b, v_mla_kv_a_norm, v_mla_w_kv_b, v_mla_q_norm, v_mla_k_norm, v_out_norm_dil, v_out_norm_mla, v_w_out, v_ffn2_norm, v_ffn2_w_gate, v_ffn2_w_up, v_ffn2_w_down):
    args = locals()
    wts = {n: args[n] for n in _NAMES}
    mom = {n: args["m_" + n] for n in _NAMES}
    var = {n: args["v_" + n] for n in _NAMES}

    matrices = [n for group in _GROUPS.values() for n in group]
    comm = _Comm({n: wts[n][0].astype(BF16) for n in matrices})
    comm.weights_landed("ffn1", _exchange(comm.gather("ffn1"), "gather_ffn1"))
    small = {n: wts[n].reshape(1, -1) if n != "rel_bias" else wts[n] for n in _SMALL}

    loss, grad_x, gw, gs = _local_step(x[0], loss_target[0], small, comm)

    last = comm.scatter("ffn1", gw)
    got = _exchange(_Ride(last.arrays + [_pack_small(gs, loss[0, 0])], last.scatter + [False]), "scatter_ffn1")
    comm.grads_landed("ffn1", got[:-1])

    res = {n: _adamw(comm.recv[n], wts[n][0], mom[n][0], var[n][0]) for n in matrices}
    shapes = {n: wts[n].shape for n in _SMALL}
    zero = jnp.zeros((), F32)
    packed = _adamw(got[-1], _pack_small(wts, zero), _pack_small(mom, zero), _pack_small(var, zero))
    loss_total = None
    for slot, q in enumerate(packed):
        vals, extra = _unpack_small(q, shapes)
        if slot == 0:
            loss_total = extra
        for n in _SMALL:
            res.setdefault(n, [None] * 4)[slot] = vals[n]
    outs = [loss_total, grad_x[None]]
    for slot in range(4):
        outs += [res[n][slot].reshape(wts[n].shape) for n in _NAMES]
    return tuple(outs)
```

```python
import math

import numpy as np
import jax
import jax.numpy as jnp
from jax import lax
from jax.experimental import pallas as pl
from jax.experimental.pallas import tpu as pltpu

F32, BF16 = jnp.float32, jnp.bfloat16
EPS = 1e-6
NEG = -1e30
N_DEV = 8

DIL_HEADS, DIL_HD = 8, 64
DIL_WIDTH = DIL_HEADS * DIL_HD
DIL_BRANCHES = ((128, 1), (512, 4), (2048, 16))
DIL_BLOCK = 128
MLA_HEADS, MLA_NOPE, MLA_ROPE, MLA_V = 4, 128, 64, 128
MLA_QK = MLA_NOPE + MLA_ROPE
MLA_PAD = 256
ROPE_BASE = 10000.0
REL_BUCKETS, REL_MAX_DIST = 32, 2048
PROJ_COLS, PROJ_PAD = 1984, 2048
FFN_RESID = 0.5
ADAM_LR, ADAM_B1, ADAM_B2, ADAM_EPS, ADAM_WD, ADAM_STEP = 0.001, 0.9, 0.999, 1e-08, 0.01, 10
VMEM_LIMIT = 56 * 1024 * 1024

_NT = (((1,), (1,)), ((), ()))
_TN = (((0,), (0,)), ((), ()))


def _dot(a, b):
    return jnp.dot(a, b, preferred_element_type=F32)


def _dot_nt(a, b):
    return lax.dot_general(a, b, _NT, preferred_element_type=F32)


def _dot_tn(a, b):
    return lax.dot_general(a, b, _TN, preferred_element_type=F32)


def _params(n_axes):
    return pltpu.CompilerParams(dimension_semantics=("arbitrary",) * n_axes, vmem_limit_bytes=VMEM_LIMIT)


def _rstd(x, n=None):
    n = x.shape[-1] if n is None else n
    return lax.rsqrt(jnp.sum(x * x, axis=-1, keepdims=True) / n + EPS)


def _rms_bwd(dy, x, g, r, n=None):
    n = x.shape[-1] if n is None else n
    u = dy * g
    dx = r * u - x * (r * r * r) * (jnp.sum(u * x, axis=-1, keepdims=True) / n)
    return dx, dy * x * r


def _sigmoid(x):
    return 1.0 / (1.0 + jnp.exp(-x))


def _split3(x):
    parts = []
    for _ in range(3):
        xb = x.astype(BF16)
        parts.append(xb)
        x = x - xb.astype(F32)
    return parts


def _ffn_fwd(x, gain, wg, wu, wd, ride=None, tm=1024, tf=256):
    T, D = x.shape
    F = wg.shape[1]
    ni, nj = T // tm, F // tf
    r_args, r_in, r_shape, r_out, r_scratch = _ride_parts(ride)

    def body(x_ref, g_ref, wg_ref, wu_ref, wd_ref, xo_ref, h_ref, gate_ref, up_ref, acc):
        j = pl.program_id(1)

        @pl.when(j == 0)
        def _():
            xv = x_ref[...]
            h_ref[...] = (xv * _rstd(xv) * g_ref[...]).astype(BF16)
            acc[...] = jnp.zeros_like(acc)

        h = h_ref[...]
        g = _dot(h, wg_ref[...])
        u = _dot(h, wu_ref[...])
        gate_ref[...] = g.astype(BF16)
        up_ref[...] = u.astype(BF16)
        a = (g * _sigmoid(g) * u).astype(BF16)
        acc[...] += _dot(a, wd_ref[...])

        @pl.when(j == nj - 1)
        def _():
            xo_ref[...] = x_ref[...] + FFN_RESID * acc[...]

    first = lambda: (pl.program_id(0) == 0) & (pl.program_id(1) == 0)
    last = lambda: (pl.program_id(0) == ni - 1) & (pl.program_id(1) == nj - 1)
    outs = pl.pallas_call(
        _riding(body, 5, 4, 1, ride, first, last), name="ffn_fwd", grid=(ni, nj),
        in_specs=[pl.BlockSpec((tm, D), lambda i, j: (i, 0)), pl.BlockSpec((1, D), lambda i, j: (0, 0)),
                  pl.BlockSpec((D, tf), lambda i, j: (0, j)), pl.BlockSpec((D, tf), lambda i, j: (0, j)),
                  pl.BlockSpec((tf, D), lambda i, j: (j, 0))] + r_in,
        out_specs=[pl.BlockSpec((tm, D), lambda i, j: (i, 0)), pl.BlockSpec((tm, D), lambda i, j: (i, 0)),
                   pl.BlockSpec((tm, tf), lambda i, j: (i, j)), pl.BlockSpec((tm, tf), lambda i, j: (i, j))] + r_out,
        out_shape=[jax.ShapeDtypeStruct((T, D), F32), jax.ShapeDtypeStruct((T, D), BF16),
                   jax.ShapeDtypeStruct((T, F), BF16), jax.ShapeDtypeStruct((T, F), BF16)] + r_shape,
        scratch_shapes=[pltpu.VMEM((tm, D), F32)] + r_scratch,
        compiler_params=_params(2),
    )(x, gain, wg, wu, wd, *r_args)
    return outs[:4], outs[4:]


def _ffn_bwd(dy, x, gain, gate, up, wg, wu, wd, ride=None, tm=1024, tf=256):
    T, D = x.shape
    F = wg.shape[1]
    ni, nj = T // tm, F // tf
    r_args, r_in, r_shape, r_out, r_scratch = _ride_parts(ride)

    def body(dy_ref, x_ref, g_ref, gate_ref, up_ref, wg_ref, wu_ref, wd_ref,
             dx_ref, a_ref, dg_ref, du_ref, dyh_ref, dgain_ref, acc):
        i, j = pl.program_id(0), pl.program_id(1)

        @pl.when((i == 0) & (j == 0))
        def _():
            dgain_ref[...] = jnp.zeros_like(dgain_ref)

        @pl.when(j == 0)
        def _():
            dyh_ref[...] = (FFN_RESID * dy_ref[...]).astype(BF16)
            acc[...] = jnp.zeros_like(acc)

        da = _dot_nt(dyh_ref[...], wd_ref[...])
        g = gate_ref[...].astype(F32)
        u = up_ref[...].astype(F32)
        sig = _sigmoid(g)
        s = g * sig
        a_ref[...] = (s * u).astype(BF16)
        dg = (da * u * (sig * (1.0 + g * (1.0 - sig)))).astype(BF16)
        du = (da * s).astype(BF16)
        dg_ref[...] = dg
        du_ref[...] = du
        acc[...] += _dot_nt(dg, wg_ref[...]) + _dot_nt(du, wu_ref[...])

        @pl.when(j == nj - 1)
        def _():
            xv = x_ref[...]
            dxn, dgc = _rms_bwd(acc[...], xv, g_ref[...], _rstd(xv))
            dx_ref[...] = dy_ref[...] + dxn
            dgain_ref[...] += jnp.sum(dgc, axis=0, keepdims=True)

    first = lambda: (pl.program_id(0) == 0) & (pl.program_id(1) == 0)
    last = lambda: (pl.program_id(0) == ni - 1) & (pl.program_id(1) == nj - 1)
    outs = pl.pallas_call(
        _riding(body, 8, 6, 1, ride, first, last), name="ffn_bwd", grid=(ni, nj),
        in_specs=[pl.BlockSpec((tm, D), lambda i, j: (i, 0)), pl.BlockSpec((tm, D), lambda i, j: (i, 0)),
                  pl.BlockSpec((1, D), lambda i, j: (0, 0)),
                  pl.BlockSpec((tm, tf), lambda i, j: (i, j)), pl.BlockSpec((tm, tf), lambda i, j: (i, j)),
                  pl.BlockSpec((D, tf), lambda i, j: (0, j)), pl.BlockSpec((D, tf), lambda i, j: (0, j)),
                  pl.BlockSpec((tf, D), lambda i, j: (j, 0))] + r_in,
        out_specs=[pl.BlockSpec((tm, D), lambda i, j: (i, 0)),
                   pl.BlockSpec((tm, tf), lambda i, j: (i, j)), pl.BlockSpec((tm, tf), lambda i, j: (i, j)),
                   pl.BlockSpec((tm, tf), lambda i, j: (i, j)),
                   pl.BlockSpec((tm, D), lambda i, j: (i, 0)), pl.BlockSpec((1, D), lambda i, j: (0, 0))] + r_out,
        out_shape=[jax.ShapeDtypeStruct((T, D), F32), jax.ShapeDtypeStruct((T, F), BF16),
                   jax.ShapeDtypeStruct((T, F), BF16), jax.ShapeDtypeStruct((T, F), BF16),
                   jax.ShapeDtypeStruct((T, D), BF16), jax.ShapeDtypeStruct((1, D), F32)] + r_shape,
        scratch_shapes=[pltpu.VMEM((tm, D), F32)] + r_scratch,
        compiler_params=_params(2),
    )(dy, x, gain, gate, up, wg, wu, wd, *r_args)
    return outs[:6], outs[6:]


def _matmul_tn(a, b, tk, tn, ride=None, tt=512):
    T, K = a.shape
    N = b.shape[1]
    tk, tn = min(tk, K), min(tn, N)
    grid = (K // tk, N // tn, T // tt)
    r_args, r_in, r_shape, r_out, r_scratch = _ride_parts(ride)

    def body(a_ref, b_ref, o_ref):
        @pl.when(pl.program_id(2) == 0)
        def _():
            o_ref[...] = jnp.zeros_like(o_ref)

        o_ref[...] += _dot_tn(a_ref[...].astype(BF16), b_ref[...].astype(BF16))

    first = lambda: (pl.program_id(0) == 0) & (pl.program_id(1) == 0) & (pl.program_id(2) == 0)
    last = lambda: ((pl.program_id(0) == grid[0] - 1) & (pl.program_id(1) == grid[1] - 1)
                    & (pl.program_id(2) == grid[2] - 1))
    outs = pl.pallas_call(
        _riding(body, 2, 1, 0, ride, first, last), name="matmul_tn", grid=grid,
        in_specs=[pl.BlockSpec((tt, tk), lambda k, n, t: (t, k)), pl.BlockSpec((tt, tn), lambda k, n, t: (t, n))] + r_in,
        out_specs=[pl.BlockSpec((tk, tn), lambda k, n, t: (k, n))] + r_out,
        out_shape=[jax.ShapeDtypeStruct((K, N), F32)] + r_shape,
        scratch_shapes=r_scratch,
        compiler_params=_params(3),
    )(a, b, *r_args)
    return outs[0], outs[1:]


def _loss_grad(y, target, tm=512):
    T, D = y.shape

    def body(y_ref, t_ref, dy_ref, loss_ref):
        @pl.when(pl.program_id(0) == 0)
        def _():
            loss_ref[...] = jnp.zeros_like(loss_ref)

        e = y_ref[...] - t_ref[...]
        dy_ref[...] = e * (1.0 / D)
        loss_ref[...] += (0.5 / D) * jnp.sum(e * e)

    return pl.pallas_call(
        body, name="loss_grad", grid=(T // tm,),
        in_specs=[pl.BlockSpec((tm, D), lambda i: (i, 0)), pl.BlockSpec((tm, D), lambda i: (i, 0))],
        out_specs=[pl.BlockSpec((tm, D), lambda i: (i, 0)), pl.BlockSpec((1, 128), lambda i: (0, 0))],
        out_shape=[jax.ShapeDtypeStruct((T, D), F32), jax.ShapeDtypeStruct((1, 128), F32)],
        compiler_params=_params(1),
    )(y, target)


def _in_proj(x, gain, w, gq, gk, tm=512):
    T, D = x.shape
    N = w.shape[1]
    W = DIL_WIDTH

    def body(x_ref, g_ref, w_ref, gq_ref, gk_ref, h_ref, p_ref, qh_ref, kh_ref):
        xv = x_ref[...]
        h = (xv * _rstd(xv) * g_ref[...]).astype(BF16)
        h_ref[...] = h
        p_ref[...] = _dot(h, w_ref[...])
        lo = lax.broadcasted_iota(jnp.int32, (tm, 128), 1) < DIL_HD
        for hp in range(DIL_HEADS // 2):
            q = p_ref[:, 128 * hp:128 * (hp + 1)]
            k = p_ref[:, W + 128 * hp:W + 128 * (hp + 1)]
            qh_ref[:, 128 * hp:128 * (hp + 1)] = (q * _pair_rstd(q, lo) * gq_ref[...]).astype(BF16).astype(F32)
            kh_ref[:, 128 * hp:128 * (hp + 1)] = (k * _pair_rstd(k, lo) * gk_ref[...]).astype(BF16).astype(F32)

    row = lambda i: (i, 0)
    fix = lambda i: (0, 0)
    return pl.pallas_call(
        body, name="in_proj", grid=(T // tm,),
        in_specs=[pl.BlockSpec((tm, D), row), pl.BlockSpec((1, D), fix), pl.BlockSpec((D, N), fix),
                  pl.BlockSpec((1, 128), fix), pl.BlockSpec((1, 128), fix)],
        out_specs=[pl.BlockSpec((tm, D), row), pl.BlockSpec((tm, N), row), pl.BlockSpec((tm, W), row),
                   pl.BlockSpec((tm, W), row)],
        out_shape=[jax.ShapeDtypeStruct((T, D), BF16), jax.ShapeDtypeStruct((T, N), F32),
                   jax.ShapeDtypeStruct((T, W), F32), jax.ShapeDtypeStruct((T, W), F32)],
        compiler_params=_params(1),
    )(x, gain, w, gq, gk)


def _in_proj_bwd(dx_up, x, gain, w, proj, gq, gk, dqkv, dcq, dckv, dkpe, tm=512):
    T, D = x.shape
    N = w.shape[1]
    W = DIL_WIDTH
    nb = len(dqkv)

    def body(*refs):
        dxu_ref, x_ref, g_ref, w_ref, q_ref, k_ref, gq_ref, gk_ref = refs[:8]
        dil_refs = refs[8:8 + 3 * nb]
        dcq_ref, dckv_ref, dkpe_ref, dx_ref, dp_ref, dgain_ref, dgq_ref, dgk_ref = refs[8 + 3 * nb:]

        @pl.when(pl.program_id(0) == 0)
        def _():
            for ref in (dgain_ref, dgq_ref, dgk_ref):
                ref[...] = jnp.zeros_like(ref)

        lo = lax.broadcasted_iota(jnp.int32, (tm, 128), 1) < DIL_HD
        norms = ((q_ref, gq_ref, dgq_ref), (k_ref, gk_ref, dgk_ref))
        for part in range(3):
            acc = dil_refs[part][...]
            for b in range(1, nb):
                acc = acc + dil_refs[3 * b + part][...]
            if part == 2:
                dp_ref[:, 2 * W:3 * W] = acc.astype(BF16)
                continue
            raw_ref, gn_ref, dgn_ref = norms[part]
            for hp in range(DIL_HEADS // 2):
                raw = raw_ref[:, 128 * hp:128 * (hp + 1)]
                d_raw, dgn = _pair_rms_bwd(acc[:, 128 * hp:128 * (hp + 1)], raw, _pair_rstd(raw, lo), gn_ref[...], lo)
                dp_ref[:, part * W + 128 * hp:part * W + 128 * (hp + 1)] = d_raw.astype(BF16)
                dgn_ref[...] += dgn
        dp_ref[:, 3 * W:3 * W + 256] = dcq_ref[...].astype(BF16)
        dp_ref[:, 3 * W + 256:3 * W + 384] = dckv_ref[...].astype(BF16)
        dp_ref[:, 3 * W + 384:N] = dkpe_ref[...].astype(BF16)
        dh = _dot_nt(dp_ref[...], w_ref[...])
        xv = x_ref[...]
        dxn, dgc = _rms_bwd(dh, xv, g_ref[...], _rstd(xv))
        dx_ref[...] = dxu_ref[...] + dxn
        dgain_ref[...] += jnp.sum(dgc, axis=0, keepdims=True)

    row = lambda i: (i, 0)
    fix = lambda i: (0, 0)
    return pl.pallas_call(
        body, name="in_proj_bwd", grid=(T // tm,),
        in_specs=[pl.BlockSpec((tm, D), row), pl.BlockSpec((tm, D), row), pl.BlockSpec((1, D), fix),
                  pl.BlockSpec((D, N), fix), pl.BlockSpec((tm, W), row), pl.BlockSpec((tm, W), lambda i: (i, 1)),
                  pl.BlockSpec((1, 128), fix), pl.BlockSpec((1, 128), fix)] + [pl.BlockSpec((tm, W), row)] * (3 * nb)
                 + [pl.BlockSpec((tm, 256), row), pl.BlockSpec((tm, 128), row), pl.BlockSpec((tm, 128), row)],
        out_specs=[pl.BlockSpec((tm, D), row), pl.BlockSpec((tm, N), row), pl.BlockSpec((1, D), fix),
                   pl.BlockSpec((1, 128), fix), pl.BlockSpec((1, 128), fix)],
        out_shape=[jax.ShapeDtypeStruct((T, D), F32), jax.ShapeDtypeStruct((T, N), BF16),
                   jax.ShapeDtypeStruct((1, D), F32), jax.ShapeDtypeStruct((1, 128), F32),
                   jax.ShapeDtypeStruct((1, 128), F32)],
        compiler_params=_params(1),
    )(dx_up, x, gain, w, proj, proj, gq, gk, *[a for triple in dqkv for a in triple], dcq, dckv, dkpe)


def _out_proj(x, o_dil, o_mla, g_dil, g_mla, w, tm=512):
    T, D = x.shape
    W = o_dil.shape[1]

    def body(x_ref, od_ref, om_ref, gd_ref, gm_ref, w_ref, xo_ref, oc_ref):
        od, om = od_ref[...], om_ref[...]
        oc_ref[:, 0:W] = (od * _rstd(od) * gd_ref[...]).astype(BF16)
        oc_ref[:, W:2 * W] = (om * _rstd(om) * gm_ref[...]).astype(BF16)
        xo_ref[...] = x_ref[...] + _dot(oc_ref[...], w_ref[...])

    row = lambda i: (i, 0)
    fix = lambda i: (0, 0)
    return pl.pallas_call(
        body, name="out_proj", grid=(T // tm,),
        in_specs=[pl.BlockSpec((tm, D), row), pl.BlockSpec((tm, W), row), pl.BlockSpec((tm, W), row),
                  pl.BlockSpec((1, W), fix), pl.BlockSpec((1, W), fix), pl.BlockSpec((2 * W, D), fix)],
        out_specs=[pl.BlockSpec((tm, D), row), pl.BlockSpec((tm, 2 * W), row)],
        out_shape=[jax.ShapeDtypeStruct((T, D), F32), jax.ShapeDtypeStruct((T, 2 * W), BF16)],
        compiler_params=_params(1),
    )(x, o_dil, o_mla, g_dil, g_mla, w)


def _out_proj_bwd(dx, o_dil, o_mla, g_dil, g_mla, w, tm=512):
    T, D = dx.shape
    W = o_dil.shape[1]

    def body(dx_ref, od_ref, om_ref, gd_ref, gm_ref, w_ref, dod_ref, dom_ref, dgd_ref, dgm_ref):
        @pl.when(pl.program_id(0) == 0)
        def _():
            dgd_ref[...] = jnp.zeros_like(dgd_ref)
            dgm_ref[...] = jnp.zeros_like(dgm_ref)

        doc = _dot_nt(dx_ref[...].astype(BF16), w_ref[...])
        od, om = od_ref[...], om_ref[...]
        dod, dgd = _rms_bwd(doc[:, 0:W], od, gd_ref[...], _rstd(od))
        dom, dgm = _rms_bwd(doc[:, W:2 * W], om, gm_ref[...], _rstd(om))
        dod_ref[...] = dod
        dom_ref[...] = dom
        dgd_ref[...] += jnp.sum(dgd, axis=0, keepdims=True)
        dgm_ref[...] += jnp.sum(dgm, axis=0, keepdims=True)

    row = lambda i: (i, 0)
    fix = lambda i: (0, 0)
    return pl.pallas_call(
        body, name="out_proj_bwd", grid=(T // tm,),
        in_specs=[pl.BlockSpec((tm, D), row), pl.BlockSpec((tm, W), row), pl.BlockSpec((tm, W), row),
                  pl.BlockSpec((1, W), fix), pl.BlockSpec((1, W), fix), pl.BlockSpec((2 * W, D), fix)],
        out_specs=[pl.BlockSpec((tm, W), row), pl.BlockSpec((tm, W), row),
                   pl.BlockSpec((1, W), fix), pl.BlockSpec((1, W), fix)],
        out_shape=[jax.ShapeDtypeStruct((T, W), F32), jax.ShapeDtypeStruct((T, W), F32),
                   jax.ShapeDtypeStruct((1, W), F32), jax.ShapeDtypeStruct((1, W), F32)],
        compiler_params=_params(1),
    )(dx, o_dil, o_mla, g_dil, g_mla, w)


def _pair_rstd(x, lo):
    sq = x * x
    s0 = jnp.sum(jnp.where(lo, sq, 0.0), axis=-1, keepdims=True)
    s1 = jnp.sum(jnp.where(lo, 0.0, sq), axis=-1, keepdims=True)
    return jnp.where(lo, lax.rsqrt(s0 / DIL_HD + EPS), lax.rsqrt(s1 / DIL_HD + EPS))


def _pair_rms_bwd(dn, x, r, g, lo):
    u = dn * g
    t = u * x
    d0 = jnp.sum(jnp.where(lo, t, 0.0), axis=-1, keepdims=True)
    d1 = jnp.sum(jnp.where(lo, 0.0, t), axis=-1, keepdims=True)
    dx = r * u - x * (r * r * r) * (jnp.where(lo, d0, d1) / DIL_HD)
    return dx, jnp.sum(dn * x * r, axis=0, keepdims=True)


def _pair_col(x, lo, e):
    sel = lo if e == 0 else jnp.logical_not(lo)
    return jnp.max(jnp.where(sel, x, NEG), axis=-1, keepdims=True)


def _dil_masks(n):
    row = lax.broadcasted_iota(jnp.int32, (DIL_BLOCK, DIL_BLOCK), 0)
    col = lax.broadcasted_iota(jnp.int32, (DIL_BLOCK, DIL_BLOCK), 1)
    return col < DIL_HD, jnp.logical_and(col >= row, n > 0), col <= row


def _dil_pairs(d):
    return 4 if d == 1 else 1


def _sub_rows(r, d):
    return pl.ds(r, DIL_BLOCK, stride=d) if d > 1 else pl.ds(0, DIL_BLOCK)


def _split_subsequences(pairs, d, P):
    for r in range(d):
        for p in range(P):
            for block, scratch in pairs:
                scratch[r * P + p] = block[_sub_rows(r, d), pl.ds(128 * p, 128)]


def _merge_subsequences(pairs, d, P):
    for r in range(d):
        for p in range(P):
            for block, scratch in pairs:
                block[_sub_rows(r, d), pl.ds(128 * p, 128)] = scratch[r * P + p]


def _dil_fwd(qh, kh, proj, bias_p, bias_c, d, prev):
    T = proj.shape[0]
    P = _dil_pairs(d)
    rows, cw, n_it = DIL_BLOCK * d, 128 * P, d * P
    nblk = T // rows
    has_prev = prev is not None

    def body(*refs):
        q_ref, kp_ref, kc_ref, vp_ref, vc_ref, bp_ref, bc_ref = refs[:7]
        refs = refs[7:]
        if has_prev:
            oin_ref, lin_ref = refs[:2]
            refs = refs[2:]
        o_ref, l_ref, qs, kps, kcs, vps, vcs, os_, ls_ = refs[:9]
        pb, n = pl.program_id(0), pl.program_id(1)
        lo, mask_p, mask_c = _dil_masks(n)
        loads = [(q_ref, qs), (kp_ref, kps), (kc_ref, kcs), (vp_ref, vps), (vc_ref, vcs)]
        if has_prev:
            ois, lis = refs[9:]
            loads += [(oin_ref, ois), (lin_ref, lis)]
        _split_subsequences(loads, d, P)

        def step(i, carry):
            h0 = 2 * (pb * P + i % P)
            qn = qs[i]
            kpn, kcn = kps[i].astype(BF16), kcs[i].astype(BF16)
            vp, vc = vps[i].astype(BF16), vcs[i].astype(BF16)
            o_e, l_e = [], []
            for e in range(2):
                sel = lo if e == 0 else jnp.logical_not(lo)
                qe = jnp.where(sel, qn, 0.0).astype(BF16)
                sp = jnp.where(mask_p, _dot_nt(qe, kpn) + bp_ref[h0 + e], NEG)
                sc = jnp.where(mask_c, _dot_nt(qe, kcn) + bc_ref[h0 + e], NEG)
                m = jnp.maximum(jnp.max(sp, axis=-1, keepdims=True), jnp.max(sc, axis=-1, keepdims=True))
                pp, pc = jnp.exp(sp - m), jnp.exp(sc - m)
                l = jnp.sum(pp, axis=-1, keepdims=True) + jnp.sum(pc, axis=-1, keepdims=True)
                o_e.append((_dot(pp.astype(BF16), vp) + _dot(pc.astype(BF16), vc)) / l)
                l_e.append(m + jnp.log(l))
            o = jnp.where(lo, o_e[0], o_e[1])
            lse = jnp.where(lo, l_e[0], l_e[1])
            if has_prev:
                lin = lis[i]
                mx = jnp.maximum(lin, lse)
                lnew = mx + jnp.log(jnp.exp(lin - mx) + jnp.exp(lse - mx))
                o = ois[i] * jnp.exp(lin - lnew) + o * jnp.exp(lse - lnew)
                lse = lnew
            os_[i] = o
            ls_[i] = lse
            return carry

        lax.fori_loop(0, n_it, step, 0, unroll=4)
        _merge_subsequences([(o_ref, os_), (l_ref, ls_)], d, P)

    blk = (rows, cw)
    vcol = 2 * DIL_WIDTH // cw
    prev_n = lambda n: jnp.maximum(n - 1, 0)
    fix3 = lambda pb, n: (0, 0, 0)
    tok = pl.BlockSpec(blk, lambda pb, n: (n, pb))
    tok_prev = pl.BlockSpec(blk, lambda pb, n: (prev_n(n), pb))
    bias_spec = pl.BlockSpec((DIL_HEADS, DIL_BLOCK, DIL_BLOCK), fix3)
    in_specs = [tok, tok_prev, tok,
                pl.BlockSpec(blk, lambda pb, n: (prev_n(n), vcol + pb)), pl.BlockSpec(blk, lambda pb, n: (n, vcol + pb)),
                bias_spec, bias_spec]
    args = [qh, kh, kh, proj, proj, bias_p, bias_c]
    n_scratch = 7
    if has_prev:
        in_specs += [tok, tok]
        args += list(prev)
        n_scratch += 2
    out = jax.ShapeDtypeStruct((T, DIL_WIDTH), F32)
    return pl.pallas_call(
        body, name=f"dil_fwd_d{d}", grid=(DIL_HEADS // 2 // P, nblk), in_specs=in_specs, out_specs=[tok, tok],
        out_shape=[out, out],
        scratch_shapes=[pltpu.VMEM((n_it, DIL_BLOCK, 128), F32)] * n_scratch,
        compiler_params=_params(2),
    )(*args)


def _dil_bwd(qh, kh, proj, o, lse, do, bias_p, bias_c, d):
    T = proj.shape[0]
    P = _dil_pairs(d)
    rows, cw, n_it = DIL_BLOCK * d, 128 * P, d * P
    nblk = T // rows

    def body(q_ref, kp_ref, kc_ref, vp_ref, vc_ref, o_ref, l_ref, do_ref, bp_ref, bc_ref,
             dq_ref, dk_ref, dv_ref, dbp_ref, dbc_ref,
             qs, kps, kcs, vps, vcs, os_, ls_, dos, dqs, dks, dvs, ck, cv):
        pb, n = pl.program_id(0), pl.program_id(1)
        lo, mask_p, mask_c = _dil_masks(n)

        @pl.when((pb == 0) & (n == 0))
        def _():
            dbp_ref[...] = jnp.zeros_like(dbp_ref)
            dbc_ref[...] = jnp.zeros_like(dbc_ref)

        @pl.when(n == 0)
        def _():
            ck[...] = jnp.zeros_like(ck)
            cv[...] = jnp.zeros_like(cv)

        _split_subsequences([(q_ref, qs), (kp_ref, kps), (kc_ref, kcs), (vp_ref, vps), (vc_ref, vcs),
                             (o_ref, os_), (l_ref, ls_), (do_ref, dos)], d, P)

        def step(i, carry):
            h0 = 2 * (pb * P + i % P)
            qn = qs[i]
            qb = qn.astype(BF16)
            kpn, kcn = kps[i].astype(BF16), kcs[i].astype(BF16)
            vp, vc = vps[i].astype(BF16), vcs[i].astype(BF16)
            dov = dos[i]
            dob = dov.astype(BF16)
            dot_o = dov * os_[i]
            lse_pair = ls_[i]
            res = []
            for e in range(2):
                sel = lo if e == 0 else jnp.logical_not(lo)
                h = h0 + e
                qe = jnp.where(sel, qn, 0.0).astype(BF16)
                doe = jnp.where(sel, dov, 0.0).astype(BF16)
                delta = jnp.sum(jnp.where(sel, dot_o, 0.0), axis=-1, keepdims=True)
                lse_e = _pair_col(lse_pair, lo, e)
                sp = jnp.where(mask_p, _dot_nt(qe, kpn) + bp_ref[h], NEG)
                sc = jnp.where(mask_c, _dot_nt(qe, kcn) + bc_ref[h], NEG)
                pp, pc = jnp.exp(sp - lse_e), jnp.exp(sc - lse_e)
                dsp = pp * (_dot_nt(doe, vp) - delta)
                dsc = pc * (_dot_nt(doe, vc) - delta)
                dbp_ref[h] += dsp
                dbc_ref[h] += dsc
                dspb, dscb = dsp.astype(BF16), dsc.astype(BF16)
                res.append((_dot(dspb, kpn) + _dot(dscb, kcn), _dot_tn(dspb, qb), _dot_tn(dscb, qb),
                            _dot_tn(pp.astype(BF16), dob), _dot_tn(pc.astype(BF16), dob)))
            dqn, dkn_p, dkn_c, dv_p, dv_c = (jnp.where(lo, a, b) for a, b in zip(res[0], res[1]))
            dqs[i] = dqn
            dks[i] = ck[i] + dkn_p
            dvs[i] = cv[i] + dv_p
            ck[i] = dkn_c
            cv[i] = dv_c
            return carry

        @pl.when(n < nblk)
        def _():
            lax.fori_loop(0, n_it, step, 0, unroll=2)
            _merge_subsequences([(dq_ref, dqs), (dk_ref, dks), (dv_ref, dvs)], d, P)

        @pl.when(n == nblk)
        def _():
            _merge_subsequences([(dk_ref, ck), (dv_ref, cv)], d, P)

    blk = (rows, cw)
    vcol = 2 * DIL_WIDTH // cw
    qn_ = lambda n: jnp.minimum(n, nblk - 1)
    pn_ = lambda n: jnp.maximum(n - 1, 0)
    fix3 = lambda pb, n: (0, 0, 0)
    tok_q = pl.BlockSpec(blk, lambda pb, n: (qn_(n), pb))
    tok_p = pl.BlockSpec(blk, lambda pb, n: (pn_(n), pb))
    bias_spec = pl.BlockSpec((DIL_HEADS, DIL_BLOCK, DIL_BLOCK), fix3)
    in_specs = [tok_q, tok_p, tok_q,
                pl.BlockSpec(blk, lambda pb, n: (pn_(n), vcol + pb)), pl.BlockSpec(blk, lambda pb, n: (qn_(n), vcol + pb)),
                tok_q, tok_q, tok_q, bias_spec, bias_spec]
    tok_shape = jax.ShapeDtypeStruct((T, DIL_WIDTH), F32)
    bias_shape = jax.ShapeDtypeStruct((DIL_HEADS, DIL_BLOCK, DIL_BLOCK), F32)
    dq, dk, dv, dbp, dbc = pl.pallas_call(
        body, name=f"dil_bwd_d{d}", grid=(DIL_HEADS // 2 // P, nblk + 1), in_specs=in_specs,
        out_specs=[tok_q, tok_p, tok_p, bias_spec, bias_spec],
        out_shape=[tok_shape, tok_shape, tok_shape, bias_shape, bias_shape],
        scratch_shapes=[pltpu.VMEM((n_it, DIL_BLOCK, 128), F32)] * 13,
        compiler_params=_params(2),
    )(qh, kh, kh, proj, proj, o, lse, do, bias_p, bias_c)
    return (dq, dk, dv), dbp, dbc


def _t5_bucket(dist):
    max_exact = REL_BUCKETS // 2
    dd = np.maximum(dist, 1).astype(np.float32)
    large = max_exact + (np.log(dd / max_exact) / np.log(REL_MAX_DIST / max_exact)
                         * (REL_BUCKETS - max_exact)).astype(np.int32)
    large = np.minimum(large, REL_BUCKETS - 1)
    return np.where(dist < max_exact, dist, large).astype(np.int32)


def _bucket_onehots():
    i = np.arange(DIL_BLOCK)[:, None]
    j = np.arange(DIL_BLOCK)[None, :]
    out = []
    for _, d in DIL_BRANCHES:
        for dist in (DIL_BLOCK + i - j, i - j):
            bucket = _t5_bucket(np.clip(dist, 0, None) * d).reshape(-1)
            out.append(jnp.asarray(np.eye(REL_BUCKETS, dtype=np.float32)[:, bucket], BF16))
    return out


def _bias_tables(rel_bias, onehots):
    n = len(onehots)

    def body(rb_ref, *refs):
        parts = _split3(rb_ref[...])
        for k in range(n):
            oh = refs[k][...]
            refs[n + k][...] = _dot(parts[0], oh) + _dot(parts[1], oh) + _dot(parts[2], oh)

    return pl.pallas_call(
        body, name="bias_tables",
        out_shape=[jax.ShapeDtypeStruct((DIL_HEADS, DIL_BLOCK * DIL_BLOCK), F32)] * n,
        compiler_params=pltpu.CompilerParams(vmem_limit_bytes=VMEM_LIMIT),
    )(rel_bias, *onehots)


def _bias_grad(dbs, onehots):
    n = len(dbs)

    def body(*refs):
        acc = jnp.zeros((DIL_HEADS, REL_BUCKETS), F32)
        for k in range(n):
            oh = refs[n + k][...]
            for part in _split3(refs[k][...]):
                acc = acc + _dot_nt(part, oh)
        refs[-1][...] = acc

    return pl.pallas_call(
        body, name="bias_grad",
        out_shape=jax.ShapeDtypeStruct((DIL_HEADS, REL_BUCKETS), F32),
        compiler_params=pltpu.CompilerParams(vmem_limit_bytes=VMEM_LIMIT),
    )(*dbs, *onehots)


def _swap_halves(x):
    lane = lax.broadcasted_iota(jnp.int32, x.shape, 1)
    first = (lane % 64) < 32
    return jnp.where(first, pltpu.roll(x, 96, 1), pltpu.roll(x, 32, 1))


def _rope_tables(T):
    pos = jnp.arange(T, dtype=F32)
    inv_freq = ROPE_BASE ** (-jnp.arange(0, MLA_ROPE, 2, dtype=F32) / MLA_ROPE)
    ang = pos[:, None] * inv_freq[None, :]
    z = jnp.zeros((T, 128 - MLA_ROPE), F32)
    cos = jnp.concatenate([jnp.cos(ang), jnp.cos(ang), z], axis=-1)
    sin = jnp.concatenate([-jnp.sin(ang), jnp.sin(ang), z], axis=-1)
    return cos, sin


def _mla_prep(proj, cos, sin, g_qa, g_kva, g_q, g_k, wq, wkv, tm=512):
    T = proj.shape[0]
    H = MLA_HEADS
    scale = MLA_QK ** -0.5

    def body(cq_ref, ckv_ref, kpe_ref, cos_ref, sin_ref, gqa_ref, gkva_ref, gq_ref, gk_ref, wq_ref, wkv_ref,
             q_ref, k_ref, v_ref):
        cosv, sinv = cos_ref[...], sin_ref[...]

        def rope(x):
            return x * cosv + _swap_halves(x) * sinv

        cq = cq_ref[...]
        qp = _dot((cq * _rstd(cq) * gqa_ref[...]).astype(BF16), wq_ref[...])
        ckv = ckv_ref[...]
        kvp = _dot((ckv * _rstd(ckv) * gkva_ref[...]).astype(BF16), wkv_ref[...])
        kpe = kpe_ref[...]
        for h in range(H):
            a = qp[:, MLA_PAD * h:MLA_PAD * (h + 1)]
            qn = a * _rstd(a, MLA_QK) * gq_ref[...]
            q_ref[h, :, 0:128] = (qn[:, 0:128] * scale).astype(BF16)
            q_ref[h, :, 128:256] = (rope(qn[:, 128:256]) * scale).astype(BF16)
            kn = kvp[:, MLA_PAD * h:MLA_PAD * h + 128]
            r = lax.rsqrt((jnp.sum(kn * kn, axis=-1, keepdims=True)
                           + jnp.sum(kpe * kpe, axis=-1, keepdims=True)) / MLA_QK + EPS)
            k_ref[h, :, 0:128] = (kn * r * gk_ref[:, 0:128]).astype(BF16)
            k_ref[h, :, 128:256] = rope(kpe * r * gk_ref[:, 128:256]).astype(BF16)
            v_ref[h] = kvp[:, MLA_PAD * h + 128:MLA_PAD * (h + 1)].astype(BF16)

    fix = lambda i: (0, 0)
    return pl.pallas_call(
        body, name="mla_prep", grid=(T // tm,),
        in_specs=[pl.BlockSpec((tm, 256), lambda i: (i, 6)), pl.BlockSpec((tm, 128), lambda i: (i, 14)),
                  pl.BlockSpec((tm, 128), lambda i: (i, 15)),
                  pl.BlockSpec((tm, 128), lambda i: (i, 0)), pl.BlockSpec((tm, 128), lambda i: (i, 0)),
                  pl.BlockSpec((1, 256), fix), pl.BlockSpec((1, 128), fix),
                  pl.BlockSpec((1, 256), fix), pl.BlockSpec((1, 256), fix),
                  pl.BlockSpec((256, H * MLA_PAD), fix), pl.BlockSpec((128, H * MLA_PAD), fix)],
        out_specs=[pl.BlockSpec((H, tm, MLA_PAD), lambda i: (0, i, 0)), pl.BlockSpec((H, tm, MLA_PAD), lambda i: (0, i, 0)),
                   pl.BlockSpec((H, tm, MLA_V), lambda i: (0, i, 0))],
        out_shape=[jax.ShapeDtypeStruct((H, T, MLA_PAD), BF16), jax.ShapeDtypeStruct((H, T, MLA_PAD), BF16),
                   jax.ShapeDtypeStruct((H, T, MLA_V), BF16)],
        compiler_params=_params(1),
    )(proj, proj, proj, cos, sin, g_qa, g_kva, g_q, g_k, wq, wkv)


def _mla_prep_bwd(proj, cos, sin, g_qa, g_kva, g_q, g_k, wq, wkv, dq, dk, dv, tm=512):
    T = proj.shape[0]
    H = MLA_HEADS
    scale = MLA_QK ** -0.5

    def body(cq_ref, ckv_ref, kpe_ref, cos_ref, sin_ref, gqa_ref, gkva_ref, gq_ref, gk_ref, wq_ref, wkv_ref,
             dq_ref, dk_ref, dv_ref,
             dcq_ref, dckv_ref, dkpe_ref, cqn_ref, ckvn_ref, dqp_ref, dkvp_ref,
             dgqa_ref, dgkva_ref, dgq_ref, dgk_ref):
        @pl.when(pl.program_id(0) == 0)
        def _():
            for ref in (dgqa_ref, dgkva_ref, dgq_ref, dgk_ref):
                ref[...] = jnp.zeros_like(ref)

        cosv, sinv = cos_ref[...], sin_ref[...]

        def rope_bwd(dy):
            return dy * cosv + _swap_halves(dy * sinv)

        cq = cq_ref[...]
        rcq = _rstd(cq)
        cqn = (cq * rcq * gqa_ref[...]).astype(BF16)
        cqn_ref[...] = cqn
        qp = _dot(cqn, wq_ref[...])
        ckv = ckv_ref[...]
        rckv = _rstd(ckv)
        ckvn = (ckv * rckv * gkva_ref[...]).astype(BF16)
        ckvn_ref[...] = ckvn
        kvp = _dot(ckvn, wkv_ref[...])
        kpe = kpe_ref[...]
        dkpe = jnp.zeros_like(kpe)
        dgq = jnp.zeros((1, MLA_PAD), F32)
        dgk = jnp.zeros((1, MLA_PAD), F32)
        for h in range(H):
            a = qp[:, MLA_PAD * h:MLA_PAD * (h + 1)]
            dqh = dq_ref[h]
            dn = jnp.concatenate([dqh[:, 0:128], rope_bwd(dqh[:, 128:256])], axis=-1) * scale
            da, dg = _rms_bwd(dn, a, gq_ref[...], _rstd(a, MLA_QK), MLA_QK)
            dgq = dgq + jnp.sum(dg, axis=0, keepdims=True)
            dqp_ref[:, MLA_PAD * h:MLA_PAD * (h + 1)] = da.astype(BF16)

            ak = jnp.concatenate([kvp[:, MLA_PAD * h:MLA_PAD * h + 128], kpe], axis=-1)
            dkh = dk_ref[h]
            dnk = jnp.concatenate([dkh[:, 0:128], rope_bwd(dkh[:, 128:256])], axis=-1)
            dak, dg = _rms_bwd(dnk, ak, gk_ref[...], _rstd(ak, MLA_QK), MLA_QK)
            dgk = dgk + jnp.sum(dg, axis=0, keepdims=True)
            dkpe = dkpe + dak[:, 128:256]
            dkvp_ref[:, MLA_PAD * h:MLA_PAD * h + 128] = dak[:, 0:128].astype(BF16)
            dkvp_ref[:, MLA_PAD * h + 128:MLA_PAD * (h + 1)] = dv_ref[h].astype(BF16)
        dkpe_ref[...] = dkpe
        dgq_ref[...] += dgq
        dgk_ref[...] += dgk
        dcq, dg = _rms_bwd(_dot_nt(dqp_ref[...], wq_ref[...]), cq, gqa_ref[...], rcq)
        dcq_ref[...] = dcq
        dgqa_ref[...] += jnp.sum(dg, axis=0, keepdims=True)
        dckv, dg = _rms_bwd(_dot_nt(dkvp_ref[...], wkv_ref[...]), ckv, gkva_ref[...], rckv)
        dckv_ref[...] = dckv
        dgkva_ref[...] += jnp.sum(dg, axis=0, keepdims=True)

    fix = lambda i: (0, 0)
    row = lambda i: (i, 0)
    head = lambda i: (0, i, 0)
    return pl.pallas_call(
        body, name="mla_prep_bwd", grid=(T // tm,),
        in_specs=[pl.BlockSpec((tm, 256), lambda i: (i, 6)), pl.BlockSpec((tm, 128), lambda i: (i, 14)),
                  pl.BlockSpec((tm, 128), lambda i: (i, 15)),
                  pl.BlockSpec((tm, 128), row), pl.BlockSpec((tm, 128), row),
                  pl.BlockSpec((1, 256), fix), pl.BlockSpec((1, 128), fix),
                  pl.BlockSpec((1, 256), fix), pl.BlockSpec((1, 256), fix),
                  pl.BlockSpec((256, H * MLA_PAD), fix), pl.BlockSpec((128, H * MLA_PAD), fix),
                  pl.BlockSpec((H, tm, MLA_PAD), head), pl.BlockSpec((H, tm, MLA_PAD), head),
                  pl.BlockSpec((H, tm, MLA_V), head)],
        out_specs=[pl.BlockSpec((tm, 256), row), pl.BlockSpec((tm, 128), row), pl.BlockSpec((tm, 128), row),
                   pl.BlockSpec((tm, 256), row), pl.BlockSpec((tm, 128), row),
                   pl.BlockSpec((tm, H * MLA_PAD), row), pl.BlockSpec((tm, H * MLA_PAD), row),
                   pl.BlockSpec((1, 256), fix), pl.BlockSpec((1, 128), fix),
                   pl.BlockSpec((1, 256), fix), pl.BlockSpec((1, 256), fix)],
        out_shape=[jax.ShapeDtypeStruct((T, 256), F32), jax.ShapeDtypeStruct((T, 128), F32),
                   jax.ShapeDtypeStruct((T, 128), F32),
                   jax.ShapeDtypeStruct((T, 256), BF16), jax.ShapeDtypeStruct((T, 128), BF16),
                   jax.ShapeDtypeStruct((T, H * MLA_PAD), BF16), jax.ShapeDtypeStruct((T, H * MLA_PAD), BF16),
                   jax.ShapeDtypeStruct((1, 256), F32), jax.ShapeDtypeStruct((1, 128), F32),
                   jax.ShapeDtypeStruct((1, 256), F32), jax.ShapeDtypeStruct((1, 256), F32)],
        compiler_params=_params(1),
    )(proj, proj, proj, cos, sin, g_qa, g_kva, g_q, g_k, wq, wkv, dq, dk, dv)


def _causal_pairs(T, tq, tk, key_major):
    pairs = [(i, j) for i in range(T // tq) for j in range(T // tk) if j * tk <= i * tq + tq - 1]
    if key_major:
        pairs.sort(key=lambda p: (p[1], p[0]))
    outer = [p[1] if key_major else p[0] for p in pairs]
    first = [int(t == 0 or outer[t] != outer[t - 1]) for t in range(len(pairs))]
    last = [int(t == len(pairs) - 1 or outer[t] != outer[t + 1]) for t in range(len(pairs))]
    tab = lambda v: jnp.asarray(np.array(v, np.int32))
    return tab([p[0] for p in pairs]), tab([p[1] for p in pairs]), tab(first), tab(last)


def _causal_scores(qv, kv, qi, ki, row0, tq, tk, masked):
    s = _dot_nt(qv, kv)
    if masked:
        row = lax.broadcasted_iota(jnp.int32, s.shape, 0) + (qi * tq + row0)
        col = lax.broadcasted_iota(jnp.int32, s.shape, 1) + ki * tk
        s = jnp.where(col <= row, s, NEG)
    return s


def _mla_attn(q, k, v, ride=None, tq=512, tk=1024, rc=256):
    H, T, _ = q.shape
    tables = _causal_pairs(T, tq, tk, key_major=False)
    n_pairs = int(tables[0].shape[0])
    r_args, r_in, r_shape, r_out, r_scratch = _ride_parts(ride)

    def body(qt, kt, ft, lt, q_ref, k_ref, v_ref, o_ref, lse_ref, m_s, l_s, acc):
        t = pl.program_id(1)
        qi, ki = qt[t], kt[t]

        @pl.when(ft[t] == 1)
        def _():
            m_s[...] = jnp.full_like(m_s, NEG)
            l_s[...] = jnp.zeros_like(l_s)
            acc[...] = jnp.zeros_like(acc)

        def update(masked):
            kk, vv = k_ref[...], v_ref[...]
            for c in range(tq // rc):
                rows = pl.ds(c * rc, rc)
                s = _causal_scores(q_ref[rows, :], kk, qi, ki, c * rc, tq, tk, masked)
                m_old = m_s[rows, :]
                m_new = jnp.maximum(m_old, jnp.max(s, axis=-1, keepdims=True))
                alpha = jnp.exp(m_old - m_new)
                p = jnp.exp(s - m_new)
                l_s[rows, :] = alpha * l_s[rows, :] + jnp.sum(p, axis=-1, keepdims=True)
                acc[rows, :] = alpha * acc[rows, :] + _dot(p.astype(BF16), vv)
                m_s[rows, :] = m_new

        diagonal = (ki + 1) * tk - 1 > qi * tq

        @pl.when(diagonal)
        def _():
            update(True)

        @pl.when(jnp.logical_not(diagonal))
        def _():
            update(False)

        @pl.when(lt[t] == 1)
        def _():
            o_ref[...] = acc[...] / l_s[...]
            lse_ref[...] = jnp.broadcast_to(m_s[...] + jnp.log(l_s[...]), lse_ref.shape)

    qrow = lambda h, t, qt, kt, ft, lt: (h, qt[t], 0)
    krow = lambda h, t, qt, kt, ft, lt: (h, kt[t], 0)
    first = lambda: (pl.program_id(0) == 0) & (pl.program_id(1) == 0)
    last = lambda: (pl.program_id(0) == H - 1) & (pl.program_id(1) == n_pairs - 1)
    outs = pl.pallas_call(
        _riding(body, 7, 2, 3, ride, first, last), name="mla_attn",
        grid_spec=pltpu.PrefetchScalarGridSpec(
            num_scalar_prefetch=4, grid=(H, n_pairs),
            in_specs=[pl.BlockSpec((None, tq, MLA_PAD), qrow), pl.BlockSpec((None, tk, MLA_PAD), krow),
                      pl.BlockSpec((None, tk, MLA_V), krow)] + r_in,
            out_specs=[pl.BlockSpec((tq, MLA_V), lambda h, t, qt, kt, ft, lt: (qt[t], h)),
                       pl.BlockSpec((None, tq, 128), qrow)] + r_out,
            scratch_shapes=[pltpu.VMEM((tq, 1), F32), pltpu.VMEM((tq, 1), F32), pltpu.VMEM((tq, MLA_V), F32)]
            + r_scratch),
        out_shape=[jax.ShapeDtypeStruct((T, H * MLA_V), F32), jax.ShapeDtypeStruct((H, T, 128), F32)] + r_shape,
        compiler_params=_params(2),
    )(*tables, q, k, v, *r_args)
    return outs[:2], outs[2:]


def _mla_attn_bwd(q, k, v, o, lse, do, ride=None, tq=512, tk=1024, rc=512):
    H, T, _ = q.shape
    tables = _causal_pairs(T, tq, tk, key_major=True)
    n_pairs = int(tables[0].shape[0])
    r_args, r_in, r_shape, r_out, r_scratch = _ride_parts(ride)

    def body(qt, kt, ft, lt, q_ref, k_ref, v_ref, o_ref, lse_ref, do_ref, dq_ref, dk_ref, dv_ref, dk_s, dv_s):
        t = pl.program_id(1)
        qi, ki = qt[t], kt[t]

        @pl.when(t == 0)
        def _():
            dq_ref[...] = jnp.zeros_like(dq_ref)

        @pl.when(ft[t] == 1)
        def _():
            dk_s[...] = jnp.zeros_like(dk_s)
            dv_s[...] = jnp.zeros_like(dv_s)

        def update(masked):
            kk, vv = k_ref[...], v_ref[...]
            for c in range(tq // rc):
                rows = pl.ds(c * rc, rc)
                qv, dov = q_ref[rows, :], do_ref[rows, :]
                delta = jnp.sum(dov * o_ref[rows, :], axis=-1, keepdims=True)
                lse_v = jnp.max(lse_ref[rows, :], axis=-1, keepdims=True)
                p = jnp.exp(_causal_scores(qv, kk, qi, ki, c * rc, tq, tk, masked) - lse_v)
                dob = dov.astype(BF16)
                dv_s[...] += _dot_tn(p.astype(BF16), dob)
                ds = (p * (_dot_nt(dob, vv) - delta)).astype(BF16)
                dk_s[...] += _dot_tn(ds, qv)
                out_rows = pl.ds(pl.multiple_of(qi * tq + c * rc, rc), rc)
                dq_ref[out_rows, :] += _dot(ds, kk)

        diagonal = (ki + 1) * tk - 1 > qi * tq

        @pl.when(diagonal)
        def _():
            update(True)

        @pl.when(jnp.logical_not(diagonal))
        def _():
            update(False)

        @pl.when(lt[t] == 1)
        def _():
            dk_ref[...] = dk_s[...]
            dv_ref[...] = dv_s[...]

    qrow = lambda h, t, qt, kt, ft, lt: (h, qt[t], 0)
    krow = lambda h, t, qt, kt, ft, lt: (h, kt[t], 0)
    qcol = lambda h, t, qt, kt, ft, lt: (qt[t], h)
    first = lambda: (pl.program_id(0) == 0) & (pl.program_id(1) == 0)
    last = lambda: (pl.program_id(0) == H - 1) & (pl.program_id(1) == n_pairs - 1)
    outs = pl.pallas_call(
        _riding(body, 10, 3, 2, ride, first, last), name="mla_attn_bwd",
        grid_spec=pltpu.PrefetchScalarGridSpec(
            num_scalar_prefetch=4, grid=(H, n_pairs),
            in_specs=[pl.BlockSpec((None, tq, MLA_PAD), qrow), pl.BlockSpec((None, tk, MLA_PAD), krow),
                      pl.BlockSpec((None, tk, MLA_V), krow), pl.BlockSpec((tq, MLA_V), qcol),
                      pl.BlockSpec((None, tq, 128), qrow), pl.BlockSpec((tq, MLA_V), qcol)] + r_in,
            out_specs=[pl.BlockSpec((None, T, MLA_PAD), lambda h, t, qt, kt, ft, lt: (h, 0, 0)),
                       pl.BlockSpec((None, tk, MLA_PAD), krow), pl.BlockSpec((None, tk, MLA_V), krow)] + r_out,
            scratch_shapes=[pltpu.VMEM((tk, MLA_PAD), F32), pltpu.VMEM((tk, MLA_V), F32)] + r_scratch),
        out_shape=[jax.ShapeDtypeStruct((H, T, MLA_PAD), F32), jax.ShapeDtypeStruct((H, T, MLA_PAD), F32),
                   jax.ShapeDtypeStruct((H, T, MLA_V), F32)] + r_shape,
        compiler_params=_params(2),
    )(*tables, q, k, v, o, lse, do, *r_args)
    return outs[:3], outs[3:]


def _pair_gain(g):
    return jnp.tile(g.reshape(1, DIL_HD), (1, 2))


def _pad_gain(g):
    return jnp.pad(g.reshape(1, MLA_QK), ((0, 0), (0, MLA_PAD - MLA_QK)))


def _local_step(x, target, s, comm):
    T = x.shape[0]
    w = comm.w
    gq, gk = _pair_gain(s["dil_q_norm"]) * DIL_HD ** -0.5, _pair_gain(s["dil_k_norm"])
    g_q, g_k = _pad_gain(s["mla_q_norm"]), _pad_gain(s["mla_k_norm"])
    cos, sin = _rope_tables(T)
    onehots = _bucket_onehots()
    tables = [t.reshape(DIL_HEADS, DIL_BLOCK, DIL_BLOCK) for t in _bias_tables(s["rel_bias"], onehots)]
    biases = list(zip(tables[0::2], tables[1::2]))

    (x1, h1, gate1, up1), got = _ffn_fwd(x, s["ffn1_norm"], w["ffn1_w_gate"], w["ffn1_w_up"], w["ffn1_w_down"],
                                         ride=comm.gather(_GROUPS["attn"]))
    comm.weights_landed(_GROUPS["attn"], got)
    hm, proj, qh, kh = _in_proj(x1, s["mix_norm"], w["w_in"], gq, gk)
    dil = None
    for (_, d), (bp, bc) in zip(DIL_BRANCHES, biases):
        dil = _dil_fwd(qh, kh, proj, bp, bc, d, dil)
    o_dil, lse_dil = dil
    q, k, v = _mla_prep(proj, cos, sin, s["mla_q_a_norm"], s["mla_kv_a_norm"], g_q, g_k, w["mla_w_q_b"], w["mla_w_kv_b"])
    (o_mla, lse_mla), got = _mla_attn(q, k, v, ride=comm.gather(_GROUPS["ffn2"]))
    comm.weights_landed(_GROUPS["ffn2"], got)
    x2, oc = _out_proj(x1, o_dil, o_mla, s["out_norm_dil"], s["out_norm_mla"], w["w_out"])
    (y, h2, gate2, up2), _ = _ffn_fwd(x2, s["ffn2_norm"], w["ffn2_w_gate"], w["ffn2_w_up"], w["ffn2_w_down"])
    dy, loss = _loss_grad(y, target)

    gw, gs = {}, {}

    def ffn_grads(name, dy_in, x_in, h, gate, up, ride=None, scatter_early=False):
        (dx, a, dg, du, dyh, dgain), got = _ffn_bwd(dy_in, x_in, s[name + "_norm"], gate, up,
                                                    w[name + "_w_gate"], w[name + "_w_up"], w[name + "_w_down"], ride=ride)
        gs[name + "_norm"] = dgain
        down, gate_n, up_n = (name + "_w_down",), (name + "_w_gate",), (name + "_w_up",)
        gw[down[0]], _ = _matmul_tn(a, dyh, 1408, 1024)
        gw[gate_n[0]], landed = _matmul_tn(h, dg, 1024, 1408, ride=comm.scatter(down, gw) if scatter_early else None)
        comm.grads_landed(down, landed)
        gw[up_n[0]], landed = _matmul_tn(h, du, 1024, 1408, ride=comm.scatter(gate_n, gw) if scatter_early else None)
        comm.grads_landed(gate_n, landed)
        return dx, got

    dx2, _ = ffn_grads("ffn2", dy, x2, h2, gate2, up2)
    gw["w_out"], _ = _matmul_tn(oc, dx2, 1024, 1024)
    do_dil, do_mla, gs["out_norm_dil"], gs["out_norm_mla"] = _out_proj_bwd(
        dx2, o_dil, o_mla, s["out_norm_dil"], s["out_norm_mla"], w["w_out"])

    (dq, dk, dv), got = _mla_attn_bwd(q, k, v, o_mla, lse_mla, do_mla, ride=comm.scatter(_GROUPS["ffn2"], gw))
    comm.grads_landed(_GROUPS["ffn2"], got)
    (dcq, dckv, dkpe, cqn, ckvn, dqp, dkvp, gs["mla_q_a_norm"], gs["mla_kv_a_norm"], dg_q, dg_k) = _mla_prep_bwd(
        proj, cos, sin, s["mla_q_a_norm"], s["mla_kv_a_norm"], g_q, g_k, w["mla_w_q_b"], w["mla_w_kv_b"], dq, dk, dv)
    gs["mla_q_norm"], gs["mla_k_norm"] = dg_q[:, :MLA_QK], dg_k[:, :MLA_QK]
    gw["mla_w_q_b"], _ = _matmul_tn(cqn, dqp, 256, 1024)
    gw["mla_w_kv_b"], _ = _matmul_tn(ckvn, dkvp, 128, 1024)

    dqkv, dbs = [], []
    for (_, d), (bp, bc) in zip(DIL_BRANCHES, biases):
        triple, dbp, dbc = _dil_bwd(qh, kh, proj, o_dil, lse_dil, do_dil, bp, bc, d)
        dqkv.append(triple)
        dbs += [dbp.reshape(DIL_HEADS, -1), dbc.reshape(DIL_HEADS, -1)]
    gs["rel_bias"] = _bias_grad(dbs, onehots)

    dx1, dproj, gs["mix_norm"], dgq, dgk = _in_proj_bwd(dx2, x1, s["mix_norm"], w["w_in"], proj, gq, gk,
                                                        dqkv, dcq, dckv, dkpe)
    gs["dil_q_norm"] = (dgq[:, :DIL_HD] + dgq[:, DIL_HD:]) * DIL_HD ** -0.5
    gs["dil_k_norm"] = dgk[:, :DIL_HD] + dgk[:, DIL_HD:]
    gw["w_in"], _ = _matmul_tn(hm, dproj, 1024, 1024)
    grad_x, got = ffn_grads("ffn1", dx1, x, h1, gate1, up1, ride=comm.scatter(_GROUPS["attn"], gw), scatter_early=True)
    comm.grads_landed(_GROUPS["attn"], got)
    return loss, grad_x, gw, gs


def _position():
    x, y, c = lax.axis_index("x"), lax.axis_index("y"), lax.axis_index("c")
    return x, y, c, 4 * x + 2 * y + c


def _peer(x, y, c, k):
    px = 1 - x if k & 4 else x
    py = 1 - y if k & 2 else y
    pc = 1 - c if k & 1 else c
    return (px, py, pc), 4 * px + 2 * py + pc


class _Ride:
    def __init__(self, arrays, scatter):
        self.arrays, self.scatter = list(arrays), list(scatter)
        self.n = n = len(self.arrays)
        self.specs = [pl.BlockSpec(memory_space=pl.ANY)] * n
        self.out_shape = [jax.ShapeDtypeStruct(a.shape if sc else (N_DEV,) + a.shape, a.dtype)
                          for a, sc in zip(self.arrays, self.scatter)]
        self.scratch = [pltpu.SemaphoreType.DMA((n, N_DEV - 1)), pltpu.SemaphoreType.DMA((n, N_DEV - 1)),
                        pltpu.SemaphoreType.DMA((n,))]

    def _copies(self, ins, outs, sems):
        send_sems, recv_sems, local_sems = sems
        x, y, c, me = _position()
        copies = []
        for a in range(self.n):
            src = ins[a].at[me] if self.scatter[a] else ins[a]
            copies.append(pltpu.make_async_copy(src, outs[a].at[me], local_sems.at[a]))
        for k in range(1, N_DEV):
            peer, peer_idx = _peer(x, y, c, k)
            for a in range(self.n):
                src = ins[a].at[peer_idx] if self.scatter[a] else ins[a]
                copies.append(pltpu.make_async_remote_copy(
                    src_ref=src, dst_ref=outs[a].at[me], send_sem=send_sems.at[a, k - 1], recv_sem=recv_sems.at[a, k - 1],
                    device_id=peer, device_id_type=pl.DeviceIdType.MESH))
        return copies

    def start(self, ins, outs, sems):
        for cp in self._copies(ins, outs, sems):
            cp.start()

    def wait(self, ins, outs, sems):
        for cp in self._copies(ins, outs, sems):
            cp.wait()


def _ride_parts(ride):
    if ride is None:
        return [], [], [], [], []
    return ride.arrays, ride.specs, ride.out_shape, ride.specs, ride.scratch


def _riding(body, n_in, n_out, n_scratch, ride, first, last):
    if ride is None:
        return body
    n = ride.n
    i1, i2 = n_in + n, n_in + n + n_out
    i3, i4 = i2 + n, i2 + n + n_scratch

    def wrapped(*refs):
        ins, outs, sems = refs[n_in:i1], refs[i2:i3], refs[i4:]

        @pl.when(first())
        def _():
            ride.start(ins, outs, sems)

        body(*refs[:n_in], *refs[i1:i2], *refs[i3:i4])

        @pl.when(last())
        def _():
            ride.wait(ins, outs, sems)

    return wrapped


def _exchange(ride, name):
    def body(*refs):
        parts = refs[:ride.n], refs[ride.n:2 * ride.n], refs[2 * ride.n:]
        ride.start(*parts)
        ride.wait(*parts)

    return pl.pallas_call(body, name=name, in_specs=ride.specs, out_specs=ride.specs, out_shape=ride.out_shape,
                          scratch_shapes=ride.scratch)(*ride.arrays)


def _adamw_math(wv, g, m, v):
    m = ADAM_B1 * m + (1.0 - ADAM_B1) * g
    v = ADAM_B2 * v + (1.0 - ADAM_B2) * (g * g)
    m_hat = m / (1.0 - ADAM_B1 ** ADAM_STEP)
    v_hat = v / (1.0 - ADAM_B2 ** ADAM_STEP)
    delta = -ADAM_LR * (m_hat / (jnp.sqrt(v_hat) + ADAM_EPS) + ADAM_WD * wv)
    return delta, m, v


def _adamw(parts, wv, m, v):
    R, C = wv.shape
    tr = max(t for t in range(16, 257, 16) if R % t == 0)

    def body(p_ref, w_ref, m_ref, v_ref, g_ref, d_ref, mo_ref, vo_ref):
        g = p_ref[0].astype(F32)
        for j in range(1, N_DEV):
            g = g + p_ref[j].astype(F32)
        d, mn, vn = _adamw_math(w_ref[...], g, m_ref[...], v_ref[...])
        g_ref[...] = g
        d_ref[...] = d
        mo_ref[...] = mn
        vo_ref[...] = vn

    row = lambda i: (i, 0)
    p_spec = pl.BlockSpec((N_DEV, tr, C), lambda i: (0, i, 0))
    out = jax.ShapeDtypeStruct((R, C), F32)
    return pl.pallas_call(
        body, name="adamw", grid=(R // tr,),
        in_specs=[p_spec, pl.BlockSpec((tr, C), row), pl.BlockSpec((tr, C), row), pl.BlockSpec((tr, C), row)],
        out_specs=[pl.BlockSpec((tr, C), row)] * 4, out_shape=[out] * 4,
        compiler_params=_params(1),
    )(parts, wv, m, v)


_ROW_SHARDED = ("ffn1_w_down", "ffn2_w_down", "w_out")
_GROUPS = {"ffn1": ("ffn1_w_gate", "ffn1_w_up", "ffn1_w_down"),
           "ffn2": ("ffn2_w_gate", "ffn2_w_up", "ffn2_w_down"),
           "attn": ("w_in", "mla_w_q_b", "mla_w_kv_b", "w_out")}
_SMALL = ("ffn1_norm", "mix_norm", "ffn2_norm", "out_norm_dil", "out_norm_mla", "mla_q_a_norm", "rel_bias",
          "mla_q_norm", "mla_k_norm", "mla_kv_a_norm", "dil_q_norm", "dil_k_norm")
_SMALL_ROWS = 48


def _cols_to_full(g):
    return g.transpose(1, 0, 2).reshape(g.shape[1], N_DEV * g.shape[2])


def _full_to_cols(f):
    return f.reshape(f.shape[0], N_DEV, f.shape[1] // N_DEV).transpose(1, 0, 2)


def _to_full(name, g):
    if name in _ROW_SHARDED:
        return g.reshape(-1, g.shape[-1])
    f = _cols_to_full(g)
    if name == "w_in":
        f = jnp.pad(f, ((0, 0), (0, PROJ_PAD - PROJ_COLS)))
    if name == "mla_w_q_b":
        f = jnp.pad(f.reshape(-1, MLA_HEADS, MLA_QK), ((0, 0), (0, 0), (0, MLA_PAD - MLA_QK)))
        f = f.reshape(-1, MLA_HEADS * MLA_PAD)
    return f


def _to_parts(name, f):
    if name in _ROW_SHARDED:
        return f.reshape(N_DEV, -1, f.shape[-1]).astype(BF16)
    if name == "w_in":
        f = f[:, :PROJ_COLS]
    if name == "mla_w_q_b":
        f = f.reshape(-1, MLA_HEADS, MLA_PAD)[:, :, :MLA_QK].reshape(-1, MLA_HEADS * MLA_QK)
    return _full_to_cols(f).astype(BF16)


class _Comm:
    def __init__(self, shards):
        self.shards, self.w, self.recv = shards, {}, {}

    def gather(self, names):
        return _Ride([self.shards[n] for n in names], [False] * len(names))

    def scatter(self, names, grads):
        return _Ride([_to_parts(n, grads[n]) for n in names], [True] * len(names))

    def weights_landed(self, names, got):
        self.w.update({n: _to_full(n, g) for n, g in zip(names, got)})

    def grads_landed(self, names, got):
        self.recv.update(zip(names, got))


def _pack_small(parts, extra):
    flat = jnp.concatenate([parts[n].reshape(-1) for n in _SMALL] + [extra.reshape(-1)])
    return jnp.pad(flat, (0, _SMALL_ROWS * 128 - flat.shape[0])).reshape(_SMALL_ROWS, 128)


def _unpack_small(packed, shapes):
    flat, out, off = packed.reshape(-1), {}, 0
    for n in _SMALL:
        size = math.prod(shapes[n])
        out[n] = flat[off:off + size].reshape(shapes[n])
        off += size
    return out, flat[off]


_NAMES = ("ffn1_norm", "ffn1_w_gate", "ffn1_w_up", "ffn1_w_down", "mix_norm", "w_in", "dil_q_norm", "dil_k_norm",
          "rel_bias", "mla_q_a_norm", "mla_w_q_b", "mla_kv_a_norm", "mla_w_kv_b", "mla_q_norm", "mla_k_norm",
          "out_norm_dil", "out_norm_mla", "w_out", "ffn2_norm", "ffn2_w_gate", "ffn2_w_up", "ffn2_w_down")


def kernel(x, ffn1_norm, ffn1_w_gate, ffn1_w_up, ffn1_w_down, mix_norm, w_in, dil_q_norm, dil_k_norm, rel_bias, mla_q_a_norm, mla_w_q_b, mla_kv_a_norm, mla_w_kv_b, mla_q_norm, mla_k_norm, out_norm_dil, out_norm_mla, w_out, ffn2_norm, ffn2_w_gate, ffn2_w_up, ffn2_w_down, loss_target, m_ffn1_norm, m_ffn1_w_gate, m_ffn1_w_up, m_ffn1_w_down, m_mix_norm, m_w_in, m_dil_q_norm, m_dil_k_norm, m_rel_bias, m_mla_q_a_norm, m_mla_w_q_b, m_mla_kv_a_norm, m_mla_w_kv_b, m_mla_q_norm, m_mla_k_norm, m_out_norm_dil, m_out_norm_mla, m_w_out, m_ffn2_norm, m_ffn2_w_gate, m_ffn2_w_up, m_ffn2_w_down, v_ffn1_norm, v_ffn1_w_gate, v_ffn1_w_up, v_ffn1_w_down, v_mix_norm, v_w_in, v_dil_q_norm, v_dil_k_norm, v_rel_bias, v_mla_q_a_norm, v_mla_w_q_b, v_mla_kv_a_norm, v_mla_w_kv_b, v_mla_q_norm, v_mla_k_norm, v_out_norm_dil, v_out_norm_mla, v_w_out, v_ffn2_norm, v_ffn2_w_gate, v_ffn2_w_up, v_ffn2_w_down):
    args = locals()
    wts = {n: args[n] for n in _NAMES}
    mom = {n: args["m_" + n] for n in _NAMES}
    var = {n: args["v_" + n] for n in _NAMES}

    matrices = [n for group in _GROUPS.values() for n in group]
    comm = _Comm({n: wts[n][0].astype(BF16) for n in matrices})
    comm.weights_landed(_GROUPS["ffn1"], _exchange(comm.gather(_GROUPS["ffn1"]), "gather_first"))
    small = {n: wts[n].reshape(1, -1) if n != "rel_bias" else wts[n] for n in _SMALL}

    loss, grad_x, gw, gs = _local_step(x[0], loss_target[0], small, comm)

    last = comm.scatter(("ffn1_w_up",), gw)
    got = _exchange(_Ride(last.arrays + [_pack_small(gs, loss[0, 0])], last.scatter + [False]), "scatter_last")
    comm.grads_landed(("ffn1_w_up",), got[:-1])

    res = {n: _adamw(comm.recv[n], wts[n][0], mom[n][0], var[n][0]) for n in matrices}
    shapes = {n: wts[n].shape for n in _SMALL}
    zero = jnp.zeros((), F32)
    packed = _adamw(got[-1], _pack_small(wts, zero), _pack_small(mom, zero), _pack_small(var, zero))
    loss_total = None
    for slot, q in enumerate(packed):
        vals, extra = _unpack_small(q, shapes)
        if slot == 0:
            loss_total = extra
        for n in _SMALL:
            res.setdefault(n, [None] * 4)[slot] = vals[n]
    outs = [loss_total, grad_x[None]]
    for slot in range(4):
        outs += [res[n][slot].reshape(wts[n].shape) for n in _NAMES]
    return tuple(outs)
```

```python
import math

import numpy as np
import jax
import jax.numpy as jnp
from jax import lax
from jax.experimental import pallas as pl
from jax.experimental.pallas import tpu as pltpu

F32, BF16 = jnp.float32, jnp.bfloat16
EPS = 1e-6
NEG = -1e30
N_DEV = 8

DIL_HEADS, DIL_HD = 8, 64
DIL_WIDTH = DIL_HEADS * DIL_HD
DIL_BRANCHES = ((128, 1), (512, 4), (2048, 16))
DIL_BLOCK = 128
MLA_HEADS, MLA_NOPE, MLA_ROPE, MLA_V = 4, 128, 64, 128
MLA_QK = MLA_NOPE + MLA_ROPE
MLA_PAD = 256
ROPE_BASE = 10000.0
REL_BUCKETS, REL_MAX_DIST = 32, 2048
PROJ_COLS, PROJ_PAD = 1984, 2048
FFN_RESID = 0.5
ADAM_LR, ADAM_B1, ADAM_B2, ADAM_EPS, ADAM_WD, ADAM_STEP = 0.001, 0.9, 0.999, 1e-08, 0.01, 10
VMEM_LIMIT = 62 * 1024 * 1024

_NT = (((1,), (1,)), ((), ()))
_TN = (((0,), (0,)), ((), ()))


def _dot(a, b):
    return jnp.dot(a, b, preferred_element_type=F32)


def _dot_nt(a, b):
    return lax.dot_general(a, b, _NT, preferred_element_type=F32)


def _dot_tn(a, b):
    return lax.dot_general(a, b, _TN, preferred_element_type=F32)


def _params(n_axes):
    return pltpu.CompilerParams(dimension_semantics=("arbitrary",) * n_axes, vmem_limit_bytes=VMEM_LIMIT)


def _rstd(x, n=None):
    n = x.shape[-1] if n is None else n
    return lax.rsqrt(jnp.sum(x * x, axis=-1, keepdims=True) / n + EPS)


def _rms_bwd(dy, x, g, r, n=None):
    n = x.shape[-1] if n is None else n
    u = dy * g
    dx = r * u - x * (r * r * r) * (jnp.sum(u * x, axis=-1, keepdims=True) / n)
    return dx, dy * x * r


def _sigmoid(x):
    return 1.0 / (1.0 + jnp.exp(-x))


def _split3(x):
    parts = []
    for _ in range(3):
        xb = x.astype(BF16)
        parts.append(xb)
        x = x - xb.astype(F32)
    return parts


def _ffn_fwd(x, gain, wg, wu, wd, ride=None, tm=512, tf=2816):
    T, D = x.shape
    F = wg.shape[1]
    ni, nj = T // tm, F // tf
    r_args, r_in, r_shape, r_out, r_scratch = _ride_parts(ride)

    def body(x_ref, g_ref, wg_ref, wu_ref, wd_ref, xo_ref, h_ref, gate_ref, up_ref, acc):
        j = pl.program_id(1)

        @pl.when(j == 0)
        def _():
            xv = x_ref[...]
            h_ref[...] = (xv * _rstd(xv) * g_ref[...]).astype(BF16)
            acc[...] = jnp.zeros_like(acc)

        h = h_ref[...]
        g = _dot(h, wg_ref[...])
        u = _dot(h, wu_ref[...])
        gate_ref[...] = g.astype(BF16)
        up_ref[...] = u.astype(BF16)
        a = (g * _sigmoid(g) * u).astype(BF16)
        acc[...] += _dot(a, wd_ref[...])

        @pl.when(j == nj - 1)
        def _():
            xo_ref[...] = x_ref[...] + FFN_RESID * acc[...]

    first = lambda: (pl.program_id(0) == 0) & (pl.program_id(1) == 0)
    last = lambda: (pl.program_id(0) == ni - 1) & (pl.program_id(1) == nj - 1)
    outs = pl.pallas_call(
        _riding(body, 5, 4, 1, ride, first, last), name="ffn_fwd", grid=(ni, nj),
        in_specs=[pl.BlockSpec((tm, D), lambda i, j: (i, 0)), pl.BlockSpec((1, D), lambda i, j: (0, 0)),
                  pl.BlockSpec((D, tf), lambda i, j: (0, j)), pl.BlockSpec((D, tf), lambda i, j: (0, j)),
                  pl.BlockSpec((tf, D), lambda i, j: (j, 0))] + r_in,
        out_specs=[pl.BlockSpec((tm, D), lambda i, j: (i, 0)), pl.BlockSpec((tm, D), lambda i, j: (i, 0)),
                   pl.BlockSpec((tm, tf), lambda i, j: (i, j)), pl.BlockSpec((tm, tf), lambda i, j: (i, j))] + r_out,
        out_shape=[jax.ShapeDtypeStruct((T, D), F32), jax.ShapeDtypeStruct((T, D), BF16),
                   jax.ShapeDtypeStruct((T, F), BF16), jax.ShapeDtypeStruct((T, F), BF16)] + r_shape,
        scratch_shapes=[pltpu.VMEM((tm, D), F32)] + r_scratch,
        compiler_params=_params(2),
    )(x, gain, wg, wu, wd, *r_args)
    return outs[:4], outs[4:]


def _ffn_bwd(dy, x, gain, gate, up, wg, wu, wd, ride=None, tm=256, tf=2816):
    T, D = x.shape
    F = wg.shape[1]
    ni, nj = T // tm, F // tf
    r_args, r_in, r_shape, r_out, r_scratch = _ride_parts(ride)

    def body(dy_ref, x_ref, g_ref, gate_ref, up_ref, wg_ref, wu_ref, wd_ref,
             dx_ref, a_ref, dg_ref, du_ref, dyh_ref, dgain_ref, acc):
        i, j = pl.program_id(0), pl.program_id(1)

        @pl.when((i == 0) & (j == 0))
        def _():
            dgain_ref[...] = jnp.zeros_like(dgain_ref)

        @pl.when(j == 0)
        def _():
            dyh_ref[...] = (FFN_RESID * dy_ref[...]).astype(BF16)
            acc[...] = jnp.zeros_like(acc)

        da = _dot_nt(dyh_ref[...], wd_ref[...])
        g = gate_ref[...].astype(F32)
        u = up_ref[...].astype(F32)
        sig = _sigmoid(g)
        s = g * sig
        a_ref[...] = (s * u).astype(BF16)
        dg = (da * u * (sig * (1.0 + g * (1.0 - sig)))).astype(BF16)
        du = (da * s).astype(BF16)
        dg_ref[...] = dg
        du_ref[...] = du
        acc[...] += _dot_nt(dg, wg_ref[...]) + _dot_nt(du, wu_ref[...])

        @pl.when(j == nj - 1)
        def _():
            xv = x_ref[...]
            dxn, dgc = _rms_bwd(acc[...], xv, g_ref[...], _rstd(xv))
            dx_ref[...] = dy_ref[...] + dxn
            dgain_ref[...] += jnp.sum(dgc, axis=0, keepdims=True)

    first = lambda: (pl.program_id(0) == 0) & (pl.program_id(1) == 0)
    last = lambda: (pl.program_id(0) == ni - 1) & (pl.program_id(1) == nj - 1)
    outs = pl.pallas_call(
        _riding(body, 8, 6, 1, ride, first, last), name="ffn_bwd", grid=(ni, nj),
        in_specs=[pl.BlockSpec((tm, D), lambda i, j: (i, 0)), pl.BlockSpec((tm, D), lambda i, j: (i, 0)),
                  pl.BlockSpec((1, D), lambda i, j: (0, 0)),
                  pl.BlockSpec((tm, tf), lambda i, j: (i, j)), pl.BlockSpec((tm, tf), lambda i, j: (i, j)),
                  pl.BlockSpec((D, tf), lambda i, j: (0, j)), pl.BlockSpec((D, tf), lambda i, j: (0, j)),
                  pl.BlockSpec((tf, D), lambda i, j: (j, 0))] + r_in,
        out_specs=[pl.BlockSpec((tm, D), lambda i, j: (i, 0)),
                   pl.BlockSpec((tm, tf), lambda i, j: (i, j)), pl.BlockSpec((tm, tf), lambda i, j: (i, j)),
                   pl.BlockSpec((tm, tf), lambda i, j: (i, j)),
                   pl.BlockSpec((tm, D), lambda i, j: (i, 0)), pl.BlockSpec((1, D), lambda i, j: (0, 0))] + r_out,
        out_shape=[jax.ShapeDtypeStruct((T, D), F32), jax.ShapeDtypeStruct((T, F), BF16),
                   jax.ShapeDtypeStruct((T, F), BF16), jax.ShapeDtypeStruct((T, F), BF16),
                   jax.ShapeDtypeStruct((T, D), BF16), jax.ShapeDtypeStruct((1, D), F32)] + r_shape,
        scratch_shapes=[pltpu.VMEM((tm, D), F32)] + r_scratch,
        compiler_params=_params(2),
    )(dy, x, gain, gate, up, wg, wu, wd, *r_args)
    return outs[:6], outs[6:]


def _matmul_tn(a, b, tk, tn, ride=None, tt=512):
    T, K = a.shape
    N = b.shape[1]
    tk, tn = min(tk, K), min(tn, N)
    grid = (K // tk, N // tn, T // tt)
    r_args, r_in, r_shape, r_out, r_scratch = _ride_parts(ride)

    def body(a_ref, b_ref, o_ref):
        @pl.when(pl.program_id(2) == 0)
        def _():
            o_ref[...] = jnp.zeros_like(o_ref)

        o_ref[...] += _dot_tn(a_ref[...].astype(BF16), b_ref[...].astype(BF16))

    first = lambda: (pl.program_id(0) == 0) & (pl.program_id(1) == 0) & (pl.program_id(2) == 0)
    last = lambda: ((pl.program_id(0) == grid[0] - 1) & (pl.program_id(1) == grid[1] - 1)
                    & (pl.program_id(2) == grid[2] - 1))
    outs = pl.pallas_call(
        _riding(body, 2, 1, 0, ride, first, last), name="matmul_tn", grid=grid,
        in_specs=[pl.BlockSpec((tt, tk), lambda k, n, t: (t, k)), pl.BlockSpec((tt, tn), lambda k, n, t: (t, n))] + r_in,
        out_specs=[pl.BlockSpec((tk, tn), lambda k, n, t: (k, n))] + r_out,
        out_shape=[jax.ShapeDtypeStruct((K, N), F32)] + r_shape,
        scratch_shapes=r_scratch,
        compiler_params=_params(3),
    )(a, b, *r_args)
    return outs[0], outs[1:]


def _loss_grad(y, target, tm=512):
    T, D = y.shape

    def body(y_ref, t_ref, dy_ref, loss_ref):
        @pl.when(pl.program_id(0) == 0)
        def _():
            loss_ref[...] = jnp.zeros_like(loss_ref)

        e = y_ref[...] - t_ref[...]
        dy_ref[...] = e * (1.0 / D)
        loss_ref[...] += (0.5 / D) * jnp.sum(e * e)

    return pl.pallas_call(
        body, name="loss_grad", grid=(T // tm,),
        in_specs=[pl.BlockSpec((tm, D), lambda i: (i, 0)), pl.BlockSpec((tm, D), lambda i: (i, 0))],
        out_specs=[pl.BlockSpec((tm, D), lambda i: (i, 0)), pl.BlockSpec((1, 128), lambda i: (0, 0))],
        out_shape=[jax.ShapeDtypeStruct((T, D), F32), jax.ShapeDtypeStruct((1, 128), F32)],
        compiler_params=_params(1),
    )(y, target)


def _in_proj(x, gain, w, gq, gk, tm=512):
    T, D = x.shape
    N = w.shape[1]
    W = DIL_WIDTH

    def body(x_ref, g_ref, w_ref, gq_ref, gk_ref, h_ref, p_ref, qh_ref, kh_ref):
        xv = x_ref[...]
        h = (xv * _rstd(xv) * g_ref[...]).astype(BF16)
        h_ref[...] = h
        p_ref[...] = _dot(h, w_ref[...])
        lo = lax.broadcasted_iota(jnp.int32, (tm, 128), 1) < DIL_HD
        for hp in range(DIL_HEADS // 2):
            q = p_ref[:, 128 * hp:128 * (hp + 1)]
            k = p_ref[:, W + 128 * hp:W + 128 * (hp + 1)]
            qh_ref[:, 128 * hp:128 * (hp + 1)] = (q * _pair_rstd(q, lo) * gq_ref[...]).astype(BF16).astype(F32)
            kh_ref[:, 128 * hp:128 * (hp + 1)] = (k * _pair_rstd(k, lo) * gk_ref[...]).astype(BF16).astype(F32)

    row = lambda i: (i, 0)
    fix = lambda i: (0, 0)
    return pl.pallas_call(
        body, name="in_proj", grid=(T // tm,),
        in_specs=[pl.BlockSpec((tm, D), row), pl.BlockSpec((1, D), fix), pl.BlockSpec((D, N), fix),
                  pl.BlockSpec((1, 128), fix), pl.BlockSpec((1, 128), fix)],
        out_specs=[pl.BlockSpec((tm, D), row), pl.BlockSpec((tm, N), row), pl.BlockSpec((tm, W), row),
                   pl.BlockSpec((tm, W), row)],
        out_shape=[jax.ShapeDtypeStruct((T, D), BF16), jax.ShapeDtypeStruct((T, N), F32),
                   jax.ShapeDtypeStruct((T, W), F32), jax.ShapeDtypeStruct((T, W), F32)],
        compiler_params=_params(1),
    )(x, gain, w, gq, gk)


def _in_proj_bwd(dx_up, x, gain, w, proj, gq, gk, dqkv, dcq, dckv, dkpe, tm=512):
    T, D = x.shape
    N = w.shape[1]
    W = DIL_WIDTH
    nb = len(dqkv)

    def body(*refs):
        dxu_ref, x_ref, g_ref, w_ref, q_ref, k_ref, gq_ref, gk_ref = refs[:8]
        dil_refs = refs[8:8 + 3 * nb]
        dcq_ref, dckv_ref, dkpe_ref, dx_ref, dp_ref, dgain_ref, dgq_ref, dgk_ref = refs[8 + 3 * nb:]

        @pl.when(pl.program_id(0) == 0)
        def _():
            for ref in (dgain_ref, dgq_ref, dgk_ref):
                ref[...] = jnp.zeros_like(ref)

        lo = lax.broadcasted_iota(jnp.int32, (tm, 128), 1) < DIL_HD
        norms = ((q_ref, gq_ref, dgq_ref), (k_ref, gk_ref, dgk_ref))
        for part in range(3):
            acc = dil_refs[part][...]
            for b in range(1, nb):
                acc = acc + dil_refs[3 * b + part][...]
            if part == 2:
                dp_ref[:, 2 * W:3 * W] = acc.astype(BF16)
                continue
            raw_ref, gn_ref, dgn_ref = norms[part]
            for hp in range(DIL_HEADS // 2):
                raw = raw_ref[:, 128 * hp:128 * (hp + 1)]
                d_raw, dgn = _pair_rms_bwd(acc[:, 128 * hp:128 * (hp + 1)], raw, _pair_rstd(raw, lo), gn_ref[...], lo)
                dp_ref[:, part * W + 128 * hp:part * W + 128 * (hp + 1)] = d_raw.astype(BF16)
                dgn_ref[...] += dgn
        dp_ref[:, 3 * W:3 * W + 256] = dcq_ref[...].astype(BF16)
        dp_ref[:, 3 * W + 256:3 * W + 384] = dckv_ref[...].astype(BF16)
        dp_ref[:, 3 * W + 384:N] = dkpe_ref[...].astype(BF16)
        dh = _dot_nt(dp_ref[...], w_ref[...])
        xv = x_ref[...]
        dxn, dgc = _rms_bwd(dh, xv, g_ref[...], _rstd(xv))
        dx_ref[...] = dxu_ref[...] + dxn
        dgain_ref[...] += jnp.sum(dgc, axis=0, keepdims=True)

    row = lambda i: (i, 0)
    fix = lambda i: (0, 0)
    return pl.pallas_call(
        body, name="in_proj_bwd", grid=(T // tm,),
        in_specs=[pl.BlockSpec((tm, D), row), pl.BlockSpec((tm, D), row), pl.BlockSpec((1, D), fix),
                  pl.BlockSpec((D, N), fix), pl.BlockSpec((tm, W), row), pl.BlockSpec((tm, W), lambda i: (i, 1)),
                  pl.BlockSpec((1, 128), fix), pl.BlockSpec((1, 128), fix)] + [pl.BlockSpec((tm, W), row)] * (3 * nb)
                 + [pl.BlockSpec((tm, 256), row), pl.BlockSpec((tm, 128), row), pl.BlockSpec((tm, 128), row)],
        out_specs=[pl.BlockSpec((tm, D), row), pl.BlockSpec((tm, N), row), pl.BlockSpec((1, D), fix),
                   pl.BlockSpec((1, 128), fix), pl.BlockSpec((1, 128), fix)],
        out_shape=[jax.ShapeDtypeStruct((T, D), F32), jax.ShapeDtypeStruct((T, N), BF16),
                   jax.ShapeDtypeStruct((1, D), F32), jax.ShapeDtypeStruct((1, 128), F32),
                   jax.ShapeDtypeStruct((1, 128), F32)],
        compiler_params=_params(1),
    )(dx_up, x, gain, w, proj, proj, gq, gk, *[a for triple in dqkv for a in triple], dcq, dckv, dkpe)


def _out_proj(x, o_dil, o_mla, g_dil, g_mla, w, tm=512):
    T, D = x.shape
    W = o_dil.shape[1]

    def body(x_ref, od_ref, om_ref, gd_ref, gm_ref, w_ref, xo_ref, oc_ref):
        od, om = od_ref[...], om_ref[...]
        oc_ref[:, 0:W] = (od * _rstd(od) * gd_ref[...]).astype(BF16)
        oc_ref[:, W:2 * W] = (om * _rstd(om) * gm_ref[...]).astype(BF16)
        xo_ref[...] = x_ref[...] + _dot(oc_ref[...], w_ref[...])

    row = lambda i: (i, 0)
    fix = lambda i: (0, 0)
    return pl.pallas_call(
        body, name="out_proj", grid=(T // tm,),
        in_specs=[pl.BlockSpec((tm, D), row), pl.BlockSpec((tm, W), row), pl.BlockSpec((tm, W), row),
                  pl.BlockSpec((1, W), fix), pl.BlockSpec((1, W), fix), pl.BlockSpec((2 * W, D), fix)],
        out_specs=[pl.BlockSpec((tm, D), row), pl.BlockSpec((tm, 2 * W), row)],
        out_shape=[jax.ShapeDtypeStruct((T, D), F32), jax.ShapeDtypeStruct((T, 2 * W), BF16)],
        compiler_params=_params(1),
    )(x, o_dil, o_mla, g_dil, g_mla, w)


def _out_proj_bwd(dx, o_dil, o_mla, g_dil, g_mla, w, tm=512):
    T, D = dx.shape
    W = o_dil.shape[1]

    def body(dx_ref, od_ref, om_ref, gd_ref, gm_ref, w_ref, dod_ref, dom_ref, dgd_ref, dgm_ref):
        @pl.when(pl.program_id(0) == 0)
        def _():
            dgd_ref[...] = jnp.zeros_like(dgd_ref)
            dgm_ref[...] = jnp.zeros_like(dgm_ref)

        doc = _dot_nt(dx_ref[...].astype(BF16), w_ref[...])
        od, om = od_ref[...], om_ref[...]
        dod, dgd = _rms_bwd(doc[:, 0:W], od, gd_ref[...], _rstd(od))
        dom, dgm = _rms_bwd(doc[:, W:2 * W], om, gm_ref[...], _rstd(om))
        dod_ref[...] = dod
        dom_ref[...] = dom
        dgd_ref[...] += jnp.sum(dgd, axis=0, keepdims=True)
        dgm_ref[...] += jnp.sum(dgm, axis=0, keepdims=True)

    row = lambda i: (i, 0)
    fix = lambda i: (0, 0)
    return pl.pallas_call(
        body, name="out_proj_bwd", grid=(T // tm,),
        in_specs=[pl.BlockSpec((tm, D), row), pl.BlockSpec((tm, W), row), pl.BlockSpec((tm, W), row),
                  pl.BlockSpec((1, W), fix), pl.BlockSpec((1, W), fix), pl.BlockSpec((2 * W, D), fix)],
        out_specs=[pl.BlockSpec((tm, W), row), pl.BlockSpec((tm, W), row),
                   pl.BlockSpec((1, W), fix), pl.BlockSpec((1, W), fix)],
        out_shape=[jax.ShapeDtypeStruct((T, W), F32), jax.ShapeDtypeStruct((T, W), F32),
                   jax.ShapeDtypeStruct((1, W), F32), jax.ShapeDtypeStruct((1, W), F32)],
        compiler_params=_params(1),
    )(dx, o_dil, o_mla, g_dil, g_mla, w)


def _pair_rstd(x, lo):
    sq = x * x
    s0 = jnp.sum(jnp.where(lo, sq, 0.0), axis=-1, keepdims=True)
    s1 = jnp.sum(jnp.where(lo, 0.0, sq), axis=-1, keepdims=True)
    return jnp.where(lo, lax.rsqrt(s0 / DIL_HD + EPS), lax.rsqrt(s1 / DIL_HD + EPS))


def _pair_rms_bwd(dn, x, r, g, lo):
    u = dn * g
    t = u * x
    d0 = jnp.sum(jnp.where(lo, t, 0.0), axis=-1, keepdims=True)
    d1 = jnp.sum(jnp.where(lo, 0.0, t), axis=-1, keepdims=True)
    dx = r * u - x * (r * r * r) * (jnp.where(lo, d0, d1) / DIL_HD)
    return dx, jnp.sum(dn * x * r, axis=0, keepdims=True)


def _pair_col(x, lo, e):
    sel = lo if e == 0 else jnp.logical_not(lo)
    return jnp.max(jnp.where(sel, x, NEG), axis=-1, keepdims=True)


def _dil_masks(n):
    row = lax.broadcasted_iota(jnp.int32, (DIL_BLOCK, DIL_BLOCK), 0)
    col = lax.broadcasted_iota(jnp.int32, (DIL_BLOCK, DIL_BLOCK), 1)
    return col < DIL_HD, jnp.logical_and(col >= row, n > 0), col <= row


def _dil_pairs(d):
    return 4 if d == 1 else 1


def _sub_rows(r, d):
    return pl.ds(r, DIL_BLOCK, stride=d) if d > 1 else pl.ds(0, DIL_BLOCK)


def _split_subsequences(pairs, d, P):
    for r in range(d):
        for p in range(P):
            for block, scratch in pairs:
                scratch[r * P + p] = block[_sub_rows(r, d), pl.ds(128 * p, 128)]


def _merge_subsequences(pairs, d, P):
    for r in range(d):
        for p in range(P):
            for block, scratch in pairs:
                block[_sub_rows(r, d), pl.ds(128 * p, 128)] = scratch[r * P + p]


def _dil_fwd(qh, kh, proj, bias_p, bias_c, d, prev):
    T = proj.shape[0]
    P = _dil_pairs(d)
    rows, cw, n_it = DIL_BLOCK * d, 128 * P, d * P
    nblk = T // rows
    has_prev = prev is not None

    def body(*refs):
        q_ref, kp_ref, kc_ref, vp_ref, vc_ref, bp_ref, bc_ref = refs[:7]
        refs = refs[7:]
        if has_prev:
            oin_ref, lin_ref = refs[:2]
            refs = refs[2:]
        o_ref, l_ref, qs, kps, kcs, vps, vcs, os_, ls_ = refs[:9]
        pb, n = pl.program_id(0), pl.program_id(1)
        lo, mask_p, mask_c = _dil_masks(n)
        loads = [(q_ref, qs), (kp_ref, kps), (kc_ref, kcs), (vp_ref, vps), (vc_ref, vcs)]
        if has_prev:
            ois, lis = refs[9:]
            loads += [(oin_ref, ois), (lin_ref, lis)]
        _split_subsequences(loads, d, P)

        def step(i, carry):
            h0 = 2 * (pb * P + i % P)
            qn = qs[i]
            kpn, kcn = kps[i].astype(BF16), kcs[i].astype(BF16)
            vp, vc = vps[i].astype(BF16), vcs[i].astype(BF16)
            o_e, l_e = [], []
            for e in range(2):
                sel = lo if e == 0 else jnp.logical_not(lo)
                qe = jnp.where(sel, qn, 0.0).astype(BF16)
                sp = jnp.where(mask_p, _dot_nt(qe, kpn) + bp_ref[h0 + e], NEG)
                sc = jnp.where(mask_c, _dot_nt(qe, kcn) + bc_ref[h0 + e], NEG)
                m = jnp.maximum(jnp.max(sp, axis=-1, keepdims=True), jnp.max(sc, axis=-1, keepdims=True))
                pp, pc = jnp.exp(sp - m), jnp.exp(sc - m)
                l = jnp.sum(pp, axis=-1, keepdims=True) + jnp.sum(pc, axis=-1, keepdims=True)
                o_e.append((_dot(pp.astype(BF16), vp) + _dot(pc.astype(BF16), vc)) / l)
                l_e.append(m + jnp.log(l))
            o = jnp.where(lo, o_e[0], o_e[1])
            lse = jnp.where(lo, l_e[0], l_e[1])
            if has_prev:
                lin = lis[i]
                mx = jnp.maximum(lin, lse)
                lnew = mx + jnp.log(jnp.exp(lin - mx) + jnp.exp(lse - mx))
                o = ois[i] * jnp.exp(lin - lnew) + o * jnp.exp(lse - lnew)
                lse = lnew
            os_[i] = o
            ls_[i] = lse
            return carry

        lax.fori_loop(0, n_it, step, 0, unroll=4)
        _merge_subsequences([(o_ref, os_), (l_ref, ls_)], d, P)

    blk = (rows, cw)
    vcol = 2 * DIL_WIDTH // cw
    prev_n = lambda n: jnp.maximum(n - 1, 0)
    fix3 = lambda pb, n: (0, 0, 0)
    tok = pl.BlockSpec(blk, lambda pb, n: (n, pb))
    tok_prev = pl.BlockSpec(blk, lambda pb, n: (prev_n(n), pb))
    bias_spec = pl.BlockSpec((DIL_HEADS, DIL_BLOCK, DIL_BLOCK), fix3)
    in_specs = [tok, tok_prev, tok,
                pl.BlockSpec(blk, lambda pb, n: (prev_n(n), vcol + pb)), pl.BlockSpec(blk, lambda pb, n: (n, vcol + pb)),
                bias_spec, bias_spec]
    args = [qh, kh, kh, proj, proj, bias_p, bias_c]
    n_scratch = 7
    if has_prev:
        in_specs += [tok, tok]
        args += list(prev)
        n_scratch += 2
    out = jax.ShapeDtypeStruct((T, DIL_WIDTH), F32)
    return pl.pallas_call(
        body, name=f"dil_fwd_d{d}", grid=(DIL_HEADS // 2 // P, nblk), in_specs=in_specs, out_specs=[tok, tok],
        out_shape=[out, out],
        scratch_shapes=[pltpu.VMEM((n_it, DIL_BLOCK, 128), F32)] * n_scratch,
        compiler_params=_params(2),
    )(*args)


def _dil_bwd(qh, kh, proj, o, lse, do, bias_p, bias_c, d):
    T = proj.shape[0]
    P = _dil_pairs(d)
    rows, cw, n_it = DIL_BLOCK * d, 128 * P, d * P
    nblk = T // rows

    def body(q_ref, kp_ref, kc_ref, vp_ref, vc_ref, o_ref, l_ref, do_ref, bp_ref, bc_ref,
             dq_ref, dk_ref, dv_ref, dbp_ref, dbc_ref,
             qs, kps, kcs, vps, vcs, os_, ls_, dos, dqs, dks, dvs, ck, cv):
        pb, n = pl.program_id(0), pl.program_id(1)
        lo, mask_p, mask_c = _dil_masks(n)

        @pl.when((pb == 0) & (n == 0))
        def _():
            dbp_ref[...] = jnp.zeros_like(dbp_ref)
            dbc_ref[...] = jnp.zeros_like(dbc_ref)

        @pl.when(n == 0)
        def _():
            ck[...] = jnp.zeros_like(ck)
            cv[...] = jnp.zeros_like(cv)

        _split_subsequences([(q_ref, qs), (kp_ref, kps), (kc_ref, kcs), (vp_ref, vps), (vc_ref, vcs),
                             (o_ref, os_), (l_ref, ls_), (do_ref, dos)], d, P)

        def step(i, carry):
            h0 = 2 * (pb * P + i % P)
            qn = qs[i]
            qb = qn.astype(BF16)
            kpn, kcn = kps[i].astype(BF16), kcs[i].astype(BF16)
            vp, vc = vps[i].astype(BF16), vcs[i].astype(BF16)
            dov = dos[i]
            dob = dov.astype(BF16)
            dot_o = dov * os_[i]
            lse_pair = ls_[i]
            res = []
            for e in range(2):
                sel = lo if e == 0 else jnp.logical_not(lo)
                h = h0 + e
                qe = jnp.where(sel, qn, 0.0).astype(BF16)
                doe = jnp.where(sel, dov, 0.0).astype(BF16)
                delta = jnp.sum(jnp.where(sel, dot_o, 0.0), axis=-1, keepdims=True)
                lse_e = _pair_col(lse_pair, lo, e)
                sp = jnp.where(mask_p, _dot_nt(qe, kpn) + bp_ref[h], NEG)
                sc = jnp.where(mask_c, _dot_nt(qe, kcn) + bc_ref[h], NEG)
                pp, pc = jnp.exp(sp - lse_e), jnp.exp(sc - lse_e)
                dsp = pp * (_dot_nt(doe, vp) - delta)
                dsc = pc * (_dot_nt(doe, vc) - delta)
                dbp_ref[h] += dsp
                dbc_ref[h] += dsc
                dspb, dscb = dsp.astype(BF16), dsc.astype(BF16)
                res.append((_dot(dspb, kpn) + _dot(dscb, kcn), _dot_tn(dspb, qb), _dot_tn(dscb, qb),
                            _dot_tn(pp.astype(BF16), dob), _dot_tn(pc.astype(BF16), dob)))
            dqn, dkn_p, dkn_c, dv_p, dv_c = (jnp.where(lo, a, b) for a, b in zip(res[0], res[1]))
            dqs[i] = dqn
            dks[i] = ck[i] + dkn_p
            dvs[i] = cv[i] + dv_p
            ck[i] = dkn_c
            cv[i] = dv_c
            return carry

        @pl.when(n < nblk)
        def _():
            lax.fori_loop(0, n_it, step, 0, unroll=2)
            _merge_subsequences([(dq_ref, dqs), (dk_ref, dks), (dv_ref, dvs)], d, P)

        @pl.when(n == nblk)
        def _():
            _merge_subsequences([(dk_ref, ck), (dv_ref, cv)], d, P)

    blk = (rows, cw)
    vcol = 2 * DIL_WIDTH // cw
    qn_ = lambda n: jnp.minimum(n, nblk - 1)
    pn_ = lambda n: jnp.maximum(n - 1, 0)
    fix3 = lambda pb, n: (0, 0, 0)
    tok_q = pl.BlockSpec(blk, lambda pb, n: (qn_(n), pb))
    tok_p = pl.BlockSpec(blk, lambda pb, n: (pn_(n), pb))
    bias_spec = pl.BlockSpec((DIL_HEADS, DIL_BLOCK, DIL_BLOCK), fix3)
    in_specs = [tok_q, tok_p, tok_q,
                pl.BlockSpec(blk, lambda pb, n: (pn_(n), vcol + pb)), pl.BlockSpec(blk, lambda pb, n: (qn_(n), vcol + pb)),
                tok_q, tok_q, tok_q, bias_spec, bias_spec]
    tok_shape = jax.ShapeDtypeStruct((T, DIL_WIDTH), F32)
    bias_shape = jax.ShapeDtypeStruct((DIL_HEADS, DIL_BLOCK, DIL_BLOCK), F32)
    dq, dk, dv, dbp, dbc = pl.pallas_call(
        body, name=f"dil_bwd_d{d}", grid=(DIL_HEADS // 2 // P, nblk + 1), in_specs=in_specs,
        out_specs=[tok_q, tok_p, tok_p, bias_spec, bias_spec],
        out_shape=[tok_shape, tok_shape, tok_shape, bias_shape, bias_shape],
        scratch_shapes=[pltpu.VMEM((n_it, DIL_BLOCK, 128), F32)] * 13,
        compiler_params=_params(2),
    )(qh, kh, kh, proj, proj, o, lse, do, bias_p, bias_c)
    return (dq, dk, dv), dbp, dbc


def _t5_bucket(dist):
    max_exact = REL_BUCKETS // 2
    dd = np.maximum(dist, 1).astype(np.float32)
    large = max_exact + (np.log(dd / max_exact) / np.log(REL_MAX_DIST / max_exact)
                         * (REL_BUCKETS - max_exact)).astype(np.int32)
    large = np.minimum(large, REL_BUCKETS - 1)
    return np.where(dist < max_exact, dist, large).astype(np.int32)


def _bucket_onehots():
    i = np.arange(DIL_BLOCK)[:, None]
    j = np.arange(DIL_BLOCK)[None, :]
    out = []
    for _, d in DIL_BRANCHES:
        for dist in (DIL_BLOCK + i - j, i - j):
            bucket = _t5_bucket(np.clip(dist, 0, None) * d).reshape(-1)
            out.append(jnp.asarray(np.eye(REL_BUCKETS, dtype=np.float32)[:, bucket], BF16))
    return out


def _bias_tables(rel_bias, onehots):
    n = len(onehots)

    def body(rb_ref, *refs):
        parts = _split3(rb_ref[...])
        for k in range(n):
            oh = refs[k][...]
            refs[n + k][...] = _dot(parts[0], oh) + _dot(parts[1], oh) + _dot(parts[2], oh)

    return pl.pallas_call(
        body, name="bias_tables",
        out_shape=[jax.ShapeDtypeStruct((DIL_HEADS, DIL_BLOCK * DIL_BLOCK), F32)] * n,
        compiler_params=pltpu.CompilerParams(vmem_limit_bytes=VMEM_LIMIT),
    )(rel_bias, *onehots)


def _bias_grad(dbs, onehots):
    n = len(dbs)

    def body(*refs):
        acc = jnp.zeros((DIL_HEADS, REL_BUCKETS), F32)
        for k in range(n):
            oh = refs[n + k][...]
            for part in _split3(refs[k][...]):
                acc = acc + _dot_nt(part, oh)
        refs[-1][...] = acc

    return pl.pallas_call(
        body, name="bias_grad",
        out_shape=jax.ShapeDtypeStruct((DIL_HEADS, REL_BUCKETS), F32),
        compiler_params=pltpu.CompilerParams(vmem_limit_bytes=VMEM_LIMIT),
    )(*dbs, *onehots)


def _swap_halves(x):
    lane = lax.broadcasted_iota(jnp.int32, x.shape, 1)
    first = (lane % 64) < 32
    return jnp.where(first, pltpu.roll(x, 96, 1), pltpu.roll(x, 32, 1))


def _rope_tables(T):
    pos = jnp.arange(T, dtype=F32)
    inv_freq = ROPE_BASE ** (-jnp.arange(0, MLA_ROPE, 2, dtype=F32) / MLA_ROPE)
    ang = pos[:, None] * inv_freq[None, :]
    z = jnp.zeros((T, 128 - MLA_ROPE), F32)
    cos = jnp.concatenate([jnp.cos(ang), jnp.cos(ang), z], axis=-1)
    sin = jnp.concatenate([-jnp.sin(ang), jnp.sin(ang), z], axis=-1)
    return cos, sin


def _mla_prep(proj, cos, sin, g_qa, g_kva, g_q, g_k, wq, wkv, tm=512):
    T = proj.shape[0]
    H = MLA_HEADS
    scale = MLA_QK ** -0.5

    def body(cq_ref, ckv_ref, kpe_ref, cos_ref, sin_ref, gqa_ref, gkva_ref, gq_ref, gk_ref, wq_ref, wkv_ref,
             q_ref, k_ref, v_ref):
        cosv, sinv = cos_ref[...], sin_ref[...]

        def rope(x):
            return x * cosv + _swap_halves(x) * sinv

        cq = cq_ref[...]
        qp = _dot((cq * _rstd(cq) * gqa_ref[...]).astype(BF16), wq_ref[...])
        ckv = ckv_ref[...]
        kvp = _dot((ckv * _rstd(ckv) * gkva_ref[...]).astype(BF16), wkv_ref[...])
        kpe = kpe_ref[...]
        for h in range(H):
            a = qp[:, MLA_PAD * h:MLA_PAD * (h + 1)]
            qn = a * _rstd(a, MLA_QK) * gq_ref[...]
            q_ref[h, :, 0:128] = (qn[:, 0:128] * scale).astype(BF16)
            q_ref[h, :, 128:256] = (rope(qn[:, 128:256]) * scale).astype(BF16)
            kn = kvp[:, MLA_PAD * h:MLA_PAD * h + 128]
            r = lax.rsqrt((jnp.sum(kn * kn, axis=-1, keepdims=True)
                           + jnp.sum(kpe * kpe, axis=-1, keepdims=True)) / MLA_QK + EPS)
            k_ref[h, :, 0:128] = (kn * r * gk_ref[:, 0:128]).astype(BF16)
            k_ref[h, :, 128:256] = rope(kpe * r * gk_ref[:, 128:256]).astype(BF16)
            v_ref[h] = kvp[:, MLA_PAD * h + 128:MLA_PAD * (h + 1)].astype(BF16)

    fix = lambda i: (0, 0)
    return pl.pallas_call(
        body, name="mla_prep", grid=(T // tm,),
        in_specs=[pl.BlockSpec((tm, 256), lambda i: (i, 6)), pl.BlockSpec((tm, 128), lambda i: (i, 14)),
                  pl.BlockSpec((tm, 128), lambda i: (i, 15)),
                  pl.BlockSpec((tm, 128), lambda i: (i, 0)), pl.BlockSpec((tm, 128), lambda i: (i, 0)),
                  pl.BlockSpec((1, 256), fix), pl.BlockSpec((1, 128), fix),
                  pl.BlockSpec((1, 256), fix), pl.BlockSpec((1, 256), fix),
                  pl.BlockSpec((256, H * MLA_PAD), fix), pl.BlockSpec((128, H * MLA_PAD), fix)],
        out_specs=[pl.BlockSpec((H, tm, MLA_PAD), lambda i: (0, i, 0)), pl.BlockSpec((H, tm, MLA_PAD), lambda i: (0, i, 0)),
                   pl.BlockSpec((H, tm, MLA_V), lambda i: (0, i, 0))],
        out_shape=[jax.ShapeDtypeStruct((H, T, MLA_PAD), BF16), jax.ShapeDtypeStruct((H, T, MLA_PAD), BF16),
                   jax.ShapeDtypeStruct((H, T, MLA_V), BF16)],
        compiler_params=_params(1),
    )(proj, proj, proj, cos, sin, g_qa, g_kva, g_q, g_k, wq, wkv)


def _mla_prep_bwd(proj, cos, sin, g_qa, g_kva, g_q, g_k, wq, wkv, dq, dk, dv, tm=512):
    T = proj.shape[0]
    H = MLA_HEADS
    scale = MLA_QK ** -0.5

    def body(cq_ref, ckv_ref, kpe_ref, cos_ref, sin_ref, gqa_ref, gkva_ref, gq_ref, gk_ref, wq_ref, wkv_ref,
             dq_ref, dk_ref, dv_ref,
             dcq_ref, dckv_ref, dkpe_ref, cqn_ref, ckvn_ref, dqp_ref, dkvp_ref,
             dgqa_ref, dgkva_ref, dgq_ref, dgk_ref):
        @pl.when(pl.program_id(0) == 0)
        def _():
            for ref in (dgqa_ref, dgkva_ref, dgq_ref, dgk_ref):
                ref[...] = jnp.zeros_like(ref)

        cosv, sinv = cos_ref[...], sin_ref[...]

        def rope_bwd(dy):
            return dy * cosv + _swap_halves(dy * sinv)

        cq = cq_ref[...]
        rcq = _rstd(cq)
        cqn = (cq * rcq * gqa_ref[...]).astype(BF16)
        cqn_ref[...] = cqn
        qp = _dot(cqn, wq_ref[...])
        ckv = ckv_ref[...]
        rckv = _rstd(ckv)
        ckvn = (ckv * rckv * gkva_ref[...]).astype(BF16)
        ckvn_ref[...] = ckvn
        kvp = _dot(ckvn, wkv_ref[...])
        kpe = kpe_ref[...]
        dkpe = jnp.zeros_like(kpe)
        dgq = jnp.zeros((1, MLA_PAD), F32)
        dgk = jnp.zeros((1, MLA_PAD), F32)
        for h in range(H):
            a = qp[:, MLA_PAD * h:MLA_PAD * (h + 1)]
            dqh = dq_ref[h]
            dn = jnp.concatenate([dqh[:, 0:128], rope_bwd(dqh[:, 128:256])], axis=-1) * scale
            da, dg = _rms_bwd(dn, a, gq_ref[...], _rstd(a, MLA_QK), MLA_QK)
            dgq = dgq + jnp.sum(dg, axis=0, keepdims=True)
            dqp_ref[:, MLA_PAD * h:MLA_PAD * (h + 1)] = da.astype(BF16)

            ak = jnp.concatenate([kvp[:, MLA_PAD * h:MLA_PAD * h + 128], kpe], axis=-1)
            dkh = dk_ref[h]
            dnk = jnp.concatenate([dkh[:, 0:128], rope_bwd(dkh[:, 128:256])], axis=-1)
            dak, dg = _rms_bwd(dnk, ak, gk_ref[...], _rstd(ak, MLA_QK), MLA_QK)
            dgk = dgk + jnp.sum(dg, axis=0, keepdims=True)
            dkpe = dkpe + dak[:, 128:256]
            dkvp_ref[:, MLA_PAD * h:MLA_PAD * h + 128] = dak[:, 0:128].astype(BF16)
            dkvp_ref[:, MLA_PAD * h + 128:MLA_PAD * (h + 1)] = dv_ref[h].astype(BF16)
        dkpe_ref[...] = dkpe
        dgq_ref[...] += dgq
        dgk_ref[...] += dgk
        dcq, dg = _rms_bwd(_dot_nt(dqp_ref[...], wq_ref[...]), cq, gqa_ref[...], rcq)
        dcq_ref[...] = dcq
        dgqa_ref[...] += jnp.sum(dg, axis=0, keepdims=True)
        dckv, dg = _rms_bwd(_dot_nt(dkvp_ref[...], wkv_ref[...]), ckv, gkva_ref[...], rckv)
        dckv_ref[...] = dckv
        dgkva_ref[...] += jnp.sum(dg, axis=0, keepdims=True)

    fix = lambda i: (0, 0)
    row = lambda i: (i, 0)
    head = lambda i: (0, i, 0)
    return pl.pallas_call(
        body, name="mla_prep_bwd", grid=(T // tm,),
        in_specs=[pl.BlockSpec((tm, 256), lambda i: (i, 6)), pl.BlockSpec((tm, 128), lambda i: (i, 14)),
                  pl.BlockSpec((tm, 128), lambda i: (i, 15)),
                  pl.BlockSpec((tm, 128), row), pl.BlockSpec((tm, 128), row),
                  pl.BlockSpec((1, 256), fix), pl.BlockSpec((1, 128), fix),
                  pl.BlockSpec((1, 256), fix), pl.BlockSpec((1, 256), fix),
                  pl.BlockSpec((256, H * MLA_PAD), fix), pl.BlockSpec((128, H * MLA_PAD), fix),
                  pl.BlockSpec((H, tm, MLA_PAD), head), pl.BlockSpec((H, tm, MLA_PAD), head),
                  pl.BlockSpec((H, tm, MLA_V), head)],
        out_specs=[pl.BlockSpec((tm, 256), row), pl.BlockSpec((tm, 128), row), pl.BlockSpec((tm, 128), row),
                   pl.BlockSpec((tm, 256), row), pl.BlockSpec((tm, 128), row),
                   pl.BlockSpec((tm, H * MLA_PAD), row), pl.BlockSpec((tm, H * MLA_PAD), row),
                   pl.BlockSpec((1, 256), fix), pl.BlockSpec((1, 128), fix),
                   pl.BlockSpec((1, 256), fix), pl.BlockSpec((1, 256), fix)],
        out_shape=[jax.ShapeDtypeStruct((T, 256), F32), jax.ShapeDtypeStruct((T, 128), F32),
                   jax.ShapeDtypeStruct((T, 128), F32),
                   jax.ShapeDtypeStruct((T, 256), BF16), jax.ShapeDtypeStruct((T, 128), BF16),
                   jax.ShapeDtypeStruct((T, H * MLA_PAD), BF16), jax.ShapeDtypeStruct((T, H * MLA_PAD), BF16),
                   jax.ShapeDtypeStruct((1, 256), F32), jax.ShapeDtypeStruct((1, 128), F32),
                   jax.ShapeDtypeStruct((1, 256), F32), jax.ShapeDtypeStruct((1, 256), F32)],
        compiler_params=_params(1),
    )(proj, proj, proj, cos, sin, g_qa, g_kva, g_q, g_k, wq, wkv, dq, dk, dv)


def _causal_pairs(T, tq, tk, key_major):
    pairs = [(i, j) for i in range(T // tq) for j in range(T // tk) if j * tk <= i * tq + tq - 1]
    if key_major:
        pairs.sort(key=lambda p: (p[1], p[0]))
    outer = [p[1] if key_major else p[0] for p in pairs]
    first = [int(t == 0 or outer[t] != outer[t - 1]) for t in range(len(pairs))]
    last = [int(t == len(pairs) - 1 or outer[t] != outer[t + 1]) for t in range(len(pairs))]
    tab = lambda v: jnp.asarray(np.array(v, np.int32))
    return tab([p[0] for p in pairs]), tab([p[1] for p in pairs]), tab(first), tab(last)


def _causal_scores(qv, kv, qi, ki, row0, tq, tk, masked):
    s = _dot_nt(qv, kv)
    if masked:
        row = lax.broadcasted_iota(jnp.int32, s.shape, 0) + (qi * tq + row0)
        col = lax.broadcasted_iota(jnp.int32, s.shape, 1) + ki * tk
        s = jnp.where(col <= row, s, NEG)
    return s


def _mla_attn(q, k, v, ride=None, tq=1024, tk=2048, rc=256):
    H, T, _ = q.shape
    tables = _causal_pairs(T, tq, tk, key_major=False)
    n_pairs = int(tables[0].shape[0])
    r_args, r_in, r_shape, r_out, r_scratch = _ride_parts(ride)

    def body(qt, kt, ft, lt, q_ref, k_ref, v_ref, o_ref, lse_ref, m_s, l_s, acc):
        t = pl.program_id(1)
        qi, ki = qt[t], kt[t]

        @pl.when(ft[t] == 1)
        def _():
            m_s[...] = jnp.full_like(m_s, NEG)
            l_s[...] = jnp.zeros_like(l_s)
            acc[...] = jnp.zeros_like(acc)

        def update(masked):
            kk, vv = k_ref[...], v_ref[...]
            for c in range(tq // rc):
                rows = pl.ds(c * rc, rc)
                s = _causal_scores(q_ref[rows, :], kk, qi, ki, c * rc, tq, tk, masked)
                m_old = m_s[rows, :]
                m_new = jnp.maximum(m_old, jnp.max(s, axis=-1, keepdims=True))
                alpha = jnp.exp(m_old - m_new)
                p = jnp.exp(s - m_new)
                l_s[rows, :] = alpha * l_s[rows, :] + jnp.sum(p, axis=-1, keepdims=True)
                acc[rows, :] = alpha * acc[rows, :] + _dot(p.astype(BF16), vv)
                m_s[rows, :] = m_new

        diagonal = (ki + 1) * tk - 1 > qi * tq

        @pl.when(diagonal)
        def _():
            update(True)

        @pl.when(jnp.logical_not(diagonal))
        def _():
            update(False)

        @pl.when(lt[t] == 1)
        def _():
            o_ref[...] = acc[...] / l_s[...]
            lse_ref[...] = jnp.broadcast_to(m_s[...] + jnp.log(l_s[...]), lse_ref.shape)

    qrow = lambda h, t, qt, kt, ft, lt: (h, qt[t], 0)
    krow = lambda h, t, qt, kt, ft, lt: (h, kt[t], 0)
    first = lambda: (pl.program_id(0) == 0) & (pl.program_id(1) == 0)
    last = lambda: (pl.program_id(0) == H - 1) & (pl.program_id(1) == n_pairs - 1)
    outs = pl.pallas_call(
        _riding(body, 7, 2, 3, ride, first, last), name="mla_attn",
        grid_spec=pltpu.PrefetchScalarGridSpec(
            num_scalar_prefetch=4, grid=(H, n_pairs),
            in_specs=[pl.BlockSpec((None, tq, MLA_PAD), qrow), pl.BlockSpec((None, tk, MLA_PAD), krow),
                      pl.BlockSpec((None, tk, MLA_V), krow)] + r_in,
            out_specs=[pl.BlockSpec((tq, MLA_V), lambda h, t, qt, kt, ft, lt: (qt[t], h)),
                       pl.BlockSpec((None, tq, 128), qrow)] + r_out,
            scratch_shapes=[pltpu.VMEM((tq, 1), F32), pltpu.VMEM((tq, 1), F32), pltpu.VMEM((tq, MLA_V), F32)]
            + r_scratch),
        out_shape=[jax.ShapeDtypeStruct((T, H * MLA_V), F32), jax.ShapeDtypeStruct((H, T, 128), F32)] + r_shape,
        compiler_params=_params(2),
    )(*tables, q, k, v, *r_args)
    return outs[:2], outs[2:]


def _mla_attn_bwd(q, k, v, o, lse, do, ride=None, tq=1024, tk=1024, rc=512):
    H, T, _ = q.shape
    tables = _causal_pairs(T, tq, tk, key_major=True)
    n_pairs = int(tables[0].shape[0])
    r_args, r_in, r_shape, r_out, r_scratch = _ride_parts(ride)

    def body(qt, kt, ft, lt, q_ref, k_ref, v_ref, o_ref, lse_ref, do_ref, dq_ref, dk_ref, dv_ref, dk_s, dv_s):
        t = pl.program_id(1)
        qi, ki = qt[t], kt[t]

        @pl.when(t == 0)
        def _():
            dq_ref[...] = jnp.zeros_like(dq_ref)

        @pl.when(ft[t] == 1)
        def _():
            dk_s[...] = jnp.zeros_like(dk_s)
            dv_s[...] = jnp.zeros_like(dv_s)

        def update(masked):
            kk, vv = k_ref[...], v_ref[...]
            for c in range(tq // rc):
                rows = pl.ds(c * rc, rc)
                qv, dov = q_ref[rows, :], do_ref[rows, :]
                delta = jnp.sum(dov * o_ref[rows, :], axis=-1, keepdims=True)
                lse_v = jnp.max(lse_ref[rows, :], axis=-1, keepdims=True)
                p = jnp.exp(_causal_scores(qv, kk, qi, ki, c * rc, tq, tk, masked) - lse_v)
                dob = dov.astype(BF16)
                dv_s[...] += _dot_tn(p.astype(BF16), dob)
                ds = (p * (_dot_nt(dob, vv) - delta)).astype(BF16)
                dk_s[...] += _dot_tn(ds, qv)
                out_rows = pl.ds(pl.multiple_of(qi * tq + c * rc, rc), rc)
                dq_ref[out_rows, :] += _dot(ds, kk)

        diagonal = (ki + 1) * tk - 1 > qi * tq

        @pl.when(diagonal)
        def _():
            update(True)

        @pl.when(jnp.logical_not(diagonal))
        def _():
            update(False)

        @pl.when(lt[t] == 1)
        def _():
            dk_ref[...] = dk_s[...]
            dv_ref[...] = dv_s[...]

    qrow = lambda h, t, qt, kt, ft, lt: (h, qt[t], 0)
    krow = lambda h, t, qt, kt, ft, lt: (h, kt[t], 0)
    qcol = lambda h, t, qt, kt, ft, lt: (qt[t], h)
    first = lambda: (pl.program_id(0) == 0) & (pl.program_id(1) == 0)
    last = lambda: (pl.program_id(0) == H - 1) & (pl.program_id(1) == n_pairs - 1)
    outs = pl.pallas_call(
        _riding(body, 10, 3, 2, ride, first, last), name="mla_attn_bwd",
        grid_spec=pltpu.PrefetchScalarGridSpec(
            num_scalar_prefetch=4, grid=(H, n_pairs),
            in_specs=[pl.BlockSpec((None, tq, MLA_PAD), qrow), pl.BlockSpec((None, tk, MLA_PAD), krow),
                      pl.BlockSpec((None, tk, MLA_V), krow), pl.BlockSpec((tq, MLA_V), qcol),
                      pl.BlockSpec((None, tq, 128), qrow), pl.BlockSpec((tq, MLA_V), qcol)] + r_in,
            out_specs=[pl.BlockSpec((None, T, MLA_PAD), lambda h, t, qt, kt, ft, lt: (h, 0, 0)),
                       pl.BlockSpec((None, tk, MLA_PAD), krow), pl.BlockSpec((None, tk, MLA_V), krow)] + r_out,
            scratch_shapes=[pltpu.VMEM((tk, MLA_PAD), F32), pltpu.VMEM((tk, MLA_V), F32)] + r_scratch),
        out_shape=[jax.ShapeDtypeStruct((H, T, MLA_PAD), F32), jax.ShapeDtypeStruct((H, T, MLA_PAD), F32),
                   jax.ShapeDtypeStruct((H, T, MLA_V), F32)] + r_shape,
        compiler_params=_params(2),
    )(*tables, q, k, v, o, lse, do, *r_args)
    return outs[:3], outs[3:]


def _pair_gain(g):
    return jnp.tile(g.reshape(1, DIL_HD), (1, 2))


def _pad_gain(g):
    return jnp.pad(g.reshape(1, MLA_QK), ((0, 0), (0, MLA_PAD - MLA_QK)))


def _local_step(x, target, s, comm):
    T = x.shape[0]
    w = comm.w
    gq, gk = _pair_gain(s["dil_q_norm"]) * DIL_HD ** -0.5, _pair_gain(s["dil_k_norm"])
    g_q, g_k = _pad_gain(s["mla_q_norm"]), _pad_gain(s["mla_k_norm"])
    cos, sin = _rope_tables(T)
    onehots = _bucket_onehots()
    tables = [t.reshape(DIL_HEADS, DIL_BLOCK, DIL_BLOCK) for t in _bias_tables(s["rel_bias"], onehots)]
    biases = list(zip(tables[0::2], tables[1::2]))

    (x1, h1, gate1, up1), got = _ffn_fwd(x, s["ffn1_norm"], w["ffn1_w_gate"], w["ffn1_w_up"], w["ffn1_w_down"],
                                         ride=comm.gather(_GROUPS["attn"]))
    comm.weights_landed(_GROUPS["attn"], got)
    hm, proj, qh, kh = _in_proj(x1, s["mix_norm"], w["w_in"], gq, gk)
    dil = None
    for (_, d), (bp, bc) in zip(DIL_BRANCHES, biases):
        dil = _dil_fwd(qh, kh, proj, bp, bc, d, dil)
    o_dil, lse_dil = dil
    q, k, v = _mla_prep(proj, cos, sin, s["mla_q_a_norm"], s["mla_kv_a_norm"], g_q, g_k, w["mla_w_q_b"], w["mla_w_kv_b"])
    (o_mla, lse_mla), got = _mla_attn(q, k, v, ride=comm.gather(_GROUPS["ffn2"]))
    comm.weights_landed(_GROUPS["ffn2"], got)
    x2, oc = _out_proj(x1, o_dil, o_mla, s["out_norm_dil"], s["out_norm_mla"], w["w_out"])
    (y, h2, gate2, up2), _ = _ffn_fwd(x2, s["ffn2_norm"], w["ffn2_w_gate"], w["ffn2_w_up"], w["ffn2_w_down"])
    dy, loss = _loss_grad(y, target)

    gw, gs = {}, {}

    def ffn_grads(name, dy_in, x_in, h, gate, up, ride=None, scatter_early=False):
        (dx, a, dg, du, dyh, dgain), got = _ffn_bwd(dy_in, x_in, s[name + "_norm"], gate, up,
                                                    w[name + "_w_gate"], w[name + "_w_up"], w[name + "_w_down"], ride=ride)
        gs[name + "_norm"] = dgain
        down, gate_n, up_n = (name + "_w_down",), (name + "_w_gate",), (name + "_w_up",)
        gw[down[0]], _ = _matmul_tn(a, dyh, 1408, 1024)
        gw[gate_n[0]], landed = _matmul_tn(h, dg, 1024, 1408, ride=comm.scatter(down, gw) if scatter_early else None)
        comm.grads_landed(down, landed)
        gw[up_n[0]], landed = _matmul_tn(h, du, 1024, 1408, ride=comm.scatter(gate_n, gw) if scatter_early else None)
        comm.grads_landed(gate_n, landed)
        return dx, got

    dx2, _ = ffn_grads("ffn2", dy, x2, h2, gate2, up2)
    gw["w_out"], _ = _matmul_tn(oc, dx2, 1024, 1024)
    do_dil, do_mla, gs["out_norm_dil"], gs["out_norm_mla"] = _out_proj_bwd(
        dx2, o_dil, o_mla, s["out_norm_dil"], s["out_norm_mla"], w["w_out"])

    (dq, dk, dv), got = _mla_attn_bwd(q, k, v, o_mla, lse_mla, do_mla, ride=comm.scatter(_GROUPS["ffn2"], gw))
    comm.grads_landed(_GROUPS["ffn2"], got)
    (dcq, dckv, dkpe, cqn, ckvn, dqp, dkvp, gs["mla_q_a_norm"], gs["mla_kv_a_norm"], dg_q, dg_k) = _mla_prep_bwd(
        proj, cos, sin, s["mla_q_a_norm"], s["mla_kv_a_norm"], g_q, g_k, w["mla_w_q_b"], w["mla_w_kv_b"], dq, dk, dv)
    gs["mla_q_norm"], gs["mla_k_norm"] = dg_q[:, :MLA_QK], dg_k[:, :MLA_QK]
    gw["mla_w_q_b"], _ = _matmul_tn(cqn, dqp, 256, 1024)
    gw["mla_w_kv_b"], _ = _matmul_tn(ckvn, dkvp, 128, 1024)

    dqkv, dbs = [], []
    for (_, d), (bp, bc) in zip(DIL_BRANCHES, biases):
        triple, dbp, dbc = _dil_bwd(qh, kh, proj, o_dil, lse_dil, do_dil, bp, bc, d)
        dqkv.append(triple)
        dbs += [dbp.reshape(DIL_HEADS, -1), dbc.reshape(DIL_HEADS, -1)]
    gs["rel_bias"] = _bias_grad(dbs, onehots)

    dx1, dproj, gs["mix_norm"], dgq, dgk = _in_proj_bwd(dx2, x1, s["mix_norm"], w["w_in"], proj, gq, gk,
                                                        dqkv, dcq, dckv, dkpe)
    gs["dil_q_norm"] = (dgq[:, :DIL_HD] + dgq[:, DIL_HD:]) * DIL_HD ** -0.5
    gs["dil_k_norm"] = dgk[:, :DIL_HD] + dgk[:, DIL_HD:]
    gw["w_in"], _ = _matmul_tn(hm, dproj, 1024, 1024)
    grad_x, got = ffn_grads("ffn1", dx1, x, h1, gate1, up1, ride=comm.scatter(_GROUPS["attn"], gw), scatter_early=True)
    comm.grads_landed(_GROUPS["attn"], got)
    return loss, grad_x, gw, gs


def _position():
    x, y, c = lax.axis_index("x"), lax.axis_index("y"), lax.axis_index("c")
    return x, y, c, 4 * x + 2 * y + c


def _peer(x, y, c, k):
    px = 1 - x if k & 4 else x
    py = 1 - y if k & 2 else y
    pc = 1 - c if k & 1 else c
    return (px, py, pc), 4 * px + 2 * py + pc


class _Ride:
    def __init__(self, arrays, scatter):
        self.arrays, self.scatter = list(arrays), list(scatter)
        self.n = n = len(self.arrays)
        self.specs = [pl.BlockSpec(memory_space=pl.ANY)] * n
        self.out_shape = [jax.ShapeDtypeStruct(a.shape if sc else (N_DEV,) + a.shape, a.dtype)
                          for a, sc in zip(self.arrays, self.scatter)]
        self.scratch = [pltpu.SemaphoreType.DMA((n, N_DEV - 1)), pltpu.SemaphoreType.DMA((n, N_DEV - 1)),
                        pltpu.SemaphoreType.DMA((n,))]

    def _copies(self, ins, outs, sems):
        send_sems, recv_sems, local_sems = sems
        x, y, c, me = _position()
        copies = []
        for a in range(self.n):
            src = ins[a].at[me] if self.scatter[a] else ins[a]
            copies.append(pltpu.make_async_copy(src, outs[a].at[me], local_sems.at[a]))
        for k in range(1, N_DEV):
            peer, peer_idx = _peer(x, y, c, k)
            for a in range(self.n):
                src = ins[a].at[peer_idx] if self.scatter[a] else ins[a]
                copies.append(pltpu.make_async_remote_copy(
                    src_ref=src, dst_ref=outs[a].at[me], send_sem=send_sems.at[a, k - 1], recv_sem=recv_sems.at[a, k - 1],
                    device_id=peer, device_id_type=pl.DeviceIdType.MESH))
        return copies

    def start(self, ins, outs, sems):
        for cp in self._copies(ins, outs, sems):
            cp.start()

    def wait(self, ins, outs, sems):
        for cp in self._copies(ins, outs, sems):
            cp.wait()


def _ride_parts(ride):
    if ride is None:
        return [], [], [], [], []
    return ride.arrays, ride.specs, ride.out_shape, ride.specs, ride.scratch


def _riding(body, n_in, n_out, n_scratch, ride, first, last):
    if ride is None:
        return body
    n = ride.n
    i1, i2 = n_in + n, n_in + n + n_out
    i3, i4 = i2 + n, i2 + n + n_scratch

    def wrapped(*refs):
        ins, outs, sems = refs[n_in:i1], refs[i2:i3], refs[i4:]

        @pl.when(first())
        def _():
            ride.start(ins, outs, sems)

        body(*refs[:n_in], *refs[i1:i2], *refs[i3:i4])

        @pl.when(last())
        def _():
            ride.wait(ins, outs, sems)

    return wrapped


def _exchange(ride, name):
    def body(*refs):
        parts = refs[:ride.n], refs[ride.n:2 * ride.n], refs[2 * ride.n:]
        ride.start(*parts)
        ride.wait(*parts)

    return pl.pallas_call(body, name=name, in_specs=ride.specs, out_specs=ride.specs, out_shape=ride.out_shape,
                          scratch_shapes=ride.scratch)(*ride.arrays)


def _adamw_math(wv, g, m, v):
    m = ADAM_B1 * m + (1.0 - ADAM_B1) * g
    v = ADAM_B2 * v + (1.0 - ADAM_B2) * (g * g)
    m_hat = m / (1.0 - ADAM_B1 ** ADAM_STEP)
    v_hat = v / (1.0 - ADAM_B2 ** ADAM_STEP)
    delta = -ADAM_LR * (m_hat / (jnp.sqrt(v_hat) + ADAM_EPS) + ADAM_WD * wv)
    return delta, m, v


def _adamw(parts, wv, m, v):
    R, C = wv.shape
    tr = max(t for t in range(16, 257, 16) if R % t == 0)

    def body(p_ref, w_ref, m_ref, v_ref, g_ref, d_ref, mo_ref, vo_ref):
        g = p_ref[0].astype(F32)
        for j in range(1, N_DEV):
            g = g + p_ref[j].astype(F32)
        d, mn, vn = _adamw_math(w_ref[...], g, m_ref[...], v_ref[...])
        g_ref[...] = g
        d_ref[...] = d
        mo_ref[...] = mn
        vo_ref[...] = vn

    row = lambda i: (i, 0)
    p_spec = pl.BlockSpec((N_DEV, tr, C), lambda i: (0, i, 0))
    out = jax.ShapeDtypeStruct((R, C), F32)
    return pl.pallas_call(
        body, name="adamw", grid=(R // tr,),
        in_specs=[p_spec, pl.BlockSpec((tr, C), row), pl.BlockSpec((tr, C), row), pl.BlockSpec((tr, C), row)],
        out_specs=[pl.BlockSpec((tr, C), row)] * 4, out_shape=[out] * 4,
        compiler_params=_params(1),
    )(parts, wv, m, v)


_ROW_SHARDED = ("ffn1_w_down", "ffn2_w_down", "w_out")
_GROUPS = {"ffn1": ("ffn1_w_gate", "ffn1_w_up", "ffn1_w_down"),
           "ffn2": ("ffn2_w_gate", "ffn2_w_up", "ffn2_w_down"),
           "attn": ("w_in", "mla_w_q_b", "mla_w_kv_b", "w_out")}
_SMALL = ("ffn1_norm", "mix_norm", "ffn2_norm", "out_norm_dil", "out_norm_mla", "mla_q_a_norm", "rel_bias",
          "mla_q_norm", "mla_k_norm", "mla_kv_a_norm", "dil_q_norm", "dil_k_norm")
_SMALL_ROWS = 48


def _cols_to_full(g):
    return g.transpose(1, 0, 2).reshape(g.shape[1], N_DEV * g.shape[2])


def _full_to_cols(f):
    return f.reshape(f.shape[0], N_DEV, f.shape[1] // N_DEV).transpose(1, 0, 2)


def _to_full(name, g):
    if name in _ROW_SHARDED:
        return g.reshape(-1, g.shape[-1])
    f = _cols_to_full(g)
    if name == "w_in":
        f = jnp.pad(f, ((0, 0), (0, PROJ_PAD - PROJ_COLS)))
    if name == "mla_w_q_b":
        f = jnp.pad(f.reshape(-1, MLA_HEADS, MLA_QK), ((0, 0), (0, 0), (0, MLA_PAD - MLA_QK)))
        f = f.reshape(-1, MLA_HEADS * MLA_PAD)
    return f


def _to_parts(name, f):
    if name in _ROW_SHARDED:
        return f.reshape(N_DEV, -1, f.shape[-1]).astype(BF16)
    if name == "w_in":
        f = f[:, :PROJ_COLS]
    if name == "mla_w_q_b":
        f = f.reshape(-1, MLA_HEADS, MLA_PAD)[:, :, :MLA_QK].reshape(-1, MLA_HEADS * MLA_QK)
    return _full_to_cols(f).astype(BF16)


class _Comm:
    def __init__(self, shards):
        self.shards, self.w, self.recv = shards, {}, {}

    def gather(self, names):
        return _Ride([self.shards[n] for n in names], [False] * len(names))

    def scatter(self, names, grads):
        return _Ride([_to_parts(n, grads[n]) for n in names], [True] * len(names))

    def weights_landed(self, names, got):
        self.w.update({n: _to_full(n, g) for n, g in zip(names, got)})

    def grads_landed(self, names, got):
        self.recv.update(zip(names, got))


def _pack_small(parts, extra):
    flat = jnp.concatenate([parts[n].reshape(-1) for n in _SMALL] + [extra.reshape(-1)])
    return jnp.pad(flat, (0, _SMALL_ROWS * 128 - flat.shape[0])).reshape(_SMALL_ROWS, 128)


def _unpack_small(packed, shapes):
    flat, out, off = packed.reshape(-1), {}, 0
    for n in _SMALL:
        size = math.prod(shapes[n])
        out[n] = flat[off:off + size].reshape(shapes[n])
        off += size
    return out, flat[off]


_NAMES = ("ffn1_norm", "ffn1_w_gate", "ffn1_w_up", "ffn1_w_down", "mix_norm", "w_in", "dil_q_norm", "dil_k_norm",
          "rel_bias", "mla_q_a_norm", "mla_w_q_b", "mla_kv_a_norm", "mla_w_kv_b", "mla_q_norm", "mla_k_norm",
          "out_norm_dil", "out_norm_mla", "w_out", "ffn2_norm", "ffn2_w_gate", "ffn2_w_up", "ffn2_w_down")


def kernel(x, ffn1_norm, ffn1_w_gate, ffn1_w_up, ffn1_w_down, mix_norm, w_in, dil_q_norm, dil_k_norm, rel_bias, mla_q_a_norm, mla_w_q_b, mla_kv_a_norm, mla_w_kv_b, mla_q_norm, mla_k_norm, out_norm_dil, out_norm_mla, w_out, ffn2_norm, ffn2_w_gate, ffn2_w_up, ffn2_w_down, loss_target, m_ffn1_norm, m_ffn1_w_gate, m_ffn1_w_up, m_ffn1_w_down, m_mix_norm, m_w_in, m_dil_q_norm, m_dil_k_norm, m_rel_bias, m_mla_q_a_norm, m_mla_w_q_b, m_mla_kv_a_norm, m_mla_w_kv_b, m_mla_q_norm, m_mla_k_norm, m_out_norm_dil, m_out_norm_mla, m_w_out, m_ffn2_norm, m_ffn2_w_gate, m_ffn2_w_up, m_ffn2_w_down, v_ffn1_norm, v_ffn1_w_gate, v_ffn1_w_up, v_ffn1_w_down, v_mix_norm, v_w_in, v_dil_q_norm, v_dil_k_norm, v_rel_bias, v_mla_q_a_norm, v_mla_w_q_b, v_mla_kv_a_norm, v_mla_w_kv_b, v_mla_q_norm, v_mla_k_norm, v_out_norm_dil, v_out_norm_mla, v_w_out, v_ffn2_norm, v_ffn2_w_gate, v_ffn2_w_up, v_ffn2_w_down):
    args = locals()
    wts = {n: args[n] for n in _NAMES}
    mom = {n: args["m_" + n] for n in _NAMES}
    var = {n: args["v_" + n] for n in _NAMES}

    matrices = [n for group in _GROUPS.values() for n in group]
    comm = _Comm({n: wts[n][0].astype(BF16) for n in matrices})
    comm.weights_landed(_GROUPS["ffn1"], _exchange(comm.gather(_GROUPS["ffn1"]), "gather_first"))
    small = {n: wts[n].reshape(1, -1) if n != "rel_bias" else wts[n] for n in _SMALL}

    loss, grad_x, gw, gs = _local_step(x[0], loss_target[0], small, comm)

    last = comm.scatter(("ffn1_w_up",), gw)
    got = _exchange(_Ride(last.arrays + [_pack_small(gs, loss[0, 0])], last.scatter + [False]), "scatter_last")
    comm.grads_landed(("ffn1_w_up",), got[:-1])

    res = {n: _adamw(comm.recv[n], wts[n][0], mom[n][0], var[n][0]) for n in matrices}
    shapes = {n: wts[n].shape for n in _SMALL}
    zero = jnp.zeros((), F32)
    packed = _adamw(got[-1], _pack_small(wts, zero), _pack_small(mom, zero), _pack_small(var, zero))
    loss_total = None
    for slot, q in enumerate(packed):
        vals, extra = _unpack_small(q, shapes)
        if slot == 0:
            loss_total = extra
        for n in _SMALL:
            res.setdefault(n, [None] * 4)[slot] = vals[n]
    outs = [loss_total, grad_x[None]]
    for slot in range(4):
        outs += [res[n][slot].reshape(wts[n].shape) for n in _NAMES]
    return tuple(outs)
```

```python
import math

import numpy as np
import jax
import jax.numpy as jnp
from jax import lax
from jax.experimental import pallas as pl
from jax.experimental.pallas import tpu as pltpu

F32, BF16 = jnp.float32, jnp.bfloat16
EPS = 1e-6
NEG = -1e30
N_DEV = 8

DIL_HEADS, DIL_HD = 8, 64
DIL_WIDTH = DIL_HEADS * DIL_HD
DIL_BRANCHES = ((128, 1), (512, 4), (2048, 16))
DIL_BLOCK = 128
MLA_HEADS, MLA_NOPE, MLA_ROPE, MLA_V = 4, 128, 64, 128
MLA_QK = MLA_NOPE + MLA_ROPE
MLA_PAD = 256
ROPE_BASE = 10000.0
REL_BUCKETS, REL_MAX_DIST = 32, 2048
PROJ_COLS, PROJ_PAD = 1984, 2048
FFN_RESID = 0.5
ADAM_LR, ADAM_B1, ADAM_B2, ADAM_EPS, ADAM_WD, ADAM_STEP = 0.001, 0.9, 0.999, 1e-08, 0.01, 10
VMEM_LIMIT = 62 * 1024 * 1024

_NT = (((1,), (1,)), ((), ()))
_TN = (((0,), (0,)), ((), ()))


def _dot(a, b):
    return jnp.dot(a, b, preferred_element_type=F32)


def _dot_nt(a, b):
    return lax.dot_general(a, b, _NT, preferred_element_type=F32)


def _dot_tn(a, b):
    return lax.dot_general(a, b, _TN, preferred_element_type=F32)


def _params(n_axes):
    return pltpu.CompilerParams(dimension_semantics=("arbitrary",) * n_axes, vmem_limit_bytes=VMEM_LIMIT)


def _rstd(x, n=None):
    n = x.shape[-1] if n is None else n
    return lax.rsqrt(jnp.sum(x * x, axis=-1, keepdims=True) / n + EPS)


def _rms_bwd(dy, x, g, r, n=None):
    n = x.shape[-1] if n is None else n
    u = dy * g
    dx = r * u - x * (r * r * r) * (jnp.sum(u * x, axis=-1, keepdims=True) / n)
    return dx, dy * x * r


def _sigmoid(x):
    return 1.0 / (1.0 + jnp.exp(-x))


def _split3(x):
    parts = []
    for _ in range(3):
        xb = x.astype(BF16)
        parts.append(xb)
        x = x - xb.astype(F32)
    return parts


def _ffn_fwd(x, gain, wg, wu, wd, ride=None, tm=512, tf=2816):
    T, D = x.shape
    F = wg.shape[1]
    ni, nj = T // tm, F // tf
    r_args, r_in, r_shape, r_out, r_scratch = _ride_parts(ride)

    def body(x_ref, g_ref, wg_ref, wu_ref, wd_ref, xo_ref, h_ref, gate_ref, up_ref, acc):
        j = pl.program_id(1)

        @pl.when(j == 0)
        def _():
            xv = x_ref[...]
            h_ref[...] = (xv * _rstd(xv) * g_ref[...]).astype(BF16)
            acc[...] = jnp.zeros_like(acc)

        h = h_ref[...]
        g = _dot(h, wg_ref[...])
        u = _dot(h, wu_ref[...])
        gate_ref[...] = g.astype(BF16)
        up_ref[...] = u.astype(BF16)
        a = (g * _sigmoid(g) * u).astype(BF16)
        acc[...] += _dot(a, wd_ref[...])

        @pl.when(j == nj - 1)
        def _():
            xo_ref[...] = x_ref[...] + FFN_RESID * acc[...]

    first = lambda: (pl.program_id(0) == 0) & (pl.program_id(1) == 0)
    last = lambda: (pl.program_id(0) == ni - 1) & (pl.program_id(1) == nj - 1)
    outs = pl.pallas_call(
        _riding(body, 5, 4, 1, ride, first, last), name="ffn_fwd", grid=(ni, nj),
        in_specs=[pl.BlockSpec((tm, D), lambda i, j: (i, 0)), pl.BlockSpec((1, D), lambda i, j: (0, 0)),
                  pl.BlockSpec((D, tf), lambda i, j: (0, j)), pl.BlockSpec((D, tf), lambda i, j: (0, j)),
                  pl.BlockSpec((tf, D), lambda i, j: (j, 0))] + r_in,
        out_specs=[pl.BlockSpec((tm, D), lambda i, j: (i, 0)), pl.BlockSpec((tm, D), lambda i, j: (i, 0)),
                   pl.BlockSpec((tm, tf), lambda i, j: (i, j)), pl.BlockSpec((tm, tf), lambda i, j: (i, j))] + r_out,
        out_shape=[jax.ShapeDtypeStruct((T, D), F32), jax.ShapeDtypeStruct((T, D), BF16),
                   jax.ShapeDtypeStruct((T, F), BF16), jax.ShapeDtypeStruct((T, F), BF16)] + r_shape,
        scratch_shapes=[pltpu.VMEM((tm, D), F32)] + r_scratch,
        compiler_params=_params(2),
    )(x, gain, wg, wu, wd, *r_args)
    return outs[:4], outs[4:]


def _ffn_bwd(dy, x, gain, gate, up, wg, wu, wd, ride=None, tm=256, tf=2816):
    T, D = x.shape
    F = wg.shape[1]
    ni, nj = T // tm, F // tf
    r_args, r_in, r_shape, r_out, r_scratch = _ride_parts(ride)

    def body(dy_ref, x_ref, g_ref, gate_ref, up_ref, wg_ref, wu_ref, wd_ref,
             dx_ref, a_ref, dg_ref, du_ref, dyh_ref, dgain_ref, acc):
        i, j = pl.program_id(0), pl.program_id(1)

        @pl.when((i == 0) & (j == 0))
        def _():
            dgain_ref[...] = jnp.zeros_like(dgain_ref)

        @pl.when(j == 0)
        def _():
            dyh_ref[...] = (FFN_RESID * dy_ref[...]).astype(BF16)
            acc[...] = jnp.zeros_like(acc)

        da = _dot_nt(dyh_ref[...], wd_ref[...])
        g = gate_ref[...].astype(F32)
        u = up_ref[...].astype(F32)
        sig = _sigmoid(g)
        s = g * sig
        a_ref[...] = (s * u).astype(BF16)
        dg = (da * u * (sig * (1.0 + g * (1.0 - sig)))).astype(BF16)
        du = (da * s).astype(BF16)
        dg_ref[...] = dg
        du_ref[...] = du
        acc[...] += _dot_nt(dg, wg_ref[...]) + _dot_nt(du, wu_ref[...])

        @pl.when(j == nj - 1)
        def _():
            xv = x_ref[...]
            dxn, dgc = _rms_bwd(acc[...], xv, g_ref[...], _rstd(xv))
            dx_ref[...] = dy_ref[...] + dxn
            dgain_ref[...] += jnp.sum(dgc, axis=0, keepdims=True)

    first = lambda: (pl.program_id(0) == 0) & (pl.program_id(1) == 0)
    last = lambda: (pl.program_id(0) == ni - 1) & (pl.program_id(1) == nj - 1)
    outs = pl.pallas_call(
        _riding(body, 8, 6, 1, ride, first, last), name="ffn_bwd", grid=(ni, nj),
        in_specs=[pl.BlockSpec((tm, D), lambda i, j: (i, 0)), pl.BlockSpec((tm, D), lambda i, j: (i, 0)),
                  pl.BlockSpec((1, D), lambda i, j: (0, 0)),
                  pl.BlockSpec((tm, tf), lambda i, j: (i, j)), pl.BlockSpec((tm, tf), lambda i, j: (i, j)),
                  pl.BlockSpec((D, tf), lambda i, j: (0, j)), pl.BlockSpec((D, tf), lambda i, j: (0, j)),
                  pl.BlockSpec((tf, D), lambda i, j: (j, 0))] + r_in,
        out_specs=[pl.BlockSpec((tm, D), lambda i, j: (i, 0)),
                   pl.BlockSpec((tm, tf), lambda i, j: (i, j)), pl.BlockSpec((tm, tf), lambda i, j: (i, j)),
                   pl.BlockSpec((tm, tf), lambda i, j: (i, j)),
                   pl.BlockSpec((tm, D), lambda i, j: (i, 0)), pl.BlockSpec((1, D), lambda i, j: (0, 0))] + r_out,
        out_shape=[jax.ShapeDtypeStruct((T, D), F32), jax.ShapeDtypeStruct((T, F), BF16),
                   jax.ShapeDtypeStruct((T, F), BF16), jax.ShapeDtypeStruct((T, F), BF16),
                   jax.ShapeDtypeStruct((T, D), BF16), jax.ShapeDtypeStruct((1, D), F32)] + r_shape,
        scratch_shapes=[pltpu.VMEM((tm, D), F32)] + r_scratch,
        compiler_params=_params(2),
    )(dy, x, gain, gate, up, wg, wu, wd, *r_args)
    return outs[:6], outs[6:]


def _matmul_tn(a, b, tk, tn, ride=None, tt=512):
    T, K = a.shape
    N = b.shape[1]
    tk, tn = min(tk, K), min(tn, N)
    grid = (K // tk, N // tn, T // tt)
    r_args, r_in, r_shape, r_out, r_scratch = _ride_parts(ride)

    def body(a_ref, b_ref, o_ref):
        @pl.when(pl.program_id(2) == 0)
        def _():
            o_ref[...] = jnp.zeros_like(o_ref)

        o_ref[...] += _dot_tn(a_ref[...].astype(BF16), b_ref[...].astype(BF16))

    first = lambda: (pl.program_id(0) == 0) & (pl.program_id(1) == 0) & (pl.program_id(2) == 0)
    last = lambda: ((pl.program_id(0) == grid[0] - 1) & (pl.program_id(1) == grid[1] - 1)
                    & (pl.program_id(2) == grid[2] - 1))
    outs = pl.pallas_call(
        _riding(body, 2, 1, 0, ride, first, last), name="matmul_tn", grid=grid,
        in_specs=[pl.BlockSpec((tt, tk), lambda k, n, t: (t, k)), pl.BlockSpec((tt, tn), lambda k, n, t: (t, n))] + r_in,
        out_specs=[pl.BlockSpec((tk, tn), lambda k, n, t: (k, n))] + r_out,
        out_shape=[jax.ShapeDtypeStruct((K, N), F32)] + r_shape,
        scratch_shapes=r_scratch,
        compiler_params=_params(3),
    )(a, b, *r_args)
    return outs[0], outs[1:]


def _loss_grad(y, target, tm=512):
    T, D = y.shape

    def body(y_ref, t_ref, dy_ref, loss_ref):
        @pl.when(pl.program_id(0) == 0)
        def _():
            loss_ref[...] = jnp.zeros_like(loss_ref)

        e = y_ref[...] - t_ref[...]
        dy_ref[...] = e * (1.0 / D)
        loss_ref[...] += (0.5 / D) * jnp.sum(e * e)

    return pl.pallas_call(
        body, name="loss_grad", grid=(T // tm,),
        in_specs=[pl.BlockSpec((tm, D), lambda i: (i, 0)), pl.BlockSpec((tm, D), lambda i: (i, 0))],
        out_specs=[pl.BlockSpec((tm, D), lambda i: (i, 0)), pl.BlockSpec((1, 128), lambda i: (0, 0))],
        out_shape=[jax.ShapeDtypeStruct((T, D), F32), jax.ShapeDtypeStruct((1, 128), F32)],
        compiler_params=_params(1),
    )(y, target)


def _in_proj(x, gain, w, gq, gk, tm=512):
    T, D = x.shape
    N = w.shape[1]
    W = DIL_WIDTH

    def body(x_ref, g_ref, w_ref, gq_ref, gk_ref, h_ref, p_ref, qh_ref, kh_ref):
        xv = x_ref[...]
        h = (xv * _rstd(xv) * g_ref[...]).astype(BF16)
        h_ref[...] = h
        p_ref[...] = _dot(h, w_ref[...])
        lo = lax.broadcasted_iota(jnp.int32, (tm, 128), 1) < DIL_HD
        for hp in range(DIL_HEADS // 2):
            q = p_ref[:, 128 * hp:128 * (hp + 1)]
            k = p_ref[:, W + 128 * hp:W + 128 * (hp + 1)]
            qh_ref[:, 128 * hp:128 * (hp + 1)] = (q * _pair_rstd(q, lo) * gq_ref[...]).astype(BF16).astype(F32)
            kh_ref[:, 128 * hp:128 * (hp + 1)] = (k * _pair_rstd(k, lo) * gk_ref[...]).astype(BF16).astype(F32)

    row = lambda i: (i, 0)
    fix = lambda i: (0, 0)
    return pl.pallas_call(
        body, name="in_proj", grid=(T // tm,),
        in_specs=[pl.BlockSpec((tm, D), row), pl.BlockSpec((1, D), fix), pl.BlockSpec((D, N), fix),
                  pl.BlockSpec((1, 128), fix), pl.BlockSpec((1, 128), fix)],
        out_specs=[pl.BlockSpec((tm, D), row), pl.BlockSpec((tm, N), row), pl.BlockSpec((tm, W), row),
                   pl.BlockSpec((tm, W), row)],
        out_shape=[jax.ShapeDtypeStruct((T, D), BF16), jax.ShapeDtypeStruct((T, N), F32),
                   jax.ShapeDtypeStruct((T, W), F32), jax.ShapeDtypeStruct((T, W), F32)],
        compiler_params=_params(1),
    )(x, gain, w, gq, gk)


def _in_proj_bwd(dx_up, x, gain, w, proj, gq, gk, dqkv, dcq, dckv, dkpe, tm=512):
    T, D = x.shape
    N = w.shape[1]
    W = DIL_WIDTH
    nb = len(dqkv)

    def body(*refs):
        dxu_ref, x_ref, g_ref, w_ref, q_ref, k_ref, gq_ref, gk_ref = refs[:8]
        dil_refs = refs[8:8 + 3 * nb]
        dcq_ref, dckv_ref, dkpe_ref, dx_ref, dp_ref, dgain_ref, dgq_ref, dgk_ref = refs[8 + 3 * nb:]

        @pl.when(pl.program_id(0) == 0)
        def _():
            for ref in (dgain_ref, dgq_ref, dgk_ref):
                ref[...] = jnp.zeros_like(ref)

        lo = lax.broadcasted_iota(jnp.int32, (tm, 128), 1) < DIL_HD
        norms = ((q_ref, gq_ref, dgq_ref), (k_ref, gk_ref, dgk_ref))
        for part in range(3):
            acc = dil_refs[part][...]
            for b in range(1, nb):
                acc = acc + dil_refs[3 * b + part][...]
            if part == 2:
                dp_ref[:, 2 * W:3 * W] = acc.astype(BF16)
                continue
            raw_ref, gn_ref, dgn_ref = norms[part]
            for hp in range(DIL_HEADS // 2):
                raw = raw_ref[:, 128 * hp:128 * (hp + 1)]
                d_raw, dgn = _pair_rms_bwd(acc[:, 128 * hp:128 * (hp + 1)], raw, _pair_rstd(raw, lo), gn_ref[...], lo)
                dp_ref[:, part * W + 128 * hp:part * W + 128 * (hp + 1)] = d_raw.astype(BF16)
                dgn_ref[...] += dgn
        dp_ref[:, 3 * W:3 * W + 256] = dcq_ref[...].astype(BF16)
        dp_ref[:, 3 * W + 256:3 * W + 384] = dckv_ref[...].astype(BF16)
        dp_ref[:, 3 * W + 384:N] = dkpe_ref[...].astype(BF16)
        dh = _dot_nt(dp_ref[...], w_ref[...])
        xv = x_ref[...]
        dxn, dgc = _rms_bwd(dh, xv, g_ref[...], _rstd(xv))
        dx_ref[...] = dxu_ref[...] + dxn
        dgain_ref[...] += jnp.sum(dgc, axis=0, keepdims=True)

    row = lambda i: (i, 0)
    fix = lambda i: (0, 0)
    return pl.pallas_call(
        body, name="in_proj_bwd", grid=(T // tm,),
        in_specs=[pl.BlockSpec((tm, D), row), pl.BlockSpec((tm, D), row), pl.BlockSpec((1, D), fix),
                  pl.BlockSpec((D, N), fix), pl.BlockSpec((tm, W), row), pl.BlockSpec((tm, W), lambda i: (i, 1)),
                  pl.BlockSpec((1, 128), fix), pl.BlockSpec((1, 128), fix)] + [pl.BlockSpec((tm, W), row)] * (3 * nb)
                 + [pl.BlockSpec((tm, 256), row), pl.BlockSpec((tm, 128), row), pl.BlockSpec((tm, 128), row)],
        out_specs=[pl.BlockSpec((tm, D), row), pl.BlockSpec((tm, N), row), pl.BlockSpec((1, D), fix),
                   pl.BlockSpec((1, 128), fix), pl.BlockSpec((1, 128), fix)],
        out_shape=[jax.ShapeDtypeStruct((T, D), F32), jax.ShapeDtypeStruct((T, N), BF16),
                   jax.ShapeDtypeStruct((1, D), F32), jax.ShapeDtypeStruct((1, 128), F32),
                   jax.ShapeDtypeStruct((1, 128), F32)],
        compiler_params=_params(1),
    )(dx_up, x, gain, w, proj, proj, gq, gk, *[a for triple in dqkv for a in triple], dcq, dckv, dkpe)


def _out_proj(x, o_dil, o_mla, g_dil, g_mla, w, tm=512):
    T, D = x.shape
    W = o_dil.shape[1]

    def body(x_ref, od_ref, om_ref, gd_ref, gm_ref, w_ref, xo_ref, oc_ref):
        od, om = od_ref[...], om_ref[...]
        oc_ref[:, 0:W] = (od * _rstd(od) * gd_ref[...]).astype(BF16)
        oc_ref[:, W:2 * W] = (om * _rstd(om) * gm_ref[...]).astype(BF16)
        xo_ref[...] = x_ref[...] + _dot(oc_ref[...], w_ref[...])

    row = lambda i: (i, 0)
    fix = lambda i: (0, 0)
    return pl.pallas_call(
        body, name="out_proj", grid=(T // tm,),
        in_specs=[pl.BlockSpec((tm, D), row), pl.BlockSpec((tm, W), row), pl.BlockSpec((tm, W), row),
                  pl.BlockSpec((1, W), fix), pl.BlockSpec((1, W), fix), pl.BlockSpec((2 * W, D), fix)],
        out_specs=[pl.BlockSpec((tm, D), row), pl.BlockSpec((tm, 2 * W), row)],
        out_shape=[jax.ShapeDtypeStruct((T, D), F32), jax.ShapeDtypeStruct((T, 2 * W), BF16)],
        compiler_params=_params(1),
    )(x, o_dil, o_mla, g_dil, g_mla, w)


def _out_proj_bwd(dx, o_dil, o_mla, g_dil, g_mla, w, tm=512):
    T, D = dx.shape
    W = o_dil.shape[1]

    def body(dx_ref, od_ref, om_ref, gd_ref, gm_ref, w_ref, dod_ref, dom_ref, dgd_ref, dgm_ref):
        @pl.when(pl.program_id(0) == 0)
        def _():
            dgd_ref[...] = jnp.zeros_like(dgd_ref)
            dgm_ref[...] = jnp.zeros_like(dgm_ref)

        doc = _dot_nt(dx_ref[...].astype(BF16), w_ref[...])
        od, om = od_ref[...], om_ref[...]
        dod, dgd = _rms_bwd(doc[:, 0:W], od, gd_ref[...], _rstd(od))
        dom, dgm = _rms_bwd(doc[:, W:2 * W], om, gm_ref[...], _rstd(om))
        dod_ref[...] = dod
        dom_ref[...] = dom
        dgd_ref[...] += jnp.sum(dgd, axis=0, keepdims=True)
        dgm_ref[...] += jnp.sum(dgm, axis=0, keepdims=True)

    row = lambda i: (i, 0)
    fix = lambda i: (0, 0)
    return pl.pallas_call(
        body, name="out_proj_bwd", grid=(T // tm,),
        in_specs=[pl.BlockSpec((tm, D), row), pl.BlockSpec((tm, W), row), pl.BlockSpec((tm, W), row),
                  pl.BlockSpec((1, W), fix), pl.BlockSpec((1, W), fix), pl.BlockSpec((2 * W, D), fix)],
        out_specs=[pl.BlockSpec((tm, W), row), pl.BlockSpec((tm, W), row),
                   pl.BlockSpec((1, W), fix), pl.BlockSpec((1, W), fix)],
        out_shape=[jax.ShapeDtypeStruct((T, W), F32), jax.ShapeDtypeStruct((T, W), F32),
                   jax.ShapeDtypeStruct((1, W), F32), jax.ShapeDtypeStruct((1, W), F32)],
        compiler_params=_params(1),
    )(dx, o_dil, o_mla, g_dil, g_mla, w)


def _pair_rstd(x, lo):
    sq = x * x
    s0 = jnp.sum(jnp.where(lo, sq, 0.0), axis=-1, keepdims=True)
    s1 = jnp.sum(jnp.where(lo, 0.0, sq), axis=-1, keepdims=True)
    return jnp.where(lo, lax.rsqrt(s0 / DIL_HD + EPS), lax.rsqrt(s1 / DIL_HD + EPS))


def _pair_rms_bwd(dn, x, r, g, lo):
    u = dn * g
    t = u * x
    d0 = jnp.sum(jnp.where(lo, t, 0.0), axis=-1, keepdims=True)
    d1 = jnp.sum(jnp.where(lo, 0.0, t), axis=-1, keepdims=True)
    dx = r * u - x * (r * r * r) * (jnp.where(lo, d0, d1) / DIL_HD)
    return dx, jnp.sum(dn * x * r, axis=0, keepdims=True)


def _pair_col(x, lo, e):
    sel = lo if e == 0 else jnp.logical_not(lo)
    return jnp.max(jnp.where(sel, x, NEG), axis=-1, keepdims=True)


def _dil_masks(n):
    lo = lax.broadcasted_iota(jnp.int32, (DIL_BLOCK, DIL_BLOCK), 1) < DIL_HD
    row = lax.broadcasted_iota(jnp.int32, (2 * DIL_BLOCK, 2 * DIL_BLOCK), 0) % DIL_BLOCK
    col = lax.broadcasted_iota(jnp.int32, (2 * DIL_BLOCK, 2 * DIL_BLOCK), 1)
    prev = jnp.logical_and(jnp.logical_and(col < DIL_BLOCK, col >= row), n > 0)
    cur = jnp.logical_and(col >= DIL_BLOCK, col - DIL_BLOCK <= row)
    return lo, jnp.logical_or(prev, cur)


def _stack_heads(x, lo):
    return jnp.concatenate([jnp.where(lo, x, 0.0), jnp.where(lo, 0.0, x)], axis=0)


def _unstack_heads(x2, lo):
    return jnp.where(lo, x2[:DIL_BLOCK], x2[DIL_BLOCK:])


def _dil_pairs(d):
    return 4 if d == 1 else 1


def _sub_rows(r, d):
    return pl.ds(r, DIL_BLOCK, stride=d) if d > 1 else pl.ds(0, DIL_BLOCK)


def _split_subsequences(loads, d, P):
    for r in range(d):
        for p in range(P):
            for block, scratch, part in loads:
                piece = block[_sub_rows(r, d), pl.ds(128 * p, 128)]
                if part is None:
                    scratch[r * P + p] = piece
                else:
                    scratch[r * P + p, pl.ds(DIL_BLOCK * part, DIL_BLOCK), :] = piece


def _merge_subsequences(pairs, d, P):
    for r in range(d):
        for p in range(P):
            for block, scratch in pairs:
                block[_sub_rows(r, d), pl.ds(128 * p, 128)] = scratch[r * P + p]


def _dil_fwd(qh, kh, proj, bias, d, prev):
    T = proj.shape[0]
    P = _dil_pairs(d)
    rows, cw, n_it = DIL_BLOCK * d, 128 * P, d * P
    nblk = T // rows
    has_prev = prev is not None

    def body(*refs):
        q_ref, kp_ref, kc_ref, vp_ref, vc_ref, bias_ref = refs[:6]
        refs = refs[6:]
        if has_prev:
            oin_ref, lin_ref = refs[:2]
            refs = refs[2:]
        o_ref, l_ref, qs, ks, vs, os_, ls_ = refs[:7]
        pb, n = pl.program_id(0), pl.program_id(1)
        lo, valid = _dil_masks(n)
        loads = [(q_ref, qs, None), (kp_ref, ks, 0), (kc_ref, ks, 1), (vp_ref, vs, 0), (vc_ref, vs, 1)]
        if has_prev:
            ois, lis = refs[7:]
            loads += [(oin_ref, ois, None), (lin_ref, lis, None)]
        _split_subsequences(loads, d, P)

        def step(i, carry):
            q2 = _stack_heads(qs[i], lo).astype(BF16)
            s = jnp.where(valid, _dot_nt(q2, ks[i].astype(BF16)) + bias_ref[pb * P + i % P], NEG)
            m = jnp.max(s, axis=-1, keepdims=True)
            p = jnp.exp(s - m)
            l = jnp.sum(p, axis=-1, keepdims=True)
            o = _unstack_heads(_dot(p.astype(BF16), vs[i].astype(BF16)) / l, lo)
            lse = _unstack_heads(jnp.broadcast_to(m + jnp.log(l), (2 * DIL_BLOCK, 128)), lo)
            if has_prev:
                lin = lis[i]
                mx = jnp.maximum(lin, lse)
                lnew = mx + jnp.log(jnp.exp(lin - mx) + jnp.exp(lse - mx))
                o = ois[i] * jnp.exp(lin - lnew) + o * jnp.exp(lse - lnew)
                lse = lnew
            os_[i] = o
            ls_[i] = lse
            return carry

        lax.fori_loop(0, n_it, step, 0, unroll=4)
        _merge_subsequences([(o_ref, os_), (l_ref, ls_)], d, P)

    blk = (rows, cw)
    vcol = 2 * DIL_WIDTH // cw
    prev_n = lambda n: jnp.maximum(n - 1, 0)
    fix3 = lambda pb, n: (0, 0, 0)
    tok = pl.BlockSpec(blk, lambda pb, n: (n, pb))
    tok_prev = pl.BlockSpec(blk, lambda pb, n: (prev_n(n), pb))
    in_specs = [tok, tok_prev, tok,
                pl.BlockSpec(blk, lambda pb, n: (prev_n(n), vcol + pb)), pl.BlockSpec(blk, lambda pb, n: (n, vcol + pb)),
                pl.BlockSpec((DIL_HEADS // 2, 2 * DIL_BLOCK, 2 * DIL_BLOCK), fix3)]
    args = [qh, kh, kh, proj, proj, bias]
    one, two = pltpu.VMEM((n_it, DIL_BLOCK, 128), F32), pltpu.VMEM((n_it, 2 * DIL_BLOCK, 128), F32)
    scratch = [one, two, two, one, one]
    if has_prev:
        in_specs += [tok, tok]
        args += list(prev)
        scratch += [one, one]
    out = jax.ShapeDtypeStruct((T, DIL_WIDTH), F32)
    return pl.pallas_call(
        body, name=f"dil_fwd_d{d}", grid=(DIL_HEADS // 2 // P, nblk), in_specs=in_specs, out_specs=[tok, tok],
        out_shape=[out, out], scratch_shapes=scratch, compiler_params=_params(2),
    )(*args)


def _dil_bwd(qh, kh, proj, o, lse, do, bias, d):
    T = proj.shape[0]
    P = _dil_pairs(d)
    rows, cw, n_it = DIL_BLOCK * d, 128 * P, d * P
    nblk = T // rows

    def body(q_ref, kp_ref, kc_ref, vp_ref, vc_ref, o_ref, l_ref, do_ref, bias_ref,
             dq_ref, dk_ref, dv_ref, db_ref,
             qs, ks, vs, os_, ls_, dos, dqs, dks, dvs, ck, cv):
        pb, n = pl.program_id(0), pl.program_id(1)
        lo, valid = _dil_masks(n)

        @pl.when((pb == 0) & (n == 0))
        def _():
            db_ref[...] = jnp.zeros_like(db_ref)

        @pl.when(n == 0)
        def _():
            ck[...] = jnp.zeros_like(ck)
            cv[...] = jnp.zeros_like(cv)

        _split_subsequences([(q_ref, qs, None), (kp_ref, ks, 0), (kc_ref, ks, 1), (vp_ref, vs, 0), (vc_ref, vs, 1),
                             (o_ref, os_, None), (l_ref, ls_, None), (do_ref, dos, None)], d, P)

        def step(i, carry):
            pair = pb * P + i % P
            q2 = _stack_heads(qs[i], lo).astype(BF16)
            kcat, vcat = ks[i].astype(BF16), vs[i].astype(BF16)
            dov = dos[i]
            do2 = _stack_heads(dov, lo).astype(BF16)
            delta = jnp.sum(_stack_heads(dov * os_[i], lo), axis=-1, keepdims=True)
            lse_pair = ls_[i]
            lse2 = jnp.concatenate([_pair_col(lse_pair, lo, 0), _pair_col(lse_pair, lo, 1)], axis=0)
            s = jnp.where(valid, _dot_nt(q2, kcat) + bias_ref[pair], NEG)
            p = jnp.exp(s - lse2)
            ds = p * (_dot_nt(do2, vcat) - delta)
            db_ref[pair] += ds
            dsb = ds.astype(BF16)
            dqs[i] = _unstack_heads(_dot(dsb, kcat), lo)
            dk2 = _dot_tn(dsb, q2)
            dv2 = _dot_tn(p.astype(BF16), do2)
            dks[i] = ck[i] + dk2[:DIL_BLOCK]
            dvs[i] = cv[i] + dv2[:DIL_BLOCK]
            ck[i] = dk2[DIL_BLOCK:]
            cv[i] = dv2[DIL_BLOCK:]
            return carry

        @pl.when(n < nblk)
        def _():
            lax.fori_loop(0, n_it, step, 0, unroll=2)
            _merge_subsequences([(dq_ref, dqs), (dk_ref, dks), (dv_ref, dvs)], d, P)

        @pl.when(n == nblk)
        def _():
            _merge_subsequences([(dk_ref, ck), (dv_ref, cv)], d, P)

    blk = (rows, cw)
    vcol = 2 * DIL_WIDTH // cw
    qn_ = lambda n: jnp.minimum(n, nblk - 1)
    pn_ = lambda n: jnp.maximum(n - 1, 0)
    fix3 = lambda pb, n: (0, 0, 0)
    tok_q = pl.BlockSpec(blk, lambda pb, n: (qn_(n), pb))
    tok_p = pl.BlockSpec(blk, lambda pb, n: (pn_(n), pb))
    bias_spec = pl.BlockSpec((DIL_HEADS // 2, 2 * DIL_BLOCK, 2 * DIL_BLOCK), fix3)
    in_specs = [tok_q, tok_p, tok_q,
                pl.BlockSpec(blk, lambda pb, n: (pn_(n), vcol + pb)), pl.BlockSpec(blk, lambda pb, n: (qn_(n), vcol + pb)),
                tok_q, tok_q, tok_q, bias_spec]
    tok_shape = jax.ShapeDtypeStruct((T, DIL_WIDTH), F32)
    one, two = pltpu.VMEM((n_it, DIL_BLOCK, 128), F32), pltpu.VMEM((n_it, 2 * DIL_BLOCK, 128), F32)
    dq, dk, dv, db = pl.pallas_call(
        body, name=f"dil_bwd_d{d}", grid=(DIL_HEADS // 2 // P, nblk + 1), in_specs=in_specs,
        out_specs=[tok_q, tok_p, tok_p, bias_spec],
        out_shape=[tok_shape, tok_shape, tok_shape, jax.ShapeDtypeStruct(bias.shape, F32)],
        scratch_shapes=[one, two, two] + [one] * 8,
        compiler_params=_params(2),
    )(qh, kh, kh, proj, proj, o, lse, do, bias)
    return (dq, dk, dv), db


def _t5_bucket(dist):
    max_exact = REL_BUCKETS // 2
    dd = np.maximum(dist, 1).astype(np.float32)
    large = max_exact + (np.log(dd / max_exact) / np.log(REL_MAX_DIST / max_exact)
                         * (REL_BUCKETS - max_exact)).astype(np.int32)
    large = np.minimum(large, REL_BUCKETS - 1)
    return np.where(dist < max_exact, dist, large).astype(np.int32)


def _bucket_onehots():
    i = np.arange(DIL_BLOCK)[:, None]
    j = np.arange(DIL_BLOCK)[None, :]
    out = []
    for _, d in DIL_BRANCHES:
        dist = np.concatenate([DIL_BLOCK + i - j, i - j], axis=1)
        bucket = _t5_bucket(np.clip(dist, 0, None) * d).reshape(-1)
        out.append(jnp.asarray(np.eye(REL_BUCKETS, dtype=np.float32)[:, bucket], BF16))
    return out


def _bias_tables(rel_bias, onehots):
    n = len(onehots)

    def body(rb_ref, *refs):
        parts = _split3(rb_ref[...])
        for k in range(n):
            oh = refs[k][...]
            refs[n + k][...] = _dot(parts[0], oh) + _dot(parts[1], oh) + _dot(parts[2], oh)

    flat = pl.pallas_call(
        body, name="bias_tables",
        out_shape=[jax.ShapeDtypeStruct((DIL_HEADS, 2 * DIL_BLOCK * DIL_BLOCK), F32)] * n,
        compiler_params=pltpu.CompilerParams(vmem_limit_bytes=VMEM_LIMIT),
    )(rel_bias, *onehots)
    return [t.reshape(DIL_HEADS // 2, 2 * DIL_BLOCK, 2 * DIL_BLOCK) for t in flat]


def _bias_grad(dbs, onehots):
    n = len(dbs)
    dbs = [t.reshape(DIL_HEADS, 2 * DIL_BLOCK * DIL_BLOCK) for t in dbs]

    def body(*refs):
        acc = jnp.zeros((DIL_HEADS, REL_BUCKETS), F32)
        for k in range(n):
            oh = refs[n + k][...]
            for part in _split3(refs[k][...]):
                acc = acc + _dot_nt(part, oh)
        refs[-1][...] = acc

    return pl.pallas_call(
        body, name="bias_grad",
        out_shape=jax.ShapeDtypeStruct((DIL_HEADS, REL_BUCKETS), F32),
        compiler_params=pltpu.CompilerParams(vmem_limit_bytes=VMEM_LIMIT),
    )(*dbs, *onehots)


def _swap_halves(x):
    lane = lax.broadcasted_iota(jnp.int32, x.shape, 1)
    first = (lane % 64) < 32
    return jnp.where(first, pltpu.roll(x, 96, 1), pltpu.roll(x, 32, 1))


def _rope_tables(T):
    pos = jnp.arange(T, dtype=F32)
    inv_freq = ROPE_BASE ** (-jnp.arange(0, MLA_ROPE, 2, dtype=F32) / MLA_ROPE)
    ang = pos[:, None] * inv_freq[None, :]
    z = jnp.zeros((T, 128 - MLA_ROPE), F32)
    cos = jnp.concatenate([jnp.cos(ang), jnp.cos(ang), z], axis=-1)
    sin = jnp.concatenate([-jnp.sin(ang), jnp.sin(ang), z], axis=-1)
    return cos, sin


def _mla_prep(proj, cos, sin, g_qa, g_kva, g_q, g_k, wq, wkv, tm=512):
    T = proj.shape[0]
    H = MLA_HEADS
    scale = MLA_QK ** -0.5

    def body(cq_ref, ckv_ref, kpe_ref, cos_ref, sin_ref, gqa_ref, gkva_ref, gq_ref, gk_ref, wq_ref, wkv_ref,
             q_ref, k_ref, v_ref):
        cosv, sinv = cos_ref[...], sin_ref[...]

        def rope(x):
            return x * cosv + _swap_halves(x) * sinv

        cq = cq_ref[...]
        qp = _dot((cq * _rstd(cq) * gqa_ref[...]).astype(BF16), wq_ref[...])
        ckv = ckv_ref[...]
        kvp = _dot((ckv * _rstd(ckv) * gkva_ref[...]).astype(BF16), wkv_ref[...])
        kpe = kpe_ref[...]
        for h in range(H):
            a = qp[:, MLA_PAD * h:MLA_PAD * (h + 1)]
            qn = a * _rstd(a, MLA_QK) * gq_ref[...]
            q_ref[h, :, 0:128] = (qn[:, 0:128] * scale).astype(BF16)
            q_ref[h, :, 128:256] = (rope(qn[:, 128:256]) * scale).astype(BF16)
            kn = kvp[:, MLA_PAD * h:MLA_PAD * h + 128]
            r = lax.rsqrt((jnp.sum(kn * kn, axis=-1, keepdims=True)
                           + jnp.sum(kpe * kpe, axis=-1, keepdims=True)) / MLA_QK + EPS)
            k_ref[h, :, 0:128] = (kn * r * gk_ref[:, 0:128]).astype(BF16)
            k_ref[h, :, 128:256] = rope(kpe * r * gk_ref[:, 128:256]).astype(BF16)
            v_ref[h] = kvp[:, MLA_PAD * h + 128:MLA_PAD * (h + 1)].astype(BF16)

    fix = lambda i: (0, 0)
    return pl.pallas_call(
        body, name="mla_prep", grid=(T // tm,),
        in_specs=[pl.BlockSpec((tm, 256), lambda i: (i, 6)), pl.BlockSpec((tm, 128), lambda i: (i, 14)),
                  pl.BlockSpec((tm, 128), lambda i: (i, 15)),
                  pl.BlockSpec((tm, 128), lambda i: (i, 0)), pl.BlockSpec((tm, 128), lambda i: (i, 0)),
                  pl.BlockSpec((1, 256), fix), pl.BlockSpec((1, 128), fix),
                  pl.BlockSpec((1, 256), fix), pl.BlockSpec((1, 256), fix),
                  pl.BlockSpec((256, H * MLA_PAD), fix), pl.BlockSpec((128, H * MLA_PAD), fix)],
        out_specs=[pl.BlockSpec((H, tm, MLA_PAD), lambda i: (0, i, 0)), pl.BlockSpec((H, tm, MLA_PAD), lambda i: (0, i, 0)),
                   pl.BlockSpec((H, tm, MLA_V), lambda i: (0, i, 0))],
        out_shape=[jax.ShapeDtypeStruct((H, T, MLA_PAD), BF16), jax.ShapeDtypeStruct((H, T, MLA_PAD), BF16),
                   jax.ShapeDtypeStruct((H, T, MLA_V), BF16)],
        compiler_params=_params(1),
    )(proj, proj, proj, cos, sin, g_qa, g_kva, g_q, g_k, wq, wkv)


def _mla_prep_bwd(proj, cos, sin, g_qa, g_kva, g_q, g_k, wq, wkv, dq, dk, dv, tm=512):
    T = proj.shape[0]
    H = MLA_HEADS
    scale = MLA_QK ** -0.5

    def body(cq_ref, ckv_ref, kpe_ref, cos_ref, sin_ref, gqa_ref, gkva_ref, gq_ref, gk_ref, wq_ref, wkv_ref,
             dq_ref, dk_ref, dv_ref,
             dcq_ref, dckv_ref, dkpe_ref, cqn_ref, ckvn_ref, dqp_ref, dkvp_ref,
             dgqa_ref, dgkva_ref, dgq_ref, dgk_ref):
        @pl.when(pl.program_id(0) == 0)
        def _():
            for ref in (dgqa_ref, dgkva_ref, dgq_ref, dgk_ref):
                ref[...] = jnp.zeros_like(ref)

        cosv, sinv = cos_ref[...], sin_ref[...]

        def rope_bwd(dy):
            return dy * cosv + _swap_halves(dy * sinv)

        cq = cq_ref[...]
        rcq = _rstd(cq)
        cqn = (cq * rcq * gqa_ref[...]).astype(BF16)
        cqn_ref[...] = cqn
        qp = _dot(cqn, wq_ref[...])
        ckv = ckv_ref[...]
        rckv = _rstd(ckv)
        ckvn = (ckv * rckv * gkva_ref[...]).astype(BF16)
        ckvn_ref[...] = ckvn
        kvp = _dot(ckvn, wkv_ref[...])
        kpe = kpe_ref[...]
        dkpe = jnp.zeros_like(kpe)
        dgq = jnp.zeros((1, MLA_PAD), F32)
        dgk = jnp.zeros((1, MLA_PAD), F32)
        for h in range(H):
            a = qp[:, MLA_PAD * h:MLA_PAD * (h + 1)]
            dqh = dq_ref[h]
            dn = jnp.concatenate([dqh[:, 0:128], rope_bwd(dqh[:, 128:256])], axis=-1) * scale
            da, dg = _rms_bwd(dn, a, gq_ref[...], _rstd(a, MLA_QK), MLA_QK)
            dgq = dgq + jnp.sum(dg, axis=0, keepdims=True)
            dqp_ref[:, MLA_PAD * h:MLA_PAD * (h + 1)] = da.astype(BF16)

            ak = jnp.concatenate([kvp[:, MLA_PAD * h:MLA_PAD * h + 128], kpe], axis=-1)
            dkh = dk_ref[h]
            dnk = jnp.concatenate([dkh[:, 0:128], rope_bwd(dkh[:, 128:256])], axis=-1)
            dak, dg = _rms_bwd(dnk, ak, gk_ref[...], _rstd(ak, MLA_QK), MLA_QK)
            dgk = dgk + jnp.sum(dg, axis=0, keepdims=True)
            dkpe = dkpe + dak[:, 128:256]
            dkvp_ref[:, MLA_PAD * h:MLA_PAD * h + 128] = dak[:, 0:128].astype(BF16)
            dkvp_ref[:, MLA_PAD * h + 128:MLA_PAD * (h + 1)] = dv_ref[h].astype(BF16)
        dkpe_ref[...] = dkpe
        dgq_ref[...] += dgq
        dgk_ref[...] += dgk
        dcq, dg = _rms_bwd(_dot_nt(dqp_ref[...], wq_ref[...]), cq, gqa_ref[...], rcq)
        dcq_ref[...] = dcq
        dgqa_ref[...] += jnp.sum(dg, axis=0, keepdims=True)
        dckv, dg = _rms_bwd(_dot_nt(dkvp_ref[...], wkv_ref[...]), ckv, gkva_ref[...], rckv)
        dckv_ref[...] = dckv
        dgkva_ref[...] += jnp.sum(dg, axis=0, keepdims=True)

    fix = lambda i: (0, 0)
    row = lambda i: (i, 0)
    head = lambda i: (0, i, 0)
    return pl.pallas_call(
        body, name="mla_prep_bwd", grid=(T // tm,),
        in_specs=[pl.BlockSpec((tm, 256), lambda i: (i, 6)), pl.BlockSpec((tm, 128), lambda i: (i, 14)),
                  pl.BlockSpec((tm, 128), lambda i: (i, 15)),
                  pl.BlockSpec((tm, 128), row), pl.BlockSpec((tm, 128), row),
                  pl.BlockSpec((1, 256), fix), pl.BlockSpec((1, 128), fix),
                  pl.BlockSpec((1, 256), fix), pl.BlockSpec((1, 256), fix),
                  pl.BlockSpec((256, H * MLA_PAD), fix), pl.BlockSpec((128, H * MLA_PAD), fix),
                  pl.BlockSpec((H, tm, MLA_PAD), head), pl.BlockSpec((H, tm, MLA_PAD), head),
                  pl.BlockSpec((H, tm, MLA_V), head)],
        out_specs=[pl.BlockSpec((tm, 256), row), pl.BlockSpec((tm, 128), row), pl.BlockSpec((tm, 128), row),
                   pl.BlockSpec((tm, 256), row), pl.BlockSpec((tm, 128), row),
                   pl.BlockSpec((tm, H * MLA_PAD), row), pl.BlockSpec((tm, H * MLA_PAD), row),
                   pl.BlockSpec((1, 256), fix), pl.BlockSpec((1, 128), fix),
                   pl.BlockSpec((1, 256), fix), pl.BlockSpec((1, 256), fix)],
        out_shape=[jax.ShapeDtypeStruct((T, 256), F32), jax.ShapeDtypeStruct((T, 128), F32),
                   jax.ShapeDtypeStruct((T, 128), F32),
                   jax.ShapeDtypeStruct((T, 256), BF16), jax.ShapeDtypeStruct((T, 128), BF16),
                   jax.ShapeDtypeStruct((T, H * MLA_PAD), BF16), jax.ShapeDtypeStruct((T, H * MLA_PAD), BF16),
                   jax.ShapeDtypeStruct((1, 256), F32), jax.ShapeDtypeStruct((1, 128), F32),
                   jax.ShapeDtypeStruct((1, 256), F32), jax.ShapeDtypeStruct((1, 256), F32)],
        compiler_params=_params(1),
    )(proj, proj, proj, cos, sin, g_qa, g_kva, g_q, g_k, wq, wkv, dq, dk, dv)


def _causal_pairs(T, tq, tk, key_major):
    pairs = [(i, j) for i in range(T // tq) for j in range(T // tk) if j * tk <= i * tq + tq - 1]
    if key_major:
        pairs.sort(key=lambda p: (p[1], p[0]))
    outer = [p[1] if key_major else p[0] for p in pairs]
    first = [int(t == 0 or outer[t] != outer[t - 1]) for t in range(len(pairs))]
    last = [int(t == len(pairs) - 1 or outer[t] != outer[t + 1]) for t in range(len(pairs))]
    tab = lambda v: jnp.asarray(np.array(v, np.int32))
    return tab([p[0] for p in pairs]), tab([p[1] for p in pairs]), tab(first), tab(last)


def _causal_scores(qv, kv, qi, ki, row0, tq, tk, masked):
    s = _dot_nt(qv, kv)
    if masked:
        row = lax.broadcasted_iota(jnp.int32, s.shape, 0) + (qi * tq + row0)
        col = lax.broadcasted_iota(jnp.int32, s.shape, 1) + ki * tk
        s = jnp.where(col <= row, s, NEG)
    return s


def _mla_attn(q, k, v, ride=None, tq=1024, tk=2048, rc=256):
    H, T, _ = q.shape
    tables = _causal_pairs(T, tq, tk, key_major=False)
    n_pairs = int(tables[0].shape[0])
    r_args, r_in, r_shape, r_out, r_scratch = _ride_parts(ride)

    def body(qt, kt, ft, lt, q_ref, k_ref, v_ref, o_ref, lse_ref, m_s, l_s, acc):
        t = pl.program_id(1)
        qi, ki = qt[t], kt[t]

        @pl.when(ft[t] == 1)
        def _():
            m_s[...] = jnp.full_like(m_s, NEG)
            l_s[...] = jnp.zeros_like(l_s)
            acc[...] = jnp.zeros_like(acc)

        def update(masked):
            kk, vv = k_ref[...], v_ref[...]
            for c in range(tq // rc):
                rows = pl.ds(c * rc, rc)
                s = _causal_scores(q_ref[rows, :], kk, qi, ki, c * rc, tq, tk, masked)
                m_old = m_s[rows, :]
                m_new = jnp.maximum(m_old, jnp.max(s, axis=-1, keepdims=True))
                alpha = jnp.exp(m_old - m_new)
                p = jnp.exp(s - m_new)
                l_s[rows, :] = alpha * l_s[rows, :] + jnp.sum(p, axis=-1, keepdims=True)
                acc[rows, :] = alpha * acc[rows, :] + _dot(p.astype(BF16), vv)
                m_s[rows, :] = m_new

        diagonal = (ki + 1) * tk - 1 > qi * tq

        @pl.when(diagonal)
        def _():
            update(True)

        @pl.when(jnp.logical_not(diagonal))
        def _():
            update(False)

        @pl.when(lt[t] == 1)
        def _():
            o_ref[...] = acc[...] / l_s[...]
            lse_ref[...] = jnp.broadcast_to(m_s[...] + jnp.log(l_s[...]), lse_ref.shape)

    qrow = lambda h, t, qt, kt, ft, lt: (h, qt[t], 0)
    krow = lambda h, t, qt, kt, ft, lt: (h, kt[t], 0)
    first = lambda: (pl.program_id(0) == 0) & (pl.program_id(1) == 0)
    last = lambda: (pl.program_id(0) == H - 1) & (pl.program_id(1) == n_pairs - 1)
    outs = pl.pallas_call(
        _riding(body, 7, 2, 3, ride, first, last), name="mla_attn",
        grid_spec=pltpu.PrefetchScalarGridSpec(
            num_scalar_prefetch=4, grid=(H, n_pairs),
            in_specs=[pl.BlockSpec((None, tq, MLA_PAD), qrow), pl.BlockSpec((None, tk, MLA_PAD), krow),
                      pl.BlockSpec((None, tk, MLA_V), krow)] + r_in,
            out_specs=[pl.BlockSpec((tq, MLA_V), lambda h, t, qt, kt, ft, lt: (qt[t], h)),
                       pl.BlockSpec((None, tq, 128), qrow)] + r_out,
            scratch_shapes=[pltpu.VMEM((tq, 1), F32), pltpu.VMEM((tq, 1), F32), pltpu.VMEM((tq, MLA_V), F32)]
            + r_scratch),
        out_shape=[jax.ShapeDtypeStruct((T, H * MLA_V), F32), jax.ShapeDtypeStruct((H, T, 128), F32)] + r_shape,
        compiler_params=_params(2),
    )(*tables, q, k, v, *r_args)
    return outs[:2], outs[2:]


def _mla_attn_bwd(q, k, v, o, lse, do, ride=None, tq=1024, tk=1024, rc=512):
    H, T, _ = q.shape
    tables = _causal_pairs(T, tq, tk, key_major=True)
    n_pairs = int(tables[0].shape[0])
    r_args, r_in, r_shape, r_out, r_scratch = _ride_parts(ride)

    def body(qt, kt, ft, lt, q_ref, k_ref, v_ref, o_ref, lse_ref, do_ref, dq_ref, dk_ref, dv_ref, dk_s, dv_s):
        t = pl.program_id(1)
        qi, ki = qt[t], kt[t]

        @pl.when(t == 0)
        def _():
            dq_ref[...] = jnp.zeros_like(dq_ref)

        @pl.when(ft[t] == 1)
        def _():
            dk_s[...] = jnp.zeros_like(dk_s)
            dv_s[...] = jnp.zeros_like(dv_s)

        def update(masked):
            kk, vv = k_ref[...], v_ref[...]
            for c in range(tq // rc):
                rows = pl.ds(c * rc, rc)
                qv, dov = q_ref[rows, :], do_ref[rows, :]
                delta = jnp.sum(dov * o_ref[rows, :], axis=-1, keepdims=True)
                lse_v = jnp.max(lse_ref[rows, :], axis=-1, keepdims=True)
                p = jnp.exp(_causal_scores(qv, kk, qi, ki, c * rc, tq, tk, masked) - lse_v)
                dob = dov.astype(BF16)
                dv_s[...] += _dot_tn(p.astype(BF16), dob)
                ds = (p * (_dot_nt(dob, vv) - delta)).astype(BF16)
                dk_s[...] += _dot_tn(ds, qv)
                out_rows = pl.ds(pl.multiple_of(qi * tq + c * rc, rc), rc)
                dq_ref[out_rows, :] += _dot(ds, kk)

        diagonal = (ki + 1) * tk - 1 > qi * tq

        @pl.when(diagonal)
        def _():
            update(True)

        @pl.when(jnp.logical_not(diagonal))
        def _():
            update(False)

        @pl.when(lt[t] == 1)
        def _():
            dk_ref[...] = dk_s[...]
            dv_ref[...] = dv_s[...]

    qrow = lambda h, t, qt, kt, ft, lt: (h, qt[t], 0)
    krow = lambda h, t, qt, kt, ft, lt: (h, kt[t], 0)
    qcol = lambda h, t, qt, kt, ft, lt: (qt[t], h)
    first = lambda: (pl.program_id(0) == 0) & (pl.program_id(1) == 0)
    last = lambda: (pl.program_id(0) == H - 1) & (pl.program_id(1) == n_pairs - 1)
    outs = pl.pallas_call(
        _riding(body, 10, 3, 2, ride, first, last), name="mla_attn_bwd",
        grid_spec=pltpu.PrefetchScalarGridSpec(
            num_scalar_prefetch=4, grid=(H, n_pairs),
            in_specs=[pl.BlockSpec((None, tq, MLA_PAD), qrow), pl.BlockSpec((None, tk, MLA_PAD), krow),
                      pl.BlockSpec((None, tk, MLA_V), krow), pl.BlockSpec((tq, MLA_V), qcol),
                      pl.BlockSpec((None, tq, 128), qrow), pl.BlockSpec((tq, MLA_V), qcol)] + r_in,
            out_specs=[pl.BlockSpec((None, T, MLA_PAD), lambda h, t, qt, kt, ft, lt: (h, 0, 0)),
                       pl.BlockSpec((None, tk, MLA_PAD), krow), pl.BlockSpec((None, tk, MLA_V), krow)] + r_out,
            scratch_shapes=[pltpu.VMEM((tk, MLA_PAD), F32), pltpu.VMEM((tk, MLA_V), F32)] + r_scratch),
        out_shape=[jax.ShapeDtypeStruct((H, T, MLA_PAD), F32), jax.ShapeDtypeStruct((H, T, MLA_PAD), F32),
                   jax.ShapeDtypeStruct((H, T, MLA_V), F32)] + r_shape,
        compiler_params=_params(2),
    )(*tables, q, k, v, o, lse, do, *r_args)
    return outs[:3], outs[3:]


def _pair_gain(g):
    return jnp.tile(g.reshape(1, DIL_HD), (1, 2))


def _pad_gain(g):
    return jnp.pad(g.reshape(1, MLA_QK), ((0, 0), (0, MLA_PAD - MLA_QK)))


def _local_step(x, target, s, comm):
    T = x.shape[0]
    w = comm.w
    gq, gk = _pair_gain(s["dil_q_norm"]) * DIL_HD ** -0.5, _pair_gain(s["dil_k_norm"])
    g_q, g_k = _pad_gain(s["mla_q_norm"]), _pad_gain(s["mla_k_norm"])
    cos, sin = _rope_tables(T)
    onehots = _bucket_onehots()
    biases = _bias_tables(s["rel_bias"], onehots)

    (x1, h1, gate1, up1), got = _ffn_fwd(x, s["ffn1_norm"], w["ffn1_w_gate"], w["ffn1_w_up"], w["ffn1_w_down"],
                                         ride=comm.gather(_GROUPS["attn"]))
    comm.weights_landed(_GROUPS["attn"], got)
    hm, proj, qh, kh = _in_proj(x1, s["mix_norm"], w["w_in"], gq, gk)
    dil = None
    for (_, d), bias in zip(DIL_BRANCHES, biases):
        dil = _dil_fwd(qh, kh, proj, bias, d, dil)
    o_dil, lse_dil = dil
    q, k, v = _mla_prep(proj, cos, sin, s["mla_q_a_norm"], s["mla_kv_a_norm"], g_q, g_k, w["mla_w_q_b"], w["mla_w_kv_b"])
    (o_mla, lse_mla), got = _mla_attn(q, k, v, ride=comm.gather(_GROUPS["ffn2"]))
    comm.weights_landed(_GROUPS["ffn2"], got)
    x2, oc = _out_proj(x1, o_dil, o_mla, s["out_norm_dil"], s["out_norm_mla"], w["w_out"])
    (y, h2, gate2, up2), _ = _ffn_fwd(x2, s["ffn2_norm"], w["ffn2_w_gate"], w["ffn2_w_up"], w["ffn2_w_down"])
    dy, loss = _loss_grad(y, target)

    gw, gs = {}, {}

    def ffn_grads(name, dy_in, x_in, h, gate, up, ride=None, scatter_early=False):
        (dx, a, dg, du, dyh, dgain), got = _ffn_bwd(dy_in, x_in, s[name + "_norm"], gate, up,
                                                    w[name + "_w_gate"], w[name + "_w_up"], w[name + "_w_down"], ride=ride)
        gs[name + "_norm"] = dgain
        down, gate_n, up_n = (name + "_w_down",), (name + "_w_gate",), (name + "_w_up",)
        gw[down[0]], _ = _matmul_tn(a, dyh, 1408, 1024)
        gw[gate_n[0]], landed = _matmul_tn(h, dg, 1024, 1408, ride=comm.scatter(down, gw) if scatter_early else None)
        comm.grads_landed(down, landed)
        gw[up_n[0]], landed = _matmul_tn(h, du, 1024, 1408, ride=comm.scatter(gate_n, gw) if scatter_early else None)
        comm.grads_landed(gate_n, landed)
        return dx, got

    dx2, _ = ffn_grads("ffn2", dy, x2, h2, gate2, up2)
    gw["w_out"], _ = _matmul_tn(oc, dx2, 1024, 1024)
    do_dil, do_mla, gs["out_norm_dil"], gs["out_norm_mla"] = _out_proj_bwd(
        dx2, o_dil, o_mla, s["out_norm_dil"], s["out_norm_mla"], w["w_out"])

    (dq, dk, dv), got = _mla_attn_bwd(q, k, v, o_mla, lse_mla, do_mla, ride=comm.scatter(_GROUPS["ffn2"], gw))
    comm.grads_landed(_GROUPS["ffn2"], got)
    (dcq, dckv, dkpe, cqn, ckvn, dqp, dkvp, gs["mla_q_a_norm"], gs["mla_kv_a_norm"], dg_q, dg_k) = _mla_prep_bwd(
        proj, cos, sin, s["mla_q_a_norm"], s["mla_kv_a_norm"], g_q, g_k, w["mla_w_q_b"], w["mla_w_kv_b"], dq, dk, dv)
    gs["mla_q_norm"], gs["mla_k_norm"] = dg_q[:, :MLA_QK], dg_k[:, :MLA_QK]
    gw["mla_w_q_b"], _ = _matmul_tn(cqn, dqp, 256, 1024)
    gw["mla_w_kv_b"], _ = _matmul_tn(ckvn, dkvp, 128, 1024)

    dqkv, dbs = [], []
    for (_, d), bias in zip(DIL_BRANCHES, biases):
        triple, db = _dil_bwd(qh, kh, proj, o_dil, lse_dil, do_dil, bias, d)
        dqkv.append(triple)
        dbs.append(db)
    gs["rel_bias"] = _bias_grad(dbs, onehots)

    dx1, dproj, gs["mix_norm"], dgq, dgk = _in_proj_bwd(dx2, x1, s["mix_norm"], w["w_in"], proj, gq, gk,
                                                        dqkv, dcq, dckv, dkpe)
    gs["dil_q_norm"] = (dgq[:, :DIL_HD] + dgq[:, DIL_HD:]) * DIL_HD ** -0.5
    gs["dil_k_norm"] = dgk[:, :DIL_HD] + dgk[:, DIL_HD:]
    gw["w_in"], _ = _matmul_tn(hm, dproj, 1024, 1024)
    grad_x, got = ffn_grads("ffn1", dx1, x, h1, gate1, up1, ride=comm.scatter(_GROUPS["attn"], gw), scatter_early=True)
    comm.grads_landed(_GROUPS["attn"], got)
    return loss, grad_x, gw, gs


def _position():
    x, y, c = lax.axis_index("x"), lax.axis_index("y"), lax.axis_index("c")
    return x, y, c, 4 * x + 2 * y + c


def _peer(x, y, c, k):
    px = 1 - x if k & 4 else x
    py = 1 - y if k & 2 else y
    pc = 1 - c if k & 1 else c
    return (px, py, pc), 4 * px + 2 * py + pc


class _Ride:
    def __init__(self, arrays, scatter):
        self.arrays, self.scatter = list(arrays), list(scatter)
        self.n = n = len(self.arrays)
        self.specs = [pl.BlockSpec(memory_space=pl.ANY)] * n
        self.out_shape = [jax.ShapeDtypeStruct(a.shape if sc else (N_DEV,) + a.shape, a.dtype)
                          for a, sc in zip(self.arrays, self.scatter)]
        self.scratch = [pltpu.SemaphoreType.DMA((n, N_DEV - 1)), pltpu.SemaphoreType.DMA((n, N_DEV - 1)),
                        pltpu.SemaphoreType.DMA((n,))]

    def _copies(self, ins, outs, sems):
        send_sems, recv_sems, local_sems = sems
        x, y, c, me = _position()
        copies = []
        for a in range(self.n):
            src = ins[a].at[me] if self.scatter[a] else ins[a]
            copies.append(pltpu.make_async_copy(src, outs[a].at[me], local_sems.at[a]))
        for k in range(1, N_DEV):
            peer, peer_idx = _peer(x, y, c, k)
            for a in range(self.n):
                src = ins[a].at[peer_idx] if self.scatter[a] else ins[a]
                copies.append(pltpu.make_async_remote_copy(
                    src_ref=src, dst_ref=outs[a].at[me], send_sem=send_sems.at[a, k - 1], recv_sem=recv_sems.at[a, k - 1],
                    device_id=peer, device_id_type=pl.DeviceIdType.MESH))
        return copies

    def start(self, ins, outs, sems):
        for cp in self._copies(ins, outs, sems):
            cp.start()

    def wait(self, ins, outs, sems):
        for cp in self._copies(ins, outs, sems):
            cp.wait()


def _ride_parts(ride):
    if ride is None:
        return [], [], [], [], []
    return ride.arrays, ride.specs, ride.out_shape, ride.specs, ride.scratch


def _riding(body, n_in, n_out, n_scratch, ride, first, last):
    if ride is None:
        return body
    n = ride.n
    i1, i2 = n_in + n, n_in + n + n_out
    i3, i4 = i2 + n, i2 + n + n_scratch

    def wrapped(*refs):
        ins, outs, sems = refs[n_in:i1], refs[i2:i3], refs[i4:]

        @pl.when(first())
        def _():
            ride.start(ins, outs, sems)

        body(*refs[:n_in], *refs[i1:i2], *refs[i3:i4])

        @pl.when(last())
        def _():
            ride.wait(ins, outs, sems)

    return wrapped


def _exchange(ride, name):
    def body(*refs):
        parts = refs[:ride.n], refs[ride.n:2 * ride.n], refs[2 * ride.n:]
        ride.start(*parts)
        ride.wait(*parts)

    return pl.pallas_call(body, name=name, in_specs=ride.specs, out_specs=ride.specs, out_shape=ride.out_shape,
                          scratch_shapes=ride.scratch)(*ride.arrays)


def _adamw_math(wv, g, m, v):
    m = ADAM_B1 * m + (1.0 - ADAM_B1) * g
    v = ADAM_B2 * v + (1.0 - ADAM_B2) * (g * g)
    m_hat = m / (1.0 - ADAM_B1 ** ADAM_STEP)
    v_hat = v / (1.0 - ADAM_B2 ** ADAM_STEP)
    delta = -ADAM_LR * (m_hat / (jnp.sqrt(v_hat) + ADAM_EPS) + ADAM_WD * wv)
    return delta, m, v


def _adamw(parts, wv, m, v):
    R, C = wv.shape
    tr = max(t for t in range(16, 257, 16) if R % t == 0)

    def body(p_ref, w_ref, m_ref, v_ref, g_ref, d_ref, mo_ref, vo_ref):
        g = p_ref[0].astype(F32)
        for j in range(1, N_DEV):
            g = g + p_ref[j].astype(F32)
        d, mn, vn = _adamw_math(w_ref[...], g, m_ref[...], v_ref[...])
        g_ref[...] = g
        d_ref[...] = d
        mo_ref[...] = mn
        vo_ref[...] = vn

    row = lambda i: (i, 0)
    p_spec = pl.BlockSpec((N_DEV, tr, C), lambda i: (0, i, 0))
    out = jax.ShapeDtypeStruct((R, C), F32)
    return pl.pallas_call(
        body, name="adamw", grid=(R // tr,),
        in_specs=[p_spec, pl.BlockSpec((tr, C), row), pl.BlockSpec((tr, C), row), pl.BlockSpec((tr, C), row)],
        out_specs=[pl.BlockSpec((tr, C), row)] * 4, out_shape=[out] * 4,
        compiler_params=_params(1),
    )(parts, wv, m, v)


_ROW_SHARDED = ("ffn1_w_down", "ffn2_w_down", "w_out")
_GROUPS = {"ffn1": ("ffn1_w_gate", "ffn1_w_up", "ffn1_w_down"),
           "ffn2": ("ffn2_w_gate", "ffn2_w_up", "ffn2_w_down"),
           "attn": ("w_in", "mla_w_q_b", "mla_w_kv_b", "w_out")}
_SMALL = ("ffn1_norm", "mix_norm", "ffn2_norm", "out_norm_dil", "out_norm_mla", "mla_q_a_norm", "rel_bias",
          "mla_q_norm", "mla_k_norm", "mla_kv_a_norm", "dil_q_norm", "dil_k_norm")
_SMALL_ROWS = 48


def _cols_to_full(g):
    return g.transpose(1, 0, 2).reshape(g.shape[1], N_DEV * g.shape[2])


def _full_to_cols(f):
    return f.reshape(f.shape[0], N_DEV, f.shape[1] // N_DEV).transpose(1, 0, 2)


def _to_full(name, g):
    if name in _ROW_SHARDED:
        return g.reshape(-1, g.shape[-1])
    f = _cols_to_full(g)
    if name == "w_in":
        f = jnp.pad(f, ((0, 0), (0, PROJ_PAD - PROJ_COLS)))
    if name == "mla_w_q_b":
        f = jnp.pad(f.reshape(-1, MLA_HEADS, MLA_QK), ((0, 0), (0, 0), (0, MLA_PAD - MLA_QK)))
        f = f.reshape(-1, MLA_HEADS * MLA_PAD)
    return f


def _to_parts(name, f):
    if name in _ROW_SHARDED:
        return f.reshape(N_DEV, -1, f.shape[-1]).astype(BF16)
    if name == "w_in":
        f = f[:, :PROJ_COLS]
    if name == "mla_w_q_b":
        f = f.reshape(-1, MLA_HEADS, MLA_PAD)[:, :, :MLA_QK].reshape(-1, MLA_HEADS * MLA_QK)
    return _full_to_cols(f).astype(BF16)


class _Comm:
    def __init__(self, shards):
        self.shards, self.w, self.recv = shards, {}, {}

    def gather(self, names):
        return _Ride([self.shards[n] for n in names], [False] * len(names))

    def scatter(self, names, grads):
        return _Ride([_to_parts(n, grads[n]) for n in names], [True] * len(names))

    def weights_landed(self, names, got):
        self.w.update({n: _to_full(n, g) for n, g in zip(names, got)})

    def grads_landed(self, names, got):
        self.recv.update(zip(names, got))


def _pack_small(parts, extra):
    flat = jnp.concatenate([parts[n].reshape(-1) for n in _SMALL] + [extra.reshape(-1)])
    return jnp.pad(flat, (0, _SMALL_ROWS * 128 - flat.shape[0])).reshape(_SMALL_ROWS, 128)


def _unpack_small(packed, shapes):
    flat, out, off = packed.reshape(-1), {}, 0
    for n in _SMALL:
        size = math.prod(shapes[n])
        out[n] = flat[off:off + size].reshape(shapes[n])
        off += size
    return out, flat[off]


_NAMES = ("ffn1_norm", "ffn1_w_gate", "ffn1_w_up", "ffn1_w_down", "mix_norm", "w_in", "dil_q_norm", "dil_k_norm",
          "rel_bias", "mla_q_a_norm", "mla_w_q_b", "mla_kv_a_norm", "mla_w_kv_b", "mla_q_norm", "mla_k_norm",
          "out_norm_dil", "out_norm_mla", "w_out", "ffn2_norm", "ffn2_w_gate", "ffn2_w_up", "ffn2_w_down")


def kernel(x, ffn1_norm, ffn1_w_gate, ffn1_w_up, ffn1_w_down, mix_norm, w_in, dil_q_norm, dil_k_norm, rel_bias, mla_q_a_norm, mla_w_q_b, mla_kv_a_norm, mla_w_kv_b, mla_q_norm, mla_k_norm, out_norm_dil, out_norm_mla, w_out, ffn2_norm, ffn2_w_gate, ffn2_w_up, ffn2_w_down, loss_target, m_ffn1_norm, m_ffn1_w_gate, m_ffn1_w_up, m_ffn1_w_down, m_mix_norm, m_w_in, m_dil_q_norm, m_dil_k_norm, m_rel_bias, m_mla_q_a_norm, m_mla_w_q_b, m_mla_kv_a_norm, m_mla_w_kv_b, m_mla_q_norm, m_mla_k_norm, m_out_norm_dil, m_out_norm_mla, m_w_out, m_ffn2_norm, m_ffn2_w_gate, m_ffn2_w_up, m_ffn2_w_down, v_ffn1_norm, v_ffn1_w_gate, v_ffn1_w_up, v_ffn1_w_down, v_mix_norm, v_w_in, v_dil_q_norm, v_dil_k_norm, v_rel_bias, v_mla_q_a_norm, v_mla_w_q_b, v_mla_kv_a_norm, v_mla_w_kv_b, v_mla_q_norm, v_mla_k_norm, v_out_norm_dil, v_out_norm_mla, v_w_out, v_ffn2_norm, v_ffn2_w_gate, v_ffn2_w_up, v_ffn2_w_down):
    args = locals()
    wts = {n: args[n] for n in _NAMES}
    mom = {n: args["m_" + n] for n in _NAMES}
    var = {n: args["v_" + n] for n in _NAMES}

    matrices = [n for group in _GROUPS.values() for n in group]
    comm = _Comm({n: wts[n][0].astype(BF16) for n in matrices})
    comm.weights_landed(_GROUPS["ffn1"], _exchange(comm.gather(_GROUPS["ffn1"]), "gather_first"))
    small = {n: wts[n].reshape(1, -1) if n != "rel_bias" else wts[n] for n in _SMALL}

    loss, grad_x, gw, gs = _local_step(x[0], loss_target[0], small, comm)

    last = comm.scatter(("ffn1_w_up",), gw)
    got = _exchange(_Ride(last.arrays + [_pack_small(gs, loss[0, 0])], last.scatter + [False]), "scatter_last")
    comm.grads_landed(("ffn1_w_up",), got[:-1])

    res = {n: _adamw(comm.recv[n], wts[n][0], mom[n][0], var[n][0]) for n in matrices}
    shapes = {n: wts[n].shape for n in _SMALL}
    zero = jnp.zeros((), F32)
    packed = _adamw(got[-1], _pack_small(wts, zero), _pack_small(mom, zero), _pack_small(var, zero))
    loss_total = None
    for slot, q in enumerate(packed):
        vals, extra = _unpack_small(q, shapes)
        if slot == 0:
            loss_total = extra
        for n in _SMALL:
            res.setdefault(n, [None] * 4)[slot] = vals[n]
    outs = [loss_total, grad_x[None]]
    for slot in range(4):
        outs += [res[n][slot].reshape(wts[n].shape) for n in _NAMES]
    return tuple(outs)
```

```python
import math

import numpy as np
import jax
import jax.numpy as jnp
from jax import lax
from jax.experimental import pallas as pl
from jax.experimental.pallas import tpu as pltpu

F32, BF16 = jnp.float32, jnp.bfloat16
EPS = 1e-6
NEG = -1e30
N_DEV = 8

DIL_HEADS, DIL_HD = 8, 64
DIL_WIDTH = DIL_HEADS * DIL_HD
DIL_BRANCHES = ((128, 1), (512, 4), (2048, 16))
DIL_BLOCK = 128
MLA_HEADS, MLA_NOPE, MLA_ROPE, MLA_V = 4, 128, 64, 128
MLA_QK = MLA_NOPE + MLA_ROPE
MLA_PAD = 256
ROPE_BASE = 10000.0
REL_BUCKETS, REL_MAX_DIST = 32, 2048
PROJ_COLS, PROJ_PAD = 1984, 2048
FFN_RESID = 0.5
ADAM_LR, ADAM_B1, ADAM_B2, ADAM_EPS, ADAM_WD, ADAM_STEP = 0.001, 0.9, 0.999, 1e-08, 0.01, 10
VMEM_LIMIT = 62 * 1024 * 1024

_NT = (((1,), (1,)), ((), ()))
_TN = (((0,), (0,)), ((), ()))


def _dot(a, b):
    return jnp.dot(a, b, preferred_element_type=F32)


def _dot_nt(a, b):
    return lax.dot_general(a, b, _NT, preferred_element_type=F32)


def _dot_tn(a, b):
    return lax.dot_general(a, b, _TN, preferred_element_type=F32)


def _params(n_axes):
    return pltpu.CompilerParams(dimension_semantics=("arbitrary",) * n_axes, vmem_limit_bytes=VMEM_LIMIT)


def _rstd(x, n=None):
    n = x.shape[-1] if n is None else n
    return lax.rsqrt(jnp.sum(x * x, axis=-1, keepdims=True) / n + EPS)


def _rms_bwd(dy, x, g, r, n=None):
    n = x.shape[-1] if n is None else n
    u = dy * g
    dx = r * u - x * (r * r * r) * (jnp.sum(u * x, axis=-1, keepdims=True) / n)
    return dx, dy * x * r


def _sigmoid(x):
    return 1.0 / (1.0 + jnp.exp(-x))


def _split3(x):
    parts = []
    for _ in range(3):
        xb = x.astype(BF16)
        parts.append(xb)
        x = x - xb.astype(F32)
    return parts


def _ffn_fwd(x, gain, wg, wu, wd, ride=None, tm=512, tf=2816):
    T, D = x.shape
    F = wg.shape[1]
    ni, nj = T // tm, F // tf
    r_args, r_in, r_shape, r_out, r_scratch = _ride_parts(ride)

    def body(x_ref, g_ref, wg_ref, wu_ref, wd_ref, xo_ref, h_ref, gate_ref, up_ref, acc):
        j = pl.program_id(1)

        @pl.when(j == 0)
        def _():
            xv = x_ref[...]
            h_ref[...] = (xv * _rstd(xv) * g_ref[...]).astype(BF16)
            acc[...] = jnp.zeros_like(acc)

        h = h_ref[...]
        g = _dot(h, wg_ref[...])
        u = _dot(h, wu_ref[...])
        gate_ref[...] = g.astype(BF16)
        up_ref[...] = u.astype(BF16)
        a = (g * _sigmoid(g) * u).astype(BF16)
        acc[...] += _dot(a, wd_ref[...])

        @pl.when(j == nj - 1)
        def _():
            xo_ref[...] = x_ref[...] + FFN_RESID * acc[...]

    first = lambda: (pl.program_id(0) == 0) & (pl.program_id(1) == 0)
    last = lambda: (pl.program_id(0) == ni - 1) & (pl.program_id(1) == nj - 1)
    outs = pl.pallas_call(
        _riding(body, 5, 4, 1, ride, first, last), name="ffn_fwd", grid=(ni, nj),
        in_specs=[pl.BlockSpec((tm, D), lambda i, j: (i, 0)), pl.BlockSpec((1, D), lambda i, j: (0, 0)),
                  pl.BlockSpec((D, tf), lambda i, j: (0, j)), pl.BlockSpec((D, tf), lambda i, j: (0, j)),
                  pl.BlockSpec((tf, D), lambda i, j: (j, 0))] + r_in,
        out_specs=[pl.BlockSpec((tm, D), lambda i, j: (i, 0)), pl.BlockSpec((tm, D), lambda i, j: (i, 0)),
                   pl.BlockSpec((tm, tf), lambda i, j: (i, j)), pl.BlockSpec((tm, tf), lambda i, j: (i, j))] + r_out,
        out_shape=[jax.ShapeDtypeStruct((T, D), F32), jax.ShapeDtypeStruct((T, D), BF16),
                   jax.ShapeDtypeStruct((T, F), BF16), jax.ShapeDtypeStruct((T, F), BF16)] + r_shape,
        scratch_shapes=[pltpu.VMEM((tm, D), F32)] + r_scratch,
        compiler_params=_params(2),
    )(x, gain, wg, wu, wd, *r_args)
    return outs[:4], outs[4:]


def _ffn_bwd(dy, x, gain, gate, up, wg, wu, wd, ride=None, tm=256, tf=2816):
    T, D = x.shape
    F = wg.shape[1]
    ni, nj = T // tm, F // tf
    r_args, r_in, r_shape, r_out, r_scratch = _ride_parts(ride)

    def body(dy_ref, x_ref, g_ref, gate_ref, up_ref, wg_ref, wu_ref, wd_ref,
             dx_ref, a_ref, dg_ref, du_ref, dyh_ref, dgain_ref, acc):
        i, j = pl.program_id(0), pl.program_id(1)

        @pl.when((i == 0) & (j == 0))
        def _():
            dgain_ref[...] = jnp.zeros_like(dgain_ref)

        @pl.when(j == 0)
        def _():
            dyh_ref[...] = (FFN_RESID * dy_ref[...]).astype(BF16)
            acc[...] = jnp.zeros_like(acc)

        da = _dot_nt(dyh_ref[...], wd_ref[...])
        g = gate_ref[...].astype(F32)
        u = up_ref[...].astype(F32)
        sig = _sigmoid(g)
        s = g * sig
        a_ref[...] = (s * u).astype(BF16)
        dg = (da * u * (sig * (1.0 + g * (1.0 - sig)))).astype(BF16)
        du = (da * s).astype(BF16)
        dg_ref[...] = dg
        du_ref[...] = du
        acc[...] += _dot_nt(dg, wg_ref[...]) + _dot_nt(du, wu_ref[...])

        @pl.when(j == nj - 1)
        def _():
            xv = x_ref[...]
            dxn, dgc = _rms_bwd(acc[...], xv, g_ref[...], _rstd(xv))
            dx_ref[...] = dy_ref[...] + dxn
            dgain_ref[...] += jnp.sum(dgc, axis=0, keepdims=True)

    first = lambda: (pl.program_id(0) == 0) & (pl.program_id(1) == 0)
    last = lambda: (pl.program_id(0) == ni - 1) & (pl.program_id(1) == nj - 1)
    outs = pl.pallas_call(
        _riding(body, 8, 6, 1, ride, first, last), name="ffn_bwd", grid=(ni, nj),
        in_specs=[pl.BlockSpec((tm, D), lambda i, j: (i, 0)), pl.BlockSpec((tm, D), lambda i, j: (i, 0)),
                  pl.BlockSpec((1, D), lambda i, j: (0, 0)),
                  pl.BlockSpec((tm, tf), lambda i, j: (i, j)), pl.BlockSpec((tm, tf), lambda i, j: (i, j)),
                  pl.BlockSpec((D, tf), lambda i, j: (0, j)), pl.BlockSpec((D, tf), lambda i, j: (0, j)),
                  pl.BlockSpec((tf, D), lambda i, j: (j, 0))] + r_in,
        out_specs=[pl.BlockSpec((tm, D), lambda i, j: (i, 0)),
                   pl.BlockSpec((tm, tf), lambda i, j: (i, j)), pl.BlockSpec((tm, tf), lambda i, j: (i, j)),
                   pl.BlockSpec((tm, tf), lambda i, j: (i, j)),
                   pl.BlockSpec((tm, D), lambda i, j: (i, 0)), pl.BlockSpec((1, D), lambda i, j: (0, 0))] + r_out,
        out_shape=[jax.ShapeDtypeStruct((T, D), F32), jax.ShapeDtypeStruct((T, F), BF16),
                   jax.ShapeDtypeStruct((T, F), BF16), jax.ShapeDtypeStruct((T, F), BF16),
                   jax.ShapeDtypeStruct((T, D), BF16), jax.ShapeDtypeStruct((1, D), F32)] + r_shape,
        scratch_shapes=[pltpu.VMEM((tm, D), F32)] + r_scratch,
        compiler_params=_params(2),
    )(dy, x, gain, gate, up, wg, wu, wd, *r_args)
    return outs[:6], outs[6:]


def _matmul_tn(a, b, tk, tn, ride=None, tt=2048):
    T, K = a.shape
    N = b.shape[1]
    tk, tn = min(tk, K), min(tn, N)
    grid = (K // tk, N // tn, T // tt)
    r_args, r_in, r_shape, r_out, r_scratch = _ride_parts(ride)

    def body(a_ref, b_ref, o_ref):
        @pl.when(pl.program_id(2) == 0)
        def _():
            o_ref[...] = jnp.zeros_like(o_ref)

        o_ref[...] += _dot_tn(a_ref[...].astype(BF16), b_ref[...].astype(BF16))

    first = lambda: (pl.program_id(0) == 0) & (pl.program_id(1) == 0) & (pl.program_id(2) == 0)
    last = lambda: ((pl.program_id(0) == grid[0] - 1) & (pl.program_id(1) == grid[1] - 1)
                    & (pl.program_id(2) == grid[2] - 1))
    outs = pl.pallas_call(
        _riding(body, 2, 1, 0, ride, first, last), name="matmul_tn", grid=grid,
        in_specs=[pl.BlockSpec((tt, tk), lambda k, n, t: (t, k)), pl.BlockSpec((tt, tn), lambda k, n, t: (t, n))] + r_in,
        out_specs=[pl.BlockSpec((tk, tn), lambda k, n, t: (k, n))] + r_out,
        out_shape=[jax.ShapeDtypeStruct((K, N), F32)] + r_shape,
        scratch_shapes=r_scratch,
        compiler_params=_params(3),
    )(a, b, *r_args)
    return outs[0], outs[1:]


def _loss_grad(y, target, tm=512):
    T, D = y.shape

    def body(y_ref, t_ref, dy_ref, loss_ref):
        @pl.when(pl.program_id(0) == 0)
        def _():
            loss_ref[...] = jnp.zeros_like(loss_ref)

        e = y_ref[...] - t_ref[...]
        dy_ref[...] = e * (1.0 / D)
        loss_ref[...] += (0.5 / D) * jnp.sum(e * e)

    return pl.pallas_call(
        body, name="loss_grad", grid=(T // tm,),
        in_specs=[pl.BlockSpec((tm, D), lambda i: (i, 0)), pl.BlockSpec((tm, D), lambda i: (i, 0))],
        out_specs=[pl.BlockSpec((tm, D), lambda i: (i, 0)), pl.BlockSpec((1, 128), lambda i: (0, 0))],
        out_shape=[jax.ShapeDtypeStruct((T, D), F32), jax.ShapeDtypeStruct((1, 128), F32)],
        compiler_params=_params(1),
    )(y, target)


def _in_proj(x, gain, w, gq, gk, tm=512):
    T, D = x.shape
    N = w.shape[1]
    W = DIL_WIDTH

    def body(x_ref, g_ref, w_ref, gq_ref, gk_ref, h_ref, p_ref, qh_ref, kh_ref):
        xv = x_ref[...]
        h = (xv * _rstd(xv) * g_ref[...]).astype(BF16)
        h_ref[...] = h
        p_ref[...] = _dot(h, w_ref[...])
        lo = lax.broadcasted_iota(jnp.int32, (tm, 128), 1) < DIL_HD
        for hp in range(DIL_HEADS // 2):
            q = p_ref[:, 128 * hp:128 * (hp + 1)]
            k = p_ref[:, W + 128 * hp:W + 128 * (hp + 1)]
            qh_ref[:, 128 * hp:128 * (hp + 1)] = (q * _pair_rstd(q, lo) * gq_ref[...]).astype(BF16).astype(F32)
            kh_ref[:, 128 * hp:128 * (hp + 1)] = (k * _pair_rstd(k, lo) * gk_ref[...]).astype(BF16).astype(F32)

    row = lambda i: (i, 0)
    fix = lambda i: (0, 0)
    return pl.pallas_call(
        body, name="in_proj", grid=(T // tm,),
        in_specs=[pl.BlockSpec((tm, D), row), pl.BlockSpec((1, D), fix), pl.BlockSpec((D, N), fix),
                  pl.BlockSpec((1, 128), fix), pl.BlockSpec((1, 128), fix)],
        out_specs=[pl.BlockSpec((tm, D), row), pl.BlockSpec((tm, N), row), pl.BlockSpec((tm, W), row),
                   pl.BlockSpec((tm, W), row)],
        out_shape=[jax.ShapeDtypeStruct((T, D), BF16), jax.ShapeDtypeStruct((T, N), F32),
                   jax.ShapeDtypeStruct((T, W), F32), jax.ShapeDtypeStruct((T, W), F32)],
        compiler_params=_params(1),
    )(x, gain, w, gq, gk)


def _in_proj_bwd(dx_up, x, gain, w, proj, gq, gk, dqkv, dcq, dckv, dkpe, tm=512):
    T, D = x.shape
    N = w.shape[1]
    W = DIL_WIDTH
    nb = len(dqkv)

    def body(*refs):
        dxu_ref, x_ref, g_ref, w_ref, q_ref, k_ref, gq_ref, gk_ref = refs[:8]
        dil_refs = refs[8:8 + 3 * nb]
        dcq_ref, dckv_ref, dkpe_ref, dx_ref, dp_ref, dgain_ref, dgq_ref, dgk_ref = refs[8 + 3 * nb:]

        @pl.when(pl.program_id(0) == 0)
        def _():
            for ref in (dgain_ref, dgq_ref, dgk_ref):
                ref[...] = jnp.zeros_like(ref)

        lo = lax.broadcasted_iota(jnp.int32, (tm, 128), 1) < DIL_HD
        norms = ((q_ref, gq_ref, dgq_ref), (k_ref, gk_ref, dgk_ref))
        for part in range(3):
            acc = dil_refs[part][...]
            for b in range(1, nb):
                acc = acc + dil_refs[3 * b + part][...]
            if part == 2:
                dp_ref[:, 2 * W:3 * W] = acc.astype(BF16)
                continue
            raw_ref, gn_ref, dgn_ref = norms[part]
            for hp in range(DIL_HEADS // 2):
                raw = raw_ref[:, 128 * hp:128 * (hp + 1)]
                d_raw, dgn = _pair_rms_bwd(acc[:, 128 * hp:128 * (hp + 1)], raw, _pair_rstd(raw, lo), gn_ref[...], lo)
                dp_ref[:, part * W + 128 * hp:part * W + 128 * (hp + 1)] = d_raw.astype(BF16)
                dgn_ref[...] += dgn
        dp_ref[:, 3 * W:3 * W + 256] = dcq_ref[...].astype(BF16)
        dp_ref[:, 3 * W + 256:3 * W + 384] = dckv_ref[...].astype(BF16)
        dp_ref[:, 3 * W + 384:N] = dkpe_ref[...].astype(BF16)
        dh = _dot_nt(dp_ref[...], w_ref[...])
        xv = x_ref[...]
        dxn, dgc = _rms_bwd(dh, xv, g_ref[...], _rstd(xv))
        dx_ref[...] = dxu_ref[...] + dxn
        dgain_ref[...] += jnp.sum(dgc, axis=0, keepdims=True)

    row = lambda i: (i, 0)
    fix = lambda i: (0, 0)
    return pl.pallas_call(
        body, name="in_proj_bwd", grid=(T // tm,),
        in_specs=[pl.BlockSpec((tm, D), row), pl.BlockSpec((tm, D), row), pl.BlockSpec((1, D), fix),
                  pl.BlockSpec((D, N), fix), pl.BlockSpec((tm, W), row), pl.BlockSpec((tm, W), lambda i: (i, 1)),
                  pl.BlockSpec((1, 128), fix), pl.BlockSpec((1, 128), fix)] + [pl.BlockSpec((tm, W), row)] * (3 * nb)
                 + [pl.BlockSpec((tm, 256), row), pl.BlockSpec((tm, 128), row), pl.BlockSpec((tm, 128), row)],
        out_specs=[pl.BlockSpec((tm, D), row), pl.BlockSpec((tm, N), row), pl.BlockSpec((1, D), fix),
                   pl.BlockSpec((1, 128), fix), pl.BlockSpec((1, 128), fix)],
        out_shape=[jax.ShapeDtypeStruct((T, D), F32), jax.ShapeDtypeStruct((T, N), BF16),
                   jax.ShapeDtypeStruct((1, D), F32), jax.ShapeDtypeStruct((1, 128), F32),
                   jax.ShapeDtypeStruct((1, 128), F32)],
        compiler_params=_params(1),
    )(dx_up, x, gain, w, proj, proj, gq, gk, *[a for triple in dqkv for a in triple], dcq, dckv, dkpe)


def _out_proj(x, o_dil, o_mla, g_dil, g_mla, w, tm=512):
    T, D = x.shape
    W = o_dil.shape[1]

    def body(x_ref, od_ref, om_ref, gd_ref, gm_ref, w_ref, xo_ref, oc_ref):
        od, om = od_ref[...], om_ref[...]
        oc_ref[:, 0:W] = (od * _rstd(od) * gd_ref[...]).astype(BF16)
        oc_ref[:, W:2 * W] = (om * _rstd(om) * gm_ref[...]).astype(BF16)
        xo_ref[...] = x_ref[...] + _dot(oc_ref[...], w_ref[...])

    row = lambda i: (i, 0)
    fix = lambda i: (0, 0)
    return pl.pallas_call(
        body, name="out_proj", grid=(T // tm,),
        in_specs=[pl.BlockSpec((tm, D), row), pl.BlockSpec((tm, W), row), pl.BlockSpec((tm, W), row),
                  pl.BlockSpec((1, W), fix), pl.BlockSpec((1, W), fix), pl.BlockSpec((2 * W, D), fix)],
        out_specs=[pl.BlockSpec((tm, D), row), pl.BlockSpec((tm, 2 * W), row)],
        out_shape=[jax.ShapeDtypeStruct((T, D), F32), jax.ShapeDtypeStruct((T, 2 * W), BF16)],
        compiler_params=_params(1),
    )(x, o_dil, o_mla, g_dil, g_mla, w)


def _out_proj_bwd(dx, o_dil, o_mla, g_dil, g_mla, w, tm=512):
    T, D = dx.shape
    W = o_dil.shape[1]

    def body(dx_ref, od_ref, om_ref, gd_ref, gm_ref, w_ref, dod_ref, dom_ref, dgd_ref, dgm_ref):
        @pl.when(pl.program_id(0) == 0)
        def _():
            dgd_ref[...] = jnp.zeros_like(dgd_ref)
            dgm_ref[...] = jnp.zeros_like(dgm_ref)

        doc = _dot_nt(dx_ref[...].astype(BF16), w_ref[...])
        od, om = od_ref[...], om_ref[...]
        dod, dgd = _rms_bwd(doc[:, 0:W], od, gd_ref[...], _rstd(od))
        dom, dgm = _rms_bwd(doc[:, W:2 * W], om, gm_ref[...], _rstd(om))
        dod_ref[...] = dod
        dom_ref[...] = dom
        dgd_ref[...] += jnp.sum(dgd, axis=0, keepdims=True)
        dgm_ref[...] += jnp.sum(dgm, axis=0, keepdims=True)

    row = lambda i: (i, 0)
    fix = lambda i: (0, 0)
    return pl.pallas_call(
        body, name="out_proj_bwd", grid=(T // tm,),
        in_specs=[pl.BlockSpec((tm, D), row), pl.BlockSpec((tm, W), row), pl.BlockSpec((tm, W), row),
                  pl.BlockSpec((1, W), fix), pl.BlockSpec((1, W), fix), pl.BlockSpec((2 * W, D), fix)],
        out_specs=[pl.BlockSpec((tm, W), row), pl.BlockSpec((tm, W), row),
                   pl.BlockSpec((1, W), fix), pl.BlockSpec((1, W), fix)],
        out_shape=[jax.ShapeDtypeStruct((T, W), F32), jax.ShapeDtypeStruct((T, W), F32),
                   jax.ShapeDtypeStruct((1, W), F32), jax.ShapeDtypeStruct((1, W), F32)],
        compiler_params=_params(1),
    )(dx, o_dil, o_mla, g_dil, g_mla, w)


def _pair_rstd(x, lo):
    sq = x * x
    s0 = jnp.sum(jnp.where(lo, sq, 0.0), axis=-1, keepdims=True)
    s1 = jnp.sum(jnp.where(lo, 0.0, sq), axis=-1, keepdims=True)
    return jnp.where(lo, lax.rsqrt(s0 / DIL_HD + EPS), lax.rsqrt(s1 / DIL_HD + EPS))


def _pair_rms_bwd(dn, x, r, g, lo):
    u = dn * g
    t = u * x
    d0 = jnp.sum(jnp.where(lo, t, 0.0), axis=-1, keepdims=True)
    d1 = jnp.sum(jnp.where(lo, 0.0, t), axis=-1, keepdims=True)
    dx = r * u - x * (r * r * r) * (jnp.where(lo, d0, d1) / DIL_HD)
    return dx, jnp.sum(dn * x * r, axis=0, keepdims=True)


def _pair_col(x, lo, e):
    sel = lo if e == 0 else jnp.logical_not(lo)
    return jnp.max(jnp.where(sel, x, NEG), axis=-1, keepdims=True)


def _dil_masks(n):
    lo = lax.broadcasted_iota(jnp.int32, (DIL_BLOCK, DIL_BLOCK), 1) < DIL_HD
    row = lax.broadcasted_iota(jnp.int32, (2 * DIL_BLOCK, 2 * DIL_BLOCK), 0) % DIL_BLOCK
    col = lax.broadcasted_iota(jnp.int32, (2 * DIL_BLOCK, 2 * DIL_BLOCK), 1)
    prev = jnp.logical_and(jnp.logical_and(col < DIL_BLOCK, col >= row), n > 0)
    cur = jnp.logical_and(col >= DIL_BLOCK, col - DIL_BLOCK <= row)
    return lo, jnp.logical_or(prev, cur)


def _stack_heads(x, lo):
    return jnp.concatenate([jnp.where(lo, x, 0.0), jnp.where(lo, 0.0, x)], axis=0)


def _unstack_heads(x2, lo):
    return jnp.where(lo, x2[:DIL_BLOCK], x2[DIL_BLOCK:])


def _dil_pairs(d):
    return 4 if d == 1 else 1


def _sub_rows(r, d):
    return pl.ds(r, DIL_BLOCK, stride=d) if d > 1 else pl.ds(0, DIL_BLOCK)


def _split_subsequences(loads, d, P):
    for r in range(d):
        for p in range(P):
            for block, scratch, part in loads:
                piece = block[_sub_rows(r, d), pl.ds(128 * p, 128)]
                if part is None:
                    scratch[r * P + p] = piece
                else:
                    scratch[r * P + p, pl.ds(DIL_BLOCK * part, DIL_BLOCK), :] = piece


def _merge_subsequences(pairs, d, P):
    for r in range(d):
        for p in range(P):
            for block, scratch in pairs:
                block[_sub_rows(r, d), pl.ds(128 * p, 128)] = scratch[r * P + p]


def _dil_fwd(qh, kh, proj, bias, d, prev):
    T = proj.shape[0]
    P = _dil_pairs(d)
    rows, cw, n_it = DIL_BLOCK * d, 128 * P, d * P
    nblk = T // rows
    has_prev = prev is not None

    def body(*refs):
        q_ref, kp_ref, kc_ref, vp_ref, vc_ref, bias_ref = refs[:6]
        refs = refs[6:]
        if has_prev:
            oin_ref, lin_ref = refs[:2]
            refs = refs[2:]
        o_ref, l_ref, qs, ks, vs, os_, ls_ = refs[:7]
        pb, n = pl.program_id(0), pl.program_id(1)
        lo, valid = _dil_masks(n)
        loads = [(q_ref, qs, None), (kp_ref, ks, 0), (kc_ref, ks, 1), (vp_ref, vs, 0), (vc_ref, vs, 1)]
        if has_prev:
            ois, lis = refs[7:]
            loads += [(oin_ref, ois, None), (lin_ref, lis, None)]
        _split_subsequences(loads, d, P)

        def step(i, carry):
            q2 = _stack_heads(qs[i], lo).astype(BF16)
            s = jnp.where(valid, _dot_nt(q2, ks[i].astype(BF16)) + bias_ref[pb * P + i % P], NEG)
            m = jnp.max(s, axis=-1, keepdims=True)
            p = jnp.exp(s - m)
            l = jnp.sum(p, axis=-1, keepdims=True)
            o = _unstack_heads(_dot(p.astype(BF16), vs[i].astype(BF16)) / l, lo)
            lse = _unstack_heads(jnp.broadcast_to(m + jnp.log(l), (2 * DIL_BLOCK, 128)), lo)
            if has_prev:
                lin = lis[i]
                mx = jnp.maximum(lin, lse)
                lnew = mx + jnp.log(jnp.exp(lin - mx) + jnp.exp(lse - mx))
                o = ois[i] * jnp.exp(lin - lnew) + o * jnp.exp(lse - lnew)
                lse = lnew
            os_[i] = o
            ls_[i] = lse
            return carry

        lax.fori_loop(0, n_it, step, 0, unroll=4)
        _merge_subsequences([(o_ref, os_), (l_ref, ls_)], d, P)

    blk = (rows, cw)
    vcol = 2 * DIL_WIDTH // cw
    prev_n = lambda n: jnp.maximum(n - 1, 0)
    fix3 = lambda pb, n: (0, 0, 0)
    tok = pl.BlockSpec(blk, lambda pb, n: (n, pb))
    tok_prev = pl.BlockSpec(blk, lambda pb, n: (prev_n(n), pb))
    in_specs = [tok, tok_prev, tok,
                pl.BlockSpec(blk, lambda pb, n: (prev_n(n), vcol + pb)), pl.BlockSpec(blk, lambda pb, n: (n, vcol + pb)),
                pl.BlockSpec((DIL_HEADS // 2, 2 * DIL_BLOCK, 2 * DIL_BLOCK), fix3)]
    args = [qh, kh, kh, proj, proj, bias]
    one, two = pltpu.VMEM((n_it, DIL_BLOCK, 128), F32), pltpu.VMEM((n_it, 2 * DIL_BLOCK, 128), F32)
    scratch = [one, two, two, one, one]
    if has_prev:
        in_specs += [tok, tok]
        args += list(prev)
        scratch += [one, one]
    out = jax.ShapeDtypeStruct((T, DIL_WIDTH), F32)
    return pl.pallas_call(
        body, name=f"dil_fwd_d{d}", grid=(DIL_HEADS // 2 // P, nblk), in_specs=in_specs, out_specs=[tok, tok],
        out_shape=[out, out], scratch_shapes=scratch, compiler_params=_params(2),
    )(*args)


def _dil_bwd(qh, kh, proj, o, lse, do, bias, d):
    T = proj.shape[0]
    P = _dil_pairs(d)
    rows, cw, n_it = DIL_BLOCK * d, 128 * P, d * P
    nblk = T // rows

    def body(q_ref, kp_ref, kc_ref, vp_ref, vc_ref, o_ref, l_ref, do_ref, bias_ref,
             dq_ref, dk_ref, dv_ref, db_ref,
             qs, ks, vs, os_, ls_, dos, dqs, dks, dvs, ck, cv):
        pb, n = pl.program_id(0), pl.program_id(1)
        lo, valid = _dil_masks(n)

        @pl.when((pb == 0) & (n == 0))
        def _():
            db_ref[...] = jnp.zeros_like(db_ref)

        @pl.when(n == 0)
        def _():
            ck[...] = jnp.zeros_like(ck)
            cv[...] = jnp.zeros_like(cv)

        _split_subsequences([(q_ref, qs, None), (kp_ref, ks, 0), (kc_ref, ks, 1), (vp_ref, vs, 0), (vc_ref, vs, 1),
                             (o_ref, os_, None), (l_ref, ls_, None), (do_ref, dos, None)], d, P)

        def step(i, carry):
            pair = pb * P + i % P
            q2 = _stack_heads(qs[i], lo).astype(BF16)
            kcat, vcat = ks[i].astype(BF16), vs[i].astype(BF16)
            dov = dos[i]
            do2 = _stack_heads(dov, lo).astype(BF16)
            delta = jnp.sum(_stack_heads(dov * os_[i], lo), axis=-1, keepdims=True)
            lse_pair = ls_[i]
            lse2 = jnp.concatenate([_pair_col(lse_pair, lo, 0), _pair_col(lse_pair, lo, 1)], axis=0)
            s = jnp.where(valid, _dot_nt(q2, kcat) + bias_ref[pair], NEG)
            p = jnp.exp(s - lse2)
            ds = p * (_dot_nt(do2, vcat) - delta)
            db_ref[pair] += ds
            dsb = ds.astype(BF16)
            dqs[i] = _unstack_heads(_dot(dsb, kcat), lo)
            dk2 = _dot_tn(dsb, q2)
            dv2 = _dot_tn(p.astype(BF16), do2)
            dks[i] = ck[i] + dk2[:DIL_BLOCK]
            dvs[i] = cv[i] + dv2[:DIL_BLOCK]
            ck[i] = dk2[DIL_BLOCK:]
            cv[i] = dv2[DIL_BLOCK:]
            return carry

        @pl.when(n < nblk)
        def _():
            lax.fori_loop(0, n_it, step, 0, unroll=2)
            _merge_subsequences([(dq_ref, dqs), (dk_ref, dks), (dv_ref, dvs)], d, P)

        @pl.when(n == nblk)
        def _():
            _merge_subsequences([(dk_ref, ck), (dv_ref, cv)], d, P)

    blk = (rows, cw)
    vcol = 2 * DIL_WIDTH // cw
    qn_ = lambda n: jnp.minimum(n, nblk - 1)
    pn_ = lambda n: jnp.maximum(n - 1, 0)
    fix3 = lambda pb, n: (0, 0, 0)
    tok_q = pl.BlockSpec(blk, lambda pb, n: (qn_(n), pb))
    tok_p = pl.BlockSpec(blk, lambda pb, n: (pn_(n), pb))
    bias_spec = pl.BlockSpec((DIL_HEADS // 2, 2 * DIL_BLOCK, 2 * DIL_BLOCK), fix3)
    in_specs = [tok_q, tok_p, tok_q,
                pl.BlockSpec(blk, lambda pb, n: (pn_(n), vcol + pb)), pl.BlockSpec(blk, lambda pb, n: (qn_(n), vcol + pb)),
                tok_q, tok_q, tok_q, bias_spec]
    tok_shape = jax.ShapeDtypeStruct((T, DIL_WIDTH), F32)
    one, two = pltpu.VMEM((n_it, DIL_BLOCK, 128), F32), pltpu.VMEM((n_it, 2 * DIL_BLOCK, 128), F32)
    dq, dk, dv, db = pl.pallas_call(
        body, name=f"dil_bwd_d{d}", grid=(DIL_HEADS // 2 // P, nblk + 1), in_specs=in_specs,
        out_specs=[tok_q, tok_p, tok_p, bias_spec],
        out_shape=[tok_shape, tok_shape, tok_shape, jax.ShapeDtypeStruct(bias.shape, F32)],
        scratch_shapes=[one, two, two] + [one] * 8,
        compiler_params=_params(2),
    )(qh, kh, kh, proj, proj, o, lse, do, bias)
    return (dq, dk, dv), db


def _t5_bucket(dist):
    max_exact = REL_BUCKETS // 2
    dd = np.maximum(dist, 1).astype(np.float32)
    large = max_exact + (np.log(dd / max_exact) / np.log(REL_MAX_DIST / max_exact)
                         * (REL_BUCKETS - max_exact)).astype(np.int32)
    large = np.minimum(large, REL_BUCKETS - 1)
    return np.where(dist < max_exact, dist, large).astype(np.int32)


def _bucket_onehots():
    i = np.arange(DIL_BLOCK)[:, None]
    j = np.arange(DIL_BLOCK)[None, :]
    out = []
    for _, d in DIL_BRANCHES:
        dist = np.concatenate([DIL_BLOCK + i - j, i - j], axis=1)
        bucket = _t5_bucket(np.clip(dist, 0, None) * d).reshape(-1)
        out.append(jnp.asarray(np.eye(REL_BUCKETS, dtype=np.float32)[:, bucket], BF16))
    return out


def _bias_tables(rel_bias, onehots):
    n = len(onehots)

    def body(rb_ref, *refs):
        parts = _split3(rb_ref[...])
        for k in range(n):
            oh = refs[k][...]
            refs[n + k][...] = _dot(parts[0], oh) + _dot(parts[1], oh) + _dot(parts[2], oh)

    flat = pl.pallas_call(
        body, name="bias_tables",
        out_shape=[jax.ShapeDtypeStruct((DIL_HEADS, 2 * DIL_BLOCK * DIL_BLOCK), F32)] * n,
        compiler_params=pltpu.CompilerParams(vmem_limit_bytes=VMEM_LIMIT),
    )(rel_bias, *onehots)
    return [t.reshape(DIL_HEADS // 2, 2 * DIL_BLOCK, 2 * DIL_BLOCK) for t in flat]


def _bias_grad(dbs, onehots):
    n = len(dbs)
    dbs = [t.reshape(DIL_HEADS, 2 * DIL_BLOCK * DIL_BLOCK) for t in dbs]

    def body(*refs):
        acc = jnp.zeros((DIL_HEADS, REL_BUCKETS), F32)
        for k in range(n):
            oh = refs[n + k][...]
            for part in _split3(refs[k][...]):
                acc = acc + _dot_nt(part, oh)
        refs[-1][...] = acc

    return pl.pallas_call(
        body, name="bias_grad",
        out_shape=jax.ShapeDtypeStruct((DIL_HEADS, REL_BUCKETS), F32),
        compiler_params=pltpu.CompilerParams(vmem_limit_bytes=VMEM_LIMIT),
    )(*dbs, *onehots)


def _swap_halves(x):
    lane = lax.broadcasted_iota(jnp.int32, x.shape, 1)
    first = (lane % 64) < 32
    return jnp.where(first, pltpu.roll(x, 96, 1), pltpu.roll(x, 32, 1))


def _rope_tables(T):
    pos = jnp.arange(T, dtype=F32)
    inv_freq = ROPE_BASE ** (-jnp.arange(0, MLA_ROPE, 2, dtype=F32) / MLA_ROPE)
    ang = pos[:, None] * inv_freq[None, :]
    z = jnp.zeros((T, 128 - MLA_ROPE), F32)
    cos = jnp.concatenate([jnp.cos(ang), jnp.cos(ang), z], axis=-1)
    sin = jnp.concatenate([-jnp.sin(ang), jnp.sin(ang), z], axis=-1)
    return cos, sin


def _mla_prep(proj, cos, sin, g_qa, g_kva, g_q, g_k, wq, wkv, tm=512):
    T = proj.shape[0]
    H = MLA_HEADS
    scale = MLA_QK ** -0.5

    def body(cq_ref, ckv_ref, kpe_ref, cos_ref, sin_ref, gqa_ref, gkva_ref, gq_ref, gk_ref, wq_ref, wkv_ref,
             q_ref, k_ref, v_ref):
        cosv, sinv = cos_ref[...], sin_ref[...]

        def rope(x):
            return x * cosv + _swap_halves(x) * sinv

        cq = cq_ref[...]
        qp = _dot((cq * _rstd(cq) * gqa_ref[...]).astype(BF16), wq_ref[...])
        ckv = ckv_ref[...]
        kvp = _dot((ckv * _rstd(ckv) * gkva_ref[...]).astype(BF16), wkv_ref[...])
        kpe = kpe_ref[...]
        for h in range(H):
            a = qp[:, MLA_PAD * h:MLA_PAD * (h + 1)]
            qn = a * _rstd(a, MLA_QK) * gq_ref[...]
            q_ref[h, :, 0:128] = (qn[:, 0:128] * scale).astype(BF16)
            q_ref[h, :, 128:256] = (rope(qn[:, 128:256]) * scale).astype(BF16)
            kn = kvp[:, MLA_PAD * h:MLA_PAD * h + 128]
            r = lax.rsqrt((jnp.sum(kn * kn, axis=-1, keepdims=True)
                           + jnp.sum(kpe * kpe, axis=-1, keepdims=True)) / MLA_QK + EPS)
            k_ref[h, :, 0:128] = (kn * r * gk_ref[:, 0:128]).astype(BF16)
            k_ref[h, :, 128:256] = rope(kpe * r * gk_ref[:, 128:256]).astype(BF16)
            v_ref[h] = kvp[:, MLA_PAD * h + 128:MLA_PAD * (h + 1)].astype(BF16)

    fix = lambda i: (0, 0)
    return pl.pallas_call(
        body, name="mla_prep", grid=(T // tm,),
        in_specs=[pl.BlockSpec((tm, 256), lambda i: (i, 6)), pl.BlockSpec((tm, 128), lambda i: (i, 14)),
                  pl.BlockSpec((tm, 128), lambda i: (i, 15)),
                  pl.BlockSpec((tm, 128), lambda i: (i, 0)), pl.BlockSpec((tm, 128), lambda i: (i, 0)),
                  pl.BlockSpec((1, 256), fix), pl.BlockSpec((1, 128), fix),
                  pl.BlockSpec((1, 256), fix), pl.BlockSpec((1, 256), fix),
                  pl.BlockSpec((256, H * MLA_PAD), fix), pl.BlockSpec((128, H * MLA_PAD), fix)],
        out_specs=[pl.BlockSpec((H, tm, MLA_PAD), lambda i: (0, i, 0)), pl.BlockSpec((H, tm, MLA_PAD), lambda i: (0, i, 0)),
                   pl.BlockSpec((H, tm, MLA_V), lambda i: (0, i, 0))],
        out_shape=[jax.ShapeDtypeStruct((H, T, MLA_PAD), BF16), jax.ShapeDtypeStruct((H, T, MLA_PAD), BF16),
                   jax.ShapeDtypeStruct((H, T, MLA_V), BF16)],
        compiler_params=_params(1),
    )(proj, proj, proj, cos, sin, g_qa, g_kva, g_q, g_k, wq, wkv)


def _mla_prep_bwd(proj, cos, sin, g_qa, g_kva, g_q, g_k, wq, wkv, dq, dk, dv, tm=512):
    T = proj.shape[0]
    H = MLA_HEADS
    scale = MLA_QK ** -0.5

    def body(cq_ref, ckv_ref, kpe_ref, cos_ref, sin_ref, gqa_ref, gkva_ref, gq_ref, gk_ref, wq_ref, wkv_ref,
             dq_ref, dk_ref, dv_ref,
             dcq_ref, dckv_ref, dkpe_ref, cqn_ref, ckvn_ref, dqp_ref, dkvp_ref,
             dgqa_ref, dgkva_ref, dgq_ref, dgk_ref):
        @pl.when(pl.program_id(0) == 0)
        def _():
            for ref in (dgqa_ref, dgkva_ref, dgq_ref, dgk_ref):
                ref[...] = jnp.zeros_like(ref)

        cosv, sinv = cos_ref[...], sin_ref[...]

        def rope_bwd(dy):
            return dy * cosv + _swap_halves(dy * sinv)

        cq = cq_ref[...]
        rcq = _rstd(cq)
        cqn = (cq * rcq * gqa_ref[...]).astype(BF16)
        cqn_ref[...] = cqn
        qp = _dot(cqn, wq_ref[...])
        ckv = ckv_ref[...]
        rckv = _rstd(ckv)
        ckvn = (ckv * rckv * gkva_ref[...]).astype(BF16)
        ckvn_ref[...] = ckvn
        kvp = _dot(ckvn, wkv_ref[...])
        kpe = kpe_ref[...]
        dkpe = jnp.zeros_like(kpe)
        dgq = jnp.zeros((1, MLA_PAD), F32)
        dgk = jnp.zeros((1, MLA_PAD), F32)
        for h in range(H):
            a = qp[:, MLA_PAD * h:MLA_PAD * (h + 1)]
            dqh = dq_ref[h]
            dn = jnp.concatenate([dqh[:, 0:128], rope_bwd(dqh[:, 128:256])], axis=-1) * scale
            da, dg = _rms_bwd(dn, a, gq_ref[...], _rstd(a, MLA_QK), MLA_QK)
            dgq = dgq + jnp.sum(dg, axis=0, keepdims=True)
            dqp_ref[:, MLA_PAD * h:MLA_PAD * (h + 1)] = da.astype(BF16)

            ak = jnp.concatenate([kvp[:, MLA_PAD * h:MLA_PAD * h + 128], kpe], axis=-1)
            dkh = dk_ref[h]
            dnk = jnp.concatenate([dkh[:, 0:128], rope_bwd(dkh[:, 128:256])], axis=-1)
            dak, dg = _rms_bwd(dnk, ak, gk_ref[...], _rstd(ak, MLA_QK), MLA_QK)
            dgk = dgk + jnp.sum(dg, axis=0, keepdims=True)
            dkpe = dkpe + dak[:, 128:256]
            dkvp_ref[:, MLA_PAD * h:MLA_PAD * h + 128] = dak[:, 0:128].astype(BF16)
            dkvp_ref[:, MLA_PAD * h + 128:MLA_PAD * (h + 1)] = dv_ref[h].astype(BF16)
        dkpe_ref[...] = dkpe
        dgq_ref[...] += dgq
        dgk_ref[...] += dgk
        dcq, dg = _rms_bwd(_dot_nt(dqp_ref[...], wq_ref[...]), cq, gqa_ref[...], rcq)
        dcq_ref[...] = dcq
        dgqa_ref[...] += jnp.sum(dg, axis=0, keepdims=True)
        dckv, dg = _rms_bwd(_dot_nt(dkvp_ref[...], wkv_ref[...]), ckv, gkva_ref[...], rckv)
        dckv_ref[...] = dckv
        dgkva_ref[...] += jnp.sum(dg, axis=0, keepdims=True)

    fix = lambda i: (0, 0)
    row = lambda i: (i, 0)
    head = lambda i: (0, i, 0)
    return pl.pallas_call(
        body, name="mla_prep_bwd", grid=(T // tm,),
        in_specs=[pl.BlockSpec((tm, 256), lambda i: (i, 6)), pl.BlockSpec((tm, 128), lambda i: (i, 14)),
                  pl.BlockSpec((tm, 128), lambda i: (i, 15)),
                  pl.BlockSpec((tm, 128), row), pl.BlockSpec((tm, 128), row),
                  pl.BlockSpec((1, 256), fix), pl.BlockSpec((1, 128), fix),
                  pl.BlockSpec((1, 256), fix), pl.BlockSpec((1, 256), fix),
                  pl.BlockSpec((256, H * MLA_PAD), fix), pl.BlockSpec((128, H * MLA_PAD), fix),
                  pl.BlockSpec((H, tm, MLA_PAD), head), pl.BlockSpec((H, tm, MLA_PAD), head),
                  pl.BlockSpec((H, tm, MLA_V), head)],
        out_specs=[pl.BlockSpec((tm, 256), row), pl.BlockSpec((tm, 128), row), pl.BlockSpec((tm, 128), row),
                   pl.BlockSpec((tm, 256), row), pl.BlockSpec((tm, 128), row),
                   pl.BlockSpec((tm, H * MLA_PAD), row), pl.BlockSpec((tm, H * MLA_PAD), row),
                   pl.BlockSpec((1, 256), fix), pl.BlockSpec((1, 128), fix),
                   pl.BlockSpec((1, 256), fix), pl.BlockSpec((1, 256), fix)],
        out_shape=[jax.ShapeDtypeStruct((T, 256), F32), jax.ShapeDtypeStruct((T, 128), F32),
                   jax.ShapeDtypeStruct((T, 128), F32),
                   jax.ShapeDtypeStruct((T, 256), BF16), jax.ShapeDtypeStruct((T, 128), BF16),
                   jax.ShapeDtypeStruct((T, H * MLA_PAD), BF16), jax.ShapeDtypeStruct((T, H * MLA_PAD), BF16),
                   jax.ShapeDtypeStruct((1, 256), F32), jax.ShapeDtypeStruct((1, 128), F32),
                   jax.ShapeDtypeStruct((1, 256), F32), jax.ShapeDtypeStruct((1, 256), F32)],
        compiler_params=_params(1),
    )(proj, proj, proj, cos, sin, g_qa, g_kva, g_q, g_k, wq, wkv, dq, dk, dv)


def _causal_pairs(T, tq, tk, key_major):
    pairs = [(i, j) for i in range(T // tq) for j in range(T // tk) if j * tk <= i * tq + tq - 1]
    if key_major:
        pairs.sort(key=lambda p: (p[1], p[0]))
    outer = [p[1] if key_major else p[0] for p in pairs]
    first = [int(t == 0 or outer[t] != outer[t - 1]) for t in range(len(pairs))]
    last = [int(t == len(pairs) - 1 or outer[t] != outer[t + 1]) for t in range(len(pairs))]
    tab = lambda v: jnp.asarray(np.array(v, np.int32))
    return tab([p[0] for p in pairs]), tab([p[1] for p in pairs]), tab(first), tab(last)


def _causal_scores(qv, kv, qi, ki, row0, tq, tk, masked):
    s = _dot_nt(qv, kv)
    if masked:
        row = lax.broadcasted_iota(jnp.int32, s.shape, 0) + (qi * tq + row0)
        col = lax.broadcasted_iota(jnp.int32, s.shape, 1) + ki * tk
        s = jnp.where(col <= row, s, NEG)
    return s


def _mla_attn(q, k, v, ride=None, tq=1024, tk=2048, rc=256):
    H, T, _ = q.shape
    tables = _causal_pairs(T, tq, tk, key_major=False)
    n_pairs = int(tables[0].shape[0])
    r_args, r_in, r_shape, r_out, r_scratch = _ride_parts(ride)

    def body(qt, kt, ft, lt, q_ref, k_ref, v_ref, o_ref, lse_ref, m_s, l_s, acc):
        t = pl.program_id(1)
        qi, ki = qt[t], kt[t]

        @pl.when(ft[t] == 1)
        def _():
            m_s[...] = jnp.full_like(m_s, NEG)
            l_s[...] = jnp.zeros_like(l_s)
            acc[...] = jnp.zeros_like(acc)

        def update(masked):
            kk, vv = k_ref[...], v_ref[...]
            for c in range(tq // rc):
                rows = pl.ds(c * rc, rc)
                s = _causal_scores(q_ref[rows, :], kk, qi, ki, c * rc, tq, tk, masked)
                m_old = m_s[rows, :]
                m_new = jnp.maximum(m_old, jnp.max(s, axis=-1, keepdims=True))
                alpha = jnp.exp(m_old - m_new)
                p = jnp.exp(s - m_new)
                l_s[rows, :] = alpha * l_s[rows, :] + jnp.sum(p, axis=-1, keepdims=True)
                acc[rows, :] = alpha * acc[rows, :] + _dot(p.astype(BF16), vv)
                m_s[rows, :] = m_new

        diagonal = (ki + 1) * tk - 1 > qi * tq

        @pl.when(diagonal)
        def _():
            update(True)

        @pl.when(jnp.logical_not(diagonal))
        def _():
            update(False)

        @pl.when(lt[t] == 1)
        def _():
            o_ref[...] = acc[...] / l_s[...]
            lse_ref[...] = jnp.broadcast_to(m_s[...] + jnp.log(l_s[...]), lse_ref.shape)

    qrow = lambda h, t, qt, kt, ft, lt: (h, qt[t], 0)
    krow = lambda h, t, qt, kt, ft, lt: (h, kt[t], 0)
    first = lambda: (pl.program_id(0) == 0) & (pl.program_id(1) == 0)
    last = lambda: (pl.program_id(0) == H - 1) & (pl.program_id(1) == n_pairs - 1)
    outs = pl.pallas_call(
        _riding(body, 7, 2, 3, ride, first, last), name="mla_attn",
        grid_spec=pltpu.PrefetchScalarGridSpec(
            num_scalar_prefetch=4, grid=(H, n_pairs),
            in_specs=[pl.BlockSpec((None, tq, MLA_PAD), qrow), pl.BlockSpec((None, tk, MLA_PAD), krow),
                      pl.BlockSpec((None, tk, MLA_V), krow)] + r_in,
            out_specs=[pl.BlockSpec((tq, MLA_V), lambda h, t, qt, kt, ft, lt: (qt[t], h)),
                       pl.BlockSpec((None, tq, 128), qrow)] + r_out,
            scratch_shapes=[pltpu.VMEM((tq, 1), F32), pltpu.VMEM((tq, 1), F32), pltpu.VMEM((tq, MLA_V), F32)]
            + r_scratch),
        out_shape=[jax.ShapeDtypeStruct((T, H * MLA_V), F32), jax.ShapeDtypeStruct((H, T, 128), F32)] + r_shape,
        compiler_params=_params(2),
    )(*tables, q, k, v, *r_args)
    return outs[:2], outs[2:]


def _mla_attn_bwd(q, k, v, o, lse, do, ride=None, tq=1024, tk=1024, rc=512):
    H, T, _ = q.shape
    tables = _causal_pairs(T, tq, tk, key_major=True)
    n_pairs = int(tables[0].shape[0])
    r_args, r_in, r_shape, r_out, r_scratch = _ride_parts(ride)

    def body(qt, kt, ft, lt, q_ref, k_ref, v_ref, o_ref, lse_ref, do_ref, dq_ref, dk_ref, dv_ref, dk_s, dv_s):
        t = pl.program_id(1)
        qi, ki = qt[t], kt[t]

        @pl.when(t == 0)
        def _():
            dq_ref[...] = jnp.zeros_like(dq_ref)

        @pl.when(ft[t] == 1)
        def _():
            dk_s[...] = jnp.zeros_like(dk_s)
            dv_s[...] = jnp.zeros_like(dv_s)

        def update(masked):
            kk, vv = k_ref[...], v_ref[...]
            for c in range(tq // rc):
                rows = pl.ds(c * rc, rc)
                qv, dov = q_ref[rows, :], do_ref[rows, :]
                delta = jnp.sum(dov * o_ref[rows, :], axis=-1, keepdims=True)
                lse_v = jnp.max(lse_ref[rows, :], axis=-1, keepdims=True)
                p = jnp.exp(_causal_scores(qv, kk, qi, ki, c * rc, tq, tk, masked) - lse_v)
                dob = dov.astype(BF16)
                dv_s[...] += _dot_tn(p.astype(BF16), dob)
                ds = (p * (_dot_nt(dob, vv) - delta)).astype(BF16)
                dk_s[...] += _dot_tn(ds, qv)
                out_rows = pl.ds(pl.multiple_of(qi * tq + c * rc, rc), rc)
                dq_ref[out_rows, :] += _dot(ds, kk)

        diagonal = (ki + 1) * tk - 1 > qi * tq

        @pl.when(diagonal)
        def _():
            update(True)

        @pl.when(jnp.logical_not(diagonal))
        def _():
            update(False)

        @pl.when(lt[t] == 1)
        def _():
            dk_ref[...] = dk_s[...]
            dv_ref[...] = dv_s[...]

    qrow = lambda h, t, qt, kt, ft, lt: (h, qt[t], 0)
    krow = lambda h, t, qt, kt, ft, lt: (h, kt[t], 0)
    qcol = lambda h, t, qt, kt, ft, lt: (qt[t], h)
    first = lambda: (pl.program_id(0) == 0) & (pl.program_id(1) == 0)
    last = lambda: (pl.program_id(0) == H - 1) & (pl.program_id(1) == n_pairs - 1)
    outs = pl.pallas_call(
        _riding(body, 10, 3, 2, ride, first, last), name="mla_attn_bwd",
        grid_spec=pltpu.PrefetchScalarGridSpec(
            num_scalar_prefetch=4, grid=(H, n_pairs),
            in_specs=[pl.BlockSpec((None, tq, MLA_PAD), qrow), pl.BlockSpec((None, tk, MLA_PAD), krow),
                      pl.BlockSpec((None, tk, MLA_V), krow), pl.BlockSpec((tq, MLA_V), qcol),
                      pl.BlockSpec((None, tq, 128), qrow), pl.BlockSpec((tq, MLA_V), qcol)] + r_in,
            out_specs=[pl.BlockSpec((None, T, MLA_PAD), lambda h, t, qt, kt, ft, lt: (h, 0, 0)),
                       pl.BlockSpec((None, tk, MLA_PAD), krow), pl.BlockSpec((None, tk, MLA_V), krow)] + r_out,
            scratch_shapes=[pltpu.VMEM((tk, MLA_PAD), F32), pltpu.VMEM((tk, MLA_V), F32)] + r_scratch),
        out_shape=[jax.ShapeDtypeStruct((H, T, MLA_PAD), F32), jax.ShapeDtypeStruct((H, T, MLA_PAD), F32),
                   jax.ShapeDtypeStruct((H, T, MLA_V), F32)] + r_shape,
        compiler_params=_params(2),
    )(*tables, q, k, v, o, lse, do, *r_args)
    return outs[:3], outs[3:]


def _pair_gain(g):
    return jnp.tile(g.reshape(1, DIL_HD), (1, 2))


def _pad_gain(g):
    return jnp.pad(g.reshape(1, MLA_QK), ((0, 0), (0, MLA_PAD - MLA_QK)))


def _local_step(x, target, s, comm):
    T = x.shape[0]
    w = comm.w
    gq, gk = _pair_gain(s["dil_q_norm"]) * DIL_HD ** -0.5, _pair_gain(s["dil_k_norm"])
    g_q, g_k = _pad_gain(s["mla_q_norm"]), _pad_gain(s["mla_k_norm"])
    cos, sin = _rope_tables(T)
    onehots = _bucket_onehots()
    biases = _bias_tables(s["rel_bias"], onehots)

    (x1, h1, gate1, up1), got = _ffn_fwd(x, s["ffn1_norm"], w["ffn1_w_gate"], w["ffn1_w_up"], w["ffn1_w_down"],
                                         ride=comm.gather(_GROUPS["attn"]))
    comm.weights_landed(_GROUPS["attn"], got)
    hm, proj, qh, kh = _in_proj(x1, s["mix_norm"], w["w_in"], gq, gk)
    dil = None
    for (_, d), bias in zip(DIL_BRANCHES, biases):
        dil = _dil_fwd(qh, kh, proj, bias, d, dil)
    o_dil, lse_dil = dil
    q, k, v = _mla_prep(proj, cos, sin, s["mla_q_a_norm"], s["mla_kv_a_norm"], g_q, g_k, w["mla_w_q_b"], w["mla_w_kv_b"])
    (o_mla, lse_mla), got = _mla_attn(q, k, v, ride=comm.gather(_GROUPS["ffn2"]))
    comm.weights_landed(_GROUPS["ffn2"], got)
    x2, oc = _out_proj(x1, o_dil, o_mla, s["out_norm_dil"], s["out_norm_mla"], w["w_out"])
    (y, h2, gate2, up2), _ = _ffn_fwd(x2, s["ffn2_norm"], w["ffn2_w_gate"], w["ffn2_w_up"], w["ffn2_w_down"])
    dy, loss = _loss_grad(y, target)

    gw, gs = {}, {}

    def ffn_grads(name, dy_in, x_in, h, gate, up, ride=None, scatter_early=False):
        (dx, a, dg, du, dyh, dgain), got = _ffn_bwd(dy_in, x_in, s[name + "_norm"], gate, up,
                                                    w[name + "_w_gate"], w[name + "_w_up"], w[name + "_w_down"], ride=ride)
        gs[name + "_norm"] = dgain
        down, gate_n, up_n = (name + "_w_down",), (name + "_w_gate",), (name + "_w_up",)
        gw[down[0]], _ = _matmul_tn(a, dyh, 1408, 1024)
        gw[gate_n[0]], landed = _matmul_tn(h, dg, 1024, 1408, ride=comm.scatter(down, gw) if scatter_early else None)
        comm.grads_landed(down, landed)
        gw[up_n[0]], landed = _matmul_tn(h, du, 1024, 1408, ride=comm.scatter(gate_n, gw) if scatter_early else None)
        comm.grads_landed(gate_n, landed)
        return dx, got

    dx2, _ = ffn_grads("ffn2", dy, x2, h2, gate2, up2)
    gw["w_out"], _ = _matmul_tn(oc, dx2, 1024, 1024)
    do_dil, do_mla, gs["out_norm_dil"], gs["out_norm_mla"] = _out_proj_bwd(
        dx2, o_dil, o_mla, s["out_norm_dil"], s["out_norm_mla"], w["w_out"])

    (dq, dk, dv), got = _mla_attn_bwd(q, k, v, o_mla, lse_mla, do_mla, ride=comm.scatter(_GROUPS["ffn2"], gw))
    comm.grads_landed(_GROUPS["ffn2"], got)
    (dcq, dckv, dkpe, cqn, ckvn, dqp, dkvp, gs["mla_q_a_norm"], gs["mla_kv_a_norm"], dg_q, dg_k) = _mla_prep_bwd(
        proj, cos, sin, s["mla_q_a_norm"], s["mla_kv_a_norm"], g_q, g_k, w["mla_w_q_b"], w["mla_w_kv_b"], dq, dk, dv)
    gs["mla_q_norm"], gs["mla_k_norm"] = dg_q[:, :MLA_QK], dg_k[:, :MLA_QK]
    gw["mla_w_q_b"], _ = _matmul_tn(cqn, dqp, 256, 1024)
    gw["mla_w_kv_b"], _ = _matmul_tn(ckvn, dkvp, 128, 1024)

    dqkv, dbs = [], []
    for (_, d), bias in zip(DIL_BRANCHES, biases):
        triple, db = _dil_bwd(qh, kh, proj, o_dil, lse_dil, do_dil, bias, d)
        dqkv.append(triple)
        dbs.append(db)
    gs["rel_bias"] = _bias_grad(dbs, onehots)

    dx1, dproj, gs["mix_norm"], dgq, dgk = _in_proj_bwd(dx2, x1, s["mix_norm"], w["w_in"], proj, gq, gk,
                                                        dqkv, dcq, dckv, dkpe)
    gs["dil_q_norm"] = (dgq[:, :DIL_HD] + dgq[:, DIL_HD:]) * DIL_HD ** -0.5
    gs["dil_k_norm"] = dgk[:, :DIL_HD] + dgk[:, DIL_HD:]
    gw["w_in"], _ = _matmul_tn(hm, dproj, 1024, 1024)
    grad_x, got = ffn_grads("ffn1", dx1, x, h1, gate1, up1, ride=comm.scatter(_GROUPS["attn"], gw), scatter_early=True)
    comm.grads_landed(_GROUPS["attn"], got)
    return loss, grad_x, gw, gs


def _position():
    x, y, c = lax.axis_index("x"), lax.axis_index("y"), lax.axis_index("c")
    return x, y, c, 4 * x + 2 * y + c


def _peer(x, y, c, k):
    px = 1 - x if k & 4 else x
    py = 1 - y if k & 2 else y
    pc = 1 - c if k & 1 else c
    return (px, py, pc), 4 * px + 2 * py + pc


class _Ride:
    def __init__(self, arrays, scatter):
        self.arrays, self.scatter = list(arrays), list(scatter)
        self.n = n = len(self.arrays)
        self.specs = [pl.BlockSpec(memory_space=pl.ANY)] * n
        self.out_shape = [jax.ShapeDtypeStruct(a.shape if sc else (N_DEV,) + a.shape, a.dtype)
                          for a, sc in zip(self.arrays, self.scatter)]
        self.scratch = [pltpu.SemaphoreType.DMA((n, N_DEV - 1)), pltpu.SemaphoreType.DMA((n, N_DEV - 1)),
                        pltpu.SemaphoreType.DMA((n,))]

    def _copies(self, ins, outs, sems):
        send_sems, recv_sems, local_sems = sems
        x, y, c, me = _position()
        copies = []
        for a in range(self.n):
            src = ins[a].at[me] if self.scatter[a] else ins[a]
            copies.append(pltpu.make_async_copy(src, outs[a].at[me], local_sems.at[a]))
        for k in range(1, N_DEV):
            peer, peer_idx = _peer(x, y, c, k)
            for a in range(self.n):
                src = ins[a].at[peer_idx] if self.scatter[a] else ins[a]
                copies.append(pltpu.make_async_remote_copy(
                    src_ref=src, dst_ref=outs[a].at[me], send_sem=send_sems.at[a, k - 1], recv_sem=recv_sems.at[a, k - 1],
                    device_id=peer, device_id_type=pl.DeviceIdType.MESH))
        return copies

    def start(self, ins, outs, sems):
        for cp in self._copies(ins, outs, sems):
            cp.start()

    def wait(self, ins, outs, sems):
        for cp in self._copies(ins, outs, sems):
            cp.wait()


def _ride_parts(ride):
    if ride is None:
        return [], [], [], [], []
    return ride.arrays, ride.specs, ride.out_shape, ride.specs, ride.scratch


def _riding(body, n_in, n_out, n_scratch, ride, first, last):
    if ride is None:
        return body
    n = ride.n
    i1, i2 = n_in + n, n_in + n + n_out
    i3, i4 = i2 + n, i2 + n + n_scratch

    def wrapped(*refs):
        ins, outs, sems = refs[n_in:i1], refs[i2:i3], refs[i4:]

        @pl.when(first())
        def _():
            ride.start(ins, outs, sems)

        body(*refs[:n_in], *refs[i1:i2], *refs[i3:i4])

        @pl.when(last())
        def _():
            ride.wait(ins, outs, sems)

    return wrapped


def _gather_two_level(arrays, name):
    n = len(arrays)
    out_shape = [jax.ShapeDtypeStruct((N_DEV,) + a.shape, a.dtype) for a in arrays]

    def body(*refs):
        ins, outs = refs[:n], refs[n:2 * n]
        send_sems, recv_sems, local_sems = refs[2 * n:]
        x, y, c, me = _position()
        sibling = (x, y, 1 - c)
        chips = [(1 - x, y), (x, 1 - y), (1 - x, 1 - y)]
        block = lambda px, py, pc: 4 * px + 2 * py + pc

        def copy(a, k, blk, to, src=None):
            dst = outs[a].at[blk]
            return pltpu.make_async_remote_copy(
                src_ref=dst if src is None else src, dst_ref=dst, send_sem=send_sems.at[a, k], recv_sem=recv_sems.at[a, k],
                device_id=to, device_id_type=pl.DeviceIdType.MESH)

        local = [pltpu.make_async_copy(ins[a], outs[a].at[me], local_sems.at[a]) for a in range(n)]
        first = []
        for a in range(n):
            first.append(copy(a, 0, me, sibling, src=ins[a]))
            first += [copy(a, 1 + j, me, (*chip, c), src=ins[a]) for j, chip in enumerate(chips)]
        for cp in local + first:
            cp.start()
        passed = []
        for j, chip in enumerate(chips):
            for a in range(n):
                copy(a, 1 + j, block(*chip, c), sibling).wait_recv()
                passed.append(copy(a, 4 + j, block(*chip, c), sibling))
                passed[-1].start()
        for a in range(n):
            copy(a, 0, block(x, y, 1 - c), sibling).wait_recv()
            for j, chip in enumerate(chips):
                copy(a, 4 + j, block(*chip, 1 - c), sibling).wait_recv()
        for cp in first + passed:
            cp.wait_send()
        for cp in local:
            cp.wait()

    any_spec = [pl.BlockSpec(memory_space=pl.ANY)] * n
    return pl.pallas_call(
        body, name=name, in_specs=any_spec, out_specs=any_spec, out_shape=out_shape,
        scratch_shapes=[pltpu.SemaphoreType.DMA((n, N_DEV - 1)), pltpu.SemaphoreType.DMA((n, N_DEV - 1)),
                        pltpu.SemaphoreType.DMA((n,))],
    )(*arrays)


def _exchange(ride, name):
    def body(*refs):
        parts = refs[:ride.n], refs[ride.n:2 * ride.n], refs[2 * ride.n:]
        ride.start(*parts)
        ride.wait(*parts)

    return pl.pallas_call(body, name=name, in_specs=ride.specs, out_specs=ride.specs, out_shape=ride.out_shape,
                          scratch_shapes=ride.scratch)(*ride.arrays)


def _adamw_math(wv, g, m, v):
    m = ADAM_B1 * m + (1.0 - ADAM_B1) * g
    v = ADAM_B2 * v + (1.0 - ADAM_B2) * (g * g)
    m_hat = m / (1.0 - ADAM_B1 ** ADAM_STEP)
    v_hat = v / (1.0 - ADAM_B2 ** ADAM_STEP)
    delta = -ADAM_LR * (m_hat / (jnp.sqrt(v_hat) + ADAM_EPS) + ADAM_WD * wv)
    return delta, m, v


def _adamw(parts, wv, m, v):
    _, R, C = wv.shape
    tr = max(t for t in range(16, 257, 16) if R % t == 0)

    def body(p_ref, w_ref, m_ref, v_ref, g_ref, d_ref, mo_ref, vo_ref):
        g = p_ref[0].astype(F32)
        for j in range(1, N_DEV):
            g = g + p_ref[j].astype(F32)
        d, mn, vn = _adamw_math(w_ref[0], g, m_ref[0], v_ref[0])
        g_ref[0] = g
        d_ref[0] = d
        mo_ref[0] = mn
        vo_ref[0] = vn

    blk = pl.BlockSpec((1, tr, C), lambda i: (0, i, 0))
    out = jax.ShapeDtypeStruct((1, R, C), F32)
    return pl.pallas_call(
        body, name="adamw", grid=(R // tr,),
        in_specs=[pl.BlockSpec((N_DEV, tr, C), lambda i: (0, i, 0)), blk, blk, blk],
        out_specs=[blk] * 4, out_shape=[out] * 4,
        compiler_params=_params(1),
    )(parts, wv, m, v)


_ROW_SHARDED = ("ffn1_w_down", "ffn2_w_down", "w_out")
_GROUPS = {"ffn1": ("ffn1_w_gate", "ffn1_w_up", "ffn1_w_down"),
           "ffn2": ("ffn2_w_gate", "ffn2_w_up", "ffn2_w_down"),
           "attn": ("w_in", "mla_w_q_b", "mla_w_kv_b", "w_out")}
_SMALL = ("ffn1_norm", "mix_norm", "ffn2_norm", "out_norm_dil", "out_norm_mla", "mla_q_a_norm", "rel_bias",
          "mla_q_norm", "mla_k_norm", "mla_kv_a_norm", "dil_q_norm", "dil_k_norm")
_SMALL_ROWS = 48


def _cols_to_full(g):
    return g.transpose(1, 0, 2).reshape(g.shape[1], N_DEV * g.shape[2])


def _full_to_cols(f):
    return f.reshape(f.shape[0], N_DEV, f.shape[1] // N_DEV).transpose(1, 0, 2)


def _to_full(name, g):
    if name in _ROW_SHARDED:
        return g.reshape(-1, g.shape[-1])
    f = _cols_to_full(g)
    if name == "w_in":
        f = jnp.pad(f, ((0, 0), (0, PROJ_PAD - PROJ_COLS)))
    if name == "mla_w_q_b":
        f = jnp.pad(f.reshape(-1, MLA_HEADS, MLA_QK), ((0, 0), (0, 0), (0, MLA_PAD - MLA_QK)))
        f = f.reshape(-1, MLA_HEADS * MLA_PAD)
    return f


def _to_parts(name, f):
    if name in _ROW_SHARDED:
        return f.reshape(N_DEV, -1, f.shape[-1]).astype(BF16)
    if name == "w_in":
        f = f[:, :PROJ_COLS]
    if name == "mla_w_q_b":
        f = f.reshape(-1, MLA_HEADS, MLA_PAD)[:, :, :MLA_QK].reshape(-1, MLA_HEADS * MLA_QK)
    return _full_to_cols(f).astype(BF16)


class _Comm:
    def __init__(self, shards):
        self.shards, self.w, self.recv = shards, {}, {}

    def gather(self, names):
        return _Ride([self.shards[n] for n in names], [False] * len(names))

    def scatter(self, names, grads):
        return _Ride([_to_parts(n, grads[n]) for n in names], [True] * len(names))

    def weights_landed(self, names, got):
        self.w.update({n: _to_full(n, g) for n, g in zip(names, got)})

    def grads_landed(self, names, got):
        self.recv.update(zip(names, got))


def _pack_small(parts, extra):
    flat = jnp.concatenate([parts[n].reshape(-1) for n in _SMALL] + [extra.reshape(-1)])
    return jnp.pad(flat, (0, _SMALL_ROWS * 128 - flat.shape[0])).reshape(_SMALL_ROWS, 128)


def _unpack_small(packed, shapes):
    flat, out, off = packed.reshape(-1), {}, 0
    for n in _SMALL:
        size = math.prod(shapes[n])
        out[n] = flat[off:off + size].reshape(shapes[n])
        off += size
    return out, flat[off]


_NAMES = ("ffn1_norm", "ffn1_w_gate", "ffn1_w_up", "ffn1_w_down", "mix_norm", "w_in", "dil_q_norm", "dil_k_norm",
          "rel_bias", "mla_q_a_norm", "mla_w_q_b", "mla_kv_a_norm", "mla_w_kv_b", "mla_q_norm", "mla_k_norm",
          "out_norm_dil", "out_norm_mla", "w_out", "ffn2_norm", "ffn2_w_gate", "ffn2_w_up", "ffn2_w_down")


def kernel(x, ffn1_norm, ffn1_w_gate, ffn1_w_up, ffn1_w_down, mix_norm, w_in, dil_q_norm, dil_k_norm, rel_bias, mla_q_a_norm, mla_w_q_b, mla_kv_a_norm, mla_w_kv_b, mla_q_norm, mla_k_norm, out_norm_dil, out_norm_mla, w_out, ffn2_norm, ffn2_w_gate, ffn2_w_up, ffn2_w_down, loss_target, m_ffn1_norm, m_ffn1_w_gate, m_ffn1_w_up, m_ffn1_w_down, m_mix_norm, m_w_in, m_dil_q_norm, m_dil_k_norm, m_rel_bias, m_mla_q_a_norm, m_mla_w_q_b, m_mla_kv_a_norm, m_mla_w_kv_b, m_mla_q_norm, m_mla_k_norm, m_out_norm_dil, m_out_norm_mla, m_w_out, m_ffn2_norm, m_ffn2_w_gate, m_ffn2_w_up, m_ffn2_w_down, v_ffn1_norm, v_ffn1_w_gate, v_ffn1_w_up, v_ffn1_w_down, v_mix_norm, v_w_in, v_dil_q_norm, v_dil_k_norm, v_rel_bias, v_mla_q_a_norm, v_mla_w_q_b, v_mla_kv_a_norm, v_mla_w_kv_b, v_mla_q_norm, v_mla_k_norm, v_out_norm_dil, v_out_norm_mla, v_w_out, v_ffn2_norm, v_ffn2_w_gate, v_ffn2_w_up, v_ffn2_w_down):
    args = locals()
    wts = {n: args[n] for n in _NAMES}
    mom = {n: args["m_" + n] for n in _NAMES}
    var = {n: args["v_" + n] for n in _NAMES}

    matrices = [n for group in _GROUPS.values() for n in group]
    comm = _Comm({n: wts[n][0].astype(BF16) for n in matrices})
    comm.weights_landed(_GROUPS["ffn1"], _gather_two_level(comm.gather(_GROUPS["ffn1"]).arrays, "gather_first"))
    small = {n: wts[n].reshape(1, -1) if n != "rel_bias" else wts[n] for n in _SMALL}

    loss, grad_x, gw, gs = _local_step(x[0], loss_target[0], small, comm)

    last = comm.scatter(("ffn1_w_up",), gw)
    got = _exchange(_Ride(last.arrays + [_pack_small(gs, loss[0, 0])], last.scatter + [False]), "scatter_last")
    comm.grads_landed(("ffn1_w_up",), got[:-1])

    res = {n: _adamw(comm.recv[n], wts[n], mom[n], var[n]) for n in matrices}
    shapes = {n: wts[n].shape for n in _SMALL}
    zero = jnp.zeros((), F32)
    packed = _adamw(got[-1], _pack_small(wts, zero)[None], _pack_small(mom, zero)[None], _pack_small(var, zero)[None])
    loss_total = None
    for slot, q in enumerate(packed):
        vals, extra = _unpack_small(q, shapes)
        if slot == 0:
            loss_total = extra
        for n in _SMALL:
            res.setdefault(n, [None] * 4)[slot] = vals[n]
    outs = [loss_total, grad_x[None]]
    for slot in range(4):
        outs += [res[n][slot].reshape(wts[n].shape) for n in _NAMES]
    return tuple(outs)
```

```python
import math

import numpy as np
import jax
import jax.numpy as jnp
from jax import lax
from jax.experimental import pallas as pl
from jax.experimental.pallas import tpu as pltpu

F32, BF16 = jnp.float32, jnp.bfloat16
EPS = 1e-6
NEG = -1e30
N_DEV = 8

DIL_HEADS, DIL_HD = 8, 64
DIL_WIDTH = DIL_HEADS * DIL_HD
DIL_BRANCHES = ((128, 1), (512, 4), (2048, 16))
DIL_BLOCK = 128
MLA_HEADS, MLA_NOPE, MLA_ROPE, MLA_V = 4, 128, 64, 128
MLA_QK = MLA_NOPE + MLA_ROPE
MLA_PAD = 256
ROPE_BASE = 10000.0
REL_BUCKETS, REL_MAX_DIST = 32, 2048
PROJ_COLS, PROJ_PAD = 1984, 2048
FFN_RESID = 0.5
ADAM_LR, ADAM_B1, ADAM_B2, ADAM_EPS, ADAM_WD, ADAM_STEP = 0.001, 0.9, 0.999, 1e-08, 0.01, 10
VMEM_LIMIT = 62 * 1024 * 1024

_NT = (((1,), (1,)), ((), ()))
_TN = (((0,), (0,)), ((), ()))


def _dot(a, b):
    return jnp.dot(a, b, preferred_element_type=F32)


def _dot_nt(a, b):
    return lax.dot_general(a, b, _NT, preferred_element_type=F32)


def _dot_tn(a, b):
    return lax.dot_general(a, b, _TN, preferred_element_type=F32)


def _params(n_axes):
    return pltpu.CompilerParams(dimension_semantics=("arbitrary",) * n_axes, vmem_limit_bytes=VMEM_LIMIT)


def _rstd(x, n=None):
    n = x.shape[-1] if n is None else n
    return lax.rsqrt(jnp.sum(x * x, axis=-1, keepdims=True) / n + EPS)


def _rms_bwd(dy, x, g, r, n=None):
    n = x.shape[-1] if n is None else n
    u = dy * g
    dx = r * u - x * (r * r * r) * (jnp.sum(u * x, axis=-1, keepdims=True) / n)
    return dx, dy * x * r


def _sigmoid(x):
    return 1.0 / (1.0 + jnp.exp(-x))


def _split3(x):
    parts = []
    for _ in range(3):
        xb = x.astype(BF16)
        parts.append(xb)
        x = x - xb.astype(F32)
    return parts


def _ffn_fwd(x, gain, wg, wu, wd, ride=None, target=None, tm=512, tf=2816):
    T, D = x.shape
    F = wg.shape[1]
    ni, nj = T // tm, F // tf
    with_loss = target is not None
    r_args, r_in, r_shape, r_out, r_scratch = _ride_parts(ride)

    def body(*refs):
        x_ref, g_ref, wg_ref, wu_ref, wd_ref = refs[:5]
        t_ref = refs[5] if with_loss else None
        xo_ref, h_ref, gate_ref, up_ref = refs[5 + with_loss:9 + with_loss]
        loss_ref = refs[-2] if with_loss else None
        acc = refs[-1]
        i, j = pl.program_id(0), pl.program_id(1)

        @pl.when(j == 0)
        def _():
            xv = x_ref[...]
            h_ref[...] = (xv * _rstd(xv) * g_ref[...]).astype(BF16)
            acc[...] = jnp.zeros_like(acc)

        h = h_ref[...]
        g = _dot(h, wg_ref[...])
        u = _dot(h, wu_ref[...])
        gate_ref[...] = g.astype(BF16)
        up_ref[...] = u.astype(BF16)
        a = (g * _sigmoid(g) * u).astype(BF16)
        acc[...] += _dot(a, wd_ref[...])

        @pl.when(j == nj - 1)
        def _():
            y = x_ref[...] + FFN_RESID * acc[...]
            if with_loss:
                @pl.when(i == 0)
                def _():
                    loss_ref[...] = jnp.zeros_like(loss_ref)

                e = y - t_ref[...]
                xo_ref[...] = e * (1.0 / D)
                loss_ref[...] += (0.5 / D) * jnp.sum(e * e)
            else:
                xo_ref[...] = y

    row = lambda i, j: (i, 0)
    tile = lambda i, j: (i, j)
    n_in, n_out = 5 + with_loss, 4 + with_loss
    first = lambda: (pl.program_id(0) == 0) & (pl.program_id(1) == 0)
    last = lambda: (pl.program_id(0) == ni - 1) & (pl.program_id(1) == nj - 1)
    outs = pl.pallas_call(
        _riding(body, n_in, n_out, 1, ride, first, last), name="ffn_fwd", grid=(ni, nj),
        in_specs=[pl.BlockSpec((tm, D), row), pl.BlockSpec((1, D), lambda i, j: (0, 0)),
                  pl.BlockSpec((D, tf), lambda i, j: (0, j)), pl.BlockSpec((D, tf), lambda i, j: (0, j)),
                  pl.BlockSpec((tf, D), lambda i, j: (j, 0))] + [pl.BlockSpec((tm, D), row)] * with_loss + r_in,
        out_specs=[pl.BlockSpec((tm, D), row), pl.BlockSpec((tm, D), row), pl.BlockSpec((tm, tf), tile),
                   pl.BlockSpec((tm, tf), tile)] + [pl.BlockSpec((1, 128), lambda i, j: (0, 0))] * with_loss + r_out,
        out_shape=[jax.ShapeDtypeStruct((T, D), F32), jax.ShapeDtypeStruct((T, D), BF16),
                   jax.ShapeDtypeStruct((T, F), BF16), jax.ShapeDtypeStruct((T, F), BF16)]
        + [jax.ShapeDtypeStruct((1, 128), F32)] * with_loss + r_shape,
        scratch_shapes=[pltpu.VMEM((tm, D), F32)] + r_scratch,
        compiler_params=_params(2),
    )(x, gain, wg, wu, wd, *([target] if with_loss else []), *r_args)
    return outs[:n_out], outs[n_out:]


def _ffn_bwd(dy, x, gain, gate, up, wg, wu, wd, ride=None, tm=256, tf=2816):
    T, D = x.shape
    F = wg.shape[1]
    ni, nj = T // tm, F // tf
    r_args, r_in, r_shape, r_out, r_scratch = _ride_parts(ride)

    def body(dy_ref, x_ref, g_ref, gate_ref, up_ref, wg_ref, wu_ref, wd_ref,
             dx_ref, a_ref, dg_ref, du_ref, dyh_ref, dgain_ref, acc):
        i, j = pl.program_id(0), pl.program_id(1)

        @pl.when((i == 0) & (j == 0))
        def _():
            dgain_ref[...] = jnp.zeros_like(dgain_ref)

        @pl.when(j == 0)
        def _():
            dyh_ref[...] = (FFN_RESID * dy_ref[...]).astype(BF16)
            acc[...] = jnp.zeros_like(acc)

        da = _dot_nt(dyh_ref[...], wd_ref[...])
        g = gate_ref[...].astype(F32)
        u = up_ref[...].astype(F32)
        sig = _sigmoid(g)
        s = g * sig
        a_ref[...] = (s * u).astype(BF16)
        dg = (da * u * (sig * (1.0 + g * (1.0 - sig)))).astype(BF16)
        du = (da * s).astype(BF16)
        dg_ref[...] = dg
        du_ref[...] = du
        acc[...] += _dot_nt(dg, wg_ref[...]) + _dot_nt(du, wu_ref[...])

        @pl.when(j == nj - 1)
        def _():
            xv = x_ref[...]
            dxn, dgc = _rms_bwd(acc[...], xv, g_ref[...], _rstd(xv))
            dx_ref[...] = dy_ref[...] + dxn
            dgain_ref[...] += jnp.sum(dgc, axis=0, keepdims=True)

    first = lambda: (pl.program_id(0) == 0) & (pl.program_id(1) == 0)
    last = lambda: (pl.program_id(0) == ni - 1) & (pl.program_id(1) == nj - 1)
    outs = pl.pallas_call(
        _riding(body, 8, 6, 1, ride, first, last), name="ffn_bwd", grid=(ni, nj),
        in_specs=[pl.BlockSpec((tm, D), lambda i, j: (i, 0)), pl.BlockSpec((tm, D), lambda i, j: (i, 0)),
                  pl.BlockSpec((1, D), lambda i, j: (0, 0)),
                  pl.BlockSpec((tm, tf), lambda i, j: (i, j)), pl.BlockSpec((tm, tf), lambda i, j: (i, j)),
                  pl.BlockSpec((D, tf), lambda i, j: (0, j)), pl.BlockSpec((D, tf), lambda i, j: (0, j)),
                  pl.BlockSpec((tf, D), lambda i, j: (j, 0))] + r_in,
        out_specs=[pl.BlockSpec((tm, D), lambda i, j: (i, 0)),
                   pl.BlockSpec((tm, tf), lambda i, j: (i, j)), pl.BlockSpec((tm, tf), lambda i, j: (i, j)),
                   pl.BlockSpec((tm, tf), lambda i, j: (i, j)),
                   pl.BlockSpec((tm, D), lambda i, j: (i, 0)), pl.BlockSpec((1, D), lambda i, j: (0, 0))] + r_out,
        out_shape=[jax.ShapeDtypeStruct((T, D), F32), jax.ShapeDtypeStruct((T, F), BF16),
                   jax.ShapeDtypeStruct((T, F), BF16), jax.ShapeDtypeStruct((T, F), BF16),
                   jax.ShapeDtypeStruct((T, D), BF16), jax.ShapeDtypeStruct((1, D), F32)] + r_shape,
        scratch_shapes=[pltpu.VMEM((tm, D), F32)] + r_scratch,
        compiler_params=_params(2),
    )(dy, x, gain, gate, up, wg, wu, wd, *r_args)
    return outs[:6], outs[6:]


def _matmul_tn(a, b, tk, tn, ride=None, tt=2048):
    T, K = a.shape
    N = b.shape[1]
    tk, tn = min(tk, K), min(tn, N)
    grid = (K // tk, N // tn, T // tt)
    r_args, r_in, r_shape, r_out, r_scratch = _ride_parts(ride)

    def body(a_ref, b_ref, o_ref):
        @pl.when(pl.program_id(2) == 0)
        def _():
            o_ref[...] = jnp.zeros_like(o_ref)

        o_ref[...] += _dot_tn(a_ref[...].astype(BF16), b_ref[...].astype(BF16))

    first = lambda: (pl.program_id(0) == 0) & (pl.program_id(1) == 0) & (pl.program_id(2) == 0)
    last = lambda: ((pl.program_id(0) == grid[0] - 1) & (pl.program_id(1) == grid[1] - 1)
                    & (pl.program_id(2) == grid[2] - 1))
    outs = pl.pallas_call(
        _riding(body, 2, 1, 0, ride, first, last), name="matmul_tn", grid=grid,
        in_specs=[pl.BlockSpec((tt, tk), lambda k, n, t: (t, k)), pl.BlockSpec((tt, tn), lambda k, n, t: (t, n))] + r_in,
        out_specs=[pl.BlockSpec((tk, tn), lambda k, n, t: (k, n))] + r_out,
        out_shape=[jax.ShapeDtypeStruct((K, N), F32)] + r_shape,
        scratch_shapes=r_scratch,
        compiler_params=_params(3),
    )(a, b, *r_args)
    return outs[0], outs[1:]


def _in_proj(x, gain, w, gq, gk, tm=512):
    T, D = x.shape
    N = w.shape[1]
    W = DIL_WIDTH

    def body(x_ref, g_ref, w_ref, gq_ref, gk_ref, h_ref, p_ref, qh_ref, kh_ref):
        xv = x_ref[...]
        h = (xv * _rstd(xv) * g_ref[...]).astype(BF16)
        h_ref[...] = h
        p_ref[...] = _dot(h, w_ref[...])
        lo = lax.broadcasted_iota(jnp.int32, (tm, 128), 1) < DIL_HD
        for hp in range(DIL_HEADS // 2):
            q = p_ref[:, 128 * hp:128 * (hp + 1)]
            k = p_ref[:, W + 128 * hp:W + 128 * (hp + 1)]
            qh_ref[:, 128 * hp:128 * (hp + 1)] = (q * _pair_rstd(q, lo) * gq_ref[...]).astype(BF16).astype(F32)
            kh_ref[:, 128 * hp:128 * (hp + 1)] = (k * _pair_rstd(k, lo) * gk_ref[...]).astype(BF16).astype(F32)

    row = lambda i: (i, 0)
    fix = lambda i: (0, 0)
    return pl.pallas_call(
        body, name="in_proj", grid=(T // tm,),
        in_specs=[pl.BlockSpec((tm, D), row), pl.BlockSpec((1, D), fix), pl.BlockSpec((D, N), fix),
                  pl.BlockSpec((1, 128), fix), pl.BlockSpec((1, 128), fix)],
        out_specs=[pl.BlockSpec((tm, D), row), pl.BlockSpec((tm, N), row), pl.BlockSpec((tm, W), row),
                   pl.BlockSpec((tm, W), row)],
        out_shape=[jax.ShapeDtypeStruct((T, D), BF16), jax.ShapeDtypeStruct((T, N), F32),
                   jax.ShapeDtypeStruct((T, W), F32), jax.ShapeDtypeStruct((T, W), F32)],
        compiler_params=_params(1),
    )(x, gain, w, gq, gk)


def _in_proj_bwd(dx_up, x, gain, w, proj, gq, gk, dqkv, dcq, dckv, dkpe, ride=None, tm=512):
    T, D = x.shape
    N = w.shape[1]
    W = DIL_WIDTH
    nb = len(dqkv)

    def body(*refs):
        dxu_ref, x_ref, g_ref, w_ref, q_ref, k_ref, gq_ref, gk_ref = refs[:8]
        dil_refs = refs[8:8 + 3 * nb]
        dcq_ref, dckv_ref, dkpe_ref, dx_ref, dp_ref, dgain_ref, dgq_ref, dgk_ref = refs[8 + 3 * nb:]

        @pl.when(pl.program_id(0) == 0)
        def _():
            for ref in (dgain_ref, dgq_ref, dgk_ref):
                ref[...] = jnp.zeros_like(ref)

        lo = lax.broadcasted_iota(jnp.int32, (tm, 128), 1) < DIL_HD
        norms = ((q_ref, gq_ref, dgq_ref), (k_ref, gk_ref, dgk_ref))
        for part in range(3):
            acc = dil_refs[part][...]
            for b in range(1, nb):
                acc = acc + dil_refs[3 * b + part][...]
            if part == 2:
                dp_ref[:, 2 * W:3 * W] = acc.astype(BF16)
                continue
            raw_ref, gn_ref, dgn_ref = norms[part]
            for hp in range(DIL_HEADS // 2):
                raw = raw_ref[:, 128 * hp:128 * (hp + 1)]
                d_raw, dgn = _pair_rms_bwd(acc[:, 128 * hp:128 * (hp + 1)], raw, _pair_rstd(raw, lo), gn_ref[...], lo)
                dp_ref[:, part * W + 128 * hp:part * W + 128 * (hp + 1)] = d_raw.astype(BF16)
                dgn_ref[...] += dgn
        dp_ref[:, 3 * W:3 * W + 256] = dcq_ref[...].astype(BF16)
        dp_ref[:, 3 * W + 256:3 * W + 384] = dckv_ref[...].astype(BF16)
        dp_ref[:, 3 * W + 384:N] = dkpe_ref[...].astype(BF16)
        dh = _dot_nt(dp_ref[...], w_ref[...])
        xv = x_ref[...]
        dxn, dgc = _rms_bwd(dh, xv, g_ref[...], _rstd(xv))
        dx_ref[...] = dxu_ref[...] + dxn
        dgain_ref[...] += jnp.sum(dgc, axis=0, keepdims=True)

    row = lambda i: (i, 0)
    fix = lambda i: (0, 0)
    r_args, r_in, r_shape, r_out, r_scratch = _ride_parts(ride)
    first = lambda: pl.program_id(0) == 0
    last = lambda: pl.program_id(0) == T // tm - 1
    outs = pl.pallas_call(
        _riding(body, 11 + 3 * nb, 5, 0, ride, first, last), name="in_proj_bwd", grid=(T // tm,),
        in_specs=[pl.BlockSpec((tm, D), row), pl.BlockSpec((tm, D), row), pl.BlockSpec((1, D), fix),
                  pl.BlockSpec((D, N), fix), pl.BlockSpec((tm, W), row), pl.BlockSpec((tm, W), lambda i: (i, 1)),
                  pl.BlockSpec((1, 128), fix), pl.BlockSpec((1, 128), fix)] + [pl.BlockSpec((tm, W), row)] * (3 * nb)
                 + [pl.BlockSpec((tm, 256), row), pl.BlockSpec((tm, 128), row), pl.BlockSpec((tm, 128), row)] + r_in,
        out_specs=[pl.BlockSpec((tm, D), row), pl.BlockSpec((tm, N), row), pl.BlockSpec((1, D), fix),
                   pl.BlockSpec((1, 128), fix), pl.BlockSpec((1, 128), fix)] + r_out,
        out_shape=[jax.ShapeDtypeStruct((T, D), F32), jax.ShapeDtypeStruct((T, N), BF16),
                   jax.ShapeDtypeStruct((1, D), F32), jax.ShapeDtypeStruct((1, 128), F32),
                   jax.ShapeDtypeStruct((1, 128), F32)] + r_shape,
        scratch_shapes=r_scratch,
        compiler_params=_params(1),
    )(dx_up, x, gain, w, proj, proj, gq, gk, *[a for triple in dqkv for a in triple], dcq, dckv, dkpe, *r_args)
    return outs[:5], outs[5:]


def _out_proj(x, o_dil, o_mla, g_dil, g_mla, w, tm=512):
    T, D = x.shape
    W = o_dil.shape[1]

    def body(x_ref, od_ref, om_ref, gd_ref, gm_ref, w_ref, xo_ref, oc_ref):
        od, om = od_ref[...], om_ref[...]
        oc_ref[:, 0:W] = (od * _rstd(od) * gd_ref[...]).astype(BF16)
        oc_ref[:, W:2 * W] = (om * _rstd(om) * gm_ref[...]).astype(BF16)
        xo_ref[...] = x_ref[...] + _dot(oc_ref[...], w_ref[...])

    row = lambda i: (i, 0)
    fix = lambda i: (0, 0)
    return pl.pallas_call(
        body, name="out_proj", grid=(T // tm,),
        in_specs=[pl.BlockSpec((tm, D), row), pl.BlockSpec((tm, W), row), pl.BlockSpec((tm, W), row),
                  pl.BlockSpec((1, W), fix), pl.BlockSpec((1, W), fix), pl.BlockSpec((2 * W, D), fix)],
        out_specs=[pl.BlockSpec((tm, D), row), pl.BlockSpec((tm, 2 * W), row)],
        out_shape=[jax.ShapeDtypeStruct((T, D), F32), jax.ShapeDtypeStruct((T, 2 * W), BF16)],
        compiler_params=_params(1),
    )(x, o_dil, o_mla, g_dil, g_mla, w)


def _out_proj_bwd(dx, o_dil, o_mla, g_dil, g_mla, w, tm=512):
    T, D = dx.shape
    W = o_dil.shape[1]

    def body(dx_ref, od_ref, om_ref, gd_ref, gm_ref, w_ref, dod_ref, dom_ref, dgd_ref, dgm_ref):
        @pl.when(pl.program_id(0) == 0)
        def _():
            dgd_ref[...] = jnp.zeros_like(dgd_ref)
            dgm_ref[...] = jnp.zeros_like(dgm_ref)

        doc = _dot_nt(dx_ref[...].astype(BF16), w_ref[...])
        od, om = od_ref[...], om_ref[...]
        dod, dgd = _rms_bwd(doc[:, 0:W], od, gd_ref[...], _rstd(od))
        dom, dgm = _rms_bwd(doc[:, W:2 * W], om, gm_ref[...], _rstd(om))
        dod_ref[...] = dod
        dom_ref[...] = dom
        dgd_ref[...] += jnp.sum(dgd, axis=0, keepdims=True)
        dgm_ref[...] += jnp.sum(dgm, axis=0, keepdims=True)

    row = lambda i: (i, 0)
    fix = lambda i: (0, 0)
    return pl.pallas_call(
        body, name="out_proj_bwd", grid=(T // tm,),
        in_specs=[pl.BlockSpec((tm, D), row), pl.BlockSpec((tm, W), row), pl.BlockSpec((tm, W), row),
                  pl.BlockSpec((1, W), fix), pl.BlockSpec((1, W), fix), pl.BlockSpec((2 * W, D), fix)],
        out_specs=[pl.BlockSpec((tm, W), row), pl.BlockSpec((tm, W), row),
                   pl.BlockSpec((1, W), fix), pl.BlockSpec((1, W), fix)],
        out_shape=[jax.ShapeDtypeStruct((T, W), F32), jax.ShapeDtypeStruct((T, W), F32),
                   jax.ShapeDtypeStruct((1, W), F32), jax.ShapeDtypeStruct((1, W), F32)],
        compiler_params=_params(1),
    )(dx, o_dil, o_mla, g_dil, g_mla, w)


def _pair_rstd(x, lo):
    sq = x * x
    s0 = jnp.sum(jnp.where(lo, sq, 0.0), axis=-1, keepdims=True)
    s1 = jnp.sum(jnp.where(lo, 0.0, sq), axis=-1, keepdims=True)
    return jnp.where(lo, lax.rsqrt(s0 / DIL_HD + EPS), lax.rsqrt(s1 / DIL_HD + EPS))


def _pair_rms_bwd(dn, x, r, g, lo):
    u = dn * g
    t = u * x
    d0 = jnp.sum(jnp.where(lo, t, 0.0), axis=-1, keepdims=True)
    d1 = jnp.sum(jnp.where(lo, 0.0, t), axis=-1, keepdims=True)
    dx = r * u - x * (r * r * r) * (jnp.where(lo, d0, d1) / DIL_HD)
    return dx, jnp.sum(dn * x * r, axis=0, keepdims=True)


def _pair_col(x, lo, e):
    sel = lo if e == 0 else jnp.logical_not(lo)
    return jnp.max(jnp.where(sel, x, NEG), axis=-1, keepdims=True)


def _dil_masks(n):
    lo = lax.broadcasted_iota(jnp.int32, (DIL_BLOCK, DIL_BLOCK), 1) < DIL_HD
    row = lax.broadcasted_iota(jnp.int32, (2 * DIL_BLOCK, 2 * DIL_BLOCK), 0) % DIL_BLOCK
    col = lax.broadcasted_iota(jnp.int32, (2 * DIL_BLOCK, 2 * DIL_BLOCK), 1)
    prev = jnp.logical_and(jnp.logical_and(col < DIL_BLOCK, col >= row), n > 0)
    cur = jnp.logical_and(col >= DIL_BLOCK, col - DIL_BLOCK <= row)
    return lo, jnp.logical_or(prev, cur)


def _stack_heads(x, lo):
    return jnp.concatenate([jnp.where(lo, x, 0.0), jnp.where(lo, 0.0, x)], axis=0)


def _unstack_heads(x2, lo):
    return jnp.where(lo, x2[:DIL_BLOCK], x2[DIL_BLOCK:])


def _dil_pairs(d):
    return 4 if d == 1 else 1


def _sub_rows(r, d):
    return pl.ds(r, DIL_BLOCK, stride=d) if d > 1 else pl.ds(0, DIL_BLOCK)


def _split_subsequences(loads, d, P):
    for r in range(d):
        for p in range(P):
            for block, scratch, part in loads:
                piece = block[_sub_rows(r, d), pl.ds(128 * p, 128)]
                if part is None:
                    scratch[r * P + p] = piece
                else:
                    scratch[r * P + p, pl.ds(DIL_BLOCK * part, DIL_BLOCK), :] = piece


def _merge_subsequences(pairs, d, P):
    for r in range(d):
        for p in range(P):
            for block, scratch in pairs:
                block[_sub_rows(r, d), pl.ds(128 * p, 128)] = scratch[r * P + p]


def _dil_fwd(qh, kh, proj, bias, d, prev):
    T = proj.shape[0]
    P = _dil_pairs(d)
    rows, cw, n_it = DIL_BLOCK * d, 128 * P, d * P
    nblk = T // rows
    has_prev = prev is not None

    def body(*refs):
        q_ref, kp_ref, kc_ref, vp_ref, vc_ref, bias_ref = refs[:6]
        refs = refs[6:]
        if has_prev:
            oin_ref, lin_ref = refs[:2]
            refs = refs[2:]
        o_ref, l_ref, qs, ks, vs, os_, ls_ = refs[:7]
        pb, n = pl.program_id(0), pl.program_id(1)
        lo, valid = _dil_masks(n)
        loads = [(q_ref, qs, None), (kp_ref, ks, 0), (kc_ref, ks, 1), (vp_ref, vs, 0), (vc_ref, vs, 1)]
        if has_prev:
            ois, lis = refs[7:]
            loads += [(oin_ref, ois, None), (lin_ref, lis, None)]
        _split_subsequences(loads, d, P)

        def step(i, carry):
            q2 = _stack_heads(qs[i], lo).astype(BF16)
            s = jnp.where(valid, _dot_nt(q2, ks[i].astype(BF16)) + bias_ref[pb * P + i % P], NEG)
            m = jnp.max(s, axis=-1, keepdims=True)
            p = jnp.exp(s - m)
            l = jnp.sum(p, axis=-1, keepdims=True)
            o = _unstack_heads(_dot(p.astype(BF16), vs[i].astype(BF16)) / l, lo)
            lse = _unstack_heads(jnp.broadcast_to(m + jnp.log(l), (2 * DIL_BLOCK, 128)), lo)
            if has_prev:
                lin = lis[i]
                mx = jnp.maximum(lin, lse)
                lnew = mx + jnp.log(jnp.exp(lin - mx) + jnp.exp(lse - mx))
                o = ois[i] * jnp.exp(lin - lnew) + o * jnp.exp(lse - lnew)
                lse = lnew
            os_[i] = o
            ls_[i] = lse
            return carry

        lax.fori_loop(0, n_it, step, 0, unroll=4)
        _merge_subsequences([(o_ref, os_), (l_ref, ls_)], d, P)

    blk = (rows, cw)
    vcol = 2 * DIL_WIDTH // cw
    prev_n = lambda n: jnp.maximum(n - 1, 0)
    fix3 = lambda pb, n: (0, 0, 0)
    tok = pl.BlockSpec(blk, lambda pb, n: (n, pb))
    tok_prev = pl.BlockSpec(blk, lambda pb, n: (prev_n(n), pb))
    in_specs = [tok, tok_prev, tok,
                pl.BlockSpec(blk, lambda pb, n: (prev_n(n), vcol + pb)), pl.BlockSpec(blk, lambda pb, n: (n, vcol + pb)),
                pl.BlockSpec((DIL_HEADS // 2, 2 * DIL_BLOCK, 2 * DIL_BLOCK), fix3)]
    args = [qh, kh, kh, proj, proj, bias]
    one, two = pltpu.VMEM((n_it, DIL_BLOCK, 128), F32), pltpu.VMEM((n_it, 2 * DIL_BLOCK, 128), F32)
    scratch = [one, two, two, one, one]
    if has_prev:
        in_specs += [tok, tok]
        args += list(prev)
        scratch += [one, one]
    out = jax.ShapeDtypeStruct((T, DIL_WIDTH), F32)
    return pl.pallas_call(
        body, name=f"dil_fwd_d{d}", grid=(DIL_HEADS // 2 // P, nblk), in_specs=in_specs, out_specs=[tok, tok],
        out_shape=[out, out], scratch_shapes=scratch, compiler_params=_params(2),
    )(*args)


def _dil_bwd(qh, kh, proj, o, lse, do, bias, d):
    T = proj.shape[0]
    P = _dil_pairs(d)
    rows, cw, n_it = DIL_BLOCK * d, 128 * P, d * P
    nblk = T // rows

    def body(q_ref, kp_ref, kc_ref, vp_ref, vc_ref, o_ref, l_ref, do_ref, bias_ref,
             dq_ref, dk_ref, dv_ref, db_ref,
             qs, ks, vs, os_, ls_, dos, dqs, dks, dvs, ck, cv):
        pb, n = pl.program_id(0), pl.program_id(1)
        lo, valid = _dil_masks(n)

        @pl.when((pb == 0) & (n == 0))
        def _():
            db_ref[...] = jnp.zeros_like(db_ref)

        @pl.when(n == 0)
        def _():
            ck[...] = jnp.zeros_like(ck)
            cv[...] = jnp.zeros_like(cv)

        _split_subsequences([(q_ref, qs, None), (kp_ref, ks, 0), (kc_ref, ks, 1), (vp_ref, vs, 0), (vc_ref, vs, 1),
                             (o_ref, os_, None), (l_ref, ls_, None), (do_ref, dos, None)], d, P)

        def step(i, carry):
            pair = pb * P + i % P
            q2 = _stack_heads(qs[i], lo).astype(BF16)
            kcat, vcat = ks[i].astype(BF16), vs[i].astype(BF16)
            dov = dos[i]
            do2 = _stack_heads(dov, lo).astype(BF16)
            delta = jnp.sum(_stack_heads(dov * os_[i], lo), axis=-1, keepdims=True)
            lse_pair = ls_[i]
            lse2 = jnp.concatenate([_pair_col(lse_pair, lo, 0), _pair_col(lse_pair, lo, 1)], axis=0)
            s = jnp.where(valid, _dot_nt(q2, kcat) + bias_ref[pair], NEG)
            p = jnp.exp(s - lse2)
            ds = p * (_dot_nt(do2, vcat) - delta)
            db_ref[pair] += ds
            dsb = ds.astype(BF16)
            dqs[i] = _unstack_heads(_dot(dsb, kcat), lo)
            dk2 = _dot_tn(dsb, q2)
            dv2 = _dot_tn(p.astype(BF16), do2)
            dks[i] = ck[i] + dk2[:DIL_BLOCK]
            dvs[i] = cv[i] + dv2[:DIL_BLOCK]
            ck[i] = dk2[DIL_BLOCK:]
            cv[i] = dv2[DIL_BLOCK:]
            return carry

        @pl.when(n < nblk)
        def _():
            lax.fori_loop(0, n_it, step, 0, unroll=2)
            _merge_subsequences([(dq_ref, dqs), (dk_ref, dks), (dv_ref, dvs)], d, P)

        @pl.when(n == nblk)
        def _():
            _merge_subsequences([(dk_ref, ck), (dv_ref, cv)], d, P)

    blk = (rows, cw)
    vcol = 2 * DIL_WIDTH // cw
    qn_ = lambda n: jnp.minimum(n, nblk - 1)
    pn_ = lambda n: jnp.maximum(n - 1, 0)
    fix3 = lambda pb, n: (0, 0, 0)
    tok_q = pl.BlockSpec(blk, lambda pb, n: (qn_(n), pb))
    tok_p = pl.BlockSpec(blk, lambda pb, n: (pn_(n), pb))
    bias_spec = pl.BlockSpec((DIL_HEADS // 2, 2 * DIL_BLOCK, 2 * DIL_BLOCK), fix3)
    in_specs = [tok_q, tok_p, tok_q,
                pl.BlockSpec(blk, lambda pb, n: (pn_(n), vcol + pb)), pl.BlockSpec(blk, lambda pb, n: (qn_(n), vcol + pb)),
                tok_q, tok_q, tok_q, bias_spec]
    tok_shape = jax.ShapeDtypeStruct((T, DIL_WIDTH), F32)
    one, two = pltpu.VMEM((n_it, DIL_BLOCK, 128), F32), pltpu.VMEM((n_it, 2 * DIL_BLOCK, 128), F32)
    dq, dk, dv, db = pl.pallas_call(
        body, name=f"dil_bwd_d{d}", grid=(DIL_HEADS // 2 // P, nblk + 1), in_specs=in_specs,
        out_specs=[tok_q, tok_p, tok_p, bias_spec],
        out_shape=[tok_shape, tok_shape, tok_shape, jax.ShapeDtypeStruct(bias.shape, F32)],
        scratch_shapes=[one, two, two] + [one] * 8,
        compiler_params=_params(2),
    )(qh, kh, kh, proj, proj, o, lse, do, bias)
    return (dq, dk, dv), db


def _t5_bucket(dist):
    max_exact = REL_BUCKETS // 2
    dd = np.maximum(dist, 1).astype(np.float32)
    large = max_exact + (np.log(dd / max_exact) / np.log(REL_MAX_DIST / max_exact)
                         * (REL_BUCKETS - max_exact)).astype(np.int32)
    large = np.minimum(large, REL_BUCKETS - 1)
    return np.where(dist < max_exact, dist, large).astype(np.int32)


def _bucket_onehots():
    i = np.arange(DIL_BLOCK)[:, None]
    j = np.arange(DIL_BLOCK)[None, :]
    out = []
    for _, d in DIL_BRANCHES:
        dist = np.concatenate([DIL_BLOCK + i - j, i - j], axis=1)
        bucket = _t5_bucket(np.clip(dist, 0, None) * d).reshape(-1)
        out.append(jnp.asarray(np.eye(REL_BUCKETS, dtype=np.float32)[:, bucket], BF16))
    return out


def _bias_tables(rel_bias, onehots):
    n = len(onehots)

    def body(rb_ref, *refs):
        parts = _split3(rb_ref[...])
        for k in range(n):
            oh = refs[k][...]
            refs[n + k][...] = _dot(parts[0], oh) + _dot(parts[1], oh) + _dot(parts[2], oh)

    flat = pl.pallas_call(
        body, name="bias_tables",
        out_shape=[jax.ShapeDtypeStruct((DIL_HEADS, 2 * DIL_BLOCK * DIL_BLOCK), F32)] * n,
        compiler_params=pltpu.CompilerParams(vmem_limit_bytes=VMEM_LIMIT),
    )(rel_bias, *onehots)
    return [t.reshape(DIL_HEADS // 2, 2 * DIL_BLOCK, 2 * DIL_BLOCK) for t in flat]


def _bias_grad(dbs, onehots):
    n = len(dbs)
    dbs = [t.reshape(DIL_HEADS, 2 * DIL_BLOCK * DIL_BLOCK) for t in dbs]

    def body(*refs):
        acc = jnp.zeros((DIL_HEADS, REL_BUCKETS), F32)
        for k in range(n):
            oh = refs[n + k][...]
            for part in _split3(refs[k][...]):
                acc = acc + _dot_nt(part, oh)
        refs[-1][...] = acc

    return pl.pallas_call(
        body, name="bias_grad",
        out_shape=jax.ShapeDtypeStruct((DIL_HEADS, REL_BUCKETS), F32),
        compiler_params=pltpu.CompilerParams(vmem_limit_bytes=VMEM_LIMIT),
    )(*dbs, *onehots)


def _swap_halves(x):
    lane = lax.broadcasted_iota(jnp.int32, x.shape, 1)
    first = (lane % 64) < 32
    return jnp.where(first, pltpu.roll(x, 96, 1), pltpu.roll(x, 32, 1))


def _rope_tables(T):
    pos = jnp.arange(T, dtype=F32)
    inv_freq = ROPE_BASE ** (-jnp.arange(0, MLA_ROPE, 2, dtype=F32) / MLA_ROPE)
    ang = pos[:, None] * inv_freq[None, :]
    z = jnp.zeros((T, 128 - MLA_ROPE), F32)
    cos = jnp.concatenate([jnp.cos(ang), jnp.cos(ang), z], axis=-1)
    sin = jnp.concatenate([-jnp.sin(ang), jnp.sin(ang), z], axis=-1)
    return cos, sin


def _mla_prep(proj, cos, sin, g_qa, g_kva, g_q, g_k, wq, wkv, tm=512):
    T = proj.shape[0]
    H = MLA_HEADS
    scale = MLA_QK ** -0.5

    def body(cq_ref, ckv_ref, kpe_ref, cos_ref, sin_ref, gqa_ref, gkva_ref, gq_ref, gk_ref, wq_ref, wkv_ref,
             q_ref, k_ref, v_ref):
        cosv, sinv = cos_ref[...], sin_ref[...]

        def rope(x):
            return x * cosv + _swap_halves(x) * sinv

        cq = cq_ref[...]
        qp = _dot((cq * _rstd(cq) * gqa_ref[...]).astype(BF16), wq_ref[...])
        ckv = ckv_ref[...]
        kvp = _dot((ckv * _rstd(ckv) * gkva_ref[...]).astype(BF16), wkv_ref[...])
        kpe = kpe_ref[...]
        one_hot_lane = (lax.broadcasted_iota(jnp.int32, (tm, 128), 1) == 0).astype(BF16)
        for h in range(H):
            a = qp[:, MLA_PAD * h:MLA_PAD * (h + 1)]
            qn = a * _rstd(a, MLA_QK) * gq_ref[...]
            q_ref[h, :, 0:128] = (qn[:, 0:128] * scale).astype(BF16)
            q_ref[h, :, 128:256] = (rope(qn[:, 128:256]) * scale).astype(BF16)
            kn = kvp[:, MLA_PAD * h:MLA_PAD * h + 128]
            r = lax.rsqrt((jnp.sum(kn * kn, axis=-1, keepdims=True)
                           + jnp.sum(kpe * kpe, axis=-1, keepdims=True)) / MLA_QK + EPS)
            k_ref[h, :, 0:128] = (kn * r * gk_ref[:, 0:128]).astype(BF16)
            k_ref[h, :, 128:256] = rope(kpe * r * gk_ref[:, 128:256]).astype(BF16)
            v_ref[h, :, 0:128] = kvp[:, MLA_PAD * h + 128:MLA_PAD * (h + 1)].astype(BF16)
            v_ref[h, :, 128:256] = one_hot_lane

    fix = lambda i: (0, 0)
    return pl.pallas_call(
        body, name="mla_prep", grid=(T // tm,),
        in_specs=[pl.BlockSpec((tm, 256), lambda i: (i, 6)), pl.BlockSpec((tm, 128), lambda i: (i, 14)),
                  pl.BlockSpec((tm, 128), lambda i: (i, 15)),
                  pl.BlockSpec((tm, 128), lambda i: (i, 0)), pl.BlockSpec((tm, 128), lambda i: (i, 0)),
                  pl.BlockSpec((1, 256), fix), pl.BlockSpec((1, 128), fix),
                  pl.BlockSpec((1, 256), fix), pl.BlockSpec((1, 256), fix),
                  pl.BlockSpec((256, H * MLA_PAD), fix), pl.BlockSpec((128, H * MLA_PAD), fix)],
        out_specs=[pl.BlockSpec((H, tm, MLA_PAD), lambda i: (0, i, 0)), pl.BlockSpec((H, tm, MLA_PAD), lambda i: (0, i, 0)),
                   pl.BlockSpec((H, tm, 2 * MLA_V), lambda i: (0, i, 0))],
        out_shape=[jax.ShapeDtypeStruct((H, T, MLA_PAD), BF16), jax.ShapeDtypeStruct((H, T, MLA_PAD), BF16),
                   jax.ShapeDtypeStruct((H, T, 2 * MLA_V), BF16)],
        compiler_params=_params(1),
    )(proj, proj, proj, cos, sin, g_qa, g_kva, g_q, g_k, wq, wkv)


def _mla_prep_bwd(proj, cos, sin, g_qa, g_kva, g_q, g_k, wq, wkv, dq, dk, dv, tm=512):
    T = proj.shape[0]
    H = MLA_HEADS
    scale = MLA_QK ** -0.5

    def body(cq_ref, ckv_ref, kpe_ref, cos_ref, sin_ref, gqa_ref, gkva_ref, gq_ref, gk_ref, wq_ref, wkv_ref,
             dq_ref, dk_ref, dv_ref,
             dcq_ref, dckv_ref, dkpe_ref, cqn_ref, ckvn_ref, dqp_ref, dkvp_ref,
             dgqa_ref, dgkva_ref, dgq_ref, dgk_ref):
        @pl.when(pl.program_id(0) == 0)
        def _():
            for ref in (dgqa_ref, dgkva_ref, dgq_ref, dgk_ref):
                ref[...] = jnp.zeros_like(ref)

        cosv, sinv = cos_ref[...], sin_ref[...]

        def rope_bwd(dy):
            return dy * cosv + _swap_halves(dy * sinv)

        cq = cq_ref[...]
        rcq = _rstd(cq)
        cqn = (cq * rcq * gqa_ref[...]).astype(BF16)
        cqn_ref[...] = cqn
        qp = _dot(cqn, wq_ref[...])
        ckv = ckv_ref[...]
        rckv = _rstd(ckv)
        ckvn = (ckv * rckv * gkva_ref[...]).astype(BF16)
        ckvn_ref[...] = ckvn
        kvp = _dot(ckvn, wkv_ref[...])
        kpe = kpe_ref[...]
        dkpe = jnp.zeros_like(kpe)
        dgq = jnp.zeros((1, MLA_PAD), F32)
        dgk = jnp.zeros((1, MLA_PAD), F32)
        for h in range(H):
            a = qp[:, MLA_PAD * h:MLA_PAD * (h + 1)]
            dqh = dq_ref[h]
            dn = jnp.concatenate([dqh[:, 0:128], rope_bwd(dqh[:, 128:256])], axis=-1) * scale
            da, dg = _rms_bwd(dn, a, gq_ref[...], _rstd(a, MLA_QK), MLA_QK)
            dgq = dgq + jnp.sum(dg, axis=0, keepdims=True)
            dqp_ref[:, MLA_PAD * h:MLA_PAD * (h + 1)] = da.astype(BF16)

            ak = jnp.concatenate([kvp[:, MLA_PAD * h:MLA_PAD * h + 128], kpe], axis=-1)
            dkh = dk_ref[h]
            dnk = jnp.concatenate([dkh[:, 0:128], rope_bwd(dkh[:, 128:256])], axis=-1)
            dak, dg = _rms_bwd(dnk, ak, gk_ref[...], _rstd(ak, MLA_QK), MLA_QK)
            dgk = dgk + jnp.sum(dg, axis=0, keepdims=True)
            dkpe = dkpe + dak[:, 128:256]
            dkvp_ref[:, MLA_PAD * h:MLA_PAD * h + 128] = dak[:, 0:128].astype(BF16)
            dkvp_ref[:, MLA_PAD * h + 128:MLA_PAD * (h + 1)] = dv_ref[h].astype(BF16)
        dkpe_ref[...] = dkpe
        dgq_ref[...] += dgq
        dgk_ref[...] += dgk
        dcq, dg = _rms_bwd(_dot_nt(dqp_ref[...], wq_ref[...]), cq, gqa_ref[...], rcq)
        dcq_ref[...] = dcq
        dgqa_ref[...] += jnp.sum(dg, axis=0, keepdims=True)
        dckv, dg = _rms_bwd(_dot_nt(dkvp_ref[...], wkv_ref[...]), ckv, gkva_ref[...], rckv)
        dckv_ref[...] = dckv
        dgkva_ref[...] += jnp.sum(dg, axis=0, keepdims=True)

    fix = lambda i: (0, 0)
    row = lambda i: (i, 0)
    head = lambda i: (0, i, 0)
    return pl.pallas_call(
        body, name="mla_prep_bwd", grid=(T // tm,),
        in_specs=[pl.BlockSpec((tm, 256), lambda i: (i, 6)), pl.BlockSpec((tm, 128), lambda i: (i, 14)),
                  pl.BlockSpec((tm, 128), lambda i: (i, 15)),
                  pl.BlockSpec((tm, 128), row), pl.BlockSpec((tm, 128), row),
                  pl.BlockSpec((1, 256), fix), pl.BlockSpec((1, 128), fix),
                  pl.BlockSpec((1, 256), fix), pl.BlockSpec((1, 256), fix),
                  pl.BlockSpec((256, H * MLA_PAD), fix), pl.BlockSpec((128, H * MLA_PAD), fix),
                  pl.BlockSpec((H, tm, MLA_PAD), head), pl.BlockSpec((H, tm, MLA_PAD), head),
                  pl.BlockSpec((H, tm, MLA_V), head)],
        out_specs=[pl.BlockSpec((tm, 256), row), pl.BlockSpec((tm, 128), row), pl.BlockSpec((tm, 128), row),
                   pl.BlockSpec((tm, 256), row), pl.BlockSpec((tm, 128), row),
                   pl.BlockSpec((tm, H * MLA_PAD), row), pl.BlockSpec((tm, H * MLA_PAD), row),
                   pl.BlockSpec((1, 256), fix), pl.BlockSpec((1, 128), fix),
                   pl.BlockSpec((1, 256), fix), pl.BlockSpec((1, 256), fix)],
        out_shape=[jax.ShapeDtypeStruct((T, 256), F32), jax.ShapeDtypeStruct((T, 128), F32),
                   jax.ShapeDtypeStruct((T, 128), F32),
                   jax.ShapeDtypeStruct((T, 256), BF16), jax.ShapeDtypeStruct((T, 128), BF16),
                   jax.ShapeDtypeStruct((T, H * MLA_PAD), BF16), jax.ShapeDtypeStruct((T, H * MLA_PAD), BF16),
                   jax.ShapeDtypeStruct((1, 256), F32), jax.ShapeDtypeStruct((1, 128), F32),
                   jax.ShapeDtypeStruct((1, 256), F32), jax.ShapeDtypeStruct((1, 256), F32)],
        compiler_params=_params(1),
    )(proj, proj, proj, cos, sin, g_qa, g_kva, g_q, g_k, wq, wkv, dq, dk, dv)


def _causal_pairs(T, tq, tk, key_major):
    pairs = [(i, j) for i in range(T // tq) for j in range(T // tk) if j * tk <= i * tq + tq - 1]
    if key_major:
        pairs.sort(key=lambda p: (p[1], p[0]))
    outer = [p[1] if key_major else p[0] for p in pairs]
    first = [int(t == 0 or outer[t] != outer[t - 1]) for t in range(len(pairs))]
    last = [int(t == len(pairs) - 1 or outer[t] != outer[t + 1]) for t in range(len(pairs))]
    tab = lambda v: jnp.asarray(np.array(v, np.int32))
    return tab([p[0] for p in pairs]), tab([p[1] for p in pairs]), tab(first), tab(last)


def _causal_scores(qv, kv, qi, ki, row0, tq, tk, masked):
    s = _dot_nt(qv, kv)
    if masked:
        row = lax.broadcasted_iota(jnp.int32, s.shape, 0) + (qi * tq + row0)
        col = lax.broadcasted_iota(jnp.int32, s.shape, 1) + ki * tk
        s = jnp.where(col <= row, s, NEG)
    return s


def _mla_attn(q, k, v, ride=None, tq=1024, tk=2048, rc=256):
    H, T, _ = q.shape
    tables = _causal_pairs(T, tq, tk, key_major=False)
    n_pairs = int(tables[0].shape[0])
    r_args, r_in, r_shape, r_out, r_scratch = _ride_parts(ride)

    def body(qt, kt, ft, lt, q_ref, k_ref, v_ref, o_ref, lse_ref, m_s, acc):
        t = pl.program_id(1)
        qi, ki = qt[t], kt[t]

        @pl.when(ft[t] == 1)
        def _():
            m_s[...] = jnp.full_like(m_s, NEG)
            acc[...] = jnp.zeros_like(acc)

        def update(masked):
            kk, vv = k_ref[...], v_ref[...]
            for c in range(tq // rc):
                rows = pl.ds(c * rc, rc)
                s = _causal_scores(q_ref[rows, :], kk, qi, ki, c * rc, tq, tk, masked)
                m_old = m_s[rows, :]
                m_new = jnp.maximum(m_old, jnp.max(s, axis=-1, keepdims=True))
                p = jnp.exp(s - m_new).astype(BF16)
                acc[rows, :] = jnp.exp(m_old - m_new) * acc[rows, :] + _dot(p, vv)
                m_s[rows, :] = m_new

        diagonal = (ki + 1) * tk - 1 > qi * tq

        @pl.when(diagonal)
        def _():
            update(True)

        @pl.when(jnp.logical_not(diagonal))
        def _():
            update(False)

        @pl.when(lt[t] == 1)
        def _():
            l = jnp.max(acc[:, MLA_V:], axis=-1, keepdims=True)
            o_ref[...] = acc[:, :MLA_V] / l
            lse_ref[...] = jnp.broadcast_to(m_s[...] + jnp.log(l), lse_ref.shape)

    qrow = lambda h, t, qt, kt, ft, lt: (h, qt[t], 0)
    krow = lambda h, t, qt, kt, ft, lt: (h, kt[t], 0)
    first = lambda: (pl.program_id(0) == 0) & (pl.program_id(1) == 0)
    last = lambda: (pl.program_id(0) == H - 1) & (pl.program_id(1) == n_pairs - 1)
    outs = pl.pallas_call(
        _riding(body, 7, 2, 2, ride, first, last), name="mla_attn",
        grid_spec=pltpu.PrefetchScalarGridSpec(
            num_scalar_prefetch=4, grid=(H, n_pairs),
            in_specs=[pl.BlockSpec((None, tq, MLA_PAD), qrow), pl.BlockSpec((None, tk, MLA_PAD), krow),
                      pl.BlockSpec((None, tk, 2 * MLA_V), krow)] + r_in,
            out_specs=[pl.BlockSpec((tq, MLA_V), lambda h, t, qt, kt, ft, lt: (qt[t], h)),
                       pl.BlockSpec((None, tq, 128), qrow)] + r_out,
            scratch_shapes=[pltpu.VMEM((tq, 1), F32), pltpu.VMEM((tq, 2 * MLA_V), F32)] + r_scratch),
        out_shape=[jax.ShapeDtypeStruct((T, H * MLA_V), F32), jax.ShapeDtypeStruct((H, T, 128), F32)] + r_shape,
        compiler_params=_params(2),
    )(*tables, q, k, v, *r_args)
    return outs[:2], outs[2:]


def _mla_attn_bwd(q, k, v, o, lse, do, ride=None, tq=1024, tk=1024, rc=512):
    H, T, _ = q.shape
    tables = _causal_pairs(T, tq, tk, key_major=True)
    n_pairs = int(tables[0].shape[0])
    r_args, r_in, r_shape, r_out, r_scratch = _ride_parts(ride)

    def body(qt, kt, ft, lt, q_ref, k_ref, v_ref, o_ref, lse_ref, do_ref, dq_ref, dk_ref, dv_ref, dk_s, dv_s):
        t = pl.program_id(1)
        qi, ki = qt[t], kt[t]

        @pl.when(t == 0)
        def _():
            dq_ref[...] = jnp.zeros_like(dq_ref)

        @pl.when(ft[t] == 1)
        def _():
            dk_s[...] = jnp.zeros_like(dk_s)
            dv_s[...] = jnp.zeros_like(dv_s)

        def update(masked):
            kk, vv = k_ref[...], v_ref[...]
            for c in range(tq // rc):
                rows = pl.ds(c * rc, rc)
                qv, dov = q_ref[rows, :], do_ref[rows, :]
                delta = jnp.sum(dov * o_ref[rows, :], axis=-1, keepdims=True)
                lse_v = jnp.max(lse_ref[rows, :], axis=-1, keepdims=True)
                p = jnp.exp(_causal_scores(qv, kk, qi, ki, c * rc, tq, tk, masked) - lse_v)
                dob = dov.astype(BF16)
                dv_s[...] += _dot_tn(p.astype(BF16), dob)
                ds = (p * (_dot_nt(dob, vv) - delta)).astype(BF16)
                dk_s[...] += _dot_tn(ds, qv)
                out_rows = pl.ds(pl.multiple_of(qi * tq + c * rc, rc), rc)
                dq_ref[out_rows, :] += _dot(ds, kk)

        diagonal = (ki + 1) * tk - 1 > qi * tq

        @pl.when(diagonal)
        def _():
            update(True)

        @pl.when(jnp.logical_not(diagonal))
        def _():
            update(False)

        @pl.when(lt[t] == 1)
        def _():
            dk_ref[...] = dk_s[...]
            dv_ref[...] = dv_s[...]

    qrow = lambda h, t, qt, kt, ft, lt: (h, qt[t], 0)
    krow = lambda h, t, qt, kt, ft, lt: (h, kt[t], 0)
    qcol = lambda h, t, qt, kt, ft, lt: (qt[t], h)
    first = lambda: (pl.program_id(0) == 0) & (pl.program_id(1) == 0)
    last = lambda: (pl.program_id(0) == H - 1) & (pl.program_id(1) == n_pairs - 1)
    outs = pl.pallas_call(
        _riding(body, 10, 3, 2, ride, first, last), name="mla_attn_bwd",
        grid_spec=pltpu.PrefetchScalarGridSpec(
            num_scalar_prefetch=4, grid=(H, n_pairs),
            in_specs=[pl.BlockSpec((None, tq, MLA_PAD), qrow), pl.BlockSpec((None, tk, MLA_PAD), krow),
                      pl.BlockSpec((None, tk, MLA_V), krow), pl.BlockSpec((tq, MLA_V), qcol),
                      pl.BlockSpec((None, tq, 128), qrow), pl.BlockSpec((tq, MLA_V), qcol)] + r_in,
            out_specs=[pl.BlockSpec((None, T, MLA_PAD), lambda h, t, qt, kt, ft, lt: (h, 0, 0)),
                       pl.BlockSpec((None, tk, MLA_PAD), krow), pl.BlockSpec((None, tk, MLA_V), krow)] + r_out,
            scratch_shapes=[pltpu.VMEM((tk, MLA_PAD), F32), pltpu.VMEM((tk, MLA_V), F32)] + r_scratch),
        out_shape=[jax.ShapeDtypeStruct((H, T, MLA_PAD), F32), jax.ShapeDtypeStruct((H, T, MLA_PAD), F32),
                   jax.ShapeDtypeStruct((H, T, MLA_V), F32)] + r_shape,
        compiler_params=_params(2),
    )(*tables, q, k, v, o, lse, do, *r_args)
    return outs[:3], outs[3:]


def _pair_gain(g):
    return jnp.tile(g.reshape(1, DIL_HD), (1, 2))


def _pad_gain(g):
    return jnp.pad(g.reshape(1, MLA_QK), ((0, 0), (0, MLA_PAD - MLA_QK)))


def _local_step(x, target, s, comm):
    T = x.shape[0]
    w = comm.w
    gq, gk = _pair_gain(s["dil_q_norm"]) * DIL_HD ** -0.5, _pair_gain(s["dil_k_norm"])
    g_q, g_k = _pad_gain(s["mla_q_norm"]), _pad_gain(s["mla_k_norm"])
    cos, sin = _rope_tables(T)
    onehots = _bucket_onehots()
    biases = _bias_tables(s["rel_bias"], onehots)

    (x1, h1, gate1, up1), got = _ffn_fwd(x, s["ffn1_norm"], w["ffn1_w_gate"], w["ffn1_w_up"], w["ffn1_w_down"],
                                         ride=comm.gather(_GROUPS["attn"]))
    comm.weights_landed(_GROUPS["attn"], got)
    hm, proj, qh, kh = _in_proj(x1, s["mix_norm"], w["w_in"], gq, gk)
    dil = None
    for (_, d), bias in zip(DIL_BRANCHES, biases):
        dil = _dil_fwd(qh, kh, proj, bias, d, dil)
    o_dil, lse_dil = dil
    q, k, v = _mla_prep(proj, cos, sin, s["mla_q_a_norm"], s["mla_kv_a_norm"], g_q, g_k, w["mla_w_q_b"], w["mla_w_kv_b"])
    (o_mla, lse_mla), got = _mla_attn(q, k, v, ride=comm.gather(_GROUPS["ffn2"]))
    comm.weights_landed(_GROUPS["ffn2"], got)
    x2, oc = _out_proj(x1, o_dil, o_mla, s["out_norm_dil"], s["out_norm_mla"], w["w_out"])
    (dy, h2, gate2, up2, loss), _ = _ffn_fwd(x2, s["ffn2_norm"], w["ffn2_w_gate"], w["ffn2_w_up"], w["ffn2_w_down"],
                                             target=target)

    gw, gs = {}, {}

    def ffn_grads(name, dy_in, x_in, h, gate, up, early=None):
        (dx, a, dg, du, dyh, dgain), _ = _ffn_bwd(dy_in, x_in, s[name + "_norm"], gate, up,
                                                  w[name + "_w_gate"], w[name + "_w_up"], w[name + "_w_down"])
        gs[name + "_norm"] = dgain
        down, gate_n, up_n = (name + "_w_down",), (name + "_w_gate",), (name + "_w_up",)
        ride = lambda names: comm.scatter(names, gw) if early is not None else None
        gw[down[0]], landed = _matmul_tn(a, dyh, 1408, 1024, ride=ride(early))
        comm.grads_landed(early or (), landed)
        gw[gate_n[0]], landed = _matmul_tn(h, dg, 1024, 1408, ride=ride(down))
        comm.grads_landed(down, landed)
        gw[up_n[0]], landed = _matmul_tn(h, du, 1024, 1408, ride=ride(gate_n))
        comm.grads_landed(gate_n, landed)
        return dx

    dx2 = ffn_grads("ffn2", dy, x2, h2, gate2, up2)
    gw["w_out"], _ = _matmul_tn(oc, dx2, 1024, 1024)
    do_dil, do_mla, gs["out_norm_dil"], gs["out_norm_mla"] = _out_proj_bwd(
        dx2, o_dil, o_mla, s["out_norm_dil"], s["out_norm_mla"], w["w_out"])

    (dq, dk, dv), got = _mla_attn_bwd(q, k, v, o_mla, lse_mla, do_mla, ride=comm.scatter(_GROUPS["ffn2"], gw))
    comm.grads_landed(_GROUPS["ffn2"], got)
    (dcq, dckv, dkpe, cqn, ckvn, dqp, dkvp, gs["mla_q_a_norm"], gs["mla_kv_a_norm"], dg_q, dg_k) = _mla_prep_bwd(
        proj, cos, sin, s["mla_q_a_norm"], s["mla_kv_a_norm"], g_q, g_k, w["mla_w_q_b"], w["mla_w_kv_b"], dq, dk, dv)
    gs["mla_q_norm"], gs["mla_k_norm"] = dg_q[:, :MLA_QK], dg_k[:, :MLA_QK]
    gw["mla_w_q_b"], _ = _matmul_tn(cqn, dqp, 256, 1024)
    gw["mla_w_kv_b"], _ = _matmul_tn(ckvn, dkvp, 128, 1024)

    dqkv, dbs = [], []
    for (_, d), bias in zip(DIL_BRANCHES, biases):
        triple, db = _dil_bwd(qh, kh, proj, o_dil, lse_dil, do_dil, bias, d)
        dqkv.append(triple)
        dbs.append(db)
    gs["rel_bias"] = _bias_grad(dbs, onehots)

    ready = tuple(n for n in _GROUPS["attn"] if n != "w_in")
    (dx1, dproj, gs["mix_norm"], dgq, dgk), got = _in_proj_bwd(dx2, x1, s["mix_norm"], w["w_in"], proj, gq, gk,
                                                               dqkv, dcq, dckv, dkpe, ride=comm.scatter(ready, gw))
    comm.grads_landed(ready, got)
    gs["dil_q_norm"] = (dgq[:, :DIL_HD] + dgq[:, DIL_HD:]) * DIL_HD ** -0.5
    gs["dil_k_norm"] = dgk[:, :DIL_HD] + dgk[:, DIL_HD:]
    gw["w_in"], _ = _matmul_tn(hm, dproj, 1024, 1024)
    grad_x = ffn_grads("ffn1", dx1, x, h1, gate1, up1, early=("w_in",))
    return loss, grad_x, gw, gs


def _position():
    x, y, c = lax.axis_index("x"), lax.axis_index("y"), lax.axis_index("c")
    return x, y, c, 4 * x + 2 * y + c


def _peer(x, y, c, k):
    px = 1 - x if k & 4 else x
    py = 1 - y if k & 2 else y
    pc = 1 - c if k & 1 else c
    return (px, py, pc), 4 * px + 2 * py + pc


class _Ride:
    def __init__(self, arrays, scatter):
        self.arrays, self.scatter = list(arrays), list(scatter)
        self.n = n = len(self.arrays)
        self.specs = [pl.BlockSpec(memory_space=pl.ANY)] * n
        self.out_shape = [jax.ShapeDtypeStruct(a.shape if sc else (N_DEV,) + a.shape, a.dtype)
                          for a, sc in zip(self.arrays, self.scatter)]
        self.scratch = [pltpu.SemaphoreType.DMA((n, N_DEV - 1)), pltpu.SemaphoreType.DMA((n, N_DEV - 1)),
                        pltpu.SemaphoreType.DMA((n,))]

    def _copies(self, ins, outs, sems):
        send_sems, recv_sems, local_sems = sems
        x, y, c, me = _position()
        copies = []
        for a in range(self.n):
            src = ins[a].at[me] if self.scatter[a] else ins[a]
            copies.append(pltpu.make_async_copy(src, outs[a].at[me], local_sems.at[a]))
        for k in range(1, N_DEV):
            peer, peer_idx = _peer(x, y, c, k)
            for a in range(self.n):
                src = ins[a].at[peer_idx] if self.scatter[a] else ins[a]
                copies.append(pltpu.make_async_remote_copy(
                    src_ref=src, dst_ref=outs[a].at[me], send_sem=send_sems.at[a, k - 1], recv_sem=recv_sems.at[a, k - 1],
                    device_id=peer, device_id_type=pl.DeviceIdType.MESH))
        return copies

    def start(self, ins, outs, sems):
        for cp in self._copies(ins, outs, sems):
            cp.start()

    def wait(self, ins, outs, sems):
        for cp in self._copies(ins, outs, sems):
            cp.wait()


def _ride_parts(ride):
    if ride is None:
        return [], [], [], [], []
    return ride.arrays, ride.specs, ride.out_shape, ride.specs, ride.scratch


def _riding(body, n_in, n_out, n_scratch, ride, first, last):
    if ride is None:
        return body
    n = ride.n
    i1, i2 = n_in + n, n_in + n + n_out
    i3, i4 = i2 + n, i2 + n + n_scratch

    def wrapped(*refs):
        ins, outs, sems = refs[n_in:i1], refs[i2:i3], refs[i4:]

        @pl.when(first())
        def _():
            ride.start(ins, outs, sems)

        body(*refs[:n_in], *refs[i1:i2], *refs[i3:i4])

        @pl.when(last())
        def _():
            ride.wait(ins, outs, sems)

    return wrapped


def _gather_two_level(arrays, name):
    n = len(arrays)
    out_shape = [jax.ShapeDtypeStruct((N_DEV,) + a.shape, a.dtype) for a in arrays]

    def body(*refs):
        ins, outs = refs[:n], refs[n:2 * n]
        send_sems, recv_sems, local_sems = refs[2 * n:]
        x, y, c, me = _position()
        sibling = (x, y, 1 - c)
        chips = [(1 - x, y), (x, 1 - y), (1 - x, 1 - y)]
        block = lambda px, py, pc: 4 * px + 2 * py + pc

        def copy(a, k, blk, to, src=None):
            dst = outs[a].at[blk]
            return pltpu.make_async_remote_copy(
                src_ref=dst if src is None else src, dst_ref=dst, send_sem=send_sems.at[a, k], recv_sem=recv_sems.at[a, k],
                device_id=to, device_id_type=pl.DeviceIdType.MESH)

        local = [pltpu.make_async_copy(ins[a], outs[a].at[me], local_sems.at[a]) for a in range(n)]
        first = []
        for a in range(n):
            first.append(copy(a, 0, me, sibling, src=ins[a]))
            first += [copy(a, 1 + j, me, (*chip, c), src=ins[a]) for j, chip in enumerate(chips)]
        for cp in local + first:
            cp.start()
        passed = []
        for j, chip in enumerate(chips):
            for a in range(n):
                copy(a, 1 + j, block(*chip, c), sibling).wait_recv()
                passed.append(copy(a, 4 + j, block(*chip, c), sibling))
                passed[-1].start()
        for a in range(n):
            copy(a, 0, block(x, y, 1 - c), sibling).wait_recv()
            for j, chip in enumerate(chips):
                copy(a, 4 + j, block(*chip, 1 - c), sibling).wait_recv()
        for cp in first + passed:
            cp.wait_send()
        for cp in local:
            cp.wait()

    any_spec = [pl.BlockSpec(memory_space=pl.ANY)] * n
    return pl.pallas_call(
        body, name=name, in_specs=any_spec, out_specs=any_spec, out_shape=out_shape,
        scratch_shapes=[pltpu.SemaphoreType.DMA((n, N_DEV - 1)), pltpu.SemaphoreType.DMA((n, N_DEV - 1)),
                        pltpu.SemaphoreType.DMA((n,))],
    )(*arrays)


def _exchange(ride, name):
    def body(*refs):
        parts = refs[:ride.n], refs[ride.n:2 * ride.n], refs[2 * ride.n:]
        ride.start(*parts)
        ride.wait(*parts)

    return pl.pallas_call(body, name=name, in_specs=ride.specs, out_specs=ride.specs, out_shape=ride.out_shape,
                          scratch_shapes=ride.scratch)(*ride.arrays)


def _adamw_math(wv, g, m, v):
    m = ADAM_B1 * m + (1.0 - ADAM_B1) * g
    v = ADAM_B2 * v + (1.0 - ADAM_B2) * (g * g)
    m_hat = m / (1.0 - ADAM_B1 ** ADAM_STEP)
    v_hat = v / (1.0 - ADAM_B2 ** ADAM_STEP)
    delta = -ADAM_LR * (m_hat / (jnp.sqrt(v_hat) + ADAM_EPS) + ADAM_WD * wv)
    return delta, m, v


def _adamw(parts, wv, m, v):
    _, R, C = wv.shape
    tr = max(t for t in range(16, 257, 16) if R % t == 0)

    def body(p_ref, w_ref, m_ref, v_ref, g_ref, d_ref, mo_ref, vo_ref):
        g = p_ref[0].astype(F32)
        for j in range(1, N_DEV):
            g = g + p_ref[j].astype(F32)
        d, mn, vn = _adamw_math(w_ref[0], g, m_ref[0], v_ref[0])
        g_ref[0] = g
        d_ref[0] = d
        mo_ref[0] = mn
        vo_ref[0] = vn

    blk = pl.BlockSpec((1, tr, C), lambda i: (0, i, 0))
    out = jax.ShapeDtypeStruct((1, R, C), F32)
    return pl.pallas_call(
        body, name="adamw", grid=(R // tr,),
        in_specs=[pl.BlockSpec((N_DEV, tr, C), lambda i: (0, i, 0)), blk, blk, blk],
        out_specs=[blk] * 4, out_shape=[out] * 4,
        compiler_params=_params(1),
    )(parts, wv, m, v)


_ROW_SHARDED = ("ffn1_w_down", "ffn2_w_down", "w_out")
_GROUPS = {"ffn1": ("ffn1_w_gate", "ffn1_w_up", "ffn1_w_down"),
           "ffn2": ("ffn2_w_gate", "ffn2_w_up", "ffn2_w_down"),
           "attn": ("w_in", "mla_w_q_b", "mla_w_kv_b", "w_out")}
_SMALL = ("ffn1_norm", "mix_norm", "ffn2_norm", "out_norm_dil", "out_norm_mla", "mla_q_a_norm", "rel_bias",
          "mla_q_norm", "mla_k_norm", "mla_kv_a_norm", "dil_q_norm", "dil_k_norm")
_SMALL_ROWS = 48


def _cols_to_full(g):
    return g.transpose(1, 0, 2).reshape(g.shape[1], N_DEV * g.shape[2])


def _full_to_cols(f):
    return f.reshape(f.shape[0], N_DEV, f.shape[1] // N_DEV).transpose(1, 0, 2)


def _to_full(name, g):
    if name in _ROW_SHARDED:
        return g.reshape(-1, g.shape[-1])
    f = _cols_to_full(g)
    if name == "w_in":
        f = jnp.pad(f, ((0, 0), (0, PROJ_PAD - PROJ_COLS)))
    if name == "mla_w_q_b":
        f = jnp.pad(f.reshape(-1, MLA_HEADS, MLA_QK), ((0, 0), (0, 0), (0, MLA_PAD - MLA_QK)))
        f = f.reshape(-1, MLA_HEADS * MLA_PAD)
    return f


def _to_parts(name, f):
    if name in _ROW_SHARDED:
        return f.reshape(N_DEV, -1, f.shape[-1]).astype(BF16)
    if name == "w_in":
        f = f[:, :PROJ_COLS]
    if name == "mla_w_q_b":
        f = f.reshape(-1, MLA_HEADS, MLA_PAD)[:, :, :MLA_QK].reshape(-1, MLA_HEADS * MLA_QK)
    return _full_to_cols(f).astype(BF16)


class _Comm:
    def __init__(self, shards):
        self.shards, self.w, self.recv = shards, {}, {}

    def gather(self, names):
        return _Ride([self.shards[n] for n in names], [False] * len(names))

    def scatter(self, names, grads):
        return _Ride([_to_parts(n, grads[n]) for n in names], [True] * len(names))

    def weights_landed(self, names, got):
        self.w.update({n: _to_full(n, g) for n, g in zip(names, got)})

    def grads_landed(self, names, got):
        self.recv.update(zip(names, got))


def _pack_small(parts, extra):
    flat = jnp.concatenate([parts[n].reshape(-1) for n in _SMALL] + [extra.reshape(-1)])
    return jnp.pad(flat, (0, _SMALL_ROWS * 128 - flat.shape[0])).reshape(_SMALL_ROWS, 128)


def _unpack_small(packed, shapes):
    flat, out, off = packed.reshape(-1), {}, 0
    for n in _SMALL:
        size = math.prod(shapes[n])
        out[n] = flat[off:off + size].reshape(shapes[n])
        off += size
    return out, flat[off]


_NAMES = ("ffn1_norm", "ffn1_w_gate", "ffn1_w_up", "ffn1_w_down", "mix_norm", "w_in", "dil_q_norm", "dil_k_norm",
          "rel_bias", "mla_q_a_norm", "mla_w_q_b", "mla_kv_a_norm", "mla_w_kv_b", "mla_q_norm", "mla_k_norm",
          "out_norm_dil", "out_norm_mla", "w_out", "ffn2_norm", "ffn2_w_gate", "ffn2_w_up", "ffn2_w_down")


def kernel(x, ffn1_norm, ffn1_w_gate, ffn1_w_up, ffn1_w_down, mix_norm, w_in, dil_q_norm, dil_k_norm, rel_bias, mla_q_a_norm, mla_w_q_b, mla_kv_a_norm, mla_w_kv_b, mla_q_norm, mla_k_norm, out_norm_dil, out_norm_mla, w_out, ffn2_norm, ffn2_w_gate, ffn2_w_up, ffn2_w_down, loss_target, m_ffn1_norm, m_ffn1_w_gate, m_ffn1_w_up, m_ffn1_w_down, m_mix_norm, m_w_in, m_dil_q_norm, m_dil_k_norm, m_rel_bias, m_mla_q_a_norm, m_mla_w_q_b, m_mla_kv_a_norm, m_mla_w_kv_b, m_mla_q_norm, m_mla_k_norm, m_out_norm_dil, m_out_norm_mla, m_w_out, m_ffn2_norm, m_ffn2_w_gate, m_ffn2_w_up, m_ffn2_w_down, v_ffn1_norm, v_ffn1_w_gate, v_ffn1_w_up, v_ffn1_w_down, v_mix_norm, v_w_in, v_dil_q_norm, v_dil_k_norm, v_rel_bias, v_mla_q_a_norm, v_mla_w_q_b, v_mla_kv_a_norm, v_mla_w_kv_b, v_mla_q_norm, v_mla_k_norm, v_out_norm_dil, v_out_norm_mla, v_w_out, v_ffn2_norm, v_ffn2_w_gate, v_ffn2_w_up, v_ffn2_w_down):
    args = locals()
    wts = {n: args[n] for n in _NAMES}
    mom = {n: args["m_" + n] for n in _NAMES}
    var = {n: args["v_" + n] for n in _NAMES}

    matrices = [n for group in _GROUPS.values() for n in group]
    comm = _Comm({n: wts[n][0].astype(BF16) for n in matrices})
    comm.weights_landed(_GROUPS["ffn1"], _gather_two_level(comm.gather(_GROUPS["ffn1"]).arrays, "gather_first"))
    small = {n: wts[n].reshape(1, -1) if n != "rel_bias" else wts[n] for n in _SMALL}

    loss, grad_x, gw, gs = _local_step(x[0], loss_target[0], small, comm)

    last = comm.scatter(("ffn1_w_up",), gw)
    got = _exchange(_Ride(last.arrays + [_pack_small(gs, loss[0, 0])], last.scatter + [False]), "scatter_last")
    comm.grads_landed(("ffn1_w_up",), got[:-1])

    res = {n: _adamw(comm.recv[n], wts[n], mom[n], var[n]) for n in matrices}
    shapes = {n: wts[n].shape for n in _SMALL}
    zero = jnp.zeros((), F32)
    packed = _adamw(got[-1], _pack_small(wts, zero)[None], _pack_small(mom, zero)[None], _pack_small(var, zero)[None])
    loss_total = None
    for slot, q in enumerate(packed):
        vals, extra = _unpack_small(q, shapes)
        if slot == 0:
            loss_total = extra
        for n in _SMALL:
            res.setdefault(n, [None] * 4)[slot] = vals[n]
    outs = [loss_total, grad_x[None]]
    for slot in range(4):
        outs += [res[n][slot].reshape(wts[n].shape) for n in _NAMES]
    return tuple(outs)
```

```python
import math

import numpy as np
import jax
import jax.numpy as jnp
from jax import lax
from jax.experimental import pallas as pl
from jax.experimental.pallas import tpu as pltpu

F32, BF16 = jnp.float32, jnp.bfloat16
EPS = 1e-6
NEG = -1e30
N_DEV = 8

DIL_HEADS, DIL_HD = 8, 64
DIL_WIDTH = DIL_HEADS * DIL_HD
DIL_BRANCHES = ((128, 1), (512, 4), (2048, 16))
DIL_BLOCK = 128
MLA_HEADS, MLA_NOPE, MLA_ROPE, MLA_V = 4, 128, 64, 128
MLA_QK = MLA_NOPE + MLA_ROPE
MLA_PAD = 256
ROPE_BASE = 10000.0
REL_BUCKETS, REL_MAX_DIST = 32, 2048
PROJ_COLS, PROJ_PAD = 1984, 2048
FFN_RESID = 0.5
ADAM_LR, ADAM_B1, ADAM_B2, ADAM_EPS, ADAM_WD, ADAM_STEP = 0.001, 0.9, 0.999, 1e-08, 0.01, 10
VMEM_LIMIT = 62 * 1024 * 1024

_NT = (((1,), (1,)), ((), ()))
_TN = (((0,), (0,)), ((), ()))


def _dot(a, b):
    return jnp.dot(a, b, preferred_element_type=F32)


def _dot_nt(a, b):
    return lax.dot_general(a, b, _NT, preferred_element_type=F32)


def _dot_tn(a, b):
    return lax.dot_general(a, b, _TN, preferred_element_type=F32)


def _params(n_axes):
    return pltpu.CompilerParams(dimension_semantics=("arbitrary",) * n_axes, vmem_limit_bytes=VMEM_LIMIT)


def _rstd(x, n=None):
    n = x.shape[-1] if n is None else n
    return lax.rsqrt(jnp.sum(x * x, axis=-1, keepdims=True) / n + EPS)


def _rms_bwd(dy, x, g, r, n=None):
    n = x.shape[-1] if n is None else n
    u = dy * g
    dx = r * u - x * (r * r * r) * (jnp.sum(u * x, axis=-1, keepdims=True) / n)
    return dx, dy * x * r


def _sigmoid(x):
    return 1.0 / (1.0 + jnp.exp(-x))


def _split3(x):
    parts = []
    for _ in range(3):
        xb = x.astype(BF16)
        parts.append(xb)
        x = x - xb.astype(F32)
    return parts


def _ffn_fwd(x, gain, wg, wu, wd, ride=None, target=None, tm=512, tf=2816):
    T, D = x.shape
    F = wg.shape[0]
    ni, nj = T // tm, F // tf
    with_loss = target is not None
    r_args, r_in, r_shape, r_out, r_scratch = _ride_parts(ride)

    def body(*refs):
        x_ref, g_ref, wg_ref, wu_ref, wd_ref = refs[:5]
        t_ref = refs[5] if with_loss else None
        xo_ref, h_ref, gate_ref, up_ref = refs[5 + with_loss:9 + with_loss]
        loss_ref = refs[-2] if with_loss else None
        acc = refs[-1]
        i, j = pl.program_id(0), pl.program_id(1)

        @pl.when(j == 0)
        def _():
            xv = x_ref[...]
            h_ref[...] = (xv * _rstd(xv) * g_ref[...]).astype(BF16)
            acc[...] = jnp.zeros_like(acc)

        h = h_ref[...]
        g = _dot_nt(h, wg_ref[...])
        u = _dot_nt(h, wu_ref[...])
        gate_ref[...] = g.astype(BF16)
        up_ref[...] = u.astype(BF16)
        a = (g * _sigmoid(g) * u).astype(BF16)
        acc[...] += _dot(a, wd_ref[...])

        @pl.when(j == nj - 1)
        def _():
            y = x_ref[...] + FFN_RESID * acc[...]
            if with_loss:
                @pl.when(i == 0)
                def _():
                    loss_ref[...] = jnp.zeros_like(loss_ref)

                e = y - t_ref[...]
                xo_ref[...] = e * (1.0 / D)
                loss_ref[...] += (0.5 / D) * jnp.sum(e * e)
            else:
                xo_ref[...] = y

    row = lambda i, j: (i, 0)
    tile = lambda i, j: (i, j)
    n_in, n_out = 5 + with_loss, 4 + with_loss
    first = lambda: (pl.program_id(0) == 0) & (pl.program_id(1) == 0)
    last = lambda: (pl.program_id(0) == ni - 1) & (pl.program_id(1) == nj - 1)
    outs = pl.pallas_call(
        _riding(body, n_in, n_out, 1, ride, first, last), name="ffn_fwd", grid=(ni, nj),
        in_specs=[pl.BlockSpec((tm, D), row), pl.BlockSpec((1, D), lambda i, j: (0, 0)),
                  pl.BlockSpec((tf, D), lambda i, j: (j, 0)), pl.BlockSpec((tf, D), lambda i, j: (j, 0)),
                  pl.BlockSpec((tf, D), lambda i, j: (j, 0))] + [pl.BlockSpec((tm, D), row)] * with_loss + r_in,
        out_specs=[pl.BlockSpec((tm, D), row), pl.BlockSpec((tm, D), row), pl.BlockSpec((tm, tf), tile),
                   pl.BlockSpec((tm, tf), tile)] + [pl.BlockSpec((1, 128), lambda i, j: (0, 0))] * with_loss + r_out,
        out_shape=[jax.ShapeDtypeStruct((T, D), F32), jax.ShapeDtypeStruct((T, D), BF16),
                   jax.ShapeDtypeStruct((T, F), BF16), jax.ShapeDtypeStruct((T, F), BF16)]
        + [jax.ShapeDtypeStruct((1, 128), F32)] * with_loss + r_shape,
        scratch_shapes=[pltpu.VMEM((tm, D), F32)] + r_scratch,
        compiler_params=_params(2),
    )(x, gain, wg, wu, wd, *([target] if with_loss else []), *r_args)
    return outs[:n_out], outs[n_out:]


def _ffn_bwd(dy, x, gain, gate, up, wg, wu, wd, ride=None, tm=256, tf=2816):
    T, D = x.shape
    F = wg.shape[0]
    ni, nj = T // tm, F // tf
    r_args, r_in, r_shape, r_out, r_scratch = _ride_parts(ride)

    def body(dy_ref, x_ref, g_ref, gate_ref, up_ref, wg_ref, wu_ref, wd_ref,
             dx_ref, a_ref, dg_ref, du_ref, dyh_ref, dgain_ref, acc):
        i, j = pl.program_id(0), pl.program_id(1)

        @pl.when((i == 0) & (j == 0))
        def _():
            dgain_ref[...] = jnp.zeros_like(dgain_ref)

        @pl.when(j == 0)
        def _():
            dyh_ref[...] = (FFN_RESID * dy_ref[...]).astype(BF16)
            acc[...] = jnp.zeros_like(acc)

        da = _dot_nt(dyh_ref[...], wd_ref[...])
        g = gate_ref[...].astype(F32)
        u = up_ref[...].astype(F32)
        sig = _sigmoid(g)
        s = g * sig
        a_ref[...] = (s * u).astype(BF16)
        dg = (da * u * (sig * (1.0 + g * (1.0 - sig)))).astype(BF16)
        du = (da * s).astype(BF16)
        dg_ref[...] = dg
        du_ref[...] = du
        acc[...] += _dot(dg, wg_ref[...]) + _dot(du, wu_ref[...])

        @pl.when(j == nj - 1)
        def _():
            xv = x_ref[...]
            dxn, dgc = _rms_bwd(acc[...], xv, g_ref[...], _rstd(xv))
            dx_ref[...] = dy_ref[...] + dxn
            dgain_ref[...] += jnp.sum(dgc, axis=0, keepdims=True)

    first = lambda: (pl.program_id(0) == 0) & (pl.program_id(1) == 0)
    last = lambda: (pl.program_id(0) == ni - 1) & (pl.program_id(1) == nj - 1)
    outs = pl.pallas_call(
        _riding(body, 8, 6, 1, ride, first, last), name="ffn_bwd", grid=(ni, nj),
        in_specs=[pl.BlockSpec((tm, D), lambda i, j: (i, 0)), pl.BlockSpec((tm, D), lambda i, j: (i, 0)),
                  pl.BlockSpec((1, D), lambda i, j: (0, 0)),
                  pl.BlockSpec((tm, tf), lambda i, j: (i, j)), pl.BlockSpec((tm, tf), lambda i, j: (i, j)),
                  pl.BlockSpec((tf, D), lambda i, j: (j, 0)), pl.BlockSpec((tf, D), lambda i, j: (j, 0)),
                  pl.BlockSpec((tf, D), lambda i, j: (j, 0))] + r_in,
        out_specs=[pl.BlockSpec((tm, D), lambda i, j: (i, 0)),
                   pl.BlockSpec((tm, tf), lambda i, j: (i, j)), pl.BlockSpec((tm, tf), lambda i, j: (i, j)),
                   pl.BlockSpec((tm, tf), lambda i, j: (i, j)),
                   pl.BlockSpec((tm, D), lambda i, j: (i, 0)), pl.BlockSpec((1, D), lambda i, j: (0, 0))] + r_out,
        out_shape=[jax.ShapeDtypeStruct((T, D), F32), jax.ShapeDtypeStruct((T, F), BF16),
                   jax.ShapeDtypeStruct((T, F), BF16), jax.ShapeDtypeStruct((T, F), BF16),
                   jax.ShapeDtypeStruct((T, D), BF16), jax.ShapeDtypeStruct((1, D), F32)] + r_shape,
        scratch_shapes=[pltpu.VMEM((tm, D), F32)] + r_scratch,
        compiler_params=_params(2),
    )(dy, x, gain, gate, up, wg, wu, wd, *r_args)
    return outs[:6], outs[6:]


def _matmul_tn(a, b, tk, tn, ride=None, tt=2048):
    T, K = a.shape
    N = b.shape[1]
    tk, tn = min(tk, K), min(tn, N)
    grid = (K // tk, N // tn, T // tt)
    r_args, r_in, r_shape, r_out, r_scratch = _ride_parts(ride)

    def body(a_ref, b_ref, o_ref):
        @pl.when(pl.program_id(2) == 0)
        def _():
            o_ref[...] = jnp.zeros_like(o_ref)

        o_ref[...] += _dot_tn(a_ref[...].astype(BF16), b_ref[...].astype(BF16))

    first = lambda: (pl.program_id(0) == 0) & (pl.program_id(1) == 0) & (pl.program_id(2) == 0)
    last = lambda: ((pl.program_id(0) == grid[0] - 1) & (pl.program_id(1) == grid[1] - 1)
                    & (pl.program_id(2) == grid[2] - 1))
    outs = pl.pallas_call(
        _riding(body, 2, 1, 0, ride, first, last), name="matmul_tn", grid=grid,
        in_specs=[pl.BlockSpec((tt, tk), lambda k, n, t: (t, k)), pl.BlockSpec((tt, tn), lambda k, n, t: (t, n))] + r_in,
        out_specs=[pl.BlockSpec((tk, tn), lambda k, n, t: (k, n))] + r_out,
        out_shape=[jax.ShapeDtypeStruct((K, N), F32)] + r_shape,
        scratch_shapes=r_scratch,
        compiler_params=_params(3),
    )(a, b, *r_args)
    return outs[0], outs[1:]


def _in_proj(x, gain, w, gq, gk, tm=512):
    T, D = x.shape
    N = w.shape[0]
    W = DIL_WIDTH

    def body(x_ref, g_ref, w_ref, gq_ref, gk_ref, h_ref, p_ref, qh_ref, kh_ref):
        xv = x_ref[...]
        h = (xv * _rstd(xv) * g_ref[...]).astype(BF16)
        h_ref[...] = h
        p_ref[...] = _dot_nt(h, w_ref[...])
        lo = lax.broadcasted_iota(jnp.int32, (tm, 128), 1) < DIL_HD
        for hp in range(DIL_HEADS // 2):
            q = p_ref[:, 128 * hp:128 * (hp + 1)]
            k = p_ref[:, W + 128 * hp:W + 128 * (hp + 1)]
            qh_ref[:, 128 * hp:128 * (hp + 1)] = (q * _pair_rstd(q, lo) * gq_ref[...]).astype(BF16).astype(F32)
            kh_ref[:, 128 * hp:128 * (hp + 1)] = (k * _pair_rstd(k, lo) * gk_ref[...]).astype(BF16).astype(F32)

    row = lambda i: (i, 0)
    fix = lambda i: (0, 0)
    return pl.pallas_call(
        body, name="in_proj", grid=(T // tm,),
        in_specs=[pl.BlockSpec((tm, D), row), pl.BlockSpec((1, D), fix), pl.BlockSpec((N, D), fix),
                  pl.BlockSpec((1, 128), fix), pl.BlockSpec((1, 128), fix)],
        out_specs=[pl.BlockSpec((tm, D), row), pl.BlockSpec((tm, N), row), pl.BlockSpec((tm, W), row),
                   pl.BlockSpec((tm, W), row)],
        out_shape=[jax.ShapeDtypeStruct((T, D), BF16), jax.ShapeDtypeStruct((T, N), F32),
                   jax.ShapeDtypeStruct((T, W), F32), jax.ShapeDtypeStruct((T, W), F32)],
        compiler_params=_params(1),
    )(x, gain, w, gq, gk)


def _in_proj_bwd(dx_up, x, gain, w, proj, gq, gk, dqkv, dcq, dckv, dkpe, ride=None, tm=512):
    T, D = x.shape
    N = w.shape[0]
    W = DIL_WIDTH
    nb = len(dqkv)

    def body(*refs):
        dxu_ref, x_ref, g_ref, w_ref, q_ref, k_ref, gq_ref, gk_ref = refs[:8]
        dil_refs = refs[8:8 + 3 * nb]
        dcq_ref, dckv_ref, dkpe_ref, dx_ref, dp_ref, dgain_ref, dgq_ref, dgk_ref = refs[8 + 3 * nb:]

        @pl.when(pl.program_id(0) == 0)
        def _():
            for ref in (dgain_ref, dgq_ref, dgk_ref):
                ref[...] = jnp.zeros_like(ref)

        lo = lax.broadcasted_iota(jnp.int32, (tm, 128), 1) < DIL_HD
        norms = ((q_ref, gq_ref, dgq_ref), (k_ref, gk_ref, dgk_ref))
        for part in range(3):
            acc = dil_refs[part][...]
            for b in range(1, nb):
                acc = acc + dil_refs[3 * b + part][...]
            if part == 2:
                dp_ref[:, 2 * W:3 * W] = acc.astype(BF16)
                continue
            raw_ref, gn_ref, dgn_ref = norms[part]
            for hp in range(DIL_HEADS // 2):
                raw = raw_ref[:, 128 * hp:128 * (hp + 1)]
                d_raw, dgn = _pair_rms_bwd(acc[:, 128 * hp:128 * (hp + 1)], raw, _pair_rstd(raw, lo), gn_ref[...], lo)
                dp_ref[:, part * W + 128 * hp:part * W + 128 * (hp + 1)] = d_raw.astype(BF16)
                dgn_ref[...] += dgn
        dp_ref[:, 3 * W:3 * W + 256] = dcq_ref[...].astype(BF16)
        dp_ref[:, 3 * W + 256:3 * W + 384] = dckv_ref[...].astype(BF16)
        dp_ref[:, 3 * W + 384:N] = dkpe_ref[...].astype(BF16)
        dh = _dot(dp_ref[...], w_ref[...])
        xv = x_ref[...]
        dxn, dgc = _rms_bwd(dh, xv, g_ref[...], _rstd(xv))
        dx_ref[...] = dxu_ref[...] + dxn
        dgain_ref[...] += jnp.sum(dgc, axis=0, keepdims=True)

    row = lambda i: (i, 0)
    fix = lambda i: (0, 0)
    r_args, r_in, r_shape, r_out, r_scratch = _ride_parts(ride)
    first = lambda: pl.program_id(0) == 0
    last = lambda: pl.program_id(0) == T // tm - 1
    outs = pl.pallas_call(
        _riding(body, 11 + 3 * nb, 5, 0, ride, first, last), name="in_proj_bwd", grid=(T // tm,),
        in_specs=[pl.BlockSpec((tm, D), row), pl.BlockSpec((tm, D), row), pl.BlockSpec((1, D), fix),
                  pl.BlockSpec((N, D), fix), pl.BlockSpec((tm, W), row), pl.BlockSpec((tm, W), lambda i: (i, 1)),
                  pl.BlockSpec((1, 128), fix), pl.BlockSpec((1, 128), fix)] + [pl.BlockSpec((tm, W), row)] * (3 * nb)
                 + [pl.BlockSpec((tm, 256), row), pl.BlockSpec((tm, 128), row), pl.BlockSpec((tm, 128), row)] + r_in,
        out_specs=[pl.BlockSpec((tm, D), row), pl.BlockSpec((tm, N), row), pl.BlockSpec((1, D), fix),
                   pl.BlockSpec((1, 128), fix), pl.BlockSpec((1, 128), fix)] + r_out,
        out_shape=[jax.ShapeDtypeStruct((T, D), F32), jax.ShapeDtypeStruct((T, N), BF16),
                   jax.ShapeDtypeStruct((1, D), F32), jax.ShapeDtypeStruct((1, 128), F32),
                   jax.ShapeDtypeStruct((1, 128), F32)] + r_shape,
        scratch_shapes=r_scratch,
        compiler_params=_params(1),
    )(dx_up, x, gain, w, proj, proj, gq, gk, *[a for triple in dqkv for a in triple], dcq, dckv, dkpe, *r_args)
    return outs[:5], outs[5:]


def _out_proj(x, o_dil, o_mla, g_dil, g_mla, w, tm=512):
    T, D = x.shape
    W = o_dil.shape[1]

    def body(x_ref, od_ref, om_ref, gd_ref, gm_ref, w_ref, xo_ref, oc_ref):
        od, om = od_ref[...], om_ref[...]
        oc_ref[:, 0:W] = (od * _rstd(od) * gd_ref[...]).astype(BF16)
        oc_ref[:, W:2 * W] = (om * _rstd(om) * gm_ref[...]).astype(BF16)
        xo_ref[...] = x_ref[...] + _dot(oc_ref[...], w_ref[...])

    row = lambda i: (i, 0)
    fix = lambda i: (0, 0)
    return pl.pallas_call(
        body, name="out_proj", grid=(T // tm,),
        in_specs=[pl.BlockSpec((tm, D), row), pl.BlockSpec((tm, W), row), pl.BlockSpec((tm, W), row),
                  pl.BlockSpec((1, W), fix), pl.BlockSpec((1, W), fix), pl.BlockSpec((2 * W, D), fix)],
        out_specs=[pl.BlockSpec((tm, D), row), pl.BlockSpec((tm, 2 * W), row)],
        out_shape=[jax.ShapeDtypeStruct((T, D), F32), jax.ShapeDtypeStruct((T, 2 * W), BF16)],
        compiler_params=_params(1),
    )(x, o_dil, o_mla, g_dil, g_mla, w)


def _out_proj_bwd(dx, o_dil, o_mla, g_dil, g_mla, w, tm=512):
    T, D = dx.shape
    W = o_dil.shape[1]

    def body(dx_ref, od_ref, om_ref, gd_ref, gm_ref, w_ref, dod_ref, dom_ref, dgd_ref, dgm_ref):
        @pl.when(pl.program_id(0) == 0)
        def _():
            dgd_ref[...] = jnp.zeros_like(dgd_ref)
            dgm_ref[...] = jnp.zeros_like(dgm_ref)

        doc = _dot_nt(dx_ref[...].astype(BF16), w_ref[...])
        od, om = od_ref[...], om_ref[...]
        dod, dgd = _rms_bwd(doc[:, 0:W], od, gd_ref[...], _rstd(od))
        dom, dgm = _rms_bwd(doc[:, W:2 * W], om, gm_ref[...], _rstd(om))
        dod_ref[...] = dod
        dom_ref[...] = dom
        dgd_ref[...] += jnp.sum(dgd, axis=0, keepdims=True)
        dgm_ref[...] += jnp.sum(dgm, axis=0, keepdims=True)

    row = lambda i: (i, 0)
    fix = lambda i: (0, 0)
    return pl.pallas_call(
        body, name="out_proj_bwd", grid=(T // tm,),
        in_specs=[pl.BlockSpec((tm, D), row), pl.BlockSpec((tm, W), row), pl.BlockSpec((tm, W), row),
                  pl.BlockSpec((1, W), fix), pl.BlockSpec((1, W), fix), pl.BlockSpec((2 * W, D), fix)],
        out_specs=[pl.BlockSpec((tm, W), row), pl.BlockSpec((tm, W), row),
                   pl.BlockSpec((1, W), fix), pl.BlockSpec((1, W), fix)],
        out_shape=[jax.ShapeDtypeStruct((T, W), F32), jax.ShapeDtypeStruct((T, W), F32),
                   jax.ShapeDtypeStruct((1, W), F32), jax.ShapeDtypeStruct((1, W), F32)],
        compiler_params=_params(1),
    )(dx, o_dil, o_mla, g_dil, g_mla, w)


def _pair_rstd(x, lo):
    sq = x * x
    s0 = jnp.sum(jnp.where(lo, sq, 0.0), axis=-1, keepdims=True)
    s1 = jnp.sum(jnp.where(lo, 0.0, sq), axis=-1, keepdims=True)
    return jnp.where(lo, lax.rsqrt(s0 / DIL_HD + EPS), lax.rsqrt(s1 / DIL_HD + EPS))


def _pair_rms_bwd(dn, x, r, g, lo):
    u = dn * g
    t = u * x
    d0 = jnp.sum(jnp.where(lo, t, 0.0), axis=-1, keepdims=True)
    d1 = jnp.sum(jnp.where(lo, 0.0, t), axis=-1, keepdims=True)
    dx = r * u - x * (r * r * r) * (jnp.where(lo, d0, d1) / DIL_HD)
    return dx, jnp.sum(dn * x * r, axis=0, keepdims=True)


def _pair_col(x, lo, e):
    sel = lo if e == 0 else jnp.logical_not(lo)
    return jnp.max(jnp.where(sel, x, NEG), axis=-1, keepdims=True)


def _dil_masks(n):
    lo = lax.broadcasted_iota(jnp.int32, (DIL_BLOCK, DIL_BLOCK), 1) < DIL_HD
    row = lax.broadcasted_iota(jnp.int32, (2 * DIL_BLOCK, 2 * DIL_BLOCK), 0) % DIL_BLOCK
    col = lax.broadcasted_iota(jnp.int32, (2 * DIL_BLOCK, 2 * DIL_BLOCK), 1)
    prev = jnp.logical_and(jnp.logical_and(col < DIL_BLOCK, col >= row), n > 0)
    cur = jnp.logical_and(col >= DIL_BLOCK, col - DIL_BLOCK <= row)
    return lo, jnp.logical_or(prev, cur)


def _stack_heads(x, lo):
    return jnp.concatenate([jnp.where(lo, x, 0.0), jnp.where(lo, 0.0, x)], axis=0)


def _unstack_heads(x2, lo):
    return jnp.where(lo, x2[:DIL_BLOCK], x2[DIL_BLOCK:])


def _dil_pairs(d):
    return 4 if d == 1 else 1


def _sub_rows(r, d):
    return pl.ds(r, DIL_BLOCK, stride=d) if d > 1 else pl.ds(0, DIL_BLOCK)


def _split_subsequences(loads, d, P):
    for r in range(d):
        for p in range(P):
            for block, scratch, part in loads:
                piece = block[_sub_rows(r, d), pl.ds(128 * p, 128)]
                if part is None:
                    scratch[r * P + p] = piece
                else:
                    scratch[r * P + p, pl.ds(DIL_BLOCK * part, DIL_BLOCK), :] = piece


def _merge_subsequences(pairs, d, P):
    for r in range(d):
        for p in range(P):
            for block, scratch in pairs:
                block[_sub_rows(r, d), pl.ds(128 * p, 128)] = scratch[r * P + p]


def _dil_fwd(qh, kh, proj, bias, d, prev):
    T = proj.shape[0]
    P = _dil_pairs(d)
    rows, cw, n_it = DIL_BLOCK * d, 128 * P, d * P
    nblk = T // rows
    has_prev = prev is not None

    def body(*refs):
        q_ref, kp_ref, kc_ref, vp_ref, vc_ref, bias_ref = refs[:6]
        refs = refs[6:]
        if has_prev:
            oin_ref, lin_ref = refs[:2]
            refs = refs[2:]
        o_ref, l_ref, qs, ks, vs, os_, ls_ = refs[:7]
        pb, n = pl.program_id(0), pl.program_id(1)
        lo, valid = _dil_masks(n)
        loads = [(q_ref, qs, None), (kp_ref, ks, 0), (kc_ref, ks, 1), (vp_ref, vs, 0), (vc_ref, vs, 1)]
        if has_prev:
            ois, lis = refs[7:]
            loads += [(oin_ref, ois, None), (lin_ref, lis, None)]
        _split_subsequences(loads, d, P)

        def step(i, carry):
            q2 = _stack_heads(qs[i], lo).astype(BF16)
            s = jnp.where(valid, _dot_nt(q2, ks[i].astype(BF16)) + bias_ref[pb * P + i % P], NEG)
            m = jnp.max(s, axis=-1, keepdims=True)
            p = jnp.exp(s - m)
            l = jnp.sum(p, axis=-1, keepdims=True)
            o = _unstack_heads(_dot(p.astype(BF16), vs[i].astype(BF16)) / l, lo)
            lse = _unstack_heads(jnp.broadcast_to(m + jnp.log(l), (2 * DIL_BLOCK, 128)), lo)
            if has_prev:
                lin = lis[i]
                mx = jnp.maximum(lin, lse)
                lnew = mx + jnp.log(jnp.exp(lin - mx) + jnp.exp(lse - mx))
                o = ois[i] * jnp.exp(lin - lnew) + o * jnp.exp(lse - lnew)
                lse = lnew
            os_[i] = o
            ls_[i] = lse
            return carry

        lax.fori_loop(0, n_it, step, 0, unroll=4)
        _merge_subsequences([(o_ref, os_), (l_ref, ls_)], d, P)

    blk = (rows, cw)
    vcol = 2 * DIL_WIDTH // cw
    prev_n = lambda n: jnp.maximum(n - 1, 0)
    fix3 = lambda pb, n: (0, 0, 0)
    tok = pl.BlockSpec(blk, lambda pb, n: (n, pb))
    tok_prev = pl.BlockSpec(blk, lambda pb, n: (prev_n(n), pb))
    in_specs = [tok, tok_prev, tok,
                pl.BlockSpec(blk, lambda pb, n: (prev_n(n), vcol + pb)), pl.BlockSpec(blk, lambda pb, n: (n, vcol + pb)),
                pl.BlockSpec((DIL_HEADS // 2, 2 * DIL_BLOCK, 2 * DIL_BLOCK), fix3)]
    args = [qh, kh, kh, proj, proj, bias]
    one, two = pltpu.VMEM((n_it, DIL_BLOCK, 128), F32), pltpu.VMEM((n_it, 2 * DIL_BLOCK, 128), F32)
    scratch = [one, two, two, one, one]
    if has_prev:
        in_specs += [tok, tok]
        args += list(prev)
        scratch += [one, one]
    out = jax.ShapeDtypeStruct((T, DIL_WIDTH), F32)
    return pl.pallas_call(
        body, name=f"dil_fwd_d{d}", grid=(DIL_HEADS // 2 // P, nblk), in_specs=in_specs, out_specs=[tok, tok],
        out_shape=[out, out], scratch_shapes=scratch, compiler_params=_params(2),
    )(*args)


def _dil_bwd(qh, kh, proj, o, lse, do, bias, d):
    T = proj.shape[0]
    P = _dil_pairs(d)
    rows, cw, n_it = DIL_BLOCK * d, 128 * P, d * P
    nblk = T // rows

    def body(q_ref, kp_ref, kc_ref, vp_ref, vc_ref, o_ref, l_ref, do_ref, bias_ref,
             dq_ref, dk_ref, dv_ref, db_ref,
             qs, ks, vs, os_, ls_, dos, dqs, dks, dvs, ck, cv):
        pb, n = pl.program_id(0), pl.program_id(1)
        lo, valid = _dil_masks(n)

        @pl.when((pb == 0) & (n == 0))
        def _():
            db_ref[...] = jnp.zeros_like(db_ref)

        @pl.when(n == 0)
        def _():
            ck[...] = jnp.zeros_like(ck)
            cv[...] = jnp.zeros_like(cv)

        _split_subsequences([(q_ref, qs, None), (kp_ref, ks, 0), (kc_ref, ks, 1), (vp_ref, vs, 0), (vc_ref, vs, 1),
                             (o_ref, os_, None), (l_ref, ls_, None), (do_ref, dos, None)], d, P)

        def step(i, carry):
            pair = pb * P + i % P
            q2 = _stack_heads(qs[i], lo).astype(BF16)
            kcat, vcat = ks[i].astype(BF16), vs[i].astype(BF16)
            dov = dos[i]
            do2 = _stack_heads(dov, lo).astype(BF16)
            delta = jnp.sum(_stack_heads(dov * os_[i], lo), axis=-1, keepdims=True)
            lse_pair = ls_[i]
            lse2 = jnp.concatenate([_pair_col(lse_pair, lo, 0), _pair_col(lse_pair, lo, 1)], axis=0)
            s = jnp.where(valid, _dot_nt(q2, kcat) + bias_ref[pair], NEG)
            p = jnp.exp(s - lse2)
            ds = p * (_dot_nt(do2, vcat) - delta)
            db_ref[pair] += ds
            dsb = ds.astype(BF16)
            dqs[i] = _unstack_heads(_dot(dsb, kcat), lo)
            dk2 = _dot_tn(dsb, q2)
            dv2 = _dot_tn(p.astype(BF16), do2)
            dks[i] = ck[i] + dk2[:DIL_BLOCK]
            dvs[i] = cv[i] + dv2[:DIL_BLOCK]
            ck[i] = dk2[DIL_BLOCK:]
            cv[i] = dv2[DIL_BLOCK:]
            return carry

        @pl.when(n < nblk)
        def _():
            lax.fori_loop(0, n_it, step, 0, unroll=2)
            _merge_subsequences([(dq_ref, dqs), (dk_ref, dks), (dv_ref, dvs)], d, P)

        @pl.when(n == nblk)
        def _():
            _merge_subsequences([(dk_ref, ck), (dv_ref, cv)], d, P)

    blk = (rows, cw)
    vcol = 2 * DIL_WIDTH // cw
    qn_ = lambda n: jnp.minimum(n, nblk - 1)
    pn_ = lambda n: jnp.maximum(n - 1, 0)
    fix3 = lambda pb, n: (0, 0, 0)
    tok_q = pl.BlockSpec(blk, lambda pb, n: (qn_(n), pb))
    tok_p = pl.BlockSpec(blk, lambda pb, n: (pn_(n), pb))
    bias_spec = pl.BlockSpec((DIL_HEADS // 2, 2 * DIL_BLOCK, 2 * DIL_BLOCK), fix3)
    in_specs = [tok_q, tok_p, tok_q,
                pl.BlockSpec(blk, lambda pb, n: (pn_(n), vcol + pb)), pl.BlockSpec(blk, lambda pb, n: (qn_(n), vcol + pb)),
                tok_q, tok_q, tok_q, bias_spec]
    tok_shape = jax.ShapeDtypeStruct((T, DIL_WIDTH), F32)
    one, two = pltpu.VMEM((n_it, DIL_BLOCK, 128), F32), pltpu.VMEM((n_it, 2 * DIL_BLOCK, 128), F32)
    dq, dk, dv, db = pl.pallas_call(
        body, name=f"dil_bwd_d{d}", grid=(DIL_HEADS // 2 // P, nblk + 1), in_specs=in_specs,
        out_specs=[tok_q, tok_p, tok_p, bias_spec],
        out_shape=[tok_shape, tok_shape, tok_shape, jax.ShapeDtypeStruct(bias.shape, F32)],
        scratch_shapes=[one, two, two] + [one] * 8,
        compiler_params=_params(2),
    )(qh, kh, kh, proj, proj, o, lse, do, bias)
    return (dq, dk, dv), db


def _t5_bucket(dist):
    max_exact = REL_BUCKETS // 2
    dd = np.maximum(dist, 1).astype(np.float32)
    large = max_exact + (np.log(dd / max_exact) / np.log(REL_MAX_DIST / max_exact)
                         * (REL_BUCKETS - max_exact)).astype(np.int32)
    large = np.minimum(large, REL_BUCKETS - 1)
    return np.where(dist < max_exact, dist, large).astype(np.int32)


def _bucket_onehots():
    i = np.arange(DIL_BLOCK)[:, None]
    j = np.arange(DIL_BLOCK)[None, :]
    out = []
    for _, d in DIL_BRANCHES:
        dist = np.concatenate([DIL_BLOCK + i - j, i - j], axis=1)
        bucket = _t5_bucket(np.clip(dist, 0, None) * d).reshape(-1)
        out.append(jnp.asarray(np.eye(REL_BUCKETS, dtype=np.float32)[:, bucket], BF16))
    return out


def _bias_tables(rel_bias, onehots):
    n = len(onehots)

    def body(rb_ref, *refs):
        parts = _split3(rb_ref[...])
        for k in range(n):
            oh = refs[k][...]
            refs[n + k][...] = _dot(parts[0], oh) + _dot(parts[1], oh) + _dot(parts[2], oh)

    flat = pl.pallas_call(
        body, name="bias_tables",
        out_shape=[jax.ShapeDtypeStruct((DIL_HEADS, 2 * DIL_BLOCK * DIL_BLOCK), F32)] * n,
        compiler_params=pltpu.CompilerParams(vmem_limit_bytes=VMEM_LIMIT),
    )(rel_bias, *onehots)
    return [t.reshape(DIL_HEADS // 2, 2 * DIL_BLOCK, 2 * DIL_BLOCK) for t in flat]


def _bias_grad(dbs, onehots):
    n = len(dbs)
    dbs = [t.reshape(DIL_HEADS, 2 * DIL_BLOCK * DIL_BLOCK) for t in dbs]

    def body(*refs):
        acc = jnp.zeros((DIL_HEADS, REL_BUCKETS), F32)
        for k in range(n):
            oh = refs[n + k][...]
            for part in _split3(refs[k][...]):
                acc = acc + _dot_nt(part, oh)
        refs[-1][...] = acc

    return pl.pallas_call(
        body, name="bias_grad",
        out_shape=jax.ShapeDtypeStruct((DIL_HEADS, REL_BUCKETS), F32),
        compiler_params=pltpu.CompilerParams(vmem_limit_bytes=VMEM_LIMIT),
    )(*dbs, *onehots)


def _swap_halves(x):
    lane = lax.broadcasted_iota(jnp.int32, x.shape, 1)
    first = (lane % 64) < 32
    return jnp.where(first, pltpu.roll(x, 96, 1), pltpu.roll(x, 32, 1))


def _rope_tables(T):
    pos = jnp.arange(T, dtype=F32)
    inv_freq = ROPE_BASE ** (-jnp.arange(0, MLA_ROPE, 2, dtype=F32) / MLA_ROPE)
    ang = pos[:, None] * inv_freq[None, :]
    z = jnp.zeros((T, 128 - MLA_ROPE), F32)
    cos = jnp.concatenate([jnp.cos(ang), jnp.cos(ang), z], axis=-1)
    sin = jnp.concatenate([-jnp.sin(ang), jnp.sin(ang), z], axis=-1)
    return cos, sin


def _mla_prep(proj, cos, sin, g_qa, g_kva, g_q, g_k, wq, wkv, tm=512):
    T = proj.shape[0]
    H = MLA_HEADS
    scale = MLA_QK ** -0.5

    def body(cq_ref, ckv_ref, kpe_ref, cos_ref, sin_ref, gqa_ref, gkva_ref, gq_ref, gk_ref, wq_ref, wkv_ref,
             q_ref, k_ref, v_ref):
        cosv, sinv = cos_ref[...], sin_ref[...]

        def rope(x):
            return x * cosv + _swap_halves(x) * sinv

        cq = cq_ref[...]
        qp = _dot_nt((cq * _rstd(cq) * gqa_ref[...]).astype(BF16), wq_ref[...])
        ckv = ckv_ref[...]
        kvp = _dot((ckv * _rstd(ckv) * gkva_ref[...]).astype(BF16), wkv_ref[...])
        kpe = kpe_ref[...]
        one_hot_lane = (lax.broadcasted_iota(jnp.int32, (tm, 128), 1) == 0).astype(BF16)
        for h in range(H):
            a = qp[:, MLA_PAD * h:MLA_PAD * (h + 1)]
            qn = a * _rstd(a, MLA_QK) * gq_ref[...]
            q_ref[h, :, 0:128] = (qn[:, 0:128] * scale).astype(BF16)
            q_ref[h, :, 128:256] = (rope(qn[:, 128:256]) * scale).astype(BF16)
            kn = kvp[:, MLA_PAD * h:MLA_PAD * h + 128]
            r = lax.rsqrt((jnp.sum(kn * kn, axis=-1, keepdims=True)
                           + jnp.sum(kpe * kpe, axis=-1, keepdims=True)) / MLA_QK + EPS)
            k_ref[h, :, 0:128] = (kn * r * gk_ref[:, 0:128]).astype(BF16)
            k_ref[h, :, 128:256] = rope(kpe * r * gk_ref[:, 128:256]).astype(BF16)
            v_ref[h, :, 0:128] = kvp[:, MLA_PAD * h + 128:MLA_PAD * (h + 1)].astype(BF16)
            v_ref[h, :, 128:256] = one_hot_lane

    fix = lambda i: (0, 0)
    return pl.pallas_call(
        body, name="mla_prep", grid=(T // tm,),
        in_specs=[pl.BlockSpec((tm, 256), lambda i: (i, 6)), pl.BlockSpec((tm, 128), lambda i: (i, 14)),
                  pl.BlockSpec((tm, 128), lambda i: (i, 15)),
                  pl.BlockSpec((tm, 128), lambda i: (i, 0)), pl.BlockSpec((tm, 128), lambda i: (i, 0)),
                  pl.BlockSpec((1, 256), fix), pl.BlockSpec((1, 128), fix),
                  pl.BlockSpec((1, 256), fix), pl.BlockSpec((1, 256), fix),
                  pl.BlockSpec((H * MLA_PAD, 256), fix), pl.BlockSpec((128, H * MLA_PAD), fix)],
        out_specs=[pl.BlockSpec((H, tm, MLA_PAD), lambda i: (0, i, 0)), pl.BlockSpec((H, tm, MLA_PAD), lambda i: (0, i, 0)),
                   pl.BlockSpec((H, tm, 2 * MLA_V), lambda i: (0, i, 0))],
        out_shape=[jax.ShapeDtypeStruct((H, T, MLA_PAD), BF16), jax.ShapeDtypeStruct((H, T, MLA_PAD), BF16),
                   jax.ShapeDtypeStruct((H, T, 2 * MLA_V), BF16)],
        compiler_params=_params(1),
    )(proj, proj, proj, cos, sin, g_qa, g_kva, g_q, g_k, wq, wkv)


def _mla_prep_bwd(proj, cos, sin, g_qa, g_kva, g_q, g_k, wq, wkv, dq, dk, dv, tm=512):
    T = proj.shape[0]
    H = MLA_HEADS
    scale = MLA_QK ** -0.5

    def body(cq_ref, ckv_ref, kpe_ref, cos_ref, sin_ref, gqa_ref, gkva_ref, gq_ref, gk_ref, wq_ref, wkv_ref,
             dq_ref, dk_ref, dv_ref,
             dcq_ref, dckv_ref, dkpe_ref, cqn_ref, ckvn_ref, dqp_ref, dkvp_ref,
             dgqa_ref, dgkva_ref, dgq_ref, dgk_ref):
        @pl.when(pl.program_id(0) == 0)
        def _():
            for ref in (dgqa_ref, dgkva_ref, dgq_ref, dgk_ref):
                ref[...] = jnp.zeros_like(ref)

        cosv, sinv = cos_ref[...], sin_ref[...]

        def rope_bwd(dy):
            return dy * cosv + _swap_halves(dy * sinv)

        cq = cq_ref[...]
        rcq = _rstd(cq)
        cqn = (cq * rcq * gqa_ref[...]).astype(BF16)
        cqn_ref[...] = cqn
        qp = _dot_nt(cqn, wq_ref[...])
        ckv = ckv_ref[...]
        rckv = _rstd(ckv)
        ckvn = (ckv * rckv * gkva_ref[...]).astype(BF16)
        ckvn_ref[...] = ckvn
        kvp = _dot(ckvn, wkv_ref[...])
        kpe = kpe_ref[...]
        dkpe = jnp.zeros_like(kpe)
        dgq = jnp.zeros((1, MLA_PAD), F32)
        dgk = jnp.zeros((1, MLA_PAD), F32)
        for h in range(H):
            a = qp[:, MLA_PAD * h:MLA_PAD * (h + 1)]
            dqh = dq_ref[h]
            dn = jnp.concatenate([dqh[:, 0:128], rope_bwd(dqh[:, 128:256])], axis=-1) * scale
            da, dg = _rms_bwd(dn, a, gq_ref[...], _rstd(a, MLA_QK), MLA_QK)
            dgq = dgq + jnp.sum(dg, axis=0, keepdims=True)
            dqp_ref[:, MLA_PAD * h:MLA_PAD * (h + 1)] = da.astype(BF16)

            ak = jnp.concatenate([kvp[:, MLA_PAD * h:MLA_PAD * h + 128], kpe], axis=-1)
            dkh = dk_ref[h]
            dnk = jnp.concatenate([dkh[:, 0:128], rope_bwd(dkh[:, 128:256])], axis=-1)
            dak, dg = _rms_bwd(dnk, ak, gk_ref[...], _rstd(ak, MLA_QK), MLA_QK)
            dgk = dgk + jnp.sum(dg, axis=0, keepdims=True)
            dkpe = dkpe + dak[:, 128:256]
            dkvp_ref[:, MLA_PAD * h:MLA_PAD * h + 128] = dak[:, 0:128].astype(BF16)
            dkvp_ref[:, MLA_PAD * h + 128:MLA_PAD * (h + 1)] = dv_ref[h].astype(BF16)
        dkpe_ref[...] = dkpe
        dgq_ref[...] += dgq
        dgk_ref[...] += dgk
        dcq, dg = _rms_bwd(_dot(dqp_ref[...], wq_ref[...]), cq, gqa_ref[...], rcq)
        dcq_ref[...] = dcq
        dgqa_ref[...] += jnp.sum(dg, axis=0, keepdims=True)
        dckv, dg = _rms_bwd(_dot_nt(dkvp_ref[...], wkv_ref[...]), ckv, gkva_ref[...], rckv)
        dckv_ref[...] = dckv
        dgkva_ref[...] += jnp.sum(dg, axis=0, keepdims=True)

    fix = lambda i: (0, 0)
    row = lambda i: (i, 0)
    head = lambda i: (0, i, 0)
    return pl.pallas_call(
        body, name="mla_prep_bwd", grid=(T // tm,),
        in_specs=[pl.BlockSpec((tm, 256), lambda i: (i, 6)), pl.BlockSpec((tm, 128), lambda i: (i, 14)),
                  pl.BlockSpec((tm, 128), lambda i: (i, 15)),
                  pl.BlockSpec((tm, 128), row), pl.BlockSpec((tm, 128), row),
                  pl.BlockSpec((1, 256), fix), pl.BlockSpec((1, 128), fix),
                  pl.BlockSpec((1, 256), fix), pl.BlockSpec((1, 256), fix),
                  pl.BlockSpec((H * MLA_PAD, 256), fix), pl.BlockSpec((128, H * MLA_PAD), fix),
                  pl.BlockSpec((H, tm, MLA_PAD), head), pl.BlockSpec((H, tm, MLA_PAD), head),
                  pl.BlockSpec((H, tm, MLA_V), head)],
        out_specs=[pl.BlockSpec((tm, 256), row), pl.BlockSpec((tm, 128), row), pl.BlockSpec((tm, 128), row),
                   pl.BlockSpec((tm, 256), row), pl.BlockSpec((tm, 128), row),
                   pl.BlockSpec((tm, H * MLA_PAD), row), pl.BlockSpec((tm, H * MLA_PAD), row),
                   pl.BlockSpec((1, 256), fix), pl.BlockSpec((1, 128), fix),
                   pl.BlockSpec((1, 256), fix), pl.BlockSpec((1, 256), fix)],
        out_shape=[jax.ShapeDtypeStruct((T, 256), F32), jax.ShapeDtypeStruct((T, 128), F32),
                   jax.ShapeDtypeStruct((T, 128), F32),
                   jax.ShapeDtypeStruct((T, 256), BF16), jax.ShapeDtypeStruct((T, 128), BF16),
                   jax.ShapeDtypeStruct((T, H * MLA_PAD), BF16), jax.ShapeDtypeStruct((T, H * MLA_PAD), BF16),
                   jax.ShapeDtypeStruct((1, 256), F32), jax.ShapeDtypeStruct((1, 128), F32),
                   jax.ShapeDtypeStruct((1, 256), F32), jax.ShapeDtypeStruct((1, 256), F32)],
        compiler_params=_params(1),
    )(proj, proj, proj, cos, sin, g_qa, g_kva, g_q, g_k, wq, wkv, dq, dk, dv)


def _causal_pairs(T, tq, tk, key_major):
    pairs = [(i, j) for i in range(T // tq) for j in range(T // tk) if j * tk <= i * tq + tq - 1]
    if key_major:
        pairs.sort(key=lambda p: (p[1], p[0]))
    outer = [p[1] if key_major else p[0] for p in pairs]
    first = [int(t == 0 or outer[t] != outer[t - 1]) for t in range(len(pairs))]
    last = [int(t == len(pairs) - 1 or outer[t] != outer[t + 1]) for t in range(len(pairs))]
    tab = lambda v: jnp.asarray(np.array(v, np.int32))
    return tab([p[0] for p in pairs]), tab([p[1] for p in pairs]), tab(first), tab(last)


def _causal_scores(qv, kv, qi, ki, row0, tq, tk, masked):
    s = _dot_nt(qv, kv)
    if masked:
        row = lax.broadcasted_iota(jnp.int32, s.shape, 0) + (qi * tq + row0)
        col = lax.broadcasted_iota(jnp.int32, s.shape, 1) + ki * tk
        s = jnp.where(col <= row, s, NEG)
    return s


def _mla_attn(q, k, v, ride=None, tq=1024, tk=2048, rc=256):
    H, T, _ = q.shape
    tables = _causal_pairs(T, tq, tk, key_major=False)
    n_pairs = int(tables[0].shape[0])
    r_args, r_in, r_shape, r_out, r_scratch = _ride_parts(ride)

    def body(qt, kt, ft, lt, q_ref, k_ref, v_ref, o_ref, lse_ref, m_s, acc):
        t = pl.program_id(1)
        qi, ki = qt[t], kt[t]

        @pl.when(ft[t] == 1)
        def _():
            m_s[...] = jnp.full_like(m_s, NEG)
            acc[...] = jnp.zeros_like(acc)

        def update(masked):
            kk, vv = k_ref[...], v_ref[...]
            for c in range(tq // rc):
                rows = pl.ds(c * rc, rc)
                s = _causal_scores(q_ref[rows, :], kk, qi, ki, c * rc, tq, tk, masked)
                m_old = m_s[rows, :]
                m_new = jnp.maximum(m_old, jnp.max(s, axis=-1, keepdims=True))
                p = jnp.exp(s - m_new).astype(BF16)
                acc[rows, :] = jnp.exp(m_old - m_new) * acc[rows, :] + _dot(p, vv)
                m_s[rows, :] = m_new

        diagonal = (ki + 1) * tk - 1 > qi * tq

        @pl.when(diagonal)
        def _():
            update(True)

        @pl.when(jnp.logical_not(diagonal))
        def _():
            update(False)

        @pl.when(lt[t] == 1)
        def _():
            l = jnp.max(acc[:, MLA_V:], axis=-1, keepdims=True)
            o_ref[...] = acc[:, :MLA_V] / l
            lse_ref[...] = jnp.broadcast_to(m_s[...] + jnp.log(l), lse_ref.shape)

    qrow = lambda h, t, qt, kt, ft, lt: (h, qt[t], 0)
    krow = lambda h, t, qt, kt, ft, lt: (h, kt[t], 0)
    first = lambda: (pl.program_id(0) == 0) & (pl.program_id(1) == 0)
    last = lambda: (pl.program_id(0) == H - 1) & (pl.program_id(1) == n_pairs - 1)
    outs = pl.pallas_call(
        _riding(body, 7, 2, 2, ride, first, last), name="mla_attn",
        grid_spec=pltpu.PrefetchScalarGridSpec(
            num_scalar_prefetch=4, grid=(H, n_pairs),
            in_specs=[pl.BlockSpec((None, tq, MLA_PAD), qrow), pl.BlockSpec((None, tk, MLA_PAD), krow),
                      pl.BlockSpec((None, tk, 2 * MLA_V), krow)] + r_in,
            out_specs=[pl.BlockSpec((tq, MLA_V), lambda h, t, qt, kt, ft, lt: (qt[t], h)),
                       pl.BlockSpec((None, tq, 128), qrow)] + r_out,
            scratch_shapes=[pltpu.VMEM((tq, 1), F32), pltpu.VMEM((tq, 2 * MLA_V), F32)] + r_scratch),
        out_shape=[jax.ShapeDtypeStruct((T, H * MLA_V), F32), jax.ShapeDtypeStruct((H, T, 128), F32)] + r_shape,
        compiler_params=_params(2),
    )(*tables, q, k, v, *r_args)
    return outs[:2], outs[2:]


def _mla_attn_bwd(q, k, v, o, lse, do, ride=None, tq=1024, tk=1024, rc=512):
    H, T, _ = q.shape
    tables = _causal_pairs(T, tq, tk, key_major=True)
    n_pairs = int(tables[0].shape[0])
    r_args, r_in, r_shape, r_out, r_scratch = _ride_parts(ride)

    def body(qt, kt, ft, lt, q_ref, k_ref, v_ref, o_ref, lse_ref, do_ref, dq_ref, dk_ref, dv_ref, dk_s, dv_s):
        t = pl.program_id(1)
        qi, ki = qt[t], kt[t]

        @pl.when(t == 0)
        def _():
            dq_ref[...] = jnp.zeros_like(dq_ref)

        @pl.when(ft[t] == 1)
        def _():
            dk_s[...] = jnp.zeros_like(dk_s)
            dv_s[...] = jnp.zeros_like(dv_s)

        def update(masked):
            kk, vv = k_ref[...], v_ref[...]
            for c in range(tq // rc):
                rows = pl.ds(c * rc, rc)
                qv, dov = q_ref[rows, :], do_ref[rows, :]
                delta = jnp.sum(dov * o_ref[rows, :], axis=-1, keepdims=True)
                lse_v = jnp.max(lse_ref[rows, :], axis=-1, keepdims=True)
                p = jnp.exp(_causal_scores(qv, kk, qi, ki, c * rc, tq, tk, masked) - lse_v)
                dob = dov.astype(BF16)
                dv_s[...] += _dot_tn(p.astype(BF16), dob)
                ds = (p * (_dot_nt(dob, vv) - delta)).astype(BF16)
                dk_s[...] += _dot_tn(ds, qv)
                out_rows = pl.ds(pl.multiple_of(qi * tq + c * rc, rc), rc)
                dq_ref[out_rows, :] += _dot(ds, kk)

        diagonal = (ki + 1) * tk - 1 > qi * tq

        @pl.when(diagonal)
        def _():
            update(True)

        @pl.when(jnp.logical_not(diagonal))
        def _():
            update(False)

        @pl.when(lt[t] == 1)
        def _():
            dk_ref[...] = dk_s[...]
            dv_ref[...] = dv_s[...]

    qrow = lambda h, t, qt, kt, ft, lt: (h, qt[t], 0)
    krow = lambda h, t, qt, kt, ft, lt: (h, kt[t], 0)
    qcol = lambda h, t, qt, kt, ft, lt: (qt[t], h)
    first = lambda: (pl.program_id(0) == 0) & (pl.program_id(1) == 0)
    last = lambda: (pl.program_id(0) == H - 1) & (pl.program_id(1) == n_pairs - 1)
    outs = pl.pallas_call(
        _riding(body, 10, 3, 2, ride, first, last), name="mla_attn_bwd",
        grid_spec=pltpu.PrefetchScalarGridSpec(
            num_scalar_prefetch=4, grid=(H, n_pairs),
            in_specs=[pl.BlockSpec((None, tq, MLA_PAD), qrow), pl.BlockSpec((None, tk, MLA_PAD), krow),
                      pl.BlockSpec((None, tk, MLA_V), krow), pl.BlockSpec((tq, MLA_V), qcol),
                      pl.BlockSpec((None, tq, 128), qrow), pl.BlockSpec((tq, MLA_V), qcol)] + r_in,
            out_specs=[pl.BlockSpec((None, T, MLA_PAD), lambda h, t, qt, kt, ft, lt: (h, 0, 0)),
                       pl.BlockSpec((None, tk, MLA_PAD), krow), pl.BlockSpec((None, tk, MLA_V), krow)] + r_out,
            scratch_shapes=[pltpu.VMEM((tk, MLA_PAD), F32), pltpu.VMEM((tk, MLA_V), F32)] + r_scratch),
        out_shape=[jax.ShapeDtypeStruct((H, T, MLA_PAD), F32), jax.ShapeDtypeStruct((H, T, MLA_PAD), F32),
                   jax.ShapeDtypeStruct((H, T, MLA_V), F32)] + r_shape,
        compiler_params=_params(2),
    )(*tables, q, k, v, o, lse, do, *r_args)
    return outs[:3], outs[3:]


def _pair_gain(g):
    return jnp.tile(g.reshape(1, DIL_HD), (1, 2))


def _pad_gain(g):
    return jnp.pad(g.reshape(1, MLA_QK), ((0, 0), (0, MLA_PAD - MLA_QK)))


def _local_step(x, target, s, comm):
    T = x.shape[0]
    w = comm.w
    gq, gk = _pair_gain(s["dil_q_norm"]) * DIL_HD ** -0.5, _pair_gain(s["dil_k_norm"])
    g_q, g_k = _pad_gain(s["mla_q_norm"]), _pad_gain(s["mla_k_norm"])
    cos, sin = _rope_tables(T)
    onehots = _bucket_onehots()
    biases = _bias_tables(s["rel_bias"], onehots)

    (x1, h1, gate1, up1), got = _ffn_fwd(x, s["ffn1_norm"], w["ffn1_w_gate"], w["ffn1_w_up"], w["ffn1_w_down"],
                                         ride=comm.gather(_GROUPS["attn"]))
    comm.weights_landed(_GROUPS["attn"], got)
    hm, proj, qh, kh = _in_proj(x1, s["mix_norm"], w["w_in"], gq, gk)
    dil = None
    for (_, d), bias in zip(DIL_BRANCHES, biases):
        dil = _dil_fwd(qh, kh, proj, bias, d, dil)
    o_dil, lse_dil = dil
    q, k, v = _mla_prep(proj, cos, sin, s["mla_q_a_norm"], s["mla_kv_a_norm"], g_q, g_k, w["mla_w_q_b"], w["mla_w_kv_b"])
    (o_mla, lse_mla), got = _mla_attn(q, k, v, ride=comm.gather(_GROUPS["ffn2"]))
    comm.weights_landed(_GROUPS["ffn2"], got)
    x2, oc = _out_proj(x1, o_dil, o_mla, s["out_norm_dil"], s["out_norm_mla"], w["w_out"])
    (dy, h2, gate2, up2, loss), _ = _ffn_fwd(x2, s["ffn2_norm"], w["ffn2_w_gate"], w["ffn2_w_up"], w["ffn2_w_down"],
                                             target=target)

    gw, gs = {}, {}

    def ffn_grads(name, dy_in, x_in, h, gate, up, early=None):
        (dx, a, dg, du, dyh, dgain), _ = _ffn_bwd(dy_in, x_in, s[name + "_norm"], gate, up,
                                                  w[name + "_w_gate"], w[name + "_w_up"], w[name + "_w_down"])
        gs[name + "_norm"] = dgain
        down, gate_n, up_n = (name + "_w_down",), (name + "_w_gate",), (name + "_w_up",)
        ride = lambda names: comm.scatter(names, gw) if early is not None else None
        gw[down[0]], landed = _matmul_tn(a, dyh, 1408, 1024, ride=ride(early))
        comm.grads_landed(early or (), landed)
        gw[gate_n[0]], landed = _matmul_tn(dg, h, 1408, 1024, ride=ride(down))
        comm.grads_landed(down, landed)
        gw[up_n[0]], landed = _matmul_tn(du, h, 1408, 1024, ride=ride(gate_n))
        comm.grads_landed(gate_n, landed)
        return dx

    dx2 = ffn_grads("ffn2", dy, x2, h2, gate2, up2)
    gw["w_out"], _ = _matmul_tn(oc, dx2, 1024, 1024)
    do_dil, do_mla, gs["out_norm_dil"], gs["out_norm_mla"] = _out_proj_bwd(
        dx2, o_dil, o_mla, s["out_norm_dil"], s["out_norm_mla"], w["w_out"])

    (dq, dk, dv), got = _mla_attn_bwd(q, k, v, o_mla, lse_mla, do_mla, ride=comm.scatter(_GROUPS["ffn2"], gw))
    comm.grads_landed(_GROUPS["ffn2"], got)
    (dcq, dckv, dkpe, cqn, ckvn, dqp, dkvp, gs["mla_q_a_norm"], gs["mla_kv_a_norm"], dg_q, dg_k) = _mla_prep_bwd(
        proj, cos, sin, s["mla_q_a_norm"], s["mla_kv_a_norm"], g_q, g_k, w["mla_w_q_b"], w["mla_w_kv_b"], dq, dk, dv)
    gs["mla_q_norm"], gs["mla_k_norm"] = dg_q[:, :MLA_QK], dg_k[:, :MLA_QK]
    gw["mla_w_q_b"], _ = _matmul_tn(dqp, cqn, 1024, 256)
    gw["mla_w_kv_b"], _ = _matmul_tn(ckvn, dkvp, 128, 1024)

    dqkv, dbs = [], []
    for (_, d), bias in zip(DIL_BRANCHES, biases):
        triple, db = _dil_bwd(qh, kh, proj, o_dil, lse_dil, do_dil, bias, d)
        dqkv.append(triple)
        dbs.append(db)
    gs["rel_bias"] = _bias_grad(dbs, onehots)

    ready = tuple(n for n in _GROUPS["attn"] if n != "w_in")
    (dx1, dproj, gs["mix_norm"], dgq, dgk), got = _in_proj_bwd(dx2, x1, s["mix_norm"], w["w_in"], proj, gq, gk,
                                                               dqkv, dcq, dckv, dkpe, ride=comm.scatter(ready, gw))
    comm.grads_landed(ready, got)
    gs["dil_q_norm"] = (dgq[:, :DIL_HD] + dgq[:, DIL_HD:]) * DIL_HD ** -0.5
    gs["dil_k_norm"] = dgk[:, :DIL_HD] + dgk[:, DIL_HD:]
    gw["w_in"], _ = _matmul_tn(dproj, hm, 1024, 1024)
    grad_x = ffn_grads("ffn1", dx1, x, h1, gate1, up1, early=("w_in",))
    return loss, grad_x, gw, gs


def _position():
    x, y, c = lax.axis_index("x"), lax.axis_index("y"), lax.axis_index("c")
    return x, y, c, 4 * x + 2 * y + c


def _peer(x, y, c, k):
    px = 1 - x if k & 4 else x
    py = 1 - y if k & 2 else y
    pc = 1 - c if k & 1 else c
    return (px, py, pc), 4 * px + 2 * py + pc


class _Ride:
    def __init__(self, arrays, scatter):
        self.arrays, self.scatter = list(arrays), list(scatter)
        self.n = n = len(self.arrays)
        self.specs = [pl.BlockSpec(memory_space=pl.ANY)] * n
        self.out_shape = [jax.ShapeDtypeStruct(a.shape if sc else (N_DEV,) + a.shape, a.dtype)
                          for a, sc in zip(self.arrays, self.scatter)]
        self.scratch = [pltpu.SemaphoreType.DMA((n, N_DEV - 1)), pltpu.SemaphoreType.DMA((n, N_DEV - 1)),
                        pltpu.SemaphoreType.DMA((n,))]

    def _copies(self, ins, outs, sems):
        send_sems, recv_sems, local_sems = sems
        x, y, c, me = _position()
        copies = []
        for a in range(self.n):
            src = ins[a].at[me] if self.scatter[a] else ins[a]
            copies.append(pltpu.make_async_copy(src, outs[a].at[me], local_sems.at[a]))
        for k in range(1, N_DEV):
            peer, peer_idx = _peer(x, y, c, k)
            for a in range(self.n):
                src = ins[a].at[peer_idx] if self.scatter[a] else ins[a]
                copies.append(pltpu.make_async_remote_copy(
                    src_ref=src, dst_ref=outs[a].at[me], send_sem=send_sems.at[a, k - 1], recv_sem=recv_sems.at[a, k - 1],
                    device_id=peer, device_id_type=pl.DeviceIdType.MESH))
        return copies

    def start(self, ins, outs, sems):
        for cp in self._copies(ins, outs, sems):
            cp.start()

    def wait(self, ins, outs, sems):
        for cp in self._copies(ins, outs, sems):
            cp.wait()


def _ride_parts(ride):
    if ride is None:
        return [], [], [], [], []
    return ride.arrays, ride.specs, ride.out_shape, ride.specs, ride.scratch


def _riding(body, n_in, n_out, n_scratch, ride, first, last):
    if ride is None:
        return body
    n = ride.n
    i1, i2 = n_in + n, n_in + n + n_out
    i3, i4 = i2 + n, i2 + n + n_scratch

    def wrapped(*refs):
        ins, outs, sems = refs[n_in:i1], refs[i2:i3], refs[i4:]

        @pl.when(first())
        def _():
            ride.start(ins, outs, sems)

        body(*refs[:n_in], *refs[i1:i2], *refs[i3:i4])

        @pl.when(last())
        def _():
            ride.wait(ins, outs, sems)

    return wrapped


def _gather_two_level(arrays, name):
    n = len(arrays)
    out_shape = [jax.ShapeDtypeStruct((N_DEV,) + a.shape, a.dtype) for a in arrays]

    def body(*refs):
        ins, outs = refs[:n], refs[n:2 * n]
        send_sems, recv_sems, local_sems = refs[2 * n:]
        x, y, c, me = _position()
        sibling = (x, y, 1 - c)
        chips = [(1 - x, y), (x, 1 - y), (1 - x, 1 - y)]
        block = lambda px, py, pc: 4 * px + 2 * py + pc

        def copy(a, k, blk, to, src=None):
            dst = outs[a].at[blk]
            return pltpu.make_async_remote_copy(
                src_ref=dst if src is None else src, dst_ref=dst, send_sem=send_sems.at[a, k], recv_sem=recv_sems.at[a, k],
                device_id=to, device_id_type=pl.DeviceIdType.MESH)

        local = [pltpu.make_async_copy(ins[a], outs[a].at[me], local_sems.at[a]) for a in range(n)]
        first = []
        for a in range(n):
            first.append(copy(a, 0, me, sibling, src=ins[a]))
            first += [copy(a, 1 + j, me, (*chip, c), src=ins[a]) for j, chip in enumerate(chips)]
        for cp in local + first:
            cp.start()
        passed = []
        for j, chip in enumerate(chips):
            for a in range(n):
                copy(a, 1 + j, block(*chip, c), sibling).wait_recv()
                passed.append(copy(a, 4 + j, block(*chip, c), sibling))
                passed[-1].start()
        for a in range(n):
            copy(a, 0, block(x, y, 1 - c), sibling).wait_recv()
            for j, chip in enumerate(chips):
                copy(a, 4 + j, block(*chip, 1 - c), sibling).wait_recv()
        for cp in first + passed:
            cp.wait_send()
        for cp in local:
            cp.wait()

    any_spec = [pl.BlockSpec(memory_space=pl.ANY)] * n
    return pl.pallas_call(
        body, name=name, in_specs=any_spec, out_specs=any_spec, out_shape=out_shape,
        scratch_shapes=[pltpu.SemaphoreType.DMA((n, N_DEV - 1)), pltpu.SemaphoreType.DMA((n, N_DEV - 1)),
                        pltpu.SemaphoreType.DMA((n,))],
    )(*arrays)


def _exchange(ride, name):
    def body(*refs):
        parts = refs[:ride.n], refs[ride.n:2 * ride.n], refs[2 * ride.n:]
        ride.start(*parts)
        ride.wait(*parts)

    return pl.pallas_call(body, name=name, in_specs=ride.specs, out_specs=ride.specs, out_shape=ride.out_shape,
                          scratch_shapes=ride.scratch)(*ride.arrays)


def _adamw_math(wv, g, m, v):
    m = ADAM_B1 * m + (1.0 - ADAM_B1) * g
    v = ADAM_B2 * v + (1.0 - ADAM_B2) * (g * g)
    m_hat = m / (1.0 - ADAM_B1 ** ADAM_STEP)
    v_hat = v / (1.0 - ADAM_B2 ** ADAM_STEP)
    delta = -ADAM_LR * (m_hat / (jnp.sqrt(v_hat) + ADAM_EPS) + ADAM_WD * wv)
    return delta, m, v


def _adamw(parts, wv, m, v):
    _, R, C = wv.shape
    tr = max([t for t in range(16, 257, 16) if R % t == 0] or [R])

    def body(p_ref, w_ref, m_ref, v_ref, g_ref, d_ref, mo_ref, vo_ref):
        g = p_ref[0].astype(F32)
        for j in range(1, N_DEV):
            g = g + p_ref[j].astype(F32)
        d, mn, vn = _adamw_math(w_ref[0], g, m_ref[0], v_ref[0])
        g_ref[0] = g
        d_ref[0] = d
        mo_ref[0] = mn
        vo_ref[0] = vn

    blk = pl.BlockSpec((1, tr, C), lambda i: (0, i, 0))
    out = jax.ShapeDtypeStruct((1, R, C), F32)
    return pl.pallas_call(
        body, name="adamw", grid=(R // tr,),
        in_specs=[pl.BlockSpec((N_DEV, tr, C), lambda i: (0, i, 0)), blk, blk, blk],
        out_specs=[blk] * 4, out_shape=[out] * 4,
        compiler_params=_params(1),
    )(parts, wv, m, v)


_TRANSPOSED = ("ffn1_w_gate", "ffn1_w_up", "ffn2_w_gate", "ffn2_w_up", "w_in", "mla_w_q_b")
_GROUPS = {"ffn1": ("ffn1_w_gate", "ffn1_w_up", "ffn1_w_down"),
           "ffn2": ("ffn2_w_gate", "ffn2_w_up", "ffn2_w_down"),
           "attn": ("w_in", "mla_w_q_b", "mla_w_kv_b", "w_out")}
_SMALL = ("ffn1_norm", "mix_norm", "ffn2_norm", "out_norm_dil", "out_norm_mla", "mla_q_a_norm", "rel_bias",
          "mla_q_norm", "mla_k_norm", "mla_kv_a_norm", "dil_q_norm", "dil_k_norm")
_SMALL_ROWS = 48


def _cols_to_full(g):
    return g.transpose(1, 0, 2).reshape(g.shape[1], N_DEV * g.shape[2])


def _full_to_cols(f):
    return f.reshape(f.shape[0], N_DEV, f.shape[1] // N_DEV).transpose(1, 0, 2)


def _shard_view(name, a):
    return jnp.swapaxes(a, 1, 2) if name in _TRANSPOSED else a


def _to_full(name, g):
    if name == "mla_w_kv_b":
        return _cols_to_full(g)
    f = g.reshape(-1, g.shape[-1])
    if name == "w_in":
        f = jnp.pad(f, ((0, PROJ_PAD - PROJ_COLS), (0, 0)))
    if name == "mla_w_q_b":
        f = jnp.pad(f.reshape(MLA_HEADS, MLA_QK, -1), ((0, 0), (0, MLA_PAD - MLA_QK), (0, 0)))
        f = f.reshape(MLA_HEADS * MLA_PAD, -1)
    return f


def _to_parts(name, f):
    if name == "mla_w_kv_b":
        return _full_to_cols(f).astype(BF16)
    if name == "w_in":
        f = f[:PROJ_COLS]
    if name == "mla_w_q_b":
        f = f.reshape(MLA_HEADS, MLA_PAD, -1)[:, :MLA_QK].reshape(MLA_HEADS * MLA_QK, -1)
    return f.reshape(N_DEV, -1, f.shape[-1]).astype(BF16)


class _Comm:
    def __init__(self, shards):
        self.shards, self.w, self.recv = shards, {}, {}

    def gather(self, names):
        return _Ride([self.shards[n] for n in names], [False] * len(names))

    def scatter(self, names, grads):
        return _Ride([_to_parts(n, grads[n]) for n in names], [True] * len(names))

    def weights_landed(self, names, got):
        self.w.update({n: _to_full(n, g) for n, g in zip(names, got)})

    def grads_landed(self, names, got):
        self.recv.update(zip(names, got))


def _pack_small(parts, extra):
    flat = jnp.concatenate([parts[n].reshape(-1) for n in _SMALL] + [extra.reshape(-1)])
    return jnp.pad(flat, (0, _SMALL_ROWS * 128 - flat.shape[0])).reshape(_SMALL_ROWS, 128)


def _unpack_small(packed, shapes):
    flat, out, off = packed.reshape(-1), {}, 0
    for n in _SMALL:
        size = math.prod(shapes[n])
        out[n] = flat[off:off + size].reshape(shapes[n])
        off += size
    return out, flat[off]


_NAMES = ("ffn1_norm", "ffn1_w_gate", "ffn1_w_up", "ffn1_w_down", "mix_norm", "w_in", "dil_q_norm", "dil_k_norm",
          "rel_bias", "mla_q_a_norm", "mla_w_q_b", "mla_kv_a_norm", "mla_w_kv_b", "mla_q_norm", "mla_k_norm",
          "out_norm_dil", "out_norm_mla", "w_out", "ffn2_norm", "ffn2_w_gate", "ffn2_w_up", "ffn2_w_down")


def kernel(x, ffn1_norm, ffn1_w_gate, ffn1_w_up, ffn1_w_down, mix_norm, w_in, dil_q_norm, dil_k_norm, rel_bias, mla_q_a_norm, mla_w_q_b, mla_kv_a_norm, mla_w_kv_b, mla_q_norm, mla_k_norm, out_norm_dil, out_norm_mla, w_out, ffn2_norm, ffn2_w_gate, ffn2_w_up, ffn2_w_down, loss_target, m_ffn1_norm, m_ffn1_w_gate, m_ffn1_w_up, m_ffn1_w_down, m_mix_norm, m_w_in, m_dil_q_norm, m_dil_k_norm, m_rel_bias, m_mla_q_a_norm, m_mla_w_q_b, m_mla_kv_a_norm, m_mla_w_kv_b, m_mla_q_norm, m_mla_k_norm, m_out_norm_dil, m_out_norm_mla, m_w_out, m_ffn2_norm, m_ffn2_w_gate, m_ffn2_w_up, m_ffn2_w_down, v_ffn1_norm, v_ffn1_w_gate, v_ffn1_w_up, v_ffn1_w_down, v_mix_norm, v_w_in, v_dil_q_norm, v_dil_k_norm, v_rel_bias, v_mla_q_a_norm, v_mla_w_q_b, v_mla_kv_a_norm, v_mla_w_kv_b, v_mla_q_norm, v_mla_k_norm, v_out_norm_dil, v_out_norm_mla, v_w_out, v_ffn2_norm, v_ffn2_w_gate, v_ffn2_w_up, v_ffn2_w_down):
    args = locals()
    wts = {n: args[n] for n in _NAMES}
    mom = {n: args["m_" + n] for n in _NAMES}
    var = {n: args["v_" + n] for n in _NAMES}

    matrices = [n for group in _GROUPS.values() for n in group]
    comm = _Comm({n: _shard_view(n, wts[n])[0].astype(BF16) for n in matrices})
    comm.weights_landed(_GROUPS["ffn1"], _gather_two_level(comm.gather(_GROUPS["ffn1"]).arrays, "gather_first"))
    small = {n: wts[n].reshape(1, -1) if n != "rel_bias" else wts[n] for n in _SMALL}

    loss, grad_x, gw, gs = _local_step(x[0], loss_target[0], small, comm)

    last = comm.scatter(("ffn1_w_up",), gw)
    got = _exchange(_Ride(last.arrays + [_pack_small(gs, loss[0, 0])], last.scatter + [False]), "scatter_last")
    comm.grads_landed(("ffn1_w_up",), got[:-1])

    res = {n: [_shard_view(n, r) for r in _adamw(comm.recv[n], *(_shard_view(n, a[n]) for a in (wts, mom, var)))]
           for n in matrices}
    shapes = {n: wts[n].shape for n in _SMALL}
    zero = jnp.zeros((), F32)
    packed = _adamw(got[-1], _pack_small(wts, zero)[None], _pack_small(mom, zero)[None], _pack_small(var, zero)[None])
    loss_total = None
    for slot, q in enumerate(packed):
        vals, extra = _unpack_small(q, shapes)
        if slot == 0:
            loss_total = extra
        for n in _SMALL:
            res.setdefault(n, [None] * 4)[slot] = vals[n]
    outs = [loss_total, grad_x[None]]
    for slot in range(4):
        outs += [res[n][slot].reshape(wts[n].shape) for n in _NAMES]
    return tuple(outs)
```

```python
import math

import numpy as np
import jax
import jax.numpy as jnp
from jax import lax
from jax.experimental import pallas as pl
from jax.experimental.pallas import tpu as pltpu

F32, BF16 = jnp.float32, jnp.bfloat16
EPS = 1e-6
NEG = -1e30
N_DEV = 8

DIL_HEADS, DIL_HD = 8, 64
DIL_WIDTH = DIL_HEADS * DIL_HD
DIL_BRANCHES = ((128, 1), (512, 4), (2048, 16))
DIL_BLOCK = 128
MLA_HEADS, MLA_NOPE, MLA_ROPE, MLA_V = 4, 128, 64, 128
MLA_QK = MLA_NOPE + MLA_ROPE
MLA_PAD = 256
ROPE_BASE = 10000.0
REL_BUCKETS, REL_MAX_DIST = 32, 2048
PROJ_COLS, PROJ_PAD = 1984, 2048
FFN_RESID = 0.5
ADAM_LR, ADAM_B1, ADAM_B2, ADAM_EPS, ADAM_WD, ADAM_STEP = 0.001, 0.9, 0.999, 1e-08, 0.01, 10
VMEM_LIMIT = 62 * 1024 * 1024

_NT = (((1,), (1,)), ((), ()))
_TN = (((0,), (0,)), ((), ()))


def _dot(a, b):
    return jnp.dot(a, b, preferred_element_type=F32)


def _dot_nt(a, b):
    return lax.dot_general(a, b, _NT, preferred_element_type=F32)


def _dot_tn(a, b):
    return lax.dot_general(a, b, _TN, preferred_element_type=F32)


def _params(n_axes):
    return pltpu.CompilerParams(dimension_semantics=("arbitrary",) * n_axes, vmem_limit_bytes=VMEM_LIMIT)


def _rstd(x, n=None):
    n = x.shape[-1] if n is None else n
    return lax.rsqrt(jnp.sum(x * x, axis=-1, keepdims=True) / n + EPS)


def _rms_bwd(dy, x, g, r, n=None):
    n = x.shape[-1] if n is None else n
    u = dy * g
    dx = r * u - x * (r * r * r) * (jnp.sum(u * x, axis=-1, keepdims=True) / n)
    return dx, dy * x * r


def _sigmoid(x):
    return 1.0 / (1.0 + jnp.exp(-x))


def _split3(x):
    parts = []
    for _ in range(3):
        xb = x.astype(BF16)
        parts.append(xb)
        x = x - xb.astype(F32)
    return parts


def _ffn_fwd(x, gain, wg, wu, wd, ride=None, target=None, tm=512, tf=2816):
    T, D = x.shape
    F = wg.shape[0]
    ni, nj = T // tm, F // tf
    with_loss = target is not None
    r_args, r_in, r_shape, r_out, r_scratch = _ride_parts(ride)

    def body(*refs):
        x_ref, g_ref, wg_ref, wu_ref, wd_ref = refs[:5]
        t_ref = refs[5] if with_loss else None
        xo_ref, h_ref, gate_ref, up_ref = refs[5 + with_loss:9 + with_loss]
        loss_ref = refs[-2] if with_loss else None
        acc = refs[-1]
        i, j = pl.program_id(0), pl.program_id(1)

        @pl.when(j == 0)
        def _():
            xv = x_ref[...]
            h_ref[...] = (xv * _rstd(xv) * g_ref[...]).astype(BF16)
            acc[...] = jnp.zeros_like(acc)

        h = h_ref[...]
        g = _dot_nt(h, wg_ref[...])
        u = _dot_nt(h, wu_ref[...])
        gate_ref[...] = g.astype(BF16)
        up_ref[...] = u.astype(BF16)
        a = (g * _sigmoid(g) * u).astype(BF16)
        acc[...] += _dot(a, wd_ref[...])

        @pl.when(j == nj - 1)
        def _():
            y = x_ref[...] + FFN_RESID * acc[...]
            if with_loss:
                @pl.when(i == 0)
                def _():
                    loss_ref[...] = jnp.zeros_like(loss_ref)

                e = y - t_ref[...]
                xo_ref[...] = e * (1.0 / D)
                loss_ref[...] += (0.5 / D) * jnp.sum(e * e)
            else:
                xo_ref[...] = y

    row = lambda i, j: (i, 0)
    tile = lambda i, j: (i, j)
    n_in, n_out = 5 + with_loss, 4 + with_loss
    first = lambda: (pl.program_id(0) == 0) & (pl.program_id(1) == 0)
    last = lambda: (pl.program_id(0) == ni - 1) & (pl.program_id(1) == nj - 1)
    outs = pl.pallas_call(
        _riding(body, n_in, n_out, 1, ride, first, last), name="ffn_fwd", grid=(ni, nj),
        in_specs=[pl.BlockSpec((tm, D), row), pl.BlockSpec((1, D), lambda i, j: (0, 0)),
                  pl.BlockSpec((tf, D), lambda i, j: (j, 0)), pl.BlockSpec((tf, D), lambda i, j: (j, 0)),
                  pl.BlockSpec((tf, D), lambda i, j: (j, 0))] + [pl.BlockSpec((tm, D), row)] * with_loss + r_in,
        out_specs=[pl.BlockSpec((tm, D), row), pl.BlockSpec((tm, D), row), pl.BlockSpec((tm, tf), tile),
                   pl.BlockSpec((tm, tf), tile)] + [pl.BlockSpec((1, 128), lambda i, j: (0, 0))] * with_loss + r_out,
        out_shape=[jax.ShapeDtypeStruct((T, D), F32), jax.ShapeDtypeStruct((T, D), BF16),
                   jax.ShapeDtypeStruct((T, F), BF16), jax.ShapeDtypeStruct((T, F), BF16)]
        + [jax.ShapeDtypeStruct((1, 128), F32)] * with_loss + r_shape,
        scratch_shapes=[pltpu.VMEM((tm, D), F32)] + r_scratch,
        compiler_params=_params(2),
    )(x, gain, wg, wu, wd, *([target] if with_loss else []), *r_args)
    return outs[:n_out], outs[n_out:]


def _ffn_bwd(dy, x, gain, gate, up, wg, wu, wd, ride=None, tm=256, tf=2816):
    T, D = x.shape
    F = wg.shape[0]
    ni, nj = T // tm, F // tf
    r_args, r_in, r_shape, r_out, r_scratch = _ride_parts(ride)

    def body(dy_ref, x_ref, g_ref, gate_ref, up_ref, wg_ref, wu_ref, wd_ref,
             dx_ref, a_ref, dg_ref, du_ref, dyh_ref, dgain_ref, acc):
        i, j = pl.program_id(0), pl.program_id(1)

        @pl.when((i == 0) & (j == 0))
        def _():
            dgain_ref[...] = jnp.zeros_like(dgain_ref)

        @pl.when(j == 0)
        def _():
            dyh_ref[...] = (FFN_RESID * dy_ref[...]).astype(BF16)
            acc[...] = jnp.zeros_like(acc)

        da = _dot_nt(dyh_ref[...], wd_ref[...])
        g = gate_ref[...].astype(F32)
        u = up_ref[...].astype(F32)
        sig = _sigmoid(g)
        s = g * sig
        a_ref[...] = (s * u).astype(BF16)
        dg = (da * u * (sig * (1.0 + g * (1.0 - sig)))).astype(BF16)
        du = (da * s).astype(BF16)
        dg_ref[...] = dg
        du_ref[...] = du
        acc[...] += _dot(dg, wg_ref[...]) + _dot(du, wu_ref[...])

        @pl.when(j == nj - 1)
        def _():
            xv = x_ref[...]
            dxn, dgc = _rms_bwd(acc[...], xv, g_ref[...], _rstd(xv))
            dx_ref[...] = dy_ref[...] + dxn
            dgain_ref[...] += jnp.sum(dgc, axis=0, keepdims=True)

    first = lambda: (pl.program_id(0) == 0) & (pl.program_id(1) == 0)
    last = lambda: (pl.program_id(0) == ni - 1) & (pl.program_id(1) == nj - 1)
    outs = pl.pallas_call(
        _riding(body, 8, 6, 1, ride, first, last), name="ffn_bwd", grid=(ni, nj),
        in_specs=[pl.BlockSpec((tm, D), lambda i, j: (i, 0)), pl.BlockSpec((tm, D), lambda i, j: (i, 0)),
                  pl.BlockSpec((1, D), lambda i, j: (0, 0)),
                  pl.BlockSpec((tm, tf), lambda i, j: (i, j)), pl.BlockSpec((tm, tf), lambda i, j: (i, j)),
                  pl.BlockSpec((tf, D), lambda i, j: (j, 0)), pl.BlockSpec((tf, D), lambda i, j: (j, 0)),
                  pl.BlockSpec((tf, D), lambda i, j: (j, 0))] + r_in,
        out_specs=[pl.BlockSpec((tm, D), lambda i, j: (i, 0)),
                   pl.BlockSpec((tm, tf), lambda i, j: (i, j)), pl.BlockSpec((tm, tf), lambda i, j: (i, j)),
                   pl.BlockSpec((tm, tf), lambda i, j: (i, j)),
                   pl.BlockSpec((tm, D), lambda i, j: (i, 0)), pl.BlockSpec((1, D), lambda i, j: (0, 0))] + r_out,
        out_shape=[jax.ShapeDtypeStruct((T, D), F32), jax.ShapeDtypeStruct((T, F), BF16),
                   jax.ShapeDtypeStruct((T, F), BF16), jax.ShapeDtypeStruct((T, F), BF16),
                   jax.ShapeDtypeStruct((T, D), BF16), jax.ShapeDtypeStruct((1, D), F32)] + r_shape,
        scratch_shapes=[pltpu.VMEM((tm, D), F32)] + r_scratch,
        compiler_params=_params(2),
    )(dy, x, gain, gate, up, wg, wu, wd, *r_args)
    return outs[:6], outs[6:]


def _matmul_tn(a, b, tk, tn, ride=None, tt=2048):
    T, K = a.shape
    N = b.shape[1]
    tk, tn = min(tk, K), min(tn, N)
    grid = (K // tk, N // tn, T // tt)
    r_args, r_in, r_shape, r_out, r_scratch = _ride_parts(ride)

    def body(a_ref, b_ref, o_ref, acc):
        t = pl.program_id(2)

        @pl.when(t == 0)
        def _():
            acc[...] = jnp.zeros_like(acc)

        acc[...] += _dot_tn(a_ref[...].astype(BF16), b_ref[...].astype(BF16))

        @pl.when(t == grid[2] - 1)
        def _():
            o_ref[...] = acc[...].astype(BF16)

    first = lambda: (pl.program_id(0) == 0) & (pl.program_id(1) == 0) & (pl.program_id(2) == 0)
    last = lambda: ((pl.program_id(0) == grid[0] - 1) & (pl.program_id(1) == grid[1] - 1)
                    & (pl.program_id(2) == grid[2] - 1))
    outs = pl.pallas_call(
        _riding(body, 2, 1, 1, ride, first, last), name="matmul_tn", grid=grid,
        in_specs=[pl.BlockSpec((tt, tk), lambda k, n, t: (t, k)), pl.BlockSpec((tt, tn), lambda k, n, t: (t, n))] + r_in,
        out_specs=[pl.BlockSpec((tk, tn), lambda k, n, t: (k, n))] + r_out,
        out_shape=[jax.ShapeDtypeStruct((K, N), BF16)] + r_shape,
        scratch_shapes=[pltpu.VMEM((tk, tn), F32)] + r_scratch,
        compiler_params=_params(3),
    )(a, b, *r_args)
    return outs[0], outs[1:]


def _in_proj(x, gain, w, gq, gk, tm=512):
    T, D = x.shape
    N = w.shape[0]
    W = DIL_WIDTH

    def body(x_ref, g_ref, w_ref, gq_ref, gk_ref, h_ref, p_ref, qh_ref, kh_ref):
        xv = x_ref[...]
        h = (xv * _rstd(xv) * g_ref[...]).astype(BF16)
        h_ref[...] = h
        p_ref[...] = _dot_nt(h, w_ref[...])
        lo = lax.broadcasted_iota(jnp.int32, (tm, 128), 1) < DIL_HD
        for hp in range(DIL_HEADS // 2):
            q = p_ref[:, 128 * hp:128 * (hp + 1)]
            k = p_ref[:, W + 128 * hp:W + 128 * (hp + 1)]
            qh_ref[:, 128 * hp:128 * (hp + 1)] = (q * _pair_rstd(q, lo) * gq_ref[...]).astype(BF16).astype(F32)
            kh_ref[:, 128 * hp:128 * (hp + 1)] = (k * _pair_rstd(k, lo) * gk_ref[...]).astype(BF16).astype(F32)

    row = lambda i: (i, 0)
    fix = lambda i: (0, 0)
    return pl.pallas_call(
        body, name="in_proj", grid=(T // tm,),
        in_specs=[pl.BlockSpec((tm, D), row), pl.BlockSpec((1, D), fix), pl.BlockSpec((N, D), fix),
                  pl.BlockSpec((1, 128), fix), pl.BlockSpec((1, 128), fix)],
        out_specs=[pl.BlockSpec((tm, D), row), pl.BlockSpec((tm, N), row), pl.BlockSpec((tm, W), row),
                   pl.BlockSpec((tm, W), row)],
        out_shape=[jax.ShapeDtypeStruct((T, D), BF16), jax.ShapeDtypeStruct((T, N), F32),
                   jax.ShapeDtypeStruct((T, W), F32), jax.ShapeDtypeStruct((T, W), F32)],
        compiler_params=_params(1),
    )(x, gain, w, gq, gk)


def _in_proj_bwd(dx_up, x, gain, w, proj, gq, gk, dqkv, dcq, dckv, dkpe, ride=None, tm=512):
    T, D = x.shape
    N = w.shape[0]
    W = DIL_WIDTH
    nb = len(dqkv)

    def body(*refs):
        dxu_ref, x_ref, g_ref, w_ref, q_ref, k_ref, gq_ref, gk_ref = refs[:8]
        dil_refs = refs[8:8 + 3 * nb]
        dcq_ref, dckv_ref, dkpe_ref, dx_ref, dp_ref, dgain_ref, dgq_ref, dgk_ref = refs[8 + 3 * nb:]

        @pl.when(pl.program_id(0) == 0)
        def _():
            for ref in (dgain_ref, dgq_ref, dgk_ref):
                ref[...] = jnp.zeros_like(ref)

        lo = lax.broadcasted_iota(jnp.int32, (tm, 128), 1) < DIL_HD
        norms = ((q_ref, gq_ref, dgq_ref), (k_ref, gk_ref, dgk_ref))
        for part in range(3):
            acc = dil_refs[part][...]
            for b in range(1, nb):
                acc = acc + dil_refs[3 * b + part][...]
            if part == 2:
                dp_ref[:, 2 * W:3 * W] = acc.astype(BF16)
                continue
            raw_ref, gn_ref, dgn_ref = norms[part]
            for hp in range(DIL_HEADS // 2):
                raw = raw_ref[:, 128 * hp:128 * (hp + 1)]
                d_raw, dgn = _pair_rms_bwd(acc[:, 128 * hp:128 * (hp + 1)], raw, _pair_rstd(raw, lo), gn_ref[...], lo)
                dp_ref[:, part * W + 128 * hp:part * W + 128 * (hp + 1)] = d_raw.astype(BF16)
                dgn_ref[...] += dgn
        dp_ref[:, 3 * W:3 * W + 256] = dcq_ref[...].astype(BF16)
        dp_ref[:, 3 * W + 256:3 * W + 384] = dckv_ref[...].astype(BF16)
        dp_ref[:, 3 * W + 384:N] = dkpe_ref[...].astype(BF16)
        dh = _dot(dp_ref[...], w_ref[...])
        xv = x_ref[...]
        dxn, dgc = _rms_bwd(dh, xv, g_ref[...], _rstd(xv))
        dx_ref[...] = dxu_ref[...] + dxn
        dgain_ref[...] += jnp.sum(dgc, axis=0, keepdims=True)

    row = lambda i: (i, 0)
    fix = lambda i: (0, 0)
    r_args, r_in, r_shape, r_out, r_scratch = _ride_parts(ride)
    first = lambda: pl.program_id(0) == 0
    last = lambda: pl.program_id(0) == T // tm - 1
    outs = pl.pallas_call(
        _riding(body, 11 + 3 * nb, 5, 0, ride, first, last), name="in_proj_bwd", grid=(T // tm,),
        in_specs=[pl.BlockSpec((tm, D), row), pl.BlockSpec((tm, D), row), pl.BlockSpec((1, D), fix),
                  pl.BlockSpec((N, D), fix), pl.BlockSpec((tm, W), row), pl.BlockSpec((tm, W), lambda i: (i, 1)),
                  pl.BlockSpec((1, 128), fix), pl.BlockSpec((1, 128), fix)] + [pl.BlockSpec((tm, W), row)] * (3 * nb)
                 + [pl.BlockSpec((tm, 256), row), pl.BlockSpec((tm, 128), row), pl.BlockSpec((tm, 128), row)] + r_in,
        out_specs=[pl.BlockSpec((tm, D), row), pl.BlockSpec((tm, N), row), pl.BlockSpec((1, D), fix),
                   pl.BlockSpec((1, 128), fix), pl.BlockSpec((1, 128), fix)] + r_out,
        out_shape=[jax.ShapeDtypeStruct((T, D), F32), jax.ShapeDtypeStruct((T, N), BF16),
                   jax.ShapeDtypeStruct((1, D), F32), jax.ShapeDtypeStruct((1, 128), F32),
                   jax.ShapeDtypeStruct((1, 128), F32)] + r_shape,
        scratch_shapes=r_scratch,
        compiler_params=_params(1),
    )(dx_up, x, gain, w, proj, proj, gq, gk, *[a for triple in dqkv for a in triple], dcq, dckv, dkpe, *r_args)
    return outs[:5], outs[5:]


def _out_proj(x, o_dil, o_mla, g_dil, g_mla, w, tm=512):
    T, D = x.shape
    W = o_dil.shape[1]

    def body(x_ref, od_ref, om_ref, gd_ref, gm_ref, w_ref, xo_ref, oc_ref):
        od, om = od_ref[...], om_ref[...]
        oc_ref[:, 0:W] = (od * _rstd(od) * gd_ref[...]).astype(BF16)
        oc_ref[:, W:2 * W] = (om * _rstd(om) * gm_ref[...]).astype(BF16)
        xo_ref[...] = x_ref[...] + _dot(oc_ref[...], w_ref[...])

    row = lambda i: (i, 0)
    fix = lambda i: (0, 0)
    return pl.pallas_call(
        body, name="out_proj", grid=(T // tm,),
        in_specs=[pl.BlockSpec((tm, D), row), pl.BlockSpec((tm, W), row), pl.BlockSpec((tm, W), row),
                  pl.BlockSpec((1, W), fix), pl.BlockSpec((1, W), fix), pl.BlockSpec((2 * W, D), fix)],
        out_specs=[pl.BlockSpec((tm, D), row), pl.BlockSpec((tm, 2 * W), row)],
        out_shape=[jax.ShapeDtypeStruct((T, D), F32), jax.ShapeDtypeStruct((T, 2 * W), BF16)],
        compiler_params=_params(1),
    )(x, o_dil, o_mla, g_dil, g_mla, w)


def _out_proj_bwd(dx, o_dil, o_mla, g_dil, g_mla, w, tm=512):
    T, D = dx.shape
    W = o_dil.shape[1]

    def body(dx_ref, od_ref, om_ref, gd_ref, gm_ref, w_ref, dod_ref, dom_ref, dgd_ref, dgm_ref):
        @pl.when(pl.program_id(0) == 0)
        def _():
            dgd_ref[...] = jnp.zeros_like(dgd_ref)
            dgm_ref[...] = jnp.zeros_like(dgm_ref)

        doc = _dot_nt(dx_ref[...].astype(BF16), w_ref[...])
        od, om = od_ref[...], om_ref[...]
        dod, dgd = _rms_bwd(doc[:, 0:W], od, gd_ref[...], _rstd(od))
        dom, dgm = _rms_bwd(doc[:, W:2 * W], om, gm_ref[...], _rstd(om))
        dod_ref[...] = dod
        dom_ref[...] = dom
        dgd_ref[...] += jnp.sum(dgd, axis=0, keepdims=True)
        dgm_ref[...] += jnp.sum(dgm, axis=0, keepdims=True)

    row = lambda i: (i, 0)
    fix = lambda i: (0, 0)
    return pl.pallas_call(
        body, name="out_proj_bwd", grid=(T // tm,),
        in_specs=[pl.BlockSpec((tm, D), row), pl.BlockSpec((tm, W), row), pl.BlockSpec((tm, W), row),
                  pl.BlockSpec((1, W), fix), pl.BlockSpec((1, W), fix), pl.BlockSpec((2 * W, D), fix)],
        out_specs=[pl.BlockSpec((tm, W), row), pl.BlockSpec((tm, W), row),
                   pl.BlockSpec((1, W), fix), pl.BlockSpec((1, W), fix)],
        out_shape=[jax.ShapeDtypeStruct((T, W), F32), jax.ShapeDtypeStruct((T, W), F32),
                   jax.ShapeDtypeStruct((1, W), F32), jax.ShapeDtypeStruct((1, W), F32)],
        compiler_params=_params(1),
    )(dx, o_dil, o_mla, g_dil, g_mla, w)


def _pair_rstd(x, lo):
    sq = x * x
    s0 = jnp.sum(jnp.where(lo, sq, 0.0), axis=-1, keepdims=True)
    s1 = jnp.sum(jnp.where(lo, 0.0, sq), axis=-1, keepdims=True)
    return jnp.where(lo, lax.rsqrt(s0 / DIL_HD + EPS), lax.rsqrt(s1 / DIL_HD + EPS))


def _pair_rms_bwd(dn, x, r, g, lo):
    u = dn * g
    t = u * x
    d0 = jnp.sum(jnp.where(lo, t, 0.0), axis=-1, keepdims=True)
    d1 = jnp.sum(jnp.where(lo, 0.0, t), axis=-1, keepdims=True)
    dx = r * u - x * (r * r * r) * (jnp.where(lo, d0, d1) / DIL_HD)
    return dx, jnp.sum(dn * x * r, axis=0, keepdims=True)


def _pair_col(x, lo, e):
    sel = lo if e == 0 else jnp.logical_not(lo)
    return jnp.max(jnp.where(sel, x, NEG), axis=-1, keepdims=True)


def _dil_masks(n):
    lo = lax.broadcasted_iota(jnp.int32, (DIL_BLOCK, DIL_BLOCK), 1) < DIL_HD
    row = lax.broadcasted_iota(jnp.int32, (2 * DIL_BLOCK, 2 * DIL_BLOCK), 0) % DIL_BLOCK
    col = lax.broadcasted_iota(jnp.int32, (2 * DIL_BLOCK, 2 * DIL_BLOCK), 1)
    prev = jnp.logical_and(jnp.logical_and(col < DIL_BLOCK, col >= row), n > 0)
    cur = jnp.logical_and(col >= DIL_BLOCK, col - DIL_BLOCK <= row)
    return lo, jnp.logical_or(prev, cur)


def _stack_heads(x, lo):
    return jnp.concatenate([jnp.where(lo, x, 0.0), jnp.where(lo, 0.0, x)], axis=0)


def _unstack_heads(x2, lo):
    return jnp.where(lo, x2[:DIL_BLOCK], x2[DIL_BLOCK:])


def _dil_pairs(d):
    return 4 if d == 1 else 1


def _sub_rows(r, d):
    return pl.ds(r, DIL_BLOCK, stride=d) if d > 1 else pl.ds(0, DIL_BLOCK)


def _split_subsequences(loads, d, P):
    for r in range(d):
        for p in range(P):
            for block, scratch, part in loads:
                piece = block[_sub_rows(r, d), pl.ds(128 * p, 128)]
                if part is None:
                    scratch[r * P + p] = piece
                else:
                    scratch[r * P + p, pl.ds(DIL_BLOCK * part, DIL_BLOCK), :] = piece


def _keep_previous_block(scratches, n):
    for scratch in scratches:
        @pl.when(n == 0)
        def _():
            scratch[:, pl.ds(0, DIL_BLOCK), :] = jnp.zeros((scratch.shape[0], DIL_BLOCK, 128), F32)

        @pl.when(n > 0)
        def _():
            scratch[:, pl.ds(0, DIL_BLOCK), :] = scratch[:, pl.ds(DIL_BLOCK, DIL_BLOCK), :]


def _merge_subsequences(pairs, d, P):
    for r in range(d):
        for p in range(P):
            for block, scratch in pairs:
                block[_sub_rows(r, d), pl.ds(128 * p, 128)] = scratch[r * P + p]


def _dil_fwd(qh, kh, proj, bias, d, prev):
    T = proj.shape[0]
    P = _dil_pairs(d)
    rows, cw, n_it = DIL_BLOCK * d, 128 * P, d * P
    nblk = T // rows
    has_prev = prev is not None

    def body(*refs):
        q_ref, kc_ref, vc_ref, bias_ref = refs[:4]
        refs = refs[4:]
        if has_prev:
            oin_ref, lin_ref = refs[:2]
            refs = refs[2:]
        o_ref, l_ref, qs, ks, vs, os_, ls_ = refs[:7]
        pb, n = pl.program_id(0), pl.program_id(1)
        lo, valid = _dil_masks(n)
        _keep_previous_block((ks, vs), n)
        loads = [(q_ref, qs, None), (kc_ref, ks, 1), (vc_ref, vs, 1)]
        if has_prev:
            ois, lis = refs[7:]
            loads += [(oin_ref, ois, None), (lin_ref, lis, None)]
        _split_subsequences(loads, d, P)

        def step(i, carry):
            q2 = _stack_heads(qs[i], lo).astype(BF16)
            s = jnp.where(valid, _dot_nt(q2, ks[i].astype(BF16)) + bias_ref[pb * P + i % P], NEG)
            m = jnp.max(s, axis=-1, keepdims=True)
            p = jnp.exp(s - m)
            l = jnp.sum(p, axis=-1, keepdims=True)
            o = _unstack_heads(_dot(p.astype(BF16), vs[i].astype(BF16)) / l, lo)
            lse = _unstack_heads(jnp.broadcast_to(m + jnp.log(l), (2 * DIL_BLOCK, 128)), lo)
            if has_prev:
                lin = lis[i]
                mx = jnp.maximum(lin, lse)
                lnew = mx + jnp.log(jnp.exp(lin - mx) + jnp.exp(lse - mx))
                o = ois[i] * jnp.exp(lin - lnew) + o * jnp.exp(lse - lnew)
                lse = lnew
            os_[i] = o
            ls_[i] = lse
            return carry

        lax.fori_loop(0, n_it, step, 0, unroll=4)
        _merge_subsequences([(o_ref, os_), (l_ref, ls_)], d, P)

    blk = (rows, cw)
    vcol = 2 * DIL_WIDTH // cw
    fix3 = lambda pb, n: (0, 0, 0)
    tok = pl.BlockSpec(blk, lambda pb, n: (n, pb))
    in_specs = [tok, tok, pl.BlockSpec(blk, lambda pb, n: (n, vcol + pb)),
                pl.BlockSpec((DIL_HEADS // 2, 2 * DIL_BLOCK, 2 * DIL_BLOCK), fix3)]
    args = [qh, kh, proj, bias]
    one, two = pltpu.VMEM((n_it, DIL_BLOCK, 128), F32), pltpu.VMEM((n_it, 2 * DIL_BLOCK, 128), F32)
    scratch = [one, two, two, one, one]
    if has_prev:
        in_specs += [tok, tok]
        args += list(prev)
        scratch += [one, one]
    out = jax.ShapeDtypeStruct((T, DIL_WIDTH), F32)
    return pl.pallas_call(
        body, name=f"dil_fwd_d{d}", grid=(DIL_HEADS // 2 // P, nblk), in_specs=in_specs, out_specs=[tok, tok],
        out_shape=[out, out], scratch_shapes=scratch, compiler_params=_params(2),
    )(*args)


def _dil_bwd(qh, kh, proj, o, lse, do, bias, d):
    T = proj.shape[0]
    P = _dil_pairs(d)
    rows, cw, n_it = DIL_BLOCK * d, 128 * P, d * P
    nblk = T // rows

    def body(q_ref, kc_ref, vc_ref, o_ref, l_ref, do_ref, bias_ref,
             dq_ref, dk_ref, dv_ref, db_ref,
             qs, ks, vs, os_, ls_, dos, dqs, dks, dvs, ck, cv):
        pb, n = pl.program_id(0), pl.program_id(1)
        lo, valid = _dil_masks(n)

        @pl.when((pb == 0) & (n == 0))
        def _():
            db_ref[...] = jnp.zeros_like(db_ref)

        @pl.when(n == 0)
        def _():
            ck[...] = jnp.zeros_like(ck)
            cv[...] = jnp.zeros_like(cv)

        _keep_previous_block((ks, vs), n)
        _split_subsequences([(q_ref, qs, None), (kc_ref, ks, 1), (vc_ref, vs, 1),
                             (o_ref, os_, None), (l_ref, ls_, None), (do_ref, dos, None)], d, P)

        def step(i, carry):
            pair = pb * P + i % P
            q2 = _stack_heads(qs[i], lo).astype(BF16)
            kcat, vcat = ks[i].astype(BF16), vs[i].astype(BF16)
            dov = dos[i]
            do2 = _stack_heads(dov, lo).astype(BF16)
            delta = jnp.sum(_stack_heads(dov * os_[i], lo), axis=-1, keepdims=True)
            lse_pair = ls_[i]
            lse2 = jnp.concatenate([_pair_col(lse_pair, lo, 0), _pair_col(lse_pair, lo, 1)], axis=0)
            s = jnp.where(valid, _dot_nt(q2, kcat) + bias_ref[pair], NEG)
            p = jnp.exp(s - lse2)
            ds = p * (_dot_nt(do2, vcat) - delta)
            db_ref[pair] += ds
            dsb = ds.astype(BF16)
            dqs[i] = _unstack_heads(_dot(dsb, kcat), lo)
            dk2 = _dot_tn(dsb, q2)
            dv2 = _dot_tn(p.astype(BF16), do2)
            dks[i] = ck[i] + dk2[:DIL_BLOCK]
            dvs[i] = cv[i] + dv2[:DIL_BLOCK]
            ck[i] = dk2[DIL_BLOCK:]
            cv[i] = dv2[DIL_BLOCK:]
            return carry

        @pl.when(n < nblk)
        def _():
            lax.fori_loop(0, n_it, step, 0, unroll=2)
            _merge_subsequences([(dq_ref, dqs), (dk_ref, dks), (dv_ref, dvs)], d, P)

        @pl.when(n == nblk)
        def _():
            _merge_subsequences([(dk_ref, ck), (dv_ref, cv)], d, P)

    blk = (rows, cw)
    vcol = 2 * DIL_WIDTH // cw
    qn_ = lambda n: jnp.minimum(n, nblk - 1)
    pn_ = lambda n: jnp.maximum(n - 1, 0)
    fix3 = lambda pb, n: (0, 0, 0)
    tok_q = pl.BlockSpec(blk, lambda pb, n: (qn_(n), pb))
    tok_p = pl.BlockSpec(blk, lambda pb, n: (pn_(n), pb))
    bias_spec = pl.BlockSpec((DIL_HEADS // 2, 2 * DIL_BLOCK, 2 * DIL_BLOCK), fix3)
    in_specs = [tok_q, tok_q, pl.BlockSpec(blk, lambda pb, n: (qn_(n), vcol + pb)), tok_q, tok_q, tok_q, bias_spec]
    tok_shape = jax.ShapeDtypeStruct((T, DIL_WIDTH), F32)
    one, two = pltpu.VMEM((n_it, DIL_BLOCK, 128), F32), pltpu.VMEM((n_it, 2 * DIL_BLOCK, 128), F32)
    dq, dk, dv, db = pl.pallas_call(
        body, name=f"dil_bwd_d{d}", grid=(DIL_HEADS // 2 // P, nblk + 1), in_specs=in_specs,
        out_specs=[tok_q, tok_p, tok_p, bias_spec],
        out_shape=[tok_shape, tok_shape, tok_shape, jax.ShapeDtypeStruct(bias.shape, F32)],
        scratch_shapes=[one, two, two] + [one] * 8,
        compiler_params=_params(2),
    )(qh, kh, proj, o, lse, do, bias)
    return (dq, dk, dv), db


def _t5_bucket(dist):
    max_exact = REL_BUCKETS // 2
    dd = np.maximum(dist, 1).astype(np.float32)
    large = max_exact + (np.log(dd / max_exact) / np.log(REL_MAX_DIST / max_exact)
                         * (REL_BUCKETS - max_exact)).astype(np.int32)
    large = np.minimum(large, REL_BUCKETS - 1)
    return np.where(dist < max_exact, dist, large).astype(np.int32)


def _bucket_onehots():
    i = np.arange(DIL_BLOCK)[:, None]
    j = np.arange(DIL_BLOCK)[None, :]
    out = []
    for _, d in DIL_BRANCHES:
        dist = np.concatenate([DIL_BLOCK + i - j, i - j], axis=1)
        bucket = _t5_bucket(np.clip(dist, 0, None) * d).reshape(-1)
        out.append(jnp.asarray(np.eye(REL_BUCKETS, dtype=np.float32)[:, bucket], BF16))
    return out


def _bias_tables(rel_bias, onehots):
    n = len(onehots)

    def body(rb_ref, *refs):
        parts = _split3(rb_ref[...])
        for k in range(n):
            oh = refs[k][...]
            refs[n + k][...] = _dot(parts[0], oh) + _dot(parts[1], oh) + _dot(parts[2], oh)

    flat = pl.pallas_call(
        body, name="bias_tables",
        out_shape=[jax.ShapeDtypeStruct((DIL_HEADS, 2 * DIL_BLOCK * DIL_BLOCK), F32)] * n,
        compiler_params=pltpu.CompilerParams(vmem_limit_bytes=VMEM_LIMIT),
    )(rel_bias, *onehots)
    return [t.reshape(DIL_HEADS // 2, 2 * DIL_BLOCK, 2 * DIL_BLOCK) for t in flat]


def _bias_grad(dbs, onehots):
    n = len(dbs)
    dbs = [t.reshape(DIL_HEADS, 2 * DIL_BLOCK * DIL_BLOCK) for t in dbs]

    def body(*refs):
        acc = jnp.zeros((DIL_HEADS, REL_BUCKETS), F32)
        for k in range(n):
            oh = refs[n + k][...]
            for part in _split3(refs[k][...]):
                acc = acc + _dot_nt(part, oh)
        refs[-1][...] = acc

    return pl.pallas_call(
        body, name="bias_grad",
        out_shape=jax.ShapeDtypeStruct((DIL_HEADS, REL_BUCKETS), F32),
        compiler_params=pltpu.CompilerParams(vmem_limit_bytes=VMEM_LIMIT),
    )(*dbs, *onehots)


def _swap_halves(x):
    lane = lax.broadcasted_iota(jnp.int32, x.shape, 1)
    first = (lane % 64) < 32
    return jnp.where(first, pltpu.roll(x, 96, 1), pltpu.roll(x, 32, 1))


def _rope_tables(T):
    pos = jnp.arange(T, dtype=F32)
    inv_freq = ROPE_BASE ** (-jnp.arange(0, MLA_ROPE, 2, dtype=F32) / MLA_ROPE)
    ang = pos[:, None] * inv_freq[None, :]
    z = jnp.zeros((T, 128 - MLA_ROPE), F32)
    cos = jnp.concatenate([jnp.cos(ang), jnp.cos(ang), z], axis=-1)
    sin = jnp.concatenate([-jnp.sin(ang), jnp.sin(ang), z], axis=-1)
    return cos, sin


def _mla_prep(proj, cos, sin, g_qa, g_kva, g_q, g_k, wq, wkv, tm=512):
    T = proj.shape[0]
    H = MLA_HEADS
    scale = MLA_QK ** -0.5

    def body(cq_ref, ckv_ref, kpe_ref, cos_ref, sin_ref, gqa_ref, gkva_ref, gq_ref, gk_ref, wq_ref, wkv_ref,
             q_ref, k_ref, v_ref):
        cosv, sinv = cos_ref[...], sin_ref[...]

        def rope(x):
            return x * cosv + _swap_halves(x) * sinv

        cq = cq_ref[...]
        qp = _dot_nt((cq * _rstd(cq) * gqa_ref[...]).astype(BF16), wq_ref[...])
        ckv = ckv_ref[...]
        kvp = _dot((ckv * _rstd(ckv) * gkva_ref[...]).astype(BF16), wkv_ref[...])
        kpe = kpe_ref[...]
        one_hot_lane = (lax.broadcasted_iota(jnp.int32, (tm, 128), 1) == 0).astype(BF16)
        for h in range(H):
            a = qp[:, MLA_PAD * h:MLA_PAD * (h + 1)]
            qn = a * _rstd(a, MLA_QK) * gq_ref[...]
            q_ref[h, :, 0:128] = (qn[:, 0:128] * scale).astype(BF16)
            q_ref[h, :, 128:256] = (rope(qn[:, 128:256]) * scale).astype(BF16)
            kn = kvp[:, MLA_PAD * h:MLA_PAD * h + 128]
            r = lax.rsqrt((jnp.sum(kn * kn, axis=-1, keepdims=True)
                           + jnp.sum(kpe * kpe, axis=-1, keepdims=True)) / MLA_QK + EPS)
            k_ref[h, :, 0:128] = (kn * r * gk_ref[:, 0:128]).astype(BF16)
            k_ref[h, :, 128:256] = rope(kpe * r * gk_ref[:, 128:256]).astype(BF16)
            v_ref[h, :, 0:128] = kvp[:, MLA_PAD * h + 128:MLA_PAD * (h + 1)].astype(BF16)
            v_ref[h, :, 128:256] = one_hot_lane

    fix = lambda i: (0, 0)
    return pl.pallas_call(
        body, name="mla_prep", grid=(T // tm,),
        in_specs=[pl.BlockSpec((tm, 256), lambda i: (i, 6)), pl.BlockSpec((tm, 128), lambda i: (i, 14)),
                  pl.BlockSpec((tm, 128), lambda i: (i, 15)),
                  pl.BlockSpec((tm, 128), lambda i: (i, 0)), pl.BlockSpec((tm, 128), lambda i: (i, 0)),
                  pl.BlockSpec((1, 256), fix), pl.BlockSpec((1, 128), fix),
                  pl.BlockSpec((1, 256), fix), pl.BlockSpec((1, 256), fix),
                  pl.BlockSpec((H * MLA_PAD, 256), fix), pl.BlockSpec((128, H * MLA_PAD), fix)],
        out_specs=[pl.BlockSpec((H, tm, MLA_PAD), lambda i: (0, i, 0)), pl.BlockSpec((H, tm, MLA_PAD), lambda i: (0, i, 0)),
                   pl.BlockSpec((H, tm, 2 * MLA_V), lambda i: (0, i, 0))],
        out_shape=[jax.ShapeDtypeStruct((H, T, MLA_PAD), BF16), jax.ShapeDtypeStruct((H, T, MLA_PAD), BF16),
                   jax.ShapeDtypeStruct((H, T, 2 * MLA_V), BF16)],
        compiler_params=_params(1),
    )(proj, proj, proj, cos, sin, g_qa, g_kva, g_q, g_k, wq, wkv)


def _mla_prep_bwd(proj, cos, sin, g_qa, g_kva, g_q, g_k, wq, wkv, dq, dk, dv, tm=512):
    T = proj.shape[0]
    H = MLA_HEADS
    scale = MLA_QK ** -0.5

    def body(cq_ref, ckv_ref, kpe_ref, cos_ref, sin_ref, gqa_ref, gkva_ref, gq_ref, gk_ref, wq_ref, wkv_ref,
             dq_ref, dk_ref, dv_ref,
             dcq_ref, dckv_ref, dkpe_ref, cqn_ref, ckvn_ref, dqp_ref, dkvp_ref,
             dgqa_ref, dgkva_ref, dgq_ref, dgk_ref):
        @pl.when(pl.program_id(0) == 0)
        def _():
            for ref in (dgqa_ref, dgkva_ref, dgq_ref, dgk_ref):
                ref[...] = jnp.zeros_like(ref)

        cosv, sinv = cos_ref[...], sin_ref[...]

        def rope_bwd(dy):
            return dy * cosv + _swap_halves(dy * sinv)

        cq = cq_ref[...]
        rcq = _rstd(cq)
        cqn = (cq * rcq * gqa_ref[...]).astype(BF16)
        cqn_ref[...] = cqn
        qp = _dot_nt(cqn, wq_ref[...])
        ckv = ckv_ref[...]
        rckv = _rstd(ckv)
        ckvn = (ckv * rckv * gkva_ref[...]).astype(BF16)
        ckvn_ref[...] = ckvn
        kvp = _dot(ckvn, wkv_ref[...])
        kpe = kpe_ref[...]
        dkpe = jnp.zeros_like(kpe)
        dgq = jnp.zeros((1, MLA_PAD), F32)
        dgk = jnp.zeros((1, MLA_PAD), F32)
        for h in range(H):
            a = qp[:, MLA_PAD * h:MLA_PAD * (h + 1)]
            dqh = dq_ref[h]
            dn = jnp.concatenate([dqh[:, 0:128], rope_bwd(dqh[:, 128:256])], axis=-1) * scale
            da, dg = _rms_bwd(dn, a, gq_ref[...], _rstd(a, MLA_QK), MLA_QK)
            dgq = dgq + jnp.sum(dg, axis=0, keepdims=True)
            dqp_ref[:, MLA_PAD * h:MLA_PAD * (h + 1)] = da.astype(BF16)

            ak = jnp.concatenate([kvp[:, MLA_PAD * h:MLA_PAD * h + 128], kpe], axis=-1)
            dkh = dk_ref[h]
            dnk = jnp.concatenate([dkh[:, 0:128], rope_bwd(dkh[:, 128:256])], axis=-1)
            dak, dg = _rms_bwd(dnk, ak, gk_ref[...], _rstd(ak, MLA_QK), MLA_QK)
            dgk = dgk + jnp.sum(dg, axis=0, keepdims=True)
            dkpe = dkpe + dak[:, 128:256]
            dkvp_ref[:, MLA_PAD * h:MLA_PAD * h + 128] = dak[:, 0:128].astype(BF16)
            dkvp_ref[:, MLA_PAD * h + 128:MLA_PAD * (h + 1)] = dv_ref[h].astype(BF16)
        dkpe_ref[...] = dkpe
        dgq_ref[...] += dgq
        dgk_ref[...] += dgk
        dcq, dg = _rms_bwd(_dot(dqp_ref[...], wq_ref[...]), cq, gqa_ref[...], rcq)
        dcq_ref[...] = dcq
        dgqa_ref[...] += jnp.sum(dg, axis=0, keepdims=True)
        dckv, dg = _rms_bwd(_dot_nt(dkvp_ref[...], wkv_ref[...]), ckv, gkva_ref[...], rckv)
        dckv_ref[...] = dckv
        dgkva_ref[...] += jnp.sum(dg, axis=0, keepdims=True)

    fix = lambda i: (0, 0)
    row = lambda i: (i, 0)
    head = lambda i: (0, i, 0)
    return pl.pallas_call(
        body, name="mla_prep_bwd", grid=(T // tm,),
        in_specs=[pl.BlockSpec((tm, 256), lambda i: (i, 6)), pl.BlockSpec((tm, 128), lambda i: (i, 14)),
                  pl.BlockSpec((tm, 128), lambda i: (i, 15)),
                  pl.BlockSpec((tm, 128), row), pl.BlockSpec((tm, 128), row),
                  pl.BlockSpec((1, 256), fix), pl.BlockSpec((1, 128), fix),
                  pl.BlockSpec((1, 256), fix), pl.BlockSpec((1, 256), fix),
                  pl.BlockSpec((H * MLA_PAD, 256), fix), pl.BlockSpec((128, H * MLA_PAD), fix),
                  pl.BlockSpec((H, tm, MLA_PAD), head), pl.BlockSpec((H, tm, MLA_PAD), head),
                  pl.BlockSpec((H, tm, MLA_V), head)],
        out_specs=[pl.BlockSpec((tm, 256), row), pl.BlockSpec((tm, 128), row), pl.BlockSpec((tm, 128), row),
                   pl.BlockSpec((tm, 256), row), pl.BlockSpec((tm, 128), row),
                   pl.BlockSpec((tm, H * MLA_PAD), row), pl.BlockSpec((tm, H * MLA_PAD), row),
                   pl.BlockSpec((1, 256), fix), pl.BlockSpec((1, 128), fix),
                   pl.BlockSpec((1, 256), fix), pl.BlockSpec((1, 256), fix)],
        out_shape=[jax.ShapeDtypeStruct((T, 256), F32), jax.ShapeDtypeStruct((T, 128), F32),
                   jax.ShapeDtypeStruct((T, 128), F32),
                   jax.ShapeDtypeStruct((T, 256), BF16), jax.ShapeDtypeStruct((T, 128), BF16),
                   jax.ShapeDtypeStruct((T, H * MLA_PAD), BF16), jax.ShapeDtypeStruct((T, H * MLA_PAD), BF16),
                   jax.ShapeDtypeStruct((1, 256), F32), jax.ShapeDtypeStruct((1, 128), F32),
                   jax.ShapeDtypeStruct((1, 256), F32), jax.ShapeDtypeStruct((1, 256), F32)],
        compiler_params=_params(1),
    )(proj, proj, proj, cos, sin, g_qa, g_kva, g_q, g_k, wq, wkv, dq, dk, dv)


def _causal_pairs(T, tq, tk, key_major):
    pairs = [(i, j) for i in range(T // tq) for j in range(T // tk) if j * tk <= i * tq + tq - 1]
    if key_major:
        pairs.sort(key=lambda p: (p[1], p[0]))
    outer = [p[1] if key_major else p[0] for p in pairs]
    first = [int(t == 0 or outer[t] != outer[t - 1]) for t in range(len(pairs))]
    last = [int(t == len(pairs) - 1 or outer[t] != outer[t + 1]) for t in range(len(pairs))]
    tab = lambda v: jnp.asarray(np.array(v, np.int32))
    return tab([p[0] for p in pairs]), tab([p[1] for p in pairs]), tab(first), tab(last)


def _causal_scores(qv, kv, qi, ki, row0, tq, tk, masked):
    s = _dot_nt(qv, kv)
    if masked:
        row = lax.broadcasted_iota(jnp.int32, s.shape, 0) + (qi * tq + row0)
        col = lax.broadcasted_iota(jnp.int32, s.shape, 1) + ki * tk
        s = jnp.where(col <= row, s, NEG)
    return s


def _mla_attn(q, k, v, ride=None, tq=1024, tk=2048, rc=256):
    H, T, _ = q.shape
    tables = _causal_pairs(T, tq, tk, key_major=False)
    n_pairs = int(tables[0].shape[0])
    r_args, r_in, r_shape, r_out, r_scratch = _ride_parts(ride)

    def body(qt, kt, ft, lt, q_ref, k_ref, v_ref, o_ref, lse_ref, m_s, acc):
        t = pl.program_id(1)
        qi, ki = qt[t], kt[t]

        @pl.when(ft[t] == 1)
        def _():
            m_s[...] = jnp.full_like(m_s, NEG)
            acc[...] = jnp.zeros_like(acc)

        def update(masked):
            kk, vv = k_ref[...], v_ref[...]
            for c in range(tq // rc):
                rows = pl.ds(c * rc, rc)
                s = _causal_scores(q_ref[rows, :], kk, qi, ki, c * rc, tq, tk, masked)
                m_old = m_s[rows, :]
                m_new = jnp.maximum(m_old, jnp.max(s, axis=-1, keepdims=True))
                p = jnp.exp(s - m_new).astype(BF16)
                acc[rows, :] = jnp.exp(m_old - m_new) * acc[rows, :] + _dot(p, vv)
                m_s[rows, :] = m_new

        diagonal = (ki + 1) * tk - 1 > qi * tq

        @pl.when(diagonal)
        def _():
            update(True)

        @pl.when(jnp.logical_not(diagonal))
        def _():
            update(False)

        @pl.when(lt[t] == 1)
        def _():
            l = jnp.max(acc[:, MLA_V:], axis=-1, keepdims=True)
            o_ref[...] = acc[:, :MLA_V] / l
            lse_ref[...] = jnp.broadcast_to(m_s[...] + jnp.log(l), lse_ref.shape)

    qrow = lambda h, t, qt, kt, ft, lt: (h, qt[t], 0)
    krow = lambda h, t, qt, kt, ft, lt: (h, kt[t], 0)
    first = lambda: (pl.program_id(0) == 0) & (pl.program_id(1) == 0)
    last = lambda: (pl.program_id(0) == H - 1) & (pl.program_id(1) == n_pairs - 1)
    outs = pl.pallas_call(
        _riding(body, 7, 2, 2, ride, first, last), name="mla_attn",
        grid_spec=pltpu.PrefetchScalarGridSpec(
            num_scalar_prefetch=4, grid=(H, n_pairs),
            in_specs=[pl.BlockSpec((None, tq, MLA_PAD), qrow), pl.BlockSpec((None, tk, MLA_PAD), krow),
                      pl.BlockSpec((None, tk, 2 * MLA_V), krow)] + r_in,
            out_specs=[pl.BlockSpec((tq, MLA_V), lambda h, t, qt, kt, ft, lt: (qt[t], h)),
                       pl.BlockSpec((None, tq, 128), qrow)] + r_out,
            scratch_shapes=[pltpu.VMEM((tq, 1), F32), pltpu.VMEM((tq, 2 * MLA_V), F32)] + r_scratch),
        out_shape=[jax.ShapeDtypeStruct((T, H * MLA_V), F32), jax.ShapeDtypeStruct((H, T, 128), F32)] + r_shape,
        compiler_params=_params(2),
    )(*tables, q, k, v, *r_args)
    return outs[:2], outs[2:]


def _mla_attn_bwd(q, k, v, o, lse, do, ride=None, tq=1024, tk=1024, rc=512):
    H, T, _ = q.shape
    tables = _causal_pairs(T, tq, tk, key_major=True)
    n_pairs = int(tables[0].shape[0])
    r_args, r_in, r_shape, r_out, r_scratch = _ride_parts(ride)

    def body(qt, kt, ft, lt, q_ref, k_ref, v_ref, o_ref, lse_ref, do_ref, dq_ref, dk_ref, dv_ref, dk_s, dv_s):
        t = pl.program_id(1)
        qi, ki = qt[t], kt[t]

        @pl.when(t == 0)
        def _():
            dq_ref[...] = jnp.zeros_like(dq_ref)

        @pl.when(ft[t] == 1)
        def _():
            dk_s[...] = jnp.zeros_like(dk_s)
            dv_s[...] = jnp.zeros_like(dv_s)

        def update(masked):
            kk, vv = k_ref[...], v_ref[...]
            for c in range(tq // rc):
                rows = pl.ds(c * rc, rc)
                qv, dov = q_ref[rows, :], do_ref[rows, :]
                delta = jnp.sum(dov * o_ref[rows, :], axis=-1, keepdims=True)
                lse_v = jnp.max(lse_ref[rows, :], axis=-1, keepdims=True)
                p = jnp.exp(_causal_scores(qv, kk, qi, ki, c * rc, tq, tk, masked) - lse_v)
                dob = dov.astype(BF16)
                dv_s[...] += _dot_tn(p.astype(BF16), dob)
                ds = (p * (_dot_nt(dob, vv) - delta)).astype(BF16)
                dk_s[...] += _dot_tn(ds, qv)
                out_rows = pl.ds(pl.multiple_of(qi * tq + c * rc, rc), rc)
                dq_ref[out_rows, :] += _dot(ds, kk)

        diagonal = (ki + 1) * tk - 1 > qi * tq

        @pl.when(diagonal)
        def _():
            update(True)

        @pl.when(jnp.logical_not(diagonal))
        def _():
            update(False)

        @pl.when(lt[t] == 1)
        def _():
            dk_ref[...] = dk_s[...]
            dv_ref[...] = dv_s[...]

    qrow = lambda h, t, qt, kt, ft, lt: (h, qt[t], 0)
    krow = lambda h, t, qt, kt, ft, lt: (h, kt[t], 0)
    qcol = lambda h, t, qt, kt, ft, lt: (qt[t], h)
    first = lambda: (pl.program_id(0) == 0) & (pl.program_id(1) == 0)
    last = lambda: (pl.program_id(0) == H - 1) & (pl.program_id(1) == n_pairs - 1)
    outs = pl.pallas_call(
        _riding(body, 10, 3, 2, ride, first, last), name="mla_attn_bwd",
        grid_spec=pltpu.PrefetchScalarGridSpec(
            num_scalar_prefetch=4, grid=(H, n_pairs),
            in_specs=[pl.BlockSpec((None, tq, MLA_PAD), qrow), pl.BlockSpec((None, tk, MLA_PAD), krow),
                      pl.BlockSpec((None, tk, MLA_V), krow), pl.BlockSpec((tq, MLA_V), qcol),
                      pl.BlockSpec((None, tq, 128), qrow), pl.BlockSpec((tq, MLA_V), qcol)] + r_in,
            out_specs=[pl.BlockSpec((None, T, MLA_PAD), lambda h, t, qt, kt, ft, lt: (h, 0, 0)),
                       pl.BlockSpec((None, tk, MLA_PAD), krow), pl.BlockSpec((None, tk, MLA_V), krow)] + r_out,
            scratch_shapes=[pltpu.VMEM((tk, MLA_PAD), F32), pltpu.VMEM((tk, MLA_V), F32)] + r_scratch),
        out_shape=[jax.ShapeDtypeStruct((H, T, MLA_PAD), F32), jax.ShapeDtypeStruct((H, T, MLA_PAD), F32),
                   jax.ShapeDtypeStruct((H, T, MLA_V), F32)] + r_shape,
        compiler_params=_params(2),
    )(*tables, q, k, v, o, lse, do, *r_args)
    return outs[:3], outs[3:]


def _pair_gain(g):
    return jnp.tile(g.reshape(1, DIL_HD), (1, 2))


def _pad_gain(g):
    return jnp.pad(g.reshape(1, MLA_QK), ((0, 0), (0, MLA_PAD - MLA_QK)))


def _local_step(x, target, s, comm):
    T = x.shape[0]
    w = comm.w
    gq, gk = _pair_gain(s["dil_q_norm"]) * DIL_HD ** -0.5, _pair_gain(s["dil_k_norm"])
    g_q, g_k = _pad_gain(s["mla_q_norm"]), _pad_gain(s["mla_k_norm"])
    cos, sin = _rope_tables(T)
    onehots = _bucket_onehots()
    biases = _bias_tables(s["rel_bias"], onehots)

    (x1, h1, gate1, up1), got = _ffn_fwd(x, s["ffn1_norm"], w["ffn1_w_gate"], w["ffn1_w_up"], w["ffn1_w_down"],
                                         ride=comm.gather(_GROUPS["attn"]))
    comm.weights_landed(_GROUPS["attn"], got)
    hm, proj, qh, kh = _in_proj(x1, s["mix_norm"], w["w_in"], gq, gk)
    dil = None
    for (_, d), bias in zip(DIL_BRANCHES, biases):
        dil = _dil_fwd(qh, kh, proj, bias, d, dil)
    o_dil, lse_dil = dil
    q, k, v = _mla_prep(proj, cos, sin, s["mla_q_a_norm"], s["mla_kv_a_norm"], g_q, g_k, w["mla_w_q_b"], w["mla_w_kv_b"])
    (o_mla, lse_mla), got = _mla_attn(q, k, v, ride=comm.gather(_GROUPS["ffn2"]))
    comm.weights_landed(_GROUPS["ffn2"], got)
    x2, oc = _out_proj(x1, o_dil, o_mla, s["out_norm_dil"], s["out_norm_mla"], w["w_out"])
    (dy, h2, gate2, up2, loss), _ = _ffn_fwd(x2, s["ffn2_norm"], w["ffn2_w_gate"], w["ffn2_w_up"], w["ffn2_w_down"],
                                             target=target)

    gw, gs = {}, {}

    def ffn_grads(name, dy_in, x_in, h, gate, up, early=None):
        (dx, a, dg, du, dyh, dgain), _ = _ffn_bwd(dy_in, x_in, s[name + "_norm"], gate, up,
                                                  w[name + "_w_gate"], w[name + "_w_up"], w[name + "_w_down"])
        gs[name + "_norm"] = dgain
        down, gate_n, up_n = (name + "_w_down",), (name + "_w_gate",), (name + "_w_up",)
        ride = lambda names: comm.scatter(names, gw) if early is not None else None
        gw[down[0]], landed = _matmul_tn(a, dyh, 1408, 1024, ride=ride(early))
        comm.grads_landed(early or (), landed)
        gw[gate_n[0]], landed = _matmul_tn(dg, h, 1408, 1024, ride=ride(down))
        comm.grads_landed(down, landed)
        gw[up_n[0]], landed = _matmul_tn(du, h, 1408, 1024, ride=ride(gate_n))
        comm.grads_landed(gate_n, landed)
        return dx

    dx2 = ffn_grads("ffn2", dy, x2, h2, gate2, up2)
    gw["w_out"], _ = _matmul_tn(oc, dx2, 1024, 1024)
    do_dil, do_mla, gs["out_norm_dil"], gs["out_norm_mla"] = _out_proj_bwd(
        dx2, o_dil, o_mla, s["out_norm_dil"], s["out_norm_mla"], w["w_out"])

    (dq, dk, dv), got = _mla_attn_bwd(q, k, v, o_mla, lse_mla, do_mla, ride=comm.scatter(_GROUPS["ffn2"], gw))
    comm.grads_landed(_GROUPS["ffn2"], got)
    (dcq, dckv, dkpe, cqn, ckvn, dqp, dkvp, gs["mla_q_a_norm"], gs["mla_kv_a_norm"], dg_q, dg_k) = _mla_prep_bwd(
        proj, cos, sin, s["mla_q_a_norm"], s["mla_kv_a_norm"], g_q, g_k, w["mla_w_q_b"], w["mla_w_kv_b"], dq, dk, dv)
    gs["mla_q_norm"], gs["mla_k_norm"] = dg_q[:, :MLA_QK], dg_k[:, :MLA_QK]
    gw["mla_w_q_b"], _ = _matmul_tn(dqp, cqn, 1024, 256)
    gw["mla_w_kv_b"], _ = _matmul_tn(ckvn, dkvp, 128, 1024)

    dqkv, dbs = [], []
    for (_, d), bias in zip(DIL_BRANCHES, biases):
        triple, db = _dil_bwd(qh, kh, proj, o_dil, lse_dil, do_dil, bias, d)
        dqkv.append(triple)
        dbs.append(db)
    gs["rel_bias"] = _bias_grad(dbs, onehots)

    ready = tuple(n for n in _GROUPS["attn"] if n != "w_in")
    (dx1, dproj, gs["mix_norm"], dgq, dgk), got = _in_proj_bwd(dx2, x1, s["mix_norm"], w["w_in"], proj, gq, gk,
                                                               dqkv, dcq, dckv, dkpe, ride=comm.scatter(ready, gw))
    comm.grads_landed(ready, got)
    gs["dil_q_norm"] = (dgq[:, :DIL_HD] + dgq[:, DIL_HD:]) * DIL_HD ** -0.5
    gs["dil_k_norm"] = dgk[:, :DIL_HD] + dgk[:, DIL_HD:]
    gw["w_in"], _ = _matmul_tn(dproj, hm, 1024, 1024)
    grad_x = ffn_grads("ffn1", dx1, x, h1, gate1, up1, early=("w_in",))
    return loss, grad_x, gw, gs


def _position():
    x, y, c = lax.axis_index("x"), lax.axis_index("y"), lax.axis_index("c")
    return x, y, c, 4 * x + 2 * y + c


def _peer(x, y, c, k):
    px = 1 - x if k & 4 else x
    py = 1 - y if k & 2 else y
    pc = 1 - c if k & 1 else c
    return (px, py, pc), 4 * px + 2 * py + pc


class _Ride:
    def __init__(self, arrays, scatter):
        self.arrays, self.scatter = list(arrays), list(scatter)
        self.n = n = len(self.arrays)
        self.specs = [pl.BlockSpec(memory_space=pl.ANY)] * n
        self.out_shape = [jax.ShapeDtypeStruct(a.shape if sc else (N_DEV,) + a.shape, a.dtype)
                          for a, sc in zip(self.arrays, self.scatter)]
        self.scratch = [pltpu.SemaphoreType.DMA((n, N_DEV - 1)), pltpu.SemaphoreType.DMA((n, N_DEV - 1)),
                        pltpu.SemaphoreType.DMA((n,))]

    def _copies(self, ins, outs, sems):
        send_sems, recv_sems, local_sems = sems
        x, y, c, me = _position()
        copies = []
        for a in range(self.n):
            src = ins[a].at[me] if self.scatter[a] else ins[a]
            copies.append(pltpu.make_async_copy(src, outs[a].at[me], local_sems.at[a]))
        for k in range(1, N_DEV):
            peer, peer_idx = _peer(x, y, c, k)
            for a in range(self.n):
                src = ins[a].at[peer_idx] if self.scatter[a] else ins[a]
                copies.append(pltpu.make_async_remote_copy(
                    src_ref=src, dst_ref=outs[a].at[me], send_sem=send_sems.at[a, k - 1], recv_sem=recv_sems.at[a, k - 1],
                    device_id=peer, device_id_type=pl.DeviceIdType.MESH))
        return copies

    def start(self, ins, outs, sems):
        for cp in self._copies(ins, outs, sems):
            cp.start()

    def wait(self, ins, outs, sems):
        for cp in self._copies(ins, outs, sems):
            cp.wait()


def _ride_parts(ride):
    if ride is None:
        return [], [], [], [], []
    return ride.arrays, ride.specs, ride.out_shape, ride.specs, ride.scratch


def _riding(body, n_in, n_out, n_scratch, ride, first, last):
    if ride is None:
        return body
    n = ride.n
    i1, i2 = n_in + n, n_in + n + n_out
    i3, i4 = i2 + n, i2 + n + n_scratch

    def wrapped(*refs):
        ins, outs, sems = refs[n_in:i1], refs[i2:i3], refs[i4:]

        @pl.when(first())
        def _():
            ride.start(ins, outs, sems)

        body(*refs[:n_in], *refs[i1:i2], *refs[i3:i4])

        @pl.when(last())
        def _():
            ride.wait(ins, outs, sems)

    return wrapped


def _gather_two_level(arrays, name):
    n = len(arrays)
    out_shape = [jax.ShapeDtypeStruct((N_DEV,) + a.shape, a.dtype) for a in arrays]

    def body(*refs):
        ins, outs = refs[:n], refs[n:2 * n]
        send_sems, recv_sems, local_sems = refs[2 * n:]
        x, y, c, me = _position()
        sibling = (x, y, 1 - c)
        chips = [(1 - x, y), (x, 1 - y), (1 - x, 1 - y)]
        block = lambda px, py, pc: 4 * px + 2 * py + pc

        def copy(a, k, blk, to, src=None):
            dst = outs[a].at[blk]
            return pltpu.make_async_remote_copy(
                src_ref=dst if src is None else src, dst_ref=dst, send_sem=send_sems.at[a, k], recv_sem=recv_sems.at[a, k],
                device_id=to, device_id_type=pl.DeviceIdType.MESH)

        local = [pltpu.make_async_copy(ins[a], outs[a].at[me], local_sems.at[a]) for a in range(n)]
        first = []
        for a in range(n):
            first.append(copy(a, 0, me, sibling, src=ins[a]))
            first += [copy(a, 1 + j, me, (*chip, c), src=ins[a]) for j, chip in enumerate(chips)]
        for cp in local + first:
            cp.start()
        passed = []
        for j, chip in enumerate(chips):
            for a in range(n):
                copy(a, 1 + j, block(*chip, c), sibling).wait_recv()
                passed.append(copy(a, 4 + j, block(*chip, c), sibling))
                passed[-1].start()
        for a in range(n):
            copy(a, 0, block(x, y, 1 - c), sibling).wait_recv()
            for j, chip in enumerate(chips):
                copy(a, 4 + j, block(*chip, 1 - c), sibling).wait_recv()
        for cp in first + passed:
            cp.wait_send()
        for cp in local:
            cp.wait()

    any_spec = [pl.BlockSpec(memory_space=pl.ANY)] * n
    return pl.pallas_call(
        body, name=name, in_specs=any_spec, out_specs=any_spec, out_shape=out_shape,
        scratch_shapes=[pltpu.SemaphoreType.DMA((n, N_DEV - 1)), pltpu.SemaphoreType.DMA((n, N_DEV - 1)),
                        pltpu.SemaphoreType.DMA((n,))],
    )(*arrays)


def _exchange(ride, name):
    def body(*refs):
        parts = refs[:ride.n], refs[ride.n:2 * ride.n], refs[2 * ride.n:]
        ride.start(*parts)
        ride.wait(*parts)

    return pl.pallas_call(body, name=name, in_specs=ride.specs, out_specs=ride.specs, out_shape=ride.out_shape,
                          scratch_shapes=ride.scratch)(*ride.arrays)


def _adamw_math(wv, g, m, v):
    m = ADAM_B1 * m + (1.0 - ADAM_B1) * g
    v = ADAM_B2 * v + (1.0 - ADAM_B2) * (g * g)
    m_hat = m / (1.0 - ADAM_B1 ** ADAM_STEP)
    v_hat = v / (1.0 - ADAM_B2 ** ADAM_STEP)
    delta = -ADAM_LR * (m_hat / (jnp.sqrt(v_hat) + ADAM_EPS) + ADAM_WD * wv)
    return delta, m, v


def _adamw(parts, wv, m, v):
    _, R, C = wv.shape
    tr = max([t for t in range(16, 257, 16) if R % t == 0] or [R])

    def body(p_ref, w_ref, m_ref, v_ref, g_ref, d_ref, mo_ref, vo_ref):
        g = p_ref[0].astype(F32)
        for j in range(1, N_DEV):
            g = g + p_ref[j].astype(F32)
        d, mn, vn = _adamw_math(w_ref[0], g, m_ref[0], v_ref[0])
        g_ref[0] = g
        d_ref[0] = d
        mo_ref[0] = mn
        vo_ref[0] = vn

    blk = pl.BlockSpec((1, tr, C), lambda i: (0, i, 0))
    out = jax.ShapeDtypeStruct((1, R, C), F32)
    return pl.pallas_call(
        body, name="adamw", grid=(R // tr,),
        in_specs=[pl.BlockSpec((N_DEV, tr, C), lambda i: (0, i, 0)), blk, blk, blk],
        out_specs=[blk] * 4, out_shape=[out] * 4,
        compiler_params=_params(1),
    )(parts, wv, m, v)


_TRANSPOSED = ("ffn1_w_gate", "ffn1_w_up", "ffn2_w_gate", "ffn2_w_up", "w_in", "mla_w_q_b")
_GROUPS = {"ffn1": ("ffn1_w_gate", "ffn1_w_up", "ffn1_w_down"),
           "ffn2": ("ffn2_w_gate", "ffn2_w_up", "ffn2_w_down"),
           "attn": ("w_in", "mla_w_q_b", "mla_w_kv_b", "w_out")}
_SMALL = ("ffn1_norm", "mix_norm", "ffn2_norm", "out_norm_dil", "out_norm_mla", "mla_q_a_norm", "rel_bias",
          "mla_q_norm", "mla_k_norm", "mla_kv_a_norm", "dil_q_norm", "dil_k_norm")
_SMALL_ROWS = 48


def _cols_to_full(g):
    return g.transpose(1, 0, 2).reshape(g.shape[1], N_DEV * g.shape[2])


def _full_to_cols(f):
    return f.reshape(f.shape[0], N_DEV, f.shape[1] // N_DEV).transpose(1, 0, 2)


def _shard_view(name, a):
    return jnp.swapaxes(a, 1, 2) if name in _TRANSPOSED else a


def _to_full(name, g):
    if name == "mla_w_kv_b":
        return _cols_to_full(g)
    f = g.reshape(-1, g.shape[-1])
    if name == "w_in":
        f = jnp.pad(f, ((0, PROJ_PAD - PROJ_COLS), (0, 0)))
    if name == "mla_w_q_b":
        f = jnp.pad(f.reshape(MLA_HEADS, MLA_QK, -1), ((0, 0), (0, MLA_PAD - MLA_QK), (0, 0)))
        f = f.reshape(MLA_HEADS * MLA_PAD, -1)
    return f


def _to_parts(name, f):
    if name == "mla_w_kv_b":
        return _full_to_cols(f).astype(BF16)
    if name == "w_in":
        f = f[:PROJ_COLS]
    if name == "mla_w_q_b":
        f = f.reshape(MLA_HEADS, MLA_PAD, -1)[:, :MLA_QK].reshape(MLA_HEADS * MLA_QK, -1)
    return f.reshape(N_DEV, -1, f.shape[-1]).astype(BF16)


class _Comm:
    def __init__(self, shards):
        self.shards, self.w, self.recv = shards, {}, {}

    def gather(self, names):
        return _Ride([self.shards[n] for n in names], [False] * len(names))

    def scatter(self, names, grads):
        return _Ride([_to_parts(n, grads[n]) for n in names], [True] * len(names))

    def weights_landed(self, names, got):
        self.w.update({n: _to_full(n, g) for n, g in zip(names, got)})

    def grads_landed(self, names, got):
        self.recv.update(zip(names, got))


def _pack_small(parts, extra):
    flat = jnp.concatenate([parts[n].reshape(-1) for n in _SMALL] + [extra.reshape(-1)])
    return jnp.pad(flat, (0, _SMALL_ROWS * 128 - flat.shape[0])).reshape(_SMALL_ROWS, 128)


def _unpack_small(packed, shapes):
    flat, out, off = packed.reshape(-1), {}, 0
    for n in _SMALL:
        size = math.prod(shapes[n])
        out[n] = flat[off:off + size].reshape(shapes[n])
        off += size
    return out, flat[off]


_NAMES = ("ffn1_norm", "ffn1_w_gate", "ffn1_w_up", "ffn1_w_down", "mix_norm", "w_in", "dil_q_norm", "dil_k_norm",
          "rel_bias", "mla_q_a_norm", "mla_w_q_b", "mla_kv_a_norm", "mla_w_kv_b", "mla_q_norm", "mla_k_norm",
          "out_norm_dil", "out_norm_mla", "w_out", "ffn2_norm", "ffn2_w_gate", "ffn2_w_up", "ffn2_w_down")


def kernel(x, ffn1_norm, ffn1_w_gate, ffn1_w_up, ffn1_w_down, mix_norm, w_in, dil_q_norm, dil_k_norm, rel_bias, mla_q_a_norm, mla_w_q_b, mla_kv_a_norm, mla_w_kv_b, mla_q_norm, mla_k_norm, out_norm_dil, out_norm_mla, w_out, ffn2_norm, ffn2_w_gate, ffn2_w_up, ffn2_w_down, loss_target, m_ffn1_norm, m_ffn1_w_gate, m_ffn1_w_up, m_ffn1_w_down, m_mix_norm, m_w_in, m_dil_q_norm, m_dil_k_norm, m_rel_bias, m_mla_q_a_norm, m_mla_w_q_b, m_mla_kv_a_norm, m_mla_w_kv_b, m_mla_q_norm, m_mla_k_norm, m_out_norm_dil, m_out_norm_mla, m_w_out, m_ffn2_norm, m_ffn2_w_gate, m_ffn2_w_up, m_ffn2_w_down, v_ffn1_norm, v_ffn1_w_gate, v_ffn1_w_up, v_ffn1_w_down, v_mix_norm, v_w_in, v_dil_q_norm, v_dil_k_norm, v_rel_bias, v_mla_q_a_norm, v_mla_w_q_b, v_mla_kv_a_norm, v_mla_w_kv_b, v_mla_q_norm, v_mla_k_norm, v_out_norm_dil, v_out_norm_mla, v_w_out, v_ffn2_norm, v_ffn2_w_gate, v_ffn2_w_up, v_ffn2_w_down):
    args = locals()
    wts = {n: args[n] for n in _NAMES}
    mom = {n: args["m_" + n] for n in _NAMES}
    var = {n: args["v_" + n] for n in _NAMES}

    matrices = [n for group in _GROUPS.values() for n in group]
    comm = _Comm({n: _shard_view(n, wts[n])[0].astype(BF16) for n in matrices})
    comm.weights_landed(_GROUPS["ffn1"], _gather_two_level(comm.gather(_GROUPS["ffn1"]).arrays, "gather_first"))
    small = {n: wts[n].reshape(1, -1) if n != "rel_bias" else wts[n] for n in _SMALL}

    loss, grad_x, gw, gs = _local_step(x[0], loss_target[0], small, comm)

    last = comm.scatter(("ffn1_w_up",), gw)
    got = _exchange(_Ride(last.arrays + [_pack_small(gs, loss[0, 0])], last.scatter + [False]), "scatter_last")
    comm.grads_landed(("ffn1_w_up",), got[:-1])

    res = {n: [_shard_view(n, r) for r in _adamw(comm.recv[n], *(_shard_view(n, a[n]) for a in (wts, mom, var)))]
           for n in matrices}
    shapes = {n: wts[n].shape for n in _SMALL}
    zero = jnp.zeros((), F32)
    packed = _adamw(got[-1], _pack_small(wts, zero)[None], _pack_small(mom, zero)[None], _pack_small(var, zero)[None])
    loss_total = None
    for slot, q in enumerate(packed):
        vals, extra = _unpack_small(q, shapes)
        if slot == 0:
            loss_total = extra
        for n in _SMALL:
            res.setdefault(n, [None] * 4)[slot] = vals[n]
    outs = [loss_total, grad_x[None]]
    for slot in range(4):
        outs += [res[n][slot].reshape(wts[n].shape) for n in _NAMES]
    return tuple(outs)
```

```python
import math

import numpy as np
import jax
import jax.numpy as jnp
from jax import lax
from jax.experimental import pallas as pl
from jax.experimental.pallas import tpu as pltpu

F32, BF16 = jnp.float32, jnp.bfloat16
EPS = 1e-6
NEG = -1e30
N_DEV = 8

DIL_HEADS, DIL_HD = 8, 64
DIL_WIDTH = DIL_HEADS * DIL_HD
DIL_BRANCHES = ((128, 1), (512, 4), (2048, 16))
DIL_BLOCK = 128
MLA_HEADS, MLA_NOPE, MLA_ROPE, MLA_V = 4, 128, 64, 128
MLA_QK = MLA_NOPE + MLA_ROPE
MLA_PAD = 256
ROPE_BASE = 10000.0
REL_BUCKETS, REL_MAX_DIST = 32, 2048
PROJ_COLS, PROJ_PAD = 1984, 2048
FFN_RESID = 0.5
ADAM_LR, ADAM_B1, ADAM_B2, ADAM_EPS, ADAM_WD, ADAM_STEP = 0.001, 0.9, 0.999, 1e-08, 0.01, 10
VMEM_LIMIT = 62 * 1024 * 1024

_NT = (((1,), (1,)), ((), ()))
_TN = (((0,), (0,)), ((), ()))


def _dot(a, b):
    return jnp.dot(a, b, preferred_element_type=F32)


def _dot_nt(a, b):
    return lax.dot_general(a, b, _NT, preferred_element_type=F32)


def _dot_tn(a, b):
    return lax.dot_general(a, b, _TN, preferred_element_type=F32)


def _params(n_axes):
    return pltpu.CompilerParams(dimension_semantics=("arbitrary",) * n_axes, vmem_limit_bytes=VMEM_LIMIT)


def _rstd(x, n=None):
    n = x.shape[-1] if n is None else n
    return lax.rsqrt(jnp.sum(x * x, axis=-1, keepdims=True) / n + EPS)


def _rms_bwd(dy, x, g, r, n=None):
    n = x.shape[-1] if n is None else n
    u = dy * g
    dx = r * u - x * (r * r * r) * (jnp.sum(u * x, axis=-1, keepdims=True) / n)
    return dx, dy * x * r


def _sigmoid(x):
    return 1.0 / (1.0 + jnp.exp(-x))


def _split3(x):
    parts = []
    for _ in range(3):
        xb = x.astype(BF16)
        parts.append(xb)
        x = x - xb.astype(F32)
    return parts


def _ffn_fwd(x, gain, wg, wu, wd, ride=None, target=None, tm=512, tf=2816):
    T, D = x.shape
    F = wg.shape[0]
    ni, nj = T // tm, F // tf
    with_loss = target is not None
    r_args, r_in, r_shape, r_out, r_scratch = _ride_parts(ride)

    def body(*refs):
        x_ref, g_ref, wg_ref, wu_ref, wd_ref = refs[:5]
        t_ref = refs[5] if with_loss else None
        xo_ref, h_ref, gate_ref, up_ref = refs[5 + with_loss:9 + with_loss]
        loss_ref = refs[-2] if with_loss else None
        acc = refs[-1]
        i, j = pl.program_id(0), pl.program_id(1)

        @pl.when(j == 0)
        def _():
            xv = x_ref[...]
            h_ref[...] = (xv * _rstd(xv) * g_ref[...]).astype(BF16)
            acc[...] = jnp.zeros_like(acc)

        h = h_ref[...]
        g = _dot_nt(h, wg_ref[...])
        u = _dot_nt(h, wu_ref[...])
        gate_ref[...] = g.astype(BF16)
        up_ref[...] = u.astype(BF16)
        a = (g * _sigmoid(g) * u).astype(BF16)
        acc[...] += _dot(a, wd_ref[...])

        @pl.when(j == nj - 1)
        def _():
            y = x_ref[...] + FFN_RESID * acc[...]
            if with_loss:
                @pl.when(i == 0)
                def _():
                    loss_ref[...] = jnp.zeros_like(loss_ref)

                e = y - t_ref[...]
                xo_ref[...] = e * (1.0 / D)
                loss_ref[...] += (0.5 / D) * jnp.sum(e * e)
            else:
                xo_ref[...] = y

    row = lambda i, j: (i, 0)
    tile = lambda i, j: (i, j)
    n_in, n_out = 5 + with_loss, 4 + with_loss
    first = lambda: (pl.program_id(0) == 0) & (pl.program_id(1) == 0)
    last = lambda: (pl.program_id(0) == ni - 1) & (pl.program_id(1) == nj - 1)
    outs = pl.pallas_call(
        _riding(body, n_in, n_out, 1, ride, first, last), name="ffn_fwd", grid=(ni, nj),
        in_specs=[pl.BlockSpec((tm, D), row), pl.BlockSpec((1, D), lambda i, j: (0, 0)),
                  pl.BlockSpec((tf, D), lambda i, j: (j, 0)), pl.BlockSpec((tf, D), lambda i, j: (j, 0)),
                  pl.BlockSpec((tf, D), lambda i, j: (j, 0))] + [pl.BlockSpec((tm, D), row)] * with_loss + r_in,
        out_specs=[pl.BlockSpec((tm, D), row), pl.BlockSpec((tm, D), row), pl.BlockSpec((tm, tf), tile),
                   pl.BlockSpec((tm, tf), tile)] + [pl.BlockSpec((1, 128), lambda i, j: (0, 0))] * with_loss + r_out,
        out_shape=[jax.ShapeDtypeStruct((T, D), F32), jax.ShapeDtypeStruct((T, D), BF16),
                   jax.ShapeDtypeStruct((T, F), BF16), jax.ShapeDtypeStruct((T, F), BF16)]
        + [jax.ShapeDtypeStruct((1, 128), F32)] * with_loss + r_shape,
        scratch_shapes=[pltpu.VMEM((tm, D), F32)] + r_scratch,
        compiler_params=_params(2),
    )(x, gain, wg, wu, wd, *([target] if with_loss else []), *r_args)
    return outs[:n_out], outs[n_out:]


def _ffn_bwd(dy, x, gain, gate, up, wg, wu, wd, ride=None, tm=256, tf=2816):
    T, D = x.shape
    F = wg.shape[0]
    ni, nj = T // tm, F // tf
    r_args, r_in, r_shape, r_out, r_scratch = _ride_parts(ride)

    def body(dy_ref, x_ref, g_ref, gate_ref, up_ref, wg_ref, wu_ref, wd_ref,
             dx_ref, a_ref, dg_ref, du_ref, dyh_ref, dgain_ref, acc):
        i, j = pl.program_id(0), pl.program_id(1)

        @pl.when((i == 0) & (j == 0))
        def _():
            dgain_ref[...] = jnp.zeros_like(dgain_ref)

        @pl.when(j == 0)
        def _():
            dyh_ref[...] = (FFN_RESID * dy_ref[...]).astype(BF16)
            acc[...] = jnp.zeros_like(acc)

        da = _dot_nt(dyh_ref[...], wd_ref[...])
        g = gate_ref[...].astype(F32)
        u = up_ref[...].astype(F32)
        sig = _sigmoid(g)
        s = g * sig
        a_ref[...] = (s * u).astype(BF16)
        dg = (da * u * (sig * (1.0 + g * (1.0 - sig)))).astype(BF16)
        du = (da * s).astype(BF16)
        dg_ref[...] = dg
        du_ref[...] = du
        acc[...] += _dot(dg, wg_ref[...]) + _dot(du, wu_ref[...])

        @pl.when(j == nj - 1)
        def _():
            xv = x_ref[...]
            dxn, dgc = _rms_bwd(acc[...], xv, g_ref[...], _rstd(xv))
            dx_ref[...] = dy_ref[...] + dxn
            dgain_ref[...] += jnp.sum(dgc, axis=0, keepdims=True)

    first = lambda: (pl.program_id(0) == 0) & (pl.program_id(1) == 0)
    last = lambda: (pl.program_id(0) == ni - 1) & (pl.program_id(1) == nj - 1)
    outs = pl.pallas_call(
        _riding(body, 8, 6, 1, ride, first, last), name="ffn_bwd", grid=(ni, nj),
        in_specs=[pl.BlockSpec((tm, D), lambda i, j: (i, 0)), pl.BlockSpec((tm, D), lambda i, j: (i, 0)),
                  pl.BlockSpec((1, D), lambda i, j: (0, 0)),
                  pl.BlockSpec((tm, tf), lambda i, j: (i, j)), pl.BlockSpec((tm, tf), lambda i, j: (i, j)),
                  pl.BlockSpec((tf, D), lambda i, j: (j, 0)), pl.BlockSpec((tf, D), lambda i, j: (j, 0)),
                  pl.BlockSpec((tf, D), lambda i, j: (j, 0))] + r_in,
        out_specs=[pl.BlockSpec((tm, D), lambda i, j: (i, 0)),
                   pl.BlockSpec((tm, tf), lambda i, j: (i, j)), pl.BlockSpec((tm, tf), lambda i, j: (i, j)),
                   pl.BlockSpec((tm, tf), lambda i, j: (i, j)),
                   pl.BlockSpec((tm, D), lambda i, j: (i, 0)), pl.BlockSpec((1, D), lambda i, j: (0, 0))] + r_out,
        out_shape=[jax.ShapeDtypeStruct((T, D), F32), jax.ShapeDtypeStruct((T, F), BF16),
                   jax.ShapeDtypeStruct((T, F), BF16), jax.ShapeDtypeStruct((T, F), BF16),
                   jax.ShapeDtypeStruct((T, D), BF16), jax.ShapeDtypeStruct((1, D), F32)] + r_shape,
        scratch_shapes=[pltpu.VMEM((tm, D), F32)] + r_scratch,
        compiler_params=_params(2),
    )(dy, x, gain, gate, up, wg, wu, wd, *r_args)
    return outs[:6], outs[6:]


def _matmul_tn(a, b, tk, tn, ride=None, tt=2048):
    T, K = a.shape
    N = b.shape[1]
    tk, tn = min(tk, K), min(tn, N)
    grid = (K // tk, N // tn, T // tt)
    r_args, r_in, r_shape, r_out, r_scratch = _ride_parts(ride)

    def body(a_ref, b_ref, o_ref, acc):
        t = pl.program_id(2)

        @pl.when(t == 0)
        def _():
            acc[...] = jnp.zeros_like(acc)

        acc[...] += _dot_tn(a_ref[...].astype(BF16), b_ref[...].astype(BF16))

        @pl.when(t == grid[2] - 1)
        def _():
            o_ref[...] = acc[...].astype(BF16)

    first = lambda: (pl.program_id(0) == 0) & (pl.program_id(1) == 0) & (pl.program_id(2) == 0)
    last = lambda: ((pl.program_id(0) == grid[0] - 1) & (pl.program_id(1) == grid[1] - 1)
                    & (pl.program_id(2) == grid[2] - 1))
    outs = pl.pallas_call(
        _riding(body, 2, 1, 1, ride, first, last), name="matmul_tn", grid=grid,
        in_specs=[pl.BlockSpec((tt, tk), lambda k, n, t: (t, k)), pl.BlockSpec((tt, tn), lambda k, n, t: (t, n))] + r_in,
        out_specs=[pl.BlockSpec((tk, tn), lambda k, n, t: (k, n))] + r_out,
        out_shape=[jax.ShapeDtypeStruct((K, N), BF16)] + r_shape,
        scratch_shapes=[pltpu.VMEM((tk, tn), F32)] + r_scratch,
        compiler_params=_params(3),
    )(a, b, *r_args)
    return outs[0], outs[1:]


def _in_proj(x, gain, w, gq, gk, tm=512):
    T, D = x.shape
    N = w.shape[0]
    W = DIL_WIDTH

    def body(x_ref, g_ref, w_ref, gq_ref, gk_ref, h_ref, p_ref, qh_ref, kh_ref):
        xv = x_ref[...]
        h = (xv * _rstd(xv) * g_ref[...]).astype(BF16)
        h_ref[...] = h
        p_ref[...] = _dot_nt(h, w_ref[...])
        lo = lax.broadcasted_iota(jnp.int32, (tm, 128), 1) < DIL_HD
        for hp in range(DIL_HEADS // 2):
            q = p_ref[:, 128 * hp:128 * (hp + 1)]
            k = p_ref[:, W + 128 * hp:W + 128 * (hp + 1)]
            qh_ref[:, 128 * hp:128 * (hp + 1)] = (q * _pair_rstd(q, lo) * gq_ref[...]).astype(BF16).astype(F32)
            kh_ref[:, 128 * hp:128 * (hp + 1)] = (k * _pair_rstd(k, lo) * gk_ref[...]).astype(BF16).astype(F32)

    row = lambda i: (i, 0)
    fix = lambda i: (0, 0)
    return pl.pallas_call(
        body, name="in_proj", grid=(T // tm,),
        in_specs=[pl.BlockSpec((tm, D), row), pl.BlockSpec((1, D), fix), pl.BlockSpec((N, D), fix),
                  pl.BlockSpec((1, 128), fix), pl.BlockSpec((1, 128), fix)],
        out_specs=[pl.BlockSpec((tm, D), row), pl.BlockSpec((tm, N), row), pl.BlockSpec((tm, W), row),
                   pl.BlockSpec((tm, W), row)],
        out_shape=[jax.ShapeDtypeStruct((T, D), BF16), jax.ShapeDtypeStruct((T, N), F32),
                   jax.ShapeDtypeStruct((T, W), F32), jax.ShapeDtypeStruct((T, W), F32)],
        compiler_params=_params(1),
    )(x, gain, w, gq, gk)


def _in_proj_bwd(dx_up, x, gain, w, proj, gq, gk, dqkv, dcq, dckv, dkpe, ride=None, tm=512):
    T, D = x.shape
    N = w.shape[0]
    W = DIL_WIDTH
    nb = len(dqkv)

    def body(*refs):
        dxu_ref, x_ref, g_ref, w_ref, q_ref, k_ref, gq_ref, gk_ref = refs[:8]
        dil_refs = refs[8:8 + 3 * nb]
        dcq_ref, dckv_ref, dkpe_ref, dx_ref, dp_ref, dgain_ref, dgq_ref, dgk_ref = refs[8 + 3 * nb:]

        @pl.when(pl.program_id(0) == 0)
        def _():
            for ref in (dgain_ref, dgq_ref, dgk_ref):
                ref[...] = jnp.zeros_like(ref)

        lo = lax.broadcasted_iota(jnp.int32, (tm, 128), 1) < DIL_HD
        norms = ((q_ref, gq_ref, dgq_ref), (k_ref, gk_ref, dgk_ref))
        for part in range(3):
            acc = dil_refs[part][...]
            for b in range(1, nb):
                acc = acc + dil_refs[3 * b + part][...]
            if part == 2:
                dp_ref[:, 2 * W:3 * W] = acc.astype(BF16)
                continue
            raw_ref, gn_ref, dgn_ref = norms[part]
            for hp in range(DIL_HEADS // 2):
                raw = raw_ref[:, 128 * hp:128 * (hp + 1)]
                d_raw, dgn = _pair_rms_bwd(acc[:, 128 * hp:128 * (hp + 1)], raw, _pair_rstd(raw, lo), gn_ref[...], lo)
                dp_ref[:, part * W + 128 * hp:part * W + 128 * (hp + 1)] = d_raw.astype(BF16)
                dgn_ref[...] += dgn
        dp_ref[:, 3 * W:3 * W + 256] = dcq_ref[...].astype(BF16)
        dp_ref[:, 3 * W + 256:3 * W + 384] = dckv_ref[...].astype(BF16)
        dp_ref[:, 3 * W + 384:N] = dkpe_ref[...].astype(BF16)
        dh = _dot(dp_ref[...], w_ref[...])
        xv = x_ref[...]
        dxn, dgc = _rms_bwd(dh, xv, g_ref[...], _rstd(xv))
        dx_ref[...] = dxu_ref[...] + dxn
        dgain_ref[...] += jnp.sum(dgc, axis=0, keepdims=True)

    row = lambda i: (i, 0)
    fix = lambda i: (0, 0)
    r_args, r_in, r_shape, r_out, r_scratch = _ride_parts(ride)
    first = lambda: pl.program_id(0) == 0
    last = lambda: pl.program_id(0) == T // tm - 1
    outs = pl.pallas_call(
        _riding(body, 11 + 3 * nb, 5, 0, ride, first, last), name="in_proj_bwd", grid=(T // tm,),
        in_specs=[pl.BlockSpec((tm, D), row), pl.BlockSpec((tm, D), row), pl.BlockSpec((1, D), fix),
                  pl.BlockSpec((N, D), fix), pl.BlockSpec((tm, W), row), pl.BlockSpec((tm, W), lambda i: (i, 1)),
                  pl.BlockSpec((1, 128), fix), pl.BlockSpec((1, 128), fix)] + [pl.BlockSpec((tm, W), row)] * (3 * nb)
                 + [pl.BlockSpec((tm, 256), row), pl.BlockSpec((tm, 128), row), pl.BlockSpec((tm, 128), row)] + r_in,
        out_specs=[pl.BlockSpec((tm, D), row), pl.BlockSpec((tm, N), row), pl.BlockSpec((1, D), fix),
                   pl.BlockSpec((1, 128), fix), pl.BlockSpec((1, 128), fix)] + r_out,
        out_shape=[jax.ShapeDtypeStruct((T, D), F32), jax.ShapeDtypeStruct((T, N), BF16),
                   jax.ShapeDtypeStruct((1, D), F32), jax.ShapeDtypeStruct((1, 128), F32),
                   jax.ShapeDtypeStruct((1, 128), F32)] + r_shape,
        scratch_shapes=r_scratch,
        compiler_params=_params(1),
    )(dx_up, x, gain, w, proj, proj, gq, gk, *[a for triple in dqkv for a in triple], dcq, dckv, dkpe, *r_args)
    return outs[:5], outs[5:]


def _out_proj(x, o_dil, o_mla, g_dil, g_mla, w, tm=512):
    T, D = x.shape
    W = o_dil.shape[1]

    def body(x_ref, od_ref, om_ref, gd_ref, gm_ref, w_ref, xo_ref, oc_ref):
        od, om = od_ref[...], om_ref[...]
        oc_ref[:, 0:W] = (od * _rstd(od) * gd_ref[...]).astype(BF16)
        oc_ref[:, W:2 * W] = (om * _rstd(om) * gm_ref[...]).astype(BF16)
        xo_ref[...] = x_ref[...] + _dot(oc_ref[...], w_ref[...])

    row = lambda i: (i, 0)
    fix = lambda i: (0, 0)
    return pl.pallas_call(
        body, name="out_proj", grid=(T // tm,),
        in_specs=[pl.BlockSpec((tm, D), row), pl.BlockSpec((tm, W), row), pl.BlockSpec((tm, W), row),
                  pl.BlockSpec((1, W), fix), pl.BlockSpec((1, W), fix), pl.BlockSpec((2 * W, D), fix)],
        out_specs=[pl.BlockSpec((tm, D), row), pl.BlockSpec((tm, 2 * W), row)],
        out_shape=[jax.ShapeDtypeStruct((T, D), F32), jax.ShapeDtypeStruct((T, 2 * W), BF16)],
        compiler_params=_params(1),
    )(x, o_dil, o_mla, g_dil, g_mla, w)


def _out_proj_bwd(dx, o_dil, o_mla, g_dil, g_mla, w, tm=512):
    T, D = dx.shape
    W = o_dil.shape[1]

    def body(dx_ref, od_ref, om_ref, gd_ref, gm_ref, w_ref, dod_ref, dom_ref, dgd_ref, dgm_ref):
        @pl.when(pl.program_id(0) == 0)
        def _():
            dgd_ref[...] = jnp.zeros_like(dgd_ref)
            dgm_ref[...] = jnp.zeros_like(dgm_ref)

        doc = _dot_nt(dx_ref[...].astype(BF16), w_ref[...])
        od, om = od_ref[...], om_ref[...]
        dod, dgd = _rms_bwd(doc[:, 0:W], od, gd_ref[...], _rstd(od))
        dom, dgm = _rms_bwd(doc[:, W:2 * W], om, gm_ref[...], _rstd(om))
        dod_ref[...] = dod
        dom_ref[...] = dom
        dgd_ref[...] += jnp.sum(dgd, axis=0, keepdims=True)
        dgm_ref[...] += jnp.sum(dgm, axis=0, keepdims=True)

    row = lambda i: (i, 0)
    fix = lambda i: (0, 0)
    return pl.pallas_call(
        body, name="out_proj_bwd", grid=(T // tm,),
        in_specs=[pl.BlockSpec((tm, D), row), pl.BlockSpec((tm, W), row), pl.BlockSpec((tm, W), row),
                  pl.BlockSpec((1, W), fix), pl.BlockSpec((1, W), fix), pl.BlockSpec((2 * W, D), fix)],
        out_specs=[pl.BlockSpec((tm, W), row), pl.BlockSpec((tm, W), row),
                   pl.BlockSpec((1, W), fix), pl.BlockSpec((1, W), fix)],
        out_shape=[jax.ShapeDtypeStruct((T, W), F32), jax.ShapeDtypeStruct((T, W), F32),
                   jax.ShapeDtypeStruct((1, W), F32), jax.ShapeDtypeStruct((1, W), F32)],
        compiler_params=_params(1),
    )(dx, o_dil, o_mla, g_dil, g_mla, w)


def _pair_rstd(x, lo):
    sq = x * x
    s0 = jnp.sum(jnp.where(lo, sq, 0.0), axis=-1, keepdims=True)
    s1 = jnp.sum(jnp.where(lo, 0.0, sq), axis=-1, keepdims=True)
    return jnp.where(lo, lax.rsqrt(s0 / DIL_HD + EPS), lax.rsqrt(s1 / DIL_HD + EPS))


def _pair_rms_bwd(dn, x, r, g, lo):
    u = dn * g
    t = u * x
    d0 = jnp.sum(jnp.where(lo, t, 0.0), axis=-1, keepdims=True)
    d1 = jnp.sum(jnp.where(lo, 0.0, t), axis=-1, keepdims=True)
    dx = r * u - x * (r * r * r) * (jnp.where(lo, d0, d1) / DIL_HD)
    return dx, jnp.sum(dn * x * r, axis=0, keepdims=True)


def _pair_col(x, lo, e):
    sel = lo if e == 0 else jnp.logical_not(lo)
    return jnp.max(jnp.where(sel, x, NEG), axis=-1, keepdims=True)


def _dil_masks(n):
    lo = lax.broadcasted_iota(jnp.int32, (DIL_BLOCK, DIL_BLOCK), 1) < DIL_HD
    row = lax.broadcasted_iota(jnp.int32, (2 * DIL_BLOCK, 2 * DIL_BLOCK), 0) % DIL_BLOCK
    col = lax.broadcasted_iota(jnp.int32, (2 * DIL_BLOCK, 2 * DIL_BLOCK), 1)
    prev = jnp.logical_and(jnp.logical_and(col < DIL_BLOCK, col >= row), n > 0)
    cur = jnp.logical_and(col >= DIL_BLOCK, col - DIL_BLOCK <= row)
    return lo, jnp.logical_or(prev, cur)


def _stack_heads(x, lo):
    return jnp.concatenate([jnp.where(lo, x, 0.0), jnp.where(lo, 0.0, x)], axis=0)


def _unstack_heads(x2, lo):
    return jnp.where(lo, x2[:DIL_BLOCK], x2[DIL_BLOCK:])


def _dil_pairs(d):
    return 4 if d == 1 else 1


def _sub_rows(r, d):
    return pl.ds(r, DIL_BLOCK, stride=d) if d > 1 else pl.ds(0, DIL_BLOCK)


def _split_subsequences(loads, d, P):
    for r in range(d):
        for p in range(P):
            for block, scratch, part in loads:
                piece = block[_sub_rows(r, d), pl.ds(128 * p, 128)]
                if part is None:
                    scratch[r * P + p] = piece
                else:
                    scratch[r * P + p, pl.ds(DIL_BLOCK * part, DIL_BLOCK), :] = piece


def _keep_previous_block(scratches, n):
    for scratch in scratches:
        @pl.when(n == 0)
        def _():
            scratch[:, pl.ds(0, DIL_BLOCK), :] = jnp.zeros((scratch.shape[0], DIL_BLOCK, 128), F32)

        @pl.when(n > 0)
        def _():
            scratch[:, pl.ds(0, DIL_BLOCK), :] = scratch[:, pl.ds(DIL_BLOCK, DIL_BLOCK), :]


def _merge_subsequences(stores, d, P):
    for r in range(d):
        for p in range(P):
            for block, scratch, plus in stores:
                part = _sub_rows(r, d), pl.ds(128 * p, 128)
                block[part] = scratch[r * P + p] if plus is None else scratch[r * P + p] + plus[part]


def _dil_fwd(qh, kh, proj, bias, d, prev):
    T = proj.shape[0]
    P = _dil_pairs(d)
    rows, cw, n_it = DIL_BLOCK * d, 128 * P, d * P
    nblk = T // rows
    has_prev = prev is not None

    def body(*refs):
        q_ref, kc_ref, vc_ref, bias_ref = refs[:4]
        refs = refs[4:]
        if has_prev:
            oin_ref, lin_ref = refs[:2]
            refs = refs[2:]
        o_ref, l_ref, qs, ks, vs, os_, ls_ = refs[:7]
        pb, n = pl.program_id(0), pl.program_id(1)
        lo, valid = _dil_masks(n)
        _keep_previous_block((ks, vs), n)
        loads = [(q_ref, qs, None), (kc_ref, ks, 1), (vc_ref, vs, 1)]
        if has_prev:
            ois, lis = refs[7:]
            loads += [(oin_ref, ois, None), (lin_ref, lis, None)]
        _split_subsequences(loads, d, P)

        def step(i, carry):
            q2 = _stack_heads(qs[i], lo).astype(BF16)
            s = jnp.where(valid, _dot_nt(q2, ks[i].astype(BF16)) + bias_ref[pb * P + i % P], NEG)
            m = jnp.max(s, axis=-1, keepdims=True)
            p = jnp.exp(s - m)
            l = jnp.sum(p, axis=-1, keepdims=True)
            o = _unstack_heads(_dot(p.astype(BF16), vs[i].astype(BF16)) / l, lo)
            lse = _unstack_heads(jnp.broadcast_to(m + jnp.log(l), (2 * DIL_BLOCK, 128)), lo)
            if has_prev:
                lin = lis[i]
                mx = jnp.maximum(lin, lse)
                lnew = mx + jnp.log(jnp.exp(lin - mx) + jnp.exp(lse - mx))
                o = ois[i] * jnp.exp(lin - lnew) + o * jnp.exp(lse - lnew)
                lse = lnew
            os_[i] = o
            ls_[i] = lse
            return carry

        lax.fori_loop(0, n_it, step, 0, unroll=4)
        _merge_subsequences([(o_ref, os_, None), (l_ref, ls_, None)], d, P)

    blk = (rows, cw)
    vcol = 2 * DIL_WIDTH // cw
    fix3 = lambda pb, n: (0, 0, 0)
    tok = pl.BlockSpec(blk, lambda pb, n: (n, pb))
    in_specs = [tok, tok, pl.BlockSpec(blk, lambda pb, n: (n, vcol + pb)),
                pl.BlockSpec((DIL_HEADS // 2, 2 * DIL_BLOCK, 2 * DIL_BLOCK), fix3)]
    args = [qh, kh, proj, bias]
    one, two = pltpu.VMEM((n_it, DIL_BLOCK, 128), F32), pltpu.VMEM((n_it, 2 * DIL_BLOCK, 128), F32)
    scratch = [one, two, two, one, one]
    if has_prev:
        in_specs += [tok, tok]
        args += list(prev)
        scratch += [one, one]
    out = jax.ShapeDtypeStruct((T, DIL_WIDTH), F32)
    return pl.pallas_call(
        body, name=f"dil_fwd_d{d}", grid=(DIL_HEADS // 2 // P, nblk), in_specs=in_specs, out_specs=[tok, tok],
        out_shape=[out, out], scratch_shapes=scratch, compiler_params=_params(2),
    )(*args)


def _dil_bwd(qh, kh, proj, o, lse, do, bias, d, prev):
    T = proj.shape[0]
    P = _dil_pairs(d)
    rows, cw, n_it = DIL_BLOCK * d, 128 * P, d * P
    nblk = T // rows
    has_prev = prev is not None

    def body(*refs):
        q_ref, kc_ref, vc_ref, o_ref, l_ref, do_ref, bias_ref = refs[:7]
        dqi_ref, dki_ref, dvi_ref = refs[7:10] if has_prev else (None, None, None)
        dq_ref, dk_ref, dv_ref, db_ref, qs, ks, vs, os_, ls_, dos, dqs, dks, dvs, ck, cv = refs[7 + 3 * has_prev:]
        pb, n = pl.program_id(0), pl.program_id(1)
        lo, valid = _dil_masks(n)

        @pl.when((pb == 0) & (n == 0))
        def _():
            db_ref[...] = jnp.zeros_like(db_ref)

        @pl.when(n == 0)
        def _():
            ck[...] = jnp.zeros_like(ck)
            cv[...] = jnp.zeros_like(cv)

        _keep_previous_block((ks, vs), n)
        _split_subsequences([(q_ref, qs, None), (kc_ref, ks, 1), (vc_ref, vs, 1),
                             (o_ref, os_, None), (l_ref, ls_, None), (do_ref, dos, None)], d, P)

        def step(i, carry):
            pair = pb * P + i % P
            q2 = _stack_heads(qs[i], lo).astype(BF16)
            kcat, vcat = ks[i].astype(BF16), vs[i].astype(BF16)
            dov = dos[i]
            do2 = _stack_heads(dov, lo).astype(BF16)
            delta = jnp.sum(_stack_heads(dov * os_[i], lo), axis=-1, keepdims=True)
            lse_pair = ls_[i]
            lse2 = jnp.concatenate([_pair_col(lse_pair, lo, 0), _pair_col(lse_pair, lo, 1)], axis=0)
            s = jnp.where(valid, _dot_nt(q2, kcat) + bias_ref[pair], NEG)
            p = jnp.exp(s - lse2)
            ds = p * (_dot_nt(do2, vcat) - delta)
            db_ref[pair] += ds
            dsb = ds.astype(BF16)
            dqs[i] = _unstack_heads(_dot(dsb, kcat), lo)
            dk2 = _dot_tn(dsb, q2)
            dv2 = _dot_tn(p.astype(BF16), do2)
            dks[i] = ck[i] + dk2[:DIL_BLOCK]
            dvs[i] = cv[i] + dv2[:DIL_BLOCK]
            ck[i] = dk2[DIL_BLOCK:]
            cv[i] = dv2[DIL_BLOCK:]
            return carry

        @pl.when(n < nblk)
        def _():
            lax.fori_loop(0, n_it, step, 0, unroll=2)
            _merge_subsequences([(dq_ref, dqs, dqi_ref), (dk_ref, dks, dki_ref), (dv_ref, dvs, dvi_ref)], d, P)

        @pl.when(n == nblk)
        def _():
            _merge_subsequences([(dk_ref, ck, dki_ref), (dv_ref, cv, dvi_ref)], d, P)

    blk = (rows, cw)
    vcol = 2 * DIL_WIDTH // cw
    qn_ = lambda n: jnp.minimum(n, nblk - 1)
    pn_ = lambda n: jnp.maximum(n - 1, 0)
    fix3 = lambda pb, n: (0, 0, 0)
    tok_q = pl.BlockSpec(blk, lambda pb, n: (qn_(n), pb))
    tok_p = pl.BlockSpec(blk, lambda pb, n: (pn_(n), pb))
    bias_spec = pl.BlockSpec((DIL_HEADS // 2, 2 * DIL_BLOCK, 2 * DIL_BLOCK), fix3)
    in_specs = [tok_q, tok_q, pl.BlockSpec(blk, lambda pb, n: (qn_(n), vcol + pb)), tok_q, tok_q, tok_q, bias_spec]
    in_specs += [tok_q, tok_p, tok_p] if has_prev else []
    tok_shape = jax.ShapeDtypeStruct((T, DIL_WIDTH), F32)
    one, two = pltpu.VMEM((n_it, DIL_BLOCK, 128), F32), pltpu.VMEM((n_it, 2 * DIL_BLOCK, 128), F32)
    dq, dk, dv, db = pl.pallas_call(
        body, name=f"dil_bwd_d{d}", grid=(DIL_HEADS // 2 // P, nblk + 1), in_specs=in_specs,
        out_specs=[tok_q, tok_p, tok_p, bias_spec],
        out_shape=[tok_shape, tok_shape, tok_shape, jax.ShapeDtypeStruct(bias.shape, F32)],
        scratch_shapes=[one, two, two] + [one] * 8,
        compiler_params=_params(2),
    )(qh, kh, proj, o, lse, do, bias, *(prev or ()))
    return (dq, dk, dv), db


def _t5_bucket(dist):
    max_exact = REL_BUCKETS // 2
    dd = np.maximum(dist, 1).astype(np.float32)
    large = max_exact + (np.log(dd / max_exact) / np.log(REL_MAX_DIST / max_exact)
                         * (REL_BUCKETS - max_exact)).astype(np.int32)
    large = np.minimum(large, REL_BUCKETS - 1)
    return np.where(dist < max_exact, dist, large).astype(np.int32)


def _bucket_onehots():
    i = np.arange(DIL_BLOCK)[:, None]
    j = np.arange(DIL_BLOCK)[None, :]
    out = []
    for _, d in DIL_BRANCHES:
        dist = np.concatenate([DIL_BLOCK + i - j, i - j], axis=1)
        bucket = _t5_bucket(np.clip(dist, 0, None) * d).reshape(-1)
        out.append(jnp.asarray(np.eye(REL_BUCKETS, dtype=np.float32)[:, bucket], BF16))
    return out


def _bias_tables(rel_bias, onehots):
    n = len(onehots)

    def body(rb_ref, *refs):
        parts = _split3(rb_ref[...])
        for k in range(n):
            oh = refs[k][...]
            refs[n + k][...] = _dot(parts[0], oh) + _dot(parts[1], oh) + _dot(parts[2], oh)

    flat = pl.pallas_call(
        body, name="bias_tables",
        out_shape=[jax.ShapeDtypeStruct((DIL_HEADS, 2 * DIL_BLOCK * DIL_BLOCK), F32)] * n,
        compiler_params=pltpu.CompilerParams(vmem_limit_bytes=VMEM_LIMIT),
    )(rel_bias, *onehots)
    return [t.reshape(DIL_HEADS // 2, 2 * DIL_BLOCK, 2 * DIL_BLOCK) for t in flat]


def _bias_grad(dbs, onehots):
    n = len(dbs)
    dbs = [t.reshape(DIL_HEADS, 2 * DIL_BLOCK * DIL_BLOCK) for t in dbs]

    def body(*refs):
        acc = jnp.zeros((DIL_HEADS, REL_BUCKETS), F32)
        for k in range(n):
            oh = refs[n + k][...]
            for part in _split3(refs[k][...]):
                acc = acc + _dot_nt(part, oh)
        refs[-1][...] = acc

    return pl.pallas_call(
        body, name="bias_grad",
        out_shape=jax.ShapeDtypeStruct((DIL_HEADS, REL_BUCKETS), F32),
        compiler_params=pltpu.CompilerParams(vmem_limit_bytes=VMEM_LIMIT),
    )(*dbs, *onehots)


def _swap_halves(x):
    lane = lax.broadcasted_iota(jnp.int32, x.shape, 1)
    first = (lane % 64) < 32
    return jnp.where(first, pltpu.roll(x, 96, 1), pltpu.roll(x, 32, 1))


def _rope_tables(T):
    pos = jnp.arange(T, dtype=F32)
    inv_freq = ROPE_BASE ** (-jnp.arange(0, MLA_ROPE, 2, dtype=F32) / MLA_ROPE)
    ang = pos[:, None] * inv_freq[None, :]
    z = jnp.zeros((T, 128 - MLA_ROPE), F32)
    cos = jnp.concatenate([jnp.cos(ang), jnp.cos(ang), z], axis=-1)
    sin = jnp.concatenate([-jnp.sin(ang), jnp.sin(ang), z], axis=-1)
    return cos, sin


def _mla_prep(proj, cos, sin, g_qa, g_kva, g_q, g_k, wq, wkv, tm=1024):
    T = proj.shape[0]
    H = MLA_HEADS
    scale = MLA_QK ** -0.5

    def body(cq_ref, ckv_ref, kpe_ref, cos_ref, sin_ref, gqa_ref, gkva_ref, gq_ref, gk_ref, wq_ref, wkv_ref,
             q_ref, k_ref, v_ref):
        cosv, sinv = cos_ref[...], sin_ref[...]

        def rope(x):
            return x * cosv + _swap_halves(x) * sinv

        cq = cq_ref[...]
        qp = _dot_nt((cq * _rstd(cq) * gqa_ref[...]).astype(BF16), wq_ref[...])
        ckv = ckv_ref[...]
        kvp = _dot((ckv * _rstd(ckv) * gkva_ref[...]).astype(BF16), wkv_ref[...])
        kpe = kpe_ref[...]
        one_hot_lane = (lax.broadcasted_iota(jnp.int32, (tm, 128), 1) == 0).astype(BF16)
        for h in range(H):
            a = qp[:, MLA_PAD * h:MLA_PAD * (h + 1)]
            qn = a * _rstd(a, MLA_QK) * gq_ref[...]
            q_ref[h, :, 0:128] = (qn[:, 0:128] * scale).astype(BF16)
            q_ref[h, :, 128:256] = (rope(qn[:, 128:256]) * scale).astype(BF16)
            kn = kvp[:, MLA_PAD * h:MLA_PAD * h + 128]
            r = lax.rsqrt((jnp.sum(kn * kn, axis=-1, keepdims=True)
                           + jnp.sum(kpe * kpe, axis=-1, keepdims=True)) / MLA_QK + EPS)
            k_ref[h, :, 0:128] = (kn * r * gk_ref[:, 0:128]).astype(BF16)
            k_ref[h, :, 128:256] = rope(kpe * r * gk_ref[:, 128:256]).astype(BF16)
            v_ref[h, :, 0:128] = kvp[:, MLA_PAD * h + 128:MLA_PAD * (h + 1)].astype(BF16)
            v_ref[h, :, 128:256] = one_hot_lane

    fix = lambda i: (0, 0)
    return pl.pallas_call(
        body, name="mla_prep", grid=(T // tm,),
        in_specs=[pl.BlockSpec((tm, 256), lambda i: (i, 6)), pl.BlockSpec((tm, 128), lambda i: (i, 14)),
                  pl.BlockSpec((tm, 128), lambda i: (i, 15)),
                  pl.BlockSpec((tm, 128), lambda i: (i, 0)), pl.BlockSpec((tm, 128), lambda i: (i, 0)),
                  pl.BlockSpec((1, 256), fix), pl.BlockSpec((1, 128), fix),
                  pl.BlockSpec((1, 256), fix), pl.BlockSpec((1, 256), fix),
                  pl.BlockSpec((H * MLA_PAD, 256), fix), pl.BlockSpec((128, H * MLA_PAD), fix)],
        out_specs=[pl.BlockSpec((H, tm, MLA_PAD), lambda i: (0, i, 0)), pl.BlockSpec((H, tm, MLA_PAD), lambda i: (0, i, 0)),
                   pl.BlockSpec((H, tm, 2 * MLA_V), lambda i: (0, i, 0))],
        out_shape=[jax.ShapeDtypeStruct((H, T, MLA_PAD), BF16), jax.ShapeDtypeStruct((H, T, MLA_PAD), BF16),
                   jax.ShapeDtypeStruct((H, T, 2 * MLA_V), BF16)],
        compiler_params=_params(1),
    )(proj, proj, proj, cos, sin, g_qa, g_kva, g_q, g_k, wq, wkv)


def _mla_prep_bwd(proj, cos, sin, g_qa, g_kva, g_q, g_k, wq, wkv, dq, dk, dv, tm=1024):
    T = proj.shape[0]
    H = MLA_HEADS
    scale = MLA_QK ** -0.5

    def body(cq_ref, ckv_ref, kpe_ref, cos_ref, sin_ref, gqa_ref, gkva_ref, gq_ref, gk_ref, wq_ref, wkv_ref,
             dq_ref, dk_ref, dv_ref,
             dcq_ref, dckv_ref, dkpe_ref, cqn_ref, ckvn_ref, dqp_ref, dkvp_ref,
             dgqa_ref, dgkva_ref, dgq_ref, dgk_ref):
        @pl.when(pl.program_id(0) == 0)
        def _():
            for ref in (dgqa_ref, dgkva_ref, dgq_ref, dgk_ref):
                ref[...] = jnp.zeros_like(ref)

        cosv, sinv = cos_ref[...], sin_ref[...]

        def rope_bwd(dy):
            return dy * cosv + _swap_halves(dy * sinv)

        cq = cq_ref[...]
        rcq = _rstd(cq)
        cqn = (cq * rcq * gqa_ref[...]).astype(BF16)
        cqn_ref[...] = cqn
        qp = _dot_nt(cqn, wq_ref[...])
        ckv = ckv_ref[...]
        rckv = _rstd(ckv)
        ckvn = (ckv * rckv * gkva_ref[...]).astype(BF16)
        ckvn_ref[...] = ckvn
        kvp = _dot(ckvn, wkv_ref[...])
        kpe = kpe_ref[...]
        dkpe = jnp.zeros_like(kpe)
        dgq = jnp.zeros((1, MLA_PAD), F32)
        dgk = jnp.zeros((1, MLA_PAD), F32)
        for h in range(H):
            a = qp[:, MLA_PAD * h:MLA_PAD * (h + 1)]
            dqh = dq_ref[h]
            dn = jnp.concatenate([dqh[:, 0:128], rope_bwd(dqh[:, 128:256])], axis=-1) * scale
            da, dg = _rms_bwd(dn, a, gq_ref[...], _rstd(a, MLA_QK), MLA_QK)
            dgq = dgq + jnp.sum(dg, axis=0, keepdims=True)
            dqp_ref[:, MLA_PAD * h:MLA_PAD * (h + 1)] = da.astype(BF16)

            ak = jnp.concatenate([kvp[:, MLA_PAD * h:MLA_PAD * h + 128], kpe], axis=-1)
            dkh = dk_ref[h]
            dnk = jnp.concatenate([dkh[:, 0:128], rope_bwd(dkh[:, 128:256])], axis=-1)
            dak, dg = _rms_bwd(dnk, ak, gk_ref[...], _rstd(ak, MLA_QK), MLA_QK)
            dgk = dgk + jnp.sum(dg, axis=0, keepdims=True)
            dkpe = dkpe + dak[:, 128:256]
            dkvp_ref[:, MLA_PAD * h:MLA_PAD * h + 128] = dak[:, 0:128].astype(BF16)
            dkvp_ref[:, MLA_PAD * h + 128:MLA_PAD * (h + 1)] = dv_ref[h].astype(BF16)
        dkpe_ref[...] = dkpe
        dgq_ref[...] += dgq
        dgk_ref[...] += dgk
        dcq, dg = _rms_bwd(_dot(dqp_ref[...], wq_ref[...]), cq, gqa_ref[...], rcq)
        dcq_ref[...] = dcq
        dgqa_ref[...] += jnp.sum(dg, axis=0, keepdims=True)
        dckv, dg = _rms_bwd(_dot_nt(dkvp_ref[...], wkv_ref[...]), ckv, gkva_ref[...], rckv)
        dckv_ref[...] = dckv
        dgkva_ref[...] += jnp.sum(dg, axis=0, keepdims=True)

    fix = lambda i: (0, 0)
    row = lambda i: (i, 0)
    head = lambda i: (0, i, 0)
    return pl.pallas_call(
        body, name="mla_prep_bwd", grid=(T // tm,),
        in_specs=[pl.BlockSpec((tm, 256), lambda i: (i, 6)), pl.BlockSpec((tm, 128), lambda i: (i, 14)),
                  pl.BlockSpec((tm, 128), lambda i: (i, 15)),
                  pl.BlockSpec((tm, 128), row), pl.BlockSpec((tm, 128), row),
                  pl.BlockSpec((1, 256), fix), pl.BlockSpec((1, 128), fix),
                  pl.BlockSpec((1, 256), fix), pl.BlockSpec((1, 256), fix),
                  pl.BlockSpec((H * MLA_PAD, 256), fix), pl.BlockSpec((128, H * MLA_PAD), fix),
                  pl.BlockSpec((H, tm, MLA_PAD), head), pl.BlockSpec((H, tm, MLA_PAD), head),
                  pl.BlockSpec((H, tm, MLA_V), head)],
        out_specs=[pl.BlockSpec((tm, 256), row), pl.BlockSpec((tm, 128), row), pl.BlockSpec((tm, 128), row),
                   pl.BlockSpec((tm, 256), row), pl.BlockSpec((tm, 128), row),
                   pl.BlockSpec((tm, H * MLA_PAD), row), pl.BlockSpec((tm, H * MLA_PAD), row),
                   pl.BlockSpec((1, 256), fix), pl.BlockSpec((1, 128), fix),
                   pl.BlockSpec((1, 256), fix), pl.BlockSpec((1, 256), fix)],
        out_shape=[jax.ShapeDtypeStruct((T, 256), F32), jax.ShapeDtypeStruct((T, 128), F32),
                   jax.ShapeDtypeStruct((T, 128), F32),
                   jax.ShapeDtypeStruct((T, 256), BF16), jax.ShapeDtypeStruct((T, 128), BF16),
                   jax.ShapeDtypeStruct((T, H * MLA_PAD), BF16), jax.ShapeDtypeStruct((T, H * MLA_PAD), BF16),
                   jax.ShapeDtypeStruct((1, 256), F32), jax.ShapeDtypeStruct((1, 128), F32),
                   jax.ShapeDtypeStruct((1, 256), F32), jax.ShapeDtypeStruct((1, 256), F32)],
        compiler_params=_params(1),
    )(proj, proj, proj, cos, sin, g_qa, g_kva, g_q, g_k, wq, wkv, dq, dk, dv)


def _causal_pairs(T, tq, tk, key_major):
    pairs = [(i, j) for i in range(T // tq) for j in range(T // tk) if j * tk <= i * tq + tq - 1]
    if key_major:
        pairs.sort(key=lambda p: (p[1], p[0]))
    outer = [p[1] if key_major else p[0] for p in pairs]
    first = [int(t == 0 or outer[t] != outer[t - 1]) for t in range(len(pairs))]
    last = [int(t == len(pairs) - 1 or outer[t] != outer[t + 1]) for t in range(len(pairs))]
    tab = lambda v: jnp.asarray(np.array(v, np.int32))
    return tab([p[0] for p in pairs]), tab([p[1] for p in pairs]), tab(first), tab(last)


def _causal_scores(qv, kv, qi, ki, row0, tq, tk, masked):
    s = _dot_nt(qv, kv)
    if masked:
        row = lax.broadcasted_iota(jnp.int32, s.shape, 0) + (qi * tq + row0)
        col = lax.broadcasted_iota(jnp.int32, s.shape, 1) + ki * tk
        s = jnp.where(col <= row, s, NEG)
    return s


def _mla_attn(q, k, v, ride=None, tq=1024, tk=2048, rc=256):
    H, T, _ = q.shape
    tables = _causal_pairs(T, tq, tk, key_major=False)
    n_pairs = int(tables[0].shape[0])
    r_args, r_in, r_shape, r_out, r_scratch = _ride_parts(ride)

    def body(qt, kt, ft, lt, q_ref, k_ref, v_ref, o_ref, lse_ref, m_s, acc):
        t = pl.program_id(1)
        qi, ki = qt[t], kt[t]

        @pl.when(ft[t] == 1)
        def _():
            m_s[...] = jnp.full_like(m_s, NEG)
            acc[...] = jnp.zeros_like(acc)

        def update(masked):
            kk, vv = k_ref[...], v_ref[...]
            for c in range(tq // rc):
                rows = pl.ds(c * rc, rc)
                s = _causal_scores(q_ref[rows, :], kk, qi, ki, c * rc, tq, tk, masked)
                m_old = m_s[rows, :]
                m_new = jnp.maximum(m_old, jnp.max(s, axis=-1, keepdims=True))
                p = jnp.exp(s - m_new).astype(BF16)
                acc[rows, :] = jnp.exp(m_old - m_new) * acc[rows, :] + _dot(p, vv)
                m_s[rows, :] = m_new

        diagonal = (ki + 1) * tk - 1 > qi * tq

        @pl.when(diagonal)
        def _():
            update(True)

        @pl.when(jnp.logical_not(diagonal))
        def _():
            update(False)

        @pl.when(lt[t] == 1)
        def _():
            l = jnp.max(acc[:, MLA_V:], axis=-1, keepdims=True)
            o_ref[...] = acc[:, :MLA_V] / l
            lse_ref[...] = jnp.broadcast_to(m_s[...] + jnp.log(l), lse_ref.shape)

    qrow = lambda h, t, qt, kt, ft, lt: (h, qt[t], 0)
    krow = lambda h, t, qt, kt, ft, lt: (h, kt[t], 0)
    first = lambda: (pl.program_id(0) == 0) & (pl.program_id(1) == 0)
    last = lambda: (pl.program_id(0) == H - 1) & (pl.program_id(1) == n_pairs - 1)
    outs = pl.pallas_call(
        _riding(body, 7, 2, 2, ride, first, last), name="mla_attn",
        grid_spec=pltpu.PrefetchScalarGridSpec(
            num_scalar_prefetch=4, grid=(H, n_pairs),
            in_specs=[pl.BlockSpec((None, tq, MLA_PAD), qrow), pl.BlockSpec((None, tk, MLA_PAD), krow),
                      pl.BlockSpec((None, tk, 2 * MLA_V), krow)] + r_in,
            out_specs=[pl.BlockSpec((tq, MLA_V), lambda h, t, qt, kt, ft, lt: (qt[t], h)),
                       pl.BlockSpec((None, tq, 128), qrow)] + r_out,
            scratch_shapes=[pltpu.VMEM((tq, 1), F32), pltpu.VMEM((tq, 2 * MLA_V), F32)] + r_scratch),
        out_shape=[jax.ShapeDtypeStruct((T, H * MLA_V), F32), jax.ShapeDtypeStruct((H, T, 128), F32)] + r_shape,
        compiler_params=_params(2),
    )(*tables, q, k, v, *r_args)
    return outs[:2], outs[2:]


def _mla_attn_bwd(q, k, v, o, lse, do, ride=None, tq=1024, tk=1024, rc=512):
    H, T, _ = q.shape
    tables = _causal_pairs(T, tq, tk, key_major=True)
    n_pairs = int(tables[0].shape[0])
    r_args, r_in, r_shape, r_out, r_scratch = _ride_parts(ride)

    def body(qt, kt, ft, lt, q_ref, k_ref, v_ref, o_ref, lse_ref, do_ref, dq_ref, dk_ref, dv_ref, dk_s, dv_s):
        t = pl.program_id(1)
        qi, ki = qt[t], kt[t]

        @pl.when(t == 0)
        def _():
            dq_ref[...] = jnp.zeros_like(dq_ref)

        @pl.when(ft[t] == 1)
        def _():
            dk_s[...] = jnp.zeros_like(dk_s)
            dv_s[...] = jnp.zeros_like(dv_s)

        def update(masked):
            kk, vv = k_ref[...], v_ref[...]
            for c in range(tq // rc):
                rows = pl.ds(c * rc, rc)
                qv, dov = q_ref[rows, :], do_ref[rows, :]
                delta = jnp.sum(dov * o_ref[rows, :], axis=-1, keepdims=True)
                lse_v = jnp.max(lse_ref[rows, :], axis=-1, keepdims=True)
                p = jnp.exp(_causal_scores(qv, kk, qi, ki, c * rc, tq, tk, masked) - lse_v)
                dob = dov.astype(BF16)
                dv_s[...] += _dot_tn(p.astype(BF16), dob)
                ds = (p * (_dot_nt(dob, vv) - delta)).astype(BF16)
                dk_s[...] += _dot_tn(ds, qv)
                out_rows = pl.ds(pl.multiple_of(qi * tq + c * rc, rc), rc)
                dq_ref[out_rows, :] += _dot(ds, kk)

        diagonal = (ki + 1) * tk - 1 > qi * tq

        @pl.when(diagonal)
        def _():
            update(True)

        @pl.when(jnp.logical_not(diagonal))
        def _():
            update(False)

        @pl.when(lt[t] == 1)
        def _():
            dk_ref[...] = dk_s[...]
            dv_ref[...] = dv_s[...]

    qrow = lambda h, t, qt, kt, ft, lt: (h, qt[t], 0)
    krow = lambda h, t, qt, kt, ft, lt: (h, kt[t], 0)
    qcol = lambda h, t, qt, kt, ft, lt: (qt[t], h)
    first = lambda: (pl.program_id(0) == 0) & (pl.program_id(1) == 0)
    last = lambda: (pl.program_id(0) == H - 1) & (pl.program_id(1) == n_pairs - 1)
    outs = pl.pallas_call(
        _riding(body, 10, 3, 2, ride, first, last), name="mla_attn_bwd",
        grid_spec=pltpu.PrefetchScalarGridSpec(
            num_scalar_prefetch=4, grid=(H, n_pairs),
            in_specs=[pl.BlockSpec((None, tq, MLA_PAD), qrow), pl.BlockSpec((None, tk, MLA_PAD), krow),
                      pl.BlockSpec((None, tk, MLA_V), krow), pl.BlockSpec((tq, MLA_V), qcol),
                      pl.BlockSpec((None, tq, 128), qrow), pl.BlockSpec((tq, MLA_V), qcol)] + r_in,
            out_specs=[pl.BlockSpec((None, T, MLA_PAD), lambda h, t, qt, kt, ft, lt: (h, 0, 0)),
                       pl.BlockSpec((None, tk, MLA_PAD), krow), pl.BlockSpec((None, tk, MLA_V), krow)] + r_out,
            scratch_shapes=[pltpu.VMEM((tk, MLA_PAD), F32), pltpu.VMEM((tk, MLA_V), F32)] + r_scratch),
        out_shape=[jax.ShapeDtypeStruct((H, T, MLA_PAD), F32), jax.ShapeDtypeStruct((H, T, MLA_PAD), F32),
                   jax.ShapeDtypeStruct((H, T, MLA_V), F32)] + r_shape,
        compiler_params=_params(2),
    )(*tables, q, k, v, o, lse, do, *r_args)
    return outs[:3], outs[3:]


def _pair_gain(g):
    return jnp.tile(g.reshape(1, DIL_HD), (1, 2))


def _pad_gain(g):
    return jnp.pad(g.reshape(1, MLA_QK), ((0, 0), (0, MLA_PAD - MLA_QK)))


def _local_step(x, target, s, comm):
    T = x.shape[0]
    w = comm.w
    gq, gk = _pair_gain(s["dil_q_norm"]) * DIL_HD ** -0.5, _pair_gain(s["dil_k_norm"])
    g_q, g_k = _pad_gain(s["mla_q_norm"]), _pad_gain(s["mla_k_norm"])
    cos, sin = _rope_tables(T)
    onehots = _bucket_onehots()
    biases = _bias_tables(s["rel_bias"], onehots)

    (x1, h1, gate1, up1), got = _ffn_fwd(x, s["ffn1_norm"], w["ffn1_w_gate"], w["ffn1_w_up"], w["ffn1_w_down"],
                                         ride=comm.gather(_GROUPS["attn"]))
    comm.weights_landed(_GROUPS["attn"], got)
    hm, proj, qh, kh = _in_proj(x1, s["mix_norm"], w["w_in"], gq, gk)
    dil = None
    for (_, d), bias in zip(DIL_BRANCHES, biases):
        dil = _dil_fwd(qh, kh, proj, bias, d, dil)
    o_dil, lse_dil = dil
    q, k, v = _mla_prep(proj, cos, sin, s["mla_q_a_norm"], s["mla_kv_a_norm"], g_q, g_k, w["mla_w_q_b"], w["mla_w_kv_b"])
    (o_mla, lse_mla), got = _mla_attn(q, k, v, ride=comm.gather(_GROUPS["ffn2"]))
    comm.weights_landed(_GROUPS["ffn2"], got)
    x2, oc = _out_proj(x1, o_dil, o_mla, s["out_norm_dil"], s["out_norm_mla"], w["w_out"])
    (dy, h2, gate2, up2, loss), _ = _ffn_fwd(x2, s["ffn2_norm"], w["ffn2_w_gate"], w["ffn2_w_up"], w["ffn2_w_down"],
                                             target=target)

    gw, gs = {}, {}

    def ffn_grads(name, dy_in, x_in, h, gate, up, early=None):
        (dx, a, dg, du, dyh, dgain), _ = _ffn_bwd(dy_in, x_in, s[name + "_norm"], gate, up,
                                                  w[name + "_w_gate"], w[name + "_w_up"], w[name + "_w_down"])
        gs[name + "_norm"] = dgain
        down, gate_n, up_n = (name + "_w_down",), (name + "_w_gate",), (name + "_w_up",)
        ride = lambda names: comm.scatter(names, gw) if early is not None else None
        gw[down[0]], landed = _matmul_tn(a, dyh, 1408, 1024, ride=ride(early))
        comm.grads_landed(early or (), landed)
        gw[gate_n[0]], landed = _matmul_tn(dg, h, 1408, 1024, ride=ride(down))
        comm.grads_landed(down, landed)
        gw[up_n[0]], landed = _matmul_tn(du, h, 1408, 1024, ride=ride(gate_n))
        comm.grads_landed(gate_n, landed)
        return dx

    dx2 = ffn_grads("ffn2", dy, x2, h2, gate2, up2)
    gw["w_out"], _ = _matmul_tn(oc, dx2, 1024, 1024)
    do_dil, do_mla, gs["out_norm_dil"], gs["out_norm_mla"] = _out_proj_bwd(
        dx2, o_dil, o_mla, s["out_norm_dil"], s["out_norm_mla"], w["w_out"])

    (dq, dk, dv), got = _mla_attn_bwd(q, k, v, o_mla, lse_mla, do_mla, ride=comm.scatter(_GROUPS["ffn2"], gw))
    comm.grads_landed(_GROUPS["ffn2"], got)
    (dcq, dckv, dkpe, cqn, ckvn, dqp, dkvp, gs["mla_q_a_norm"], gs["mla_kv_a_norm"], dg_q, dg_k) = _mla_prep_bwd(
        proj, cos, sin, s["mla_q_a_norm"], s["mla_kv_a_norm"], g_q, g_k, w["mla_w_q_b"], w["mla_w_kv_b"], dq, dk, dv)
    gs["mla_q_norm"], gs["mla_k_norm"] = dg_q[:, :MLA_QK], dg_k[:, :MLA_QK]
    gw["mla_w_q_b"], _ = _matmul_tn(dqp, cqn, 1024, 256)
    gw["mla_w_kv_b"], _ = _matmul_tn(ckvn, dkvp, 128, 1024)

    dqkv, dbs = None, []
    for (_, d), bias in reversed(list(zip(DIL_BRANCHES, biases))):
        dqkv, db = _dil_bwd(qh, kh, proj, o_dil, lse_dil, do_dil, bias, d, dqkv)
        dbs.insert(0, db)
    dqkv = [dqkv]
    gs["rel_bias"] = _bias_grad(dbs, onehots)

    ready = tuple(n for n in _GROUPS["attn"] if n != "w_in")
    (dx1, dproj, gs["mix_norm"], dgq, dgk), got = _in_proj_bwd(dx2, x1, s["mix_norm"], w["w_in"], proj, gq, gk,
                                                               dqkv, dcq, dckv, dkpe, ride=comm.scatter(ready, gw))
    comm.grads_landed(ready, got)
    gs["dil_q_norm"] = (dgq[:, :DIL_HD] + dgq[:, DIL_HD:]) * DIL_HD ** -0.5
    gs["dil_k_norm"] = dgk[:, :DIL_HD] + dgk[:, DIL_HD:]
    gw["w_in"], _ = _matmul_tn(dproj, hm, 1024, 1024)
    grad_x = ffn_grads("ffn1", dx1, x, h1, gate1, up1, early=("w_in",))
    return loss, grad_x, gw, gs


def _position():
    x, y, c = lax.axis_index("x"), lax.axis_index("y"), lax.axis_index("c")
    return x, y, c, 4 * x + 2 * y + c


def _peer(x, y, c, k):
    px = 1 - x if k & 4 else x
    py = 1 - y if k & 2 else y
    pc = 1 - c if k & 1 else c
    return (px, py, pc), 4 * px + 2 * py + pc


class _Ride:
    def __init__(self, arrays, scatter):
        self.arrays, self.scatter = list(arrays), list(scatter)
        self.n = n = len(self.arrays)
        self.specs = [pl.BlockSpec(memory_space=pl.ANY)] * n
        self.out_shape = [jax.ShapeDtypeStruct(a.shape if sc else (N_DEV,) + a.shape, a.dtype)
                          for a, sc in zip(self.arrays, self.scatter)]
        self.scratch = [pltpu.SemaphoreType.DMA((n, N_DEV - 1)), pltpu.SemaphoreType.DMA((n, N_DEV - 1)),
                        pltpu.SemaphoreType.DMA((n,))]

    def _copies(self, ins, outs, sems):
        send_sems, recv_sems, local_sems = sems
        x, y, c, me = _position()
        copies = []
        for a in range(self.n):
            src = ins[a].at[me] if self.scatter[a] else ins[a]
            copies.append(pltpu.make_async_copy(src, outs[a].at[me], local_sems.at[a]))
        for k in range(1, N_DEV):
            peer, peer_idx = _peer(x, y, c, k)
            for a in range(self.n):
                src = ins[a].at[peer_idx] if self.scatter[a] else ins[a]
                copies.append(pltpu.make_async_remote_copy(
                    src_ref=src, dst_ref=outs[a].at[me], send_sem=send_sems.at[a, k - 1], recv_sem=recv_sems.at[a, k - 1],
                    device_id=peer, device_id_type=pl.DeviceIdType.MESH))
        return copies

    def start(self, ins, outs, sems):
        for cp in self._copies(ins, outs, sems):
            cp.start()

    def wait(self, ins, outs, sems):
        for cp in self._copies(ins, outs, sems):
            cp.wait()


def _ride_parts(ride):
    if ride is None:
        return [], [], [], [], []
    return ride.arrays, ride.specs, ride.out_shape, ride.specs, ride.scratch


def _riding(body, n_in, n_out, n_scratch, ride, first, last):
    if ride is None:
        return body
    n = ride.n
    i1, i2 = n_in + n, n_in + n + n_out
    i3, i4 = i2 + n, i2 + n + n_scratch

    def wrapped(*refs):
        ins, outs, sems = refs[n_in:i1], refs[i2:i3], refs[i4:]

        @pl.when(first())
        def _():
            ride.start(ins, outs, sems)

        body(*refs[:n_in], *refs[i1:i2], *refs[i3:i4])

        @pl.when(last())
        def _():
            ride.wait(ins, outs, sems)

    return wrapped


def _gather_two_level(arrays, name):
    n = len(arrays)
    out_shape = [jax.ShapeDtypeStruct((N_DEV,) + a.shape, a.dtype) for a in arrays]

    def body(*refs):
        ins, outs = refs[:n], refs[n:2 * n]
        send_sems, recv_sems, local_sems = refs[2 * n:]
        x, y, c, me = _position()
        sibling = (x, y, 1 - c)
        chips = [(1 - x, y), (x, 1 - y), (1 - x, 1 - y)]
        block = lambda px, py, pc: 4 * px + 2 * py + pc

        def copy(a, k, blk, to, src=None):
            dst = outs[a].at[blk]
            return pltpu.make_async_remote_copy(
                src_ref=dst if src is None else src, dst_ref=dst, send_sem=send_sems.at[a, k], recv_sem=recv_sems.at[a, k],
                device_id=to, device_id_type=pl.DeviceIdType.MESH)

        local = [pltpu.make_async_copy(ins[a], outs[a].at[me], local_sems.at[a]) for a in range(n)]
        first = []
        for a in range(n):
            first.append(copy(a, 0, me, sibling, src=ins[a]))
            first += [copy(a, 1 + j, me, (*chip, c), src=ins[a]) for j, chip in enumerate(chips)]
        for cp in local + first:
            cp.start()
        passed = []
        for j, chip in enumerate(chips):
            for a in range(n):
                copy(a, 1 + j, block(*chip, c), sibling).wait_recv()
                passed.append(copy(a, 4 + j, block(*chip, c), sibling))
                passed[-1].start()
        for a in range(n):
            copy(a, 0, block(x, y, 1 - c), sibling).wait_recv()
            for j, chip in enumerate(chips):
                copy(a, 4 + j, block(*chip, 1 - c), sibling).wait_recv()
        for cp in first + passed:
            cp.wait_send()
        for cp in local:
            cp.wait()

    any_spec = [pl.BlockSpec(memory_space=pl.ANY)] * n
    return pl.pallas_call(
        body, name=name, in_specs=any_spec, out_specs=any_spec, out_shape=out_shape,
        scratch_shapes=[pltpu.SemaphoreType.DMA((n, N_DEV - 1)), pltpu.SemaphoreType.DMA((n, N_DEV - 1)),
                        pltpu.SemaphoreType.DMA((n,))],
    )(*arrays)


def _exchange(ride, name):
    def body(*refs):
        parts = refs[:ride.n], refs[ride.n:2 * ride.n], refs[2 * ride.n:]
        ride.start(*parts)
        ride.wait(*parts)

    return pl.pallas_call(body, name=name, in_specs=ride.specs, out_specs=ride.specs, out_shape=ride.out_shape,
                          scratch_shapes=ride.scratch)(*ride.arrays)


def _adamw_math(wv, g, m, v):
    m = ADAM_B1 * m + (1.0 - ADAM_B1) * g
    v = ADAM_B2 * v + (1.0 - ADAM_B2) * (g * g)
    m_hat = m / (1.0 - ADAM_B1 ** ADAM_STEP)
    v_hat = v / (1.0 - ADAM_B2 ** ADAM_STEP)
    delta = -ADAM_LR * (m_hat / (jnp.sqrt(v_hat) + ADAM_EPS) + ADAM_WD * wv)
    return delta, m, v


def _adamw(parts, wv, m, v):
    _, R, C = wv.shape
    tr = max([t for t in range(16, 257, 16) if R % t == 0] or [R])

    def body(p_ref, w_ref, m_ref, v_ref, g_ref, d_ref, mo_ref, vo_ref):
        g = p_ref[0].astype(F32)
        for j in range(1, N_DEV):
            g = g + p_ref[j].astype(F32)
        d, mn, vn = _adamw_math(w_ref[0], g, m_ref[0], v_ref[0])
        g_ref[0] = g
        d_ref[0] = d
        mo_ref[0] = mn
        vo_ref[0] = vn

    blk = pl.BlockSpec((1, tr, C), lambda i: (0, i, 0))
    out = jax.ShapeDtypeStruct((1, R, C), F32)
    return pl.pallas_call(
        body, name="adamw", grid=(R // tr,),
        in_specs=[pl.BlockSpec((N_DEV, tr, C), lambda i: (0, i, 0)), blk, blk, blk],
        out_specs=[blk] * 4, out_shape=[out] * 4,
        compiler_params=_params(1),
    )(parts, wv, m, v)


_TRANSPOSED = ("ffn1_w_gate", "ffn1_w_up", "ffn2_w_gate", "ffn2_w_up", "w_in", "mla_w_q_b")
_GROUPS = {"ffn1": ("ffn1_w_gate", "ffn1_w_up", "ffn1_w_down"),
           "ffn2": ("ffn2_w_gate", "ffn2_w_up", "ffn2_w_down"),
           "attn": ("w_in", "mla_w_q_b", "mla_w_kv_b", "w_out")}
_SMALL = ("ffn1_norm", "mix_norm", "ffn2_norm", "out_norm_dil", "out_norm_mla", "mla_q_a_norm", "rel_bias",
          "mla_q_norm", "mla_k_norm", "mla_kv_a_norm", "dil_q_norm", "dil_k_norm")
_SMALL_ROWS = 48


def _cols_to_full(g):
    return g.transpose(1, 0, 2).reshape(g.shape[1], N_DEV * g.shape[2])


def _full_to_cols(f):
    return f.reshape(f.shape[0], N_DEV, f.shape[1] // N_DEV).transpose(1, 0, 2)


def _shard_view(name, a):
    return jnp.swapaxes(a, 1, 2) if name in _TRANSPOSED else a


def _to_full(name, g):
    if name == "mla_w_kv_b":
        return _cols_to_full(g)
    f = g.reshape(-1, g.shape[-1])
    if name == "w_in":
        f = jnp.pad(f, ((0, PROJ_PAD - PROJ_COLS), (0, 0)))
    if name == "mla_w_q_b":
        f = jnp.pad(f.reshape(MLA_HEADS, MLA_QK, -1), ((0, 0), (0, MLA_PAD - MLA_QK), (0, 0)))
        f = f.reshape(MLA_HEADS * MLA_PAD, -1)
    return f


def _to_parts(name, f):
    if name == "mla_w_kv_b":
        return _full_to_cols(f).astype(BF16)
    if name == "w_in":
        f = f[:PROJ_COLS]
    if name == "mla_w_q_b":
        f = f.reshape(MLA_HEADS, MLA_PAD, -1)[:, :MLA_QK].reshape(MLA_HEADS * MLA_QK, -1)
    return f.reshape(N_DEV, -1, f.shape[-1]).astype(BF16)


class _Comm:
    def __init__(self, shards):
        self.shards, self.w, self.recv = shards, {}, {}

    def gather(self, names):
        return _Ride([self.shards[n] for n in names], [False] * len(names))

    def scatter(self, names, grads):
        return _Ride([_to_parts(n, grads[n]) for n in names], [True] * len(names))

    def weights_landed(self, names, got):
        self.w.update({n: _to_full(n, g) for n, g in zip(names, got)})

    def grads_landed(self, names, got):
        self.recv.update(zip(names, got))


def _pack_small(parts, extra):
    flat = jnp.concatenate([parts[n].reshape(-1) for n in _SMALL] + [extra.reshape(-1)])
    return jnp.pad(flat, (0, _SMALL_ROWS * 128 - flat.shape[0])).reshape(_SMALL_ROWS, 128)


def _unpack_small(packed, shapes):
    flat, out, off = packed.reshape(-1), {}, 0
    for n in _SMALL:
        size = math.prod(shapes[n])
        out[n] = flat[off:off + size].reshape(shapes[n])
        off += size
    return out, flat[off]


_NAMES = ("ffn1_norm", "ffn1_w_gate", "ffn1_w_up", "ffn1_w_down", "mix_norm", "w_in", "dil_q_norm", "dil_k_norm",
          "rel_bias", "mla_q_a_norm", "mla_w_q_b", "mla_kv_a_norm", "mla_w_kv_b", "mla_q_norm", "mla_k_norm",
          "out_norm_dil", "out_norm_mla", "w_out", "ffn2_norm", "ffn2_w_gate", "ffn2_w_up", "ffn2_w_down")


def kernel(x, ffn1_norm, ffn1_w_gate, ffn1_w_up, ffn1_w_down, mix_norm, w_in, dil_q_norm, dil_k_norm, rel_bias, mla_q_a_norm, mla_w_q_b, mla_kv_a_norm, mla_w_kv_b, mla_q_norm, mla_k_norm, out_norm_dil, out_norm_mla, w_out, ffn2_norm, ffn2_w_gate, ffn2_w_up, ffn2_w_down, loss_target, m_ffn1_norm, m_ffn1_w_gate, m_ffn1_w_up, m_ffn1_w_down, m_mix_norm, m_w_in, m_dil_q_norm, m_dil_k_norm, m_rel_bias, m_mla_q_a_norm, m_mla_w_q_b, m_mla_kv_a_norm, m_mla_w_kv_b, m_mla_q_norm, m_mla_k_norm, m_out_norm_dil, m_out_norm_mla, m_w_out, m_ffn2_norm, m_ffn2_w_gate, m_ffn2_w_up, m_ffn2_w_down, v_ffn1_norm, v_ffn1_w_gate, v_ffn1_w_up, v_ffn1_w_down, v_mix_norm, v_w_in, v_dil_q_norm, v_dil_k_norm, v_rel_bias, v_mla_q_a_norm, v_mla_w_q_b, v_mla_kv_a_norm, v_mla_w_kv_b, v_mla_q_norm, v_mla_k_norm, v_out_norm_dil, v_out_norm_mla, v_w_out, v_ffn2_norm, v_ffn2_w_gate, v_ffn2_w_up, v_ffn2_w_down):
    args = locals()
    wts = {n: args[n] for n in _NAMES}
    mom = {n: args["m_" + n] for n in _NAMES}
    var = {n: args["v_" + n] for n in _NAMES}

    matrices = [n for group in _GROUPS.values() for n in group]
    comm = _Comm({n: _shard_view(n, wts[n])[0].astype(BF16) for n in matrices})
    comm.weights_landed(_GROUPS["ffn1"], _gather_two_level(comm.gather(_GROUPS["ffn1"]).arrays, "gather_first"))
    small = {n: wts[n].reshape(1, -1) if n != "rel_bias" else wts[n] for n in _SMALL}

    loss, grad_x, gw, gs = _local_step(x[0], loss_target[0], small, comm)

    last = comm.scatter(("ffn1_w_up",), gw)
    got = _exchange(_Ride(last.arrays + [_pack_small(gs, loss[0, 0])], last.scatter + [False]), "scatter_last")
    comm.grads_landed(("ffn1_w_up",), got[:-1])

    res = {n: [_shard_view(n, r) for r in _adamw(comm.recv[n], *(_shard_view(n, a[n]) for a in (wts, mom, var)))]
           for n in matrices}
    shapes = {n: wts[n].shape for n in _SMALL}
    zero = jnp.zeros((), F32)
    packed = _adamw(got[-1], _pack_small(wts, zero)[None], _pack_small(mom, zero)[None], _pack_small(var, zero)[None])
    loss_total = None
    for slot, q in enumerate(packed):
        vals, extra = _unpack_small(q, shapes)
        if slot == 0:
            loss_total = extra
        for n in _SMALL:
            res.setdefault(n, [None] * 4)[slot] = vals[n]
    outs = [loss_total, grad_x[None]]
    for slot in range(4):
        outs += [res[n][slot].reshape(wts[n].shape) for n in _NAMES]
    return tuple(outs)
```

```python
import math

import numpy as np
import jax
import jax.numpy as jnp
from jax import lax
from jax.experimental import pallas as pl
from jax.experimental.pallas import tpu as pltpu

F32, BF16 = jnp.float32, jnp.bfloat16
EPS = 1e-6
NEG = -1e30
N_DEV = 8

DIL_HEADS, DIL_HD = 8, 64
DIL_WIDTH = DIL_HEADS * DIL_HD
DIL_BRANCHES = ((128, 1), (512, 4), (2048, 16))
DIL_BLOCK = 128
MLA_HEADS, MLA_NOPE, MLA_ROPE, MLA_V = 4, 128, 64, 128
MLA_QK = MLA_NOPE + MLA_ROPE
MLA_PAD = 256
ROPE_BASE = 10000.0
REL_BUCKETS, REL_MAX_DIST = 32, 2048
PROJ_COLS, PROJ_PAD = 1984, 2048
FFN_RESID = 0.5
ADAM_LR, ADAM_B1, ADAM_B2, ADAM_EPS, ADAM_WD, ADAM_STEP = 0.001, 0.9, 0.999, 1e-08, 0.01, 10
VMEM_LIMIT = 62 * 1024 * 1024

_NT = (((1,), (1,)), ((), ()))
_TN = (((0,), (0,)), ((), ()))


def _dot(a, b):
    return jnp.dot(a, b, preferred_element_type=F32)


def _dot_nt(a, b):
    return lax.dot_general(a, b, _NT, preferred_element_type=F32)


def _dot_tn(a, b):
    return lax.dot_general(a, b, _TN, preferred_element_type=F32)


def _params(n_axes):
    return pltpu.CompilerParams(dimension_semantics=("arbitrary",) * n_axes, vmem_limit_bytes=VMEM_LIMIT)


def _rstd(x, n=None):
    n = x.shape[-1] if n is None else n
    return lax.rsqrt(jnp.sum(x * x, axis=-1, keepdims=True) / n + EPS)


def _rms_bwd(dy, x, g, r, n=None):
    n = x.shape[-1] if n is None else n
    u = dy * g
    dx = r * u - x * (r * r * r) * (jnp.sum(u * x, axis=-1, keepdims=True) / n)
    return dx, dy * x * r


def _sigmoid(x):
    return 1.0 / (1.0 + jnp.exp(-x))


def _split3(x):
    parts = []
    for _ in range(3):
        xb = x.astype(BF16)
        parts.append(xb)
        x = x - xb.astype(F32)
    return parts


def _ffn_fwd(x, gain, wg, wu, wd, ride=None, target=None, tm=512, tf=2816):
    T, D = x.shape
    F = wg.shape[0]
    ni, nj = T // tm, F // tf
    with_loss = target is not None
    r_args, r_in, r_shape, r_out, r_scratch = _ride_parts(ride)

    def body(*refs):
        x_ref, g_ref, wg_ref, wu_ref, wd_ref = refs[:5]
        t_ref = refs[5] if with_loss else None
        xo_ref, h_ref, gate_ref, up_ref = refs[5 + with_loss:9 + with_loss]
        loss_ref = refs[-2] if with_loss else None
        acc = refs[-1]
        i, j = pl.program_id(0), pl.program_id(1)

        @pl.when(j == 0)
        def _():
            xv = x_ref[...]
            h_ref[...] = (xv * _rstd(xv) * g_ref[...]).astype(BF16)
            acc[...] = jnp.zeros_like(acc)

        h = h_ref[...]
        g = _dot_nt(h, wg_ref[...])
        u = _dot_nt(h, wu_ref[...])
        gate_ref[...] = g.astype(BF16)
        up_ref[...] = u.astype(BF16)
        a = (g * _sigmoid(g) * u).astype(BF16)
        acc[...] += _dot(a, wd_ref[...])

        @pl.when(j == nj - 1)
        def _():
            y = x_ref[...] + FFN_RESID * acc[...]
            if with_loss:
                @pl.when(i == 0)
                def _():
                    loss_ref[...] = jnp.zeros_like(loss_ref)

                e = y - t_ref[...]
                xo_ref[...] = e * (1.0 / D)
                loss_ref[...] += (0.5 / D) * jnp.sum(e * e)
            else:
                xo_ref[...] = y

    row = lambda i, j: (i, 0)
    tile = lambda i, j: (i, j)
    n_in, n_out = 5 + with_loss, 4 + with_loss
    first = lambda: (pl.program_id(0) == 0) & (pl.program_id(1) == 0)
    last = lambda: (pl.program_id(0) == ni - 1) & (pl.program_id(1) == nj - 1)
    outs = pl.pallas_call(
        _riding(body, n_in, n_out, 1, ride, first, last), name="ffn_fwd", grid=(ni, nj),
        in_specs=[pl.BlockSpec((tm, D), row), pl.BlockSpec((1, D), lambda i, j: (0, 0)),
                  pl.BlockSpec((tf, D), lambda i, j: (j, 0)), pl.BlockSpec((tf, D), lambda i, j: (j, 0)),
                  pl.BlockSpec((tf, D), lambda i, j: (j, 0))] + [pl.BlockSpec((tm, D), row)] * with_loss + r_in,
        out_specs=[pl.BlockSpec((tm, D), row), pl.BlockSpec((tm, D), row), pl.BlockSpec((tm, tf), tile),
                   pl.BlockSpec((tm, tf), tile)] + [pl.BlockSpec((1, 128), lambda i, j: (0, 0))] * with_loss + r_out,
        out_shape=[jax.ShapeDtypeStruct((T, D), F32), jax.ShapeDtypeStruct((T, D), BF16),
                   jax.ShapeDtypeStruct((T, F), BF16), jax.ShapeDtypeStruct((T, F), BF16)]
        + [jax.ShapeDtypeStruct((1, 128), F32)] * with_loss + r_shape,
        scratch_shapes=[pltpu.VMEM((tm, D), F32)] + r_scratch,
        compiler_params=_params(2),
    )(x, gain, wg, wu, wd, *([target] if with_loss else []), *r_args)
    return outs[:n_out], outs[n_out:]


def _ffn_bwd(dy, x, gain, gate, up, wg, wu, wd, ride=None, tm=256, tf=2816):
    T, D = x.shape
    F = wg.shape[0]
    ni, nj = T // tm, F // tf
    r_args, r_in, r_shape, r_out, r_scratch = _ride_parts(ride)

    def body(dy_ref, x_ref, g_ref, gate_ref, up_ref, wg_ref, wu_ref, wd_ref,
             dx_ref, a_ref, dg_ref, du_ref, dyh_ref, dgain_ref, acc):
        i, j = pl.program_id(0), pl.program_id(1)

        @pl.when((i == 0) & (j == 0))
        def _():
            dgain_ref[...] = jnp.zeros_like(dgain_ref)

        @pl.when(j == 0)
        def _():
            dyh_ref[...] = (FFN_RESID * dy_ref[...]).astype(BF16)
            acc[...] = jnp.zeros_like(acc)

        da = _dot_nt(dyh_ref[...], wd_ref[...])
        g = gate_ref[...].astype(F32)
        u = up_ref[...].astype(F32)
        sig = _sigmoid(g)
        s = g * sig
        a_ref[...] = (s * u).astype(BF16)
        dg = (da * u * (sig * (1.0 + g * (1.0 - sig)))).astype(BF16)
        du = (da * s).astype(BF16)
        dg_ref[...] = dg
        du_ref[...] = du
        acc[...] += _dot(dg, wg_ref[...]) + _dot(du, wu_ref[...])

        @pl.when(j == nj - 1)
        def _():
            xv = x_ref[...]
            dxn, dgc = _rms_bwd(acc[...], xv, g_ref[...], _rstd(xv))
            dx_ref[...] = dy_ref[...] + dxn
            dgain_ref[...] += jnp.sum(dgc, axis=0, keepdims=True)

    first = lambda: (pl.program_id(0) == 0) & (pl.program_id(1) == 0)
    last = lambda: (pl.program_id(0) == ni - 1) & (pl.program_id(1) == nj - 1)
    outs = pl.pallas_call(
        _riding(body, 8, 6, 1, ride, first, last), name="ffn_bwd", grid=(ni, nj),
        in_specs=[pl.BlockSpec((tm, D), lambda i, j: (i, 0)), pl.BlockSpec((tm, D), lambda i, j: (i, 0)),
                  pl.BlockSpec((1, D), lambda i, j: (0, 0)),
                  pl.BlockSpec((tm, tf), lambda i, j: (i, j)), pl.BlockSpec((tm, tf), lambda i, j: (i, j)),
                  pl.BlockSpec((tf, D), lambda i, j: (j, 0)), pl.BlockSpec((tf, D), lambda i, j: (j, 0)),
                  pl.BlockSpec((tf, D), lambda i, j: (j, 0))] + r_in,
        out_specs=[pl.BlockSpec((tm, D), lambda i, j: (i, 0)),
                   pl.BlockSpec((tm, tf), lambda i, j: (i, j)), pl.BlockSpec((tm, tf), lambda i, j: (i, j)),
                   pl.BlockSpec((tm, tf), lambda i, j: (i, j)),
                   pl.BlockSpec((tm, D), lambda i, j: (i, 0)), pl.BlockSpec((1, D), lambda i, j: (0, 0))] + r_out,
        out_shape=[jax.ShapeDtypeStruct((T, D), F32), jax.ShapeDtypeStruct((T, F), BF16),
                   jax.ShapeDtypeStruct((T, F), BF16), jax.ShapeDtypeStruct((T, F), BF16),
                   jax.ShapeDtypeStruct((T, D), BF16), jax.ShapeDtypeStruct((1, D), F32)] + r_shape,
        scratch_shapes=[pltpu.VMEM((tm, D), F32)] + r_scratch,
        compiler_params=_params(2),
    )(dy, x, gain, gate, up, wg, wu, wd, *r_args)
    return outs[:6], outs[6:]


def _matmul_tn(a, b, tk, tn, ride=None, tt=2048):
    T, K = a.shape
    N = b.shape[1]
    tk, tn = min(tk, K), min(tn, N)
    grid = (K // tk, N // tn, T // tt)
    r_args, r_in, r_shape, r_out, r_scratch = _ride_parts(ride)

    def body(a_ref, b_ref, o_ref, acc):
        t = pl.program_id(2)

        @pl.when(t == 0)
        def _():
            acc[...] = jnp.zeros_like(acc)

        acc[...] += _dot_tn(a_ref[...].astype(BF16), b_ref[...].astype(BF16))

        @pl.when(t == grid[2] - 1)
        def _():
            o_ref[...] = acc[...].astype(BF16)

    first = lambda: (pl.program_id(0) == 0) & (pl.program_id(1) == 0) & (pl.program_id(2) == 0)
    last = lambda: ((pl.program_id(0) == grid[0] - 1) & (pl.program_id(1) == grid[1] - 1)
                    & (pl.program_id(2) == grid[2] - 1))
    outs = pl.pallas_call(
        _riding(body, 2, 1, 1, ride, first, last), name="matmul_tn", grid=grid,
        in_specs=[pl.BlockSpec((tt, tk), lambda k, n, t: (t, k)), pl.BlockSpec((tt, tn), lambda k, n, t: (t, n))] + r_in,
        out_specs=[pl.BlockSpec((tk, tn), lambda k, n, t: (k, n))] + r_out,
        out_shape=[jax.ShapeDtypeStruct((K, N), BF16)] + r_shape,
        scratch_shapes=[pltpu.VMEM((tk, tn), F32)] + r_scratch,
        compiler_params=_params(3),
    )(a, b, *r_args)
    return outs[0], outs[1:]


def _in_proj(x, gain, w, gq, gk, tm=512):
    T, D = x.shape
    N = w.shape[0]
    W = DIL_WIDTH

    def body(x_ref, g_ref, w_ref, gq_ref, gk_ref, h_ref, p_ref, qh_ref, kh_ref):
        xv = x_ref[...]
        h = (xv * _rstd(xv) * g_ref[...]).astype(BF16)
        h_ref[...] = h
        p_ref[...] = _dot_nt(h, w_ref[...])
        lo = lax.broadcasted_iota(jnp.int32, (tm, 128), 1) < DIL_HD
        for hp in range(DIL_HEADS // 2):
            q = p_ref[:, 128 * hp:128 * (hp + 1)]
            k = p_ref[:, W + 128 * hp:W + 128 * (hp + 1)]
            qh_ref[:, 128 * hp:128 * (hp + 1)] = (q * _pair_rstd(q, lo) * gq_ref[...]).astype(BF16).astype(F32)
            kh_ref[:, 128 * hp:128 * (hp + 1)] = (k * _pair_rstd(k, lo) * gk_ref[...]).astype(BF16).astype(F32)

    row = lambda i: (i, 0)
    fix = lambda i: (0, 0)
    return pl.pallas_call(
        body, name="in_proj", grid=(T // tm,),
        in_specs=[pl.BlockSpec((tm, D), row), pl.BlockSpec((1, D), fix), pl.BlockSpec((N, D), fix),
                  pl.BlockSpec((1, 128), fix), pl.BlockSpec((1, 128), fix)],
        out_specs=[pl.BlockSpec((tm, D), row), pl.BlockSpec((tm, N), row), pl.BlockSpec((tm, W), row),
                   pl.BlockSpec((tm, W), row)],
        out_shape=[jax.ShapeDtypeStruct((T, D), BF16), jax.ShapeDtypeStruct((T, N), F32),
                   jax.ShapeDtypeStruct((T, W), F32), jax.ShapeDtypeStruct((T, W), F32)],
        compiler_params=_params(1),
    )(x, gain, w, gq, gk)


def _in_proj_bwd(dx_up, x, gain, w, proj, gq, gk, dqkv, dcq, dckv, dkpe, ride=None, tm=512):
    T, D = x.shape
    N = w.shape[0]
    W = DIL_WIDTH
    nb = len(dqkv)

    def body(*refs):
        dxu_ref, x_ref, g_ref, w_ref, q_ref, k_ref, gq_ref, gk_ref = refs[:8]
        dil_refs = refs[8:8 + 3 * nb]
        dcq_ref, dckv_ref, dkpe_ref, dx_ref, dp_ref, dgain_ref, dgq_ref, dgk_ref = refs[8 + 3 * nb:]

        @pl.when(pl.program_id(0) == 0)
        def _():
            for ref in (dgain_ref, dgq_ref, dgk_ref):
                ref[...] = jnp.zeros_like(ref)

        lo = lax.broadcasted_iota(jnp.int32, (tm, 128), 1) < DIL_HD
        norms = ((q_ref, gq_ref, dgq_ref), (k_ref, gk_ref, dgk_ref))
        for part in range(3):
            acc = dil_refs[part][...]
            for b in range(1, nb):
                acc = acc + dil_refs[3 * b + part][...]
            if part == 2:
                dp_ref[:, 2 * W:3 * W] = acc.astype(BF16)
                continue
            raw_ref, gn_ref, dgn_ref = norms[part]
            for hp in range(DIL_HEADS // 2):
                raw = raw_ref[:, 128 * hp:128 * (hp + 1)]
                d_raw, dgn = _pair_rms_bwd(acc[:, 128 * hp:128 * (hp + 1)], raw, _pair_rstd(raw, lo), gn_ref[...], lo)
                dp_ref[:, part * W + 128 * hp:part * W + 128 * (hp + 1)] = d_raw.astype(BF16)
                dgn_ref[...] += dgn
        dp_ref[:, 3 * W:3 * W + 256] = dcq_ref[...].astype(BF16)
        dp_ref[:, 3 * W + 256:3 * W + 384] = dckv_ref[...].astype(BF16)
        dp_ref[:, 3 * W + 384:N] = dkpe_ref[...].astype(BF16)
        dh = _dot(dp_ref[...], w_ref[...])
        xv = x_ref[...]
        dxn, dgc = _rms_bwd(dh, xv, g_ref[...], _rstd(xv))
        dx_ref[...] = dxu_ref[...] + dxn
        dgain_ref[...] += jnp.sum(dgc, axis=0, keepdims=True)

    row = lambda i: (i, 0)
    fix = lambda i: (0, 0)
    r_args, r_in, r_shape, r_out, r_scratch = _ride_parts(ride)
    first = lambda: pl.program_id(0) == 0
    last = lambda: pl.program_id(0) == T // tm - 1
    outs = pl.pallas_call(
        _riding(body, 11 + 3 * nb, 5, 0, ride, first, last), name="in_proj_bwd", grid=(T // tm,),
        in_specs=[pl.BlockSpec((tm, D), row), pl.BlockSpec((tm, D), row), pl.BlockSpec((1, D), fix),
                  pl.BlockSpec((N, D), fix), pl.BlockSpec((tm, W), row), pl.BlockSpec((tm, W), lambda i: (i, 1)),
                  pl.BlockSpec((1, 128), fix), pl.BlockSpec((1, 128), fix)] + [pl.BlockSpec((tm, W), row)] * (3 * nb)
                 + [pl.BlockSpec((tm, 256), row), pl.BlockSpec((tm, 128), row), pl.BlockSpec((tm, 128), row)] + r_in,
        out_specs=[pl.BlockSpec((tm, D), row), pl.BlockSpec((tm, N), row), pl.BlockSpec((1, D), fix),
                   pl.BlockSpec((1, 128), fix), pl.BlockSpec((1, 128), fix)] + r_out,
        out_shape=[jax.ShapeDtypeStruct((T, D), F32), jax.ShapeDtypeStruct((T, N), BF16),
                   jax.ShapeDtypeStruct((1, D), F32), jax.ShapeDtypeStruct((1, 128), F32),
                   jax.ShapeDtypeStruct((1, 128), F32)] + r_shape,
        scratch_shapes=r_scratch,
        compiler_params=_params(1),
    )(dx_up, x, gain, w, proj, proj, gq, gk, *[a for triple in dqkv for a in triple], dcq, dckv, dkpe, *r_args)
    return outs[:5], outs[5:]


def _out_proj(x, o_dil, o_mla, g_dil, g_mla, w, tm=512):
    T, D = x.shape
    W = o_dil.shape[1]

    def body(x_ref, od_ref, om_ref, gd_ref, gm_ref, w_ref, xo_ref, oc_ref):
        od, om = od_ref[...], om_ref[...]
        oc_ref[:, 0:W] = (od * _rstd(od) * gd_ref[...]).astype(BF16)
        oc_ref[:, W:2 * W] = (om * _rstd(om) * gm_ref[...]).astype(BF16)
        xo_ref[...] = x_ref[...] + _dot(oc_ref[...], w_ref[...])

    row = lambda i: (i, 0)
    fix = lambda i: (0, 0)
    return pl.pallas_call(
        body, name="out_proj", grid=(T // tm,),
        in_specs=[pl.BlockSpec((tm, D), row), pl.BlockSpec((tm, W), row), pl.BlockSpec((tm, W), row),
                  pl.BlockSpec((1, W), fix), pl.BlockSpec((1, W), fix), pl.BlockSpec((2 * W, D), fix)],
        out_specs=[pl.BlockSpec((tm, D), row), pl.BlockSpec((tm, 2 * W), row)],
        out_shape=[jax.ShapeDtypeStruct((T, D), F32), jax.ShapeDtypeStruct((T, 2 * W), BF16)],
        compiler_params=_params(1),
    )(x, o_dil, o_mla, g_dil, g_mla, w)


def _out_proj_bwd(dx, o_dil, o_mla, g_dil, g_mla, w, tm=512):
    T, D = dx.shape
    W = o_dil.shape[1]

    def body(dx_ref, od_ref, om_ref, gd_ref, gm_ref, w_ref, dod_ref, dom_ref, dgd_ref, dgm_ref):
        @pl.when(pl.program_id(0) == 0)
        def _():
            dgd_ref[...] = jnp.zeros_like(dgd_ref)
            dgm_ref[...] = jnp.zeros_like(dgm_ref)

        doc = _dot_nt(dx_ref[...].astype(BF16), w_ref[...])
        od, om = od_ref[...], om_ref[...]
        dod, dgd = _rms_bwd(doc[:, 0:W], od, gd_ref[...], _rstd(od))
        dom, dgm = _rms_bwd(doc[:, W:2 * W], om, gm_ref[...], _rstd(om))
        dod_ref[...] = dod
        dom_ref[...] = dom
        dgd_ref[...] += jnp.sum(dgd, axis=0, keepdims=True)
        dgm_ref[...] += jnp.sum(dgm, axis=0, keepdims=True)

    row = lambda i: (i, 0)
    fix = lambda i: (0, 0)
    return pl.pallas_call(
        body, name="out_proj_bwd", grid=(T // tm,),
        in_specs=[pl.BlockSpec((tm, D), row), pl.BlockSpec((tm, W), row), pl.BlockSpec((tm, W), row),
                  pl.BlockSpec((1, W), fix), pl.BlockSpec((1, W), fix), pl.BlockSpec((2 * W, D), fix)],
        out_specs=[pl.BlockSpec((tm, W), row), pl.BlockSpec((tm, W), row),
                   pl.BlockSpec((1, W), fix), pl.BlockSpec((1, W), fix)],
        out_shape=[jax.ShapeDtypeStruct((T, W), F32), jax.ShapeDtypeStruct((T, W), F32),
                   jax.ShapeDtypeStruct((1, W), F32), jax.ShapeDtypeStruct((1, W), F32)],
        compiler_params=_params(1),
    )(dx, o_dil, o_mla, g_dil, g_mla, w)


def _pair_rstd(x, lo):
    sq = x * x
    s0 = jnp.sum(jnp.where(lo, sq, 0.0), axis=-1, keepdims=True)
    s1 = jnp.sum(jnp.where(lo, 0.0, sq), axis=-1, keepdims=True)
    return jnp.where(lo, lax.rsqrt(s0 / DIL_HD + EPS), lax.rsqrt(s1 / DIL_HD + EPS))


def _pair_rms_bwd(dn, x, r, g, lo):
    u = dn * g
    t = u * x
    d0 = jnp.sum(jnp.where(lo, t, 0.0), axis=-1, keepdims=True)
    d1 = jnp.sum(jnp.where(lo, 0.0, t), axis=-1, keepdims=True)
    dx = r * u - x * (r * r * r) * (jnp.where(lo, d0, d1) / DIL_HD)
    return dx, jnp.sum(dn * x * r, axis=0, keepdims=True)


def _pair_col(x, lo, e):
    sel = lo if e == 0 else jnp.logical_not(lo)
    return jnp.max(jnp.where(sel, x, NEG), axis=-1, keepdims=True)


def _dil_masks(n):
    lo = lax.broadcasted_iota(jnp.int32, (DIL_BLOCK, DIL_BLOCK), 1) < DIL_HD
    row = lax.broadcasted_iota(jnp.int32, (2 * DIL_BLOCK, 2 * DIL_BLOCK), 0) % DIL_BLOCK
    col = lax.broadcasted_iota(jnp.int32, (2 * DIL_BLOCK, 2 * DIL_BLOCK), 1)
    prev = jnp.logical_and(jnp.logical_and(col < DIL_BLOCK, col >= row), n > 0)
    cur = jnp.logical_and(col >= DIL_BLOCK, col - DIL_BLOCK <= row)
    return lo, jnp.logical_or(prev, cur)


def _stack_heads(x, lo):
    return jnp.concatenate([jnp.where(lo, x, 0.0), jnp.where(lo, 0.0, x)], axis=0)


def _unstack_heads(x2, lo):
    return jnp.where(lo, x2[:DIL_BLOCK], x2[DIL_BLOCK:])


def _dil_pairs(d):
    return 4 if d == 1 else 1


def _sub_rows(r, d):
    return pl.ds(r, DIL_BLOCK, stride=d) if d > 1 else pl.ds(0, DIL_BLOCK)


def _split_subsequences(loads, d, P):
    for r in range(d):
        for p in range(P):
            for block, scratch, part in loads:
                piece = block[_sub_rows(r, d), pl.ds(128 * p, 128)]
                if part is None:
                    scratch[r * P + p] = piece
                else:
                    scratch[r * P + p, pl.ds(DIL_BLOCK * part, DIL_BLOCK), :] = piece


def _keep_previous_block(scratches, n):
    for scratch in scratches:
        @pl.when(n == 0)
        def _():
            scratch[:, pl.ds(0, DIL_BLOCK), :] = jnp.zeros((scratch.shape[0], DIL_BLOCK, 128), F32)

        @pl.when(n > 0)
        def _():
            scratch[:, pl.ds(0, DIL_BLOCK), :] = scratch[:, pl.ds(DIL_BLOCK, DIL_BLOCK), :]


def _merge_subsequences(stores, d, P):
    for r in range(d):
        for p in range(P):
            for block, scratch, plus in stores:
                part = _sub_rows(r, d), pl.ds(128 * p, 128)
                block[part] = scratch[r * P + p] if plus is None else scratch[r * P + p] + plus[part]


def _dil_fwd(qh, kh, proj, bias, d, prev):
    T = proj.shape[0]
    P = _dil_pairs(d)
    rows, cw, n_it = DIL_BLOCK * d, 128 * P, d * P
    nblk = T // rows
    has_prev = prev is not None

    def body(*refs):
        q_ref, kc_ref, vc_ref, bias_ref = refs[:4]
        refs = refs[4:]
        if has_prev:
            oin_ref, lin_ref = refs[:2]
            refs = refs[2:]
        o_ref, l_ref, qs, ks, vs, os_, ls_ = refs[:7]
        pb, n = pl.program_id(0), pl.program_id(1)
        lo, valid = _dil_masks(n)
        _keep_previous_block((ks, vs), n)
        loads = [(q_ref, qs, None), (kc_ref, ks, 1), (vc_ref, vs, 1)]
        if has_prev:
            ois, lis = refs[7:]
            loads += [(oin_ref, ois, None), (lin_ref, lis, None)]
        _split_subsequences(loads, d, P)

        def step(i, carry):
            q2 = _stack_heads(qs[i], lo).astype(BF16)
            s = jnp.where(valid, _dot_nt(q2, ks[i].astype(BF16)) + bias_ref[pb * P + i % P], NEG)
            m = jnp.max(s, axis=-1, keepdims=True)
            p = jnp.exp(s - m)
            l = jnp.sum(p, axis=-1, keepdims=True)
            o = _unstack_heads(_dot(p.astype(BF16), vs[i].astype(BF16)) / l, lo)
            lse = _unstack_heads(jnp.broadcast_to(m + jnp.log(l), (2 * DIL_BLOCK, 128)), lo)
            if has_prev:
                lin = lis[i]
                mx = jnp.maximum(lin, lse)
                lnew = mx + jnp.log(jnp.exp(lin - mx) + jnp.exp(lse - mx))
                o = ois[i] * jnp.exp(lin - lnew) + o * jnp.exp(lse - lnew)
                lse = lnew
            os_[i] = o
            ls_[i] = lse
            return carry

        lax.fori_loop(0, n_it, step, 0, unroll=4)
        _merge_subsequences([(o_ref, os_, None), (l_ref, ls_, None)], d, P)

    blk = (rows, cw)
    vcol = 2 * DIL_WIDTH // cw
    fix3 = lambda pb, n: (0, 0, 0)
    tok = pl.BlockSpec(blk, lambda pb, n: (n, pb))
    in_specs = [tok, tok, pl.BlockSpec(blk, lambda pb, n: (n, vcol + pb)),
                pl.BlockSpec((DIL_HEADS // 2, 2 * DIL_BLOCK, 2 * DIL_BLOCK), fix3)]
    args = [qh, kh, proj, bias]
    one, two = pltpu.VMEM((n_it, DIL_BLOCK, 128), F32), pltpu.VMEM((n_it, 2 * DIL_BLOCK, 128), F32)
    scratch = [one, two, two, one, one]
    if has_prev:
        in_specs += [tok, tok]
        args += list(prev)
        scratch += [one, one]
    out = jax.ShapeDtypeStruct((T, DIL_WIDTH), F32)
    return pl.pallas_call(
        body, name=f"dil_fwd_d{d}", grid=(DIL_HEADS // 2 // P, nblk), in_specs=in_specs, out_specs=[tok, tok],
        out_shape=[out, out], scratch_shapes=scratch, compiler_params=_params(2),
    )(*args)


def _dil_bwd(qh, kh, proj, o, lse, do, bias, d, prev):
    T = proj.shape[0]
    P = _dil_pairs(d)
    rows, cw, n_it = DIL_BLOCK * d, 128 * P, d * P
    nblk = T // rows
    has_prev = prev is not None

    def body(*refs):
        q_ref, kc_ref, vc_ref, o_ref, l_ref, do_ref, bias_ref = refs[:7]
        dqi_ref, dki_ref, dvi_ref = refs[7:10] if has_prev else (None, None, None)
        dq_ref, dk_ref, dv_ref, db_ref, qs, ks, vs, os_, ls_, dos, dqs, dks, dvs, ck, cv = refs[7 + 3 * has_prev:]
        pb, n = pl.program_id(0), pl.program_id(1)
        lo, valid = _dil_masks(n)

        @pl.when((pb == 0) & (n == 0))
        def _():
            db_ref[...] = jnp.zeros_like(db_ref)

        @pl.when(n == 0)
        def _():
            ck[...] = jnp.zeros_like(ck)
            cv[...] = jnp.zeros_like(cv)

        _keep_previous_block((ks, vs), n)
        _split_subsequences([(q_ref, qs, None), (kc_ref, ks, 1), (vc_ref, vs, 1),
                             (o_ref, os_, None), (l_ref, ls_, None), (do_ref, dos, None)], d, P)

        def step(i, carry):
            pair = pb * P + i % P
            q2 = _stack_heads(qs[i], lo).astype(BF16)
            kcat, vcat = ks[i].astype(BF16), vs[i].astype(BF16)
            dov = dos[i]
            do2 = _stack_heads(dov, lo).astype(BF16)
            delta = jnp.sum(_stack_heads(dov * os_[i], lo), axis=-1, keepdims=True)
            lse_pair = ls_[i]
            lse2 = jnp.concatenate([_pair_col(lse_pair, lo, 0), _pair_col(lse_pair, lo, 1)], axis=0)
            s = jnp.where(valid, _dot_nt(q2, kcat) + bias_ref[pair], NEG)
            p = jnp.exp(s - lse2)
            ds = p * (_dot_nt(do2, vcat) - delta)
            db_ref[pair] += ds
            dsb = ds.astype(BF16)
            dqs[i] = _unstack_heads(_dot(dsb, kcat), lo)
            dk2 = _dot_tn(dsb, q2)
            dv2 = _dot_tn(p.astype(BF16), do2)
            dks[i] = ck[i] + dk2[:DIL_BLOCK]
            dvs[i] = cv[i] + dv2[:DIL_BLOCK]
            ck[i] = dk2[DIL_BLOCK:]
            cv[i] = dv2[DIL_BLOCK:]
            return carry

        @pl.when(n < nblk)
        def _():
            lax.fori_loop(0, n_it, step, 0, unroll=2)
            _merge_subsequences([(dq_ref, dqs, dqi_ref), (dk_ref, dks, dki_ref), (dv_ref, dvs, dvi_ref)], d, P)

        @pl.when(n == nblk)
        def _():
            _merge_subsequences([(dk_ref, ck, dki_ref), (dv_ref, cv, dvi_ref)], d, P)

    blk = (rows, cw)
    vcol = 2 * DIL_WIDTH // cw
    qn_ = lambda n: jnp.minimum(n, nblk - 1)
    pn_ = lambda n: jnp.maximum(n - 1, 0)
    fix3 = lambda pb, n: (0, 0, 0)
    tok_q = pl.BlockSpec(blk, lambda pb, n: (qn_(n), pb))
    tok_p = pl.BlockSpec(blk, lambda pb, n: (pn_(n), pb))
    bias_spec = pl.BlockSpec((DIL_HEADS // 2, 2 * DIL_BLOCK, 2 * DIL_BLOCK), fix3)
    in_specs = [tok_q, tok_q, pl.BlockSpec(blk, lambda pb, n: (qn_(n), vcol + pb)), tok_q, tok_q, tok_q, bias_spec]
    in_specs += [tok_q, tok_p, tok_p] if has_prev else []
    tok_shape = jax.ShapeDtypeStruct((T, DIL_WIDTH), F32)
    one, two = pltpu.VMEM((n_it, DIL_BLOCK, 128), F32), pltpu.VMEM((n_it, 2 * DIL_BLOCK, 128), F32)
    dq, dk, dv, db = pl.pallas_call(
        body, name=f"dil_bwd_d{d}", grid=(DIL_HEADS // 2 // P, nblk + 1), in_specs=in_specs,
        out_specs=[tok_q, tok_p, tok_p, bias_spec],
        out_shape=[tok_shape, tok_shape, tok_shape, jax.ShapeDtypeStruct(bias.shape, F32)],
        scratch_shapes=[one, two, two] + [one] * 8,
        compiler_params=_params(2),
    )(qh, kh, proj, o, lse, do, bias, *(prev or ()))
    return (dq, dk, dv), db


def _t5_bucket(dist):
    max_exact = REL_BUCKETS // 2
    dd = np.maximum(dist, 1).astype(np.float32)
    large = max_exact + (np.log(dd / max_exact) / np.log(REL_MAX_DIST / max_exact)
                         * (REL_BUCKETS - max_exact)).astype(np.int32)
    large = np.minimum(large, REL_BUCKETS - 1)
    return np.where(dist < max_exact, dist, large).astype(np.int32)


def _bucket_onehots():
    i = np.arange(DIL_BLOCK)[:, None]
    j = np.arange(DIL_BLOCK)[None, :]
    out = []
    for _, d in DIL_BRANCHES:
        dist = np.concatenate([DIL_BLOCK + i - j, i - j], axis=1)
        bucket = _t5_bucket(np.clip(dist, 0, None) * d).reshape(-1)
        out.append(jnp.asarray(np.eye(REL_BUCKETS, dtype=np.float32)[:, bucket], BF16))
    return out


def _bias_tables(rel_bias, onehots):
    n = len(onehots)

    def body(rb_ref, *refs):
        parts = _split3(rb_ref[...])
        for k in range(n):
            oh = refs[k][...]
            refs[n + k][...] = _dot(parts[0], oh) + _dot(parts[1], oh) + _dot(parts[2], oh)

    flat = pl.pallas_call(
        body, name="bias_tables",
        out_shape=[jax.ShapeDtypeStruct((DIL_HEADS, 2 * DIL_BLOCK * DIL_BLOCK), F32)] * n,
        compiler_params=pltpu.CompilerParams(vmem_limit_bytes=VMEM_LIMIT),
    )(rel_bias, *onehots)
    return [t.reshape(DIL_HEADS // 2, 2 * DIL_BLOCK, 2 * DIL_BLOCK) for t in flat]


def _bias_grad(dbs, onehots):
    n = len(dbs)
    dbs = [t.reshape(DIL_HEADS, 2 * DIL_BLOCK * DIL_BLOCK) for t in dbs]

    def body(*refs):
        acc = jnp.zeros((DIL_HEADS, REL_BUCKETS), F32)
        for k in range(n):
            oh = refs[n + k][...]
            for part in _split3(refs[k][...]):
                acc = acc + _dot_nt(part, oh)
        refs[-1][...] = acc

    return pl.pallas_call(
        body, name="bias_grad",
        out_shape=jax.ShapeDtypeStruct((DIL_HEADS, REL_BUCKETS), F32),
        compiler_params=pltpu.CompilerParams(vmem_limit_bytes=VMEM_LIMIT),
    )(*dbs, *onehots)


def _swap_halves(x):
    lane = lax.broadcasted_iota(jnp.int32, x.shape, 1)
    first = (lane % 64) < 32
    return jnp.where(first, pltpu.roll(x, 96, 1), pltpu.roll(x, 32, 1))


def _rope_tables(T):
    pos = jnp.arange(T, dtype=F32)
    inv_freq = ROPE_BASE ** (-jnp.arange(0, MLA_ROPE, 2, dtype=F32) / MLA_ROPE)
    ang = pos[:, None] * inv_freq[None, :]
    z = jnp.zeros((T, 128 - MLA_ROPE), F32)
    cos = jnp.concatenate([jnp.cos(ang), jnp.cos(ang), z], axis=-1)
    sin = jnp.concatenate([-jnp.sin(ang), jnp.sin(ang), z], axis=-1)
    return cos, sin


def _mla_prep(proj, cos, sin, g_qa, g_kva, g_q, g_k, wq, wkv, tm=1024):
    T = proj.shape[0]
    H = MLA_HEADS
    scale = MLA_QK ** -0.5

    def body(cq_ref, ckv_ref, kpe_ref, cos_ref, sin_ref, gqa_ref, gkva_ref, gq_ref, gk_ref, wq_ref, wkv_ref,
             q_ref, k_ref, v_ref):
        cosv, sinv = cos_ref[...], sin_ref[...]

        def rope(x):
            return x * cosv + _swap_halves(x) * sinv

        cq = cq_ref[...]
        qp = _dot_nt((cq * _rstd(cq) * gqa_ref[...]).astype(BF16), wq_ref[...])
        ckv = ckv_ref[...]
        kvp = _dot((ckv * _rstd(ckv) * gkva_ref[...]).astype(BF16), wkv_ref[...])
        kpe = kpe_ref[...]
        one_hot_lane = (lax.broadcasted_iota(jnp.int32, (tm, 128), 1) == 0).astype(BF16)
        for h in range(H):
            a = qp[:, MLA_PAD * h:MLA_PAD * (h + 1)]
            qn = a * _rstd(a, MLA_QK) * gq_ref[...]
            q_ref[h, :, 0:128] = (qn[:, 0:128] * scale).astype(BF16)
            q_ref[h, :, 128:256] = (rope(qn[:, 128:256]) * scale).astype(BF16)
            kn = kvp[:, MLA_PAD * h:MLA_PAD * h + 128]
            r = lax.rsqrt((jnp.sum(kn * kn, axis=-1, keepdims=True)
                           + jnp.sum(kpe * kpe, axis=-1, keepdims=True)) / MLA_QK + EPS)
            k_ref[h, :, 0:128] = (kn * r * gk_ref[:, 0:128]).astype(BF16)
            k_ref[h, :, 128:256] = rope(kpe * r * gk_ref[:, 128:256]).astype(BF16)
            v_ref[h, :, 0:128] = kvp[:, MLA_PAD * h + 128:MLA_PAD * (h + 1)].astype(BF16)
            v_ref[h, :, 128:256] = one_hot_lane

    fix = lambda i: (0, 0)
    return pl.pallas_call(
        body, name="mla_prep", grid=(T // tm,),
        in_specs=[pl.BlockSpec((tm, 256), lambda i: (i, 6)), pl.BlockSpec((tm, 128), lambda i: (i, 14)),
                  pl.BlockSpec((tm, 128), lambda i: (i, 15)),
                  pl.BlockSpec((tm, 128), lambda i: (i, 0)), pl.BlockSpec((tm, 128), lambda i: (i, 0)),
                  pl.BlockSpec((1, 256), fix), pl.BlockSpec((1, 128), fix),
                  pl.BlockSpec((1, 256), fix), pl.BlockSpec((1, 256), fix),
                  pl.BlockSpec((H * MLA_PAD, 256), fix), pl.BlockSpec((128, H * MLA_PAD), fix)],
        out_specs=[pl.BlockSpec((H, tm, MLA_PAD), lambda i: (0, i, 0)), pl.BlockSpec((H, tm, MLA_PAD), lambda i: (0, i, 0)),
                   pl.BlockSpec((H, tm, 2 * MLA_V), lambda i: (0, i, 0))],
        out_shape=[jax.ShapeDtypeStruct((H, T, MLA_PAD), BF16), jax.ShapeDtypeStruct((H, T, MLA_PAD), BF16),
                   jax.ShapeDtypeStruct((H, T, 2 * MLA_V), BF16)],
        compiler_params=_params(1),
    )(proj, proj, proj, cos, sin, g_qa, g_kva, g_q, g_k, wq, wkv)


def _mla_prep_bwd(proj, cos, sin, g_qa, g_kva, g_q, g_k, wq, wkv, dq, dk, dv, tm=1024):
    T = proj.shape[0]
    H = MLA_HEADS
    scale = MLA_QK ** -0.5

    def body(cq_ref, ckv_ref, kpe_ref, cos_ref, sin_ref, gqa_ref, gkva_ref, gq_ref, gk_ref, wq_ref, wkv_ref,
             dq_ref, dk_ref, dv_ref,
             dcq_ref, dckv_ref, dkpe_ref, cqn_ref, ckvn_ref, dqp_ref, dkvp_ref,
             dgqa_ref, dgkva_ref, dgq_ref, dgk_ref):
        @pl.when(pl.program_id(0) == 0)
        def _():
            for ref in (dgqa_ref, dgkva_ref, dgq_ref, dgk_ref):
                ref[...] = jnp.zeros_like(ref)

        cosv, sinv = cos_ref[...], sin_ref[...]

        def rope_bwd(dy):
            return dy * cosv + _swap_halves(dy * sinv)

        cq = cq_ref[...]
        rcq = _rstd(cq)
        cqn = (cq * rcq * gqa_ref[...]).astype(BF16)
        cqn_ref[...] = cqn
        qp = _dot_nt(cqn, wq_ref[...])
        ckv = ckv_ref[...]
        rckv = _rstd(ckv)
        ckvn = (ckv * rckv * gkva_ref[...]).astype(BF16)
        ckvn_ref[...] = ckvn
        kvp = _dot(ckvn, wkv_ref[...])
        kpe = kpe_ref[...]
        dkpe = jnp.zeros_like(kpe)
        dgq = jnp.zeros((1, MLA_PAD), F32)
        dgk = jnp.zeros((1, MLA_PAD), F32)
        for h in range(H):
            a = qp[:, MLA_PAD * h:MLA_PAD * (h + 1)]
            dqh = dq_ref[h]
            dn = jnp.concatenate([dqh[:, 0:128], rope_bwd(dqh[:, 128:256])], axis=-1) * scale
            da, dg = _rms_bwd(dn, a, gq_ref[...], _rstd(a, MLA_QK), MLA_QK)
            dgq = dgq + jnp.sum(dg, axis=0, keepdims=True)
            dqp_ref[:, MLA_PAD * h:MLA_PAD * (h + 1)] = da.astype(BF16)

            ak = jnp.concatenate([kvp[:, MLA_PAD * h:MLA_PAD * h + 128], kpe], axis=-1)
            dkh = dk_ref[h]
            dnk = jnp.concatenate([dkh[:, 0:128], rope_bwd(dkh[:, 128:256])], axis=-1)
            dak, dg = _rms_bwd(dnk, ak, gk_ref[...], _rstd(ak, MLA_QK), MLA_QK)
            dgk = dgk + jnp.sum(dg, axis=0, keepdims=True)
            dkpe = dkpe + dak[:, 128:256]
            dkvp_ref[:, MLA_PAD * h:MLA_PAD * h + 128] = dak[:, 0:128].astype(BF16)
            dkvp_ref[:, MLA_PAD * h + 128:MLA_PAD * (h + 1)] = dv_ref[h].astype(BF16)
        dkpe_ref[...] = dkpe
        dgq_ref[...] += dgq
        dgk_ref[...] += dgk
        dcq, dg = _rms_bwd(_dot(dqp_ref[...], wq_ref[...]), cq, gqa_ref[...], rcq)
        dcq_ref[...] = dcq
        dgqa_ref[...] += jnp.sum(dg, axis=0, keepdims=True)
        dckv, dg = _rms_bwd(_dot_nt(dkvp_ref[...], wkv_ref[...]), ckv, gkva_ref[...], rckv)
        dckv_ref[...] = dckv
        dgkva_ref[...] += jnp.sum(dg, axis=0, keepdims=True)

    fix = lambda i: (0, 0)
    row = lambda i: (i, 0)
    head = lambda i: (0, i, 0)
    return pl.pallas_call(
        body, name="mla_prep_bwd", grid=(T // tm,),
        in_specs=[pl.BlockSpec((tm, 256), lambda i: (i, 6)), pl.BlockSpec((tm, 128), lambda i: (i, 14)),
                  pl.BlockSpec((tm, 128), lambda i: (i, 15)),
                  pl.BlockSpec((tm, 128), row), pl.BlockSpec((tm, 128), row),
                  pl.BlockSpec((1, 256), fix), pl.BlockSpec((1, 128), fix),
                  pl.BlockSpec((1, 256), fix), pl.BlockSpec((1, 256), fix),
                  pl.BlockSpec((H * MLA_PAD, 256), fix), pl.BlockSpec((128, H * MLA_PAD), fix),
                  pl.BlockSpec((H, tm, MLA_PAD), head), pl.BlockSpec((H, tm, MLA_PAD), head),
                  pl.BlockSpec((H, tm, MLA_V), head)],
        out_specs=[pl.BlockSpec((tm, 256), row), pl.BlockSpec((tm, 128), row), pl.BlockSpec((tm, 128), row),
                   pl.BlockSpec((tm, 256), row), pl.BlockSpec((tm, 128), row),
                   pl.BlockSpec((tm, H * MLA_PAD), row), pl.BlockSpec((tm, H * MLA_PAD), row),
                   pl.BlockSpec((1, 256), fix), pl.BlockSpec((1, 128), fix),
                   pl.BlockSpec((1, 256), fix), pl.BlockSpec((1, 256), fix)],
        out_shape=[jax.ShapeDtypeStruct((T, 256), F32), jax.ShapeDtypeStruct((T, 128), F32),
                   jax.ShapeDtypeStruct((T, 128), F32),
                   jax.ShapeDtypeStruct((T, 256), BF16), jax.ShapeDtypeStruct((T, 128), BF16),
                   jax.ShapeDtypeStruct((T, H * MLA_PAD), BF16), jax.ShapeDtypeStruct((T, H * MLA_PAD), BF16),
                   jax.ShapeDtypeStruct((1, 256), F32), jax.ShapeDtypeStruct((1, 128), F32),
                   jax.ShapeDtypeStruct((1, 256), F32), jax.ShapeDtypeStruct((1, 256), F32)],
        compiler_params=_params(1),
    )(proj, proj, proj, cos, sin, g_qa, g_kva, g_q, g_k, wq, wkv, dq, dk, dv)


def _causal_pairs(T, tq, tk, key_major):
    pairs = [(i, j) for i in range(T // tq) for j in range(T // tk) if j * tk <= i * tq + tq - 1]
    if key_major:
        pairs.sort(key=lambda p: (p[1], p[0]))
    outer = [p[1] if key_major else p[0] for p in pairs]
    first = [int(t == 0 or outer[t] != outer[t - 1]) for t in range(len(pairs))]
    last = [int(t == len(pairs) - 1 or outer[t] != outer[t + 1]) for t in range(len(pairs))]
    tab = lambda v: jnp.asarray(np.array(v, np.int32))
    return tab([p[0] for p in pairs]), tab([p[1] for p in pairs]), tab(first), tab(last)


def _causal_scores(qv, kv, row0):
    s = _dot_nt(qv, kv)
    if row0 is not None:
        row = lax.broadcasted_iota(jnp.int32, s.shape, 0) + row0
        col = lax.broadcasted_iota(jnp.int32, s.shape, 1)
        s = jnp.where(col <= row, s, NEG)
    return s


def _causal_variants(qi, ki, tq, tk, update):
    assert tk % tq == 0
    diag = qi * tq - ki * tk
    for off in range(0, tk, tq):
        pl.when(diag == off)(lambda off=off: update(off))
    pl.when(diag >= tk)(lambda: update(None))


def _visible_keys(off, row0, rows, tk):
    return tk if off is None else min(tk, off + row0 + rows)


def _mla_attn(q, k, v, ride=None, tq=1024, tk=2048, rc=256):
    H, T, _ = q.shape
    tables = _causal_pairs(T, tq, tk, key_major=False)
    n_pairs = int(tables[0].shape[0])
    r_args, r_in, r_shape, r_out, r_scratch = _ride_parts(ride)

    def body(qt, kt, ft, lt, q_ref, k_ref, v_ref, o_ref, lse_ref, m_s, acc):
        t = pl.program_id(1)
        qi, ki = qt[t], kt[t]

        @pl.when(ft[t] == 1)
        def _():
            m_s[...] = jnp.full_like(m_s, NEG)
            acc[...] = jnp.zeros_like(acc)

        def update(off):
            for c in range(tq // rc):
                rows = pl.ds(c * rc, rc)
                keys = pl.ds(0, _visible_keys(off, c * rc, rc, tk))
                s = _causal_scores(q_ref[rows, :], k_ref[keys, :], None if off is None else off + c * rc)
                m_old = m_s[rows, :]
                m_new = jnp.maximum(m_old, jnp.max(s, axis=-1, keepdims=True))
                p = jnp.exp(s - m_new).astype(BF16)
                acc[rows, :] = jnp.exp(m_old - m_new) * acc[rows, :] + _dot(p, v_ref[keys, :])
                m_s[rows, :] = m_new

        _causal_variants(qi, ki, tq, tk, update)

        @pl.when(lt[t] == 1)
        def _():
            l = jnp.max(acc[:, MLA_V:], axis=-1, keepdims=True)
            o_ref[...] = acc[:, :MLA_V] / l
            lse_ref[...] = jnp.broadcast_to(m_s[...] + jnp.log(l), lse_ref.shape)

    qrow = lambda h, t, qt, kt, ft, lt: (h, qt[t], 0)
    krow = lambda h, t, qt, kt, ft, lt: (h, kt[t], 0)
    first = lambda: (pl.program_id(0) == 0) & (pl.program_id(1) == 0)
    last = lambda: (pl.program_id(0) == H - 1) & (pl.program_id(1) == n_pairs - 1)
    outs = pl.pallas_call(
        _riding(body, 7, 2, 2, ride, first, last), name="mla_attn",
        grid_spec=pltpu.PrefetchScalarGridSpec(
            num_scalar_prefetch=4, grid=(H, n_pairs),
            in_specs=[pl.BlockSpec((None, tq, MLA_PAD), qrow), pl.BlockSpec((None, tk, MLA_PAD), krow),
                      pl.BlockSpec((None, tk, 2 * MLA_V), krow)] + r_in,
            out_specs=[pl.BlockSpec((tq, MLA_V), lambda h, t, qt, kt, ft, lt: (qt[t], h)),
                       pl.BlockSpec((None, tq, 128), qrow)] + r_out,
            scratch_shapes=[pltpu.VMEM((tq, 1), F32), pltpu.VMEM((tq, 2 * MLA_V), F32)] + r_scratch),
        out_shape=[jax.ShapeDtypeStruct((T, H * MLA_V), F32), jax.ShapeDtypeStruct((H, T, 128), F32)] + r_shape,
        compiler_params=_params(2),
    )(*tables, q, k, v, *r_args)
    return outs[:2], outs[2:]


def _mla_attn_bwd(q, k, v, o, lse, do, ride=None, tq=1024, tk=1024, rc=512, rc_diagonal=256):
    H, T, _ = q.shape
    tables = _causal_pairs(T, tq, tk, key_major=True)
    n_pairs = int(tables[0].shape[0])
    r_args, r_in, r_shape, r_out, r_scratch = _ride_parts(ride)

    def body(qt, kt, ft, lt, q_ref, k_ref, v_ref, o_ref, lse_ref, do_ref, dq_ref, dk_ref, dv_ref, dk_s, dv_s):
        t = pl.program_id(1)
        qi, ki = qt[t], kt[t]

        @pl.when(t == 0)
        def _():
            dq_ref[...] = jnp.zeros_like(dq_ref)

        @pl.when(ft[t] == 1)
        def _():
            dk_s[...] = jnp.zeros_like(dk_s)
            dv_s[...] = jnp.zeros_like(dv_s)

        def update(off):
            rows_per = rc if off is None else rc_diagonal
            for c in range(tq // rows_per):
                rows = pl.ds(c * rows_per, rows_per)
                keys = pl.ds(0, _visible_keys(off, c * rows_per, rows_per, tk))
                kk, vv = k_ref[keys, :], v_ref[keys, :]
                qv, dov = q_ref[rows, :], do_ref[rows, :]
                delta = jnp.sum(dov * o_ref[rows, :], axis=-1, keepdims=True)
                lse_v = jnp.max(lse_ref[rows, :], axis=-1, keepdims=True)
                p = jnp.exp(_causal_scores(qv, kk, None if off is None else off + c * rows_per) - lse_v)
                dob = dov.astype(BF16)
                dv_s[keys, :] += _dot_tn(p.astype(BF16), dob)
                ds = (p * (_dot_nt(dob, vv) - delta)).astype(BF16)
                dk_s[keys, :] += _dot_tn(ds, qv)
                out_rows = pl.ds(pl.multiple_of(qi * tq + c * rows_per, rows_per), rows_per)
                dq_ref[out_rows, :] += _dot(ds, kk)

        _causal_variants(qi, ki, tq, tk, update)

        @pl.when(lt[t] == 1)
        def _():
            dk_ref[...] = dk_s[...]
            dv_ref[...] = dv_s[...]

    qrow = lambda h, t, qt, kt, ft, lt: (h, qt[t], 0)
    krow = lambda h, t, qt, kt, ft, lt: (h, kt[t], 0)
    qcol = lambda h, t, qt, kt, ft, lt: (qt[t], h)
    first = lambda: (pl.program_id(0) == 0) & (pl.program_id(1) == 0)
    last = lambda: (pl.program_id(0) == H - 1) & (pl.program_id(1) == n_pairs - 1)
    outs = pl.pallas_call(
        _riding(body, 10, 3, 2, ride, first, last), name="mla_attn_bwd",
        grid_spec=pltpu.PrefetchScalarGridSpec(
            num_scalar_prefetch=4, grid=(H, n_pairs),
            in_specs=[pl.BlockSpec((None, tq, MLA_PAD), qrow), pl.BlockSpec((None, tk, MLA_PAD), krow),
                      pl.BlockSpec((None, tk, MLA_V), krow), pl.BlockSpec((tq, MLA_V), qcol),
                      pl.BlockSpec((None, tq, 128), qrow), pl.BlockSpec((tq, MLA_V), qcol)] + r_in,
            out_specs=[pl.BlockSpec((None, T, MLA_PAD), lambda h, t, qt, kt, ft, lt: (h, 0, 0)),
                       pl.BlockSpec((None, tk, MLA_PAD), krow), pl.BlockSpec((None, tk, MLA_V), krow)] + r_out,
            scratch_shapes=[pltpu.VMEM((tk, MLA_PAD), F32), pltpu.VMEM((tk, MLA_V), F32)] + r_scratch),
        out_shape=[jax.ShapeDtypeStruct((H, T, MLA_PAD), F32), jax.ShapeDtypeStruct((H, T, MLA_PAD), F32),
                   jax.ShapeDtypeStruct((H, T, MLA_V), F32)] + r_shape,
        compiler_params=_params(2),
    )(*tables, q, k, v, o, lse, do, *r_args)
    return outs[:3], outs[3:]


def _pair_gain(g):
    return jnp.tile(g.reshape(1, DIL_HD), (1, 2))


def _pad_gain(g):
    return jnp.pad(g.reshape(1, MLA_QK), ((0, 0), (0, MLA_PAD - MLA_QK)))


def _local_step(x, target, s, comm):
    T = x.shape[0]
    w = comm.w
    gq, gk = _pair_gain(s["dil_q_norm"]) * DIL_HD ** -0.5, _pair_gain(s["dil_k_norm"])
    g_q, g_k = _pad_gain(s["mla_q_norm"]), _pad_gain(s["mla_k_norm"])
    cos, sin = _rope_tables(T)
    onehots = _bucket_onehots()
    biases = _bias_tables(s["rel_bias"], onehots)

    (x1, h1, gate1, up1), got = _ffn_fwd(x, s["ffn1_norm"], w["ffn1_w_gate"], w["ffn1_w_up"], w["ffn1_w_down"],
                                         ride=comm.gather(_GROUPS["attn"]))
    comm.weights_landed(_GROUPS["attn"], got)
    hm, proj, qh, kh = _in_proj(x1, s["mix_norm"], w["w_in"], gq, gk)
    dil = None
    for (_, d), bias in zip(DIL_BRANCHES, biases):
        dil = _dil_fwd(qh, kh, proj, bias, d, dil)
    o_dil, lse_dil = dil
    q, k, v = _mla_prep(proj, cos, sin, s["mla_q_a_norm"], s["mla_kv_a_norm"], g_q, g_k, w["mla_w_q_b"], w["mla_w_kv_b"])
    (o_mla, lse_mla), got = _mla_attn(q, k, v, ride=comm.gather(_GROUPS["ffn2"]))
    comm.weights_landed(_GROUPS["ffn2"], got)
    x2, oc = _out_proj(x1, o_dil, o_mla, s["out_norm_dil"], s["out_norm_mla"], w["w_out"])
    (dy, h2, gate2, up2, loss), _ = _ffn_fwd(x2, s["ffn2_norm"], w["ffn2_w_gate"], w["ffn2_w_up"], w["ffn2_w_down"],
                                             target=target)

    gw, gs = {}, {}

    def ffn_grads(name, dy_in, x_in, h, gate, up, early=None):
        (dx, a, dg, du, dyh, dgain), _ = _ffn_bwd(dy_in, x_in, s[name + "_norm"], gate, up,
                                                  w[name + "_w_gate"], w[name + "_w_up"], w[name + "_w_down"])
        gs[name + "_norm"] = dgain
        down, gate_n, up_n = (name + "_w_down",), (name + "_w_gate",), (name + "_w_up",)
        ride = lambda names: comm.scatter(names, gw) if early is not None else None
        gw[down[0]], landed = _matmul_tn(a, dyh, 1408, 1024, ride=ride(early))
        comm.grads_landed(early or (), landed)
        gw[gate_n[0]], landed = _matmul_tn(dg, h, 1408, 1024, ride=ride(down))
        comm.grads_landed(down, landed)
        gw[up_n[0]], landed = _matmul_tn(du, h, 1408, 1024, ride=ride(gate_n))
        comm.grads_landed(gate_n, landed)
        return dx

    dx2 = ffn_grads("ffn2", dy, x2, h2, gate2, up2)
    gw["w_out"], _ = _matmul_tn(oc, dx2, 1024, 1024)
    do_dil, do_mla, gs["out_norm_dil"], gs["out_norm_mla"] = _out_proj_bwd(
        dx2, o_dil, o_mla, s["out_norm_dil"], s["out_norm_mla"], w["w_out"])

    (dq, dk, dv), got = _mla_attn_bwd(q, k, v, o_mla, lse_mla, do_mla, ride=comm.scatter(_GROUPS["ffn2"], gw))
    comm.grads_landed(_GROUPS["ffn2"], got)
    (dcq, dckv, dkpe, cqn, ckvn, dqp, dkvp, gs["mla_q_a_norm"], gs["mla_kv_a_norm"], dg_q, dg_k) = _mla_prep_bwd(
        proj, cos, sin, s["mla_q_a_norm"], s["mla_kv_a_norm"], g_q, g_k, w["mla_w_q_b"], w["mla_w_kv_b"], dq, dk, dv)
    gs["mla_q_norm"], gs["mla_k_norm"] = dg_q[:, :MLA_QK], dg_k[:, :MLA_QK]
    gw["mla_w_q_b"], _ = _matmul_tn(dqp, cqn, 1024, 256)
    gw["mla_w_kv_b"], _ = _matmul_tn(ckvn, dkvp, 128, 1024)

    dqkv, dbs = None, []
    for (_, d), bias in reversed(list(zip(DIL_BRANCHES, biases))):
        dqkv, db = _dil_bwd(qh, kh, proj, o_dil, lse_dil, do_dil, bias, d, dqkv)
        dbs.insert(0, db)
    dqkv = [dqkv]
    gs["rel_bias"] = _bias_grad(dbs, onehots)

    ready = tuple(n for n in _GROUPS["attn"] if n != "w_in")
    (dx1, dproj, gs["mix_norm"], dgq, dgk), got = _in_proj_bwd(dx2, x1, s["mix_norm"], w["w_in"], proj, gq, gk,
                                                               dqkv, dcq, dckv, dkpe, ride=comm.scatter(ready, gw))
    comm.grads_landed(ready, got)
    gs["dil_q_norm"] = (dgq[:, :DIL_HD] + dgq[:, DIL_HD:]) * DIL_HD ** -0.5
    gs["dil_k_norm"] = dgk[:, :DIL_HD] + dgk[:, DIL_HD:]
    gw["w_in"], _ = _matmul_tn(dproj, hm, 1024, 1024)
    grad_x = ffn_grads("ffn1", dx1, x, h1, gate1, up1, early=("w_in",))
    return loss, grad_x, gw, gs


def _position():
    x, y, c = lax.axis_index("x"), lax.axis_index("y"), lax.axis_index("c")
    return x, y, c, 4 * x + 2 * y + c


def _peer(x, y, c, k):
    px = 1 - x if k & 4 else x
    py = 1 - y if k & 2 else y
    pc = 1 - c if k & 1 else c
    return (px, py, pc), 4 * px + 2 * py + pc


class _Ride:
    def __init__(self, arrays, scatter):
        self.arrays, self.scatter = list(arrays), list(scatter)
        self.n = n = len(self.arrays)
        self.specs = [pl.BlockSpec(memory_space=pl.ANY)] * n
        self.out_shape = [jax.ShapeDtypeStruct(a.shape if sc else (N_DEV,) + a.shape, a.dtype)
                          for a, sc in zip(self.arrays, self.scatter)]
        self.scratch = [pltpu.SemaphoreType.DMA((n, N_DEV - 1)), pltpu.SemaphoreType.DMA((n, N_DEV - 1)),
                        pltpu.SemaphoreType.DMA((n,))]

    def _copies(self, ins, outs, sems):
        send_sems, recv_sems, local_sems = sems
        x, y, c, me = _position()
        copies = []
        for a in range(self.n):
            src = ins[a].at[me] if self.scatter[a] else ins[a]
            copies.append(pltpu.make_async_copy(src, outs[a].at[me], local_sems.at[a]))
        for k in range(1, N_DEV):
            peer, peer_idx = _peer(x, y, c, k)
            for a in range(self.n):
                src = ins[a].at[peer_idx] if self.scatter[a] else ins[a]
                copies.append(pltpu.make_async_remote_copy(
                    src_ref=src, dst_ref=outs[a].at[me], send_sem=send_sems.at[a, k - 1], recv_sem=recv_sems.at[a, k - 1],
                    device_id=peer, device_id_type=pl.DeviceIdType.MESH))
        return copies

    def start(self, ins, outs, sems):
        for cp in self._copies(ins, outs, sems):
            cp.start()

    def wait(self, ins, outs, sems):
        for cp in self._copies(ins, outs, sems):
            cp.wait()


def _ride_parts(ride):
    if ride is None:
        return [], [], [], [], []
    return ride.arrays, ride.specs, ride.out_shape, ride.specs, ride.scratch


def _riding(body, n_in, n_out, n_scratch, ride, first, last):
    if ride is None:
        return body
    n = ride.n
    i1, i2 = n_in + n, n_in + n + n_out
    i3, i4 = i2 + n, i2 + n + n_scratch

    def wrapped(*refs):
        ins, outs, sems = refs[n_in:i1], refs[i2:i3], refs[i4:]

        @pl.when(first())
        def _():
            ride.start(ins, outs, sems)

        body(*refs[:n_in], *refs[i1:i2], *refs[i3:i4])

        @pl.when(last())
        def _():
            ride.wait(ins, outs, sems)

    return wrapped


def _gather_two_level(arrays, name):
    n = len(arrays)
    out_shape = [jax.ShapeDtypeStruct((N_DEV,) + a.shape, a.dtype) for a in arrays]

    def body(*refs):
        ins, outs = refs[:n], refs[n:2 * n]
        send_sems, recv_sems, local_sems = refs[2 * n:]
        x, y, c, me = _position()
        sibling = (x, y, 1 - c)
        chips = [(1 - x, y), (x, 1 - y), (1 - x, 1 - y)]
        block = lambda px, py, pc: 4 * px + 2 * py + pc

        def copy(a, k, blk, to, src=None):
            dst = outs[a].at[blk]
            return pltpu.make_async_remote_copy(
                src_ref=dst if src is None else src, dst_ref=dst, send_sem=send_sems.at[a, k], recv_sem=recv_sems.at[a, k],
                device_id=to, device_id_type=pl.DeviceIdType.MESH)

        local = [pltpu.make_async_copy(ins[a], outs[a].at[me], local_sems.at[a]) for a in range(n)]
        first = []
        for a in range(n):
            first.append(copy(a, 0, me, sibling, src=ins[a]))
            first += [copy(a, 1 + j, me, (*chip, c), src=ins[a]) for j, chip in enumerate(chips)]
        for cp in local + first:
            cp.start()
        passed = []
        for j, chip in enumerate(chips):
            for a in range(n):
                copy(a, 1 + j, block(*chip, c), sibling).wait_recv()
                passed.append(copy(a, 4 + j, block(*chip, c), sibling))
                passed[-1].start()
        for a in range(n):
            copy(a, 0, block(x, y, 1 - c), sibling).wait_recv()
            for j, chip in enumerate(chips):
                copy(a, 4 + j, block(*chip, 1 - c), sibling).wait_recv()
        for cp in first + passed:
            cp.wait_send()
        for cp in local:
            cp.wait()

    any_spec = [pl.BlockSpec(memory_space=pl.ANY)] * n
    return pl.pallas_call(
        body, name=name, in_specs=any_spec, out_specs=any_spec, out_shape=out_shape,
        scratch_shapes=[pltpu.SemaphoreType.DMA((n, N_DEV - 1)), pltpu.SemaphoreType.DMA((n, N_DEV - 1)),
                        pltpu.SemaphoreType.DMA((n,))],
    )(*arrays)


def _exchange(ride, name):
    def body(*refs):
        parts = refs[:ride.n], refs[ride.n:2 * ride.n], refs[2 * ride.n:]
        ride.start(*parts)
        ride.wait(*parts)

    return pl.pallas_call(body, name=name, in_specs=ride.specs, out_specs=ride.specs, out_shape=ride.out_shape,
                          scratch_shapes=ride.scratch)(*ride.arrays)


def _adamw_math(wv, g, m, v):
    m = ADAM_B1 * m + (1.0 - ADAM_B1) * g
    v = ADAM_B2 * v + (1.0 - ADAM_B2) * (g * g)
    m_hat = m / (1.0 - ADAM_B1 ** ADAM_STEP)
    v_hat = v / (1.0 - ADAM_B2 ** ADAM_STEP)
    delta = -ADAM_LR * (m_hat / (jnp.sqrt(v_hat) + ADAM_EPS) + ADAM_WD * wv)
    return delta, m, v


def _adamw(parts, wv, m, v):
    _, R, C = wv.shape
    tr = max([t for t in range(16, 257, 16) if R % t == 0] or [R])

    def body(p_ref, w_ref, m_ref, v_ref, g_ref, d_ref, mo_ref, vo_ref):
        g = p_ref[0].astype(F32)
        for j in range(1, N_DEV):
            g = g + p_ref[j].astype(F32)
        d, mn, vn = _adamw_math(w_ref[0], g, m_ref[0], v_ref[0])
        g_ref[0] = g
        d_ref[0] = d
        mo_ref[0] = mn
        vo_ref[0] = vn

    blk = pl.BlockSpec((1, tr, C), lambda i: (0, i, 0))
    out = jax.ShapeDtypeStruct((1, R, C), F32)
    return pl.pallas_call(
        body, name="adamw", grid=(R // tr,),
        in_specs=[pl.BlockSpec((N_DEV, tr, C), lambda i: (0, i, 0)), blk, blk, blk],
        out_specs=[blk] * 4, out_shape=[out] * 4,
        compiler_params=_params(1),
    )(parts, wv, m, v)


_TRANSPOSED = ("ffn1_w_gate", "ffn1_w_up", "ffn2_w_gate", "ffn2_w_up", "w_in", "mla_w_q_b")
_GROUPS = {"ffn1": ("ffn1_w_gate", "ffn1_w_up", "ffn1_w_down"),
           "ffn2": ("ffn2_w_gate", "ffn2_w_up", "ffn2_w_down"),
           "attn": ("w_in", "mla_w_q_b", "mla_w_kv_b", "w_out")}
_SMALL = ("ffn1_norm", "mix_norm", "ffn2_norm", "out_norm_dil", "out_norm_mla", "mla_q_a_norm", "rel_bias",
          "mla_q_norm", "mla_k_norm", "mla_kv_a_norm", "dil_q_norm", "dil_k_norm")
_SMALL_ROWS = 48


def _cols_to_full(g):
    return g.transpose(1, 0, 2).reshape(g.shape[1], N_DEV * g.shape[2])


def _full_to_cols(f):
    return f.reshape(f.shape[0], N_DEV, f.shape[1] // N_DEV).transpose(1, 0, 2)


def _shard_view(name, a):
    return jnp.swapaxes(a, 1, 2) if name in _TRANSPOSED else a


def _to_full(name, g):
    if name == "mla_w_kv_b":
        return _cols_to_full(g)
    f = g.reshape(-1, g.shape[-1])
    if name == "w_in":
        f = jnp.pad(f, ((0, PROJ_PAD - PROJ_COLS), (0, 0)))
    if name == "mla_w_q_b":
        f = jnp.pad(f.reshape(MLA_HEADS, MLA_QK, -1), ((0, 0), (0, MLA_PAD - MLA_QK), (0, 0)))
        f = f.reshape(MLA_HEADS * MLA_PAD, -1)
    return f


def _to_parts(name, f):
    if name == "mla_w_kv_b":
        return _full_to_cols(f).astype(BF16)
    if name == "w_in":
        f = f[:PROJ_COLS]
    if name == "mla_w_q_b":
        f = f.reshape(MLA_HEADS, MLA_PAD, -1)[:, :MLA_QK].reshape(MLA_HEADS * MLA_QK, -1)
    return f.reshape(N_DEV, -1, f.shape[-1]).astype(BF16)


class _Comm:
    def __init__(self, shards):
        self.shards, self.w, self.recv = shards, {}, {}

    def gather(self, names):
        return _Ride([self.shards[n] for n in names], [False] * len(names))

    def scatter(self, names, grads):
        return _Ride([_to_parts(n, grads[n]) for n in names], [True] * len(names))

    def weights_landed(self, names, got):
        self.w.update({n: _to_full(n, g) for n, g in zip(names, got)})

    def grads_landed(self, names, got):
        self.recv.update(zip(names, got))


def _pack_small(parts, extra):
    flat = jnp.concatenate([parts[n].reshape(-1) for n in _SMALL] + [extra.reshape(-1)])
    return jnp.pad(flat, (0, _SMALL_ROWS * 128 - flat.shape[0])).reshape(_SMALL_ROWS, 128)


def _unpack_small(packed, shapes):
    flat, out, off = packed.reshape(-1), {}, 0
    for n in _SMALL:
        size = math.prod(shapes[n])
        out[n] = flat[off:off + size].reshape(shapes[n])
        off += size
    return out, flat[off]


_NAMES = ("ffn1_norm", "ffn1_w_gate", "ffn1_w_up", "ffn1_w_down", "mix_norm", "w_in", "dil_q_norm", "dil_k_norm",
          "rel_bias", "mla_q_a_norm", "mla_w_q_b", "mla_kv_a_norm", "mla_w_kv_b", "mla_q_norm", "mla_k_norm",
          "out_norm_dil", "out_norm_mla", "w_out", "ffn2_norm", "ffn2_w_gate", "ffn2_w_up", "ffn2_w_down")


def kernel(x, ffn1_norm, ffn1_w_gate, ffn1_w_up, ffn1_w_down, mix_norm, w_in, dil_q_norm, dil_k_norm, rel_bias, mla_q_a_norm, mla_w_q_b, mla_kv_a_norm, mla_w_kv_b, mla_q_norm, mla_k_norm, out_norm_dil, out_norm_mla, w_out, ffn2_norm, ffn2_w_gate, ffn2_w_up, ffn2_w_down, loss_target, m_ffn1_norm, m_ffn1_w_gate, m_ffn1_w_up, m_ffn1_w_down, m_mix_norm, m_w_in, m_dil_q_norm, m_dil_k_norm, m_rel_bias, m_mla_q_a_norm, m_mla_w_q_b, m_mla_kv_a_norm, m_mla_w_kv_b, m_mla_q_norm, m_mla_k_norm, m_out_norm_dil, m_out_norm_mla, m_w_out, m_ffn2_norm, m_ffn2_w_gate, m_ffn2_w_up, m_ffn2_w_down, v_ffn1_norm, v_ffn1_w_gate, v_ffn1_w_up, v_ffn1_w_down, v_mix_norm, v_w_in, v_dil_q_norm, v_dil_k_norm, v_rel_bias, v_mla_q_a_norm, v_mla_w_q_b, v_mla_kv_a_norm, v_mla_w_kv_b, v_mla_q_norm, v_mla_k_norm, v_out_norm_dil, v_out_norm_mla, v_w_out, v_ffn2_norm, v_ffn2_w_gate, v_ffn2_w_up, v_ffn2_w_down):
    args = locals()
    wts = {n: args[n] for n in _NAMES}
    mom = {n: args["m_" + n] for n in _NAMES}
    var = {n: args["v_" + n] for n in _NAMES}

    matrices = [n for group in _GROUPS.values() for n in group]
    comm = _Comm({n: _shard_view(n, wts[n])[0].astype(BF16) for n in matrices})
    comm.weights_landed(_GROUPS["ffn1"], _gather_two_level(comm.gather(_GROUPS["ffn1"]).arrays, "gather_first"))
    small = {n: wts[n].reshape(1, -1) if n != "rel_bias" else wts[n] for n in _SMALL}

    loss, grad_x, gw, gs = _local_step(x[0], loss_target[0], small, comm)

    last = comm.scatter(("ffn1_w_up",), gw)
    got = _exchange(_Ride(last.arrays + [_pack_small(gs, loss[0, 0])], last.scatter + [False]), "scatter_last")
    comm.grads_landed(("ffn1_w_up",), got[:-1])

    res = {n: [_shard_view(n, r) for r in _adamw(comm.recv[n], *(_shard_view(n, a[n]) for a in (wts, mom, var)))]
           for n in matrices}
    shapes = {n: wts[n].shape for n in _SMALL}
    zero = jnp.zeros((), F32)
    packed = _adamw(got[-1], _pack_small(wts, zero)[None], _pack_small(mom, zero)[None], _pack_small(var, zero)[None])
    loss_total = None
    for slot, q in enumerate(packed):
        vals, extra = _unpack_small(q, shapes)
        if slot == 0:
            loss_total = extra
        for n in _SMALL:
            res.setdefault(n, [None] * 4)[slot] = vals[n]
    outs = [loss_total, grad_x[None]]
    for slot in range(4):
        outs += [res[n][slot].reshape(wts[n].shape) for n in _NAMES]
    return tuple(outs)
```

```python
import math

import numpy as np
import jax
import jax.numpy as jnp
from jax import lax
from jax.experimental import pallas as pl
from jax.experimental.pallas import tpu as pltpu

F32, BF16 = jnp.float32, jnp.bfloat16
EPS = 1e-6
NEG = -1e30
N_DEV = 8

DIL_HEADS, DIL_HD = 8, 64
DIL_WIDTH = DIL_HEADS * DIL_HD
DIL_BRANCHES = ((128, 1), (512, 4), (2048, 16))
DIL_BLOCK = 128
MLA_HEADS, MLA_NOPE, MLA_ROPE, MLA_V = 4, 128, 64, 128
MLA_QK = MLA_NOPE + MLA_ROPE
MLA_PAD = 256
ROPE_BASE = 10000.0
REL_BUCKETS, REL_MAX_DIST = 32, 2048
PROJ_COLS, PROJ_PAD = 1984, 2048
FFN_RESID = 0.5
ADAM_LR, ADAM_B1, ADAM_B2, ADAM_EPS, ADAM_WD, ADAM_STEP = 0.001, 0.9, 0.999, 1e-08, 0.01, 10
VMEM_LIMIT = 62 * 1024 * 1024

_NT = (((1,), (1,)), ((), ()))
_TN = (((0,), (0,)), ((), ()))


def _dot(a, b):
    return jnp.dot(a, b, preferred_element_type=F32)


def _dot_nt(a, b):
    return lax.dot_general(a, b, _NT, preferred_element_type=F32)


def _dot_tn(a, b):
    return lax.dot_general(a, b, _TN, preferred_element_type=F32)


def _params(n_axes):
    return pltpu.CompilerParams(dimension_semantics=("arbitrary",) * n_axes, vmem_limit_bytes=VMEM_LIMIT)


def _rstd(x, n=None):
    n = x.shape[-1] if n is None else n
    return lax.rsqrt(jnp.sum(x * x, axis=-1, keepdims=True) / n + EPS)


def _rms_bwd(dy, x, g, r, n=None):
    n = x.shape[-1] if n is None else n
    u = dy * g
    dx = r * u - x * (r * r * r) * (jnp.sum(u * x, axis=-1, keepdims=True) / n)
    return dx, dy * x * r


def _sigmoid(x):
    return 1.0 / (1.0 + jnp.exp(-x))


def _split3(x):
    parts = []
    for _ in range(3):
        xb = x.astype(BF16)
        parts.append(xb)
        x = x - xb.astype(F32)
    return parts


def _ffn_fwd(x, gain, wg, wu, wd, ride=None, target=None, tm=512, tf=2816):
    T, D = x.shape
    F = wg.shape[0]
    ni, nj = T // tm, F // tf
    with_loss = target is not None
    r_args, r_in, r_shape, r_out, r_scratch = _ride_parts(ride)

    def body(*refs):
        x_ref, g_ref, wg_ref, wu_ref, wd_ref = refs[:5]
        t_ref = refs[5] if with_loss else None
        xo_ref, h_ref, gate_ref, up_ref = refs[5 + with_loss:9 + with_loss]
        loss_ref = refs[-2] if with_loss else None
        acc = refs[-1]
        i, j = pl.program_id(0), pl.program_id(1)

        @pl.when(j == 0)
        def _():
            xv = x_ref[...]
            h_ref[...] = (xv * _rstd(xv) * g_ref[...]).astype(BF16)
            acc[...] = jnp.zeros_like(acc)

        h = h_ref[...]
        g = _dot_nt(h, wg_ref[...])
        u = _dot_nt(h, wu_ref[...])
        gate_ref[...] = g.astype(BF16)
        up_ref[...] = u.astype(BF16)
        a = (g * _sigmoid(g) * u).astype(BF16)
        acc[...] += _dot(a, wd_ref[...])

        @pl.when(j == nj - 1)
        def _():
            y = x_ref[...] + FFN_RESID * acc[...]
            if with_loss:
                @pl.when(i == 0)
                def _():
                    loss_ref[...] = jnp.zeros_like(loss_ref)

                e = y - t_ref[...]
                xo_ref[...] = e * (1.0 / D)
                loss_ref[...] += (0.5 / D) * jnp.sum(e * e)
            else:
                xo_ref[...] = y

    row = lambda i, j: (i, 0)
    tile = lambda i, j: (i, j)
    n_in, n_out = 5 + with_loss, 4 + with_loss
    first = lambda: (pl.program_id(0) == 0) & (pl.program_id(1) == 0)
    last = lambda: (pl.program_id(0) == ni - 1) & (pl.program_id(1) == nj - 1)
    outs = pl.pallas_call(
        _riding(body, n_in, n_out, 1, ride, first, last), name="ffn_fwd", grid=(ni, nj),
        in_specs=[pl.BlockSpec((tm, D), row), pl.BlockSpec((1, D), lambda i, j: (0, 0)),
                  pl.BlockSpec((tf, D), lambda i, j: (j, 0)), pl.BlockSpec((tf, D), lambda i, j: (j, 0)),
                  pl.BlockSpec((tf, D), lambda i, j: (j, 0))] + [pl.BlockSpec((tm, D), row)] * with_loss + r_in,
        out_specs=[pl.BlockSpec((tm, D), row), pl.BlockSpec((tm, D), row), pl.BlockSpec((tm, tf), tile),
                   pl.BlockSpec((tm, tf), tile)] + [pl.BlockSpec((1, 128), lambda i, j: (0, 0))] * with_loss + r_out,
        out_shape=[jax.ShapeDtypeStruct((T, D), F32), jax.ShapeDtypeStruct((T, D), BF16),
                   jax.ShapeDtypeStruct((T, F), BF16), jax.ShapeDtypeStruct((T, F), BF16)]
        + [jax.ShapeDtypeStruct((1, 128), F32)] * with_loss + r_shape,
        scratch_shapes=[pltpu.VMEM((tm, D), F32)] + r_scratch,
        compiler_params=_params(2),
    )(x, gain, wg, wu, wd, *([target] if with_loss else []), *r_args)
    return outs[:n_out], outs[n_out:]


def _ffn_bwd(dy, x, gain, gate, up, wg, wu, wd, ride=None, tm=256, tf=2816):
    T, D = x.shape
    F = wg.shape[0]
    ni, nj = T // tm, F // tf
    r_args, r_in, r_shape, r_out, r_scratch = _ride_parts(ride)

    def body(dy_ref, x_ref, g_ref, gate_ref, up_ref, wg_ref, wu_ref, wd_ref,
             dx_ref, a_ref, dg_ref, du_ref, dyh_ref, dgain_ref, acc):
        i, j = pl.program_id(0), pl.program_id(1)

        @pl.when((i == 0) & (j == 0))
        def _():
            dgain_ref[...] = jnp.zeros_like(dgain_ref)

        @pl.when(j == 0)
        def _():
            dyh_ref[...] = (FFN_RESID * dy_ref[...]).astype(BF16)
            acc[...] = jnp.zeros_like(acc)

        da = _dot_nt(dyh_ref[...], wd_ref[...])
        g = gate_ref[...].astype(F32)
        u = up_ref[...].astype(F32)
        sig = _sigmoid(g)
        s = g * sig
        a_ref[...] = (s * u).astype(BF16)
        dg = (da * u * (sig * (1.0 + g * (1.0 - sig)))).astype(BF16)
        du = (da * s).astype(BF16)
        dg_ref[...] = dg
        du_ref[...] = du
        acc[...] += _dot(dg, wg_ref[...]) + _dot(du, wu_ref[...])

        @pl.when(j == nj - 1)
        def _():
            xv = x_ref[...]
            dxn, dgc = _rms_bwd(acc[...], xv, g_ref[...], _rstd(xv))
            dx_ref[...] = dy_ref[...] + dxn
            dgain_ref[...] += jnp.sum(dgc, axis=0, keepdims=True)

    first = lambda: (pl.program_id(0) == 0) & (pl.program_id(1) == 0)
    last = lambda: (pl.program_id(0) == ni - 1) & (pl.program_id(1) == nj - 1)
    outs = pl.pallas_call(
        _riding(body, 8, 6, 1, ride, first, last), name="ffn_bwd", grid=(ni, nj),
        in_specs=[pl.BlockSpec((tm, D), lambda i, j: (i, 0)), pl.BlockSpec((tm, D), lambda i, j: (i, 0)),
                  pl.BlockSpec((1, D), lambda i, j: (0, 0)),
                  pl.BlockSpec((tm, tf), lambda i, j: (i, j)), pl.BlockSpec((tm, tf), lambda i, j: (i, j)),
                  pl.BlockSpec((tf, D), lambda i, j: (j, 0)), pl.BlockSpec((tf, D), lambda i, j: (j, 0)),
                  pl.BlockSpec((tf, D), lambda i, j: (j, 0))] + r_in,
        out_specs=[pl.BlockSpec((tm, D), lambda i, j: (i, 0)),
                   pl.BlockSpec((tm, tf), lambda i, j: (i, j)), pl.BlockSpec((tm, tf), lambda i, j: (i, j)),
                   pl.BlockSpec((tm, tf), lambda i, j: (i, j)),
                   pl.BlockSpec((tm, D), lambda i, j: (i, 0)), pl.BlockSpec((1, D), lambda i, j: (0, 0))] + r_out,
        out_shape=[jax.ShapeDtypeStruct((T, D), F32), jax.ShapeDtypeStruct((T, F), BF16),
                   jax.ShapeDtypeStruct((T, F), BF16), jax.ShapeDtypeStruct((T, F), BF16),
                   jax.ShapeDtypeStruct((T, D), BF16), jax.ShapeDtypeStruct((1, D), F32)] + r_shape,
        scratch_shapes=[pltpu.VMEM((tm, D), F32)] + r_scratch,
        compiler_params=_params(2),
    )(dy, x, gain, gate, up, wg, wu, wd, *r_args)
    return outs[:6], outs[6:]


def _matmul_tn(a, b, tk, tn, ride=None, tt=2048):
    T, K = a.shape
    N = b.shape[1]
    tk, tn = min(tk, K), min(tn, N)
    grid = (K // tk, N // tn, T // tt)
    r_args, r_in, r_shape, r_out, r_scratch = _ride_parts(ride)

    def body(a_ref, b_ref, o_ref, acc):
        t = pl.program_id(2)

        @pl.when(t == 0)
        def _():
            acc[...] = jnp.zeros_like(acc)

        acc[...] += _dot_tn(a_ref[...].astype(BF16), b_ref[...].astype(BF16))

        @pl.when(t == grid[2] - 1)
        def _():
            o_ref[...] = acc[...].astype(BF16)

    first = lambda: (pl.program_id(0) == 0) & (pl.program_id(1) == 0) & (pl.program_id(2) == 0)
    last = lambda: ((pl.program_id(0) == grid[0] - 1) & (pl.program_id(1) == grid[1] - 1)
                    & (pl.program_id(2) == grid[2] - 1))
    outs = pl.pallas_call(
        _riding(body, 2, 1, 1, ride, first, last), name="matmul_tn", grid=grid,
        in_specs=[pl.BlockSpec((tt, tk), lambda k, n, t: (t, k)), pl.BlockSpec((tt, tn), lambda k, n, t: (t, n))] + r_in,
        out_specs=[pl.BlockSpec((tk, tn), lambda k, n, t: (k, n))] + r_out,
        out_shape=[jax.ShapeDtypeStruct((K, N), BF16)] + r_shape,
        scratch_shapes=[pltpu.VMEM((tk, tn), F32)] + r_scratch,
        compiler_params=_params(3),
    )(a, b, *r_args)
    return outs[0], outs[1:]


def _in_proj(x, gain, w, gq, gk, tm=512):
    T, D = x.shape
    N = w.shape[0]
    W = DIL_WIDTH

    def body(x_ref, g_ref, w_ref, gq_ref, gk_ref, h_ref, p_ref, qh_ref, kh_ref):
        xv = x_ref[...]
        h = (xv * _rstd(xv) * g_ref[...]).astype(BF16)
        h_ref[...] = h
        p_ref[...] = _dot_nt(h, w_ref[...])
        lo = lax.broadcasted_iota(jnp.int32, (tm, 128), 1) < DIL_HD
        for hp in range(DIL_HEADS // 2):
            q = p_ref[:, 128 * hp:128 * (hp + 1)]
            k = p_ref[:, W + 128 * hp:W + 128 * (hp + 1)]
            qh_ref[:, 128 * hp:128 * (hp + 1)] = (q * _pair_rstd(q, lo) * gq_ref[...]).astype(BF16).astype(F32)
            kh_ref[:, 128 * hp:128 * (hp + 1)] = (k * _pair_rstd(k, lo) * gk_ref[...]).astype(BF16).astype(F32)

    row = lambda i: (i, 0)
    fix = lambda i: (0, 0)
    return pl.pallas_call(
        body, name="in_proj", grid=(T // tm,),
        in_specs=[pl.BlockSpec((tm, D), row), pl.BlockSpec((1, D), fix), pl.BlockSpec((N, D), fix),
                  pl.BlockSpec((1, 128), fix), pl.BlockSpec((1, 128), fix)],
        out_specs=[pl.BlockSpec((tm, D), row), pl.BlockSpec((tm, N), row), pl.BlockSpec((tm, W), row),
                   pl.BlockSpec((tm, W), row)],
        out_shape=[jax.ShapeDtypeStruct((T, D), BF16), jax.ShapeDtypeStruct((T, N), F32),
                   jax.ShapeDtypeStruct((T, W), F32), jax.ShapeDtypeStruct((T, W), F32)],
        compiler_params=_params(1),
    )(x, gain, w, gq, gk)


def _in_proj_bwd(dx_up, x, gain, w, proj, gq, gk, dqkv, dcq, dckv, dkpe, ride=None, tm=512):
    T, D = x.shape
    N = w.shape[0]
    W = DIL_WIDTH
    nb = len(dqkv)

    def body(*refs):
        dxu_ref, x_ref, g_ref, w_ref, q_ref, k_ref, gq_ref, gk_ref = refs[:8]
        dil_refs = refs[8:8 + 3 * nb]
        dcq_ref, dckv_ref, dkpe_ref, dx_ref, dp_ref, dgain_ref, dgq_ref, dgk_ref = refs[8 + 3 * nb:]

        @pl.when(pl.program_id(0) == 0)
        def _():
            for ref in (dgain_ref, dgq_ref, dgk_ref):
                ref[...] = jnp.zeros_like(ref)

        lo = lax.broadcasted_iota(jnp.int32, (tm, 128), 1) < DIL_HD
        norms = ((q_ref, gq_ref, dgq_ref), (k_ref, gk_ref, dgk_ref))
        for part in range(3):
            acc = dil_refs[part][...]
            for b in range(1, nb):
                acc = acc + dil_refs[3 * b + part][...]
            if part == 2:
                dp_ref[:, 2 * W:3 * W] = acc.astype(BF16)
                continue
            raw_ref, gn_ref, dgn_ref = norms[part]
            for hp in range(DIL_HEADS // 2):
                raw = raw_ref[:, 128 * hp:128 * (hp + 1)]
                d_raw, dgn = _pair_rms_bwd(acc[:, 128 * hp:128 * (hp + 1)], raw, _pair_rstd(raw, lo), gn_ref[...], lo)
                dp_ref[:, part * W + 128 * hp:part * W + 128 * (hp + 1)] = d_raw.astype(BF16)
                dgn_ref[...] += dgn
        dp_ref[:, 3 * W:3 * W + 256] = dcq_ref[...].astype(BF16)
        dp_ref[:, 3 * W + 256:3 * W + 384] = dckv_ref[...].astype(BF16)
        dp_ref[:, 3 * W + 384:N] = dkpe_ref[...].astype(BF16)
        dh = _dot(dp_ref[...], w_ref[...])
        xv = x_ref[...]
        dxn, dgc = _rms_bwd(dh, xv, g_ref[...], _rstd(xv))
        dx_ref[...] = dxu_ref[...] + dxn
        dgain_ref[...] += jnp.sum(dgc, axis=0, keepdims=True)

    row = lambda i: (i, 0)
    fix = lambda i: (0, 0)
    r_args, r_in, r_shape, r_out, r_scratch = _ride_parts(ride)
    first = lambda: pl.program_id(0) == 0
    last = lambda: pl.program_id(0) == T // tm - 1
    outs = pl.pallas_call(
        _riding(body, 11 + 3 * nb, 5, 0, ride, first, last), name="in_proj_bwd", grid=(T // tm,),
        in_specs=[pl.BlockSpec((tm, D), row), pl.BlockSpec((tm, D), row), pl.BlockSpec((1, D), fix),
                  pl.BlockSpec((N, D), fix), pl.BlockSpec((tm, W), row), pl.BlockSpec((tm, W), lambda i: (i, 1)),
                  pl.BlockSpec((1, 128), fix), pl.BlockSpec((1, 128), fix)] + [pl.BlockSpec((tm, W), row)] * (3 * nb)
                 + [pl.BlockSpec((tm, 256), row), pl.BlockSpec((tm, 128), row), pl.BlockSpec((tm, 128), row)] + r_in,
        out_specs=[pl.BlockSpec((tm, D), row), pl.BlockSpec((tm, N), row), pl.BlockSpec((1, D), fix),
                   pl.BlockSpec((1, 128), fix), pl.BlockSpec((1, 128), fix)] + r_out,
        out_shape=[jax.ShapeDtypeStruct((T, D), F32), jax.ShapeDtypeStruct((T, N), BF16),
                   jax.ShapeDtypeStruct((1, D), F32), jax.ShapeDtypeStruct((1, 128), F32),
                   jax.ShapeDtypeStruct((1, 128), F32)] + r_shape,
        scratch_shapes=r_scratch,
        compiler_params=_params(1),
    )(dx_up, x, gain, w, proj, proj, gq, gk, *[a for triple in dqkv for a in triple], dcq, dckv, dkpe, *r_args)
    return outs[:5], outs[5:]


def _out_proj(x, o_dil, o_mla, g_dil, g_mla, w, tm=512):
    T, D = x.shape
    W = o_dil.shape[1]

    def body(x_ref, od_ref, om_ref, gd_ref, gm_ref, w_ref, xo_ref, oc_ref):
        od, om = od_ref[...], om_ref[...]
        oc_ref[:, 0:W] = (od * _rstd(od) * gd_ref[...]).astype(BF16)
        oc_ref[:, W:2 * W] = (om * _rstd(om) * gm_ref[...]).astype(BF16)
        xo_ref[...] = x_ref[...] + _dot(oc_ref[...], w_ref[...])

    row = lambda i: (i, 0)
    fix = lambda i: (0, 0)
    return pl.pallas_call(
        body, name="out_proj", grid=(T // tm,),
        in_specs=[pl.BlockSpec((tm, D), row), pl.BlockSpec((tm, W), row), pl.BlockSpec((tm, W), row),
                  pl.BlockSpec((1, W), fix), pl.BlockSpec((1, W), fix), pl.BlockSpec((2 * W, D), fix)],
        out_specs=[pl.BlockSpec((tm, D), row), pl.BlockSpec((tm, 2 * W), row)],
        out_shape=[jax.ShapeDtypeStruct((T, D), F32), jax.ShapeDtypeStruct((T, 2 * W), BF16)],
        compiler_params=_params(1),
    )(x, o_dil, o_mla, g_dil, g_mla, w)


def _out_proj_bwd(dx, o_dil, o_mla, g_dil, g_mla, w, tm=512):
    T, D = dx.shape
    W = o_dil.shape[1]

    def body(dx_ref, od_ref, om_ref, gd_ref, gm_ref, w_ref, dod_ref, dom_ref, dgd_ref, dgm_ref):
        @pl.when(pl.program_id(0) == 0)
        def _():
            dgd_ref[...] = jnp.zeros_like(dgd_ref)
            dgm_ref[...] = jnp.zeros_like(dgm_ref)

        doc = _dot_nt(dx_ref[...].astype(BF16), w_ref[...])
        od, om = od_ref[...], om_ref[...]
        dod, dgd = _rms_bwd(doc[:, 0:W], od, gd_ref[...], _rstd(od))
        dom, dgm = _rms_bwd(doc[:, W:2 * W], om, gm_ref[...], _rstd(om))
        dod_ref[...] = dod
        dom_ref[...] = dom
        dgd_ref[...] += jnp.sum(dgd, axis=0, keepdims=True)
        dgm_ref[...] += jnp.sum(dgm, axis=0, keepdims=True)

    row = lambda i: (i, 0)
    fix = lambda i: (0, 0)
    return pl.pallas_call(
        body, name="out_proj_bwd", grid=(T // tm,),
        in_specs=[pl.BlockSpec((tm, D), row), pl.BlockSpec((tm, W), row), pl.BlockSpec((tm, W), row),
                  pl.BlockSpec((1, W), fix), pl.BlockSpec((1, W), fix), pl.BlockSpec((2 * W, D), fix)],
        out_specs=[pl.BlockSpec((tm, W), row), pl.BlockSpec((tm, W), row),
                   pl.BlockSpec((1, W), fix), pl.BlockSpec((1, W), fix)],
        out_shape=[jax.ShapeDtypeStruct((T, W), F32), jax.ShapeDtypeStruct((T, W), F32),
                   jax.ShapeDtypeStruct((1, W), F32), jax.ShapeDtypeStruct((1, W), F32)],
        compiler_params=_params(1),
    )(dx, o_dil, o_mla, g_dil, g_mla, w)


def _pair_rstd(x, lo):
    sq = x * x
    s0 = jnp.sum(jnp.where(lo, sq, 0.0), axis=-1, keepdims=True)
    s1 = jnp.sum(jnp.where(lo, 0.0, sq), axis=-1, keepdims=True)
    return jnp.where(lo, lax.rsqrt(s0 / DIL_HD + EPS), lax.rsqrt(s1 / DIL_HD + EPS))


def _pair_rms_bwd(dn, x, r, g, lo):
    u = dn * g
    t = u * x
    d0 = jnp.sum(jnp.where(lo, t, 0.0), axis=-1, keepdims=True)
    d1 = jnp.sum(jnp.where(lo, 0.0, t), axis=-1, keepdims=True)
    dx = r * u - x * (r * r * r) * (jnp.where(lo, d0, d1) / DIL_HD)
    return dx, jnp.sum(dn * x * r, axis=0, keepdims=True)


def _pair_col(x, lo, e):
    sel = lo if e == 0 else jnp.logical_not(lo)
    return jnp.max(jnp.where(sel, x, NEG), axis=-1, keepdims=True)


def _dil_masks(n):
    lo = lax.broadcasted_iota(jnp.int32, (DIL_BLOCK, DIL_BLOCK), 1) < DIL_HD
    row = lax.broadcasted_iota(jnp.int32, (2 * DIL_BLOCK, 2 * DIL_BLOCK), 0) % DIL_BLOCK
    col = lax.broadcasted_iota(jnp.int32, (2 * DIL_BLOCK, 2 * DIL_BLOCK), 1)
    prev = jnp.logical_and(jnp.logical_and(col < DIL_BLOCK, col >= row), n > 0)
    cur = jnp.logical_and(col >= DIL_BLOCK, col - DIL_BLOCK <= row)
    return lo, jnp.logical_or(prev, cur)


def _stack_heads(x, lo):
    return jnp.concatenate([jnp.where(lo, x, 0.0), jnp.where(lo, 0.0, x)], axis=0)


def _unstack_heads(x2, lo):
    return jnp.where(lo, x2[:DIL_BLOCK], x2[DIL_BLOCK:])


def _dil_pairs(d):
    return 4 if d == 1 else 1


def _sub_rows(r, d):
    return pl.ds(r, DIL_BLOCK, stride=d) if d > 1 else pl.ds(0, DIL_BLOCK)


def _store_piece(scratch, i, part, piece):
    if part is None:
        scratch[i] = piece
    else:
        scratch[i, pl.ds(DIL_BLOCK * part, DIL_BLOCK), :] = piece


def _split_subsequences(loads, d, P, stage=None):
    if d == 16:
        group = 4 * DIL_BLOCK
        for block, scratch, part in loads:
            for a in range(4):
                stage[pl.ds(a * group, group), :] = block[pl.ds(a, group, stride=4), :]
            for a in range(4):
                for b in range(4):
                    _store_piece(scratch, a + 4 * b, part, stage[pl.ds(a * group + b, DIL_BLOCK, stride=4), :])
        return
    for r in range(d):
        for p in range(P):
            for block, scratch, part in loads:
                _store_piece(scratch, r * P + p, part, block[_sub_rows(r, d), pl.ds(128 * p, 128)])


def _keep_previous_block(scratches, n):
    for scratch in scratches:
        @pl.when(n == 0)
        def _():
            scratch[:, pl.ds(0, DIL_BLOCK), :] = jnp.zeros((scratch.shape[0], DIL_BLOCK, 128), F32)

        @pl.when(n > 0)
        def _():
            scratch[:, pl.ds(0, DIL_BLOCK), :] = scratch[:, pl.ds(DIL_BLOCK, DIL_BLOCK), :]


def _merge_subsequences(stores, d, P, stage=None):
    if d == 16:
        group = 4 * DIL_BLOCK
        for block, scratch, plus in stores:
            for a in range(4):
                for b in range(4):
                    stage[pl.ds(a * group + b, DIL_BLOCK, stride=4), :] = scratch[a + 4 * b]
            for a in range(4):
                rows = pl.ds(a, group, stride=4)
                val = stage[pl.ds(a * group, group), :]
                block[rows, :] = val if plus is None else val + plus[rows, :]
        return
    for r in range(d):
        for p in range(P):
            for block, scratch, plus in stores:
                part = _sub_rows(r, d), pl.ds(128 * p, 128)
                block[part] = scratch[r * P + p] if plus is None else scratch[r * P + p] + plus[part]


def _dil_fwd(qh, kh, proj, bias, d, prev):
    T = proj.shape[0]
    P = _dil_pairs(d)
    rows, cw, n_it = DIL_BLOCK * d, 128 * P, d * P
    nblk = T // rows
    has_prev = prev is not None

    def body(*refs):
        q_ref, kc_ref, vc_ref, bias_ref = refs[:4]
        refs = refs[4:]
        if has_prev:
            oin_ref, lin_ref = refs[:2]
            refs = refs[2:]
        o_ref, l_ref, stage, qs, ks, vs, os_, ls_ = refs[:8]
        pb, n = pl.program_id(0), pl.program_id(1)
        lo, valid = _dil_masks(n)
        _keep_previous_block((ks, vs), n)
        loads = [(q_ref, qs, None), (kc_ref, ks, 1), (vc_ref, vs, 1)]
        if has_prev:
            ois, lis = refs[8:]
            loads += [(oin_ref, ois, None), (lin_ref, lis, None)]
        _split_subsequences(loads, d, P, stage)

        def step(i, carry):
            q2 = _stack_heads(qs[i], lo).astype(BF16)
            s = jnp.where(valid, _dot_nt(q2, ks[i].astype(BF16)) + bias_ref[pb * P + i % P], NEG)
            m = jnp.max(s, axis=-1, keepdims=True)
            p = jnp.exp(s - m)
            l = jnp.sum(p, axis=-1, keepdims=True)
            o = _unstack_heads(_dot(p.astype(BF16), vs[i].astype(BF16)) / l, lo)
            lse = _unstack_heads(jnp.broadcast_to(m + jnp.log(l), (2 * DIL_BLOCK, 128)), lo)
            if has_prev:
                lin = lis[i]
                mx = jnp.maximum(lin, lse)
                lnew = mx + jnp.log(jnp.exp(lin - mx) + jnp.exp(lse - mx))
                o = ois[i] * jnp.exp(lin - lnew) + o * jnp.exp(lse - lnew)
                lse = lnew
            os_[i] = o
            ls_[i] = lse
            return carry

        lax.fori_loop(0, n_it, step, 0, unroll=4)
        _merge_subsequences([(o_ref, os_, None), (l_ref, ls_, None)], d, P, stage)

    blk = (rows, cw)
    vcol = 2 * DIL_WIDTH // cw
    fix3 = lambda pb, n: (0, 0, 0)
    tok = pl.BlockSpec(blk, lambda pb, n: (n, pb))
    in_specs = [tok, tok, pl.BlockSpec(blk, lambda pb, n: (n, vcol + pb)),
                pl.BlockSpec((DIL_HEADS // 2, 2 * DIL_BLOCK, 2 * DIL_BLOCK), fix3)]
    args = [qh, kh, proj, bias]
    one, two = pltpu.VMEM((n_it, DIL_BLOCK, 128), F32), pltpu.VMEM((n_it, 2 * DIL_BLOCK, 128), F32)
    scratch = [pltpu.VMEM((rows, 128), F32), one, two, two, one, one]
    if has_prev:
        in_specs += [tok, tok]
        args += list(prev)
        scratch += [one, one]
    out = jax.ShapeDtypeStruct((T, DIL_WIDTH), F32)
    return pl.pallas_call(
        body, name=f"dil_fwd_d{d}", grid=(DIL_HEADS // 2 // P, nblk), in_specs=in_specs, out_specs=[tok, tok],
        out_shape=[out, out], scratch_shapes=scratch, compiler_params=_params(2),
    )(*args)


def _dil_bwd(qh, kh, proj, o, lse, do, bias, d, prev):
    T = proj.shape[0]
    P = _dil_pairs(d)
    rows, cw, n_it = DIL_BLOCK * d, 128 * P, d * P
    nblk = T // rows
    has_prev = prev is not None

    def body(*refs):
        q_ref, kc_ref, vc_ref, o_ref, l_ref, do_ref, bias_ref = refs[:7]
        dqi_ref, dki_ref, dvi_ref = refs[7:10] if has_prev else (None, None, None)
        dq_ref, dk_ref, dv_ref, db_ref, stage, qs, ks, vs, os_, ls_, dos, dqs, dks, dvs, ck, cv = refs[7 + 3 * has_prev:]
        pb, n = pl.program_id(0), pl.program_id(1)
        lo, valid = _dil_masks(n)

        @pl.when((pb == 0) & (n == 0))
        def _():
            db_ref[...] = jnp.zeros_like(db_ref)

        @pl.when(n == 0)
        def _():
            ck[...] = jnp.zeros_like(ck)
            cv[...] = jnp.zeros_like(cv)

        _keep_previous_block((ks, vs), n)
        _split_subsequences([(q_ref, qs, None), (kc_ref, ks, 1), (vc_ref, vs, 1),
                             (o_ref, os_, None), (l_ref, ls_, None), (do_ref, dos, None)], d, P, stage)

        def step(i, carry):
            pair = pb * P + i % P
            q2 = _stack_heads(qs[i], lo).astype(BF16)
            kcat, vcat = ks[i].astype(BF16), vs[i].astype(BF16)
            dov = dos[i]
            do2 = _stack_heads(dov, lo).astype(BF16)
            delta = jnp.sum(_stack_heads(dov * os_[i], lo), axis=-1, keepdims=True)
            lse_pair = ls_[i]
            lse2 = jnp.concatenate([_pair_col(lse_pair, lo, 0), _pair_col(lse_pair, lo, 1)], axis=0)
            s = jnp.where(valid, _dot_nt(q2, kcat) + bias_ref[pair], NEG)
            p = jnp.exp(s - lse2)
            ds = p * (_dot_nt(do2, vcat) - delta)
            db_ref[pair] += ds
            dsb = ds.astype(BF16)
            dqs[i] = _unstack_heads(_dot(dsb, kcat), lo)
            dk2 = _dot_tn(dsb, q2)
            dv2 = _dot_tn(p.astype(BF16), do2)
            dks[i] = ck[i] + dk2[:DIL_BLOCK]
            dvs[i] = cv[i] + dv2[:DIL_BLOCK]
            ck[i] = dk2[DIL_BLOCK:]
            cv[i] = dv2[DIL_BLOCK:]
            return carry

        @pl.when(n < nblk)
        def _():
            lax.fori_loop(0, n_it, step, 0, unroll=4)
            _merge_subsequences([(dq_ref, dqs, dqi_ref), (dk_ref, dks, dki_ref), (dv_ref, dvs, dvi_ref)], d, P, stage)

        @pl.when(n == nblk)
        def _():
            _merge_subsequences([(dk_ref, ck, dki_ref), (dv_ref, cv, dvi_ref)], d, P, stage)

    blk = (rows, cw)
    vcol = 2 * DIL_WIDTH // cw
    qn_ = lambda n: jnp.minimum(n, nblk - 1)
    pn_ = lambda n: jnp.maximum(n - 1, 0)
    fix3 = lambda pb, n: (0, 0, 0)
    tok_q = pl.BlockSpec(blk, lambda pb, n: (qn_(n), pb))
    tok_p = pl.BlockSpec(blk, lambda pb, n: (pn_(n), pb))
    bias_spec = pl.BlockSpec((DIL_HEADS // 2, 2 * DIL_BLOCK, 2 * DIL_BLOCK), fix3)
    in_specs = [tok_q, tok_q, pl.BlockSpec(blk, lambda pb, n: (qn_(n), vcol + pb)), tok_q, tok_q, tok_q, bias_spec]
    in_specs += [tok_q, tok_p, tok_p] if has_prev else []
    tok_shape = jax.ShapeDtypeStruct((T, DIL_WIDTH), F32)
    one, two = pltpu.VMEM((n_it, DIL_BLOCK, 128), F32), pltpu.VMEM((n_it, 2 * DIL_BLOCK, 128), F32)
    dq, dk, dv, db = pl.pallas_call(
        body, name=f"dil_bwd_d{d}", grid=(DIL_HEADS // 2 // P, nblk + 1), in_specs=in_specs,
        out_specs=[tok_q, tok_p, tok_p, bias_spec],
        out_shape=[tok_shape, tok_shape, tok_shape, jax.ShapeDtypeStruct(bias.shape, F32)],
        scratch_shapes=[pltpu.VMEM((rows, 128), F32), one, two, two] + [one] * 8,
        compiler_params=_params(2),
    )(qh, kh, proj, o, lse, do, bias, *(prev or ()))
    return (dq, dk, dv), db


def _t5_bucket(dist):
    max_exact = REL_BUCKETS // 2
    dd = np.maximum(dist, 1).astype(np.float32)
    large = max_exact + (np.log(dd / max_exact) / np.log(REL_MAX_DIST / max_exact)
                         * (REL_BUCKETS - max_exact)).astype(np.int32)
    large = np.minimum(large, REL_BUCKETS - 1)
    return np.where(dist < max_exact, dist, large).astype(np.int32)


def _bucket_onehots():
    i = np.arange(DIL_BLOCK)[:, None]
    j = np.arange(DIL_BLOCK)[None, :]
    out = []
    for _, d in DIL_BRANCHES:
        dist = np.concatenate([DIL_BLOCK + i - j, i - j], axis=1)
        bucket = _t5_bucket(np.clip(dist, 0, None) * d).reshape(-1)
        out.append(jnp.asarray(np.eye(REL_BUCKETS, dtype=np.float32)[:, bucket], BF16))
    return out


def _bias_tables(rel_bias, onehots):
    n = len(onehots)

    def body(rb_ref, *refs):
        parts = _split3(rb_ref[...])
        for k in range(n):
            oh = refs[k][...]
            refs[n + k][...] = _dot(parts[0], oh) + _dot(parts[1], oh) + _dot(parts[2], oh)

    flat = pl.pallas_call(
        body, name="bias_tables",
        out_shape=[jax.ShapeDtypeStruct((DIL_HEADS, 2 * DIL_BLOCK * DIL_BLOCK), F32)] * n,
        compiler_params=pltpu.CompilerParams(vmem_limit_bytes=VMEM_LIMIT),
    )(rel_bias, *onehots)
    return [t.reshape(DIL_HEADS // 2, 2 * DIL_BLOCK, 2 * DIL_BLOCK) for t in flat]


def _bias_grad(dbs, onehots):
    n = len(dbs)
    dbs = [t.reshape(DIL_HEADS, 2 * DIL_BLOCK * DIL_BLOCK) for t in dbs]

    def body(*refs):
        acc = jnp.zeros((DIL_HEADS, REL_BUCKETS), F32)
        for k in range(n):
            oh = refs[n + k][...]
            for part in _split3(refs[k][...]):
                acc = acc + _dot_nt(part, oh)
        refs[-1][...] = acc

    return pl.pallas_call(
        body, name="bias_grad",
        out_shape=jax.ShapeDtypeStruct((DIL_HEADS, REL_BUCKETS), F32),
        compiler_params=pltpu.CompilerParams(vmem_limit_bytes=VMEM_LIMIT),
    )(*dbs, *onehots)


def _swap_halves(x):
    lane = lax.broadcasted_iota(jnp.int32, x.shape, 1)
    first = (lane % 64) < 32
    return jnp.where(first, pltpu.roll(x, 96, 1), pltpu.roll(x, 32, 1))


def _rope_tables(T):
    pos = jnp.arange(T, dtype=F32)
    inv_freq = ROPE_BASE ** (-jnp.arange(0, MLA_ROPE, 2, dtype=F32) / MLA_ROPE)
    ang = pos[:, None] * inv_freq[None, :]
    z = jnp.zeros((T, 128 - MLA_ROPE), F32)
    cos = jnp.concatenate([jnp.cos(ang), jnp.cos(ang), z], axis=-1)
    sin = jnp.concatenate([-jnp.sin(ang), jnp.sin(ang), z], axis=-1)
    return cos, sin


def _mla_prep(proj, cos, sin, g_qa, g_kva, g_q, g_k, wq, wkv, tm=1024):
    T = proj.shape[0]
    H = MLA_HEADS
    scale = MLA_QK ** -0.5

    def body(cq_ref, ckv_ref, kpe_ref, cos_ref, sin_ref, gqa_ref, gkva_ref, gq_ref, gk_ref, wq_ref, wkv_ref,
             q_ref, k_ref, v_ref):
        cosv, sinv = cos_ref[...], sin_ref[...]

        def rope(x):
            return x * cosv + _swap_halves(x) * sinv

        cq = cq_ref[...]
        qp = _dot_nt((cq * _rstd(cq) * gqa_ref[...]).astype(BF16), wq_ref[...])
        ckv = ckv_ref[...]
        kvp = _dot((ckv * _rstd(ckv) * gkva_ref[...]).astype(BF16), wkv_ref[...])
        kpe = kpe_ref[...]
        one_hot_lane = (lax.broadcasted_iota(jnp.int32, (tm, 128), 1) == 0).astype(BF16)
        for h in range(H):
            a = qp[:, MLA_PAD * h:MLA_PAD * (h + 1)]
            qn = a * _rstd(a, MLA_QK) * gq_ref[...]
            q_ref[h, :, 0:128] = (qn[:, 0:128] * scale).astype(BF16)
            q_ref[h, :, 128:256] = (rope(qn[:, 128:256]) * scale).astype(BF16)
            kn = kvp[:, MLA_PAD * h:MLA_PAD * h + 128]
            r = lax.rsqrt((jnp.sum(kn * kn, axis=-1, keepdims=True)
                           + jnp.sum(kpe * kpe, axis=-1, keepdims=True)) / MLA_QK + EPS)
            k_ref[h, :, 0:128] = (kn * r * gk_ref[:, 0:128]).astype(BF16)
            k_ref[h, :, 128:256] = rope(kpe * r * gk_ref[:, 128:256]).astype(BF16)
            v_ref[h, :, 0:128] = kvp[:, MLA_PAD * h + 128:MLA_PAD * (h + 1)].astype(BF16)
            v_ref[h, :, 128:256] = one_hot_lane

    fix = lambda i: (0, 0)
    return pl.pallas_call(
        body, name="mla_prep", grid=(T // tm,),
        in_specs=[pl.BlockSpec((tm, 256), lambda i: (i, 6)), pl.BlockSpec((tm, 128), lambda i: (i, 14)),
                  pl.BlockSpec((tm, 128), lambda i: (i, 15)),
                  pl.BlockSpec((tm, 128), lambda i: (i, 0)), pl.BlockSpec((tm, 128), lambda i: (i, 0)),
                  pl.BlockSpec((1, 256), fix), pl.BlockSpec((1, 128), fix),
                  pl.BlockSpec((1, 256), fix), pl.BlockSpec((1, 256), fix),
                  pl.BlockSpec((H * MLA_PAD, 256), fix), pl.BlockSpec((128, H * MLA_PAD), fix)],
        out_specs=[pl.BlockSpec((H, tm, MLA_PAD), lambda i: (0, i, 0)), pl.BlockSpec((H, tm, MLA_PAD), lambda i: (0, i, 0)),
                   pl.BlockSpec((H, tm, 2 * MLA_V), lambda i: (0, i, 0))],
        out_shape=[jax.ShapeDtypeStruct((H, T, MLA_PAD), BF16), jax.ShapeDtypeStruct((H, T, MLA_PAD), BF16),
                   jax.ShapeDtypeStruct((H, T, 2 * MLA_V), BF16)],
        compiler_params=_params(1),
    )(proj, proj, proj, cos, sin, g_qa, g_kva, g_q, g_k, wq, wkv)


def _mla_prep_bwd(proj, cos, sin, g_qa, g_kva, g_q, g_k, wq, wkv, dq, dk, dv, tm=1024):
    T = proj.shape[0]
    H = MLA_HEADS
    scale = MLA_QK ** -0.5

    def body(cq_ref, ckv_ref, kpe_ref, cos_ref, sin_ref, gqa_ref, gkva_ref, gq_ref, gk_ref, wq_ref, wkv_ref,
             dq_ref, dk_ref, dv_ref,
             dcq_ref, dckv_ref, dkpe_ref, cqn_ref, ckvn_ref, dqp_ref, dkvp_ref,
             dgqa_ref, dgkva_ref, dgq_ref, dgk_ref):
        @pl.when(pl.program_id(0) == 0)
        def _():
            for ref in (dgqa_ref, dgkva_ref, dgq_ref, dgk_ref):
                ref[...] = jnp.zeros_like(ref)

        cosv, sinv = cos_ref[...], sin_ref[...]

        def rope_bwd(dy):
            return dy * cosv + _swap_halves(dy * sinv)

        cq = cq_ref[...]
        rcq = _rstd(cq)
        cqn = (cq * rcq * gqa_ref[...]).astype(BF16)
        cqn_ref[...] = cqn
        qp = _dot_nt(cqn, wq_ref[...])
        ckv = ckv_ref[...]
        rckv = _rstd(ckv)
        ckvn = (ckv * rckv * gkva_ref[...]).astype(BF16)
        ckvn_ref[...] = ckvn
        kvp = _dot(ckvn, wkv_ref[...])
        kpe = kpe_ref[...]
        dkpe = jnp.zeros_like(kpe)
        dgq = jnp.zeros((1, MLA_PAD), F32)
        dgk = jnp.zeros((1, MLA_PAD), F32)
        for h in range(H):
            a = qp[:, MLA_PAD * h:MLA_PAD * (h + 1)]
            dqh = dq_ref[h]
            dn = jnp.concatenate([dqh[:, 0:128], rope_bwd(dqh[:, 128:256])], axis=-1) * scale
            da, dg = _rms_bwd(dn, a, gq_ref[...], _rstd(a, MLA_QK), MLA_QK)
            dgq = dgq + jnp.sum(dg, axis=0, keepdims=True)
            dqp_ref[:, MLA_PAD * h:MLA_PAD * (h + 1)] = da.astype(BF16)

            ak = jnp.concatenate([kvp[:, MLA_PAD * h:MLA_PAD * h + 128], kpe], axis=-1)
            dkh = dk_ref[h]
            dnk = jnp.concatenate([dkh[:, 0:128], rope_bwd(dkh[:, 128:256])], axis=-1)
            dak, dg = _rms_bwd(dnk, ak, gk_ref[...], _rstd(ak, MLA_QK), MLA_QK)
            dgk = dgk + jnp.sum(dg, axis=0, keepdims=True)
            dkpe = dkpe + dak[:, 128:256]
            dkvp_ref[:, MLA_PAD * h:MLA_PAD * h + 128] = dak[:, 0:128].astype(BF16)
            dkvp_ref[:, MLA_PAD * h + 128:MLA_PAD * (h + 1)] = dv_ref[h].astype(BF16)
        dkpe_ref[...] = dkpe
        dgq_ref[...] += dgq
        dgk_ref[...] += dgk
        dcq, dg = _rms_bwd(_dot(dqp_ref[...], wq_ref[...]), cq, gqa_ref[...], rcq)
        dcq_ref[...] = dcq
        dgqa_ref[...] += jnp.sum(dg, axis=0, keepdims=True)
        dckv, dg = _rms_bwd(_dot_nt(dkvp_ref[...], wkv_ref[...]), ckv, gkva_ref[...], rckv)
        dckv_ref[...] = dckv
        dgkva_ref[...] += jnp.sum(dg, axis=0, keepdims=True)

    fix = lambda i: (0, 0)
    row = lambda i: (i, 0)
    head = lambda i: (0, i, 0)
    return pl.pallas_call(
        body, name="mla_prep_bwd", grid=(T // tm,),
        in_specs=[pl.BlockSpec((tm, 256), lambda i: (i, 6)), pl.BlockSpec((tm, 128), lambda i: (i, 14)),
                  pl.BlockSpec((tm, 128), lambda i: (i, 15)),
                  pl.BlockSpec((tm, 128), row), pl.BlockSpec((tm, 128), row),
                  pl.BlockSpec((1, 256), fix), pl.BlockSpec((1, 128), fix),
                  pl.BlockSpec((1, 256), fix), pl.BlockSpec((1, 256), fix),
                  pl.BlockSpec((H * MLA_PAD, 256), fix), pl.BlockSpec((128, H * MLA_PAD), fix),
                  pl.BlockSpec((H, tm, MLA_PAD), head), pl.BlockSpec((H, tm, MLA_PAD), head),
                  pl.BlockSpec((H, tm, MLA_V), head)],
        out_specs=[pl.BlockSpec((tm, 256), row), pl.BlockSpec((tm, 128), row), pl.BlockSpec((tm, 128), row),
                   pl.BlockSpec((tm, 256), row), pl.BlockSpec((tm, 128), row),
                   pl.BlockSpec((tm, H * MLA_PAD), row), pl.BlockSpec((tm, H * MLA_PAD), row),
                   pl.BlockSpec((1, 256), fix), pl.BlockSpec((1, 128), fix),
                   pl.BlockSpec((1, 256), fix), pl.BlockSpec((1, 256), fix)],
        out_shape=[jax.ShapeDtypeStruct((T, 256), F32), jax.ShapeDtypeStruct((T, 128), F32),
                   jax.ShapeDtypeStruct((T, 128), F32),
                   jax.ShapeDtypeStruct((T, 256), BF16), jax.ShapeDtypeStruct((T, 128), BF16),
                   jax.ShapeDtypeStruct((T, H * MLA_PAD), BF16), jax.ShapeDtypeStruct((T, H * MLA_PAD), BF16),
                   jax.ShapeDtypeStruct((1, 256), F32), jax.ShapeDtypeStruct((1, 128), F32),
                   jax.ShapeDtypeStruct((1, 256), F32), jax.ShapeDtypeStruct((1, 256), F32)],
        compiler_params=_params(1),
    )(proj, proj, proj, cos, sin, g_qa, g_kva, g_q, g_k, wq, wkv, dq, dk, dv)


def _causal_pairs(T, tq, tk, key_major):
    pairs = [(i, j) for i in range(T // tq) for j in range(T // tk) if j * tk <= i * tq + tq - 1]
    if key_major:
        pairs.sort(key=lambda p: (p[1], p[0]))
    outer = [p[1] if key_major else p[0] for p in pairs]
    first = [int(t == 0 or outer[t] != outer[t - 1]) for t in range(len(pairs))]
    last = [int(t == len(pairs) - 1 or outer[t] != outer[t + 1]) for t in range(len(pairs))]
    tab = lambda v: jnp.asarray(np.array(v, np.int32))
    return tab([p[0] for p in pairs]), tab([p[1] for p in pairs]), tab(first), tab(last)


def _causal_scores(qv, kv, row0):
    s = _dot_nt(qv, kv)
    if row0 is not None:
        row = lax.broadcasted_iota(jnp.int32, s.shape, 0) + row0
        col = lax.broadcasted_iota(jnp.int32, s.shape, 1)
        s = jnp.where(col <= row, s, NEG)
    return s


def _causal_variants(qi, ki, tq, tk, update):
    assert tk % tq == 0
    diag = qi * tq - ki * tk
    for off in range(0, tk, tq):
        pl.when(diag == off)(lambda off=off: update(off))
    pl.when(diag >= tk)(lambda: update(None))


def _visible_keys(off, row0, rows, tk):
    return tk if off is None else min(tk, off + row0 + rows)


def _mla_attn(q, k, v, ride=None, tq=1024, tk=2048, rc=256):
    H, T, _ = q.shape
    tables = _causal_pairs(T, tq, tk, key_major=False)
    n_pairs = int(tables[0].shape[0])
    r_args, r_in, r_shape, r_out, r_scratch = _ride_parts(ride)

    def body(qt, kt, ft, lt, q_ref, k_ref, v_ref, o_ref, lse_ref, m_s, acc):
        t = pl.program_id(1)
        qi, ki = qt[t], kt[t]

        @pl.when(ft[t] == 1)
        def _():
            m_s[...] = jnp.full_like(m_s, NEG)
            acc[...] = jnp.zeros_like(acc)

        def update(off):
            for c in range(tq // rc):
                rows = pl.ds(c * rc, rc)
                keys = pl.ds(0, _visible_keys(off, c * rc, rc, tk))
                s = _causal_scores(q_ref[rows, :], k_ref[keys, :], None if off is None else off + c * rc)
                m_old = m_s[rows, :]
                m_new = jnp.maximum(m_old, jnp.max(s, axis=-1, keepdims=True))
                p = jnp.exp(s - m_new).astype(BF16)
                acc[rows, :] = jnp.exp(m_old - m_new) * acc[rows, :] + _dot(p, v_ref[keys, :])
                m_s[rows, :] = m_new

        _causal_variants(qi, ki, tq, tk, update)

        @pl.when(lt[t] == 1)
        def _():
            l = jnp.max(acc[:, MLA_V:], axis=-1, keepdims=True)
            o_ref[...] = acc[:, :MLA_V] / l
            lse_ref[...] = jnp.broadcast_to(m_s[...] + jnp.log(l), lse_ref.shape)

    qrow = lambda h, t, qt, kt, ft, lt: (h, qt[t], 0)
    krow = lambda h, t, qt, kt, ft, lt: (h, kt[t], 0)
    first = lambda: (pl.program_id(0) == 0) & (pl.program_id(1) == 0)
    last = lambda: (pl.program_id(0) == H - 1) & (pl.program_id(1) == n_pairs - 1)
    outs = pl.pallas_call(
        _riding(body, 7, 2, 2, ride, first, last), name="mla_attn",
        grid_spec=pltpu.PrefetchScalarGridSpec(
            num_scalar_prefetch=4, grid=(H, n_pairs),
            in_specs=[pl.BlockSpec((None, tq, MLA_PAD), qrow), pl.BlockSpec((None, tk, MLA_PAD), krow),
                      pl.BlockSpec((None, tk, 2 * MLA_V), krow)] + r_in,
            out_specs=[pl.BlockSpec((tq, MLA_V), lambda h, t, qt, kt, ft, lt: (qt[t], h)),
                       pl.BlockSpec((None, tq, 128), qrow)] + r_out,
            scratch_shapes=[pltpu.VMEM((tq, 1), F32), pltpu.VMEM((tq, 2 * MLA_V), F32)] + r_scratch),
        out_shape=[jax.ShapeDtypeStruct((T, H * MLA_V), F32), jax.ShapeDtypeStruct((H, T, 128), F32)] + r_shape,
        compiler_params=_params(2),
    )(*tables, q, k, v, *r_args)
    return outs[:2], outs[2:]


def _mla_attn_bwd(q, k, v, o, lse, do, ride=None, tq=1024, tk=1024, rc=512, rc_diagonal=256):
    H, T, _ = q.shape
    tables = _causal_pairs(T, tq, tk, key_major=True)
    n_pairs = int(tables[0].shape[0])
    r_args, r_in, r_shape, r_out, r_scratch = _ride_parts(ride)

    def body(qt, kt, ft, lt, q_ref, k_ref, v_ref, o_ref, lse_ref, do_ref, dq_ref, dk_ref, dv_ref, dk_s, dv_s):
        t = pl.program_id(1)
        qi, ki = qt[t], kt[t]

        @pl.when(t == 0)
        def _():
            dq_ref[...] = jnp.zeros_like(dq_ref)

        @pl.when(ft[t] == 1)
        def _():
            dk_s[...] = jnp.zeros_like(dk_s)
            dv_s[...] = jnp.zeros_like(dv_s)

        def update(off):
            rows_per = rc if off is None else rc_diagonal
            for c in range(tq // rows_per):
                rows = pl.ds(c * rows_per, rows_per)
                keys = pl.ds(0, _visible_keys(off, c * rows_per, rows_per, tk))
                kk, vv = k_ref[keys, :], v_ref[keys, :]
                qv, dov = q_ref[rows, :], do_ref[rows, :]
                delta = jnp.sum(dov * o_ref[rows, :], axis=-1, keepdims=True)
                lse_v = jnp.max(lse_ref[rows, :], axis=-1, keepdims=True)
                p = jnp.exp(_causal_scores(qv, kk, None if off is None else off + c * rows_per) - lse_v)
                dob = dov.astype(BF16)
                dv_s[keys, :] += _dot_tn(p.astype(BF16), dob)
                ds = (p * (_dot_nt(dob, vv) - delta)).astype(BF16)
                dk_s[keys, :] += _dot_tn(ds, qv)
                out_rows = pl.ds(pl.multiple_of(qi * tq + c * rows_per, rows_per), rows_per)
                dq_ref[out_rows, :] += _dot(ds, kk)

        _causal_variants(qi, ki, tq, tk, update)

        @pl.when(lt[t] == 1)
        def _():
            dk_ref[...] = dk_s[...]
            dv_ref[...] = dv_s[...]

    qrow = lambda h, t, qt, kt, ft, lt: (h, qt[t], 0)
    krow = lambda h, t, qt, kt, ft, lt: (h, kt[t], 0)
    qcol = lambda h, t, qt, kt, ft, lt: (qt[t], h)
    first = lambda: (pl.program_id(0) == 0) & (pl.program_id(1) == 0)
    last = lambda: (pl.program_id(0) == H - 1) & (pl.program_id(1) == n_pairs - 1)
    outs = pl.pallas_call(
        _riding(body, 10, 3, 2, ride, first, last), name="mla_attn_bwd",
        grid_spec=pltpu.PrefetchScalarGridSpec(
            num_scalar_prefetch=4, grid=(H, n_pairs),
            in_specs=[pl.BlockSpec((None, tq, MLA_PAD), qrow), pl.BlockSpec((None, tk, MLA_PAD), krow),
                      pl.BlockSpec((None, tk, MLA_V), krow), pl.BlockSpec((tq, MLA_V), qcol),
                      pl.BlockSpec((None, tq, 128), qrow), pl.BlockSpec((tq, MLA_V), qcol)] + r_in,
            out_specs=[pl.BlockSpec((None, T, MLA_PAD), lambda h, t, qt, kt, ft, lt: (h, 0, 0)),
                       pl.BlockSpec((None, tk, MLA_PAD), krow), pl.BlockSpec((None, tk, MLA_V), krow)] + r_out,
            scratch_shapes=[pltpu.VMEM((tk, MLA_PAD), F32), pltpu.VMEM((tk, MLA_V), F32)] + r_scratch),
        out_shape=[jax.ShapeDtypeStruct((H, T, MLA_PAD), F32), jax.ShapeDtypeStruct((H, T, MLA_PAD), F32),
                   jax.ShapeDtypeStruct((H, T, MLA_V), F32)] + r_shape,
        compiler_params=_params(2),
    )(*tables, q, k, v, o, lse, do, *r_args)
    return outs[:3], outs[3:]


def _pair_gain(g):
    return jnp.tile(g.reshape(1, DIL_HD), (1, 2))


def _pad_gain(g):
    return jnp.pad(g.reshape(1, MLA_QK), ((0, 0), (0, MLA_PAD - MLA_QK)))


def _local_step(x, target, s, comm):
    T = x.shape[0]
    w = comm.w
    gq, gk = _pair_gain(s["dil_q_norm"]) * DIL_HD ** -0.5, _pair_gain(s["dil_k_norm"])
    g_q, g_k = _pad_gain(s["mla_q_norm"]), _pad_gain(s["mla_k_norm"])
    cos, sin = _rope_tables(T)
    onehots = _bucket_onehots()
    biases = _bias_tables(s["rel_bias"], onehots)

    (x1, h1, gate1, up1), got = _ffn_fwd(x, s["ffn1_norm"], w["ffn1_w_gate"], w["ffn1_w_up"], w["ffn1_w_down"],
                                         ride=comm.gather(_GROUPS["attn"]))
    comm.weights_landed(_GROUPS["attn"], got)
    hm, proj, qh, kh = _in_proj(x1, s["mix_norm"], w["w_in"], gq, gk)
    dil = None
    for (_, d), bias in zip(DIL_BRANCHES, biases):
        dil = _dil_fwd(qh, kh, proj, bias, d, dil)
    o_dil, lse_dil = dil
    q, k, v = _mla_prep(proj, cos, sin, s["mla_q_a_norm"], s["mla_kv_a_norm"], g_q, g_k, w["mla_w_q_b"], w["mla_w_kv_b"])
    (o_mla, lse_mla), got = _mla_attn(q, k, v, ride=comm.gather(_GROUPS["ffn2"]))
    comm.weights_landed(_GROUPS["ffn2"], got)
    x2, oc = _out_proj(x1, o_dil, o_mla, s["out_norm_dil"], s["out_norm_mla"], w["w_out"])
    (dy, h2, gate2, up2, loss), _ = _ffn_fwd(x2, s["ffn2_norm"], w["ffn2_w_gate"], w["ffn2_w_up"], w["ffn2_w_down"],
                                             target=target)

    gw, gs = {}, {}

    def ffn_grads(name, dy_in, x_in, h, gate, up, early=None):
        (dx, a, dg, du, dyh, dgain), _ = _ffn_bwd(dy_in, x_in, s[name + "_norm"], gate, up,
                                                  w[name + "_w_gate"], w[name + "_w_up"], w[name + "_w_down"])
        gs[name + "_norm"] = dgain
        down, gate_n, up_n = (name + "_w_down",), (name + "_w_gate",), (name + "_w_up",)
        ride = lambda names: comm.scatter(names, gw) if early is not None else None
        gw[down[0]], landed = _matmul_tn(a, dyh, 1408, 1024, ride=ride(early))
        comm.grads_landed(early or (), landed)
        gw[gate_n[0]], landed = _matmul_tn(dg, h, 1408, 1024, ride=ride(down))
        comm.grads_landed(down, landed)
        gw[up_n[0]], landed = _matmul_tn(du, h, 1408, 1024, ride=ride(gate_n))
        comm.grads_landed(gate_n, landed)
        return dx

    dx2 = ffn_grads("ffn2", dy, x2, h2, gate2, up2)
    gw["w_out"], _ = _matmul_tn(oc, dx2, 1024, 1024)
    do_dil, do_mla, gs["out_norm_dil"], gs["out_norm_mla"] = _out_proj_bwd(
        dx2, o_dil, o_mla, s["out_norm_dil"], s["out_norm_mla"], w["w_out"])

    (dq, dk, dv), got = _mla_attn_bwd(q, k, v, o_mla, lse_mla, do_mla, ride=comm.scatter(_GROUPS["ffn2"], gw))
    comm.grads_landed(_GROUPS["ffn2"], got)
    (dcq, dckv, dkpe, cqn, ckvn, dqp, dkvp, gs["mla_q_a_norm"], gs["mla_kv_a_norm"], dg_q, dg_k) = _mla_prep_bwd(
        proj, cos, sin, s["mla_q_a_norm"], s["mla_kv_a_norm"], g_q, g_k, w["mla_w_q_b"], w["mla_w_kv_b"], dq, dk, dv)
    gs["mla_q_norm"], gs["mla_k_norm"] = dg_q[:, :MLA_QK], dg_k[:, :MLA_QK]
    gw["mla_w_q_b"], _ = _matmul_tn(dqp, cqn, 1024, 256)
    gw["mla_w_kv_b"], _ = _matmul_tn(ckvn, dkvp, 128, 1024)

    dqkv, dbs = None, []
    for (_, d), bias in reversed(list(zip(DIL_BRANCHES, biases))):
        dqkv, db = _dil_bwd(qh, kh, proj, o_dil, lse_dil, do_dil, bias, d, dqkv)
        dbs.insert(0, db)
    dqkv = [dqkv]
    gs["rel_bias"] = _bias_grad(dbs, onehots)

    ready = tuple(n for n in _GROUPS["attn"] if n != "w_in")
    (dx1, dproj, gs["mix_norm"], dgq, dgk), got = _in_proj_bwd(dx2, x1, s["mix_norm"], w["w_in"], proj, gq, gk,
                                                               dqkv, dcq, dckv, dkpe, ride=comm.scatter(ready, gw))
    comm.grads_landed(ready, got)
    gs["dil_q_norm"] = (dgq[:, :DIL_HD] + dgq[:, DIL_HD:]) * DIL_HD ** -0.5
    gs["dil_k_norm"] = dgk[:, :DIL_HD] + dgk[:, DIL_HD:]
    gw["w_in"], _ = _matmul_tn(dproj, hm, 1024, 1024)
    grad_x = ffn_grads("ffn1", dx1, x, h1, gate1, up1, early=("w_in",))
    return loss, grad_x, gw, gs


def _position():
    x, y, c = lax.axis_index("x"), lax.axis_index("y"), lax.axis_index("c")
    return x, y, c, 4 * x + 2 * y + c


def _peer(x, y, c, k):
    px = 1 - x if k & 4 else x
    py = 1 - y if k & 2 else y
    pc = 1 - c if k & 1 else c
    return (px, py, pc), 4 * px + 2 * py + pc


class _Ride:
    def __init__(self, arrays, scatter):
        self.arrays, self.scatter = list(arrays), list(scatter)
        self.n = n = len(self.arrays)
        self.specs = [pl.BlockSpec(memory_space=pl.ANY)] * n
        self.out_shape = [jax.ShapeDtypeStruct(a.shape if sc else (N_DEV,) + a.shape, a.dtype)
                          for a, sc in zip(self.arrays, self.scatter)]
        self.scratch = [pltpu.SemaphoreType.DMA((n, N_DEV - 1)), pltpu.SemaphoreType.DMA((n, N_DEV - 1)),
                        pltpu.SemaphoreType.DMA((n,))]

    def _copies(self, ins, outs, sems):
        send_sems, recv_sems, local_sems = sems
        x, y, c, me = _position()
        copies = []
        for a in range(self.n):
            src = ins[a].at[me] if self.scatter[a] else ins[a]
            copies.append(pltpu.make_async_copy(src, outs[a].at[me], local_sems.at[a]))
        for k in range(1, N_DEV):
            peer, peer_idx = _peer(x, y, c, k)
            for a in range(self.n):
                src = ins[a].at[peer_idx] if self.scatter[a] else ins[a]
                copies.append(pltpu.make_async_remote_copy(
                    src_ref=src, dst_ref=outs[a].at[me], send_sem=send_sems.at[a, k - 1], recv_sem=recv_sems.at[a, k - 1],
                    device_id=peer, device_id_type=pl.DeviceIdType.MESH))
        return copies

    def start(self, ins, outs, sems):
        for cp in self._copies(ins, outs, sems):
            cp.start()

    def wait(self, ins, outs, sems):
        for cp in self._copies(ins, outs, sems):
            cp.wait()


def _ride_parts(ride):
    if ride is None:
        return [], [], [], [], []
    return ride.arrays, ride.specs, ride.out_shape, ride.specs, ride.scratch


def _riding(body, n_in, n_out, n_scratch, ride, first, last):
    if ride is None:
        return body
    n = ride.n
    i1, i2 = n_in + n, n_in + n + n_out
    i3, i4 = i2 + n, i2 + n + n_scratch

    def wrapped(*refs):
        ins, outs, sems = refs[n_in:i1], refs[i2:i3], refs[i4:]

        @pl.when(first())
        def _():
            ride.start(ins, outs, sems)

        body(*refs[:n_in], *refs[i1:i2], *refs[i3:i4])

        @pl.when(last())
        def _():
            ride.wait(ins, outs, sems)

    return wrapped


def _gather_two_level(arrays, name):
    n = len(arrays)
    out_shape = [jax.ShapeDtypeStruct((N_DEV,) + a.shape, a.dtype) for a in arrays]

    def body(*refs):
        ins, outs = refs[:n], refs[n:2 * n]
        send_sems, recv_sems, local_sems = refs[2 * n:]
        x, y, c, me = _position()
        sibling = (x, y, 1 - c)
        chips = [(1 - x, y), (x, 1 - y), (1 - x, 1 - y)]
        block = lambda px, py, pc: 4 * px + 2 * py + pc

        def copy(a, k, blk, to, src=None):
            dst = outs[a].at[blk]
            return pltpu.make_async_remote_copy(
                src_ref=dst if src is None else src, dst_ref=dst, send_sem=send_sems.at[a, k], recv_sem=recv_sems.at[a, k],
                device_id=to, device_id_type=pl.DeviceIdType.MESH)

        local = [pltpu.make_async_copy(ins[a], outs[a].at[me], local_sems.at[a]) for a in range(n)]
        first = []
        for a in range(n):
            first.append(copy(a, 0, me, sibling, src=ins[a]))
            first += [copy(a, 1 + j, me, (*chip, c), src=ins[a]) for j, chip in enumerate(chips)]
        for cp in local + first:
            cp.start()
        passed = []
        for j, chip in enumerate(chips):
            for a in range(n):
                copy(a, 1 + j, block(*chip, c), sibling).wait_recv()
                passed.append(copy(a, 4 + j, block(*chip, c), sibling))
                passed[-1].start()
        for a in range(n):
            copy(a, 0, block(x, y, 1 - c), sibling).wait_recv()
            for j, chip in enumerate(chips):
                copy(a, 4 + j, block(*chip, 1 - c), sibling).wait_recv()
        for cp in first + passed:
            cp.wait_send()
        for cp in local:
            cp.wait()

    any_spec = [pl.BlockSpec(memory_space=pl.ANY)] * n
    return pl.pallas_call(
        body, name=name, in_specs=any_spec, out_specs=any_spec, out_shape=out_shape,
        scratch_shapes=[pltpu.SemaphoreType.DMA((n, N_DEV - 1)), pltpu.SemaphoreType.DMA((n, N_DEV - 1)),
                        pltpu.SemaphoreType.DMA((n,))],
    )(*arrays)


def _exchange(ride, name):
    def body(*refs):
        parts = refs[:ride.n], refs[ride.n:2 * ride.n], refs[2 * ride.n:]
        ride.start(*parts)
        ride.wait(*parts)

    return pl.pallas_call(body, name=name, in_specs=ride.specs, out_specs=ride.specs, out_shape=ride.out_shape,
                          scratch_shapes=ride.scratch)(*ride.arrays)


def _adamw_math(wv, g, m, v):
    m = ADAM_B1 * m + (1.0 - ADAM_B1) * g
    v = ADAM_B2 * v + (1.0 - ADAM_B2) * (g * g)
    m_hat = m / (1.0 - ADAM_B1 ** ADAM_STEP)
    v_hat = v / (1.0 - ADAM_B2 ** ADAM_STEP)
    delta = -ADAM_LR * (m_hat / (jnp.sqrt(v_hat) + ADAM_EPS) + ADAM_WD * wv)
    return delta, m, v


def _adamw(parts, wv, m, v):
    _, R, C = wv.shape
    tr = max([t for t in range(16, 257, 16) if R % t == 0] or [R])

    def body(p_ref, w_ref, m_ref, v_ref, g_ref, d_ref, mo_ref, vo_ref):
        g = p_ref[0].astype(F32)
        for j in range(1, N_DEV):
            g = g + p_ref[j].astype(F32)
        d, mn, vn = _adamw_math(w_ref[0], g, m_ref[0], v_ref[0])
        g_ref[0] = g
        d_ref[0] = d
        mo_ref[0] = mn
        vo_ref[0] = vn

    blk = pl.BlockSpec((1, tr, C), lambda i: (0, i, 0))
    out = jax.ShapeDtypeStruct((1, R, C), F32)
    return pl.pallas_call(
        body, name="adamw", grid=(R // tr,),
        in_specs=[pl.BlockSpec((N_DEV, tr, C), lambda i: (0, i, 0)), blk, blk, blk],
        out_specs=[blk] * 4, out_shape=[out] * 4,
        compiler_params=_params(1),
    )(parts, wv, m, v)


_TRANSPOSED = ("ffn1_w_gate", "ffn1_w_up", "ffn2_w_gate", "ffn2_w_up", "w_in", "mla_w_q_b")
_GROUPS = {"ffn1": ("ffn1_w_gate", "ffn1_w_up", "ffn1_w_down"),
           "ffn2": ("ffn2_w_gate", "ffn2_w_up", "ffn2_w_down"),
           "attn": ("w_in", "mla_w_q_b", "mla_w_kv_b", "w_out")}
_SMALL = ("ffn1_norm", "mix_norm", "ffn2_norm", "out_norm_dil", "out_norm_mla", "mla_q_a_norm", "rel_bias",
          "mla_q_norm", "mla_k_norm", "mla_kv_a_norm", "dil_q_norm", "dil_k_norm")
_SMALL_ROWS = 48


def _cols_to_full(g):
    return g.transpose(1, 0, 2).reshape(g.shape[1], N_DEV * g.shape[2])


def _full_to_cols(f):
    return f.reshape(f.shape[0], N_DEV, f.shape[1] // N_DEV).transpose(1, 0, 2)


def _shard_view(name, a):
    return jnp.swapaxes(a, 1, 2) if name in _TRANSPOSED else a


def _to_full(name, g):
    if name == "mla_w_kv_b":
        return _cols_to_full(g)
    f = g.reshape(-1, g.shape[-1])
    if name == "w_in":
        f = jnp.pad(f, ((0, PROJ_PAD - PROJ_COLS), (0, 0)))
    if name == "mla_w_q_b":
        f = jnp.pad(f.reshape(MLA_HEADS, MLA_QK, -1), ((0, 0), (0, MLA_PAD - MLA_QK), (0, 0)))
        f = f.reshape(MLA_HEADS * MLA_PAD, -1)
    return f


def _to_parts(name, f):
    if name == "mla_w_kv_b":
        return _full_to_cols(f).astype(BF16)
    if name == "w_in":
        f = f[:PROJ_COLS]
    if name == "mla_w_q_b":
        f = f.reshape(MLA_HEADS, MLA_PAD, -1)[:, :MLA_QK].reshape(MLA_HEADS * MLA_QK, -1)
    return f.reshape(N_DEV, -1, f.shape[-1]).astype(BF16)


class _Comm:
    def __init__(self, shards):
        self.shards, self.w, self.recv = shards, {}, {}

    def gather(self, names):
        return _Ride([self.shards[n] for n in names], [False] * len(names))

    def scatter(self, names, grads):
        return _Ride([_to_parts(n, grads[n]) for n in names], [True] * len(names))

    def weights_landed(self, names, got):
        self.w.update({n: _to_full(n, g) for n, g in zip(names, got)})

    def grads_landed(self, names, got):
        self.recv.update(zip(names, got))


def _pack_small(parts, extra):
    flat = jnp.concatenate([parts[n].reshape(-1) for n in _SMALL] + [extra.reshape(-1)])
    return jnp.pad(flat, (0, _SMALL_ROWS * 128 - flat.shape[0])).reshape(_SMALL_ROWS, 128)


def _unpack_small(packed, shapes):
    flat, out, off = packed.reshape(-1), {}, 0
    for n in _SMALL:
        size = math.prod(shapes[n])
        out[n] = flat[off:off + size].reshape(shapes[n])
        off += size
    return out, flat[off]


_NAMES = ("ffn1_norm", "ffn1_w_gate", "ffn1_w_up", "ffn1_w_down", "mix_norm", "w_in", "dil_q_norm", "dil_k_norm",
          "rel_bias", "mla_q_a_norm", "mla_w_q_b", "mla_kv_a_norm", "mla_w_kv_b", "mla_q_norm", "mla_k_norm",
          "out_norm_dil", "out_norm_mla", "w_out", "ffn2_norm", "ffn2_w_gate", "ffn2_w_up", "ffn2_w_down")


def kernel(x, ffn1_norm, ffn1_w_gate, ffn1_w_up, ffn1_w_down, mix_norm, w_in, dil_q_norm, dil_k_norm, rel_bias, mla_q_a_norm, mla_w_q_b, mla_kv_a_norm, mla_w_kv_b, mla_q_norm, mla_k_norm, out_norm_dil, out_norm_mla, w_out, ffn2_norm, ffn2_w_gate, ffn2_w_up, ffn2_w_down, loss_target, m_ffn1_norm, m_ffn1_w_gate, m_ffn1_w_up, m_ffn1_w_down, m_mix_norm, m_w_in, m_dil_q_norm, m_dil_k_norm, m_rel_bias, m_mla_q_a_norm, m_mla_w_q_b, m_mla_kv_a_norm, m_mla_w_kv_b, m_mla_q_norm, m_mla_k_norm, m_out_norm_dil, m_out_norm_mla, m_w_out, m_ffn2_norm, m_ffn2_w_gate, m_ffn2_w_up, m_ffn2_w_down, v_ffn1_norm, v_ffn1_w_gate, v_ffn1_w_up, v_ffn1_w_down, v_mix_norm, v_w_in, v_dil_q_norm, v_dil_k_norm, v_rel_bias, v_mla_q_a_norm, v_mla_w_q_b, v_mla_kv_a_norm, v_mla_w_kv_b, v_mla_q_norm, v_mla_k_norm, v_out_norm_dil, v_out_norm_mla, v_w_out, v_ffn2_norm, v_ffn2_w_gate, v_ffn2_w_up, v_ffn2_w_down):
    args = locals()
    wts = {n: args[n] for n in _NAMES}
    mom = {n: args["m_" + n] for n in _NAMES}
    var = {n: args["v_" + n] for n in _NAMES}

    matrices = [n for group in _GROUPS.values() for n in group]
    comm = _Comm({n: _shard_view(n, wts[n])[0].astype(BF16) for n in matrices})
    comm.weights_landed(_GROUPS["ffn1"], _gather_two_level(comm.gather(_GROUPS["ffn1"]).arrays, "gather_first"))
    small = {n: wts[n].reshape(1, -1) if n != "rel_bias" else wts[n] for n in _SMALL}

    loss, grad_x, gw, gs = _local_step(x[0], loss_target[0], small, comm)

    last = comm.scatter(("ffn1_w_up",), gw)
    got = _exchange(_Ride(last.arrays + [_pack_small(gs, loss[0, 0])], last.scatter + [False]), "scatter_last")
    comm.grads_landed(("ffn1_w_up",), got[:-1])

    res = {n: [_shard_view(n, r) for r in _adamw(comm.recv[n], *(_shard_view(n, a[n]) for a in (wts, mom, var)))]
           for n in matrices}
    shapes = {n: wts[n].shape for n in _SMALL}
    zero = jnp.zeros((), F32)
    packed = _adamw(got[-1], _pack_small(wts, zero)[None], _pack_small(mom, zero)[None], _pack_small(var, zero)[None])
    loss_total = None
    for slot, q in enumerate(packed):
        vals, extra = _unpack_small(q, shapes)
        if slot == 0:
            loss_total = extra
        for n in _SMALL:
            res.setdefault(n, [None] * 4)[slot] = vals[n]
    outs = [loss_total, grad_x[None]]
    for slot in range(4):
        outs += [res[n][slot].reshape(wts[n].shape) for n in _NAMES]
    return tuple(outs)
```

```python
import math

import numpy as np
import jax
import jax.numpy as jnp
from jax import lax
from jax.experimental import pallas as pl
from jax.experimental.pallas import tpu as pltpu

F32, BF16 = jnp.float32, jnp.bfloat16
EPS = 1e-6
NEG = -1e30
N_DEV = 8

DIL_HEADS, DIL_HD = 8, 64
DIL_WIDTH = DIL_HEADS * DIL_HD
DIL_BRANCHES = ((128, 1), (512, 4), (2048, 16))
DIL_BLOCK = 128
MLA_HEADS, MLA_NOPE, MLA_ROPE, MLA_V = 4, 128, 64, 128
MLA_QK = MLA_NOPE + MLA_ROPE
MLA_PAD = 256
ROPE_BASE = 10000.0
REL_BUCKETS, REL_MAX_DIST = 32, 2048
PROJ_COLS, PROJ_PAD = 1984, 2048
FFN_RESID = 0.5
ADAM_LR, ADAM_B1, ADAM_B2, ADAM_EPS, ADAM_WD, ADAM_STEP = 0.001, 0.9, 0.999, 1e-08, 0.01, 10
VMEM_LIMIT = 62 * 1024 * 1024

_NT = (((1,), (1,)), ((), ()))
_TN = (((0,), (0,)), ((), ()))


def _dot(a, b):
    return jnp.dot(a, b, preferred_element_type=F32)


def _dot_nt(a, b):
    return lax.dot_general(a, b, _NT, preferred_element_type=F32)


def _dot_tn(a, b):
    return lax.dot_general(a, b, _TN, preferred_element_type=F32)


def _params(n_axes):
    return pltpu.CompilerParams(dimension_semantics=("arbitrary",) * n_axes, vmem_limit_bytes=VMEM_LIMIT)


def _rstd(x, n=None):
    n = x.shape[-1] if n is None else n
    return lax.rsqrt(jnp.sum(x * x, axis=-1, keepdims=True) / n + EPS)


def _rms_bwd(dy, x, g, r, n=None):
    n = x.shape[-1] if n is None else n
    u = dy * g
    dx = r * u - x * (r * r * r) * (jnp.sum(u * x, axis=-1, keepdims=True) / n)
    return dx, dy * x * r


def _sigmoid(x):
    return 1.0 / (1.0 + jnp.exp(-x))


def _split3(x):
    parts = []
    for _ in range(3):
        xb = x.astype(BF16)
        parts.append(xb)
        x = x - xb.astype(F32)
    return parts


def _ffn_fwd(x, gain, wg, wu, wd, ride=None, target=None, tm=512, tf=2816):
    T, D = x.shape
    F = wg.shape[0]
    ni, nj = T // tm, F // tf
    with_loss = target is not None
    r_args, r_in, r_shape, r_out, r_scratch = _ride_parts(ride)

    def body(*refs):
        x_ref, g_ref, wg_ref, wu_ref, wd_ref = refs[:5]
        t_ref = refs[5] if with_loss else None
        xo_ref, h_ref, gate_ref, up_ref = refs[5 + with_loss:9 + with_loss]
        loss_ref = refs[-2] if with_loss else None
        acc = refs[-1]
        i, j = pl.program_id(0), pl.program_id(1)

        @pl.when(j == 0)
        def _():
            xv = x_ref[...]
            h_ref[...] = (xv * _rstd(xv) * g_ref[...]).astype(BF16)
            acc[...] = jnp.zeros_like(acc)

        h = h_ref[...]
        g = _dot_nt(h, wg_ref[...])
        u = _dot_nt(h, wu_ref[...])
        gate_ref[...] = g.astype(BF16)
        up_ref[...] = u.astype(BF16)
        a = (g * _sigmoid(g) * u).astype(BF16)
        acc[...] += _dot(a, wd_ref[...])

        @pl.when(j == nj - 1)
        def _():
            y = x_ref[...] + FFN_RESID * acc[...]
            if with_loss:
                @pl.when(i == 0)
                def _():
                    loss_ref[...] = jnp.zeros_like(loss_ref)

                e = y - t_ref[...]
                xo_ref[...] = e * (1.0 / D)
                loss_ref[...] += (0.5 / D) * jnp.sum(e * e)
            else:
                xo_ref[...] = y

    row = lambda i, j: (i, 0)
    tile = lambda i, j: (i, j)
    n_in, n_out = 5 + with_loss, 4 + with_loss
    first = lambda: (pl.program_id(0) == 0) & (pl.program_id(1) == 0)
    last = lambda: (pl.program_id(0) == ni - 1) & (pl.program_id(1) == nj - 1)
    outs = pl.pallas_call(
        _riding(body, n_in, n_out, 1, ride, first, last), name="ffn_fwd", grid=(ni, nj),
        in_specs=[pl.BlockSpec((tm, D), row), pl.BlockSpec((1, D), lambda i, j: (0, 0)),
                  pl.BlockSpec((tf, D), lambda i, j: (j, 0)), pl.BlockSpec((tf, D), lambda i, j: (j, 0)),
                  pl.BlockSpec((tf, D), lambda i, j: (j, 0))] + [pl.BlockSpec((tm, D), row)] * with_loss + r_in,
        out_specs=[pl.BlockSpec((tm, D), row), pl.BlockSpec((tm, D), row), pl.BlockSpec((tm, tf), tile),
                   pl.BlockSpec((tm, tf), tile)] + [pl.BlockSpec((1, 128), lambda i, j: (0, 0))] * with_loss + r_out,
        out_shape=[jax.ShapeDtypeStruct((T, D), F32), jax.ShapeDtypeStruct((T, D), BF16),
                   jax.ShapeDtypeStruct((T, F), BF16), jax.ShapeDtypeStruct((T, F), BF16)]
        + [jax.ShapeDtypeStruct((1, 128), F32)] * with_loss + r_shape,
        scratch_shapes=[pltpu.VMEM((tm, D), F32)] + r_scratch,
        compiler_params=_params(2),
    )(x, gain, wg, wu, wd, *([target] if with_loss else []), *r_args)
    return outs[:n_out], outs[n_out:]


def _ffn_bwd(dy, x, gain, gate, up, wg, wu, wd, ride=None, tm=256, tf=2816):
    T, D = x.shape
    F = wg.shape[0]
    ni, nj = T // tm, F // tf
    r_args, r_in, r_shape, r_out, r_scratch = _ride_parts(ride)

    def body(dy_ref, x_ref, g_ref, gate_ref, up_ref, wg_ref, wu_ref, wd_ref,
             dx_ref, a_ref, dg_ref, du_ref, dyh_ref, dgain_ref, acc):
        i, j = pl.program_id(0), pl.program_id(1)

        @pl.when((i == 0) & (j == 0))
        def _():
            dgain_ref[...] = jnp.zeros_like(dgain_ref)

        @pl.when(j == 0)
        def _():
            dyh_ref[...] = (FFN_RESID * dy_ref[...]).astype(BF16)
            acc[...] = jnp.zeros_like(acc)

        da = _dot_nt(dyh_ref[...], wd_ref[...])
        g = gate_ref[...].astype(F32)
        u = up_ref[...].astype(F32)
        sig = _sigmoid(g)
        s = g * sig
        a_ref[...] = (s * u).astype(BF16)
        dg = (da * u * (sig * (1.0 + g * (1.0 - sig)))).astype(BF16)
        du = (da * s).astype(BF16)
        dg_ref[...] = dg
        du_ref[...] = du
        acc[...] += _dot(dg, wg_ref[...]) + _dot(du, wu_ref[...])

        @pl.when(j == nj - 1)
        def _():
            xv = x_ref[...]
            dxn, dgc = _rms_bwd(acc[...], xv, g_ref[...], _rstd(xv))
            dx_ref[...] = dy_ref[...] + dxn
            dgain_ref[...] += jnp.sum(dgc, axis=0, keepdims=True)

    first = lambda: (pl.program_id(0) == 0) & (pl.program_id(1) == 0)
    last = lambda: (pl.program_id(0) == ni - 1) & (pl.program_id(1) == nj - 1)
    outs = pl.pallas_call(
        _riding(body, 8, 6, 1, ride, first, last), name="ffn_bwd", grid=(ni, nj),
        in_specs=[pl.BlockSpec((tm, D), lambda i, j: (i, 0)), pl.BlockSpec((tm, D), lambda i, j: (i, 0)),
                  pl.BlockSpec((1, D), lambda i, j: (0, 0)),
                  pl.BlockSpec((tm, tf), lambda i, j: (i, j)), pl.BlockSpec((tm, tf), lambda i, j: (i, j)),
                  pl.BlockSpec((tf, D), lambda i, j: (j, 0)), pl.BlockSpec((tf, D), lambda i, j: (j, 0)),
                  pl.BlockSpec((tf, D), lambda i, j: (j, 0))] + r_in,
        out_specs=[pl.BlockSpec((tm, D), lambda i, j: (i, 0)),
                   pl.BlockSpec((tm, tf), lambda i, j: (i, j)), pl.BlockSpec((tm, tf), lambda i, j: (i, j)),
                   pl.BlockSpec((tm, tf), lambda i, j: (i, j)),
                   pl.BlockSpec((tm, D), lambda i, j: (i, 0)), pl.BlockSpec((1, D), lambda i, j: (0, 0))] + r_out,
        out_shape=[jax.ShapeDtypeStruct((T, D), F32), jax.ShapeDtypeStruct((T, F), BF16),
                   jax.ShapeDtypeStruct((T, F), BF16), jax.ShapeDtypeStruct((T, F), BF16),
                   jax.ShapeDtypeStruct((T, D), BF16), jax.ShapeDtypeStruct((1, D), F32)] + r_shape,
        scratch_shapes=[pltpu.VMEM((tm, D), F32)] + r_scratch,
        compiler_params=_params(2),
    )(dy, x, gain, gate, up, wg, wu, wd, *r_args)
    return outs[:6], outs[6:]


def _matmul_tn(a, b, tk, tn, ride=None, tt=2048):
    T, K = a.shape
    N = b.shape[1]
    tk, tn = min(tk, K), min(tn, N)
    grid = (K // tk, N // tn, T // tt)
    r_args, r_in, r_shape, r_out, r_scratch = _ride_parts(ride)

    def body(a_ref, b_ref, o_ref, acc):
        t = pl.program_id(2)

        @pl.when(t == 0)
        def _():
            acc[...] = jnp.zeros_like(acc)

        acc[...] += _dot_tn(a_ref[...].astype(BF16), b_ref[...].astype(BF16))

        @pl.when(t == grid[2] - 1)
        def _():
            o_ref[...] = acc[...].astype(BF16)

    first = lambda: (pl.program_id(0) == 0) & (pl.program_id(1) == 0) & (pl.program_id(2) == 0)
    last = lambda: ((pl.program_id(0) == grid[0] - 1) & (pl.program_id(1) == grid[1] - 1)
                    & (pl.program_id(2) == grid[2] - 1))
    outs = pl.pallas_call(
        _riding(body, 2, 1, 1, ride, first, last), name="matmul_tn", grid=grid,
        in_specs=[pl.BlockSpec((tt, tk), lambda k, n, t: (t, k)), pl.BlockSpec((tt, tn), lambda k, n, t: (t, n))] + r_in,
        out_specs=[pl.BlockSpec((tk, tn), lambda k, n, t: (k, n))] + r_out,
        out_shape=[jax.ShapeDtypeStruct((K, N), BF16)] + r_shape,
        scratch_shapes=[pltpu.VMEM((tk, tn), F32)] + r_scratch,
        compiler_params=_params(3),
    )(a, b, *r_args)
    return outs[0], outs[1:]


def _in_proj(x, gain, w, gq, gk, tm=512):
    T, D = x.shape
    N = w.shape[0]
    W = DIL_WIDTH

    def body(x_ref, g_ref, w_ref, gq_ref, gk_ref, h_ref, p_ref, qh_ref, kh_ref):
        xv = x_ref[...]
        h = (xv * _rstd(xv) * g_ref[...]).astype(BF16)
        h_ref[...] = h
        p_ref[...] = _dot_nt(h, w_ref[...])
        lo = lax.broadcasted_iota(jnp.int32, (tm, 128), 1) < DIL_HD
        for hp in range(DIL_HEADS // 2):
            q = p_ref[:, 128 * hp:128 * (hp + 1)]
            k = p_ref[:, W + 128 * hp:W + 128 * (hp + 1)]
            qh_ref[:, 128 * hp:128 * (hp + 1)] = (q * _pair_rstd(q, lo) * gq_ref[...]).astype(BF16).astype(F32)
            kh_ref[:, 128 * hp:128 * (hp + 1)] = (k * _pair_rstd(k, lo) * gk_ref[...]).astype(BF16).astype(F32)

    row = lambda i: (i, 0)
    fix = lambda i: (0, 0)
    return pl.pallas_call(
        body, name="in_proj", grid=(T // tm,),
        in_specs=[pl.BlockSpec((tm, D), row), pl.BlockSpec((1, D), fix), pl.BlockSpec((N, D), fix),
                  pl.BlockSpec((1, 128), fix), pl.BlockSpec((1, 128), fix)],
        out_specs=[pl.BlockSpec((tm, D), row), pl.BlockSpec((tm, N), row), pl.BlockSpec((tm, W), row),
                   pl.BlockSpec((tm, W), row)],
        out_shape=[jax.ShapeDtypeStruct((T, D), BF16), jax.ShapeDtypeStruct((T, N), F32),
                   jax.ShapeDtypeStruct((T, W), F32), jax.ShapeDtypeStruct((T, W), F32)],
        compiler_params=_params(1),
    )(x, gain, w, gq, gk)


def _in_proj_bwd(dx_up, x, gain, w, proj, gq, gk, dqkv, dcq, dckv, dkpe, ride=None, tm=512):
    T, D = x.shape
    N = w.shape[0]
    W = DIL_WIDTH
    nb = len(dqkv)

    def body(*refs):
        dxu_ref, x_ref, g_ref, w_ref, q_ref, k_ref, gq_ref, gk_ref = refs[:8]
        dil_refs = refs[8:8 + 3 * nb]
        dcq_ref, dckv_ref, dkpe_ref, dx_ref, dp_ref, dgain_ref, dgq_ref, dgk_ref = refs[8 + 3 * nb:]

        @pl.when(pl.program_id(0) == 0)
        def _():
            for ref in (dgain_ref, dgq_ref, dgk_ref):
                ref[...] = jnp.zeros_like(ref)

        lo = lax.broadcasted_iota(jnp.int32, (tm, 128), 1) < DIL_HD
        norms = ((q_ref, gq_ref, dgq_ref), (k_ref, gk_ref, dgk_ref))
        for part in range(3):
            acc = dil_refs[part][...]
            for b in range(1, nb):
                acc = acc + dil_refs[3 * b + part][...]
            if part == 2:
                dp_ref[:, 2 * W:3 * W] = acc.astype(BF16)
                continue
            raw_ref, gn_ref, dgn_ref = norms[part]
            for hp in range(DIL_HEADS // 2):
                raw = raw_ref[:, 128 * hp:128 * (hp + 1)]
                d_raw, dgn = _pair_rms_bwd(acc[:, 128 * hp:128 * (hp + 1)], raw, _pair_rstd(raw, lo), gn_ref[...], lo)
                dp_ref[:, part * W + 128 * hp:part * W + 128 * (hp + 1)] = d_raw.astype(BF16)
                dgn_ref[...] += dgn
        dp_ref[:, 3 * W:3 * W + 256] = dcq_ref[...].astype(BF16)
        dp_ref[:, 3 * W + 256:3 * W + 384] = dckv_ref[...].astype(BF16)
        dp_ref[:, 3 * W + 384:N] = dkpe_ref[...].astype(BF16)
        dh = _dot(dp_ref[...], w_ref[...])
        xv = x_ref[...]
        dxn, dgc = _rms_bwd(dh, xv, g_ref[...], _rstd(xv))
        dx_ref[...] = dxu_ref[...] + dxn
        dgain_ref[...] += jnp.sum(dgc, axis=0, keepdims=True)

    row = lambda i: (i, 0)
    fix = lambda i: (0, 0)
    r_args, r_in, r_shape, r_out, r_scratch = _ride_parts(ride)
    first = lambda: pl.program_id(0) == 0
    last = lambda: pl.program_id(0) == T // tm - 1
    outs = pl.pallas_call(
        _riding(body, 11 + 3 * nb, 5, 0, ride, first, last), name="in_proj_bwd", grid=(T // tm,),
        in_specs=[pl.BlockSpec((tm, D), row), pl.BlockSpec((tm, D), row), pl.BlockSpec((1, D), fix),
                  pl.BlockSpec((N, D), fix), pl.BlockSpec((tm, W), row), pl.BlockSpec((tm, W), lambda i: (i, 1)),
                  pl.BlockSpec((1, 128), fix), pl.BlockSpec((1, 128), fix)] + [pl.BlockSpec((tm, W), row)] * (3 * nb)
                 + [pl.BlockSpec((tm, 256), row), pl.BlockSpec((tm, 128), row), pl.BlockSpec((tm, 128), row)] + r_in,
        out_specs=[pl.BlockSpec((tm, D), row), pl.BlockSpec((tm, N), row), pl.BlockSpec((1, D), fix),
                   pl.BlockSpec((1, 128), fix), pl.BlockSpec((1, 128), fix)] + r_out,
        out_shape=[jax.ShapeDtypeStruct((T, D), F32), jax.ShapeDtypeStruct((T, N), BF16),
                   jax.ShapeDtypeStruct((1, D), F32), jax.ShapeDtypeStruct((1, 128), F32),
                   jax.ShapeDtypeStruct((1, 128), F32)] + r_shape,
        scratch_shapes=r_scratch,
        compiler_params=_params(1),
    )(dx_up, x, gain, w, proj, proj, gq, gk, *[a for triple in dqkv for a in triple], dcq, dckv, dkpe, *r_args)
    return outs[:5], outs[5:]


def _out_proj(x, o_dil, o_mla, g_dil, g_mla, w, tm=512):
    T, D = x.shape
    W = o_dil.shape[1]

    def body(x_ref, od_ref, om_ref, gd_ref, gm_ref, w_ref, xo_ref, oc_ref):
        od, om = od_ref[...], om_ref[...]
        oc_ref[:, 0:W] = (od * _rstd(od) * gd_ref[...]).astype(BF16)
        oc_ref[:, W:2 * W] = (om * _rstd(om) * gm_ref[...]).astype(BF16)
        xo_ref[...] = x_ref[...] + _dot(oc_ref[...], w_ref[...])

    row = lambda i: (i, 0)
    fix = lambda i: (0, 0)
    return pl.pallas_call(
        body, name="out_proj", grid=(T // tm,),
        in_specs=[pl.BlockSpec((tm, D), row), pl.BlockSpec((tm, W), row), pl.BlockSpec((tm, W), row),
                  pl.BlockSpec((1, W), fix), pl.BlockSpec((1, W), fix), pl.BlockSpec((2 * W, D), fix)],
        out_specs=[pl.BlockSpec((tm, D), row), pl.BlockSpec((tm, 2 * W), row)],
        out_shape=[jax.ShapeDtypeStruct((T, D), F32), jax.ShapeDtypeStruct((T, 2 * W), BF16)],
        compiler_params=_params(1),
    )(x, o_dil, o_mla, g_dil, g_mla, w)


def _out_proj_bwd(dx, o_dil, o_mla, g_dil, g_mla, w, tm=512):
    T, D = dx.shape
    W = o_dil.shape[1]

    def body(dx_ref, od_ref, om_ref, gd_ref, gm_ref, w_ref, dod_ref, dom_ref, dgd_ref, dgm_ref):
        @pl.when(pl.program_id(0) == 0)
        def _():
            dgd_ref[...] = jnp.zeros_like(dgd_ref)
            dgm_ref[...] = jnp.zeros_like(dgm_ref)

        doc = _dot_nt(dx_ref[...].astype(BF16), w_ref[...])
        od, om = od_ref[...], om_ref[...]
        dod, dgd = _rms_bwd(doc[:, 0:W], od, gd_ref[...], _rstd(od))
        dom, dgm = _rms_bwd(doc[:, W:2 * W], om, gm_ref[...], _rstd(om))
        dod_ref[...] = dod
        dom_ref[...] = dom
        dgd_ref[...] += jnp.sum(dgd, axis=0, keepdims=True)
        dgm_ref[...] += jnp.sum(dgm, axis=0, keepdims=True)

    row = lambda i: (i, 0)
    fix = lambda i: (0, 0)
    return pl.pallas_call(
        body, name="out_proj_bwd", grid=(T // tm,),
        in_specs=[pl.BlockSpec((tm, D), row), pl.BlockSpec((tm, W), row), pl.BlockSpec((tm, W), row),
                  pl.BlockSpec((1, W), fix), pl.BlockSpec((1, W), fix), pl.BlockSpec((2 * W, D), fix)],
        out_specs=[pl.BlockSpec((tm, W), row), pl.BlockSpec((tm, W), row),
                   pl.BlockSpec((1, W), fix), pl.BlockSpec((1, W), fix)],
        out_shape=[jax.ShapeDtypeStruct((T, W), F32), jax.ShapeDtypeStruct((T, W), F32),
                   jax.ShapeDtypeStruct((1, W), F32), jax.ShapeDtypeStruct((1, W), F32)],
        compiler_params=_params(1),
    )(dx, o_dil, o_mla, g_dil, g_mla, w)


def _pair_rstd(x, lo):
    sq = x * x
    s0 = jnp.sum(jnp.where(lo, sq, 0.0), axis=-1, keepdims=True)
    s1 = jnp.sum(jnp.where(lo, 0.0, sq), axis=-1, keepdims=True)
    return jnp.where(lo, lax.rsqrt(s0 / DIL_HD + EPS), lax.rsqrt(s1 / DIL_HD + EPS))


def _pair_rms_bwd(dn, x, r, g, lo):
    u = dn * g
    t = u * x
    d0 = jnp.sum(jnp.where(lo, t, 0.0), axis=-1, keepdims=True)
    d1 = jnp.sum(jnp.where(lo, 0.0, t), axis=-1, keepdims=True)
    dx = r * u - x * (r * r * r) * (jnp.where(lo, d0, d1) / DIL_HD)
    return dx, jnp.sum(dn * x * r, axis=0, keepdims=True)


def _pair_col(x, lo, e):
    sel = lo if e == 0 else jnp.logical_not(lo)
    return jnp.max(jnp.where(sel, x, NEG), axis=-1, keepdims=True)


def _dil_masks(n):
    lo = lax.broadcasted_iota(jnp.int32, (DIL_BLOCK, DIL_BLOCK), 1) < DIL_HD
    row = lax.broadcasted_iota(jnp.int32, (2 * DIL_BLOCK, 2 * DIL_BLOCK), 0) % DIL_BLOCK
    col = lax.broadcasted_iota(jnp.int32, (2 * DIL_BLOCK, 2 * DIL_BLOCK), 1)
    prev = jnp.logical_and(jnp.logical_and(col < DIL_BLOCK, col >= row), n > 0)
    cur = jnp.logical_and(col >= DIL_BLOCK, col - DIL_BLOCK <= row)
    return lo, jnp.logical_or(prev, cur)


def _stack_heads(x, lo):
    return jnp.concatenate([jnp.where(lo, x, 0.0), jnp.where(lo, 0.0, x)], axis=0)


def _unstack_heads(x2, lo):
    return jnp.where(lo, x2[:DIL_BLOCK], x2[DIL_BLOCK:])


def _dil_pairs(d):
    return 4 if d == 1 else 1


def _sub_rows(r, d):
    return pl.ds(r, DIL_BLOCK, stride=d) if d > 1 else pl.ds(0, DIL_BLOCK)


def _store_piece(scratch, i, part, piece):
    if part is None:
        scratch[i] = piece
    else:
        scratch[i, pl.ds(DIL_BLOCK * part, DIL_BLOCK), :] = piece


def _split_subsequences(loads, d, P, stage=None):
    if d == 16:
        group = 4 * DIL_BLOCK
        for block, scratch, part in loads:
            for a in range(4):
                stage[pl.ds(a * group, group), :] = block[pl.ds(a, group, stride=4), :]
            for a in range(4):
                for b in range(4):
                    _store_piece(scratch, a + 4 * b, part, stage[pl.ds(a * group + b, DIL_BLOCK, stride=4), :])
        return
    for r in range(d):
        for p in range(P):
            for block, scratch, part in loads:
                _store_piece(scratch, r * P + p, part, block[_sub_rows(r, d), pl.ds(128 * p, 128)])


def _keep_previous_block(scratches, n):
    for scratch in scratches:
        @pl.when(n == 0)
        def _():
            scratch[:, pl.ds(0, DIL_BLOCK), :] = jnp.zeros((scratch.shape[0], DIL_BLOCK, 128), F32)

        @pl.when(n > 0)
        def _():
            scratch[:, pl.ds(0, DIL_BLOCK), :] = scratch[:, pl.ds(DIL_BLOCK, DIL_BLOCK), :]


def _merge_subsequences(stores, d, P, stage=None):
    if d == 16:
        group = 4 * DIL_BLOCK
        for block, scratch, plus in stores:
            for a in range(4):
                for b in range(4):
                    stage[pl.ds(a * group + b, DIL_BLOCK, stride=4), :] = scratch[a + 4 * b]
            for a in range(4):
                rows = pl.ds(a, group, stride=4)
                val = stage[pl.ds(a * group, group), :]
                block[rows, :] = val if plus is None else val + plus[rows, :]
        return
    for r in range(d):
        for p in range(P):
            for block, scratch, plus in stores:
                part = _sub_rows(r, d), pl.ds(128 * p, 128)
                block[part] = scratch[r * P + p] if plus is None else scratch[r * P + p] + plus[part]


def _dil_fwd(qh, kh, proj, bias, d, prev):
    T = proj.shape[0]
    P = _dil_pairs(d)
    rows, cw, n_it = DIL_BLOCK * d, 128 * P, d * P
    nblk = T // rows
    has_prev = prev is not None

    def body(*refs):
        q_ref, kc_ref, vc_ref, bias_ref = refs[:4]
        refs = refs[4:]
        if has_prev:
            oin_ref, lin_ref = refs[:2]
            refs = refs[2:]
        o_ref, l_ref, stage, qs, ks, vs, os_, ls_ = refs[:8]
        pb, n = pl.program_id(0), pl.program_id(1)
        lo, valid = _dil_masks(n)
        _keep_previous_block((ks, vs), n)
        loads = [(q_ref, qs, None), (kc_ref, ks, 1), (vc_ref, vs, 1)]
        if has_prev:
            ois, lis = refs[8:]
            loads += [(oin_ref, ois, None), (lin_ref, lis, None)]
        _split_subsequences(loads, d, P, stage)

        def step(i, carry):
            q2 = _stack_heads(qs[i], lo).astype(BF16)
            s = jnp.where(valid, _dot_nt(q2, ks[i].astype(BF16)) + bias_ref[pb * P + i % P], NEG)
            m = jnp.max(s, axis=-1, keepdims=True)
            p = jnp.exp(s - m)
            l = jnp.sum(p, axis=-1, keepdims=True)
            o = _unstack_heads(_dot(p.astype(BF16), vs[i].astype(BF16)) / l, lo)
            lse = _unstack_heads(jnp.broadcast_to(m + jnp.log(l), (2 * DIL_BLOCK, 128)), lo)
            if has_prev:
                lin = lis[i]
                mx = jnp.maximum(lin, lse)
                lnew = mx + jnp.log(jnp.exp(lin - mx) + jnp.exp(lse - mx))
                o = ois[i] * jnp.exp(lin - lnew) + o * jnp.exp(lse - lnew)
                lse = lnew
            os_[i] = o
            ls_[i] = lse
            return carry

        lax.fori_loop(0, n_it, step, 0, unroll=min(n_it, 8))
        _merge_subsequences([(o_ref, os_, None), (l_ref, ls_, None)], d, P, stage)

    blk = (rows, cw)
    vcol = 2 * DIL_WIDTH // cw
    fix3 = lambda pb, n: (0, 0, 0)
    tok = pl.BlockSpec(blk, lambda pb, n: (n, pb))
    in_specs = [tok, tok, pl.BlockSpec(blk, lambda pb, n: (n, vcol + pb)),
                pl.BlockSpec((DIL_HEADS // 2, 2 * DIL_BLOCK, 2 * DIL_BLOCK), fix3)]
    args = [qh, kh, proj, bias]
    one, two = pltpu.VMEM((n_it, DIL_BLOCK, 128), F32), pltpu.VMEM((n_it, 2 * DIL_BLOCK, 128), F32)
    scratch = [pltpu.VMEM((rows, 128), F32), one, two, two, one, one]
    if has_prev:
        in_specs += [tok, tok]
        args += list(prev)
        scratch += [one, one]
    out = jax.ShapeDtypeStruct((T, DIL_WIDTH), F32)
    return pl.pallas_call(
        body, name=f"dil_fwd_d{d}", grid=(DIL_HEADS // 2 // P, nblk), in_specs=in_specs, out_specs=[tok, tok],
        out_shape=[out, out], scratch_shapes=scratch, compiler_params=_params(2),
    )(*args)


def _dil_bwd(qh, kh, proj, o, lse, do, bias, d, prev):
    T = proj.shape[0]
    P = _dil_pairs(d)
    rows, cw, n_it = DIL_BLOCK * d, 128 * P, d * P
    nblk = T // rows
    has_prev = prev is not None

    def body(*refs):
        q_ref, kc_ref, vc_ref, o_ref, l_ref, do_ref, bias_ref = refs[:7]
        dqi_ref, dki_ref, dvi_ref = refs[7:10] if has_prev else (None, None, None)
        dq_ref, dk_ref, dv_ref, db_ref, stage, qs, ks, vs, os_, ls_, dos, dqs, dks, dvs, ck, cv = refs[7 + 3 * has_prev:]
        pb, n = pl.program_id(0), pl.program_id(1)
        lo, valid = _dil_masks(n)

        @pl.when((pb == 0) & (n == 0))
        def _():
            db_ref[...] = jnp.zeros_like(db_ref)

        @pl.when(n == 0)
        def _():
            ck[...] = jnp.zeros_like(ck)
            cv[...] = jnp.zeros_like(cv)

        _keep_previous_block((ks, vs), n)
        _split_subsequences([(q_ref, qs, None), (kc_ref, ks, 1), (vc_ref, vs, 1),
                             (o_ref, os_, None), (l_ref, ls_, None), (do_ref, dos, None)], d, P, stage)

        def step(i, carry):
            pair = pb * P + i % P
            q2 = _stack_heads(qs[i], lo).astype(BF16)
            kcat, vcat = ks[i].astype(BF16), vs[i].astype(BF16)
            dov = dos[i]
            do2 = _stack_heads(dov, lo).astype(BF16)
            delta = jnp.sum(_stack_heads(dov * os_[i], lo), axis=-1, keepdims=True)
            lse_pair = ls_[i]
            lse2 = jnp.concatenate([_pair_col(lse_pair, lo, 0), _pair_col(lse_pair, lo, 1)], axis=0)
            s = jnp.where(valid, _dot_nt(q2, kcat) + bias_ref[pair], NEG)
            p = jnp.exp(s - lse2)
            ds = p * (_dot_nt(do2, vcat) - delta)
            db_ref[pair] += ds
            dsb = ds.astype(BF16)
            dqs[i] = _unstack_heads(_dot(dsb, kcat), lo)
            dk2 = _dot_tn(dsb, q2)
            dv2 = _dot_tn(p.astype(BF16), do2)
            dks[i] = ck[i] + dk2[:DIL_BLOCK]
            dvs[i] = cv[i] + dv2[:DIL_BLOCK]
            ck[i] = dk2[DIL_BLOCK:]
            cv[i] = dv2[DIL_BLOCK:]
            return carry

        @pl.when(n < nblk)
        def _():
            lax.fori_loop(0, n_it, step, 0, unroll=min(n_it, 8))
            _merge_subsequences([(dq_ref, dqs, dqi_ref), (dk_ref, dks, dki_ref), (dv_ref, dvs, dvi_ref)], d, P, stage)

        @pl.when(n == nblk)
        def _():
            _merge_subsequences([(dk_ref, ck, dki_ref), (dv_ref, cv, dvi_ref)], d, P, stage)

    blk = (rows, cw)
    vcol = 2 * DIL_WIDTH // cw
    qn_ = lambda n: jnp.minimum(n, nblk - 1)
    pn_ = lambda n: jnp.maximum(n - 1, 0)
    fix3 = lambda pb, n: (0, 0, 0)
    tok_q = pl.BlockSpec(blk, lambda pb, n: (qn_(n), pb))
    tok_p = pl.BlockSpec(blk, lambda pb, n: (pn_(n), pb))
    bias_spec = pl.BlockSpec((DIL_HEADS // 2, 2 * DIL_BLOCK, 2 * DIL_BLOCK), fix3)
    in_specs = [tok_q, tok_q, pl.BlockSpec(blk, lambda pb, n: (qn_(n), vcol + pb)), tok_q, tok_q, tok_q, bias_spec]
    in_specs += [tok_q, tok_p, tok_p] if has_prev else []
    tok_shape = jax.ShapeDtypeStruct((T, DIL_WIDTH), F32)
    one, two = pltpu.VMEM((n_it, DIL_BLOCK, 128), F32), pltpu.VMEM((n_it, 2 * DIL_BLOCK, 128), F32)
    dq, dk, dv, db = pl.pallas_call(
        body, name=f"dil_bwd_d{d}", grid=(DIL_HEADS // 2 // P, nblk + 1), in_specs=in_specs,
        out_specs=[tok_q, tok_p, tok_p, bias_spec],
        out_shape=[tok_shape, tok_shape, tok_shape, jax.ShapeDtypeStruct(bias.shape, F32)],
        scratch_shapes=[pltpu.VMEM((rows, 128), F32), one, two, two] + [one] * 8,
        compiler_params=_params(2),
    )(qh, kh, proj, o, lse, do, bias, *(prev or ()))
    return (dq, dk, dv), db


def _t5_bucket(dist):
    max_exact = REL_BUCKETS // 2
    dd = np.maximum(dist, 1).astype(np.float32)
    large = max_exact + (np.log(dd / max_exact) / np.log(REL_MAX_DIST / max_exact)
                         * (REL_BUCKETS - max_exact)).astype(np.int32)
    large = np.minimum(large, REL_BUCKETS - 1)
    return np.where(dist < max_exact, dist, large).astype(np.int32)


def _bucket_onehots():
    i = np.arange(DIL_BLOCK)[:, None]
    j = np.arange(DIL_BLOCK)[None, :]
    out = []
    for _, d in DIL_BRANCHES:
        dist = np.concatenate([DIL_BLOCK + i - j, i - j], axis=1)
        bucket = _t5_bucket(np.clip(dist, 0, None) * d).reshape(-1)
        out.append(jnp.asarray(np.eye(REL_BUCKETS, dtype=np.float32)[:, bucket], BF16))
    return out


def _bias_tables(rel_bias, onehots):
    n = len(onehots)

    def body(rb_ref, *refs):
        parts = _split3(rb_ref[...])
        for k in range(n):
            oh = refs[k][...]
            refs[n + k][...] = _dot(parts[0], oh) + _dot(parts[1], oh) + _dot(parts[2], oh)

    flat = pl.pallas_call(
        body, name="bias_tables",
        out_shape=[jax.ShapeDtypeStruct((DIL_HEADS, 2 * DIL_BLOCK * DIL_BLOCK), F32)] * n,
        compiler_params=pltpu.CompilerParams(vmem_limit_bytes=VMEM_LIMIT),
    )(rel_bias, *onehots)
    return [t.reshape(DIL_HEADS // 2, 2 * DIL_BLOCK, 2 * DIL_BLOCK) for t in flat]


def _bias_grad(dbs, onehots):
    n = len(dbs)
    dbs = [t.reshape(DIL_HEADS, 2 * DIL_BLOCK * DIL_BLOCK) for t in dbs]

    def body(*refs):
        acc = jnp.zeros((DIL_HEADS, REL_BUCKETS), F32)
        for k in range(n):
            oh = refs[n + k][...]
            for part in _split3(refs[k][...]):
                acc = acc + _dot_nt(part, oh)
        refs[-1][...] = acc

    return pl.pallas_call(
        body, name="bias_grad",
        out_shape=jax.ShapeDtypeStruct((DIL_HEADS, REL_BUCKETS), F32),
        compiler_params=pltpu.CompilerParams(vmem_limit_bytes=VMEM_LIMIT),
    )(*dbs, *onehots)


def _swap_halves(x):
    lane = lax.broadcasted_iota(jnp.int32, x.shape, 1)
    first = (lane % 64) < 32
    return jnp.where(first, pltpu.roll(x, 96, 1), pltpu.roll(x, 32, 1))


def _rope_tables(T):
    pos = jnp.arange(T, dtype=F32)
    inv_freq = ROPE_BASE ** (-jnp.arange(0, MLA_ROPE, 2, dtype=F32) / MLA_ROPE)
    ang = pos[:, None] * inv_freq[None, :]
    z = jnp.zeros((T, 128 - MLA_ROPE), F32)
    cos = jnp.concatenate([jnp.cos(ang), jnp.cos(ang), z], axis=-1)
    sin = jnp.concatenate([-jnp.sin(ang), jnp.sin(ang), z], axis=-1)
    return cos, sin


def _mla_prep(proj, cos, sin, g_qa, g_kva, g_q, g_k, wq, wkv, tm=1024):
    T = proj.shape[0]
    H = MLA_HEADS
    scale = MLA_QK ** -0.5

    def body(cq_ref, ckv_ref, kpe_ref, cos_ref, sin_ref, gqa_ref, gkva_ref, gq_ref, gk_ref, wq_ref, wkv_ref,
             q_ref, k_ref, v_ref):
        cosv, sinv = cos_ref[...], sin_ref[...]

        def rope(x):
            return x * cosv + _swap_halves(x) * sinv

        cq = cq_ref[...]
        qp = _dot_nt((cq * _rstd(cq) * gqa_ref[...]).astype(BF16), wq_ref[...])
        ckv = ckv_ref[...]
        kvp = _dot((ckv * _rstd(ckv) * gkva_ref[...]).astype(BF16), wkv_ref[...])
        kpe = kpe_ref[...]
        one_hot_lane = (lax.broadcasted_iota(jnp.int32, (tm, 128), 1) == 0).astype(BF16)
        for h in range(H):
            a = qp[:, MLA_PAD * h:MLA_PAD * (h + 1)]
            qn = a * _rstd(a, MLA_QK) * gq_ref[...]
            q_ref[h, :, 0:128] = (qn[:, 0:128] * scale).astype(BF16)
            q_ref[h, :, 128:256] = (rope(qn[:, 128:256]) * scale).astype(BF16)
            kn = kvp[:, MLA_PAD * h:MLA_PAD * h + 128]
            r = lax.rsqrt((jnp.sum(kn * kn, axis=-1, keepdims=True)
                           + jnp.sum(kpe * kpe, axis=-1, keepdims=True)) / MLA_QK + EPS)
            k_ref[h, :, 0:128] = (kn * r * gk_ref[:, 0:128]).astype(BF16)
            k_ref[h, :, 128:256] = rope(kpe * r * gk_ref[:, 128:256]).astype(BF16)
            v_ref[h, :, 0:128] = kvp[:, MLA_PAD * h + 128:MLA_PAD * (h + 1)].astype(BF16)
            v_ref[h, :, 128:256] = one_hot_lane

    fix = lambda i: (0, 0)
    return pl.pallas_call(
        body, name="mla_prep", grid=(T // tm,),
        in_specs=[pl.BlockSpec((tm, 256), lambda i: (i, 6)), pl.BlockSpec((tm, 128), lambda i: (i, 14)),
                  pl.BlockSpec((tm, 128), lambda i: (i, 15)),
                  pl.BlockSpec((tm, 128), lambda i: (i, 0)), pl.BlockSpec((tm, 128), lambda i: (i, 0)),
                  pl.BlockSpec((1, 256), fix), pl.BlockSpec((1, 128), fix),
                  pl.BlockSpec((1, 256), fix), pl.BlockSpec((1, 256), fix),
                  pl.BlockSpec((H * MLA_PAD, 256), fix), pl.BlockSpec((128, H * MLA_PAD), fix)],
        out_specs=[pl.BlockSpec((H, tm, MLA_PAD), lambda i: (0, i, 0)), pl.BlockSpec((H, tm, MLA_PAD), lambda i: (0, i, 0)),
                   pl.BlockSpec((H, tm, 2 * MLA_V), lambda i: (0, i, 0))],
        out_shape=[jax.ShapeDtypeStruct((H, T, MLA_PAD), BF16), jax.ShapeDtypeStruct((H, T, MLA_PAD), BF16),
                   jax.ShapeDtypeStruct((H, T, 2 * MLA_V), BF16)],
        compiler_params=_params(1),
    )(proj, proj, proj, cos, sin, g_qa, g_kva, g_q, g_k, wq, wkv)


def _mla_prep_bwd(proj, cos, sin, g_qa, g_kva, g_q, g_k, wq, wkv, dq, dk, dv, tm=1024):
    T = proj.shape[0]
    H = MLA_HEADS
    scale = MLA_QK ** -0.5

    def body(cq_ref, ckv_ref, kpe_ref, cos_ref, sin_ref, gqa_ref, gkva_ref, gq_ref, gk_ref, wq_ref, wkv_ref,
             dq_ref, dk_ref, dv_ref,
             dcq_ref, dckv_ref, dkpe_ref, cqn_ref, ckvn_ref, dqp_ref, dkvp_ref,
             dgqa_ref, dgkva_ref, dgq_ref, dgk_ref):
        @pl.when(pl.program_id(0) == 0)
        def _():
            for ref in (dgqa_ref, dgkva_ref, dgq_ref, dgk_ref):
                ref[...] = jnp.zeros_like(ref)

        cosv, sinv = cos_ref[...], sin_ref[...]

        def rope_bwd(dy):
            return dy * cosv + _swap_halves(dy * sinv)

        cq = cq_ref[...]
        rcq = _rstd(cq)
        cqn = (cq * rcq * gqa_ref[...]).astype(BF16)
        cqn_ref[...] = cqn
        qp = _dot_nt(cqn, wq_ref[...])
        ckv = ckv_ref[...]
        rckv = _rstd(ckv)
        ckvn = (ckv * rckv * gkva_ref[...]).astype(BF16)
        ckvn_ref[...] = ckvn
        kvp = _dot(ckvn, wkv_ref[...])
        kpe = kpe_ref[...]
        dkpe = jnp.zeros_like(kpe)
        dgq = jnp.zeros((1, MLA_PAD), F32)
        dgk = jnp.zeros((1, MLA_PAD), F32)
        for h in range(H):
            a = qp[:, MLA_PAD * h:MLA_PAD * (h + 1)]
            dqh = dq_ref[h]
            dn = jnp.concatenate([dqh[:, 0:128], rope_bwd(dqh[:, 128:256])], axis=-1) * scale
            da, dg = _rms_bwd(dn, a, gq_ref[...], _rstd(a, MLA_QK), MLA_QK)
            dgq = dgq + jnp.sum(dg, axis=0, keepdims=True)
            dqp_ref[:, MLA_PAD * h:MLA_PAD * (h + 1)] = da.astype(BF16)

            ak = jnp.concatenate([kvp[:, MLA_PAD * h:MLA_PAD * h + 128], kpe], axis=-1)
            dkh = dk_ref[h]
            dnk = jnp.concatenate([dkh[:, 0:128], rope_bwd(dkh[:, 128:256])], axis=-1)
            dak, dg = _rms_bwd(dnk, ak, gk_ref[...], _rstd(ak, MLA_QK), MLA_QK)
            dgk = dgk + jnp.sum(dg, axis=0, keepdims=True)
            dkpe = dkpe + dak[:, 128:256]
            dkvp_ref[:, MLA_PAD * h:MLA_PAD * h + 128] = dak[:, 0:128].astype(BF16)
            dkvp_ref[:, MLA_PAD * h + 128:MLA_PAD * (h + 1)] = dv_ref[h].astype(BF16)
        dkpe_ref[...] = dkpe
        dgq_ref[...] += dgq
        dgk_ref[...] += dgk
        dcq, dg = _rms_bwd(_dot(dqp_ref[...], wq_ref[...]), cq, gqa_ref[...], rcq)
        dcq_ref[...] = dcq
        dgqa_ref[...] += jnp.sum(dg, axis=0, keepdims=True)
        dckv, dg = _rms_bwd(_dot_nt(dkvp_ref[...], wkv_ref[...]), ckv, gkva_ref[...], rckv)
        dckv_ref[...] = dckv
        dgkva_ref[...] += jnp.sum(dg, axis=0, keepdims=True)

    fix = lambda i: (0, 0)
    row = lambda i: (i, 0)
    head = lambda i: (0, i, 0)
    return pl.pallas_call(
        body, name="mla_prep_bwd", grid=(T // tm,),
        in_specs=[pl.BlockSpec((tm, 256), lambda i: (i, 6)), pl.BlockSpec((tm, 128), lambda i: (i, 14)),
                  pl.BlockSpec((tm, 128), lambda i: (i, 15)),
                  pl.BlockSpec((tm, 128), row), pl.BlockSpec((tm, 128), row),
                  pl.BlockSpec((1, 256), fix), pl.BlockSpec((1, 128), fix),
                  pl.BlockSpec((1, 256), fix), pl.BlockSpec((1, 256), fix),
                  pl.BlockSpec((H * MLA_PAD, 256), fix), pl.BlockSpec((128, H * MLA_PAD), fix),
                  pl.BlockSpec((H, tm, MLA_PAD), head), pl.BlockSpec((H, tm, MLA_PAD), head),
                  pl.BlockSpec((H, tm, MLA_V), head)],
        out_specs=[pl.BlockSpec((tm, 256), row), pl.BlockSpec((tm, 128), row), pl.BlockSpec((tm, 128), row),
                   pl.BlockSpec((tm, 256), row), pl.BlockSpec((tm, 128), row),
                   pl.BlockSpec((tm, H * MLA_PAD), row), pl.BlockSpec((tm, H * MLA_PAD), row),
                   pl.BlockSpec((1, 256), fix), pl.BlockSpec((1, 128), fix),
                   pl.BlockSpec((1, 256), fix), pl.BlockSpec((1, 256), fix)],
        out_shape=[jax.ShapeDtypeStruct((T, 256), F32), jax.ShapeDtypeStruct((T, 128), F32),
                   jax.ShapeDtypeStruct((T, 128), F32),
                   jax.ShapeDtypeStruct((T, 256), BF16), jax.ShapeDtypeStruct((T, 128), BF16),
                   jax.ShapeDtypeStruct((T, H * MLA_PAD), BF16), jax.ShapeDtypeStruct((T, H * MLA_PAD), BF16),
                   jax.ShapeDtypeStruct((1, 256), F32), jax.ShapeDtypeStruct((1, 128), F32),
                   jax.ShapeDtypeStruct((1, 256), F32), jax.ShapeDtypeStruct((1, 256), F32)],
        compiler_params=_params(1),
    )(proj, proj, proj, cos, sin, g_qa, g_kva, g_q, g_k, wq, wkv, dq, dk, dv)


def _causal_pairs(T, tq, tk, key_major):
    pairs = [(i, j) for i in range(T // tq) for j in range(T // tk) if j * tk <= i * tq + tq - 1]
    if key_major:
        pairs.sort(key=lambda p: (p[1], p[0]))
    outer = [p[1] if key_major else p[0] for p in pairs]
    first = [int(t == 0 or outer[t] != outer[t - 1]) for t in range(len(pairs))]
    last = [int(t == len(pairs) - 1 or outer[t] != outer[t + 1]) for t in range(len(pairs))]
    tab = lambda v: jnp.asarray(np.array(v, np.int32))
    return tab([p[0] for p in pairs]), tab([p[1] for p in pairs]), tab(first), tab(last)


def _causal_scores(qv, kv, row0):
    s = _dot_nt(qv, kv)
    if row0 is not None:
        row = lax.broadcasted_iota(jnp.int32, s.shape, 0) + row0
        col = lax.broadcasted_iota(jnp.int32, s.shape, 1)
        s = jnp.where(col <= row, s, NEG)
    return s


def _causal_variants(qi, ki, tq, tk, update):
    assert tk % tq == 0
    diag = qi * tq - ki * tk
    for off in range(0, tk, tq):
        pl.when(diag == off)(lambda off=off: update(off))
    pl.when(diag >= tk)(lambda: update(None))


def _visible_keys(off, row0, rows, tk):
    return tk if off is None else min(tk, off + row0 + rows)


def _mla_attn(q, k, v, ride=None, tq=1024, tk=2048, rc=256):
    H, T, _ = q.shape
    tables = _causal_pairs(T, tq, tk, key_major=False)
    n_pairs = int(tables[0].shape[0])
    r_args, r_in, r_shape, r_out, r_scratch = _ride_parts(ride)

    def body(qt, kt, ft, lt, q_ref, k_ref, v_ref, o_ref, lse_ref, m_s, acc):
        t = pl.program_id(1)
        qi, ki = qt[t], kt[t]

        @pl.when(ft[t] == 1)
        def _():
            m_s[...] = jnp.full_like(m_s, NEG)
            acc[...] = jnp.zeros_like(acc)

        def update(off):
            for c in range(tq // rc):
                rows = pl.ds(c * rc, rc)
                keys = pl.ds(0, _visible_keys(off, c * rc, rc, tk))
                s = _causal_scores(q_ref[rows, :], k_ref[keys, :], None if off is None else off + c * rc)
                m_old = m_s[rows, :]
                m_new = jnp.maximum(m_old, jnp.max(s, axis=-1, keepdims=True))
                p = jnp.exp(s - m_new).astype(BF16)
                acc[rows, :] = jnp.exp(m_old - m_new) * acc[rows, :] + _dot(p, v_ref[keys, :])
                m_s[rows, :] = m_new

        _causal_variants(qi, ki, tq, tk, update)

        @pl.when(lt[t] == 1)
        def _():
            l = jnp.max(acc[:, MLA_V:], axis=-1, keepdims=True)
            o_ref[...] = acc[:, :MLA_V] / l
            lse_ref[...] = jnp.broadcast_to(m_s[...] + jnp.log(l), lse_ref.shape)

    qrow = lambda h, t, qt, kt, ft, lt: (h, qt[t], 0)
    krow = lambda h, t, qt, kt, ft, lt: (h, kt[t], 0)
    first = lambda: (pl.program_id(0) == 0) & (pl.program_id(1) == 0)
    last = lambda: (pl.program_id(0) == H - 1) & (pl.program_id(1) == n_pairs - 1)
    outs = pl.pallas_call(
        _riding(body, 7, 2, 2, ride, first, last), name="mla_attn",
        grid_spec=pltpu.PrefetchScalarGridSpec(
            num_scalar_prefetch=4, grid=(H, n_pairs),
            in_specs=[pl.BlockSpec((None, tq, MLA_PAD), qrow), pl.BlockSpec((None, tk, MLA_PAD), krow),
                      pl.BlockSpec((None, tk, 2 * MLA_V), krow)] + r_in,
            out_specs=[pl.BlockSpec((tq, MLA_V), lambda h, t, qt, kt, ft, lt: (qt[t], h)),
                       pl.BlockSpec((None, tq, 128), qrow)] + r_out,
            scratch_shapes=[pltpu.VMEM((tq, 1), F32), pltpu.VMEM((tq, 2 * MLA_V), F32)] + r_scratch),
        out_shape=[jax.ShapeDtypeStruct((T, H * MLA_V), F32), jax.ShapeDtypeStruct((H, T, 128), F32)] + r_shape,
        compiler_params=_params(2),
    )(*tables, q, k, v, *r_args)
    return outs[:2], outs[2:]


def _mla_attn_bwd(q, k, v, o, lse, do, ride=None, tq=1024, tk=1024, rc=512, rc_diagonal=256):
    H, T, _ = q.shape
    tables = _causal_pairs(T, tq, tk, key_major=True)
    n_pairs = int(tables[0].shape[0])
    r_args, r_in, r_shape, r_out, r_scratch = _ride_parts(ride)

    def body(qt, kt, ft, lt, q_ref, k_ref, v_ref, o_ref, lse_ref, do_ref, dq_ref, dk_ref, dv_ref, dk_s, dv_s):
        t = pl.program_id(1)
        qi, ki = qt[t], kt[t]

        @pl.when(t == 0)
        def _():
            dq_ref[...] = jnp.zeros_like(dq_ref)

        @pl.when(ft[t] == 1)
        def _():
            dk_s[...] = jnp.zeros_like(dk_s)
            dv_s[...] = jnp.zeros_like(dv_s)

        def update(off):
            rows_per = rc if off is None else rc_diagonal
            for c in range(tq // rows_per):
                rows = pl.ds(c * rows_per, rows_per)
                keys = pl.ds(0, _visible_keys(off, c * rows_per, rows_per, tk))
                kk, vv = k_ref[keys, :], v_ref[keys, :]
                qv, dov = q_ref[rows, :], do_ref[rows, :]
                delta = jnp.sum(dov * o_ref[rows, :], axis=-1, keepdims=True)
                lse_v = jnp.max(lse_ref[rows, :], axis=-1, keepdims=True)
                p = jnp.exp(_causal_scores(qv, kk, None if off is None else off + c * rows_per) - lse_v)
                dob = dov.astype(BF16)
                dv_s[keys, :] += _dot_tn(p.astype(BF16), dob)
                ds = (p * (_dot_nt(dob, vv) - delta)).astype(BF16)
                dk_s[keys, :] += _dot_tn(ds, qv)
                out_rows = pl.ds(pl.multiple_of(qi * tq + c * rows_per, rows_per), rows_per)
                dq_ref[out_rows, :] += _dot(ds, kk)

        _causal_variants(qi, ki, tq, tk, update)

        @pl.when(lt[t] == 1)
        def _():
            dk_ref[...] = dk_s[...]
            dv_ref[...] = dv_s[...]

    qrow = lambda h, t, qt, kt, ft, lt: (h, qt[t], 0)
    krow = lambda h, t, qt, kt, ft, lt: (h, kt[t], 0)
    qcol = lambda h, t, qt, kt, ft, lt: (qt[t], h)
    first = lambda: (pl.program_id(0) == 0) & (pl.program_id(1) == 0)
    last = lambda: (pl.program_id(0) == H - 1) & (pl.program_id(1) == n_pairs - 1)
    outs = pl.pallas_call(
        _riding(body, 10, 3, 2, ride, first, last), name="mla_attn_bwd",
        grid_spec=pltpu.PrefetchScalarGridSpec(
            num_scalar_prefetch=4, grid=(H, n_pairs),
            in_specs=[pl.BlockSpec((None, tq, MLA_PAD), qrow), pl.BlockSpec((None, tk, MLA_PAD), krow),
                      pl.BlockSpec((None, tk, MLA_V), krow), pl.BlockSpec((tq, MLA_V), qcol),
                      pl.BlockSpec((None, tq, 128), qrow), pl.BlockSpec((tq, MLA_V), qcol)] + r_in,
            out_specs=[pl.BlockSpec((None, T, MLA_PAD), lambda h, t, qt, kt, ft, lt: (h, 0, 0)),
                       pl.BlockSpec((None, tk, MLA_PAD), krow), pl.BlockSpec((None, tk, MLA_V), krow)] + r_out,
            scratch_shapes=[pltpu.VMEM((tk, MLA_PAD), F32), pltpu.VMEM((tk, MLA_V), F32)] + r_scratch),
        out_shape=[jax.ShapeDtypeStruct((H, T, MLA_PAD), F32), jax.ShapeDtypeStruct((H, T, MLA_PAD), F32),
                   jax.ShapeDtypeStruct((H, T, MLA_V), F32)] + r_shape,
        compiler_params=_params(2),
    )(*tables, q, k, v, o, lse, do, *r_args)
    return outs[:3], outs[3:]


def _pair_gain(g):
    return jnp.tile(g.reshape(1, DIL_HD), (1, 2))


def _pad_gain(g):
    return jnp.pad(g.reshape(1, MLA_QK), ((0, 0), (0, MLA_PAD - MLA_QK)))


def _local_step(x, target, s, comm):
    T = x.shape[0]
    w = comm.w
    gq, gk = _pair_gain(s["dil_q_norm"]) * DIL_HD ** -0.5, _pair_gain(s["dil_k_norm"])
    g_q, g_k = _pad_gain(s["mla_q_norm"]), _pad_gain(s["mla_k_norm"])
    cos, sin = _rope_tables(T)
    onehots = _bucket_onehots()
    biases = _bias_tables(s["rel_bias"], onehots)

    (x1, h1, gate1, up1), got = _ffn_fwd(x, s["ffn1_norm"], w["ffn1_w_gate"], w["ffn1_w_up"], w["ffn1_w_down"],
                                         ride=comm.gather(_GROUPS["attn"]))
    comm.weights_landed(_GROUPS["attn"], got)
    hm, proj, qh, kh = _in_proj(x1, s["mix_norm"], w["w_in"], gq, gk)
    dil = None
    for (_, d), bias in zip(DIL_BRANCHES, biases):
        dil = _dil_fwd(qh, kh, proj, bias, d, dil)
    o_dil, lse_dil = dil
    q, k, v = _mla_prep(proj, cos, sin, s["mla_q_a_norm"], s["mla_kv_a_norm"], g_q, g_k, w["mla_w_q_b"], w["mla_w_kv_b"])
    (o_mla, lse_mla), got = _mla_attn(q, k, v, ride=comm.gather(_GROUPS["ffn2"]))
    comm.weights_landed(_GROUPS["ffn2"], got)
    x2, oc = _out_proj(x1, o_dil, o_mla, s["out_norm_dil"], s["out_norm_mla"], w["w_out"])
    (dy, h2, gate2, up2, loss), _ = _ffn_fwd(x2, s["ffn2_norm"], w["ffn2_w_gate"], w["ffn2_w_up"], w["ffn2_w_down"],
                                             target=target)

    gw, gs = {}, {}

    def ffn_grads(name, dy_in, x_in, h, gate, up, early=None):
        (dx, a, dg, du, dyh, dgain), _ = _ffn_bwd(dy_in, x_in, s[name + "_norm"], gate, up,
                                                  w[name + "_w_gate"], w[name + "_w_up"], w[name + "_w_down"])
        gs[name + "_norm"] = dgain
        down, gate_n, up_n = (name + "_w_down",), (name + "_w_gate",), (name + "_w_up",)
        ride = lambda names: comm.scatter(names, gw) if early is not None else None
        gw[down[0]], landed = _matmul_tn(a, dyh, 1408, 1024, ride=ride(early))
        comm.grads_landed(early or (), landed)
        gw[gate_n[0]], landed = _matmul_tn(dg, h, 1408, 1024, ride=ride(down))
        comm.grads_landed(down, landed)
        gw[up_n[0]], landed = _matmul_tn(du, h, 1408, 1024, ride=ride(gate_n))
        comm.grads_landed(gate_n, landed)
        return dx

    dx2 = ffn_grads("ffn2", dy, x2, h2, gate2, up2)
    gw["w_out"], _ = _matmul_tn(oc, dx2, 1024, 1024)
    do_dil, do_mla, gs["out_norm_dil"], gs["out_norm_mla"] = _out_proj_bwd(
        dx2, o_dil, o_mla, s["out_norm_dil"], s["out_norm_mla"], w["w_out"])

    (dq, dk, dv), got = _mla_attn_bwd(q, k, v, o_mla, lse_mla, do_mla, ride=comm.scatter(_GROUPS["ffn2"], gw))
    comm.grads_landed(_GROUPS["ffn2"], got)
    (dcq, dckv, dkpe, cqn, ckvn, dqp, dkvp, gs["mla_q_a_norm"], gs["mla_kv_a_norm"], dg_q, dg_k) = _mla_prep_bwd(
        proj, cos, sin, s["mla_q_a_norm"], s["mla_kv_a_norm"], g_q, g_k, w["mla_w_q_b"], w["mla_w_kv_b"], dq, dk, dv)
    gs["mla_q_norm"], gs["mla_k_norm"] = dg_q[:, :MLA_QK], dg_k[:, :MLA_QK]
    gw["mla_w_q_b"], _ = _matmul_tn(dqp, cqn, 1024, 256)
    gw["mla_w_kv_b"], _ = _matmul_tn(ckvn, dkvp, 128, 1024)

    dqkv, dbs = None, []
    for (_, d), bias in reversed(list(zip(DIL_BRANCHES, biases))):
        dqkv, db = _dil_bwd(qh, kh, proj, o_dil, lse_dil, do_dil, bias, d, dqkv)
        dbs.insert(0, db)
    dqkv = [dqkv]
    gs["rel_bias"] = _bias_grad(dbs, onehots)

    ready = tuple(n for n in _GROUPS["attn"] if n != "w_in")
    (dx1, dproj, gs["mix_norm"], dgq, dgk), got = _in_proj_bwd(dx2, x1, s["mix_norm"], w["w_in"], proj, gq, gk,
                                                               dqkv, dcq, dckv, dkpe, ride=comm.scatter(ready, gw))
    comm.grads_landed(ready, got)
    gs["dil_q_norm"] = (dgq[:, :DIL_HD] + dgq[:, DIL_HD:]) * DIL_HD ** -0.5
    gs["dil_k_norm"] = dgk[:, :DIL_HD] + dgk[:, DIL_HD:]
    gw["w_in"], _ = _matmul_tn(dproj, hm, 1024, 1024)
    grad_x = ffn_grads("ffn1", dx1, x, h1, gate1, up1, early=("w_in",))
    return loss, grad_x, gw, gs


def _position():
    x, y, c = lax.axis_index("x"), lax.axis_index("y"), lax.axis_index("c")
    return x, y, c, 4 * x + 2 * y + c


def _peer(x, y, c, k):
    px = 1 - x if k & 4 else x
    py = 1 - y if k & 2 else y
    pc = 1 - c if k & 1 else c
    return (px, py, pc), 4 * px + 2 * py + pc


class _Ride:
    def __init__(self, arrays, scatter):
        self.arrays, self.scatter = list(arrays), list(scatter)
        self.n = n = len(self.arrays)
        self.specs = [pl.BlockSpec(memory_space=pl.ANY)] * n
        self.out_shape = [jax.ShapeDtypeStruct(a.shape if sc else (N_DEV,) + a.shape, a.dtype)
                          for a, sc in zip(self.arrays, self.scatter)]
        self.scratch = [pltpu.SemaphoreType.DMA((n, N_DEV - 1)), pltpu.SemaphoreType.DMA((n, N_DEV - 1)),
                        pltpu.SemaphoreType.DMA((n,))]

    def _copies(self, ins, outs, sems):
        send_sems, recv_sems, local_sems = sems
        x, y, c, me = _position()
        copies = []
        for a in range(self.n):
            src = ins[a].at[me] if self.scatter[a] else ins[a]
            copies.append(pltpu.make_async_copy(src, outs[a].at[me], local_sems.at[a]))
        for k in range(1, N_DEV):
            peer, peer_idx = _peer(x, y, c, k)
            for a in range(self.n):
                src = ins[a].at[peer_idx] if self.scatter[a] else ins[a]
                copies.append(pltpu.make_async_remote_copy(
                    src_ref=src, dst_ref=outs[a].at[me], send_sem=send_sems.at[a, k - 1], recv_sem=recv_sems.at[a, k - 1],
                    device_id=peer, device_id_type=pl.DeviceIdType.MESH))
        return copies

    def start(self, ins, outs, sems):
        for cp in self._copies(ins, outs, sems):
            cp.start()

    def wait(self, ins, outs, sems):
        for cp in self._copies(ins, outs, sems):
            cp.wait()


def _ride_parts(ride):
    if ride is None:
        return [], [], [], [], []
    return ride.arrays, ride.specs, ride.out_shape, ride.specs, ride.scratch


def _riding(body, n_in, n_out, n_scratch, ride, first, last):
    if ride is None:
        return body
    n = ride.n
    i1, i2 = n_in + n, n_in + n + n_out
    i3, i4 = i2 + n, i2 + n + n_scratch

    def wrapped(*refs):
        ins, outs, sems = refs[n_in:i1], refs[i2:i3], refs[i4:]

        @pl.when(first())
        def _():
            ride.start(ins, outs, sems)

        body(*refs[:n_in], *refs[i1:i2], *refs[i3:i4])

        @pl.when(last())
        def _():
            ride.wait(ins, outs, sems)

    return wrapped


def _gather_two_level(arrays, name):
    n = len(arrays)
    out_shape = [jax.ShapeDtypeStruct((N_DEV,) + a.shape, a.dtype) for a in arrays]

    def body(*refs):
        ins, outs = refs[:n], refs[n:2 * n]
        send_sems, recv_sems, local_sems = refs[2 * n:]
        x, y, c, me = _position()
        sibling = (x, y, 1 - c)
        chips = [(1 - x, y), (x, 1 - y), (1 - x, 1 - y)]
        block = lambda px, py, pc: 4 * px + 2 * py + pc

        def copy(a, k, blk, to, src=None):
            dst = outs[a].at[blk]
            return pltpu.make_async_remote_copy(
                src_ref=dst if src is None else src, dst_ref=dst, send_sem=send_sems.at[a, k], recv_sem=recv_sems.at[a, k],
                device_id=to, device_id_type=pl.DeviceIdType.MESH)

        local = [pltpu.make_async_copy(ins[a], outs[a].at[me], local_sems.at[a]) for a in range(n)]
        first = []
        for a in range(n):
            first.append(copy(a, 0, me, sibling, src=ins[a]))
            first += [copy(a, 1 + j, me, (*chip, c), src=ins[a]) for j, chip in enumerate(chips)]
        for cp in local + first:
            cp.start()
        passed = []
        for j, chip in enumerate(chips):
            for a in range(n):
                copy(a, 1 + j, block(*chip, c), sibling).wait_recv()
                passed.append(copy(a, 4 + j, block(*chip, c), sibling))
                passed[-1].start()
        for a in range(n):
            copy(a, 0, block(x, y, 1 - c), sibling).wait_recv()
            for j, chip in enumerate(chips):
                copy(a, 4 + j, block(*chip, 1 - c), sibling).wait_recv()
        for cp in first + passed:
            cp.wait_send()
        for cp in local:
            cp.wait()

    any_spec = [pl.BlockSpec(memory_space=pl.ANY)] * n
    return pl.pallas_call(
        body, name=name, in_specs=any_spec, out_specs=any_spec, out_shape=out_shape,
        scratch_shapes=[pltpu.SemaphoreType.DMA((n, N_DEV - 1)), pltpu.SemaphoreType.DMA((n, N_DEV - 1)),
                        pltpu.SemaphoreType.DMA((n,))],
    )(*arrays)


def _adamw_math(wv, g, m, v):
    m = ADAM_B1 * m + (1.0 - ADAM_B1) * g
    v = ADAM_B2 * v + (1.0 - ADAM_B2) * (g * g)
    m_hat = m / (1.0 - ADAM_B1 ** ADAM_STEP)
    v_hat = v / (1.0 - ADAM_B2 ** ADAM_STEP)
    delta = -ADAM_LR * (m_hat / (jnp.sqrt(v_hat) + ADAM_EPS) + ADAM_WD * wv)
    return delta, m, v


def _adamw(items, ride=None, max_rows=256):
    K = len(items)
    tiles, spans, start = [], [], 0
    for _, wv, _, _ in items:
        R = wv.shape[1]
        tr = max([t for t in range(16, max_rows + 1, 16) if R % t == 0] or [R])
        tiles.append(tr)
        spans.append((start, R // tr))
        start += R // tr
    total = start
    r_args, r_in, r_shape, r_out, r_scratch = _ride_parts(ride)

    def body(*refs):
        i = pl.program_id(0)
        for k, (first_step, n_steps) in enumerate(spans):
            def update(k=k):
                p_ref, w_ref, m_ref, v_ref = refs[4 * k:4 * k + 4]
                g_ref, d_ref, mo_ref, vo_ref = refs[4 * K + 4 * k:4 * K + 4 * k + 4]
                g = p_ref[0].astype(F32)
                for j in range(1, N_DEV):
                    g = g + p_ref[j].astype(F32)
                d, mn, vn = _adamw_math(w_ref[0], g, m_ref[0], v_ref[0])
                g_ref[0] = g
                d_ref[0] = d
                mo_ref[0] = mn
                vo_ref[0] = vn

            pl.when((i >= first_step) & (i < first_step + n_steps))(update)

    in_specs, out_specs, out_shape, args = [], [], [], []
    for (parts, wv, m, v), tr, (first_step, n_steps) in zip(items, tiles, spans):
        C = wv.shape[2]
        tile = lambda i, s=first_step, n=n_steps: (0, jnp.clip(i - s, 0, n - 1), 0)
        blk = pl.BlockSpec((1, tr, C), tile)
        in_specs += [pl.BlockSpec((N_DEV, tr, C), tile), blk, blk, blk]
        out_specs += [blk] * 4
        out_shape += [jax.ShapeDtypeStruct(wv.shape, F32)] * 4
        args += [parts, wv, m, v]
    outs = pl.pallas_call(
        _riding(body, 4 * K, 4 * K, 0, ride, lambda: pl.program_id(0) == 0, lambda: pl.program_id(0) == total - 1),
        name="adamw", grid=(total,),
        in_specs=in_specs + r_in, out_specs=out_specs + r_out, out_shape=out_shape + r_shape,
        scratch_shapes=r_scratch, compiler_params=_params(1),
    )(*args, *r_args)
    return [outs[4 * k:4 * k + 4] for k in range(K)], outs[4 * K:]


_TRANSPOSED = ("ffn1_w_gate", "ffn1_w_up", "ffn2_w_gate", "ffn2_w_up", "w_in", "mla_w_q_b")
_GROUPS = {"ffn1": ("ffn1_w_gate", "ffn1_w_up", "ffn1_w_down"),
           "ffn2": ("ffn2_w_gate", "ffn2_w_up", "ffn2_w_down"),
           "attn": ("w_in", "mla_w_q_b", "mla_w_kv_b", "w_out")}
_SMALL = ("ffn1_norm", "mix_norm", "ffn2_norm", "out_norm_dil", "out_norm_mla", "mla_q_a_norm", "rel_bias",
          "mla_q_norm", "mla_k_norm", "mla_kv_a_norm", "dil_q_norm", "dil_k_norm")
_SMALL_ROWS = 48


def _cols_to_full(g):
    return g.transpose(1, 0, 2).reshape(g.shape[1], N_DEV * g.shape[2])


def _full_to_cols(f):
    return f.reshape(f.shape[0], N_DEV, f.shape[1] // N_DEV).transpose(1, 0, 2)


def _shard_view(name, a):
    return jnp.swapaxes(a, 1, 2) if name in _TRANSPOSED else a


def _to_full(name, g):
    if name == "mla_w_kv_b":
        return _cols_to_full(g)
    f = g.reshape(-1, g.shape[-1])
    if name == "w_in":
        f = jnp.pad(f, ((0, PROJ_PAD - PROJ_COLS), (0, 0)))
    if name == "mla_w_q_b":
        f = jnp.pad(f.reshape(MLA_HEADS, MLA_QK, -1), ((0, 0), (0, MLA_PAD - MLA_QK), (0, 0)))
        f = f.reshape(MLA_HEADS * MLA_PAD, -1)
    return f


def _to_parts(name, f):
    if name == "mla_w_kv_b":
        return _full_to_cols(f).astype(BF16)
    if name == "w_in":
        f = f[:PROJ_COLS]
    if name == "mla_w_q_b":
        f = f.reshape(MLA_HEADS, MLA_PAD, -1)[:, :MLA_QK].reshape(MLA_HEADS * MLA_QK, -1)
    return f.reshape(N_DEV, -1, f.shape[-1]).astype(BF16)


class _Comm:
    def __init__(self, shards):
        self.shards, self.w, self.recv = shards, {}, {}

    def gather(self, names):
        return _Ride([self.shards[n] for n in names], [False] * len(names))

    def scatter(self, names, grads):
        return _Ride([_to_parts(n, grads[n]) for n in names], [True] * len(names))

    def weights_landed(self, names, got):
        self.w.update({n: _to_full(n, g) for n, g in zip(names, got)})

    def grads_landed(self, names, got):
        self.recv.update(zip(names, got))


def _pack_small(parts, extra):
    flat = jnp.concatenate([parts[n].reshape(-1) for n in _SMALL] + [extra.reshape(-1)])
    return jnp.pad(flat, (0, _SMALL_ROWS * 128 - flat.shape[0])).reshape(_SMALL_ROWS, 128)


def _unpack_small(packed, shapes):
    flat, out, off = packed.reshape(-1), {}, 0
    for n in _SMALL:
        size = math.prod(shapes[n])
        out[n] = flat[off:off + size].reshape(shapes[n])
        off += size
    return out, flat[off]


_NAMES = ("ffn1_norm", "ffn1_w_gate", "ffn1_w_up", "ffn1_w_down", "mix_norm", "w_in", "dil_q_norm", "dil_k_norm",
          "rel_bias", "mla_q_a_norm", "mla_w_q_b", "mla_kv_a_norm", "mla_w_kv_b", "mla_q_norm", "mla_k_norm",
          "out_norm_dil", "out_norm_mla", "w_out", "ffn2_norm", "ffn2_w_gate", "ffn2_w_up", "ffn2_w_down")


def kernel(x, ffn1_norm, ffn1_w_gate, ffn1_w_up, ffn1_w_down, mix_norm, w_in, dil_q_norm, dil_k_norm, rel_bias, mla_q_a_norm, mla_w_q_b, mla_kv_a_norm, mla_w_kv_b, mla_q_norm, mla_k_norm, out_norm_dil, out_norm_mla, w_out, ffn2_norm, ffn2_w_gate, ffn2_w_up, ffn2_w_down, loss_target, m_ffn1_norm, m_ffn1_w_gate, m_ffn1_w_up, m_ffn1_w_down, m_mix_norm, m_w_in, m_dil_q_norm, m_dil_k_norm, m_rel_bias, m_mla_q_a_norm, m_mla_w_q_b, m_mla_kv_a_norm, m_mla_w_kv_b, m_mla_q_norm, m_mla_k_norm, m_out_norm_dil, m_out_norm_mla, m_w_out, m_ffn2_norm, m_ffn2_w_gate, m_ffn2_w_up, m_ffn2_w_down, v_ffn1_norm, v_ffn1_w_gate, v_ffn1_w_up, v_ffn1_w_down, v_mix_norm, v_w_in, v_dil_q_norm, v_dil_k_norm, v_rel_bias, v_mla_q_a_norm, v_mla_w_q_b, v_mla_kv_a_norm, v_mla_w_kv_b, v_mla_q_norm, v_mla_k_norm, v_out_norm_dil, v_out_norm_mla, v_w_out, v_ffn2_norm, v_ffn2_w_gate, v_ffn2_w_up, v_ffn2_w_down):
    args = locals()
    wts = {n: args[n] for n in _NAMES}
    mom = {n: args["m_" + n] for n in _NAMES}
    var = {n: args["v_" + n] for n in _NAMES}

    matrices = [n for group in _GROUPS.values() for n in group]
    comm = _Comm({n: _shard_view(n, wts[n])[0].astype(BF16) for n in matrices})
    comm.weights_landed(_GROUPS["ffn1"], _gather_two_level(comm.gather(_GROUPS["ffn1"]).arrays, "gather_first"))
    small = {n: wts[n].reshape(1, -1) if n != "rel_bias" else wts[n] for n in _SMALL}

    loss, grad_x, gw, gs = _local_step(x[0], loss_target[0], small, comm)

    item = lambda n: (comm.recv[n],) + tuple(_shard_view(n, a[n]) for a in (wts, mom, var))
    landed = [n for n in matrices if n != "ffn1_w_up"]
    last = comm.scatter(("ffn1_w_up",), gw)
    updates, got = _adamw([item(n) for n in landed], max_rows=32,
                          ride=_Ride(last.arrays + [_pack_small(gs, loss[0, 0])], last.scatter + [False]))
    comm.grads_landed(("ffn1_w_up",), got[:-1])

    zero = jnp.zeros((), F32)
    small_item = (got[-1],) + tuple(_pack_small(a, zero)[None] for a in (wts, mom, var))
    (up_update, packed), _ = _adamw([item("ffn1_w_up"), small_item])
    res = {n: [_shard_view(n, r) for r in u] for n, u in zip(landed + ["ffn1_w_up"], updates + [up_update])}
    shapes = {n: wts[n].shape for n in _SMALL}
    loss_total = None
    for slot, q in enumerate(packed):
        vals, extra = _unpack_small(q, shapes)
        if slot == 0:
            loss_total = extra
        for n in _SMALL:
            res.setdefault(n, [None] * 4)[slot] = vals[n]
    outs = [loss_total, grad_x[None]]
    for slot in range(4):
        outs += [res[n][slot].reshape(wts[n].shape) for n in _NAMES]
    return tuple(outs)
```

```python
import math

import numpy as np
import jax
import jax.numpy as jnp
from jax import lax
from jax.experimental import pallas as pl
from jax.experimental.pallas import tpu as pltpu

F32, BF16 = jnp.float32, jnp.bfloat16
EPS = 1e-6
NEG = -1e30
N_DEV = 8

DIL_HEADS, DIL_HD = 8, 64
DIL_WIDTH = DIL_HEADS * DIL_HD
DIL_BRANCHES = ((128, 1), (512, 4), (2048, 16))
DIL_BLOCK = 128
MLA_HEADS, MLA_NOPE, MLA_ROPE, MLA_V = 4, 128, 64, 128
MLA_QK = MLA_NOPE + MLA_ROPE
MLA_PAD = 256
ROPE_BASE = 10000.0
REL_BUCKETS, REL_MAX_DIST = 32, 2048
PROJ_COLS, PROJ_PAD = 1984, 2048
FFN_RESID = 0.5
ADAM_LR, ADAM_B1, ADAM_B2, ADAM_EPS, ADAM_WD, ADAM_STEP = 0.001, 0.9, 0.999, 1e-08, 0.01, 10
VMEM_LIMIT = 62 * 1024 * 1024

_NT = (((1,), (1,)), ((), ()))
_TN = (((0,), (0,)), ((), ()))


def _dot(a, b):
    return jnp.dot(a, b, preferred_element_type=F32)


def _dot_nt(a, b):
    return lax.dot_general(a, b, _NT, preferred_element_type=F32)


def _dot_tn(a, b):
    return lax.dot_general(a, b, _TN, preferred_element_type=F32)


def _params(n_axes):
    return pltpu.CompilerParams(dimension_semantics=("arbitrary",) * n_axes, vmem_limit_bytes=VMEM_LIMIT)


def _rstd(x, n=None):
    n = x.shape[-1] if n is None else n
    return lax.rsqrt(jnp.sum(x * x, axis=-1, keepdims=True) / n + EPS)


def _rms_bwd(dy, x, g, r, n=None):
    n = x.shape[-1] if n is None else n
    u = dy * g
    dx = r * u - x * (r * r * r) * (jnp.sum(u * x, axis=-1, keepdims=True) / n)
    return dx, dy * x * r


def _sigmoid(x):
    return 1.0 / (1.0 + jnp.exp(-x))


def _split3(x):
    parts = []
    for _ in range(3):
        xb = x.astype(BF16)
        parts.append(xb)
        x = x - xb.astype(F32)
    return parts


def _ffn_fwd(x, gain, wg, wu, wd, ride=None, target=None, tm=512, tf=2816):
    T, D = x.shape
    F = wg.shape[0]
    ni, nj = T // tm, F // tf
    with_loss = target is not None
    r_args, r_in, r_shape, r_out, r_scratch = _ride_parts(ride)

    def body(*refs):
        x_ref, g_ref, wg_ref, wu_ref, wd_ref = refs[:5]
        t_ref = refs[5] if with_loss else None
        xo_ref, h_ref, gate_ref, up_ref = refs[5 + with_loss:9 + with_loss]
        loss_ref = refs[-2] if with_loss else None
        acc = refs[-1]
        i, j = pl.program_id(0), pl.program_id(1)

        @pl.when(j == 0)
        def _():
            xv = x_ref[...]
            h_ref[...] = (xv * _rstd(xv) * g_ref[...]).astype(BF16)
            acc[...] = jnp.zeros_like(acc)

        h = h_ref[...]
        g = _dot_nt(h, wg_ref[...])
        u = _dot_nt(h, wu_ref[...])
        gate_ref[...] = g.astype(BF16)
        up_ref[...] = u.astype(BF16)
        a = (g * _sigmoid(g) * u).astype(BF16)
        acc[...] += _dot(a, wd_ref[...])

        @pl.when(j == nj - 1)
        def _():
            y = x_ref[...] + FFN_RESID * acc[...]
            if with_loss:
                @pl.when(i == 0)
                def _():
                    loss_ref[...] = jnp.zeros_like(loss_ref)

                e = y - t_ref[...]
                xo_ref[...] = e * (1.0 / D)
                loss_ref[...] += (0.5 / D) * jnp.sum(e * e)
            else:
                xo_ref[...] = y

    row = lambda i, j: (i, 0)
    tile = lambda i, j: (i, j)
    n_in, n_out = 5 + with_loss, 4 + with_loss
    first = lambda: (pl.program_id(0) == 0) & (pl.program_id(1) == 0)
    last = lambda: (pl.program_id(0) == ni - 1) & (pl.program_id(1) == nj - 1)
    outs = pl.pallas_call(
        _riding(body, n_in, n_out, 1, ride, first, last), name="ffn_fwd", grid=(ni, nj),
        in_specs=[pl.BlockSpec((tm, D), row), pl.BlockSpec((1, D), lambda i, j: (0, 0)),
                  pl.BlockSpec((tf, D), lambda i, j: (j, 0)), pl.BlockSpec((tf, D), lambda i, j: (j, 0)),
                  pl.BlockSpec((tf, D), lambda i, j: (j, 0))] + [pl.BlockSpec((tm, D), row)] * with_loss + r_in,
        out_specs=[pl.BlockSpec((tm, D), row), pl.BlockSpec((tm, D), row), pl.BlockSpec((tm, tf), tile),
                   pl.BlockSpec((tm, tf), tile)] + [pl.BlockSpec((1, 128), lambda i, j: (0, 0))] * with_loss + r_out,
        out_shape=[jax.ShapeDtypeStruct((T, D), F32), jax.ShapeDtypeStruct((T, D), BF16),
                   jax.ShapeDtypeStruct((T, F), BF16), jax.ShapeDtypeStruct((T, F), BF16)]
        + [jax.ShapeDtypeStruct((1, 128), F32)] * with_loss + r_shape,
        scratch_shapes=[pltpu.VMEM((tm, D), F32)] + r_scratch,
        compiler_params=_params(2),
    )(x, gain, wg, wu, wd, *([target] if with_loss else []), *r_args)
    return outs[:n_out], outs[n_out:]


def _ffn_bwd(dy, x, gain, gate, up, wg, wu, wd, ride=None, tm=256, tf=2816):
    T, D = x.shape
    F = wg.shape[0]
    ni, nj = T // tm, F // tf
    r_args, r_in, r_shape, r_out, r_scratch = _ride_parts(ride)

    def body(dy_ref, x_ref, g_ref, gate_ref, up_ref, wg_ref, wu_ref, wd_ref,
             dx_ref, a_ref, dg_ref, du_ref, dyh_ref, dgain_ref, acc):
        i, j = pl.program_id(0), pl.program_id(1)

        @pl.when((i == 0) & (j == 0))
        def _():
            dgain_ref[...] = jnp.zeros_like(dgain_ref)

        @pl.when(j == 0)
        def _():
            dyh_ref[...] = (FFN_RESID * dy_ref[...]).astype(BF16)
            acc[...] = jnp.zeros_like(acc)

        da = _dot_nt(dyh_ref[...], wd_ref[...])
        g = gate_ref[...].astype(F32)
        u = up_ref[...].astype(F32)
        sig = _sigmoid(g)
        s = g * sig
        a_ref[...] = (s * u).astype(BF16)
        dg = (da * u * (sig * (1.0 + g * (1.0 - sig)))).astype(BF16)
        du = (da * s).astype(BF16)
        dg_ref[...] = dg
        du_ref[...] = du
        acc[...] += _dot(dg, wg_ref[...]) + _dot(du, wu_ref[...])

        @pl.when(j == nj - 1)
        def _():
            xv = x_ref[...]
            dxn, dgc = _rms_bwd(acc[...], xv, g_ref[...], _rstd(xv))
            dx_ref[...] = dy_ref[...] + dxn
            dgain_ref[...] += jnp.sum(dgc, axis=0, keepdims=True)

    first = lambda: (pl.program_id(0) == 0) & (pl.program_id(1) == 0)
    last = lambda: (pl.program_id(0) == ni - 1) & (pl.program_id(1) == nj - 1)
    outs = pl.pallas_call(
        _riding(body, 8, 6, 1, ride, first, last), name="ffn_bwd", grid=(ni, nj),
        in_specs=[pl.BlockSpec((tm, D), lambda i, j: (i, 0)), pl.BlockSpec((tm, D), lambda i, j: (i, 0)),
                  pl.BlockSpec((1, D), lambda i, j: (0, 0)),
                  pl.BlockSpec((tm, tf), lambda i, j: (i, j)), pl.BlockSpec((tm, tf), lambda i, j: (i, j)),
                  pl.BlockSpec((tf, D), lambda i, j: (j, 0)), pl.BlockSpec((tf, D), lambda i, j: (j, 0)),
                  pl.BlockSpec((tf, D), lambda i, j: (j, 0))] + r_in,
        out_specs=[pl.BlockSpec((tm, D), lambda i, j: (i, 0)),
                   pl.BlockSpec((tm, tf), lambda i, j: (i, j)), pl.BlockSpec((tm, tf), lambda i, j: (i, j)),
                   pl.BlockSpec((tm, tf), lambda i, j: (i, j)),
                   pl.BlockSpec((tm, D), lambda i, j: (i, 0)), pl.BlockSpec((1, D), lambda i, j: (0, 0))] + r_out,
        out_shape=[jax.ShapeDtypeStruct((T, D), F32), jax.ShapeDtypeStruct((T, F), BF16),
                   jax.ShapeDtypeStruct((T, F), BF16), jax.ShapeDtypeStruct((T, F), BF16),
                   jax.ShapeDtypeStruct((T, D), BF16), jax.ShapeDtypeStruct((1, D), F32)] + r_shape,
        scratch_shapes=[pltpu.VMEM((tm, D), F32)] + r_scratch,
        compiler_params=_params(2),
    )(dy, x, gain, gate, up, wg, wu, wd, *r_args)
    return outs[:6], outs[6:]


def _matmul_tn(a, b, tk, tn, ride=None, tt=2048):
    T, K = a.shape
    N = b.shape[1]
    tk, tn = min(tk, K), min(tn, N)
    grid = (K // tk, N // tn, T // tt)
    r_args, r_in, r_shape, r_out, r_scratch = _ride_parts(ride)

    def body(a_ref, b_ref, o_ref, acc):
        t = pl.program_id(2)

        @pl.when(t == 0)
        def _():
            acc[...] = jnp.zeros_like(acc)

        acc[...] += _dot_tn(a_ref[...].astype(BF16), b_ref[...].astype(BF16))

        @pl.when(t == grid[2] - 1)
        def _():
            o_ref[...] = acc[...].astype(BF16)

    first = lambda: (pl.program_id(0) == 0) & (pl.program_id(1) == 0) & (pl.program_id(2) == 0)
    last = lambda: ((pl.program_id(0) == grid[0] - 1) & (pl.program_id(1) == grid[1] - 1)
                    & (pl.program_id(2) == grid[2] - 1))
    outs = pl.pallas_call(
        _riding(body, 2, 1, 1, ride, first, last), name="matmul_tn", grid=grid,
        in_specs=[pl.BlockSpec((tt, tk), lambda k, n, t: (t, k)), pl.BlockSpec((tt, tn), lambda k, n, t: (t, n))] + r_in,
        out_specs=[pl.BlockSpec((tk, tn), lambda k, n, t: (k, n))] + r_out,
        out_shape=[jax.ShapeDtypeStruct((K, N), BF16)] + r_shape,
        scratch_shapes=[pltpu.VMEM((tk, tn), F32)] + r_scratch,
        compiler_params=_params(3),
    )(a, b, *r_args)
    return outs[0], outs[1:]


def _in_proj(x, gain, w, gq, gk, tm=1024):
    T, D = x.shape
    N = w.shape[0]
    W = DIL_WIDTH

    def body(x_ref, g_ref, w_ref, gq_ref, gk_ref, h_ref, p_ref, qh_ref, kh_ref):
        xv = x_ref[...]
        h = (xv * _rstd(xv) * g_ref[...]).astype(BF16)
        h_ref[...] = h
        p_ref[...] = _dot_nt(h, w_ref[...])
        lo = lax.broadcasted_iota(jnp.int32, (tm, 128), 1) < DIL_HD
        for hp in range(DIL_HEADS // 2):
            q = p_ref[:, 128 * hp:128 * (hp + 1)]
            k = p_ref[:, W + 128 * hp:W + 128 * (hp + 1)]
            qh_ref[:, 128 * hp:128 * (hp + 1)] = (q * _pair_rstd(q, lo) * gq_ref[...]).astype(BF16).astype(F32)
            kh_ref[:, 128 * hp:128 * (hp + 1)] = (k * _pair_rstd(k, lo) * gk_ref[...]).astype(BF16).astype(F32)

    row = lambda i: (i, 0)
    fix = lambda i: (0, 0)
    return pl.pallas_call(
        body, name="in_proj", grid=(T // tm,),
        in_specs=[pl.BlockSpec((tm, D), row), pl.BlockSpec((1, D), fix), pl.BlockSpec((N, D), fix),
                  pl.BlockSpec((1, 128), fix), pl.BlockSpec((1, 128), fix)],
        out_specs=[pl.BlockSpec((tm, D), row), pl.BlockSpec((tm, N), row), pl.BlockSpec((tm, W), row),
                   pl.BlockSpec((tm, W), row)],
        out_shape=[jax.ShapeDtypeStruct((T, D), BF16), jax.ShapeDtypeStruct((T, N), F32),
                   jax.ShapeDtypeStruct((T, W), F32), jax.ShapeDtypeStruct((T, W), F32)],
        compiler_params=_params(1),
    )(x, gain, w, gq, gk)


def _in_proj_bwd(dx_up, x, gain, w, proj, gq, gk, dqkv, dcq, dckv, dkpe, ride=None, tm=512):
    T, D = x.shape
    N = w.shape[0]
    W = DIL_WIDTH
    nb = len(dqkv)

    def body(*refs):
        dxu_ref, x_ref, g_ref, w_ref, q_ref, k_ref, gq_ref, gk_ref = refs[:8]
        dil_refs = refs[8:8 + 3 * nb]
        dcq_ref, dckv_ref, dkpe_ref, dx_ref, dp_ref, dgain_ref, dgq_ref, dgk_ref = refs[8 + 3 * nb:]

        @pl.when(pl.program_id(0) == 0)
        def _():
            for ref in (dgain_ref, dgq_ref, dgk_ref):
                ref[...] = jnp.zeros_like(ref)

        lo = lax.broadcasted_iota(jnp.int32, (tm, 128), 1) < DIL_HD
        norms = ((q_ref, gq_ref, dgq_ref), (k_ref, gk_ref, dgk_ref))
        for part in range(3):
            acc = dil_refs[part][...]
            for b in range(1, nb):
                acc = acc + dil_refs[3 * b + part][...]
            if part == 2:
                dp_ref[:, 2 * W:3 * W] = acc.astype(BF16)
                continue
            raw_ref, gn_ref, dgn_ref = norms[part]
            for hp in range(DIL_HEADS // 2):
                raw = raw_ref[:, 128 * hp:128 * (hp + 1)]
                d_raw, dgn = _pair_rms_bwd(acc[:, 128 * hp:128 * (hp + 1)], raw, _pair_rstd(raw, lo), gn_ref[...], lo)
                dp_ref[:, part * W + 128 * hp:part * W + 128 * (hp + 1)] = d_raw.astype(BF16)
                dgn_ref[...] += dgn
        dp_ref[:, 3 * W:3 * W + 256] = dcq_ref[...].astype(BF16)
        dp_ref[:, 3 * W + 256:3 * W + 384] = dckv_ref[...].astype(BF16)
        dp_ref[:, 3 * W + 384:N] = dkpe_ref[...].astype(BF16)
        dh = _dot(dp_ref[...], w_ref[...])
        xv = x_ref[...]
        dxn, dgc = _rms_bwd(dh, xv, g_ref[...], _rstd(xv))
        dx_ref[...] = dxu_ref[...] + dxn
        dgain_ref[...] += jnp.sum(dgc, axis=0, keepdims=True)

    row = lambda i: (i, 0)
    fix = lambda i: (0, 0)
    r_args, r_in, r_shape, r_out, r_scratch = _ride_parts(ride)
    first = lambda: pl.program_id(0) == 0
    last = lambda: pl.program_id(0) == T // tm - 1
    outs = pl.pallas_call(
        _riding(body, 11 + 3 * nb, 5, 0, ride, first, last), name="in_proj_bwd", grid=(T // tm,),
        in_specs=[pl.BlockSpec((tm, D), row), pl.BlockSpec((tm, D), row), pl.BlockSpec((1, D), fix),
                  pl.BlockSpec((N, D), fix), pl.BlockSpec((tm, W), row), pl.BlockSpec((tm, W), lambda i: (i, 1)),
                  pl.BlockSpec((1, 128), fix), pl.BlockSpec((1, 128), fix)] + [pl.BlockSpec((tm, W), row)] * (3 * nb)
                 + [pl.BlockSpec((tm, 256), row), pl.BlockSpec((tm, 128), row), pl.BlockSpec((tm, 128), row)] + r_in,
        out_specs=[pl.BlockSpec((tm, D), row), pl.BlockSpec((tm, N), row), pl.BlockSpec((1, D), fix),
                   pl.BlockSpec((1, 128), fix), pl.BlockSpec((1, 128), fix)] + r_out,
        out_shape=[jax.ShapeDtypeStruct((T, D), F32), jax.ShapeDtypeStruct((T, N), BF16),
                   jax.ShapeDtypeStruct((1, D), F32), jax.ShapeDtypeStruct((1, 128), F32),
                   jax.ShapeDtypeStruct((1, 128), F32)] + r_shape,
        scratch_shapes=r_scratch,
        compiler_params=_params(1),
    )(dx_up, x, gain, w, proj, proj, gq, gk, *[a for triple in dqkv for a in triple], dcq, dckv, dkpe, *r_args)
    return outs[:5], outs[5:]


def _out_proj(x, o_dil, o_mla, g_dil, g_mla, w, tm=1024):
    T, D = x.shape
    W = o_dil.shape[1]

    def body(x_ref, od_ref, om_ref, gd_ref, gm_ref, w_ref, xo_ref, oc_ref):
        od, om = od_ref[...], om_ref[...]
        oc_ref[:, 0:W] = (od * _rstd(od) * gd_ref[...]).astype(BF16)
        oc_ref[:, W:2 * W] = (om * _rstd(om) * gm_ref[...]).astype(BF16)
        xo_ref[...] = x_ref[...] + _dot(oc_ref[...], w_ref[...])

    row = lambda i: (i, 0)
    fix = lambda i: (0, 0)
    return pl.pallas_call(
        body, name="out_proj", grid=(T // tm,),
        in_specs=[pl.BlockSpec((tm, D), row), pl.BlockSpec((tm, W), row), pl.BlockSpec((tm, W), row),
                  pl.BlockSpec((1, W), fix), pl.BlockSpec((1, W), fix), pl.BlockSpec((2 * W, D), fix)],
        out_specs=[pl.BlockSpec((tm, D), row), pl.BlockSpec((tm, 2 * W), row)],
        out_shape=[jax.ShapeDtypeStruct((T, D), F32), jax.ShapeDtypeStruct((T, 2 * W), BF16)],
        compiler_params=_params(1),
    )(x, o_dil, o_mla, g_dil, g_mla, w)


def _out_proj_bwd(dx, o_dil, o_mla, g_dil, g_mla, w, tm=1024):
    T, D = dx.shape
    W = o_dil.shape[1]

    def body(dx_ref, od_ref, om_ref, gd_ref, gm_ref, w_ref, dod_ref, dom_ref, dgd_ref, dgm_ref):
        @pl.when(pl.program_id(0) == 0)
        def _():
            dgd_ref[...] = jnp.zeros_like(dgd_ref)
            dgm_ref[...] = jnp.zeros_like(dgm_ref)

        doc = _dot_nt(dx_ref[...].astype(BF16), w_ref[...])
        od, om = od_ref[...], om_ref[...]
        dod, dgd = _rms_bwd(doc[:, 0:W], od, gd_ref[...], _rstd(od))
        dom, dgm = _rms_bwd(doc[:, W:2 * W], om, gm_ref[...], _rstd(om))
        dod_ref[...] = dod
        dom_ref[...] = dom
        dgd_ref[...] += jnp.sum(dgd, axis=0, keepdims=True)
        dgm_ref[...] += jnp.sum(dgm, axis=0, keepdims=True)

    row = lambda i: (i, 0)
    fix = lambda i: (0, 0)
    return pl.pallas_call(
        body, name="out_proj_bwd", grid=(T // tm,),
        in_specs=[pl.BlockSpec((tm, D), row), pl.BlockSpec((tm, W), row), pl.BlockSpec((tm, W), row),
                  pl.BlockSpec((1, W), fix), pl.BlockSpec((1, W), fix), pl.BlockSpec((2 * W, D), fix)],
        out_specs=[pl.BlockSpec((tm, W), row), pl.BlockSpec((tm, W), row),
                   pl.BlockSpec((1, W), fix), pl.BlockSpec((1, W), fix)],
        out_shape=[jax.ShapeDtypeStruct((T, W), F32), jax.ShapeDtypeStruct((T, W), F32),
                   jax.ShapeDtypeStruct((1, W), F32), jax.ShapeDtypeStruct((1, W), F32)],
        compiler_params=_params(1),
    )(dx, o_dil, o_mla, g_dil, g_mla, w)


def _pair_rstd(x, lo):
    sq = x * x
    s0 = jnp.sum(jnp.where(lo, sq, 0.0), axis=-1, keepdims=True)
    s1 = jnp.sum(jnp.where(lo, 0.0, sq), axis=-1, keepdims=True)
    return jnp.where(lo, lax.rsqrt(s0 / DIL_HD + EPS), lax.rsqrt(s1 / DIL_HD + EPS))


def _pair_rms_bwd(dn, x, r, g, lo):
    u = dn * g
    t = u * x
    d0 = jnp.sum(jnp.where(lo, t, 0.0), axis=-1, keepdims=True)
    d1 = jnp.sum(jnp.where(lo, 0.0, t), axis=-1, keepdims=True)
    dx = r * u - x * (r * r * r) * (jnp.where(lo, d0, d1) / DIL_HD)
    return dx, jnp.sum(dn * x * r, axis=0, keepdims=True)


def _pair_col(x, lo, e):
    sel = lo if e == 0 else jnp.logical_not(lo)
    return jnp.max(jnp.where(sel, x, NEG), axis=-1, keepdims=True)


def _first_head_lanes():
    return lax.broadcasted_iota(jnp.int32, (DIL_BLOCK, DIL_BLOCK), 1) < DIL_HD


def _window_masks():
    i = np.arange(DIL_BLOCK)[:, None]
    j = np.arange(DIL_BLOCK)[None, :]
    cur = j <= i
    both = np.concatenate([j >= i, cur], axis=1)
    first = np.concatenate([np.zeros_like(cur), cur], axis=1)
    return jnp.asarray(np.where(np.stack([both, first]), 0.0, NEG).reshape(2, -1), F32)


def _stack_heads(x, lo):
    return jnp.concatenate([jnp.where(lo, x, 0.0), jnp.where(lo, 0.0, x)], axis=0)


def _unstack_heads(x2, lo):
    return jnp.where(lo, x2[:DIL_BLOCK], x2[DIL_BLOCK:])


def _dil_pairs(d):
    return 4 if d == 1 else 1


def _sub_rows(r, d):
    return pl.ds(r, DIL_BLOCK, stride=d) if d > 1 else pl.ds(0, DIL_BLOCK)


def _store_piece(scratch, i, part, piece):
    if part is None:
        scratch[i] = piece
    else:
        scratch[i, pl.ds(DIL_BLOCK * part, DIL_BLOCK), :] = piece


def _split_subsequences(loads, d, P, stage=None):
    if d == 16:
        group = 4 * DIL_BLOCK
        for block, scratch, part in loads:
            for a in range(4):
                stage[pl.ds(a * group, group), :] = block[pl.ds(a, group, stride=4), :]
            for a in range(4):
                for b in range(4):
                    _store_piece(scratch, a + 4 * b, part, stage[pl.ds(a * group + b, DIL_BLOCK, stride=4), :])
        return
    for r in range(d):
        for p in range(P):
            for block, scratch, part in loads:
                _store_piece(scratch, r * P + p, part, block[_sub_rows(r, d), pl.ds(128 * p, 128)])


def _keep_previous_block(scratches, n):
    for scratch in scratches:
        @pl.when(n == 0)
        def _():
            scratch[:, pl.ds(0, DIL_BLOCK), :] = jnp.zeros((scratch.shape[0], DIL_BLOCK, 128), F32)

        @pl.when(n > 0)
        def _():
            scratch[:, pl.ds(0, DIL_BLOCK), :] = scratch[:, pl.ds(DIL_BLOCK, DIL_BLOCK), :]


def _merge_subsequences(stores, d, P, stage=None):
    if d == 16:
        group = 4 * DIL_BLOCK
        for block, scratch, plus in stores:
            for a in range(4):
                for b in range(4):
                    stage[pl.ds(a * group + b, DIL_BLOCK, stride=4), :] = scratch[a + 4 * b]
            for a in range(4):
                rows = pl.ds(a, group, stride=4)
                val = stage[pl.ds(a * group, group), :]
                block[rows, :] = val if plus is None else val + plus[rows, :]
        return
    for r in range(d):
        for p in range(P):
            for block, scratch, plus in stores:
                part = _sub_rows(r, d), pl.ds(128 * p, 128)
                block[part] = scratch[r * P + p] if plus is None else scratch[r * P + p] + plus[part]


def _dil_fwd(qh, kh, proj, bias, d, prev):
    T = proj.shape[0]
    P = _dil_pairs(d)
    rows, cw, n_it = DIL_BLOCK * d, 128 * P, d * P
    nblk = T // rows
    has_prev = prev is not None

    def body(*refs):
        q_ref, kc_ref, vc_ref, bias_ref = refs[:4]
        refs = refs[4:]
        if has_prev:
            oin_ref, lin_ref = refs[:2]
            refs = refs[2:]
        o_ref, l_ref, stage, qs, ks, vs, os_, ls_ = refs[:8]
        pb, n = pl.program_id(0), pl.program_id(1)
        lo = _first_head_lanes()
        first = (n == 0).astype(jnp.int32)
        _keep_previous_block((ks, vs), n)
        loads = [(q_ref, qs, None), (kc_ref, ks, 1), (vc_ref, vs, 1)]
        if has_prev:
            ois, lis = refs[8:]
            loads += [(oin_ref, ois, None), (lin_ref, lis, None)]
        _split_subsequences(loads, d, P, stage)

        def step(i, carry):
            q2 = _stack_heads(qs[i], lo).astype(BF16)
            s = _dot_nt(q2, ks[i].astype(BF16)) + bias_ref[first, pb * P + i % P]
            m = jnp.max(s, axis=-1, keepdims=True)
            p = jnp.exp(s - m)
            l = jnp.sum(p, axis=-1, keepdims=True)
            o = _unstack_heads(_dot(p.astype(BF16), vs[i].astype(BF16)) / l, lo)
            lse = _unstack_heads(jnp.broadcast_to(m + jnp.log(l), (2 * DIL_BLOCK, 128)), lo)
            if has_prev:
                lin = lis[i]
                mx = jnp.maximum(lin, lse)
                lnew = mx + jnp.log(jnp.exp(lin - mx) + jnp.exp(lse - mx))
                o = ois[i] * jnp.exp(lin - lnew) + o * jnp.exp(lse - lnew)
                lse = lnew
            os_[i] = o
            ls_[i] = lse
            return carry

        lax.fori_loop(0, n_it, step, 0, unroll=min(n_it, 8))
        _merge_subsequences([(o_ref, os_, None), (l_ref, ls_, None)], d, P, stage)

    blk = (rows, cw)
    vcol = 2 * DIL_WIDTH // cw
    tok = pl.BlockSpec(blk, lambda pb, n: (n, pb))
    in_specs = [tok, tok, pl.BlockSpec(blk, lambda pb, n: (n, vcol + pb)),
                pl.BlockSpec(bias.shape, lambda pb, n: (0, 0, 0, 0))]
    args = [qh, kh, proj, bias]
    one, two = pltpu.VMEM((n_it, DIL_BLOCK, 128), F32), pltpu.VMEM((n_it, 2 * DIL_BLOCK, 128), F32)
    scratch = [pltpu.VMEM((rows, 128), F32), one, two, two, one, one]
    if has_prev:
        in_specs += [tok, tok]
        args += list(prev)
        scratch += [one, one]
    out = jax.ShapeDtypeStruct((T, DIL_WIDTH), F32)
    return pl.pallas_call(
        body, name=f"dil_fwd_d{d}", grid=(DIL_HEADS // 2 // P, nblk), in_specs=in_specs, out_specs=[tok, tok],
        out_shape=[out, out], scratch_shapes=scratch, compiler_params=_params(2),
    )(*args)


def _dil_bwd(qh, kh, proj, o, lse, do, bias, d, prev):
    T = proj.shape[0]
    P = _dil_pairs(d)
    rows, cw, n_it = DIL_BLOCK * d, 128 * P, d * P
    nblk = T // rows
    has_prev = prev is not None

    def body(*refs):
        q_ref, kc_ref, vc_ref, o_ref, l_ref, do_ref, bias_ref = refs[:7]
        dqi_ref, dki_ref, dvi_ref = refs[7:10] if has_prev else (None, None, None)
        dq_ref, dk_ref, dv_ref, db_ref, stage, qs, ks, vs, os_, ls_, dos, dqs, dks, dvs, ck, cv = refs[7 + 3 * has_prev:]
        pb, n = pl.program_id(0), pl.program_id(1)
        lo = _first_head_lanes()
        first = (n == 0).astype(jnp.int32)

        @pl.when((pb == 0) & (n == 0))
        def _():
            db_ref[...] = jnp.zeros_like(db_ref)

        @pl.when(n == 0)
        def _():
            ck[...] = jnp.zeros_like(ck)
            cv[...] = jnp.zeros_like(cv)

        _keep_previous_block((ks, vs), n)
        _split_subsequences([(q_ref, qs, None), (kc_ref, ks, 1), (vc_ref, vs, 1),
                             (o_ref, os_, None), (l_ref, ls_, None), (do_ref, dos, None)], d, P, stage)

        def step(i, carry):
            pair = pb * P + i % P
            q2 = _stack_heads(qs[i], lo).astype(BF16)
            kcat, vcat = ks[i].astype(BF16), vs[i].astype(BF16)
            dov = dos[i]
            do2 = _stack_heads(dov, lo).astype(BF16)
            delta = jnp.sum(_stack_heads(dov * os_[i], lo), axis=-1, keepdims=True)
            lse_pair = ls_[i]
            lse2 = jnp.concatenate([_pair_col(lse_pair, lo, 0), _pair_col(lse_pair, lo, 1)], axis=0)
            s = _dot_nt(q2, kcat) + bias_ref[first, pair]
            p = jnp.exp(s - lse2)
            ds = p * (_dot_nt(do2, vcat) - delta)
            db_ref[pair] += ds
            dsb = ds.astype(BF16)
            dqs[i] = _unstack_heads(_dot(dsb, kcat), lo)
            dk2 = _dot_tn(dsb, q2)
            dv2 = _dot_tn(p.astype(BF16), do2)
            dks[i] = ck[i] + dk2[:DIL_BLOCK]
            dvs[i] = cv[i] + dv2[:DIL_BLOCK]
            ck[i] = dk2[DIL_BLOCK:]
            cv[i] = dv2[DIL_BLOCK:]
            return carry

        @pl.when(n < nblk)
        def _():
            lax.fori_loop(0, n_it, step, 0, unroll=min(n_it, 8))
            _merge_subsequences([(dq_ref, dqs, dqi_ref), (dk_ref, dks, dki_ref), (dv_ref, dvs, dvi_ref)], d, P, stage)

        @pl.when(n == nblk)
        def _():
            _merge_subsequences([(dk_ref, ck, dki_ref), (dv_ref, cv, dvi_ref)], d, P, stage)

    blk = (rows, cw)
    vcol = 2 * DIL_WIDTH // cw
    qn_ = lambda n: jnp.minimum(n, nblk - 1)
    pn_ = lambda n: jnp.maximum(n - 1, 0)
    fix3 = lambda pb, n: (0, 0, 0)
    tok_q = pl.BlockSpec(blk, lambda pb, n: (qn_(n), pb))
    tok_p = pl.BlockSpec(blk, lambda pb, n: (pn_(n), pb))
    in_specs = [tok_q, tok_q, pl.BlockSpec(blk, lambda pb, n: (qn_(n), vcol + pb)), tok_q, tok_q, tok_q,
                pl.BlockSpec(bias.shape, lambda pb, n: (0, 0, 0, 0))]
    in_specs += [tok_q, tok_p, tok_p] if has_prev else []
    tok_shape = jax.ShapeDtypeStruct((T, DIL_WIDTH), F32)
    one, two = pltpu.VMEM((n_it, DIL_BLOCK, 128), F32), pltpu.VMEM((n_it, 2 * DIL_BLOCK, 128), F32)
    dq, dk, dv, db = pl.pallas_call(
        body, name=f"dil_bwd_d{d}", grid=(DIL_HEADS // 2 // P, nblk + 1), in_specs=in_specs,
        out_specs=[tok_q, tok_p, tok_p, pl.BlockSpec(bias.shape[1:], fix3)],
        out_shape=[tok_shape, tok_shape, tok_shape, jax.ShapeDtypeStruct(bias.shape[1:], F32)],
        scratch_shapes=[pltpu.VMEM((rows, 128), F32), one, two, two] + [one] * 8,
        compiler_params=_params(2),
    )(qh, kh, proj, o, lse, do, bias, *(prev or ()))
    return (dq, dk, dv), db


def _t5_bucket(dist):
    max_exact = REL_BUCKETS // 2
    dd = np.maximum(dist, 1).astype(np.float32)
    large = max_exact + (np.log(dd / max_exact) / np.log(REL_MAX_DIST / max_exact)
                         * (REL_BUCKETS - max_exact)).astype(np.int32)
    large = np.minimum(large, REL_BUCKETS - 1)
    return np.where(dist < max_exact, dist, large).astype(np.int32)


def _bucket_onehots():
    i = np.arange(DIL_BLOCK)[:, None]
    j = np.arange(DIL_BLOCK)[None, :]
    out = []
    for _, d in DIL_BRANCHES:
        dist = np.concatenate([DIL_BLOCK + i - j, i - j], axis=1)
        bucket = _t5_bucket(np.clip(dist, 0, None) * d).reshape(-1)
        out.append(jnp.asarray(np.eye(REL_BUCKETS, dtype=np.float32)[:, bucket], BF16))
    return out


def _bias_tables(rel_bias, onehots):
    n = len(onehots)

    def body(rb_ref, mask_ref, *refs):
        parts = _split3(rb_ref[...])
        for k in range(n):
            oh = refs[k][...]
            bias = _dot(parts[0], oh) + _dot(parts[1], oh) + _dot(parts[2], oh)
            refs[n + k][0] = bias + mask_ref[0:1, :]
            refs[n + k][1] = bias + mask_ref[1:2, :]

    flat = pl.pallas_call(
        body, name="bias_tables",
        out_shape=[jax.ShapeDtypeStruct((2, DIL_HEADS, 2 * DIL_BLOCK * DIL_BLOCK), F32)] * n,
        compiler_params=pltpu.CompilerParams(vmem_limit_bytes=VMEM_LIMIT),
    )(rel_bias, _window_masks(), *onehots)
    return [t.reshape(2, DIL_HEADS // 2, 2 * DIL_BLOCK, 2 * DIL_BLOCK) for t in flat]


def _bias_grad(dbs, onehots):
    n = len(dbs)
    dbs = [t.reshape(DIL_HEADS, 2 * DIL_BLOCK * DIL_BLOCK) for t in dbs]

    def body(*refs):
        acc = jnp.zeros((DIL_HEADS, REL_BUCKETS), F32)
        for k in range(n):
            oh = refs[n + k][...]
            for part in _split3(refs[k][...]):
                acc = acc + _dot_nt(part, oh)
        refs[-1][...] = acc

    return pl.pallas_call(
        body, name="bias_grad",
        out_shape=jax.ShapeDtypeStruct((DIL_HEADS, REL_BUCKETS), F32),
        compiler_params=pltpu.CompilerParams(vmem_limit_bytes=VMEM_LIMIT),
    )(*dbs, *onehots)


def _swap_halves(x):
    lane = lax.broadcasted_iota(jnp.int32, x.shape, 1)
    first = (lane % 64) < 32
    return jnp.where(first, pltpu.roll(x, 96, 1), pltpu.roll(x, 32, 1))


def _rope_tables(T):
    pos = jnp.arange(T, dtype=F32)
    inv_freq = ROPE_BASE ** (-jnp.arange(0, MLA_ROPE, 2, dtype=F32) / MLA_ROPE)
    ang = pos[:, None] * inv_freq[None, :]
    z = jnp.zeros((T, 128 - MLA_ROPE), F32)
    cos = jnp.concatenate([jnp.cos(ang), jnp.cos(ang), z], axis=-1)
    sin = jnp.concatenate([-jnp.sin(ang), jnp.sin(ang), z], axis=-1)
    return cos, sin


def _mla_prep(proj, cos, sin, g_qa, g_kva, g_q, g_k, wq, wkv, tm=1024):
    T = proj.shape[0]
    H = MLA_HEADS
    scale = MLA_QK ** -0.5

    def body(cq_ref, ckv_ref, kpe_ref, cos_ref, sin_ref, gqa_ref, gkva_ref, gq_ref, gk_ref, wq_ref, wkv_ref,
             q_ref, k_ref, v_ref):
        cosv, sinv = cos_ref[...], sin_ref[...]

        def rope(x):
            return x * cosv + _swap_halves(x) * sinv

        cq = cq_ref[...]
        qp = _dot_nt((cq * _rstd(cq) * gqa_ref[...]).astype(BF16), wq_ref[...])
        ckv = ckv_ref[...]
        kvp = _dot((ckv * _rstd(ckv) * gkva_ref[...]).astype(BF16), wkv_ref[...])
        kpe = kpe_ref[...]
        one_hot_lane = (lax.broadcasted_iota(jnp.int32, (tm, 128), 1) == 0).astype(BF16)
        for h in range(H):
            a = qp[:, MLA_PAD * h:MLA_PAD * (h + 1)]
            qn = a * _rstd(a, MLA_QK) * gq_ref[...]
            q_ref[h, :, 0:128] = (qn[:, 0:128] * scale).astype(BF16)
            q_ref[h, :, 128:256] = (rope(qn[:, 128:256]) * scale).astype(BF16)
            kn = kvp[:, MLA_PAD * h:MLA_PAD * h + 128]
            r = lax.rsqrt((jnp.sum(kn * kn, axis=-1, keepdims=True)
                           + jnp.sum(kpe * kpe, axis=-1, keepdims=True)) / MLA_QK + EPS)
            k_ref[h, :, 0:128] = (kn * r * gk_ref[:, 0:128]).astype(BF16)
            k_ref[h, :, 128:256] = rope(kpe * r * gk_ref[:, 128:256]).astype(BF16)
            v_ref[h, :, 0:128] = kvp[:, MLA_PAD * h + 128:MLA_PAD * (h + 1)].astype(BF16)
            v_ref[h, :, 128:256] = one_hot_lane

    fix = lambda i: (0, 0)
    return pl.pallas_call(
        body, name="mla_prep", grid=(T // tm,),
        in_specs=[pl.BlockSpec((tm, 256), lambda i: (i, 6)), pl.BlockSpec((tm, 128), lambda i: (i, 14)),
                  pl.BlockSpec((tm, 128), lambda i: (i, 15)),
                  pl.BlockSpec((tm, 128), lambda i: (i, 0)), pl.BlockSpec((tm, 128), lambda i: (i, 0)),
                  pl.BlockSpec((1, 256), fix), pl.BlockSpec((1, 128), fix),
                  pl.BlockSpec((1, 256), fix), pl.BlockSpec((1, 256), fix),
                  pl.BlockSpec((H * MLA_PAD, 256), fix), pl.BlockSpec((128, H * MLA_PAD), fix)],
        out_specs=[pl.BlockSpec((H, tm, MLA_PAD), lambda i: (0, i, 0)), pl.BlockSpec((H, tm, MLA_PAD), lambda i: (0, i, 0)),
                   pl.BlockSpec((H, tm, 2 * MLA_V), lambda i: (0, i, 0))],
        out_shape=[jax.ShapeDtypeStruct((H, T, MLA_PAD), BF16), jax.ShapeDtypeStruct((H, T, MLA_PAD), BF16),
                   jax.ShapeDtypeStruct((H, T, 2 * MLA_V), BF16)],
        compiler_params=_params(1),
    )(proj, proj, proj, cos, sin, g_qa, g_kva, g_q, g_k, wq, wkv)


def _mla_prep_bwd(proj, cos, sin, g_qa, g_kva, g_q, g_k, wq, wkv, dq, dk, dv, tm=1024):
    T = proj.shape[0]
    H = MLA_HEADS
    scale = MLA_QK ** -0.5

    def body(cq_ref, ckv_ref, kpe_ref, cos_ref, sin_ref, gqa_ref, gkva_ref, gq_ref, gk_ref, wq_ref, wkv_ref,
             dq_ref, dk_ref, dv_ref,
             dcq_ref, dckv_ref, dkpe_ref, cqn_ref, ckvn_ref, dqp_ref, dkvp_ref,
             dgqa_ref, dgkva_ref, dgq_ref, dgk_ref):
        @pl.when(pl.program_id(0) == 0)
        def _():
            for ref in (dgqa_ref, dgkva_ref, dgq_ref, dgk_ref):
                ref[...] = jnp.zeros_like(ref)

        cosv, sinv = cos_ref[...], sin_ref[...]

        def rope_bwd(dy):
            return dy * cosv + _swap_halves(dy * sinv)

        cq = cq_ref[...]
        rcq = _rstd(cq)
        cqn = (cq * rcq * gqa_ref[...]).astype(BF16)
        cqn_ref[...] = cqn
        qp = _dot_nt(cqn, wq_ref[...])
        ckv = ckv_ref[...]
        rckv = _rstd(ckv)
        ckvn = (ckv * rckv * gkva_ref[...]).astype(BF16)
        ckvn_ref[...] = ckvn
        kvp = _dot(ckvn, wkv_ref[...])
        kpe = kpe_ref[...]
        dkpe = jnp.zeros_like(kpe)
        dgq = jnp.zeros((1, MLA_PAD), F32)
        dgk = jnp.zeros((1, MLA_PAD), F32)
        for h in range(H):
            a = qp[:, MLA_PAD * h:MLA_PAD * (h + 1)]
            dqh = dq_ref[h]
            dn = jnp.concatenate([dqh[:, 0:128], rope_bwd(dqh[:, 128:256])], axis=-1) * scale
            da, dg = _rms_bwd(dn, a, gq_ref[...], _rstd(a, MLA_QK), MLA_QK)
            dgq = dgq + jnp.sum(dg, axis=0, keepdims=True)
            dqp_ref[:, MLA_PAD * h:MLA_PAD * (h + 1)] = da.astype(BF16)

            ak = jnp.concatenate([kvp[:, MLA_PAD * h:MLA_PAD * h + 128], kpe], axis=-1)
            dkh = dk_ref[h]
            dnk = jnp.concatenate([dkh[:, 0:128], rope_bwd(dkh[:, 128:256])], axis=-1)
            dak, dg = _rms_bwd(dnk, ak, gk_ref[...], _rstd(ak, MLA_QK), MLA_QK)
            dgk = dgk + jnp.sum(dg, axis=0, keepdims=True)
            dkpe = dkpe + dak[:, 128:256]
            dkvp_ref[:, MLA_PAD * h:MLA_PAD * h + 128] = dak[:, 0:128].astype(BF16)
            dkvp_ref[:, MLA_PAD * h + 128:MLA_PAD * (h + 1)] = dv_ref[h].astype(BF16)
        dkpe_ref[...] = dkpe
        dgq_ref[...] += dgq
        dgk_ref[...] += dgk
        dcq, dg = _rms_bwd(_dot(dqp_ref[...], wq_ref[...]), cq, gqa_ref[...], rcq)
        dcq_ref[...] = dcq
        dgqa_ref[...] += jnp.sum(dg, axis=0, keepdims=True)
        dckv, dg = _rms_bwd(_dot_nt(dkvp_ref[...], wkv_ref[...]), ckv, gkva_ref[...], rckv)
        dckv_ref[...] = dckv
        dgkva_ref[...] += jnp.sum(dg, axis=0, keepdims=True)

    fix = lambda i: (0, 0)
    row = lambda i: (i, 0)
    head = lambda i: (0, i, 0)
    return pl.pallas_call(
        body, name="mla_prep_bwd", grid=(T // tm,),
        in_specs=[pl.BlockSpec((tm, 256), lambda i: (i, 6)), pl.BlockSpec((tm, 128), lambda i: (i, 14)),
                  pl.BlockSpec((tm, 128), lambda i: (i, 15)),
                  pl.BlockSpec((tm, 128), row), pl.BlockSpec((tm, 128), row),
                  pl.BlockSpec((1, 256), fix), pl.BlockSpec((1, 128), fix),
                  pl.BlockSpec((1, 256), fix), pl.BlockSpec((1, 256), fix),
                  pl.BlockSpec((H * MLA_PAD, 256), fix), pl.BlockSpec((128, H * MLA_PAD), fix),
                  pl.BlockSpec((H, tm, MLA_PAD), head), pl.BlockSpec((H, tm, MLA_PAD), head),
                  pl.BlockSpec((H, tm, MLA_V), head)],
        out_specs=[pl.BlockSpec((tm, 256), row), pl.BlockSpec((tm, 128), row), pl.BlockSpec((tm, 128), row),
                   pl.BlockSpec((tm, 256), row), pl.BlockSpec((tm, 128), row),
                   pl.BlockSpec((tm, H * MLA_PAD), row), pl.BlockSpec((tm, H * MLA_PAD), row),
                   pl.BlockSpec((1, 256), fix), pl.BlockSpec((1, 128), fix),
                   pl.BlockSpec((1, 256), fix), pl.BlockSpec((1, 256), fix)],
        out_shape=[jax.ShapeDtypeStruct((T, 256), F32), jax.ShapeDtypeStruct((T, 128), F32),
                   jax.ShapeDtypeStruct((T, 128), F32),
                   jax.ShapeDtypeStruct((T, 256), BF16), jax.ShapeDtypeStruct((T, 128), BF16),
                   jax.ShapeDtypeStruct((T, H * MLA_PAD), BF16), jax.ShapeDtypeStruct((T, H * MLA_PAD), BF16),
                   jax.ShapeDtypeStruct((1, 256), F32), jax.ShapeDtypeStruct((1, 128), F32),
                   jax.ShapeDtypeStruct((1, 256), F32), jax.ShapeDtypeStruct((1, 256), F32)],
        compiler_params=_params(1),
    )(proj, proj, proj, cos, sin, g_qa, g_kva, g_q, g_k, wq, wkv, dq, dk, dv)


def _causal_pairs(T, tq, tk, key_major):
    pairs = [(i, j) for i in range(T // tq) for j in range(T // tk) if j * tk <= i * tq + tq - 1]
    if key_major:
        pairs.sort(key=lambda p: (p[1], p[0]))
    outer = [p[1] if key_major else p[0] for p in pairs]
    first = [int(t == 0 or outer[t] != outer[t - 1]) for t in range(len(pairs))]
    last = [int(t == len(pairs) - 1 or outer[t] != outer[t + 1]) for t in range(len(pairs))]
    tab = lambda v: jnp.asarray(np.array(v, np.int32))
    return tab([p[0] for p in pairs]), tab([p[1] for p in pairs]), tab(first), tab(last)


def _causal_scores(qv, kv, row0):
    s = _dot_nt(qv, kv)
    if row0 is not None:
        row = lax.broadcasted_iota(jnp.int32, s.shape, 0) + row0
        col = lax.broadcasted_iota(jnp.int32, s.shape, 1)
        s = jnp.where(col <= row, s, NEG)
    return s


def _causal_variants(qi, ki, tq, tk, update):
    assert tk % tq == 0
    diag = qi * tq - ki * tk
    for off in range(0, tk, tq):
        pl.when(diag == off)(lambda off=off: update(off))
    pl.when(diag >= tk)(lambda: update(None))


def _visible_keys(off, row0, rows, tk):
    return tk if off is None else min(tk, off + row0 + rows)


def _mla_attn(q, k, v, ride=None, tq=1024, tk=2048, rc=256):
    H, T, _ = q.shape
    tables = _causal_pairs(T, tq, tk, key_major=False)
    n_pairs = int(tables[0].shape[0])
    r_args, r_in, r_shape, r_out, r_scratch = _ride_parts(ride)

    def body(qt, kt, ft, lt, q_ref, k_ref, v_ref, o_ref, lse_ref, m_s, acc):
        t = pl.program_id(1)
        qi, ki = qt[t], kt[t]

        @pl.when(ft[t] == 1)
        def _():
            m_s[...] = jnp.full_like(m_s, NEG)
            acc[...] = jnp.zeros_like(acc)

        def update(off):
            for c in range(tq // rc):
                rows = pl.ds(c * rc, rc)
                keys = pl.ds(0, _visible_keys(off, c * rc, rc, tk))
                s = _causal_scores(q_ref[rows, :], k_ref[keys, :], None if off is None else off + c * rc)
                m_old = m_s[rows, :]
                m_new = jnp.maximum(m_old, jnp.max(s, axis=-1, keepdims=True))
                p = jnp.exp(s - m_new).astype(BF16)
                acc[rows, :] = jnp.exp(m_old - m_new) * acc[rows, :] + _dot(p, v_ref[keys, :])
                m_s[rows, :] = m_new

        _causal_variants(qi, ki, tq, tk, update)

        @pl.when(lt[t] == 1)
        def _():
            l = jnp.max(acc[:, MLA_V:], axis=-1, keepdims=True)
            o_ref[...] = acc[:, :MLA_V] / l
            lse_ref[...] = jnp.broadcast_to(m_s[...] + jnp.log(l), lse_ref.shape)

    qrow = lambda h, t, qt, kt, ft, lt: (h, qt[t], 0)
    krow = lambda h, t, qt, kt, ft, lt: (h, kt[t], 0)
    first = lambda: (pl.program_id(0) == 0) & (pl.program_id(1) == 0)
    last = lambda: (pl.program_id(0) == H - 1) & (pl.program_id(1) == n_pairs - 1)
    outs = pl.pallas_call(
        _riding(body, 7, 2, 2, ride, first, last), name="mla_attn",
        grid_spec=pltpu.PrefetchScalarGridSpec(
            num_scalar_prefetch=4, grid=(H, n_pairs),
            in_specs=[pl.BlockSpec((None, tq, MLA_PAD), qrow), pl.BlockSpec((None, tk, MLA_PAD), krow),
                      pl.BlockSpec((None, tk, 2 * MLA_V), krow)] + r_in,
            out_specs=[pl.BlockSpec((tq, MLA_V), lambda h, t, qt, kt, ft, lt: (qt[t], h)),
                       pl.BlockSpec((None, tq, 128), qrow)] + r_out,
            scratch_shapes=[pltpu.VMEM((tq, 1), F32), pltpu.VMEM((tq, 2 * MLA_V), F32)] + r_scratch),
        out_shape=[jax.ShapeDtypeStruct((T, H * MLA_V), F32), jax.ShapeDtypeStruct((H, T, 128), F32)] + r_shape,
        compiler_params=_params(2),
    )(*tables, q, k, v, *r_args)
    return outs[:2], outs[2:]


def _mla_attn_bwd(q, k, v, o, lse, do, ride=None, tq=1024, tk=1024, rc=512, rc_diagonal=256):
    H, T, _ = q.shape
    tables = _causal_pairs(T, tq, tk, key_major=True)
    n_pairs = int(tables[0].shape[0])
    r_args, r_in, r_shape, r_out, r_scratch = _ride_parts(ride)

    def body(qt, kt, ft, lt, q_ref, k_ref, v_ref, o_ref, lse_ref, do_ref, dq_ref, dk_ref, dv_ref, dk_s, dv_s):
        t = pl.program_id(1)
        qi, ki = qt[t], kt[t]

        @pl.when(t == 0)
        def _():
            dq_ref[...] = jnp.zeros_like(dq_ref)

        @pl.when(ft[t] == 1)
        def _():
            dk_s[...] = jnp.zeros_like(dk_s)
            dv_s[...] = jnp.zeros_like(dv_s)

        def update(off):
            rows_per = rc if off is None else rc_diagonal
            for c in range(tq // rows_per):
                rows = pl.ds(c * rows_per, rows_per)
                keys = pl.ds(0, _visible_keys(off, c * rows_per, rows_per, tk))
                kk, vv = k_ref[keys, :], v_ref[keys, :]
                qv, dov = q_ref[rows, :], do_ref[rows, :]
                delta = jnp.sum(dov * o_ref[rows, :], axis=-1, keepdims=True)
                lse_v = jnp.max(lse_ref[rows, :], axis=-1, keepdims=True)
                p = jnp.exp(_causal_scores(qv, kk, None if off is None else off + c * rows_per) - lse_v)
                dob = dov.astype(BF16)
                dv_s[keys, :] += _dot_tn(p.astype(BF16), dob)
                ds = (p * (_dot_nt(dob, vv) - delta)).astype(BF16)
                dk_s[keys, :] += _dot_tn(ds, qv)
                out_rows = pl.ds(pl.multiple_of(qi * tq + c * rows_per, rows_per), rows_per)
                dq_ref[out_rows, :] += _dot(ds, kk)

        _causal_variants(qi, ki, tq, tk, update)

        @pl.when(lt[t] == 1)
        def _():
            dk_ref[...] = dk_s[...]
            dv_ref[...] = dv_s[...]

    qrow = lambda h, t, qt, kt, ft, lt: (h, qt[t], 0)
    krow = lambda h, t, qt, kt, ft, lt: (h, kt[t], 0)
    qcol = lambda h, t, qt, kt, ft, lt: (qt[t], h)
    first = lambda: (pl.program_id(0) == 0) & (pl.program_id(1) == 0)
    last = lambda: (pl.program_id(0) == H - 1) & (pl.program_id(1) == n_pairs - 1)
    outs = pl.pallas_call(
        _riding(body, 10, 3, 2, ride, first, last), name="mla_attn_bwd",
        grid_spec=pltpu.PrefetchScalarGridSpec(
            num_scalar_prefetch=4, grid=(H, n_pairs),
            in_specs=[pl.BlockSpec((None, tq, MLA_PAD), qrow), pl.BlockSpec((None, tk, MLA_PAD), krow),
                      pl.BlockSpec((None, tk, MLA_V), krow), pl.BlockSpec((tq, MLA_V), qcol),
                      pl.BlockSpec((None, tq, 128), qrow), pl.BlockSpec((tq, MLA_V), qcol)] + r_in,
            out_specs=[pl.BlockSpec((None, T, MLA_PAD), lambda h, t, qt, kt, ft, lt: (h, 0, 0)),
                       pl.BlockSpec((None, tk, MLA_PAD), krow), pl.BlockSpec((None, tk, MLA_V), krow)] + r_out,
            scratch_shapes=[pltpu.VMEM((tk, MLA_PAD), F32), pltpu.VMEM((tk, MLA_V), F32)] + r_scratch),
        out_shape=[jax.ShapeDtypeStruct((H, T, MLA_PAD), F32), jax.ShapeDtypeStruct((H, T, MLA_PAD), F32),
                   jax.ShapeDtypeStruct((H, T, MLA_V), F32)] + r_shape,
        compiler_params=_params(2),
    )(*tables, q, k, v, o, lse, do, *r_args)
    return outs[:3], outs[3:]


def _pair_gain(g):
    return jnp.tile(g.reshape(1, DIL_HD), (1, 2))


def _pad_gain(g):
    return jnp.pad(g.reshape(1, MLA_QK), ((0, 0), (0, MLA_PAD - MLA_QK)))


def _local_step(x, target, s, comm):
    T = x.shape[0]
    w = comm.w
    gq, gk = _pair_gain(s["dil_q_norm"]) * DIL_HD ** -0.5, _pair_gain(s["dil_k_norm"])
    g_q, g_k = _pad_gain(s["mla_q_norm"]), _pad_gain(s["mla_k_norm"])
    cos, sin = _rope_tables(T)
    onehots = _bucket_onehots()
    biases = _bias_tables(s["rel_bias"], onehots)

    (x1, h1, gate1, up1), got = _ffn_fwd(x, s["ffn1_norm"], w["ffn1_w_gate"], w["ffn1_w_up"], w["ffn1_w_down"],
                                         ride=comm.gather(_GROUPS["attn"]))
    comm.weights_landed(_GROUPS["attn"], got)
    hm, proj, qh, kh = _in_proj(x1, s["mix_norm"], w["w_in"], gq, gk)
    dil = None
    for (_, d), bias in zip(DIL_BRANCHES, biases):
        dil = _dil_fwd(qh, kh, proj, bias, d, dil)
    o_dil, lse_dil = dil
    q, k, v = _mla_prep(proj, cos, sin, s["mla_q_a_norm"], s["mla_kv_a_norm"], g_q, g_k, w["mla_w_q_b"], w["mla_w_kv_b"])
    (o_mla, lse_mla), got = _mla_attn(q, k, v, ride=comm.gather(_GROUPS["ffn2"]))
    comm.weights_landed(_GROUPS["ffn2"], got)
    x2, oc = _out_proj(x1, o_dil, o_mla, s["out_norm_dil"], s["out_norm_mla"], w["w_out"])
    (dy, h2, gate2, up2, loss), _ = _ffn_fwd(x2, s["ffn2_norm"], w["ffn2_w_gate"], w["ffn2_w_up"], w["ffn2_w_down"],
                                             target=target)

    gw, gs = {}, {}

    def ffn_grads(name, dy_in, x_in, h, gate, up, early=None):
        (dx, a, dg, du, dyh, dgain), _ = _ffn_bwd(dy_in, x_in, s[name + "_norm"], gate, up,
                                                  w[name + "_w_gate"], w[name + "_w_up"], w[name + "_w_down"])
        gs[name + "_norm"] = dgain
        down, gate_n, up_n = (name + "_w_down",), (name + "_w_gate",), (name + "_w_up",)
        ride = lambda names: comm.scatter(names, gw) if early is not None else None
        gw[down[0]], landed = _matmul_tn(a, dyh, 1408, 1024, ride=ride(early))
        comm.grads_landed(early or (), landed)
        gw[gate_n[0]], landed = _matmul_tn(dg, h, 1408, 1024, ride=ride(down))
        comm.grads_landed(down, landed)
        gw[up_n[0]], landed = _matmul_tn(du, h, 1408, 1024, ride=ride(gate_n))
        comm.grads_landed(gate_n, landed)
        return dx

    dx2 = ffn_grads("ffn2", dy, x2, h2, gate2, up2)
    gw["w_out"], _ = _matmul_tn(oc, dx2, 1024, 1024)
    do_dil, do_mla, gs["out_norm_dil"], gs["out_norm_mla"] = _out_proj_bwd(
        dx2, o_dil, o_mla, s["out_norm_dil"], s["out_norm_mla"], w["w_out"])

    (dq, dk, dv), got = _mla_attn_bwd(q, k, v, o_mla, lse_mla, do_mla, ride=comm.scatter(_GROUPS["ffn2"], gw))
    comm.grads_landed(_GROUPS["ffn2"], got)
    (dcq, dckv, dkpe, cqn, ckvn, dqp, dkvp, gs["mla_q_a_norm"], gs["mla_kv_a_norm"], dg_q, dg_k) = _mla_prep_bwd(
        proj, cos, sin, s["mla_q_a_norm"], s["mla_kv_a_norm"], g_q, g_k, w["mla_w_q_b"], w["mla_w_kv_b"], dq, dk, dv)
    gs["mla_q_norm"], gs["mla_k_norm"] = dg_q[:, :MLA_QK], dg_k[:, :MLA_QK]
    gw["mla_w_q_b"], _ = _matmul_tn(dqp, cqn, 1024, 256)
    gw["mla_w_kv_b"], _ = _matmul_tn(ckvn, dkvp, 128, 1024)

    dqkv, dbs = None, []
    for (_, d), bias in reversed(list(zip(DIL_BRANCHES, biases))):
        dqkv, db = _dil_bwd(qh, kh, proj, o_dil, lse_dil, do_dil, bias, d, dqkv)
        dbs.insert(0, db)
    dqkv = [dqkv]
    gs["rel_bias"] = _bias_grad(dbs, onehots)

    ready = tuple(n for n in _GROUPS["attn"] if n != "w_in")
    (dx1, dproj, gs["mix_norm"], dgq, dgk), got = _in_proj_bwd(dx2, x1, s["mix_norm"], w["w_in"], proj, gq, gk,
                                                               dqkv, dcq, dckv, dkpe, ride=comm.scatter(ready, gw))
    comm.grads_landed(ready, got)
    gs["dil_q_norm"] = (dgq[:, :DIL_HD] + dgq[:, DIL_HD:]) * DIL_HD ** -0.5
    gs["dil_k_norm"] = dgk[:, :DIL_HD] + dgk[:, DIL_HD:]
    gw["w_in"], _ = _matmul_tn(dproj, hm, 1024, 1024)
    grad_x = ffn_grads("ffn1", dx1, x, h1, gate1, up1, early=("w_in",))
    return loss, grad_x, gw, gs


def _position():
    x, y, c = lax.axis_index("x"), lax.axis_index("y"), lax.axis_index("c")
    return x, y, c, 4 * x + 2 * y + c


def _peer(x, y, c, k):
    px = 1 - x if k & 4 else x
    py = 1 - y if k & 2 else y
    pc = 1 - c if k & 1 else c
    return (px, py, pc), 4 * px + 2 * py + pc


class _Ride:
    def __init__(self, arrays, scatter):
        self.arrays, self.scatter = list(arrays), list(scatter)
        self.n = n = len(self.arrays)
        self.specs = [pl.BlockSpec(memory_space=pl.ANY)] * n
        self.out_shape = [jax.ShapeDtypeStruct(a.shape if sc else (N_DEV,) + a.shape, a.dtype)
                          for a, sc in zip(self.arrays, self.scatter)]
        self.scratch = [pltpu.SemaphoreType.DMA((n, N_DEV - 1)), pltpu.SemaphoreType.DMA((n, N_DEV - 1)),
                        pltpu.SemaphoreType.DMA((n,))]

    def _copies(self, ins, outs, sems):
        send_sems, recv_sems, local_sems = sems
        x, y, c, me = _position()
        copies = []
        for a in range(self.n):
            src = ins[a].at[me] if self.scatter[a] else ins[a]
            copies.append(pltpu.make_async_copy(src, outs[a].at[me], local_sems.at[a]))
        for k in range(1, N_DEV):
            peer, peer_idx = _peer(x, y, c, k)
            for a in range(self.n):
                src = ins[a].at[peer_idx] if self.scatter[a] else ins[a]
                copies.append(pltpu.make_async_remote_copy(
                    src_ref=src, dst_ref=outs[a].at[me], send_sem=send_sems.at[a, k - 1], recv_sem=recv_sems.at[a, k - 1],
                    device_id=peer, device_id_type=pl.DeviceIdType.MESH))
        return copies

    def start(self, ins, outs, sems):
        for cp in self._copies(ins, outs, sems):
            cp.start()

    def wait(self, ins, outs, sems):
        for cp in self._copies(ins, outs, sems):
            cp.wait()


def _ride_parts(ride):
    if ride is None:
        return [], [], [], [], []
    return ride.arrays, ride.specs, ride.out_shape, ride.specs, ride.scratch


def _riding(body, n_in, n_out, n_scratch, ride, first, last):
    if ride is None:
        return body
    n = ride.n
    i1, i2 = n_in + n, n_in + n + n_out
    i3, i4 = i2 + n, i2 + n + n_scratch

    def wrapped(*refs):
        ins, outs, sems = refs[n_in:i1], refs[i2:i3], refs[i4:]

        @pl.when(first())
        def _():
            ride.start(ins, outs, sems)

        body(*refs[:n_in], *refs[i1:i2], *refs[i3:i4])

        @pl.when(last())
        def _():
            ride.wait(ins, outs, sems)

    return wrapped


def _gather_two_level(arrays, name):
    n = len(arrays)
    out_shape = [jax.ShapeDtypeStruct((N_DEV,) + a.shape, a.dtype) for a in arrays]

    def body(*refs):
        ins, outs = refs[:n], refs[n:2 * n]
        send_sems, recv_sems, local_sems = refs[2 * n:]
        x, y, c, me = _position()
        sibling = (x, y, 1 - c)
        chips = [(1 - x, y), (x, 1 - y), (1 - x, 1 - y)]
        block = lambda px, py, pc: 4 * px + 2 * py + pc

        def copy(a, k, blk, to, src=None):
            dst = outs[a].at[blk]
            return pltpu.make_async_remote_copy(
                src_ref=dst if src is None else src, dst_ref=dst, send_sem=send_sems.at[a, k], recv_sem=recv_sems.at[a, k],
                device_id=to, device_id_type=pl.DeviceIdType.MESH)

        local = [pltpu.make_async_copy(ins[a], outs[a].at[me], local_sems.at[a]) for a in range(n)]
        first = []
        for a in range(n):
            first.append(copy(a, 0, me, sibling, src=ins[a]))
            first += [copy(a, 1 + j, me, (*chip, c), src=ins[a]) for j, chip in enumerate(chips)]
        for cp in local + first:
            cp.start()
        passed = []
        for j, chip in enumerate(chips):
            for a in range(n):
                copy(a, 1 + j, block(*chip, c), sibling).wait_recv()
                passed.append(copy(a, 4 + j, block(*chip, c), sibling))
                passed[-1].start()
        for a in range(n):
            copy(a, 0, block(x, y, 1 - c), sibling).wait_recv()
            for j, chip in enumerate(chips):
                copy(a, 4 + j, block(*chip, 1 - c), sibling).wait_recv()
        for cp in first + passed:
            cp.wait_send()
        for cp in local:
            cp.wait()

    any_spec = [pl.BlockSpec(memory_space=pl.ANY)] * n
    return pl.pallas_call(
        body, name=name, in_specs=any_spec, out_specs=any_spec, out_shape=out_shape,
        scratch_shapes=[pltpu.SemaphoreType.DMA((n, N_DEV - 1)), pltpu.SemaphoreType.DMA((n, N_DEV - 1)),
                        pltpu.SemaphoreType.DMA((n,))],
    )(*arrays)


def _adamw_math(wv, g, m, v):
    m = ADAM_B1 * m + (1.0 - ADAM_B1) * g
    v = ADAM_B2 * v + (1.0 - ADAM_B2) * (g * g)
    m_hat = m / (1.0 - ADAM_B1 ** ADAM_STEP)
    v_hat = v / (1.0 - ADAM_B2 ** ADAM_STEP)
    delta = -ADAM_LR * (m_hat / (jnp.sqrt(v_hat) + ADAM_EPS) + ADAM_WD * wv)
    return delta, m, v


def _adamw(items, ride=None, max_rows=256):
    K = len(items)
    tiles, spans, start = [], [], 0
    for _, wv, _, _ in items:
        R = wv.shape[1]
        tr = max([t for t in range(16, max_rows + 1, 16) if R % t == 0] or [R])
        tiles.append(tr)
        spans.append((start, R // tr))
        start += R // tr
    total = start
    r_args, r_in, r_shape, r_out, r_scratch = _ride_parts(ride)

    def body(*refs):
        i = pl.program_id(0)
        for k, (first_step, n_steps) in enumerate(spans):
            def update(k=k):
                p_ref, w_ref, m_ref, v_ref = refs[4 * k:4 * k + 4]
                g_ref, d_ref, mo_ref, vo_ref = refs[4 * K + 4 * k:4 * K + 4 * k + 4]
                g = p_ref[0].astype(F32)
                for j in range(1, N_DEV):
                    g = g + p_ref[j].astype(F32)
                d, mn, vn = _adamw_math(w_ref[0], g, m_ref[0], v_ref[0])
                g_ref[0] = g
                d_ref[0] = d
                mo_ref[0] = mn
                vo_ref[0] = vn

            pl.when((i >= first_step) & (i < first_step + n_steps))(update)

    in_specs, out_specs, out_shape, args = [], [], [], []
    for (parts, wv, m, v), tr, (first_step, n_steps) in zip(items, tiles, spans):
        C = wv.shape[2]
        tile = lambda i, s=first_step, n=n_steps: (0, jnp.clip(i - s, 0, n - 1), 0)
        blk = pl.BlockSpec((1, tr, C), tile)
        in_specs += [pl.BlockSpec((N_DEV, tr, C), tile), blk, blk, blk]
        out_specs += [blk] * 4
        out_shape += [jax.ShapeDtypeStruct(wv.shape, F32)] * 4
        args += [parts, wv, m, v]
    outs = pl.pallas_call(
        _riding(body, 4 * K, 4 * K, 0, ride, lambda: pl.program_id(0) == 0, lambda: pl.program_id(0) == total - 1),
        name="adamw", grid=(total,),
        in_specs=in_specs + r_in, out_specs=out_specs + r_out, out_shape=out_shape + r_shape,
        scratch_shapes=r_scratch, compiler_params=_params(1),
    )(*args, *r_args)
    return [outs[4 * k:4 * k + 4] for k in range(K)], outs[4 * K:]


_TRANSPOSED = ("ffn1_w_gate", "ffn1_w_up", "ffn2_w_gate", "ffn2_w_up", "w_in", "mla_w_q_b")
_GROUPS = {"ffn1": ("ffn1_w_gate", "ffn1_w_up", "ffn1_w_down"),
           "ffn2": ("ffn2_w_gate", "ffn2_w_up", "ffn2_w_down"),
           "attn": ("w_in", "mla_w_q_b", "mla_w_kv_b", "w_out")}
_SMALL = ("ffn1_norm", "mix_norm", "ffn2_norm", "out_norm_dil", "out_norm_mla", "mla_q_a_norm", "rel_bias",
          "mla_q_norm", "mla_k_norm", "mla_kv_a_norm", "dil_q_norm", "dil_k_norm")
_SMALL_ROWS = 48


def _cols_to_full(g):
    return g.transpose(1, 0, 2).reshape(g.shape[1], N_DEV * g.shape[2])


def _full_to_cols(f):
    return f.reshape(f.shape[0], N_DEV, f.shape[1] // N_DEV).transpose(1, 0, 2)


def _shard_view(name, a):
    return jnp.swapaxes(a, 1, 2) if name in _TRANSPOSED else a


def _to_full(name, g):
    if name == "mla_w_kv_b":
        return _cols_to_full(g)
    f = g.reshape(-1, g.shape[-1])
    if name == "w_in":
        f = jnp.pad(f, ((0, PROJ_PAD - PROJ_COLS), (0, 0)))
    if name == "mla_w_q_b":
        f = jnp.pad(f.reshape(MLA_HEADS, MLA_QK, -1), ((0, 0), (0, MLA_PAD - MLA_QK), (0, 0)))
        f = f.reshape(MLA_HEADS * MLA_PAD, -1)
    return f


def _to_parts(name, f):
    if name == "mla_w_kv_b":
        return _full_to_cols(f).astype(BF16)
    if name == "w_in":
        f = f[:PROJ_COLS]
    if name == "mla_w_q_b":
        f = f.reshape(MLA_HEADS, MLA_PAD, -1)[:, :MLA_QK].reshape(MLA_HEADS * MLA_QK, -1)
    return f.reshape(N_DEV, -1, f.shape[-1]).astype(BF16)


class _Comm:
    def __init__(self, shards):
        self.shards, self.w, self.recv = shards, {}, {}

    def gather(self, names):
        return _Ride([self.shards[n] for n in names], [False] * len(names))

    def scatter(self, names, grads):
        return _Ride([_to_parts(n, grads[n]) for n in names], [True] * len(names))

    def weights_landed(self, names, got):
        self.w.update({n: _to_full(n, g) for n, g in zip(names, got)})

    def grads_landed(self, names, got):
        self.recv.update(zip(names, got))


def _pack_small(parts, extra):
    flat = jnp.concatenate([parts[n].reshape(-1) for n in _SMALL] + [extra.reshape(-1)])
    return jnp.pad(flat, (0, _SMALL_ROWS * 128 - flat.shape[0])).reshape(_SMALL_ROWS, 128)


def _unpack_small(packed, shapes):
    flat, out, off = packed.reshape(-1), {}, 0
    for n in _SMALL:
        size = math.prod(shapes[n])
        out[n] = flat[off:off + size].reshape(shapes[n])
        off += size
    return out, flat[off]


_NAMES = ("ffn1_norm", "ffn1_w_gate", "ffn1_w_up", "ffn1_w_down", "mix_norm", "w_in", "dil_q_norm", "dil_k_norm",
          "rel_bias", "mla_q_a_norm", "mla_w_q_b", "mla_kv_a_norm", "mla_w_kv_b", "mla_q_norm", "mla_k_norm",
          "out_norm_dil", "out_norm_mla", "w_out", "ffn2_norm", "ffn2_w_gate", "ffn2_w_up", "ffn2_w_down")


def kernel(x, ffn1_norm, ffn1_w_gate, ffn1_w_up, ffn1_w_down, mix_norm, w_in, dil_q_norm, dil_k_norm, rel_bias, mla_q_a_norm, mla_w_q_b, mla_kv_a_norm, mla_w_kv_b, mla_q_norm, mla_k_norm, out_norm_dil, out_norm_mla, w_out, ffn2_norm, ffn2_w_gate, ffn2_w_up, ffn2_w_down, loss_target, m_ffn1_norm, m_ffn1_w_gate, m_ffn1_w_up, m_ffn1_w_down, m_mix_norm, m_w_in, m_dil_q_norm, m_dil_k_norm, m_rel_bias, m_mla_q_a_norm, m_mla_w_q_b, m_mla_kv_a_norm, m_mla_w_kv_b, m_mla_q_norm, m_mla_k_norm, m_out_norm_dil, m_out_norm_mla, m_w_out, m_ffn2_norm, m_ffn2_w_gate, m_ffn2_w_up, m_ffn2_w_down, v_ffn1_norm, v_ffn1_w_gate, v_ffn1_w_up, v_ffn1_w_down, v_mix_norm, v_w_in, v_dil_q_norm, v_dil_k_norm, v_rel_bias, v_mla_q_a_norm, v_mla_w_q_b, v_mla_kv_a_norm, v_mla_w_kv_b, v_mla_q_norm, v_mla_k_norm, v_out_norm_dil, v_out_norm_mla, v_w_out, v_ffn2_norm, v_ffn2_w_gate, v_ffn2_w_up, v_ffn2_w_down):
    args = locals()
    wts = {n: args[n] for n in _NAMES}
    mom = {n: args["m_" + n] for n in _NAMES}
    var = {n: args["v_" + n] for n in _NAMES}

    matrices = [n for group in _GROUPS.values() for n in group]
    comm = _Comm({n: _shard_view(n, wts[n])[0].astype(BF16) for n in matrices})
    comm.weights_landed(_GROUPS["ffn1"], _gather_two_level(comm.gather(_GROUPS["ffn1"]).arrays, "gather_first"))
    small = {n: wts[n].reshape(1, -1) if n != "rel_bias" else wts[n] for n in _SMALL}

    loss, grad_x, gw, gs = _local_step(x[0], loss_target[0], small, comm)

    item = lambda n: (comm.recv[n],) + tuple(_shard_view(n, a[n]) for a in (wts, mom, var))
    landed = [n for n in matrices if n != "ffn1_w_up"]
    last = comm.scatter(("ffn1_w_up",), gw)
    updates, got = _adamw([item(n) for n in landed], max_rows=32,
                          ride=_Ride(last.arrays + [_pack_small(gs, loss[0, 0])], last.scatter + [False]))
    comm.grads_landed(("ffn1_w_up",), got[:-1])

    zero = jnp.zeros((), F32)
    small_item = (got[-1],) + tuple(_pack_small(a, zero)[None] for a in (wts, mom, var))
    (up_update, packed), _ = _adamw([item("ffn1_w_up"), small_item])
    res = {n: [_shard_view(n, r) for r in u] for n, u in zip(landed + ["ffn1_w_up"], updates + [up_update])}
    shapes = {n: wts[n].shape for n in _SMALL}
    loss_total = None
    for slot, q in enumerate(packed):
        vals, extra = _unpack_small(q, shapes)
        if slot == 0:
            loss_total = extra
        for n in _SMALL:
            res.setdefault(n, [None] * 4)[slot] = vals[n]
    outs = [loss_total, grad_x[None]]
    for slot in range(4):
        outs += [res[n][slot].reshape(wts[n].shape) for n in _NAMES]
    return tuple(outs)
```

```python
import math

import numpy as np
import jax
import jax.numpy as jnp
from jax import lax
from jax.experimental import pallas as pl
from jax.experimental.pallas import tpu as pltpu

F32, BF16 = jnp.float32, jnp.bfloat16
EPS = 1e-6
NEG = -1e30
N_DEV = 8

DIL_HEADS, DIL_HD = 8, 64
DIL_WIDTH = DIL_HEADS * DIL_HD
DIL_BRANCHES = ((128, 1), (512, 4), (2048, 16))
DIL_BLOCK = 128
MLA_HEADS, MLA_NOPE, MLA_ROPE, MLA_V = 4, 128, 64, 128
MLA_QK = MLA_NOPE + MLA_ROPE
MLA_PAD = 256
ROPE_BASE = 10000.0
REL_BUCKETS, REL_MAX_DIST = 32, 2048
MLA_Q_RANK, MLA_KV_RANK = 256, 128
PROJ_COLS, PROJ_PAD = 1984, 2048
CQ_COL = 3 * DIL_WIDTH
CKV_COL, KPE_COL = CQ_COL + MLA_Q_RANK, CQ_COL + MLA_Q_RANK + MLA_KV_RANK
FFN_RESID = 0.5
ADAM_LR, ADAM_B1, ADAM_B2, ADAM_EPS, ADAM_WD, ADAM_STEP = 0.001, 0.9, 0.999, 1e-08, 0.01, 10
VMEM_LIMIT = 62 * 1024 * 1024

_NT = (((1,), (1,)), ((), ()))
_TN = (((0,), (0,)), ((), ()))


def _dot(a, b):
    return jnp.dot(a, b, preferred_element_type=F32)


def _dot_nt(a, b):
    return lax.dot_general(a, b, _NT, preferred_element_type=F32)


def _dot_tn(a, b):
    return lax.dot_general(a, b, _TN, preferred_element_type=F32)


def _params(n_axes):
    return pltpu.CompilerParams(dimension_semantics=("arbitrary",) * n_axes, vmem_limit_bytes=VMEM_LIMIT)


def _rstd(x, n=None):
    n = x.shape[-1] if n is None else n
    return lax.rsqrt(jnp.sum(x * x, axis=-1, keepdims=True) / n + EPS)


def _rms_bwd(dy, x, g, r, n=None):
    n = x.shape[-1] if n is None else n
    u = dy * g
    dx = r * u - x * (r * r * r) * (jnp.sum(u * x, axis=-1, keepdims=True) / n)
    return dx, dy * x * r


def _sigmoid(x):
    return 1.0 / (1.0 + jnp.exp(-x))


def _split3(x):
    parts = []
    for _ in range(3):
        xb = x.astype(BF16)
        parts.append(xb)
        x = x - xb.astype(F32)
    return parts


def _ffn_fwd(x, gain, wg, wu, wd, ride=None, target=None, tm=512, tf=2816):
    T, D = x.shape
    F = wg.shape[0]
    ni, nj = T // tm, F // tf
    with_loss = target is not None
    r_args, r_in, r_shape, r_out, r_scratch = _ride_parts(ride)

    def body(*refs):
        x_ref, g_ref, wg_ref, wu_ref, wd_ref = refs[:5]
        t_ref = refs[5] if with_loss else None
        xo_ref, h_ref, gate_ref, up_ref = refs[5 + with_loss:9 + with_loss]
        loss_ref = refs[-2] if with_loss else None
        acc = refs[-1]
        i, j = pl.program_id(0), pl.program_id(1)

        @pl.when(j == 0)
        def _():
            xv = x_ref[...]
            h_ref[...] = (xv * _rstd(xv) * g_ref[...]).astype(BF16)
            acc[...] = jnp.zeros_like(acc)

        h = h_ref[...]
        g = _dot_nt(h, wg_ref[...])
        u = _dot_nt(h, wu_ref[...])
        gate_ref[...] = g.astype(BF16)
        up_ref[...] = u.astype(BF16)
        a = (g * _sigmoid(g) * u).astype(BF16)
        acc[...] += _dot(a, wd_ref[...])

        @pl.when(j == nj - 1)
        def _():
            y = x_ref[...] + FFN_RESID * acc[...]
            if with_loss:
                @pl.when(i == 0)
                def _():
                    loss_ref[...] = jnp.zeros_like(loss_ref)

                e = y - t_ref[...]
                xo_ref[...] = e * (1.0 / D)
                loss_ref[...] += (0.5 / D) * jnp.sum(e * e)
            else:
                xo_ref[...] = y

    row = lambda i, j: (i, 0)
    tile = lambda i, j: (i, j)
    n_in, n_out = 5 + with_loss, 4 + with_loss
    first = lambda: (pl.program_id(0) == 0) & (pl.program_id(1) == 0)
    last = lambda: (pl.program_id(0) == ni - 1) & (pl.program_id(1) == nj - 1)
    outs = pl.pallas_call(
        _riding(body, n_in, n_out, 1, ride, first, last), name="ffn_fwd", grid=(ni, nj),
        in_specs=[pl.BlockSpec((tm, D), row), pl.BlockSpec((1, D), lambda i, j: (0, 0)),
                  pl.BlockSpec((tf, D), lambda i, j: (j, 0)), pl.BlockSpec((tf, D), lambda i, j: (j, 0)),
                  pl.BlockSpec((tf, D), lambda i, j: (j, 0))] + [pl.BlockSpec((tm, D), row)] * with_loss + r_in,
        out_specs=[pl.BlockSpec((tm, D), row), pl.BlockSpec((tm, D), row), pl.BlockSpec((tm, tf), tile),
                   pl.BlockSpec((tm, tf), tile)] + [pl.BlockSpec((1, 128), lambda i, j: (0, 0))] * with_loss + r_out,
        out_shape=[jax.ShapeDtypeStruct((T, D), F32), jax.ShapeDtypeStruct((T, D), BF16),
                   jax.ShapeDtypeStruct((T, F), BF16), jax.ShapeDtypeStruct((T, F), BF16)]
        + [jax.ShapeDtypeStruct((1, 128), F32)] * with_loss + r_shape,
        scratch_shapes=[pltpu.VMEM((tm, D), F32)] + r_scratch,
        compiler_params=_params(2),
    )(x, gain, wg, wu, wd, *([target] if with_loss else []), *r_args)
    return outs[:n_out], outs[n_out:]


def _ffn_bwd(dy, x, gain, gate, up, wg, wu, wd, tm=256, tf=2816):
    T, D = x.shape
    F = wg.shape[0]
    ni, nj = T // tm, F // tf

    def body(dy_ref, x_ref, g_ref, gate_ref, up_ref, wg_ref, wu_ref, wd_ref,
             dx_ref, a_ref, dg_ref, du_ref, dyh_ref, dgain_ref, acc):
        i, j = pl.program_id(0), pl.program_id(1)

        @pl.when((i == 0) & (j == 0))
        def _():
            dgain_ref[...] = jnp.zeros_like(dgain_ref)

        @pl.when(j == 0)
        def _():
            dyh_ref[...] = (FFN_RESID * dy_ref[...]).astype(BF16)
            acc[...] = jnp.zeros_like(acc)

        da = _dot_nt(dyh_ref[...], wd_ref[...])
        g = gate_ref[...].astype(F32)
        u = up_ref[...].astype(F32)
        sig = _sigmoid(g)
        s = g * sig
        a_ref[...] = (s * u).astype(BF16)
        dg = (da * u * (sig * (1.0 + g * (1.0 - sig)))).astype(BF16)
        du = (da * s).astype(BF16)
        dg_ref[...] = dg
        du_ref[...] = du
        acc[...] += _dot(dg, wg_ref[...]) + _dot(du, wu_ref[...])

        @pl.when(j == nj - 1)
        def _():
            xv = x_ref[...]
            dxn, dgc = _rms_bwd(acc[...], xv, g_ref[...], _rstd(xv))
            dx_ref[...] = dy_ref[...] + dxn
            dgain_ref[...] += jnp.sum(dgc, axis=0, keepdims=True)

    return pl.pallas_call(
        body, name="ffn_bwd", grid=(ni, nj),
        in_specs=[pl.BlockSpec((tm, D), lambda i, j: (i, 0)), pl.BlockSpec((tm, D), lambda i, j: (i, 0)),
                  pl.BlockSpec((1, D), lambda i, j: (0, 0)),
                  pl.BlockSpec((tm, tf), lambda i, j: (i, j)), pl.BlockSpec((tm, tf), lambda i, j: (i, j)),
                  pl.BlockSpec((tf, D), lambda i, j: (j, 0)), pl.BlockSpec((tf, D), lambda i, j: (j, 0)),
                  pl.BlockSpec((tf, D), lambda i, j: (j, 0))],
        out_specs=[pl.BlockSpec((tm, D), lambda i, j: (i, 0)),
                   pl.BlockSpec((tm, tf), lambda i, j: (i, j)), pl.BlockSpec((tm, tf), lambda i, j: (i, j)),
                   pl.BlockSpec((tm, tf), lambda i, j: (i, j)),
                   pl.BlockSpec((tm, D), lambda i, j: (i, 0)), pl.BlockSpec((1, D), lambda i, j: (0, 0))],
        out_shape=[jax.ShapeDtypeStruct((T, D), F32), jax.ShapeDtypeStruct((T, F), BF16),
                   jax.ShapeDtypeStruct((T, F), BF16), jax.ShapeDtypeStruct((T, F), BF16),
                   jax.ShapeDtypeStruct((T, D), BF16), jax.ShapeDtypeStruct((1, D), F32)],
        scratch_shapes=[pltpu.VMEM((tm, D), F32)],
        compiler_params=_params(2),
    )(dy, x, gain, gate, up, wg, wu, wd)


def _matmul_tn(a, b, tk, tn, ride=None, tt=2048):
    T, K = a.shape
    N = b.shape[1]
    tk, tn = min(tk, K), min(tn, N)
    grid = (K // tk, N // tn, T // tt)
    r_args, r_in, r_shape, r_out, r_scratch = _ride_parts(ride)

    def body(a_ref, b_ref, o_ref, acc):
        t = pl.program_id(2)

        @pl.when(t == 0)
        def _():
            acc[...] = jnp.zeros_like(acc)

        acc[...] += _dot_tn(a_ref[...].astype(BF16), b_ref[...].astype(BF16))

        @pl.when(t == grid[2] - 1)
        def _():
            o_ref[...] = acc[...].astype(BF16)

    first = lambda: (pl.program_id(0) == 0) & (pl.program_id(1) == 0) & (pl.program_id(2) == 0)
    last = lambda: ((pl.program_id(0) == grid[0] - 1) & (pl.program_id(1) == grid[1] - 1)
                    & (pl.program_id(2) == grid[2] - 1))
    outs = pl.pallas_call(
        _riding(body, 2, 1, 1, ride, first, last), name="matmul_tn", grid=grid,
        in_specs=[pl.BlockSpec((tt, tk), lambda k, n, t: (t, k)), pl.BlockSpec((tt, tn), lambda k, n, t: (t, n))] + r_in,
        out_specs=[pl.BlockSpec((tk, tn), lambda k, n, t: (k, n))] + r_out,
        out_shape=[jax.ShapeDtypeStruct((K, N), BF16)] + r_shape,
        scratch_shapes=[pltpu.VMEM((tk, tn), F32)] + r_scratch,
        compiler_params=_params(3),
    )(a, b, *r_args)
    return outs[0], outs[1:]


def _in_proj(x, gain, w, gq, gk, tm=1024):
    T, D = x.shape
    N = w.shape[0]
    W = DIL_WIDTH

    def body(x_ref, g_ref, w_ref, gq_ref, gk_ref, h_ref, p_ref, qh_ref, kh_ref):
        xv = x_ref[...]
        h = (xv * _rstd(xv) * g_ref[...]).astype(BF16)
        h_ref[...] = h
        p_ref[...] = _dot_nt(h, w_ref[...])
        lo = lax.broadcasted_iota(jnp.int32, (tm, 128), 1) < DIL_HD
        for hp in range(DIL_HEADS // 2):
            q = p_ref[:, 128 * hp:128 * (hp + 1)]
            k = p_ref[:, W + 128 * hp:W + 128 * (hp + 1)]
            qh_ref[:, 128 * hp:128 * (hp + 1)] = (q * _pair_rstd(q, lo) * gq_ref[...]).astype(BF16).astype(F32)
            kh_ref[:, 128 * hp:128 * (hp + 1)] = (k * _pair_rstd(k, lo) * gk_ref[...]).astype(BF16).astype(F32)

    row = lambda i: (i, 0)
    fix = lambda i: (0, 0)
    return pl.pallas_call(
        body, name="in_proj", grid=(T // tm,),
        in_specs=[pl.BlockSpec((tm, D), row), pl.BlockSpec((1, D), fix), pl.BlockSpec((N, D), fix),
                  pl.BlockSpec((1, 128), fix), pl.BlockSpec((1, 128), fix)],
        out_specs=[pl.BlockSpec((tm, D), row), pl.BlockSpec((tm, N), row), pl.BlockSpec((tm, W), row),
                   pl.BlockSpec((tm, W), row)],
        out_shape=[jax.ShapeDtypeStruct((T, D), BF16), jax.ShapeDtypeStruct((T, N), F32),
                   jax.ShapeDtypeStruct((T, W), F32), jax.ShapeDtypeStruct((T, W), F32)],
        compiler_params=_params(1),
    )(x, gain, w, gq, gk)


def _in_proj_bwd(dx_up, x, gain, w, proj, gq, gk, dqkv, dcq, dckv, dkpe, ride=None, tm=512):
    T, D = x.shape
    N = w.shape[0]
    W = DIL_WIDTH
    nb = len(dqkv)

    def body(*refs):
        dxu_ref, x_ref, g_ref, w_ref, q_ref, k_ref, gq_ref, gk_ref = refs[:8]
        dil_refs = refs[8:8 + 3 * nb]
        dcq_ref, dckv_ref, dkpe_ref, dx_ref, dp_ref, dgain_ref, dgq_ref, dgk_ref = refs[8 + 3 * nb:]

        @pl.when(pl.program_id(0) == 0)
        def _():
            for ref in (dgain_ref, dgq_ref, dgk_ref):
                ref[...] = jnp.zeros_like(ref)

        lo = lax.broadcasted_iota(jnp.int32, (tm, 128), 1) < DIL_HD
        norms = ((q_ref, gq_ref, dgq_ref), (k_ref, gk_ref, dgk_ref))
        for part in range(3):
            acc = dil_refs[part][...]
            for b in range(1, nb):
                acc = acc + dil_refs[3 * b + part][...]
            if part == 2:
                dp_ref[:, 2 * W:3 * W] = acc.astype(BF16)
                continue
            raw_ref, gn_ref, dgn_ref = norms[part]
            for hp in range(DIL_HEADS // 2):
                raw = raw_ref[:, 128 * hp:128 * (hp + 1)]
                d_raw, dgn = _pair_rms_bwd(acc[:, 128 * hp:128 * (hp + 1)], raw, _pair_rstd(raw, lo), gn_ref[...], lo)
                dp_ref[:, part * W + 128 * hp:part * W + 128 * (hp + 1)] = d_raw.astype(BF16)
                dgn_ref[...] += dgn
        dp_ref[:, 3 * W:3 * W + 256] = dcq_ref[...].astype(BF16)
        dp_ref[:, 3 * W + 256:3 * W + 384] = dckv_ref[...].astype(BF16)
        dp_ref[:, 3 * W + 384:N] = dkpe_ref[...].astype(BF16)
        dh = _dot(dp_ref[...], w_ref[...])
        xv = x_ref[...]
        dxn, dgc = _rms_bwd(dh, xv, g_ref[...], _rstd(xv))
        dx_ref[...] = dxu_ref[...] + dxn
        dgain_ref[...] += jnp.sum(dgc, axis=0, keepdims=True)

    row = lambda i: (i, 0)
    fix = lambda i: (0, 0)
    r_args, r_in, r_shape, r_out, r_scratch = _ride_parts(ride)
    first = lambda: pl.program_id(0) == 0
    last = lambda: pl.program_id(0) == T // tm - 1
    outs = pl.pallas_call(
        _riding(body, 11 + 3 * nb, 5, 0, ride, first, last), name="in_proj_bwd", grid=(T // tm,),
        in_specs=[pl.BlockSpec((tm, D), row), pl.BlockSpec((tm, D), row), pl.BlockSpec((1, D), fix),
                  pl.BlockSpec((N, D), fix), pl.BlockSpec((tm, W), row), pl.BlockSpec((tm, W), lambda i: (i, 1)),
                  pl.BlockSpec((1, 128), fix), pl.BlockSpec((1, 128), fix)] + [pl.BlockSpec((tm, W), row)] * (3 * nb)
                 + [pl.BlockSpec((tm, 256), row), pl.BlockSpec((tm, 128), row), pl.BlockSpec((tm, 128), row)] + r_in,
        out_specs=[pl.BlockSpec((tm, D), row), pl.BlockSpec((tm, N), row), pl.BlockSpec((1, D), fix),
                   pl.BlockSpec((1, 128), fix), pl.BlockSpec((1, 128), fix)] + r_out,
        out_shape=[jax.ShapeDtypeStruct((T, D), F32), jax.ShapeDtypeStruct((T, N), BF16),
                   jax.ShapeDtypeStruct((1, D), F32), jax.ShapeDtypeStruct((1, 128), F32),
                   jax.ShapeDtypeStruct((1, 128), F32)] + r_shape,
        scratch_shapes=r_scratch,
        compiler_params=_params(1),
    )(dx_up, x, gain, w, proj, proj, gq, gk, *[a for triple in dqkv for a in triple], dcq, dckv, dkpe, *r_args)
    return outs[:5], outs[5:]


def _out_proj(x, o_dil, o_mla, g_dil, g_mla, w, tm=1024):
    T, D = x.shape
    W = o_dil.shape[1]

    def body(x_ref, od_ref, om_ref, gd_ref, gm_ref, w_ref, xo_ref, oc_ref):
        od, om = od_ref[...], om_ref[...]
        oc_ref[:, 0:W] = (od * _rstd(od) * gd_ref[...]).astype(BF16)
        oc_ref[:, W:2 * W] = (om * _rstd(om) * gm_ref[...]).astype(BF16)
        xo_ref[...] = x_ref[...] + _dot(oc_ref[...], w_ref[...])

    row = lambda i: (i, 0)
    fix = lambda i: (0, 0)
    return pl.pallas_call(
        body, name="out_proj", grid=(T // tm,),
        in_specs=[pl.BlockSpec((tm, D), row), pl.BlockSpec((tm, W), row), pl.BlockSpec((tm, W), row),
                  pl.BlockSpec((1, W), fix), pl.BlockSpec((1, W), fix), pl.BlockSpec((2 * W, D), fix)],
        out_specs=[pl.BlockSpec((tm, D), row), pl.BlockSpec((tm, 2 * W), row)],
        out_shape=[jax.ShapeDtypeStruct((T, D), F32), jax.ShapeDtypeStruct((T, 2 * W), BF16)],
        compiler_params=_params(1),
    )(x, o_dil, o_mla, g_dil, g_mla, w)


def _out_proj_bwd(dx, o_dil, o_mla, g_dil, g_mla, w, tm=1024):
    T, D = dx.shape
    W = o_dil.shape[1]

    def body(dx_ref, od_ref, om_ref, gd_ref, gm_ref, w_ref, dod_ref, dom_ref, dgd_ref, dgm_ref):
        @pl.when(pl.program_id(0) == 0)
        def _():
            dgd_ref[...] = jnp.zeros_like(dgd_ref)
            dgm_ref[...] = jnp.zeros_like(dgm_ref)

        doc = _dot_nt(dx_ref[...].astype(BF16), w_ref[...])
        od, om = od_ref[...], om_ref[...]
        dod, dgd = _rms_bwd(doc[:, 0:W], od, gd_ref[...], _rstd(od))
        dom, dgm = _rms_bwd(doc[:, W:2 * W], om, gm_ref[...], _rstd(om))
        dod_ref[...] = dod
        dom_ref[...] = dom
        dgd_ref[...] += jnp.sum(dgd, axis=0, keepdims=True)
        dgm_ref[...] += jnp.sum(dgm, axis=0, keepdims=True)

    row = lambda i: (i, 0)
    fix = lambda i: (0, 0)
    return pl.pallas_call(
        body, name="out_proj_bwd", grid=(T // tm,),
        in_specs=[pl.BlockSpec((tm, D), row), pl.BlockSpec((tm, W), row), pl.BlockSpec((tm, W), row),
                  pl.BlockSpec((1, W), fix), pl.BlockSpec((1, W), fix), pl.BlockSpec((2 * W, D), fix)],
        out_specs=[pl.BlockSpec((tm, W), row), pl.BlockSpec((tm, W), row),
                   pl.BlockSpec((1, W), fix), pl.BlockSpec((1, W), fix)],
        out_shape=[jax.ShapeDtypeStruct((T, W), F32), jax.ShapeDtypeStruct((T, W), F32),
                   jax.ShapeDtypeStruct((1, W), F32), jax.ShapeDtypeStruct((1, W), F32)],
        compiler_params=_params(1),
    )(dx, o_dil, o_mla, g_dil, g_mla, w)


def _pair_rstd(x, lo):
    sq = x * x
    s0 = jnp.sum(jnp.where(lo, sq, 0.0), axis=-1, keepdims=True)
    s1 = jnp.sum(jnp.where(lo, 0.0, sq), axis=-1, keepdims=True)
    return jnp.where(lo, lax.rsqrt(s0 / DIL_HD + EPS), lax.rsqrt(s1 / DIL_HD + EPS))


def _pair_rms_bwd(dn, x, r, g, lo):
    u = dn * g
    t = u * x
    d0 = jnp.sum(jnp.where(lo, t, 0.0), axis=-1, keepdims=True)
    d1 = jnp.sum(jnp.where(lo, 0.0, t), axis=-1, keepdims=True)
    dx = r * u - x * (r * r * r) * (jnp.where(lo, d0, d1) / DIL_HD)
    return dx, jnp.sum(dn * x * r, axis=0, keepdims=True)


def _pair_col(x, lo, e):
    sel = lo if e == 0 else jnp.logical_not(lo)
    return jnp.max(jnp.where(sel, x, NEG), axis=-1, keepdims=True)


def _first_head_lanes():
    return lax.broadcasted_iota(jnp.int32, (DIL_BLOCK, DIL_BLOCK), 1) < DIL_HD


def _window_masks():
    i = np.arange(DIL_BLOCK)[:, None]
    j = np.arange(DIL_BLOCK)[None, :]
    cur = j <= i
    both = np.concatenate([j >= i, cur], axis=1)
    first = np.concatenate([np.zeros_like(cur), cur], axis=1)
    return jnp.asarray(np.where(np.stack([both, first]), 0.0, NEG).reshape(2, -1), F32)


def _stack_heads(x, lo):
    return jnp.concatenate([jnp.where(lo, x, 0.0), jnp.where(lo, 0.0, x)], axis=0)


def _unstack_heads(x2, lo):
    return jnp.where(lo, x2[:DIL_BLOCK], x2[DIL_BLOCK:])


def _dil_pairs(d):
    return 4 if d == 1 else 1


def _sub_rows(r, d):
    return pl.ds(r, DIL_BLOCK, stride=d) if d > 1 else pl.ds(0, DIL_BLOCK)


def _store_piece(scratch, i, part, piece):
    if part is None:
        scratch[i] = piece
    else:
        scratch[i, pl.ds(DIL_BLOCK * part, DIL_BLOCK), :] = piece


def _split_subsequences(loads, d, P, stage=None):
    if d == 16:
        group = 4 * DIL_BLOCK
        for block, scratch, part in loads:
            for a in range(4):
                stage[pl.ds(a * group, group), :] = block[pl.ds(a, group, stride=4), :]
            for a in range(4):
                for b in range(4):
                    _store_piece(scratch, a + 4 * b, part, stage[pl.ds(a * group + b, DIL_BLOCK, stride=4), :])
        return
    for r in range(d):
        for p in range(P):
            for block, scratch, part in loads:
                _store_piece(scratch, r * P + p, part, block[_sub_rows(r, d), pl.ds(128 * p, 128)])


def _keep_previous_block(scratches, n):
    for scratch in scratches:
        @pl.when(n == 0)
        def _():
            scratch[:, pl.ds(0, DIL_BLOCK), :] = jnp.zeros((scratch.shape[0], DIL_BLOCK, 128), F32)

        @pl.when(n > 0)
        def _():
            scratch[:, pl.ds(0, DIL_BLOCK), :] = scratch[:, pl.ds(DIL_BLOCK, DIL_BLOCK), :]


def _merge_subsequences(stores, d, P, stage=None):
    if d == 16:
        group = 4 * DIL_BLOCK
        for block, scratch, plus in stores:
            for a in range(4):
                for b in range(4):
                    stage[pl.ds(a * group + b, DIL_BLOCK, stride=4), :] = scratch[a + 4 * b]
            for a in range(4):
                rows = pl.ds(a, group, stride=4)
                val = stage[pl.ds(a * group, group), :]
                block[rows, :] = val if plus is None else val + plus[rows, :]
        return
    for r in range(d):
        for p in range(P):
            for block, scratch, plus in stores:
                part = _sub_rows(r, d), pl.ds(128 * p, 128)
                block[part] = scratch[r * P + p] if plus is None else scratch[r * P + p] + plus[part]


def _dil_fwd(qh, kh, proj, bias, d, prev):
    T = proj.shape[0]
    P = _dil_pairs(d)
    rows, cw, n_it = DIL_BLOCK * d, 128 * P, d * P
    nblk = T // rows
    has_prev = prev is not None

    def body(*refs):
        q_ref, kc_ref, vc_ref, bias_ref = refs[:4]
        refs = refs[4:]
        if has_prev:
            oin_ref, lin_ref = refs[:2]
            refs = refs[2:]
        o_ref, l_ref, stage, qs, ks, vs, os_, ls_ = refs[:8]
        pb, n = pl.program_id(0), pl.program_id(1)
        lo = _first_head_lanes()
        first = (n == 0).astype(jnp.int32)
        _keep_previous_block((ks, vs), n)
        loads = [(q_ref, qs, None), (kc_ref, ks, 1), (vc_ref, vs, 1)]
        if has_prev:
            ois, lis = refs[8:]
            loads += [(oin_ref, ois, None), (lin_ref, lis, None)]
        _split_subsequences(loads, d, P, stage)

        def step(i, carry):
            q2 = _stack_heads(qs[i], lo).astype(BF16)
            s = _dot_nt(q2, ks[i].astype(BF16)) + bias_ref[first, pb * P + i % P]
            m = jnp.max(s, axis=-1, keepdims=True)
            p = jnp.exp(s - m)
            l = jnp.sum(p, axis=-1, keepdims=True)
            o = _unstack_heads(_dot(p.astype(BF16), vs[i].astype(BF16)) / l, lo)
            lse = _unstack_heads(jnp.broadcast_to(m + jnp.log(l), (2 * DIL_BLOCK, 128)), lo)
            if has_prev:
                lin = lis[i]
                mx = jnp.maximum(lin, lse)
                lnew = mx + jnp.log(jnp.exp(lin - mx) + jnp.exp(lse - mx))
                o = ois[i] * jnp.exp(lin - lnew) + o * jnp.exp(lse - lnew)
                lse = lnew
            os_[i] = o
            ls_[i] = lse
            return carry

        lax.fori_loop(0, n_it, step, 0, unroll=min(n_it, 8))
        _merge_subsequences([(o_ref, os_, None), (l_ref, ls_, None)], d, P, stage)

    blk = (rows, cw)
    vcol = 2 * DIL_WIDTH // cw
    tok = pl.BlockSpec(blk, lambda pb, n: (n, pb))
    in_specs = [tok, tok, pl.BlockSpec(blk, lambda pb, n: (n, vcol + pb)),
                pl.BlockSpec(bias.shape, lambda pb, n: (0, 0, 0, 0))]
    args = [qh, kh, proj, bias]
    one, two = pltpu.VMEM((n_it, DIL_BLOCK, 128), F32), pltpu.VMEM((n_it, 2 * DIL_BLOCK, 128), F32)
    scratch = [pltpu.VMEM((rows, 128), F32), one, two, two, one, one]
    if has_prev:
        in_specs += [tok, tok]
        args += list(prev)
        scratch += [one, one]
    out = jax.ShapeDtypeStruct((T, DIL_WIDTH), F32)
    return pl.pallas_call(
        body, name=f"dil_fwd_d{d}", grid=(DIL_HEADS // 2 // P, nblk), in_specs=in_specs, out_specs=[tok, tok],
        out_shape=[out, out], scratch_shapes=scratch, compiler_params=_params(2),
    )(*args)


def _dil_bwd(qh, kh, proj, o, lse, do, bias, d, prev):
    T = proj.shape[0]
    P = _dil_pairs(d)
    rows, cw, n_it = DIL_BLOCK * d, 128 * P, d * P
    nblk = T // rows
    has_prev = prev is not None

    def body(*refs):
        q_ref, kc_ref, vc_ref, o_ref, l_ref, do_ref, bias_ref = refs[:7]
        dqi_ref, dki_ref, dvi_ref = refs[7:10] if has_prev else (None, None, None)
        dq_ref, dk_ref, dv_ref, db_ref, stage, qs, ks, vs, os_, ls_, dos, dqs, dks, dvs, ck, cv = refs[7 + 3 * has_prev:]
        pb, n = pl.program_id(0), pl.program_id(1)
        lo = _first_head_lanes()
        first = (n == 0).astype(jnp.int32)

        @pl.when((pb == 0) & (n == 0))
        def _():
            db_ref[...] = jnp.zeros_like(db_ref)

        @pl.when(n == 0)
        def _():
            ck[...] = jnp.zeros_like(ck)
            cv[...] = jnp.zeros_like(cv)

        _keep_previous_block((ks, vs), n)
        _split_subsequences([(q_ref, qs, None), (kc_ref, ks, 1), (vc_ref, vs, 1),
                             (o_ref, os_, None), (l_ref, ls_, None), (do_ref, dos, None)], d, P, stage)

        def step(i, carry):
            pair = pb * P + i % P
            q2 = _stack_heads(qs[i], lo).astype(BF16)
            kcat, vcat = ks[i].astype(BF16), vs[i].astype(BF16)
            dov = dos[i]
            do2 = _stack_heads(dov, lo).astype(BF16)
            delta = jnp.sum(_stack_heads(dov * os_[i], lo), axis=-1, keepdims=True)
            lse_pair = ls_[i]
            lse2 = jnp.concatenate([_pair_col(lse_pair, lo, 0), _pair_col(lse_pair, lo, 1)], axis=0)
            s = _dot_nt(q2, kcat) + bias_ref[first, pair]
            p = jnp.exp(s - lse2)
            ds = p * (_dot_nt(do2, vcat) - delta)
            db_ref[pair] += ds
            dsb = ds.astype(BF16)
            dqs[i] = _unstack_heads(_dot(dsb, kcat), lo)
            dk2 = _dot_tn(dsb, q2)
            dv2 = _dot_tn(p.astype(BF16), do2)
            dks[i] = ck[i] + dk2[:DIL_BLOCK]
            dvs[i] = cv[i] + dv2[:DIL_BLOCK]
            ck[i] = dk2[DIL_BLOCK:]
            cv[i] = dv2[DIL_BLOCK:]
            return carry

        @pl.when(n < nblk)
        def _():
            lax.fori_loop(0, n_it, step, 0, unroll=min(n_it, 8))
            _merge_subsequences([(dq_ref, dqs, dqi_ref), (dk_ref, dks, dki_ref), (dv_ref, dvs, dvi_ref)], d, P, stage)

        @pl.when(n == nblk)
        def _():
            _merge_subsequences([(dk_ref, ck, dki_ref), (dv_ref, cv, dvi_ref)], d, P, stage)

    blk = (rows, cw)
    vcol = 2 * DIL_WIDTH // cw
    qn_ = lambda n: jnp.minimum(n, nblk - 1)
    pn_ = lambda n: jnp.maximum(n - 1, 0)
    fix3 = lambda pb, n: (0, 0, 0)
    tok_q = pl.BlockSpec(blk, lambda pb, n: (qn_(n), pb))
    tok_p = pl.BlockSpec(blk, lambda pb, n: (pn_(n), pb))
    in_specs = [tok_q, tok_q, pl.BlockSpec(blk, lambda pb, n: (qn_(n), vcol + pb)), tok_q, tok_q, tok_q,
                pl.BlockSpec(bias.shape, lambda pb, n: (0, 0, 0, 0))]
    in_specs += [tok_q, tok_p, tok_p] if has_prev else []
    tok_shape = jax.ShapeDtypeStruct((T, DIL_WIDTH), F32)
    one, two = pltpu.VMEM((n_it, DIL_BLOCK, 128), F32), pltpu.VMEM((n_it, 2 * DIL_BLOCK, 128), F32)
    dq, dk, dv, db = pl.pallas_call(
        body, name=f"dil_bwd_d{d}", grid=(DIL_HEADS // 2 // P, nblk + 1), in_specs=in_specs,
        out_specs=[tok_q, tok_p, tok_p, pl.BlockSpec(bias.shape[1:], fix3)],
        out_shape=[tok_shape, tok_shape, tok_shape, jax.ShapeDtypeStruct(bias.shape[1:], F32)],
        scratch_shapes=[pltpu.VMEM((rows, 128), F32), one, two, two] + [one] * 8,
        compiler_params=_params(2),
    )(qh, kh, proj, o, lse, do, bias, *(prev or ()))
    return (dq, dk, dv), db


def _t5_bucket(dist):
    max_exact = REL_BUCKETS // 2
    dd = np.maximum(dist, 1).astype(np.float32)
    large = max_exact + (np.log(dd / max_exact) / np.log(REL_MAX_DIST / max_exact)
                         * (REL_BUCKETS - max_exact)).astype(np.int32)
    large = np.minimum(large, REL_BUCKETS - 1)
    return np.where(dist < max_exact, dist, large).astype(np.int32)


def _bucket_onehots():
    i = np.arange(DIL_BLOCK)[:, None]
    j = np.arange(DIL_BLOCK)[None, :]
    out = []
    for _, d in DIL_BRANCHES:
        dist = np.concatenate([DIL_BLOCK + i - j, i - j], axis=1)
        bucket = _t5_bucket(np.clip(dist, 0, None) * d).reshape(-1)
        out.append(jnp.asarray(np.eye(REL_BUCKETS, dtype=np.float32)[:, bucket], BF16))
    return out


def _bias_tables(rel_bias, onehots):
    n = len(onehots)

    def body(rb_ref, mask_ref, *refs):
        parts = _split3(rb_ref[...])
        for k in range(n):
            oh = refs[k][...]
            bias = _dot(parts[0], oh) + _dot(parts[1], oh) + _dot(parts[2], oh)
            refs[n + k][0] = bias + mask_ref[0:1, :]
            refs[n + k][1] = bias + mask_ref[1:2, :]

    flat = pl.pallas_call(
        body, name="bias_tables",
        out_shape=[jax.ShapeDtypeStruct((2, DIL_HEADS, 2 * DIL_BLOCK * DIL_BLOCK), F32)] * n,
        compiler_params=pltpu.CompilerParams(vmem_limit_bytes=VMEM_LIMIT),
    )(rel_bias, _window_masks(), *onehots)
    return [t.reshape(2, DIL_HEADS // 2, 2 * DIL_BLOCK, 2 * DIL_BLOCK) for t in flat]


def _bias_grad(dbs, onehots):
    n = len(dbs)
    dbs = [t.reshape(DIL_HEADS, 2 * DIL_BLOCK * DIL_BLOCK) for t in dbs]

    def body(*refs):
        acc = jnp.zeros((DIL_HEADS, REL_BUCKETS), F32)
        for k in range(n):
            oh = refs[n + k][...]
            for part in _split3(refs[k][...]):
                acc = acc + _dot_nt(part, oh)
        refs[-1][...] = acc

    return pl.pallas_call(
        body, name="bias_grad",
        out_shape=jax.ShapeDtypeStruct((DIL_HEADS, REL_BUCKETS), F32),
        compiler_params=pltpu.CompilerParams(vmem_limit_bytes=VMEM_LIMIT),
    )(*dbs, *onehots)


def _swap_halves(x):
    lane = lax.broadcasted_iota(jnp.int32, x.shape, 1)
    first = (lane % 64) < 32
    return jnp.where(first, pltpu.roll(x, 96, 1), pltpu.roll(x, 32, 1))


def _rope_tables(T):
    pos = jnp.arange(T, dtype=F32)
    inv_freq = ROPE_BASE ** (-jnp.arange(0, MLA_ROPE, 2, dtype=F32) / MLA_ROPE)
    ang = pos[:, None] * inv_freq[None, :]
    z = jnp.zeros((T, 128 - MLA_ROPE), F32)
    cos = jnp.concatenate([jnp.cos(ang), jnp.cos(ang), z], axis=-1)
    sin = jnp.concatenate([-jnp.sin(ang), jnp.sin(ang), z], axis=-1)
    return cos, sin


def _mla_prep(proj, cos, sin, g_qa, g_kva, g_q, g_k, wq, wkv, tm=1024):
    T = proj.shape[0]
    H = MLA_HEADS
    scale = MLA_QK ** -0.5

    def body(cq_ref, ckv_ref, kpe_ref, cos_ref, sin_ref, gqa_ref, gkva_ref, gq_ref, gk_ref, wq_ref, wkv_ref,
             q_ref, k_ref, v_ref):
        cosv, sinv = cos_ref[...], sin_ref[...]

        def rope(x):
            return x * cosv + _swap_halves(x) * sinv

        cq = cq_ref[...]
        qp = _dot_nt((cq * _rstd(cq) * gqa_ref[...]).astype(BF16), wq_ref[...])
        ckv = ckv_ref[...]
        kvp = _dot((ckv * _rstd(ckv) * gkva_ref[...]).astype(BF16), wkv_ref[...])
        kpe = kpe_ref[...]
        one_hot_lane = (lax.broadcasted_iota(jnp.int32, (tm, 128), 1) == 0).astype(BF16)
        for h in range(H):
            a = qp[:, MLA_PAD * h:MLA_PAD * (h + 1)]
            qn = a * _rstd(a, MLA_QK) * gq_ref[...]
            q_ref[h, :, 0:128] = (qn[:, 0:128] * scale).astype(BF16)
            q_ref[h, :, 128:256] = (rope(qn[:, 128:256]) * scale).astype(BF16)
            kn = kvp[:, MLA_PAD * h:MLA_PAD * h + 128]
            r = lax.rsqrt((jnp.sum(kn * kn, axis=-1, keepdims=True)
                           + jnp.sum(kpe * kpe, axis=-1, keepdims=True)) / MLA_QK + EPS)
            k_ref[h, :, 0:128] = (kn * r * gk_ref[:, 0:128]).astype(BF16)
            k_ref[h, :, 128:256] = rope(kpe * r * gk_ref[:, 128:256]).astype(BF16)
            v_ref[h, :, 0:128] = kvp[:, MLA_PAD * h + 128:MLA_PAD * (h + 1)].astype(BF16)
            v_ref[h, :, 128:256] = one_hot_lane

    fix = lambda i: (0, 0)
    return pl.pallas_call(
        body, name="mla_prep", grid=(T // tm,),
        in_specs=[pl.BlockSpec((tm, MLA_Q_RANK), lambda i: (i, CQ_COL // MLA_Q_RANK)),
                  pl.BlockSpec((tm, MLA_KV_RANK), lambda i: (i, CKV_COL // MLA_KV_RANK)),
                  pl.BlockSpec((tm, 128), lambda i: (i, KPE_COL // 128)),
                  pl.BlockSpec((tm, 128), lambda i: (i, 0)), pl.BlockSpec((tm, 128), lambda i: (i, 0)),
                  pl.BlockSpec((1, 256), fix), pl.BlockSpec((1, 128), fix),
                  pl.BlockSpec((1, 256), fix), pl.BlockSpec((1, 256), fix),
                  pl.BlockSpec((H * MLA_PAD, 256), fix), pl.BlockSpec((128, H * MLA_PAD), fix)],
        out_specs=[pl.BlockSpec((H, tm, MLA_PAD), lambda i: (0, i, 0)), pl.BlockSpec((H, tm, MLA_PAD), lambda i: (0, i, 0)),
                   pl.BlockSpec((H, tm, 2 * MLA_V), lambda i: (0, i, 0))],
        out_shape=[jax.ShapeDtypeStruct((H, T, MLA_PAD), BF16), jax.ShapeDtypeStruct((H, T, MLA_PAD), BF16),
                   jax.ShapeDtypeStruct((H, T, 2 * MLA_V), BF16)],
        compiler_params=_params(1),
    )(proj, proj, proj, cos, sin, g_qa, g_kva, g_q, g_k, wq, wkv)


def _mla_prep_bwd(proj, cos, sin, g_qa, g_kva, g_q, g_k, wq, wkv, dq, dk, dv, tm=1024):
    T = proj.shape[0]
    H = MLA_HEADS
    scale = MLA_QK ** -0.5

    def body(cq_ref, ckv_ref, kpe_ref, cos_ref, sin_ref, gqa_ref, gkva_ref, gq_ref, gk_ref, wq_ref, wkv_ref,
             dq_ref, dk_ref, dv_ref,
             dcq_ref, dckv_ref, dkpe_ref, cqn_ref, ckvn_ref, dqp_ref, dkvp_ref,
             dgqa_ref, dgkva_ref, dgq_ref, dgk_ref):
        @pl.when(pl.program_id(0) == 0)
        def _():
            for ref in (dgqa_ref, dgkva_ref, dgq_ref, dgk_ref):
                ref[...] = jnp.zeros_like(ref)

        cosv, sinv = cos_ref[...], sin_ref[...]

        def rope_bwd(dy):
            return dy * cosv + _swap_halves(dy * sinv)

        cq = cq_ref[...]
        rcq = _rstd(cq)
        cqn = (cq * rcq * gqa_ref[...]).astype(BF16)
        cqn_ref[...] = cqn
        qp = _dot_nt(cqn, wq_ref[...])
        ckv = ckv_ref[...]
        rckv = _rstd(ckv)
        ckvn = (ckv * rckv * gkva_ref[...]).astype(BF16)
        ckvn_ref[...] = ckvn
        kvp = _dot(ckvn, wkv_ref[...])
        kpe = kpe_ref[...]
        dkpe = jnp.zeros_like(kpe)
        dgq = jnp.zeros((1, MLA_PAD), F32)
        dgk = jnp.zeros((1, MLA_PAD), F32)
        for h in range(H):
            a = qp[:, MLA_PAD * h:MLA_PAD * (h + 1)]
            dqh = dq_ref[h]
            dn = jnp.concatenate([dqh[:, 0:128], rope_bwd(dqh[:, 128:256])], axis=-1) * scale
            da, dg = _rms_bwd(dn, a, gq_ref[...], _rstd(a, MLA_QK), MLA_QK)
            dgq = dgq + jnp.sum(dg, axis=0, keepdims=True)
            dqp_ref[:, MLA_PAD * h:MLA_PAD * (h + 1)] = da.astype(BF16)

            ak = jnp.concatenate([kvp[:, MLA_PAD * h:MLA_PAD * h + 128], kpe], axis=-1)
            dkh = dk_ref[h]
            dnk = jnp.concatenate([dkh[:, 0:128], rope_bwd(dkh[:, 128:256])], axis=-1)
            dak, dg = _rms_bwd(dnk, ak, gk_ref[...], _rstd(ak, MLA_QK), MLA_QK)
            dgk = dgk + jnp.sum(dg, axis=0, keepdims=True)
            dkpe = dkpe + dak[:, 128:256]
            dkvp_ref[:, MLA_PAD * h:MLA_PAD * h + 128] = dak[:, 0:128].astype(BF16)
            dkvp_ref[:, MLA_PAD * h + 128:MLA_PAD * (h + 1)] = dv_ref[h].astype(BF16)
        dkpe_ref[...] = dkpe
        dgq_ref[...] += dgq
        dgk_ref[...] += dgk
        dcq, dg = _rms_bwd(_dot(dqp_ref[...], wq_ref[...]), cq, gqa_ref[...], rcq)
        dcq_ref[...] = dcq
        dgqa_ref[...] += jnp.sum(dg, axis=0, keepdims=True)
        dckv, dg = _rms_bwd(_dot_nt(dkvp_ref[...], wkv_ref[...]), ckv, gkva_ref[...], rckv)
        dckv_ref[...] = dckv
        dgkva_ref[...] += jnp.sum(dg, axis=0, keepdims=True)

    fix = lambda i: (0, 0)
    row = lambda i: (i, 0)
    head = lambda i: (0, i, 0)
    return pl.pallas_call(
        body, name="mla_prep_bwd", grid=(T // tm,),
        in_specs=[pl.BlockSpec((tm, MLA_Q_RANK), lambda i: (i, CQ_COL // MLA_Q_RANK)),
                  pl.BlockSpec((tm, MLA_KV_RANK), lambda i: (i, CKV_COL // MLA_KV_RANK)),
                  pl.BlockSpec((tm, 128), lambda i: (i, KPE_COL // 128)),
                  pl.BlockSpec((tm, 128), row), pl.BlockSpec((tm, 128), row),
                  pl.BlockSpec((1, 256), fix), pl.BlockSpec((1, 128), fix),
                  pl.BlockSpec((1, 256), fix), pl.BlockSpec((1, 256), fix),
                  pl.BlockSpec((H * MLA_PAD, 256), fix), pl.BlockSpec((128, H * MLA_PAD), fix),
                  pl.BlockSpec((H, tm, MLA_PAD), head), pl.BlockSpec((H, tm, MLA_PAD), head),
                  pl.BlockSpec((H, tm, MLA_V), head)],
        out_specs=[pl.BlockSpec((tm, 256), row), pl.BlockSpec((tm, 128), row), pl.BlockSpec((tm, 128), row),
                   pl.BlockSpec((tm, 256), row), pl.BlockSpec((tm, 128), row),
                   pl.BlockSpec((tm, H * MLA_PAD), row), pl.BlockSpec((tm, H * MLA_PAD), row),
                   pl.BlockSpec((1, 256), fix), pl.BlockSpec((1, 128), fix),
                   pl.BlockSpec((1, 256), fix), pl.BlockSpec((1, 256), fix)],
        out_shape=[jax.ShapeDtypeStruct((T, 256), F32), jax.ShapeDtypeStruct((T, 128), F32),
                   jax.ShapeDtypeStruct((T, 128), F32),
                   jax.ShapeDtypeStruct((T, 256), BF16), jax.ShapeDtypeStruct((T, 128), BF16),
                   jax.ShapeDtypeStruct((T, H * MLA_PAD), BF16), jax.ShapeDtypeStruct((T, H * MLA_PAD), BF16),
                   jax.ShapeDtypeStruct((1, 256), F32), jax.ShapeDtypeStruct((1, 128), F32),
                   jax.ShapeDtypeStruct((1, 256), F32), jax.ShapeDtypeStruct((1, 256), F32)],
        compiler_params=_params(1),
    )(proj, proj, proj, cos, sin, g_qa, g_kva, g_q, g_k, wq, wkv, dq, dk, dv)


def _causal_pairs(T, tq, tk, key_major):
    pairs = [(i, j) for i in range(T // tq) for j in range(T // tk) if j * tk <= i * tq + tq - 1]
    if key_major:
        pairs.sort(key=lambda p: (p[1], p[0]))
    outer = [p[1] if key_major else p[0] for p in pairs]
    first = [int(t == 0 or outer[t] != outer[t - 1]) for t in range(len(pairs))]
    last = [int(t == len(pairs) - 1 or outer[t] != outer[t + 1]) for t in range(len(pairs))]
    tab = lambda v: jnp.asarray(np.array(v, np.int32))
    return tab([p[0] for p in pairs]), tab([p[1] for p in pairs]), tab(first), tab(last)


def _causal_scores(qv, kv, row0):
    s = _dot_nt(qv, kv)
    if row0 is not None:
        row = lax.broadcasted_iota(jnp.int32, s.shape, 0) + row0
        col = lax.broadcasted_iota(jnp.int32, s.shape, 1)
        s = jnp.where(col <= row, s, NEG)
    return s


def _causal_variants(qi, ki, tq, tk, update):
    assert tk % tq == 0
    diag = qi * tq - ki * tk
    for off in range(0, tk, tq):
        pl.when(diag == off)(lambda off=off: update(off))
    pl.when(diag >= tk)(lambda: update(None))


def _visible_keys(off, row0, rows, tk):
    return tk if off is None else min(tk, off + row0 + rows)


def _mla_attn(q, k, v, ride=None, tq=2048, tk=2048, rc=256):
    H, T, _ = q.shape
    tables = _causal_pairs(T, tq, tk, key_major=False)
    n_pairs = int(tables[0].shape[0])
    r_args, r_in, r_shape, r_out, r_scratch = _ride_parts(ride)

    def body(qt, kt, ft, lt, q_ref, k_ref, v_ref, o_ref, lse_ref, m_s, acc):
        t = pl.program_id(1)
        qi, ki = qt[t], kt[t]

        @pl.when(ft[t] == 1)
        def _():
            m_s[...] = jnp.full_like(m_s, NEG)
            acc[...] = jnp.zeros_like(acc)

        def update(off):
            for c in range(tq // rc):
                rows = pl.ds(c * rc, rc)
                keys = pl.ds(0, _visible_keys(off, c * rc, rc, tk))
                s = _causal_scores(q_ref[rows, :], k_ref[keys, :], None if off is None else off + c * rc)
                m_old = m_s[rows, :]
                m_new = jnp.maximum(m_old, jnp.max(s, axis=-1, keepdims=True))
                p = jnp.exp(s - m_new).astype(BF16)
                acc[rows, :] = jnp.exp(m_old - m_new) * acc[rows, :] + _dot(p, v_ref[keys, :])
                m_s[rows, :] = m_new

        _causal_variants(qi, ki, tq, tk, update)

        @pl.when(lt[t] == 1)
        def _():
            l = jnp.max(acc[:, MLA_V:], axis=-1, keepdims=True)
            o_ref[...] = acc[:, :MLA_V] / l
            lse_ref[...] = jnp.broadcast_to(m_s[...] + jnp.log(l), lse_ref.shape)

    qrow = lambda h, t, qt, kt, ft, lt: (h, qt[t], 0)
    krow = lambda h, t, qt, kt, ft, lt: (h, kt[t], 0)
    first = lambda: (pl.program_id(0) == 0) & (pl.program_id(1) == 0)
    last = lambda: (pl.program_id(0) == H - 1) & (pl.program_id(1) == n_pairs - 1)
    outs = pl.pallas_call(
        _riding(body, 7, 2, 2, ride, first, last), name="mla_attn",
        grid_spec=pltpu.PrefetchScalarGridSpec(
            num_scalar_prefetch=4, grid=(H, n_pairs),
            in_specs=[pl.BlockSpec((None, tq, MLA_PAD), qrow), pl.BlockSpec((None, tk, MLA_PAD), krow),
                      pl.BlockSpec((None, tk, 2 * MLA_V), krow)] + r_in,
            out_specs=[pl.BlockSpec((tq, MLA_V), lambda h, t, qt, kt, ft, lt: (qt[t], h)),
                       pl.BlockSpec((None, tq, 128), qrow)] + r_out,
            scratch_shapes=[pltpu.VMEM((tq, 1), F32), pltpu.VMEM((tq, 2 * MLA_V), F32)] + r_scratch),
        out_shape=[jax.ShapeDtypeStruct((T, H * MLA_V), F32), jax.ShapeDtypeStruct((H, T, 128), F32)] + r_shape,
        compiler_params=_params(2),
    )(*tables, q, k, v, *r_args)
    return outs[:2], outs[2:]


def _mla_attn_bwd(q, k, v, o, lse, do, ride=None, tq=1024, tk=1024, rc=512, rc_diagonal=256):
    H, T, _ = q.shape
    tables = _causal_pairs(T, tq, tk, key_major=True)
    n_pairs = int(tables[0].shape[0])
    r_args, r_in, r_shape, r_out, r_scratch = _ride_parts(ride)

    def body(qt, kt, ft, lt, q_ref, k_ref, v_ref, o_ref, lse_ref, do_ref, dq_ref, dk_ref, dv_ref, dk_s, dv_s):
        t = pl.program_id(1)
        qi, ki = qt[t], kt[t]

        @pl.when(t == 0)
        def _():
            dq_ref[...] = jnp.zeros_like(dq_ref)

        @pl.when(ft[t] == 1)
        def _():
            dk_s[...] = jnp.zeros_like(dk_s)
            dv_s[...] = jnp.zeros_like(dv_s)

        def update(off):
            rows_per = rc if off is None else rc_diagonal
            for c in range(tq // rows_per):
                rows = pl.ds(c * rows_per, rows_per)
                keys = pl.ds(0, _visible_keys(off, c * rows_per, rows_per, tk))
                kk, vv = k_ref[keys, :], v_ref[keys, :]
                qv, dov = q_ref[rows, :], do_ref[rows, :]
                delta = jnp.sum(dov * o_ref[rows, :], axis=-1, keepdims=True)
                lse_v = jnp.max(lse_ref[rows, :], axis=-1, keepdims=True)
                p = jnp.exp(_causal_scores(qv, kk, None if off is None else off + c * rows_per) - lse_v)
                dob = dov.astype(BF16)
                dv_s[keys, :] += _dot_tn(p.astype(BF16), dob)
                ds = (p * (_dot_nt(dob, vv) - delta)).astype(BF16)
                dk_s[keys, :] += _dot_tn(ds, qv)
                out_rows = pl.ds(pl.multiple_of(qi * tq + c * rows_per, rows_per), rows_per)
                dq_ref[out_rows, :] += _dot(ds, kk)

        _causal_variants(qi, ki, tq, tk, update)

        @pl.when(lt[t] == 1)
        def _():
            dk_ref[...] = dk_s[...]
            dv_ref[...] = dv_s[...]

    qrow = lambda h, t, qt, kt, ft, lt: (h, qt[t], 0)
    krow = lambda h, t, qt, kt, ft, lt: (h, kt[t], 0)
    qcol = lambda h, t, qt, kt, ft, lt: (qt[t], h)
    first = lambda: (pl.program_id(0) == 0) & (pl.program_id(1) == 0)
    last = lambda: (pl.program_id(0) == H - 1) & (pl.program_id(1) == n_pairs - 1)
    outs = pl.pallas_call(
        _riding(body, 10, 3, 2, ride, first, last), name="mla_attn_bwd",
        grid_spec=pltpu.PrefetchScalarGridSpec(
            num_scalar_prefetch=4, grid=(H, n_pairs),
            in_specs=[pl.BlockSpec((None, tq, MLA_PAD), qrow), pl.BlockSpec((None, tk, MLA_PAD), krow),
                      pl.BlockSpec((None, tk, MLA_V), krow), pl.BlockSpec((tq, MLA_V), qcol),
                      pl.BlockSpec((None, tq, 128), qrow), pl.BlockSpec((tq, MLA_V), qcol)] + r_in,
            out_specs=[pl.BlockSpec((None, T, MLA_PAD), lambda h, t, qt, kt, ft, lt: (h, 0, 0)),
                       pl.BlockSpec((None, tk, MLA_PAD), krow), pl.BlockSpec((None, tk, MLA_V), krow)] + r_out,
            scratch_shapes=[pltpu.VMEM((tk, MLA_PAD), F32), pltpu.VMEM((tk, MLA_V), F32)] + r_scratch),
        out_shape=[jax.ShapeDtypeStruct((H, T, MLA_PAD), F32), jax.ShapeDtypeStruct((H, T, MLA_PAD), F32),
                   jax.ShapeDtypeStruct((H, T, MLA_V), F32)] + r_shape,
        compiler_params=_params(2),
    )(*tables, q, k, v, o, lse, do, *r_args)
    return outs[:3], outs[3:]


def _pair_gain(g):
    return jnp.tile(g.reshape(1, DIL_HD), (1, 2))


def _pad_gain(g):
    return jnp.pad(g.reshape(1, MLA_QK), ((0, 0), (0, MLA_PAD - MLA_QK)))


def _local_step(x, target, s, comm):
    T = x.shape[0]
    w = comm.w
    gq, gk = _pair_gain(s["dil_q_norm"]) * DIL_HD ** -0.5, _pair_gain(s["dil_k_norm"])
    g_q, g_k = _pad_gain(s["mla_q_norm"]), _pad_gain(s["mla_k_norm"])
    cos, sin = _rope_tables(T)
    onehots = _bucket_onehots()
    biases = _bias_tables(s["rel_bias"], onehots)

    (x1, h1, gate1, up1), got = _ffn_fwd(x, s["ffn1_norm"], w["ffn1_w_gate"], w["ffn1_w_up"], w["ffn1_w_down"],
                                         ride=comm.gather(_GROUPS["attn"]))
    comm.weights_landed(_GROUPS["attn"], got)
    hm, proj, qh, kh = _in_proj(x1, s["mix_norm"], w["w_in"], gq, gk)
    dil = None
    for (_, d), bias in zip(DIL_BRANCHES, biases):
        dil = _dil_fwd(qh, kh, proj, bias, d, dil)
    o_dil, lse_dil = dil
    q, k, v = _mla_prep(proj, cos, sin, s["mla_q_a_norm"], s["mla_kv_a_norm"], g_q, g_k, w["mla_w_q_b"], w["mla_w_kv_b"])
    (o_mla, lse_mla), got = _mla_attn(q, k, v, ride=comm.gather(_GROUPS["ffn2"]))
    comm.weights_landed(_GROUPS["ffn2"], got)
    x2, oc = _out_proj(x1, o_dil, o_mla, s["out_norm_dil"], s["out_norm_mla"], w["w_out"])
    (dy, h2, gate2, up2, loss), _ = _ffn_fwd(x2, s["ffn2_norm"], w["ffn2_w_gate"], w["ffn2_w_up"], w["ffn2_w_down"],
                                             target=target)

    gw, gs = {}, {}

    def ffn_grads(name, dy_in, x_in, h, gate, up, early=None):
        dx, a, dg, du, dyh, dgain = _ffn_bwd(dy_in, x_in, s[name + "_norm"], gate, up,
                                             w[name + "_w_gate"], w[name + "_w_up"], w[name + "_w_down"])
        gs[name + "_norm"] = dgain
        down, gate_n, up_n = (name + "_w_down",), (name + "_w_gate",), (name + "_w_up",)
        ride = lambda names: comm.scatter(names, gw) if early is not None else None
        gw[down[0]], landed = _matmul_tn(a, dyh, 1408, 1024, ride=ride(early))
        comm.grads_landed(early or (), landed)
        gw[gate_n[0]], landed = _matmul_tn(dg, h, 1408, 1024, ride=ride(down))
        comm.grads_landed(down, landed)
        gw[up_n[0]], landed = _matmul_tn(du, h, 1408, 1024, ride=ride(gate_n))
        comm.grads_landed(gate_n, landed)
        return dx

    dx2 = ffn_grads("ffn2", dy, x2, h2, gate2, up2)
    gw["w_out"], _ = _matmul_tn(oc, dx2, 1024, 1024)
    do_dil, do_mla, gs["out_norm_dil"], gs["out_norm_mla"] = _out_proj_bwd(
        dx2, o_dil, o_mla, s["out_norm_dil"], s["out_norm_mla"], w["w_out"])

    (dq, dk, dv), got = _mla_attn_bwd(q, k, v, o_mla, lse_mla, do_mla, ride=comm.scatter(_GROUPS["ffn2"], gw))
    comm.grads_landed(_GROUPS["ffn2"], got)
    (dcq, dckv, dkpe, cqn, ckvn, dqp, dkvp, gs["mla_q_a_norm"], gs["mla_kv_a_norm"], dg_q, dg_k) = _mla_prep_bwd(
        proj, cos, sin, s["mla_q_a_norm"], s["mla_kv_a_norm"], g_q, g_k, w["mla_w_q_b"], w["mla_w_kv_b"], dq, dk, dv)
    gs["mla_q_norm"], gs["mla_k_norm"] = dg_q[:, :MLA_QK], dg_k[:, :MLA_QK]
    gw["mla_w_q_b"], _ = _matmul_tn(dqp, cqn, 1024, 256)
    gw["mla_w_kv_b"], _ = _matmul_tn(ckvn, dkvp, 128, 1024)

    dqkv, dbs = None, []
    for (_, d), bias in reversed(list(zip(DIL_BRANCHES, biases))):
        dqkv, db = _dil_bwd(qh, kh, proj, o_dil, lse_dil, do_dil, bias, d, dqkv)
        dbs.insert(0, db)
    dqkv = [dqkv]
    gs["rel_bias"] = _bias_grad(dbs, onehots)

    ready = tuple(n for n in _GROUPS["attn"] if n != "w_in")
    (dx1, dproj, gs["mix_norm"], dgq, dgk), got = _in_proj_bwd(dx2, x1, s["mix_norm"], w["w_in"], proj, gq, gk,
                                                               dqkv, dcq, dckv, dkpe, ride=comm.scatter(ready, gw))
    comm.grads_landed(ready, got)
    gs["dil_q_norm"] = (dgq[:, :DIL_HD] + dgq[:, DIL_HD:]) * DIL_HD ** -0.5
    gs["dil_k_norm"] = dgk[:, :DIL_HD] + dgk[:, DIL_HD:]
    gw["w_in"], _ = _matmul_tn(dproj, hm, 1024, 1024)
    grad_x = ffn_grads("ffn1", dx1, x, h1, gate1, up1, early=("w_in",))
    return loss, grad_x, gw, gs


def _position():
    x, y, c = lax.axis_index("x"), lax.axis_index("y"), lax.axis_index("c")
    return x, y, c, 4 * x + 2 * y + c


def _peer(x, y, c, k):
    px = 1 - x if k & 4 else x
    py = 1 - y if k & 2 else y
    pc = 1 - c if k & 1 else c
    return (px, py, pc), 4 * px + 2 * py + pc


class _Ride:
    def __init__(self, arrays, scatter):
        self.arrays, self.scatter = list(arrays), list(scatter)
        self.n = n = len(self.arrays)
        self.specs = [pl.BlockSpec(memory_space=pl.ANY)] * n
        self.out_shape = [jax.ShapeDtypeStruct(a.shape if sc else (N_DEV,) + a.shape, a.dtype)
                          for a, sc in zip(self.arrays, self.scatter)]
        self.scratch = [pltpu.SemaphoreType.DMA((n, N_DEV - 1)), pltpu.SemaphoreType.DMA((n, N_DEV - 1)),
                        pltpu.SemaphoreType.DMA((n,))]

    def _copies(self, ins, outs, sems):
        send_sems, recv_sems, local_sems = sems
        x, y, c, me = _position()
        copies = []
        for a in range(self.n):
            src = ins[a].at[me] if self.scatter[a] else ins[a]
            copies.append(pltpu.make_async_copy(src, outs[a].at[me], local_sems.at[a]))
        for k in range(1, N_DEV):
            peer, peer_idx = _peer(x, y, c, k)
            for a in range(self.n):
                src = ins[a].at[peer_idx] if self.scatter[a] else ins[a]
                copies.append(pltpu.make_async_remote_copy(
                    src_ref=src, dst_ref=outs[a].at[me], send_sem=send_sems.at[a, k - 1], recv_sem=recv_sems.at[a, k - 1],
                    device_id=peer, device_id_type=pl.DeviceIdType.MESH))
        return copies

    def start(self, ins, outs, sems):
        for cp in self._copies(ins, outs, sems):
            cp.start()

    def wait(self, ins, outs, sems):
        for cp in self._copies(ins, outs, sems):
            cp.wait()


def _ride_parts(ride):
    if ride is None:
        return [], [], [], [], []
    return ride.arrays, ride.specs, ride.out_shape, ride.specs, ride.scratch


def _riding(body, n_in, n_out, n_scratch, ride, first, last):
    if ride is None:
        return body
    n = ride.n
    i1, i2 = n_in + n, n_in + n + n_out
    i3, i4 = i2 + n, i2 + n + n_scratch

    def wrapped(*refs):
        ins, outs, sems = refs[n_in:i1], refs[i2:i3], refs[i4:]

        @pl.when(first())
        def _():
            ride.start(ins, outs, sems)

        body(*refs[:n_in], *refs[i1:i2], *refs[i3:i4])

        @pl.when(last())
        def _():
            ride.wait(ins, outs, sems)

    return wrapped


def _gather_two_level(arrays, name):
    n = len(arrays)
    out_shape = [jax.ShapeDtypeStruct((N_DEV,) + a.shape, a.dtype) for a in arrays]

    def body(*refs):
        ins, outs = refs[:n], refs[n:2 * n]
        send_sems, recv_sems, local_sems = refs[2 * n:]
        x, y, c, me = _position()
        sibling = (x, y, 1 - c)
        chips = [(1 - x, y), (x, 1 - y), (1 - x, 1 - y)]
        block = lambda px, py, pc: 4 * px + 2 * py + pc

        def copy(a, k, blk, to, src=None):
            dst = outs[a].at[blk]
            return pltpu.make_async_remote_copy(
                src_ref=dst if src is None else src, dst_ref=dst, send_sem=send_sems.at[a, k], recv_sem=recv_sems.at[a, k],
                device_id=to, device_id_type=pl.DeviceIdType.MESH)

        local = [pltpu.make_async_copy(ins[a], outs[a].at[me], local_sems.at[a]) for a in range(n)]
        first = []
        for a in range(n):
            first.append(copy(a, 0, me, sibling, src=ins[a]))
            first += [copy(a, 1 + j, me, (*chip, c), src=ins[a]) for j, chip in enumerate(chips)]
        for cp in local + first:
            cp.start()
        passed = []
        for j, chip in enumerate(chips):
            for a in range(n):
                copy(a, 1 + j, block(*chip, c), sibling).wait_recv()
                passed.append(copy(a, 4 + j, block(*chip, c), sibling))
                passed[-1].start()
        for a in range(n):
            copy(a, 0, block(x, y, 1 - c), sibling).wait_recv()
            for j, chip in enumerate(chips):
                copy(a, 4 + j, block(*chip, 1 - c), sibling).wait_recv()
        for cp in first + passed:
            cp.wait_send()
        for cp in local:
            cp.wait()

    any_spec = [pl.BlockSpec(memory_space=pl.ANY)] * n
    return pl.pallas_call(
        body, name=name, in_specs=any_spec, out_specs=any_spec, out_shape=out_shape,
        scratch_shapes=[pltpu.SemaphoreType.DMA((n, N_DEV - 1)), pltpu.SemaphoreType.DMA((n, N_DEV - 1)),
                        pltpu.SemaphoreType.DMA((n,))],
    )(*arrays)


def _adamw_math(wv, g, m, v):
    m = ADAM_B1 * m + (1.0 - ADAM_B1) * g
    v = ADAM_B2 * v + (1.0 - ADAM_B2) * (g * g)
    m_hat = m / (1.0 - ADAM_B1 ** ADAM_STEP)
    v_hat = v / (1.0 - ADAM_B2 ** ADAM_STEP)
    delta = -ADAM_LR * (m_hat / (jnp.sqrt(v_hat) + ADAM_EPS) + ADAM_WD * wv)
    return delta, m, v


def _adamw(items, ride=None, max_rows=256):
    K = len(items)
    tiles, spans, start = [], [], 0
    for _, wv, _, _ in items:
        R = wv.shape[1]
        tr = max([t for t in range(16, max_rows + 1, 16) if R % t == 0] or [R])
        tiles.append(tr)
        spans.append((start, R // tr))
        start += R // tr
    total = start
    r_args, r_in, r_shape, r_out, r_scratch = _ride_parts(ride)

    def body(*refs):
        i = pl.program_id(0)
        for k, (first_step, n_steps) in enumerate(spans):
            def update(k=k):
                p_ref, w_ref, m_ref, v_ref = refs[4 * k:4 * k + 4]
                g_ref, d_ref, mo_ref, vo_ref = refs[4 * K + 4 * k:4 * K + 4 * k + 4]
                g = p_ref[0].astype(F32)
                for j in range(1, N_DEV):
                    g = g + p_ref[j].astype(F32)
                d, mn, vn = _adamw_math(w_ref[0], g, m_ref[0], v_ref[0])
                g_ref[0] = g
                d_ref[0] = d
                mo_ref[0] = mn
                vo_ref[0] = vn

            pl.when((i >= first_step) & (i < first_step + n_steps))(update)

    in_specs, out_specs, out_shape, args = [], [], [], []
    for (parts, wv, m, v), tr, (first_step, n_steps) in zip(items, tiles, spans):
        C = wv.shape[2]
        tile = lambda i, s=first_step, n=n_steps: (0, jnp.clip(i - s, 0, n - 1), 0)
        blk = pl.BlockSpec((1, tr, C), tile)
        in_specs += [pl.BlockSpec((N_DEV, tr, C), tile), blk, blk, blk]
        out_specs += [blk] * 4
        out_shape += [jax.ShapeDtypeStruct(wv.shape, F32)] * 4
        args += [parts, wv, m, v]
    outs = pl.pallas_call(
        _riding(body, 4 * K, 4 * K, 0, ride, lambda: pl.program_id(0) == 0, lambda: pl.program_id(0) == total - 1),
        name="adamw", grid=(total,),
        in_specs=in_specs + r_in, out_specs=out_specs + r_out, out_shape=out_shape + r_shape,
        scratch_shapes=r_scratch, compiler_params=_params(1),
    )(*args, *r_args)
    return [outs[4 * k:4 * k + 4] for k in range(K)], outs[4 * K:]


_TRANSPOSED = ("ffn1_w_gate", "ffn1_w_up", "ffn2_w_gate", "ffn2_w_up", "w_in", "mla_w_q_b")
_GROUPS = {"ffn1": ("ffn1_w_gate", "ffn1_w_up", "ffn1_w_down"),
           "ffn2": ("ffn2_w_gate", "ffn2_w_up", "ffn2_w_down"),
           "attn": ("w_in", "mla_w_q_b", "mla_w_kv_b", "w_out")}
_SMALL = ("ffn1_norm", "mix_norm", "ffn2_norm", "out_norm_dil", "out_norm_mla", "mla_q_a_norm", "rel_bias",
          "mla_q_norm", "mla_k_norm", "mla_kv_a_norm", "dil_q_norm", "dil_k_norm")
_SMALL_ROWS = 48


def _cols_to_full(g):
    return g.transpose(1, 0, 2).reshape(g.shape[1], N_DEV * g.shape[2])


def _full_to_cols(f):
    return f.reshape(f.shape[0], N_DEV, f.shape[1] // N_DEV).transpose(1, 0, 2)


def _shard_view(name, a):
    return jnp.swapaxes(a, 1, 2) if name in _TRANSPOSED else a


def _to_full(name, g):
    if name == "mla_w_kv_b":
        return _cols_to_full(g)
    f = g.reshape(-1, g.shape[-1])
    if name == "w_in":
        f = jnp.pad(f, ((0, PROJ_PAD - PROJ_COLS), (0, 0)))
    if name == "mla_w_q_b":
        f = jnp.pad(f.reshape(MLA_HEADS, MLA_QK, -1), ((0, 0), (0, MLA_PAD - MLA_QK), (0, 0)))
        f = f.reshape(MLA_HEADS * MLA_PAD, -1)
    return f


def _to_parts(name, f):
    if name == "mla_w_kv_b":
        return _full_to_cols(f).astype(BF16)
    if name == "w_in":
        f = f[:PROJ_COLS]
    if name == "mla_w_q_b":
        f = f.reshape(MLA_HEADS, MLA_PAD, -1)[:, :MLA_QK].reshape(MLA_HEADS * MLA_QK, -1)
    return f.reshape(N_DEV, -1, f.shape[-1]).astype(BF16)


class _Comm:
    def __init__(self, shards):
        self.shards, self.w, self.recv = shards, {}, {}

    def gather(self, names):
        return _Ride([self.shards[n] for n in names], [False] * len(names))

    def scatter(self, names, grads):
        return _Ride([_to_parts(n, grads[n]) for n in names], [True] * len(names))

    def weights_landed(self, names, got):
        self.w.update({n: _to_full(n, g) for n, g in zip(names, got)})

    def grads_landed(self, names, got):
        self.recv.update(zip(names, got))


def _pack_small(parts, extra):
    flat = jnp.concatenate([parts[n].reshape(-1) for n in _SMALL] + [extra.reshape(-1)])
    return jnp.pad(flat, (0, _SMALL_ROWS * 128 - flat.shape[0])).reshape(_SMALL_ROWS, 128)


def _unpack_small(packed, shapes):
    flat, out, off = packed.reshape(-1), {}, 0
    for n in _SMALL:
        size = math.prod(shapes[n])
        out[n] = flat[off:off + size].reshape(shapes[n])
        off += size
    return out, flat[off]


_NAMES = ("ffn1_norm", "ffn1_w_gate", "ffn1_w_up", "ffn1_w_down", "mix_norm", "w_in", "dil_q_norm", "dil_k_norm",
          "rel_bias", "mla_q_a_norm", "mla_w_q_b", "mla_kv_a_norm", "mla_w_kv_b", "mla_q_norm", "mla_k_norm",
          "out_norm_dil", "out_norm_mla", "w_out", "ffn2_norm", "ffn2_w_gate", "ffn2_w_up", "ffn2_w_down")


def kernel(x, ffn1_norm, ffn1_w_gate, ffn1_w_up, ffn1_w_down, mix_norm, w_in, dil_q_norm, dil_k_norm, rel_bias, mla_q_a_norm, mla_w_q_b, mla_kv_a_norm, mla_w_kv_b, mla_q_norm, mla_k_norm, out_norm_dil, out_norm_mla, w_out, ffn2_norm, ffn2_w_gate, ffn2_w_up, ffn2_w_down, loss_target, m_ffn1_norm, m_ffn1_w_gate, m_ffn1_w_up, m_ffn1_w_down, m_mix_norm, m_w_in, m_dil_q_norm, m_dil_k_norm, m_rel_bias, m_mla_q_a_norm, m_mla_w_q_b, m_mla_kv_a_norm, m_mla_w_kv_b, m_mla_q_norm, m_mla_k_norm, m_out_norm_dil, m_out_norm_mla, m_w_out, m_ffn2_norm, m_ffn2_w_gate, m_ffn2_w_up, m_ffn2_w_down, v_ffn1_norm, v_ffn1_w_gate, v_ffn1_w_up, v_ffn1_w_down, v_mix_norm, v_w_in, v_dil_q_norm, v_dil_k_norm, v_rel_bias, v_mla_q_a_norm, v_mla_w_q_b, v_mla_kv_a_norm, v_mla_w_kv_b, v_mla_q_norm, v_mla_k_norm, v_out_norm_dil, v_out_norm_mla, v_w_out, v_ffn2_norm, v_ffn2_w_gate, v_ffn2_w_up, v_ffn2_w_down):
    args = locals()
    wts = {n: args[n] for n in _NAMES}
    mom = {n: args["m_" + n] for n in _NAMES}
    var = {n: args["v_" + n] for n in _NAMES}

    matrices = [n for group in _GROUPS.values() for n in group]
    comm = _Comm({n: _shard_view(n, wts[n])[0].astype(BF16) for n in matrices})
    comm.weights_landed(_GROUPS["ffn1"], _gather_two_level(comm.gather(_GROUPS["ffn1"]).arrays, "gather_first"))
    small = {n: wts[n].reshape(1, -1) if n != "rel_bias" else wts[n] for n in _SMALL}

    loss, grad_x, gw, gs = _local_step(x[0], loss_target[0], small, comm)

    item = lambda n: (comm.recv[n],) + tuple(_shard_view(n, a[n]) for a in (wts, mom, var))
    landed = [n for n in matrices if n != "ffn1_w_up"]
    last = comm.scatter(("ffn1_w_up",), gw)
    updates, got = _adamw([item(n) for n in landed], max_rows=32,
                          ride=_Ride(last.arrays + [_pack_small(gs, loss[0, 0])], last.scatter + [False]))
    comm.grads_landed(("ffn1_w_up",), got[:-1])

    zero = jnp.zeros((), F32)
    small_item = (got[-1],) + tuple(_pack_small(a, zero)[None] for a in (wts, mom, var))
    (up_update, packed), _ = _adamw([item("ffn1_w_up"), small_item])
    res = {n: [_shard_view(n, r) for r in u] for n, u in zip(landed + ["ffn1_w_up"], updates + [up_update])}
    shapes = {n: wts[n].shape for n in _SMALL}
    loss_total = None
    for slot, q in enumerate(packed):
        vals, extra = _unpack_small(q, shapes)
        if slot == 0:
            loss_total = extra
        for n in _SMALL:
            res.setdefault(n, [None] * 4)[slot] = vals[n]
    outs = [loss_total, grad_x[None]]
    for slot in range(4):
        outs += [res[n][slot].reshape(wts[n].shape) for n in _NAMES]
    return tuple(outs)
```

```python
import math

import numpy as np
import jax
import jax.numpy as jnp
from jax import lax
from jax.experimental import pallas as pl
from jax.experimental.pallas import tpu as pltpu

F32, BF16 = jnp.float32, jnp.bfloat16
EPS = 1e-6
NEG = -1e30
N_DEV = 8

DIL_HEADS, DIL_HD = 8, 64
DIL_WIDTH = DIL_HEADS * DIL_HD
DIL_BRANCHES = ((128, 1), (512, 4), (2048, 16))
DIL_BLOCK = 128
MLA_HEADS, MLA_NOPE, MLA_ROPE, MLA_V = 4, 128, 64, 128
MLA_QK = MLA_NOPE + MLA_ROPE
MLA_PAD = 256
ROPE_BASE = 10000.0
REL_BUCKETS, REL_MAX_DIST = 32, 2048
MLA_Q_RANK, MLA_KV_RANK = 256, 128
PROJ_COLS, PROJ_PAD = 1984, 2048
CQ_COL = 3 * DIL_WIDTH
CKV_COL, KPE_COL = CQ_COL + MLA_Q_RANK, CQ_COL + MLA_Q_RANK + MLA_KV_RANK
FFN_RESID = 0.5
ADAM_LR, ADAM_B1, ADAM_B2, ADAM_EPS, ADAM_WD, ADAM_STEP = 0.001, 0.9, 0.999, 1e-08, 0.01, 10
VMEM_LIMIT = 62 * 1024 * 1024

_NT = (((1,), (1,)), ((), ()))
_TN = (((0,), (0,)), ((), ()))


def _dot(a, b):
    return jnp.dot(a, b, preferred_element_type=F32)


def _dot_nt(a, b):
    return lax.dot_general(a, b, _NT, preferred_element_type=F32)


def _dot_tn(a, b):
    return lax.dot_general(a, b, _TN, preferred_element_type=F32)


def _params(n_axes):
    return pltpu.CompilerParams(dimension_semantics=("arbitrary",) * n_axes, vmem_limit_bytes=VMEM_LIMIT)


def _rstd(x, n=None):
    n = x.shape[-1] if n is None else n
    return lax.rsqrt(jnp.sum(x * x, axis=-1, keepdims=True) / n + EPS)


def _rms_bwd(dy, x, g, r, n=None):
    n = x.shape[-1] if n is None else n
    u = dy * g
    dx = r * u - x * (r * r * r) * (jnp.sum(u * x, axis=-1, keepdims=True) / n)
    return dx, dy * x * r


def _sigmoid(x):
    return 1.0 / (1.0 + jnp.exp(-x))


def _split3(x):
    parts = []
    for _ in range(3):
        xb = x.astype(BF16)
        parts.append(xb)
        x = x - xb.astype(F32)
    return parts


def _ffn_fwd(x, gain, wg, wu, wd, ride=None, target=None, tm=512, tf=2816):
    T, D = x.shape
    F = wg.shape[0]
    ni, nj = T // tm, F // tf
    with_loss = target is not None
    r_args, r_in, r_shape, r_out, r_scratch = _ride_parts(ride)

    def body(*refs):
        x_ref, g_ref, wg_ref, wu_ref, wd_ref = refs[:5]
        t_ref = refs[5] if with_loss else None
        xo_ref, h_ref, gate_ref, up_ref = refs[5 + with_loss:9 + with_loss]
        loss_ref = refs[-2] if with_loss else None
        acc = refs[-1]
        i, j = pl.program_id(0), pl.program_id(1)

        @pl.when(j == 0)
        def _():
            xv = x_ref[...]
            h_ref[...] = (xv * _rstd(xv) * g_ref[...]).astype(BF16)
            acc[...] = jnp.zeros_like(acc)

        h = h_ref[...]
        g = _dot_nt(h, wg_ref[...])
        u = _dot_nt(h, wu_ref[...])
        gate_ref[...] = g.astype(BF16)
        up_ref[...] = u.astype(BF16)
        a = (g * _sigmoid(g) * u).astype(BF16)
        acc[...] += _dot(a, wd_ref[...])

        @pl.when(j == nj - 1)
        def _():
            y = x_ref[...] + FFN_RESID * acc[...]
            if with_loss:
                @pl.when(i == 0)
                def _():
                    loss_ref[...] = jnp.zeros_like(loss_ref)

                e = y - t_ref[...]
                xo_ref[...] = e * (1.0 / D)
                loss_ref[...] += (0.5 / D) * jnp.sum(e * e)
            else:
                xo_ref[...] = y

    row = lambda i, j: (i, 0)
    tile = lambda i, j: (i, j)
    n_in, n_out = 5 + with_loss, 4 + with_loss
    first = lambda: (pl.program_id(0) == 0) & (pl.program_id(1) == 0)
    last = lambda: (pl.program_id(0) == ni - 1) & (pl.program_id(1) == nj - 1)
    outs = pl.pallas_call(
        _riding(body, n_in, n_out, 1, ride, first, last), name="ffn_fwd", grid=(ni, nj),
        in_specs=[pl.BlockSpec((tm, D), row), pl.BlockSpec((1, D), lambda i, j: (0, 0)),
                  pl.BlockSpec((tf, D), lambda i, j: (j, 0)), pl.BlockSpec((tf, D), lambda i, j: (j, 0)),
                  pl.BlockSpec((tf, D), lambda i, j: (j, 0))] + [pl.BlockSpec((tm, D), row)] * with_loss + r_in,
        out_specs=[pl.BlockSpec((tm, D), row), pl.BlockSpec((tm, D), row), pl.BlockSpec((tm, tf), tile),
                   pl.BlockSpec((tm, tf), tile)] + [pl.BlockSpec((1, 128), lambda i, j: (0, 0))] * with_loss + r_out,
        out_shape=[jax.ShapeDtypeStruct((T, D), F32), jax.ShapeDtypeStruct((T, D), BF16),
                   jax.ShapeDtypeStruct((T, F), BF16), jax.ShapeDtypeStruct((T, F), BF16)]
        + [jax.ShapeDtypeStruct((1, 128), F32)] * with_loss + r_shape,
        scratch_shapes=[pltpu.VMEM((tm, D), F32)] + r_scratch,
        compiler_params=_params(2),
    )(x, gain, wg, wu, wd, *([target] if with_loss else []), *r_args)
    return outs[:n_out], outs[n_out:]


def _ffn_bwd(dy, x, gain, gate, up, wg, wu, wd, tm=256, tf=2816):
    T, D = x.shape
    F = wg.shape[0]
    ni, nj = T // tm, F // tf

    def body(dy_ref, x_ref, g_ref, gate_ref, up_ref, wg_ref, wu_ref, wd_ref,
             dx_ref, a_ref, dg_ref, du_ref, dyh_ref, dgain_ref, acc):
        i, j = pl.program_id(0), pl.program_id(1)

        @pl.when((i == 0) & (j == 0))
        def _():
            dgain_ref[...] = jnp.zeros_like(dgain_ref)

        @pl.when(j == 0)
        def _():
            dyh_ref[...] = (FFN_RESID * dy_ref[...]).astype(BF16)
            acc[...] = jnp.zeros_like(acc)

        da = _dot_nt(dyh_ref[...], wd_ref[...])
        g = gate_ref[...].astype(F32)
        u = up_ref[...].astype(F32)
        sig = _sigmoid(g)
        s = g * sig
        a_ref[...] = (s * u).astype(BF16)
        dg = (da * u * (sig * (1.0 + g * (1.0 - sig)))).astype(BF16)
        du = (da * s).astype(BF16)
        dg_ref[...] = dg
        du_ref[...] = du
        acc[...] += _dot(dg, wg_ref[...]) + _dot(du, wu_ref[...])

        @pl.when(j == nj - 1)
        def _():
            xv = x_ref[...]
            dxn, dgc = _rms_bwd(acc[...], xv, g_ref[...], _rstd(xv))
            dx_ref[...] = dy_ref[...] + dxn
            dgain_ref[...] += jnp.sum(dgc, axis=0, keepdims=True)

    return pl.pallas_call(
        body, name="ffn_bwd", grid=(ni, nj),
        in_specs=[pl.BlockSpec((tm, D), lambda i, j: (i, 0)), pl.BlockSpec((tm, D), lambda i, j: (i, 0)),
                  pl.BlockSpec((1, D), lambda i, j: (0, 0)),
                  pl.BlockSpec((tm, tf), lambda i, j: (i, j)), pl.BlockSpec((tm, tf), lambda i, j: (i, j)),
                  pl.BlockSpec((tf, D), lambda i, j: (j, 0)), pl.BlockSpec((tf, D), lambda i, j: (j, 0)),
                  pl.BlockSpec((tf, D), lambda i, j: (j, 0))],
        out_specs=[pl.BlockSpec((tm, D), lambda i, j: (i, 0)),
                   pl.BlockSpec((tm, tf), lambda i, j: (i, j)), pl.BlockSpec((tm, tf), lambda i, j: (i, j)),
                   pl.BlockSpec((tm, tf), lambda i, j: (i, j)),
                   pl.BlockSpec((tm, D), lambda i, j: (i, 0)), pl.BlockSpec((1, D), lambda i, j: (0, 0))],
        out_shape=[jax.ShapeDtypeStruct((T, D), F32), jax.ShapeDtypeStruct((T, F), BF16),
                   jax.ShapeDtypeStruct((T, F), BF16), jax.ShapeDtypeStruct((T, F), BF16),
                   jax.ShapeDtypeStruct((T, D), BF16), jax.ShapeDtypeStruct((1, D), F32)],
        scratch_shapes=[pltpu.VMEM((tm, D), F32)],
        compiler_params=_params(2),
    )(dy, x, gain, gate, up, wg, wu, wd)


def _matmul_tn(a, b, tk, tn, ride=None, tt=2048):
    T, K = a.shape
    N = b.shape[1]
    tk, tn = min(tk, K), min(tn, N)
    grid = (K // tk, N // tn, T // tt)
    r_args, r_in, r_shape, r_out, r_scratch = _ride_parts(ride)

    def body(a_ref, b_ref, o_ref, acc):
        t = pl.program_id(2)

        @pl.when(t == 0)
        def _():
            acc[...] = jnp.zeros_like(acc)

        acc[...] += _dot_tn(a_ref[...].astype(BF16), b_ref[...].astype(BF16))

        @pl.when(t == grid[2] - 1)
        def _():
            o_ref[...] = acc[...].astype(BF16)

    first = lambda: (pl.program_id(0) == 0) & (pl.program_id(1) == 0) & (pl.program_id(2) == 0)
    last = lambda: ((pl.program_id(0) == grid[0] - 1) & (pl.program_id(1) == grid[1] - 1)
                    & (pl.program_id(2) == grid[2] - 1))
    outs = pl.pallas_call(
        _riding(body, 2, 1, 1, ride, first, last), name="matmul_tn", grid=grid,
        in_specs=[pl.BlockSpec((tt, tk), lambda k, n, t: (t, k)), pl.BlockSpec((tt, tn), lambda k, n, t: (t, n))] + r_in,
        out_specs=[pl.BlockSpec((tk, tn), lambda k, n, t: (k, n))] + r_out,
        out_shape=[jax.ShapeDtypeStruct((K, N), BF16)] + r_shape,
        scratch_shapes=[pltpu.VMEM((tk, tn), F32)] + r_scratch,
        compiler_params=_params(3),
    )(a, b, *r_args)
    return outs[0], outs[1:]


def _in_proj(x, gain, w, gq, gk, tm=1024):
    T, D = x.shape
    N = w.shape[0]
    W = DIL_WIDTH

    def body(x_ref, g_ref, w_ref, gq_ref, gk_ref, h_ref, p_ref, qh_ref, kh_ref):
        xv = x_ref[...]
        h = (xv * _rstd(xv) * g_ref[...]).astype(BF16)
        h_ref[...] = h
        p_ref[...] = _dot_nt(h, w_ref[...])
        lo = lax.broadcasted_iota(jnp.int32, (tm, 128), 1) < DIL_HD
        for hp in range(DIL_HEADS // 2):
            q = p_ref[:, 128 * hp:128 * (hp + 1)]
            k = p_ref[:, W + 128 * hp:W + 128 * (hp + 1)]
            qh_ref[:, 128 * hp:128 * (hp + 1)] = (q * _pair_rstd(q, lo) * gq_ref[...]).astype(BF16).astype(F32)
            kh_ref[:, 128 * hp:128 * (hp + 1)] = (k * _pair_rstd(k, lo) * gk_ref[...]).astype(BF16).astype(F32)

    row = lambda i: (i, 0)
    fix = lambda i: (0, 0)
    return pl.pallas_call(
        body, name="in_proj", grid=(T // tm,),
        in_specs=[pl.BlockSpec((tm, D), row), pl.BlockSpec((1, D), fix), pl.BlockSpec((N, D), fix),
                  pl.BlockSpec((1, 128), fix), pl.BlockSpec((1, 128), fix)],
        out_specs=[pl.BlockSpec((tm, D), row), pl.BlockSpec((tm, N), row), pl.BlockSpec((tm, W), row),
                   pl.BlockSpec((tm, W), row)],
        out_shape=[jax.ShapeDtypeStruct((T, D), BF16), jax.ShapeDtypeStruct((T, N), F32),
                   jax.ShapeDtypeStruct((T, W), F32), jax.ShapeDtypeStruct((T, W), F32)],
        compiler_params=_params(1),
    )(x, gain, w, gq, gk)


def _in_proj_bwd(dx_up, x, gain, w, proj, gq, gk, dqkv, dcq, dckv, dkpe, ride=None, tm=512):
    T, D = x.shape
    N = w.shape[0]
    W = DIL_WIDTH
    nb = len(dqkv)

    def body(*refs):
        dxu_ref, x_ref, g_ref, w_ref, q_ref, k_ref, gq_ref, gk_ref = refs[:8]
        dil_refs = refs[8:8 + 3 * nb]
        dcq_ref, dckv_ref, dkpe_ref, dx_ref, dp_ref, dgain_ref, dgq_ref, dgk_ref = refs[8 + 3 * nb:]

        @pl.when(pl.program_id(0) == 0)
        def _():
            for ref in (dgain_ref, dgq_ref, dgk_ref):
                ref[...] = jnp.zeros_like(ref)

        lo = lax.broadcasted_iota(jnp.int32, (tm, 128), 1) < DIL_HD
        norms = ((q_ref, gq_ref, dgq_ref), (k_ref, gk_ref, dgk_ref))
        for part in range(3):
            acc = dil_refs[part][...]
            for b in range(1, nb):
                acc = acc + dil_refs[3 * b + part][...]
            if part == 2:
                dp_ref[:, 2 * W:3 * W] = acc.astype(BF16)
                continue
            raw_ref, gn_ref, dgn_ref = norms[part]
            for hp in range(DIL_HEADS // 2):
                raw = raw_ref[:, 128 * hp:128 * (hp + 1)]
                d_raw, dgn = _pair_rms_bwd(acc[:, 128 * hp:128 * (hp + 1)], raw, _pair_rstd(raw, lo), gn_ref[...], lo)
                dp_ref[:, part * W + 128 * hp:part * W + 128 * (hp + 1)] = d_raw.astype(BF16)
                dgn_ref[...] += dgn
        dp_ref[:, 3 * W:3 * W + 256] = dcq_ref[...].astype(BF16)
        dp_ref[:, 3 * W + 256:3 * W + 384] = dckv_ref[...].astype(BF16)
        dp_ref[:, 3 * W + 384:N] = dkpe_ref[...].astype(BF16)
        dh = _dot(dp_ref[...], w_ref[...])
        xv = x_ref[...]
        dxn, dgc = _rms_bwd(dh, xv, g_ref[...], _rstd(xv))
        dx_ref[...] = dxu_ref[...] + dxn
        dgain_ref[...] += jnp.sum(dgc, axis=0, keepdims=True)

    row = lambda i: (i, 0)
    fix = lambda i: (0, 0)
    r_args, r_in, r_shape, r_out, r_scratch = _ride_parts(ride)
    first = lambda: pl.program_id(0) == 0
    last = lambda: pl.program_id(0) == T // tm - 1
    outs = pl.pallas_call(
        _riding(body, 11 + 3 * nb, 5, 0, ride, first, last), name="in_proj_bwd", grid=(T // tm,),
        in_specs=[pl.BlockSpec((tm, D), row), pl.BlockSpec((tm, D), row), pl.BlockSpec((1, D), fix),
                  pl.BlockSpec((N, D), fix), pl.BlockSpec((tm, W), row), pl.BlockSpec((tm, W), lambda i: (i, 1)),
                  pl.BlockSpec((1, 128), fix), pl.BlockSpec((1, 128), fix)] + [pl.BlockSpec((tm, W), row)] * (3 * nb)
                 + [pl.BlockSpec((tm, 256), row), pl.BlockSpec((tm, 128), row), pl.BlockSpec((tm, 128), row)] + r_in,
        out_specs=[pl.BlockSpec((tm, D), row), pl.BlockSpec((tm, N), row), pl.BlockSpec((1, D), fix),
                   pl.BlockSpec((1, 128), fix), pl.BlockSpec((1, 128), fix)] + r_out,
        out_shape=[jax.ShapeDtypeStruct((T, D), F32), jax.ShapeDtypeStruct((T, N), BF16),
                   jax.ShapeDtypeStruct((1, D), F32), jax.ShapeDtypeStruct((1, 128), F32),
                   jax.ShapeDtypeStruct((1, 128), F32)] + r_shape,
        scratch_shapes=r_scratch,
        compiler_params=_params(1),
    )(dx_up, x, gain, w, proj, proj, gq, gk, *[a for triple in dqkv for a in triple], dcq, dckv, dkpe, *r_args)
    return outs[:5], outs[5:]


def _out_proj(x, o_dil, o_mla, g_dil, g_mla, w, tm=1024):
    T, D = x.shape
    W = o_dil.shape[1]

    def body(x_ref, od_ref, om_ref, gd_ref, gm_ref, w_ref, xo_ref, oc_ref):
        od, om = od_ref[...], om_ref[...]
        oc_ref[:, 0:W] = (od * _rstd(od) * gd_ref[...]).astype(BF16)
        oc_ref[:, W:2 * W] = (om * _rstd(om) * gm_ref[...]).astype(BF16)
        xo_ref[...] = x_ref[...] + _dot(oc_ref[...], w_ref[...])

    row = lambda i: (i, 0)
    fix = lambda i: (0, 0)
    return pl.pallas_call(
        body, name="out_proj", grid=(T // tm,),
        in_specs=[pl.BlockSpec((tm, D), row), pl.BlockSpec((tm, W), row), pl.BlockSpec((tm, W), row),
                  pl.BlockSpec((1, W), fix), pl.BlockSpec((1, W), fix), pl.BlockSpec((2 * W, D), fix)],
        out_specs=[pl.BlockSpec((tm, D), row), pl.BlockSpec((tm, 2 * W), row)],
        out_shape=[jax.ShapeDtypeStruct((T, D), F32), jax.ShapeDtypeStruct((T, 2 * W), BF16)],
        compiler_params=_params(1),
    )(x, o_dil, o_mla, g_dil, g_mla, w)


def _out_proj_bwd(dx, o_dil, o_mla, g_dil, g_mla, w, tm=1024):
    T, D = dx.shape
    W = o_dil.shape[1]

    def body(dx_ref, od_ref, om_ref, gd_ref, gm_ref, w_ref, dod_ref, dom_ref, dgd_ref, dgm_ref):
        @pl.when(pl.program_id(0) == 0)
        def _():
            dgd_ref[...] = jnp.zeros_like(dgd_ref)
            dgm_ref[...] = jnp.zeros_like(dgm_ref)

        doc = _dot_nt(dx_ref[...].astype(BF16), w_ref[...])
        od, om = od_ref[...], om_ref[...]
        dod, dgd = _rms_bwd(doc[:, 0:W], od, gd_ref[...], _rstd(od))
        dom, dgm = _rms_bwd(doc[:, W:2 * W], om, gm_ref[...], _rstd(om))
        dod_ref[...] = dod
        dom_ref[...] = dom
        dgd_ref[...] += jnp.sum(dgd, axis=0, keepdims=True)
        dgm_ref[...] += jnp.sum(dgm, axis=0, keepdims=True)

    row = lambda i: (i, 0)
    fix = lambda i: (0, 0)
    return pl.pallas_call(
        body, name="out_proj_bwd", grid=(T // tm,),
        in_specs=[pl.BlockSpec((tm, D), row), pl.BlockSpec((tm, W), row), pl.BlockSpec((tm, W), row),
                  pl.BlockSpec((1, W), fix), pl.BlockSpec((1, W), fix), pl.BlockSpec((2 * W, D), fix)],
        out_specs=[pl.BlockSpec((tm, W), row), pl.BlockSpec((tm, W), row),
                   pl.BlockSpec((1, W), fix), pl.BlockSpec((1, W), fix)],
        out_shape=[jax.ShapeDtypeStruct((T, W), F32), jax.ShapeDtypeStruct((T, W), F32),
                   jax.ShapeDtypeStruct((1, W), F32), jax.ShapeDtypeStruct((1, W), F32)],
        compiler_params=_params(1),
    )(dx, o_dil, o_mla, g_dil, g_mla, w)


def _pair_rstd(x, lo):
    sq = x * x
    s0 = jnp.sum(jnp.where(lo, sq, 0.0), axis=-1, keepdims=True)
    s1 = jnp.sum(jnp.where(lo, 0.0, sq), axis=-1, keepdims=True)
    return jnp.where(lo, lax.rsqrt(s0 / DIL_HD + EPS), lax.rsqrt(s1 / DIL_HD + EPS))


def _pair_rms_bwd(dn, x, r, g, lo):
    u = dn * g
    t = u * x
    d0 = jnp.sum(jnp.where(lo, t, 0.0), axis=-1, keepdims=True)
    d1 = jnp.sum(jnp.where(lo, 0.0, t), axis=-1, keepdims=True)
    dx = r * u - x * (r * r * r) * (jnp.where(lo, d0, d1) / DIL_HD)
    return dx, jnp.sum(dn * x * r, axis=0, keepdims=True)


def _pair_col(x, lo, e):
    sel = lo if e == 0 else jnp.logical_not(lo)
    return jnp.max(jnp.where(sel, x, NEG), axis=-1, keepdims=True)


def _first_head_lanes():
    return lax.broadcasted_iota(jnp.int32, (DIL_BLOCK, DIL_BLOCK), 1) < DIL_HD


def _window_masks():
    i = np.arange(DIL_BLOCK)[:, None]
    j = np.arange(DIL_BLOCK)[None, :]
    cur = j <= i
    both = np.concatenate([j >= i, cur], axis=1)
    first = np.concatenate([np.zeros_like(cur), cur], axis=1)
    return jnp.asarray(np.where(np.stack([both, first]), 0.0, NEG).reshape(2, -1), F32)


def _stack_heads(x, lo):
    return jnp.concatenate([jnp.where(lo, x, 0.0), jnp.where(lo, 0.0, x)], axis=0)


def _unstack_heads(x2, lo):
    return jnp.where(lo, x2[:DIL_BLOCK], x2[DIL_BLOCK:])


def _dil_pairs(d):
    return 4 if d == 1 else 1


def _sub_rows(r, d):
    return pl.ds(r, DIL_BLOCK, stride=d) if d > 1 else pl.ds(0, DIL_BLOCK)


def _store_piece(scratch, i, part, piece):
    if part is None:
        scratch[i] = piece
    else:
        scratch[i, pl.ds(DIL_BLOCK * part, DIL_BLOCK), :] = piece


def _split_subsequences(loads, d, P, stage=None):
    if d == 16:
        group = 4 * DIL_BLOCK
        for block, scratch, part in loads:
            for a in range(4):
                stage[pl.ds(a * group, group), :] = block[pl.ds(a, group, stride=4), :]
            for a in range(4):
                for b in range(4):
                    _store_piece(scratch, a + 4 * b, part, stage[pl.ds(a * group + b, DIL_BLOCK, stride=4), :])
        return
    for r in range(d):
        for p in range(P):
            for block, scratch, part in loads:
                _store_piece(scratch, r * P + p, part, block[_sub_rows(r, d), pl.ds(128 * p, 128)])


def _keep_previous_block(scratches, n):
    for scratch in scratches:
        @pl.when(n == 0)
        def _():
            scratch[:, pl.ds(0, DIL_BLOCK), :] = jnp.zeros((scratch.shape[0], DIL_BLOCK, 128), F32)

        @pl.when(n > 0)
        def _():
            scratch[:, pl.ds(0, DIL_BLOCK), :] = scratch[:, pl.ds(DIL_BLOCK, DIL_BLOCK), :]


def _merge_subsequences(stores, d, P, stage=None):
    if d == 16:
        group = 4 * DIL_BLOCK
        for block, scratch, plus in stores:
            for a in range(4):
                for b in range(4):
                    stage[pl.ds(a * group + b, DIL_BLOCK, stride=4), :] = scratch[a + 4 * b]
            for a in range(4):
                rows = pl.ds(a, group, stride=4)
                val = stage[pl.ds(a * group, group), :]
                block[rows, :] = val if plus is None else val + plus[rows, :]
        return
    for r in range(d):
        for p in range(P):
            for block, scratch, plus in stores:
                part = _sub_rows(r, d), pl.ds(128 * p, 128)
                block[part] = scratch[r * P + p] if plus is None else scratch[r * P + p] + plus[part]


def _dil_fwd(qh, kh, proj, bias, d, prev):
    T = proj.shape[0]
    P = _dil_pairs(d)
    rows, cw, n_it = DIL_BLOCK * d, 128 * P, d * P
    nblk = T // rows
    has_prev = prev is not None

    def body(*refs):
        q_ref, kc_ref, vc_ref, bias_ref = refs[:4]
        refs = refs[4:]
        if has_prev:
            oin_ref, lin_ref = refs[:2]
            refs = refs[2:]
        o_ref, l_ref, stage, qs, ks, vs, os_, ls_ = refs[:8]
        pb, n = pl.program_id(0), pl.program_id(1)
        lo = _first_head_lanes()
        first = (n == 0).astype(jnp.int32)
        _keep_previous_block((ks, vs), n)
        loads = [(q_ref, qs, None), (kc_ref, ks, 1), (vc_ref, vs, 1)]
        if has_prev:
            ois, lis = refs[8:]
            loads += [(oin_ref, ois, None), (lin_ref, lis, None)]
        _split_subsequences(loads, d, P, stage)

        def step(i, carry):
            q2 = _stack_heads(qs[i], lo).astype(BF16)
            s = _dot_nt(q2, ks[i].astype(BF16)) + bias_ref[first, pb * P + i % P]
            m = jnp.max(s, axis=-1, keepdims=True)
            p = jnp.exp(s - m)
            l = jnp.sum(p, axis=-1, keepdims=True)
            o = _unstack_heads(_dot(p.astype(BF16), vs[i].astype(BF16)) / l, lo)
            lse = _unstack_heads(jnp.broadcast_to(m + jnp.log(l), (2 * DIL_BLOCK, 128)), lo)
            if has_prev:
                lin = lis[i]
                mx = jnp.maximum(lin, lse)
                lnew = mx + jnp.log(jnp.exp(lin - mx) + jnp.exp(lse - mx))
                o = ois[i] * jnp.exp(lin - lnew) + o * jnp.exp(lse - lnew)
                lse = lnew
            os_[i] = o
            ls_[i] = lse
            return carry

        lax.fori_loop(0, n_it, step, 0, unroll=min(n_it, 8))
        _merge_subsequences([(o_ref, os_, None), (l_ref, ls_, None)], d, P, stage)

    blk = (rows, cw)
    vcol = 2 * DIL_WIDTH // cw
    tok = pl.BlockSpec(blk, lambda pb, n: (n, pb))
    in_specs = [tok, tok, pl.BlockSpec(blk, lambda pb, n: (n, vcol + pb)),
                pl.BlockSpec(bias.shape, lambda pb, n: (0, 0, 0, 0))]
    args = [qh, kh, proj, bias]
    one, two = pltpu.VMEM((n_it, DIL_BLOCK, 128), F32), pltpu.VMEM((n_it, 2 * DIL_BLOCK, 128), F32)
    scratch = [pltpu.VMEM((rows, 128), F32), one, two, two, one, one]
    if has_prev:
        in_specs += [tok, tok]
        args += list(prev)
        scratch += [one, one]
    out = jax.ShapeDtypeStruct((T, DIL_WIDTH), F32)
    return pl.pallas_call(
        body, name=f"dil_fwd_d{d}", grid=(DIL_HEADS // 2 // P, nblk), in_specs=in_specs, out_specs=[tok, tok],
        out_shape=[out, out], scratch_shapes=scratch, compiler_params=_params(2),
    )(*args)


def _dil_bwd(qh, kh, proj, o, lse, do, bias, d, prev):
    T = proj.shape[0]
    P = _dil_pairs(d)
    rows, cw, n_it = DIL_BLOCK * d, 128 * P, d * P
    nblk = T // rows
    has_prev = prev is not None

    def body(*refs):
        q_ref, kc_ref, vc_ref, o_ref, l_ref, do_ref, bias_ref = refs[:7]
        dqi_ref, dki_ref, dvi_ref = refs[7:10] if has_prev else (None, None, None)
        dq_ref, dk_ref, dv_ref, db_ref, stage, qs, ks, vs, os_, ls_, dos, dqs, dks, dvs, ck, cv = refs[7 + 3 * has_prev:]
        pb, n = pl.program_id(0), pl.program_id(1)
        lo = _first_head_lanes()
        first = (n == 0).astype(jnp.int32)

        @pl.when((pb == 0) & (n == 0))
        def _():
            db_ref[...] = jnp.zeros_like(db_ref)

        @pl.when(n == 0)
        def _():
            ck[...] = jnp.zeros_like(ck)
            cv[...] = jnp.zeros_like(cv)

        _keep_previous_block((ks, vs), n)
        _split_subsequences([(q_ref, qs, None), (kc_ref, ks, 1), (vc_ref, vs, 1),
                             (o_ref, os_, None), (l_ref, ls_, None), (do_ref, dos, None)], d, P, stage)

        def step(i, carry):
            pair = pb * P + i % P
            q2 = _stack_heads(qs[i], lo).astype(BF16)
            kcat, vcat = ks[i].astype(BF16), vs[i].astype(BF16)
            dov = dos[i]
            do2 = _stack_heads(dov, lo).astype(BF16)
            delta = jnp.sum(_stack_heads(dov * os_[i], lo), axis=-1, keepdims=True)
            lse_pair = ls_[i]
            lse2 = jnp.concatenate([_pair_col(lse_pair, lo, 0), _pair_col(lse_pair, lo, 1)], axis=0)
            s = _dot_nt(q2, kcat) + bias_ref[first, pair]
            p = jnp.exp(s - lse2)
            ds = p * (_dot_nt(do2, vcat) - delta)
            db_ref[pair] += ds
            dsb = ds.astype(BF16)
            dqs[i] = _unstack_heads(_dot(dsb, kcat), lo)
            dk2 = _dot_tn(dsb, q2)
            dv2 = _dot_tn(p.astype(BF16), do2)
            dks[i] = ck[i] + dk2[:DIL_BLOCK]
            dvs[i] = cv[i] + dv2[:DIL_BLOCK]
            ck[i] = dk2[DIL_BLOCK:]
            cv[i] = dv2[DIL_BLOCK:]
            return carry

        @pl.when(n < nblk)
        def _():
            lax.fori_loop(0, n_it, step, 0, unroll=min(n_it, 8))
            _merge_subsequences([(dq_ref, dqs, dqi_ref), (dk_ref, dks, dki_ref), (dv_ref, dvs, dvi_ref)], d, P, stage)

        @pl.when(n == nblk)
        def _():
            _merge_subsequences([(dk_ref, ck, dki_ref), (dv_ref, cv, dvi_ref)], d, P, stage)

    blk = (rows, cw)
    vcol = 2 * DIL_WIDTH // cw
    qn_ = lambda n: jnp.minimum(n, nblk - 1)
    pn_ = lambda n: jnp.maximum(n - 1, 0)
    fix3 = lambda pb, n: (0, 0, 0)
    tok_q = pl.BlockSpec(blk, lambda pb, n: (qn_(n), pb))
    tok_p = pl.BlockSpec(blk, lambda pb, n: (pn_(n), pb))
    in_specs = [tok_q, tok_q, pl.BlockSpec(blk, lambda pb, n: (qn_(n), vcol + pb)), tok_q, tok_q, tok_q,
                pl.BlockSpec(bias.shape, lambda pb, n: (0, 0, 0, 0))]
    in_specs += [tok_q, tok_p, tok_p] if has_prev else []
    tok_shape = jax.ShapeDtypeStruct((T, DIL_WIDTH), F32)
    one, two = pltpu.VMEM((n_it, DIL_BLOCK, 128), F32), pltpu.VMEM((n_it, 2 * DIL_BLOCK, 128), F32)
    dq, dk, dv, db = pl.pallas_call(
        body, name=f"dil_bwd_d{d}", grid=(DIL_HEADS // 2 // P, nblk + 1), in_specs=in_specs,
        out_specs=[tok_q, tok_p, tok_p, pl.BlockSpec(bias.shape[1:], fix3)],
        out_shape=[tok_shape, tok_shape, tok_shape, jax.ShapeDtypeStruct(bias.shape[1:], F32)],
        scratch_shapes=[pltpu.VMEM((rows, 128), F32), one, two, two] + [one] * 8,
        compiler_params=_params(2),
    )(qh, kh, proj, o, lse, do, bias, *(prev or ()))
    return (dq, dk, dv), db


def _t5_bucket(dist):
    max_exact = REL_BUCKETS // 2
    dd = np.maximum(dist, 1).astype(np.float32)
    large = max_exact + (np.log(dd / max_exact) / np.log(REL_MAX_DIST / max_exact)
                         * (REL_BUCKETS - max_exact)).astype(np.int32)
    large = np.minimum(large, REL_BUCKETS - 1)
    return np.where(dist < max_exact, dist, large).astype(np.int32)


def _bucket_onehots():
    i = np.arange(DIL_BLOCK)[:, None]
    j = np.arange(DIL_BLOCK)[None, :]
    out = []
    for _, d in DIL_BRANCHES:
        dist = np.concatenate([DIL_BLOCK + i - j, i - j], axis=1)
        bucket = _t5_bucket(np.clip(dist, 0, None) * d).reshape(-1)
        out.append(jnp.asarray(np.eye(REL_BUCKETS, dtype=np.float32)[:, bucket], BF16))
    return out


def _bias_tables(rel_bias, onehots):
    n = len(onehots)

    def body(rb_ref, mask_ref, *refs):
        parts = _split3(rb_ref[...])
        for k in range(n):
            oh = refs[k][...]
            bias = _dot(parts[0], oh) + _dot(parts[1], oh) + _dot(parts[2], oh)
            refs[n + k][0] = bias + mask_ref[0:1, :]
            refs[n + k][1] = bias + mask_ref[1:2, :]

    flat = pl.pallas_call(
        body, name="bias_tables",
        out_shape=[jax.ShapeDtypeStruct((2, DIL_HEADS, 2 * DIL_BLOCK * DIL_BLOCK), F32)] * n,
        compiler_params=pltpu.CompilerParams(vmem_limit_bytes=VMEM_LIMIT),
    )(rel_bias, _window_masks(), *onehots)
    return [t.reshape(2, DIL_HEADS // 2, 2 * DIL_BLOCK, 2 * DIL_BLOCK) for t in flat]


def _bias_grad(dbs, onehots):
    n = len(dbs)
    dbs = [t.reshape(DIL_HEADS, 2 * DIL_BLOCK * DIL_BLOCK) for t in dbs]

    def body(*refs):
        acc = jnp.zeros((DIL_HEADS, REL_BUCKETS), F32)
        for k in range(n):
            oh = refs[n + k][...]
            for part in _split3(refs[k][...]):
                acc = acc + _dot_nt(part, oh)
        refs[-1][...] = acc

    return pl.pallas_call(
        body, name="bias_grad",
        out_shape=jax.ShapeDtypeStruct((DIL_HEADS, REL_BUCKETS), F32),
        compiler_params=pltpu.CompilerParams(vmem_limit_bytes=VMEM_LIMIT),
    )(*dbs, *onehots)


def _swap_halves(x):
    lane = lax.broadcasted_iota(jnp.int32, x.shape, 1)
    first = (lane % 64) < 32
    return jnp.where(first, pltpu.roll(x, 96, 1), pltpu.roll(x, 32, 1))


def _rope_tables(T):
    pos = jnp.arange(T, dtype=F32)
    inv_freq = ROPE_BASE ** (-jnp.arange(0, MLA_ROPE, 2, dtype=F32) / MLA_ROPE)
    ang = pos[:, None] * inv_freq[None, :]
    z = jnp.zeros((T, 128 - MLA_ROPE), F32)
    cos = jnp.concatenate([jnp.cos(ang), jnp.cos(ang), z], axis=-1)
    sin = jnp.concatenate([-jnp.sin(ang), jnp.sin(ang), z], axis=-1)
    return cos, sin


def _mla_prep(proj, cos, sin, g_qa, g_kva, g_q, g_k, wq, wkv, tm=1024):
    T = proj.shape[0]
    H = MLA_HEADS
    scale = MLA_QK ** -0.5

    def body(cq_ref, ckv_ref, kpe_ref, cos_ref, sin_ref, gqa_ref, gkva_ref, gq_ref, gk_ref, wq_ref, wkv_ref,
             q_ref, k_ref, v_ref):
        cosv, sinv = cos_ref[...], sin_ref[...]

        def rope(x):
            return x * cosv + _swap_halves(x) * sinv

        cq = cq_ref[...]
        qp = _dot_nt((cq * _rstd(cq) * gqa_ref[...]).astype(BF16), wq_ref[...])
        ckv = ckv_ref[...]
        kvp = _dot((ckv * _rstd(ckv) * gkva_ref[...]).astype(BF16), wkv_ref[...])
        kpe = kpe_ref[...]
        one_hot_lane = (lax.broadcasted_iota(jnp.int32, (tm, 128), 1) == 0).astype(BF16)
        for h in range(H):
            a = qp[:, MLA_PAD * h:MLA_PAD * (h + 1)]
            qn = a * _rstd(a, MLA_QK) * gq_ref[...]
            q_ref[h, :, 0:128] = (qn[:, 0:128] * scale).astype(BF16)
            q_ref[h, :, 128:256] = (rope(qn[:, 128:256]) * scale).astype(BF16)
            kn = kvp[:, MLA_PAD * h:MLA_PAD * h + 128]
            r = lax.rsqrt((jnp.sum(kn * kn, axis=-1, keepdims=True)
                           + jnp.sum(kpe * kpe, axis=-1, keepdims=True)) / MLA_QK + EPS)
            k_ref[h, :, 0:128] = (kn * r * gk_ref[:, 0:128]).astype(BF16)
            k_ref[h, :, 128:256] = rope(kpe * r * gk_ref[:, 128:256]).astype(BF16)
            v_ref[h, :, 0:128] = kvp[:, MLA_PAD * h + 128:MLA_PAD * (h + 1)].astype(BF16)
            v_ref[h, :, 128:256] = one_hot_lane

    fix = lambda i: (0, 0)
    return pl.pallas_call(
        body, name="mla_prep", grid=(T // tm,),
        in_specs=[pl.BlockSpec((tm, MLA_Q_RANK), lambda i: (i, CQ_COL // MLA_Q_RANK)),
                  pl.BlockSpec((tm, MLA_KV_RANK), lambda i: (i, CKV_COL // MLA_KV_RANK)),
                  pl.BlockSpec((tm, 128), lambda i: (i, KPE_COL // 128)),
                  pl.BlockSpec((tm, 128), lambda i: (i, 0)), pl.BlockSpec((tm, 128), lambda i: (i, 0)),
                  pl.BlockSpec((1, 256), fix), pl.BlockSpec((1, 128), fix),
                  pl.BlockSpec((1, 256), fix), pl.BlockSpec((1, 256), fix),
                  pl.BlockSpec((H * MLA_PAD, 256), fix), pl.BlockSpec((128, H * MLA_PAD), fix)],
        out_specs=[pl.BlockSpec((H, tm, MLA_PAD), lambda i: (0, i, 0)), pl.BlockSpec((H, tm, MLA_PAD), lambda i: (0, i, 0)),
                   pl.BlockSpec((H, tm, 2 * MLA_V), lambda i: (0, i, 0))],
        out_shape=[jax.ShapeDtypeStruct((H, T, MLA_PAD), BF16), jax.ShapeDtypeStruct((H, T, MLA_PAD), BF16),
                   jax.ShapeDtypeStruct((H, T, 2 * MLA_V), BF16)],
        compiler_params=_params(1),
    )(proj, proj, proj, cos, sin, g_qa, g_kva, g_q, g_k, wq, wkv)


def _mla_prep_bwd(proj, cos, sin, g_qa, g_kva, g_q, g_k, wq, wkv, dq, dk, dv, tm=1024):
    T = proj.shape[0]
    H = MLA_HEADS
    scale = MLA_QK ** -0.5

    def body(cq_ref, ckv_ref, kpe_ref, cos_ref, sin_ref, gqa_ref, gkva_ref, gq_ref, gk_ref, wq_ref, wkv_ref,
             dq_ref, dk_ref, dv_ref,
             dcq_ref, dckv_ref, dkpe_ref, cqn_ref, ckvn_ref, dqp_ref, dkvp_ref,
             dgqa_ref, dgkva_ref, dgq_ref, dgk_ref):
        @pl.when(pl.program_id(0) == 0)
        def _():
            for ref in (dgqa_ref, dgkva_ref, dgq_ref, dgk_ref):
                ref[...] = jnp.zeros_like(ref)

        cosv, sinv = cos_ref[...], sin_ref[...]

        def rope_bwd(dy):
            return dy * cosv + _swap_halves(dy * sinv)

        cq = cq_ref[...]
        rcq = _rstd(cq)
        cqn = (cq * rcq * gqa_ref[...]).astype(BF16)
        cqn_ref[...] = cqn
        qp = _dot_nt(cqn, wq_ref[...])
        ckv = ckv_ref[...]
        rckv = _rstd(ckv)
        ckvn = (ckv * rckv * gkva_ref[...]).astype(BF16)
        ckvn_ref[...] = ckvn
        kvp = _dot(ckvn, wkv_ref[...])
        kpe = kpe_ref[...]
        dkpe = jnp.zeros_like(kpe)
        dgq = jnp.zeros((1, MLA_PAD), F32)
        dgk = jnp.zeros((1, MLA_PAD), F32)
        for h in range(H):
            a = qp[:, MLA_PAD * h:MLA_PAD * (h + 1)]
            dqh = dq_ref[h]
            dn = jnp.concatenate([dqh[:, 0:128], rope_bwd(dqh[:, 128:256])], axis=-1) * scale
            da, dg = _rms_bwd(dn, a, gq_ref[...], _rstd(a, MLA_QK), MLA_QK)
            dgq = dgq + jnp.sum(dg, axis=0, keepdims=True)
            dqp_ref[:, MLA_PAD * h:MLA_PAD * (h + 1)] = da.astype(BF16)

            ak = jnp.concatenate([kvp[:, MLA_PAD * h:MLA_PAD * h + 128], kpe], axis=-1)
            dkh = dk_ref[h]
            dnk = jnp.concatenate([dkh[:, 0:128], rope_bwd(dkh[:, 128:256])], axis=-1)
            dak, dg = _rms_bwd(dnk, ak, gk_ref[...], _rstd(ak, MLA_QK), MLA_QK)
            dgk = dgk + jnp.sum(dg, axis=0, keepdims=True)
            dkpe = dkpe + dak[:, 128:256]
            dkvp_ref[:, MLA_PAD * h:MLA_PAD * h + 128] = dak[:, 0:128].astype(BF16)
            dkvp_ref[:, MLA_PAD * h + 128:MLA_PAD * (h + 1)] = dv_ref[h].astype(BF16)
        dkpe_ref[...] = dkpe
        dgq_ref[...] += dgq
        dgk_ref[...] += dgk
        dcq, dg = _rms_bwd(_dot(dqp_ref[...], wq_ref[...]), cq, gqa_ref[...], rcq)
        dcq_ref[...] = dcq
        dgqa_ref[...] += jnp.sum(dg, axis=0, keepdims=True)
        dckv, dg = _rms_bwd(_dot_nt(dkvp_ref[...], wkv_ref[...]), ckv, gkva_ref[...], rckv)
        dckv_ref[...] = dckv
        dgkva_ref[...] += jnp.sum(dg, axis=0, keepdims=True)

    fix = lambda i: (0, 0)
    row = lambda i: (i, 0)
    head = lambda i: (0, i, 0)
    return pl.pallas_call(
        body, name="mla_prep_bwd", grid=(T // tm,),
        in_specs=[pl.BlockSpec((tm, MLA_Q_RANK), lambda i: (i, CQ_COL // MLA_Q_RANK)),
                  pl.BlockSpec((tm, MLA_KV_RANK), lambda i: (i, CKV_COL // MLA_KV_RANK)),
                  pl.BlockSpec((tm, 128), lambda i: (i, KPE_COL // 128)),
                  pl.BlockSpec((tm, 128), row), pl.BlockSpec((tm, 128), row),
                  pl.BlockSpec((1, 256), fix), pl.BlockSpec((1, 128), fix),
                  pl.BlockSpec((1, 256), fix), pl.BlockSpec((1, 256), fix),
                  pl.BlockSpec((H * MLA_PAD, 256), fix), pl.BlockSpec((128, H * MLA_PAD), fix),
                  pl.BlockSpec((H, tm, MLA_PAD), head), pl.BlockSpec((H, tm, MLA_PAD), head),
                  pl.BlockSpec((H, tm, MLA_V), head)],
        out_specs=[pl.BlockSpec((tm, 256), row), pl.BlockSpec((tm, 128), row), pl.BlockSpec((tm, 128), row),
                   pl.BlockSpec((tm, 256), row), pl.BlockSpec((tm, 128), row),
                   pl.BlockSpec((tm, H * MLA_PAD), row), pl.BlockSpec((tm, H * MLA_PAD), row),
                   pl.BlockSpec((1, 256), fix), pl.BlockSpec((1, 128), fix),
                   pl.BlockSpec((1, 256), fix), pl.BlockSpec((1, 256), fix)],
        out_shape=[jax.ShapeDtypeStruct((T, 256), F32), jax.ShapeDtypeStruct((T, 128), F32),
                   jax.ShapeDtypeStruct((T, 128), F32),
                   jax.ShapeDtypeStruct((T, 256), BF16), jax.ShapeDtypeStruct((T, 128), BF16),
                   jax.ShapeDtypeStruct((T, H * MLA_PAD), BF16), jax.ShapeDtypeStruct((T, H * MLA_PAD), BF16),
                   jax.ShapeDtypeStruct((1, 256), F32), jax.ShapeDtypeStruct((1, 128), F32),
                   jax.ShapeDtypeStruct((1, 256), F32), jax.ShapeDtypeStruct((1, 256), F32)],
        compiler_params=_params(1),
    )(proj, proj, proj, cos, sin, g_qa, g_kva, g_q, g_k, wq, wkv, dq, dk, dv)


def _causal_pairs(T, tq, tk, key_major):
    pairs = [(i, j) for i in range(T // tq) for j in range(T // tk) if j * tk <= i * tq + tq - 1]
    if key_major:
        pairs.sort(key=lambda p: (p[1], p[0]))
    outer = [p[1] if key_major else p[0] for p in pairs]
    first = [int(t == 0 or outer[t] != outer[t - 1]) for t in range(len(pairs))]
    last = [int(t == len(pairs) - 1 or outer[t] != outer[t + 1]) for t in range(len(pairs))]
    tab = lambda v: jnp.asarray(np.array(v, np.int32))
    return tab([p[0] for p in pairs]), tab([p[1] for p in pairs]), tab(first), tab(last)


def _causal_scores(qv, kv, row0):
    s = _dot_nt(qv, kv)
    if row0 is not None:
        row = lax.broadcasted_iota(jnp.int32, s.shape, 0) + row0
        col = lax.broadcasted_iota(jnp.int32, s.shape, 1)
        s = jnp.where(col <= row, s, NEG)
    return s


def _causal_variants(qi, ki, tq, tk, update):
    assert tk % tq == 0
    diag = qi * tq - ki * tk
    for off in range(0, tk, tq):
        pl.when(diag == off)(lambda off=off: update(off))
    pl.when(diag >= tk)(lambda: update(None))


def _visible_keys(off, row0, rows, tk):
    return tk if off is None else min(tk, off + row0 + rows)


def _mla_attn(q, k, v, ride=None, tq=2048, tk=2048, rc=256):
    H, T, _ = q.shape
    tables = _causal_pairs(T, tq, tk, key_major=False)
    n_pairs = int(tables[0].shape[0])
    r_args, r_in, r_shape, r_out, r_scratch = _ride_parts(ride)

    def body(qt, kt, ft, lt, q_ref, k_ref, v_ref, o_ref, lse_ref, m_s, acc):
        t = pl.program_id(1)
        qi, ki = qt[t], kt[t]

        @pl.when(ft[t] == 1)
        def _():
            m_s[...] = jnp.full_like(m_s, NEG)
            acc[...] = jnp.zeros_like(acc)

        def update(off):
            for c in range(tq // rc):
                rows = pl.ds(c * rc, rc)
                keys = pl.ds(0, _visible_keys(off, c * rc, rc, tk))
                s = _causal_scores(q_ref[rows, :], k_ref[keys, :], None if off is None else off + c * rc)
                m_old = m_s[rows, :]
                m_new = jnp.maximum(m_old, jnp.max(s, axis=-1, keepdims=True))
                p = jnp.exp(s - m_new).astype(BF16)
                acc[rows, :] = jnp.exp(m_old - m_new) * acc[rows, :] + _dot(p, v_ref[keys, :])
                m_s[rows, :] = m_new

        _causal_variants(qi, ki, tq, tk, update)

        @pl.when(lt[t] == 1)
        def _():
            l = jnp.max(acc[:, MLA_V:], axis=-1, keepdims=True)
            o_ref[...] = acc[:, :MLA_V] / l
            lse_ref[...] = jnp.broadcast_to(m_s[...] + jnp.log(l), lse_ref.shape)

    qrow = lambda h, t, qt, kt, ft, lt: (h, qt[t], 0)
    krow = lambda h, t, qt, kt, ft, lt: (h, kt[t], 0)
    first = lambda: (pl.program_id(0) == 0) & (pl.program_id(1) == 0)
    last = lambda: (pl.program_id(0) == H - 1) & (pl.program_id(1) == n_pairs - 1)
    outs = pl.pallas_call(
        _riding(body, 7, 2, 2, ride, first, last), name="mla_attn",
        grid_spec=pltpu.PrefetchScalarGridSpec(
            num_scalar_prefetch=4, grid=(H, n_pairs),
            in_specs=[pl.BlockSpec((None, tq, MLA_PAD), qrow), pl.BlockSpec((None, tk, MLA_PAD), krow),
                      pl.BlockSpec((None, tk, 2 * MLA_V), krow)] + r_in,
            out_specs=[pl.BlockSpec((tq, MLA_V), lambda h, t, qt, kt, ft, lt: (qt[t], h)),
                       pl.BlockSpec((None, tq, 128), qrow)] + r_out,
            scratch_shapes=[pltpu.VMEM((tq, 1), F32), pltpu.VMEM((tq, 2 * MLA_V), F32)] + r_scratch),
        out_shape=[jax.ShapeDtypeStruct((T, H * MLA_V), F32), jax.ShapeDtypeStruct((H, T, 128), F32)] + r_shape,
        compiler_params=_params(2),
    )(*tables, q, k, v, *r_args)
    return outs[:2], outs[2:]


def _mla_attn_bwd(q, k, v, o, lse, do, ride=None, tq=2048, tk=2048, rc=512, rc_diagonal=256):
    H, T, _ = q.shape
    tables = _causal_pairs(T, tq, tk, key_major=True)
    n_pairs = int(tables[0].shape[0])
    r_args, r_in, r_shape, r_out, r_scratch = _ride_parts(ride)

    def body(qt, kt, ft, lt, q_ref, k_ref, v_ref, o_ref, lse_ref, do_ref, dq_ref, dk_ref, dv_ref, dk_s, dv_s):
        t = pl.program_id(1)
        qi, ki = qt[t], kt[t]

        @pl.when(t == 0)
        def _():
            dq_ref[...] = jnp.zeros_like(dq_ref)

        @pl.when(ft[t] == 1)
        def _():
            dk_s[...] = jnp.zeros_like(dk_s)
            dv_s[...] = jnp.zeros_like(dv_s)

        def update(off):
            rows_per = rc if off is None else rc_diagonal
            for c in range(tq // rows_per):
                rows = pl.ds(c * rows_per, rows_per)
                keys = pl.ds(0, _visible_keys(off, c * rows_per, rows_per, tk))
                kk, vv = k_ref[keys, :], v_ref[keys, :]
                qv, dov = q_ref[rows, :], do_ref[rows, :]
                delta = jnp.sum(dov * o_ref[rows, :], axis=-1, keepdims=True)
                lse_v = jnp.max(lse_ref[rows, :], axis=-1, keepdims=True)
                p = jnp.exp(_causal_scores(qv, kk, None if off is None else off + c * rows_per) - lse_v)
                dob = dov.astype(BF16)
                dv_s[keys, :] += _dot_tn(p.astype(BF16), dob)
                ds = (p * (_dot_nt(dob, vv) - delta)).astype(BF16)
                dk_s[keys, :] += _dot_tn(ds, qv)
                out_rows = pl.ds(pl.multiple_of(qi * tq + c * rows_per, rows_per), rows_per)
                dq_ref[out_rows, :] += _dot(ds, kk)

        _causal_variants(qi, ki, tq, tk, update)

        @pl.when(lt[t] == 1)
        def _():
            dk_ref[...] = dk_s[...]
            dv_ref[...] = dv_s[...]

    qrow = lambda h, t, qt, kt, ft, lt: (h, qt[t], 0)
    krow = lambda h, t, qt, kt, ft, lt: (h, kt[t], 0)
    qcol = lambda h, t, qt, kt, ft, lt: (qt[t], h)
    first = lambda: (pl.program_id(0) == 0) & (pl.program_id(1) == 0)
    last = lambda: (pl.program_id(0) == H - 1) & (pl.program_id(1) == n_pairs - 1)
    outs = pl.pallas_call(
        _riding(body, 10, 3, 2, ride, first, last), name="mla_attn_bwd",
        grid_spec=pltpu.PrefetchScalarGridSpec(
            num_scalar_prefetch=4, grid=(H, n_pairs),
            in_specs=[pl.BlockSpec((None, tq, MLA_PAD), qrow), pl.BlockSpec((None, tk, MLA_PAD), krow),
                      pl.BlockSpec((None, tk, MLA_V), krow), pl.BlockSpec((tq, MLA_V), qcol),
                      pl.BlockSpec((None, tq, 128), qrow), pl.BlockSpec((tq, MLA_V), qcol)] + r_in,
            out_specs=[pl.BlockSpec((None, T, MLA_PAD), lambda h, t, qt, kt, ft, lt: (h, 0, 0)),
                       pl.BlockSpec((None, tk, MLA_PAD), krow), pl.BlockSpec((None, tk, MLA_V), krow)] + r_out,
            scratch_shapes=[pltpu.VMEM((tk, MLA_PAD), F32), pltpu.VMEM((tk, MLA_V), F32)] + r_scratch),
        out_shape=[jax.ShapeDtypeStruct((H, T, MLA_PAD), F32), jax.ShapeDtypeStruct((H, T, MLA_PAD), F32),
                   jax.ShapeDtypeStruct((H, T, MLA_V), F32)] + r_shape,
        compiler_params=_params(2),
    )(*tables, q, k, v, o, lse, do, *r_args)
    return outs[:3], outs[3:]


def _pair_gain(g):
    return jnp.tile(g.reshape(1, DIL_HD), (1, 2))


def _pad_gain(g):
    return jnp.pad(g.reshape(1, MLA_QK), ((0, 0), (0, MLA_PAD - MLA_QK)))


def _local_step(x, target, s, comm):
    T = x.shape[0]
    w = comm.w
    gq, gk = _pair_gain(s["dil_q_norm"]) * DIL_HD ** -0.5, _pair_gain(s["dil_k_norm"])
    g_q, g_k = _pad_gain(s["mla_q_norm"]), _pad_gain(s["mla_k_norm"])
    cos, sin = _rope_tables(T)
    onehots = _bucket_onehots()
    biases = _bias_tables(s["rel_bias"], onehots)

    (x1, h1, gate1, up1), got = _ffn_fwd(x, s["ffn1_norm"], w["ffn1_w_gate"], w["ffn1_w_up"], w["ffn1_w_down"],
                                         ride=comm.gather(_GROUPS["attn"]))
    comm.weights_landed(_GROUPS["attn"], got)
    hm, proj, qh, kh = _in_proj(x1, s["mix_norm"], w["w_in"], gq, gk)
    dil = None
    for (_, d), bias in zip(DIL_BRANCHES, biases):
        dil = _dil_fwd(qh, kh, proj, bias, d, dil)
    o_dil, lse_dil = dil
    q, k, v = _mla_prep(proj, cos, sin, s["mla_q_a_norm"], s["mla_kv_a_norm"], g_q, g_k, w["mla_w_q_b"], w["mla_w_kv_b"])
    (o_mla, lse_mla), got = _mla_attn(q, k, v, ride=comm.gather(_GROUPS["ffn2"]))
    comm.weights_landed(_GROUPS["ffn2"], got)
    x2, oc = _out_proj(x1, o_dil, o_mla, s["out_norm_dil"], s["out_norm_mla"], w["w_out"])
    (dy, h2, gate2, up2, loss), _ = _ffn_fwd(x2, s["ffn2_norm"], w["ffn2_w_gate"], w["ffn2_w_up"], w["ffn2_w_down"],
                                             target=target)

    gw, gs = {}, {}

    def ffn_grads(name, dy_in, x_in, h, gate, up, early=None):
        dx, a, dg, du, dyh, dgain = _ffn_bwd(dy_in, x_in, s[name + "_norm"], gate, up,
                                             w[name + "_w_gate"], w[name + "_w_up"], w[name + "_w_down"])
        gs[name + "_norm"] = dgain
        down, gate_n, up_n = (name + "_w_down",), (name + "_w_gate",), (name + "_w_up",)
        ride = lambda names: comm.scatter(names, gw) if early is not None else None
        gw[down[0]], landed = _matmul_tn(a, dyh, 1408, 1024, ride=ride(early))
        comm.grads_landed(early or (), landed)
        gw[gate_n[0]], landed = _matmul_tn(dg, h, 1408, 1024, ride=ride(down))
        comm.grads_landed(down, landed)
        gw[up_n[0]], landed = _matmul_tn(du, h, 1408, 1024, ride=ride(gate_n))
        comm.grads_landed(gate_n, landed)
        return dx

    dx2 = ffn_grads("ffn2", dy, x2, h2, gate2, up2)
    gw["w_out"], _ = _matmul_tn(oc, dx2, 1024, 1024)
    do_dil, do_mla, gs["out_norm_dil"], gs["out_norm_mla"] = _out_proj_bwd(
        dx2, o_dil, o_mla, s["out_norm_dil"], s["out_norm_mla"], w["w_out"])

    (dq, dk, dv), got = _mla_attn_bwd(q, k, v, o_mla, lse_mla, do_mla, ride=comm.scatter(_GROUPS["ffn2"], gw))
    comm.grads_landed(_GROUPS["ffn2"], got)
    (dcq, dckv, dkpe, cqn, ckvn, dqp, dkvp, gs["mla_q_a_norm"], gs["mla_kv_a_norm"], dg_q, dg_k) = _mla_prep_bwd(
        proj, cos, sin, s["mla_q_a_norm"], s["mla_kv_a_norm"], g_q, g_k, w["mla_w_q_b"], w["mla_w_kv_b"], dq, dk, dv)
    gs["mla_q_norm"], gs["mla_k_norm"] = dg_q[:, :MLA_QK], dg_k[:, :MLA_QK]
    gw["mla_w_q_b"], _ = _matmul_tn(dqp, cqn, 1024, 256)
    gw["mla_w_kv_b"], _ = _matmul_tn(ckvn, dkvp, 128, 1024)

    dqkv, dbs = None, []
    for (_, d), bias in reversed(list(zip(DIL_BRANCHES, biases))):
        dqkv, db = _dil_bwd(qh, kh, proj, o_dil, lse_dil, do_dil, bias, d, dqkv)
        dbs.insert(0, db)
    dqkv = [dqkv]
    gs["rel_bias"] = _bias_grad(dbs, onehots)

    ready = tuple(n for n in _GROUPS["attn"] if n != "w_in")
    (dx1, dproj, gs["mix_norm"], dgq, dgk), got = _in_proj_bwd(dx2, x1, s["mix_norm"], w["w_in"], proj, gq, gk,
                                                               dqkv, dcq, dckv, dkpe, ride=comm.scatter(ready, gw))
    comm.grads_landed(ready, got)
    gs["dil_q_norm"] = (dgq[:, :DIL_HD] + dgq[:, DIL_HD:]) * DIL_HD ** -0.5
    gs["dil_k_norm"] = dgk[:, :DIL_HD] + dgk[:, DIL_HD:]
    gw["w_in"], _ = _matmul_tn(dproj, hm, 1024, 1024)
    grad_x = ffn_grads("ffn1", dx1, x, h1, gate1, up1, early=("w_in",))
    return loss, grad_x, gw, gs


def _position():
    x, y, c = lax.axis_index("x"), lax.axis_index("y"), lax.axis_index("c")
    return x, y, c, 4 * x + 2 * y + c


def _peer(x, y, c, k):
    px = 1 - x if k & 4 else x
    py = 1 - y if k & 2 else y
    pc = 1 - c if k & 1 else c
    return (px, py, pc), 4 * px + 2 * py + pc


class _Ride:
    def __init__(self, arrays, scatter):
        self.arrays, self.scatter = list(arrays), list(scatter)
        self.n = n = len(self.arrays)
        self.specs = [pl.BlockSpec(memory_space=pl.ANY)] * n
        self.out_shape = [jax.ShapeDtypeStruct(a.shape if sc else (N_DEV,) + a.shape, a.dtype)
                          for a, sc in zip(self.arrays, self.scatter)]
        self.scratch = [pltpu.SemaphoreType.DMA((n, N_DEV - 1)), pltpu.SemaphoreType.DMA((n, N_DEV - 1)),
                        pltpu.SemaphoreType.DMA((n,))]

    def _copies(self, ins, outs, sems):
        send_sems, recv_sems, local_sems = sems
        x, y, c, me = _position()
        copies = []
        for a in range(self.n):
            src = ins[a].at[me] if self.scatter[a] else ins[a]
            copies.append(pltpu.make_async_copy(src, outs[a].at[me], local_sems.at[a]))
        for k in range(1, N_DEV):
            peer, peer_idx = _peer(x, y, c, k)
            for a in range(self.n):
                src = ins[a].at[peer_idx] if self.scatter[a] else ins[a]
                copies.append(pltpu.make_async_remote_copy(
                    src_ref=src, dst_ref=outs[a].at[me], send_sem=send_sems.at[a, k - 1], recv_sem=recv_sems.at[a, k - 1],
                    device_id=peer, device_id_type=pl.DeviceIdType.MESH))
        return copies

    def start(self, ins, outs, sems):
        for cp in self._copies(ins, outs, sems):
            cp.start()

    def wait(self, ins, outs, sems):
        for cp in self._copies(ins, outs, sems):
            cp.wait()


def _ride_parts(ride):
    if ride is None:
        return [], [], [], [], []
    return ride.arrays, ride.specs, ride.out_shape, ride.specs, ride.scratch


def _riding(body, n_in, n_out, n_scratch, ride, first, last):
    if ride is None:
        return body
    n = ride.n
    i1, i2 = n_in + n, n_in + n + n_out
    i3, i4 = i2 + n, i2 + n + n_scratch

    def wrapped(*refs):
        ins, outs, sems = refs[n_in:i1], refs[i2:i3], refs[i4:]

        @pl.when(first())
        def _():
            ride.start(ins, outs, sems)

        body(*refs[:n_in], *refs[i1:i2], *refs[i3:i4])

        @pl.when(last())
        def _():
            ride.wait(ins, outs, sems)

    return wrapped


def _gather_two_level(arrays, name):
    n = len(arrays)
    out_shape = [jax.ShapeDtypeStruct((N_DEV,) + a.shape, a.dtype) for a in arrays]

    def body(*refs):
        ins, outs = refs[:n], refs[n:2 * n]
        send_sems, recv_sems, local_sems = refs[2 * n:]
        x, y, c, me = _position()
        sibling = (x, y, 1 - c)
        chips = [(1 - x, y), (x, 1 - y), (1 - x, 1 - y)]
        block = lambda px, py, pc: 4 * px + 2 * py + pc

        def copy(a, k, blk, to, src=None):
            dst = outs[a].at[blk]
            return pltpu.make_async_remote_copy(
                src_ref=dst if src is None else src, dst_ref=dst, send_sem=send_sems.at[a, k], recv_sem=recv_sems.at[a, k],
                device_id=to, device_id_type=pl.DeviceIdType.MESH)

        local = [pltpu.make_async_copy(ins[a], outs[a].at[me], local_sems.at[a]) for a in range(n)]
        first = []
        for a in range(n):
            first.append(copy(a, 0, me, sibling, src=ins[a]))
            first += [copy(a, 1 + j, me, (*chip, c), src=ins[a]) for j, chip in enumerate(chips)]
        for cp in local + first:
            cp.start()
        passed = []
        for j, chip in enumerate(chips):
            for a in range(n):
                copy(a, 1 + j, block(*chip, c), sibling).wait_recv()
                passed.append(copy(a, 4 + j, block(*chip, c), sibling))
                passed[-1].start()
        for a in range(n):
            copy(a, 0, block(x, y, 1 - c), sibling).wait_recv()
            for j, chip in enumerate(chips):
                copy(a, 4 + j, block(*chip, 1 - c), sibling).wait_recv()
        for cp in first + passed:
            cp.wait_send()
        for cp in local:
            cp.wait()

    any_spec = [pl.BlockSpec(memory_space=pl.ANY)] * n
    return pl.pallas_call(
        body, name=name, in_specs=any_spec, out_specs=any_spec, out_shape=out_shape,
        scratch_shapes=[pltpu.SemaphoreType.DMA((n, N_DEV - 1)), pltpu.SemaphoreType.DMA((n, N_DEV - 1)),
                        pltpu.SemaphoreType.DMA((n,))],
    )(*arrays)


def _adamw_math(wv, g, m, v):
    m = ADAM_B1 * m + (1.0 - ADAM_B1) * g
    v = ADAM_B2 * v + (1.0 - ADAM_B2) * (g * g)
    m_hat = m / (1.0 - ADAM_B1 ** ADAM_STEP)
    v_hat = v / (1.0 - ADAM_B2 ** ADAM_STEP)
    delta = -ADAM_LR * (m_hat / (jnp.sqrt(v_hat) + ADAM_EPS) + ADAM_WD * wv)
    return delta, m, v


def _adamw(items, ride=None, max_rows=256):
    K = len(items)
    tiles, spans, start = [], [], 0
    for _, wv, _, _ in items:
        R = wv.shape[1]
        tr = max([t for t in range(16, max_rows + 1, 16) if R % t == 0] or [R])
        tiles.append(tr)
        spans.append((start, R // tr))
        start += R // tr
    total = start
    r_args, r_in, r_shape, r_out, r_scratch = _ride_parts(ride)

    def body(*refs):
        i = pl.program_id(0)
        for k, (first_step, n_steps) in enumerate(spans):
            def update(k=k):
                p_ref, w_ref, m_ref, v_ref = refs[4 * k:4 * k + 4]
                g_ref, d_ref, mo_ref, vo_ref = refs[4 * K + 4 * k:4 * K + 4 * k + 4]
                g = p_ref[0].astype(F32)
                for j in range(1, N_DEV):
                    g = g + p_ref[j].astype(F32)
                d, mn, vn = _adamw_math(w_ref[0], g, m_ref[0], v_ref[0])
                g_ref[0] = g
                d_ref[0] = d
                mo_ref[0] = mn
                vo_ref[0] = vn

            pl.when((i >= first_step) & (i < first_step + n_steps))(update)

    in_specs, out_specs, out_shape, args = [], [], [], []
    for (parts, wv, m, v), tr, (first_step, n_steps) in zip(items, tiles, spans):
        C = wv.shape[2]
        tile = lambda i, s=first_step, n=n_steps: (0, jnp.clip(i - s, 0, n - 1), 0)
        blk = pl.BlockSpec((1, tr, C), tile)
        in_specs += [pl.BlockSpec((N_DEV, tr, C), tile), blk, blk, blk]
        out_specs += [blk] * 4
        out_shape += [jax.ShapeDtypeStruct(wv.shape, F32)] * 4
        args += [parts, wv, m, v]
    outs = pl.pallas_call(
        _riding(body, 4 * K, 4 * K, 0, ride, lambda: pl.program_id(0) == 0, lambda: pl.program_id(0) == total - 1),
        name="adamw", grid=(total,),
        in_specs=in_specs + r_in, out_specs=out_specs + r_out, out_shape=out_shape + r_shape,
        scratch_shapes=r_scratch, compiler_params=_params(1),
    )(*args, *r_args)
    return [outs[4 * k:4 * k + 4] for k in range(K)], outs[4 * K:]


_TRANSPOSED = ("ffn1_w_gate", "ffn1_w_up", "ffn2_w_gate", "ffn2_w_up", "w_in", "mla_w_q_b")
_GROUPS = {"ffn1": ("ffn1_w_gate", "ffn1_w_up", "ffn1_w_down"),
           "ffn2": ("ffn2_w_gate", "ffn2_w_up", "ffn2_w_down"),
           "attn": ("w_in", "mla_w_q_b", "mla_w_kv_b", "w_out")}
_SMALL = ("ffn1_norm", "mix_norm", "ffn2_norm", "out_norm_dil", "out_norm_mla", "mla_q_a_norm", "rel_bias",
          "mla_q_norm", "mla_k_norm", "mla_kv_a_norm", "dil_q_norm", "dil_k_norm")
_SMALL_ROWS = 48


def _cols_to_full(g):
    return g.transpose(1, 0, 2).reshape(g.shape[1], N_DEV * g.shape[2])


def _full_to_cols(f):
    return f.reshape(f.shape[0], N_DEV, f.shape[1] // N_DEV).transpose(1, 0, 2)


def _shard_view(name, a):
    return jnp.swapaxes(a, 1, 2) if name in _TRANSPOSED else a


def _to_full(name, g):
    if name == "mla_w_kv_b":
        return _cols_to_full(g)
    f = g.reshape(-1, g.shape[-1])
    if name == "w_in":
        f = jnp.pad(f, ((0, PROJ_PAD - PROJ_COLS), (0, 0)))
    if name == "mla_w_q_b":
        f = jnp.pad(f.reshape(MLA_HEADS, MLA_QK, -1), ((0, 0), (0, MLA_PAD - MLA_QK), (0, 0)))
        f = f.reshape(MLA_HEADS * MLA_PAD, -1)
    return f


def _to_parts(name, f):
    if name == "mla_w_kv_b":
        return _full_to_cols(f).astype(BF16)
    if name == "w_in":
        f = f[:PROJ_COLS]
    if name == "mla_w_q_b":
        f = f.reshape(MLA_HEADS, MLA_PAD, -1)[:, :MLA_QK].reshape(MLA_HEADS * MLA_QK, -1)
    return f.reshape(N_DEV, -1, f.shape[-1]).astype(BF16)


class _Comm:
    def __init__(self, shards):
        self.shards, self.w, self.recv = shards, {}, {}

    def gather(self, names):
        return _Ride([self.shards[n] for n in names], [False] * len(names))

    def scatter(self, names, grads):
        return _Ride([_to_parts(n, grads[n]) for n in names], [True] * len(names))

    def weights_landed(self, names, got):
        self.w.update({n: _to_full(n, g) for n, g in zip(names, got)})

    def grads_landed(self, names, got):
        self.recv.update(zip(names, got))


def _pack_small(parts, extra):
    flat = jnp.concatenate([parts[n].reshape(-1) for n in _SMALL] + [extra.reshape(-1)])
    return jnp.pad(flat, (0, _SMALL_ROWS * 128 - flat.shape[0])).reshape(_SMALL_ROWS, 128)


def _unpack_small(packed, shapes):
    flat, out, off = packed.reshape(-1), {}, 0
    for n in _SMALL:
        size = math.prod(shapes[n])
        out[n] = flat[off:off + size].reshape(shapes[n])
        off += size
    return out, flat[off]


_NAMES = ("ffn1_norm", "ffn1_w_gate", "ffn1_w_up", "ffn1_w_down", "mix_norm", "w_in", "dil_q_norm", "dil_k_norm",
          "rel_bias", "mla_q_a_norm", "mla_w_q_b", "mla_kv_a_norm", "mla_w_kv_b", "mla_q_norm", "mla_k_norm",
          "out_norm_dil", "out_norm_mla", "w_out", "ffn2_norm", "ffn2_w_gate", "ffn2_w_up", "ffn2_w_down")


def kernel(x, ffn1_norm, ffn1_w_gate, ffn1_w_up, ffn1_w_down, mix_norm, w_in, dil_q_norm, dil_k_norm, rel_bias, mla_q_a_norm, mla_w_q_b, mla_kv_a_norm, mla_w_kv_b, mla_q_norm, mla_k_norm, out_norm_dil, out_norm_mla, w_out, ffn2_norm, ffn2_w_gate, ffn2_w_up, ffn2_w_down, loss_target, m_ffn1_norm, m_ffn1_w_gate, m_ffn1_w_up, m_ffn1_w_down, m_mix_norm, m_w_in, m_dil_q_norm, m_dil_k_norm, m_rel_bias, m_mla_q_a_norm, m_mla_w_q_b, m_mla_kv_a_norm, m_mla_w_kv_b, m_mla_q_norm, m_mla_k_norm, m_out_norm_dil, m_out_norm_mla, m_w_out, m_ffn2_norm, m_ffn2_w_gate, m_ffn2_w_up, m_ffn2_w_down, v_ffn1_norm, v_ffn1_w_gate, v_ffn1_w_up, v_ffn1_w_down, v_mix_norm, v_w_in, v_dil_q_norm, v_dil_k_norm, v_rel_bias, v_mla_q_a_norm, v_mla_w_q_b, v_mla_kv_a_norm, v_mla_w_kv_b, v_mla_q_norm, v_mla_k_norm, v_out_norm_dil, v_out_norm_mla, v_w_out, v_ffn2_norm, v_ffn2_w_gate, v_ffn2_w_up, v_ffn2_w_down):
    args = locals()
    wts = {n: args[n] for n in _NAMES}
    mom = {n: args["m_" + n] for n in _NAMES}
    var = {n: args["v_" + n] for n in _NAMES}

    matrices = [n for group in _GROUPS.values() for n in group]
    comm = _Comm({n: _shard_view(n, wts[n])[0].astype(BF16) for n in matrices})
    comm.weights_landed(_GROUPS["ffn1"], _gather_two_level(comm.gather(_GROUPS["ffn1"]).arrays, "gather_first"))
    small = {n: wts[n].reshape(1, -1) if n != "rel_bias" else wts[n] for n in _SMALL}

    loss, grad_x, gw, gs = _local_step(x[0], loss_target[0], small, comm)

    item = lambda n: (comm.recv[n],) + tuple(_shard_view(n, a[n]) for a in (wts, mom, var))
    landed = [n for n in matrices if n != "ffn1_w_up"]
    last = comm.scatter(("ffn1_w_up",), gw)
    updates, got = _adamw([item(n) for n in landed], max_rows=32,
                          ride=_Ride(last.arrays + [_pack_small(gs, loss[0, 0])], last.scatter + [False]))
    comm.grads_landed(("ffn1_w_up",), got[:-1])

    zero = jnp.zeros((), F32)
    small_item = (got[-1],) + tuple(_pack_small(a, zero)[None] for a in (wts, mom, var))
    (up_update, packed), _ = _adamw([item("ffn1_w_up"), small_item])
    res = {n: [_shard_view(n, r) for r in u] for n, u in zip(landed + ["ffn1_w_up"], updates + [up_update])}
    shapes = {n: wts[n].shape for n in _SMALL}
    loss_total = None
    for slot, q in enumerate(packed):
        vals, extra = _unpack_small(q, shapes)
        if slot == 0:
            loss_total = extra
        for n in _SMALL:
            res.setdefault(n, [None] * 4)[slot] = vals[n]
    outs = [loss_total, grad_x[None]]
    for slot in range(4):
        outs += [res[n][slot].reshape(wts[n].shape) for n in _NAMES]
    return tuple(outs)
```

```python
import math

import numpy as np
import jax
import jax.numpy as jnp
from jax import lax
from jax.experimental import pallas as pl
from jax.experimental.pallas import tpu as pltpu

F32, BF16 = jnp.float32, jnp.bfloat16
EPS = 1e-6
NEG = -1e30
N_DEV = 8

DIL_HEADS, DIL_HD = 8, 64
DIL_WIDTH = DIL_HEADS * DIL_HD
DIL_BRANCHES = ((128, 1), (512, 4), (2048, 16))
DIL_BLOCK = 128
MLA_HEADS, MLA_NOPE, MLA_ROPE, MLA_V = 4, 128, 64, 128
MLA_QK = MLA_NOPE + MLA_ROPE
MLA_PAD = 256
ROPE_BASE = 10000.0
REL_BUCKETS, REL_MAX_DIST = 32, 2048
MLA_Q_RANK, MLA_KV_RANK = 256, 128
PROJ_COLS, PROJ_PAD = 1984, 2048
CQ_COL = 3 * DIL_WIDTH
CKV_COL, KPE_COL = CQ_COL + MLA_Q_RANK, CQ_COL + MLA_Q_RANK + MLA_KV_RANK
FFN_RESID = 0.5
ADAM_LR, ADAM_B1, ADAM_B2, ADAM_EPS, ADAM_WD, ADAM_STEP = 0.001, 0.9, 0.999, 1e-08, 0.01, 10
VMEM_LIMIT = 62 * 1024 * 1024

_NT = (((1,), (1,)), ((), ()))
_TN = (((0,), (0,)), ((), ()))


def _dot(a, b):
    return jnp.dot(a, b, preferred_element_type=F32)


def _dot_nt(a, b):
    return lax.dot_general(a, b, _NT, preferred_element_type=F32)


def _dot_tn(a, b):
    return lax.dot_general(a, b, _TN, preferred_element_type=F32)


def _params(n_axes):
    return pltpu.CompilerParams(dimension_semantics=("arbitrary",) * n_axes, vmem_limit_bytes=VMEM_LIMIT)


def _rstd(x, n=None):
    n = x.shape[-1] if n is None else n
    return lax.rsqrt(jnp.sum(x * x, axis=-1, keepdims=True) / n + EPS)


def _rms_bwd(dy, x, g, r, n=None):
    n = x.shape[-1] if n is None else n
    u = dy * g
    dx = r * u - x * (r * r * r) * (jnp.sum(u * x, axis=-1, keepdims=True) / n)
    return dx, dy * x * r


def _sigmoid(x):
    return 1.0 / (1.0 + jnp.exp(-x))


def _split3(x):
    parts = []
    for _ in range(3):
        xb = x.astype(BF16)
        parts.append(xb)
        x = x - xb.astype(F32)
    return parts


def _ffn_fwd(x, gain, wg, wu, wd, ride=None, target=None, tm=512, tf=2816):
    T, D = x.shape
    F = wg.shape[0]
    ni, nj = T // tm, F // tf
    with_loss = target is not None
    r_args, r_in, r_shape, r_out, r_scratch = _ride_parts(ride)

    def body(*refs):
        x_ref, g_ref, wg_ref, wu_ref, wd_ref = refs[:5]
        t_ref = refs[5] if with_loss else None
        xo_ref, h_ref, gate_ref, up_ref = refs[5 + with_loss:9 + with_loss]
        loss_ref = refs[-2] if with_loss else None
        acc = refs[-1]
        i, j = pl.program_id(0), pl.program_id(1)

        @pl.when(j == 0)
        def _():
            xv = x_ref[...]
            h_ref[...] = (xv * _rstd(xv) * g_ref[...]).astype(BF16)
            acc[...] = jnp.zeros_like(acc)

        h = h_ref[...]
        g = _dot_nt(h, wg_ref[...])
        u = _dot_nt(h, wu_ref[...])
        gate_ref[...] = g.astype(BF16)
        up_ref[...] = u.astype(BF16)
        a = (g * _sigmoid(g) * u).astype(BF16)
        acc[...] += _dot(a, wd_ref[...])

        @pl.when(j == nj - 1)
        def _():
            y = x_ref[...] + FFN_RESID * acc[...]
            if with_loss:
                @pl.when(i == 0)
                def _():
                    loss_ref[...] = jnp.zeros_like(loss_ref)

                e = y - t_ref[...]
                xo_ref[...] = e * (1.0 / D)
                loss_ref[...] += (0.5 / D) * jnp.sum(e * e)
            else:
                xo_ref[...] = y

    row = lambda i, j: (i, 0)
    tile = lambda i, j: (i, j)
    n_in, n_out = 5 + with_loss, 4 + with_loss
    first = lambda: (pl.program_id(0) == 0) & (pl.program_id(1) == 0)
    last = lambda: (pl.program_id(0) == ni - 1) & (pl.program_id(1) == nj - 1)
    outs = pl.pallas_call(
        _riding(body, n_in, n_out, 1, ride, first, last), name="ffn_fwd", grid=(ni, nj),
        in_specs=[pl.BlockSpec((tm, D), row), pl.BlockSpec((1, D), lambda i, j: (0, 0)),
                  pl.BlockSpec((tf, D), lambda i, j: (j, 0)), pl.BlockSpec((tf, D), lambda i, j: (j, 0)),
                  pl.BlockSpec((tf, D), lambda i, j: (j, 0))] + [pl.BlockSpec((tm, D), row)] * with_loss + r_in,
        out_specs=[pl.BlockSpec((tm, D), row), pl.BlockSpec((tm, D), row), pl.BlockSpec((tm, tf), tile),
                   pl.BlockSpec((tm, tf), tile)] + [pl.BlockSpec((1, 128), lambda i, j: (0, 0))] * with_loss + r_out,
        out_shape=[jax.ShapeDtypeStruct((T, D), F32), jax.ShapeDtypeStruct((T, D), BF16),
                   jax.ShapeDtypeStruct((T, F), BF16), jax.ShapeDtypeStruct((T, F), BF16)]
        + [jax.ShapeDtypeStruct((1, 128), F32)] * with_loss + r_shape,
        scratch_shapes=[pltpu.VMEM((tm, D), F32)] + r_scratch,
        compiler_params=_params(2),
    )(x, gain, wg, wu, wd, *([target] if with_loss else []), *r_args)
    return outs[:n_out], outs[n_out:]


def _ffn_bwd(dy, x, gain, gate, up, wg, wu, wd, tm=256, tf=2816):
    T, D = x.shape
    F = wg.shape[0]
    ni, nj = T // tm, F // tf

    def body(dy_ref, x_ref, g_ref, gate_ref, up_ref, wg_ref, wu_ref, wd_ref,
             dx_ref, a_ref, dg_ref, du_ref, dyh_ref, dgain_ref, acc):
        i, j = pl.program_id(0), pl.program_id(1)

        @pl.when((i == 0) & (j == 0))
        def _():
            dgain_ref[...] = jnp.zeros_like(dgain_ref)

        @pl.when(j == 0)
        def _():
            dyh_ref[...] = (FFN_RESID * dy_ref[...]).astype(BF16)
            acc[...] = jnp.zeros_like(acc)

        da = _dot_nt(dyh_ref[...], wd_ref[...])
        g = gate_ref[...].astype(F32)
        u = up_ref[...].astype(F32)
        sig = _sigmoid(g)
        s = g * sig
        a_ref[...] = (s * u).astype(BF16)
        dg = (da * u * (sig * (1.0 + g * (1.0 - sig)))).astype(BF16)
        du = (da * s).astype(BF16)
        dg_ref[...] = dg
        du_ref[...] = du
        acc[...] += _dot(dg, wg_ref[...]) + _dot(du, wu_ref[...])

        @pl.when(j == nj - 1)
        def _():
            xv = x_ref[...]
            dxn, dgc = _rms_bwd(acc[...], xv, g_ref[...], _rstd(xv))
            dx_ref[...] = dy_ref[...] + dxn
            dgain_ref[...] += jnp.sum(dgc, axis=0, keepdims=True)

    return pl.pallas_call(
        body, name="ffn_bwd", grid=(ni, nj),
        in_specs=[pl.BlockSpec((tm, D), lambda i, j: (i, 0)), pl.BlockSpec((tm, D), lambda i, j: (i, 0)),
                  pl.BlockSpec((1, D), lambda i, j: (0, 0)),
                  pl.BlockSpec((tm, tf), lambda i, j: (i, j)), pl.BlockSpec((tm, tf), lambda i, j: (i, j)),
                  pl.BlockSpec((tf, D), lambda i, j: (j, 0)), pl.BlockSpec((tf, D), lambda i, j: (j, 0)),
                  pl.BlockSpec((tf, D), lambda i, j: (j, 0))],
        out_specs=[pl.BlockSpec((tm, D), lambda i, j: (i, 0)),
                   pl.BlockSpec((tm, tf), lambda i, j: (i, j)), pl.BlockSpec((tm, tf), lambda i, j: (i, j)),
                   pl.BlockSpec((tm, tf), lambda i, j: (i, j)),
                   pl.BlockSpec((tm, D), lambda i, j: (i, 0)), pl.BlockSpec((1, D), lambda i, j: (0, 0))],
        out_shape=[jax.ShapeDtypeStruct((T, D), F32), jax.ShapeDtypeStruct((T, F), BF16),
                   jax.ShapeDtypeStruct((T, F), BF16), jax.ShapeDtypeStruct((T, F), BF16),
                   jax.ShapeDtypeStruct((T, D), BF16), jax.ShapeDtypeStruct((1, D), F32)],
        scratch_shapes=[pltpu.VMEM((tm, D), F32)],
        compiler_params=_params(2),
    )(dy, x, gain, gate, up, wg, wu, wd)


def _matmul_tn(a, b, tk, tn, ride=None, tt=2048):
    T, K = a.shape
    N = b.shape[1]
    tk, tn = min(tk, K), min(tn, N)
    grid = (K // tk, N // tn, T // tt)
    r_args, r_in, r_shape, r_out, r_scratch = _ride_parts(ride)

    def body(a_ref, b_ref, o_ref, acc):
        t = pl.program_id(2)

        @pl.when(t == 0)
        def _():
            acc[...] = jnp.zeros_like(acc)

        acc[...] += _dot_tn(a_ref[...].astype(BF16), b_ref[...].astype(BF16))

        @pl.when(t == grid[2] - 1)
        def _():
            o_ref[...] = acc[...].astype(BF16)

    first = lambda: (pl.program_id(0) == 0) & (pl.program_id(1) == 0) & (pl.program_id(2) == 0)
    last = lambda: ((pl.program_id(0) == grid[0] - 1) & (pl.program_id(1) == grid[1] - 1)
                    & (pl.program_id(2) == grid[2] - 1))
    outs = pl.pallas_call(
        _riding(body, 2, 1, 1, ride, first, last), name="matmul_tn", grid=grid,
        in_specs=[pl.BlockSpec((tt, tk), lambda k, n, t: (t, k)), pl.BlockSpec((tt, tn), lambda k, n, t: (t, n))] + r_in,
        out_specs=[pl.BlockSpec((tk, tn), lambda k, n, t: (k, n))] + r_out,
        out_shape=[jax.ShapeDtypeStruct((K, N), BF16)] + r_shape,
        scratch_shapes=[pltpu.VMEM((tk, tn), F32)] + r_scratch,
        compiler_params=_params(3),
    )(a, b, *r_args)
    return outs[0], outs[1:]


def _in_proj(x, gain, w, gq, gk, tm=1024):
    T, D = x.shape
    N = w.shape[0]
    W = DIL_WIDTH

    def body(x_ref, g_ref, w_ref, gq_ref, gk_ref, h_ref, p_ref, qh_ref, kh_ref):
        xv = x_ref[...]
        h = (xv * _rstd(xv) * g_ref[...]).astype(BF16)
        h_ref[...] = h
        p_ref[...] = _dot_nt(h, w_ref[...])
        lo = lax.broadcasted_iota(jnp.int32, (tm, 128), 1) < DIL_HD
        for hp in range(DIL_HEADS // 2):
            q = p_ref[:, 128 * hp:128 * (hp + 1)]
            k = p_ref[:, W + 128 * hp:W + 128 * (hp + 1)]
            qh_ref[:, 128 * hp:128 * (hp + 1)] = (q * _pair_rstd(q, lo) * gq_ref[...]).astype(BF16).astype(F32)
            kh_ref[:, 128 * hp:128 * (hp + 1)] = (k * _pair_rstd(k, lo) * gk_ref[...]).astype(BF16).astype(F32)

    row = lambda i: (i, 0)
    fix = lambda i: (0, 0)
    return pl.pallas_call(
        body, name="in_proj", grid=(T // tm,),
        in_specs=[pl.BlockSpec((tm, D), row), pl.BlockSpec((1, D), fix), pl.BlockSpec((N, D), fix),
                  pl.BlockSpec((1, 128), fix), pl.BlockSpec((1, 128), fix)],
        out_specs=[pl.BlockSpec((tm, D), row), pl.BlockSpec((tm, N), row), pl.BlockSpec((tm, W), row),
                   pl.BlockSpec((tm, W), row)],
        out_shape=[jax.ShapeDtypeStruct((T, D), BF16), jax.ShapeDtypeStruct((T, N), F32),
                   jax.ShapeDtypeStruct((T, W), F32), jax.ShapeDtypeStruct((T, W), F32)],
        compiler_params=_params(1),
    )(x, gain, w, gq, gk)


def _in_proj_bwd(dx_up, x, gain, w, proj, gq, gk, dqkv, dcq, dckv, dkpe, ride=None, tm=512):
    T, D = x.shape
    N = w.shape[0]
    W = DIL_WIDTH
    nb = len(dqkv)

    def body(*refs):
        dxu_ref, x_ref, g_ref, w_ref, q_ref, k_ref, gq_ref, gk_ref = refs[:8]
        dil_refs = refs[8:8 + 3 * nb]
        dcq_ref, dckv_ref, dkpe_ref, dx_ref, dp_ref, dgain_ref, dgq_ref, dgk_ref = refs[8 + 3 * nb:]

        @pl.when(pl.program_id(0) == 0)
        def _():
            for ref in (dgain_ref, dgq_ref, dgk_ref):
                ref[...] = jnp.zeros_like(ref)

        lo = lax.broadcasted_iota(jnp.int32, (tm, 128), 1) < DIL_HD
        norms = ((q_ref, gq_ref, dgq_ref), (k_ref, gk_ref, dgk_ref))
        for part in range(3):
            acc = dil_refs[part][...]
            for b in range(1, nb):
                acc = acc + dil_refs[3 * b + part][...]
            if part == 2:
                dp_ref[:, 2 * W:3 * W] = acc.astype(BF16)
                continue
            raw_ref, gn_ref, dgn_ref = norms[part]
            for hp in range(DIL_HEADS // 2):
                raw = raw_ref[:, 128 * hp:128 * (hp + 1)]
                d_raw, dgn = _pair_rms_bwd(acc[:, 128 * hp:128 * (hp + 1)], raw, _pair_rstd(raw, lo), gn_ref[...], lo)
                dp_ref[:, part * W + 128 * hp:part * W + 128 * (hp + 1)] = d_raw.astype(BF16)
                dgn_ref[...] += dgn
        dp_ref[:, 3 * W:3 * W + 256] = dcq_ref[...].astype(BF16)
        dp_ref[:, 3 * W + 256:3 * W + 384] = dckv_ref[...].astype(BF16)
        dp_ref[:, 3 * W + 384:N] = dkpe_ref[...].astype(BF16)
        dh = _dot(dp_ref[...], w_ref[...])
        xv = x_ref[...]
        dxn, dgc = _rms_bwd(dh, xv, g_ref[...], _rstd(xv))
        dx_ref[...] = dxu_ref[...] + dxn
        dgain_ref[...] += jnp.sum(dgc, axis=0, keepdims=True)

    row = lambda i: (i, 0)
    fix = lambda i: (0, 0)
    r_args, r_in, r_shape, r_out, r_scratch = _ride_parts(ride)
    first = lambda: pl.program_id(0) == 0
    last = lambda: pl.program_id(0) == T // tm - 1
    outs = pl.pallas_call(
        _riding(body, 11 + 3 * nb, 5, 0, ride, first, last), name="in_proj_bwd", grid=(T // tm,),
        in_specs=[pl.BlockSpec((tm, D), row), pl.BlockSpec((tm, D), row), pl.BlockSpec((1, D), fix),
                  pl.BlockSpec((N, D), fix), pl.BlockSpec((tm, W), row), pl.BlockSpec((tm, W), lambda i: (i, 1)),
                  pl.BlockSpec((1, 128), fix), pl.BlockSpec((1, 128), fix)] + [pl.BlockSpec((tm, W), row)] * (3 * nb)
                 + [pl.BlockSpec((tm, 256), row), pl.BlockSpec((tm, 128), row), pl.BlockSpec((tm, 128), row)] + r_in,
        out_specs=[pl.BlockSpec((tm, D), row), pl.BlockSpec((tm, N), row), pl.BlockSpec((1, D), fix),
                   pl.BlockSpec((1, 128), fix), pl.BlockSpec((1, 128), fix)] + r_out,
        out_shape=[jax.ShapeDtypeStruct((T, D), F32), jax.ShapeDtypeStruct((T, N), BF16),
                   jax.ShapeDtypeStruct((1, D), F32), jax.ShapeDtypeStruct((1, 128), F32),
                   jax.ShapeDtypeStruct((1, 128), F32)] + r_shape,
        scratch_shapes=r_scratch,
        compiler_params=_params(1),
    )(dx_up, x, gain, w, proj, proj, gq, gk, *[a for triple in dqkv for a in triple], dcq, dckv, dkpe, *r_args)
    return outs[:5], outs[5:]


def _out_proj(x, o_dil, o_mla, g_dil, g_mla, w, tm=1024):
    T, D = x.shape
    W = o_dil.shape[1]

    def body(x_ref, od_ref, om_ref, gd_ref, gm_ref, w_ref, xo_ref, oc_ref):
        od, om = od_ref[...], om_ref[...]
        oc_ref[:, 0:W] = (od * _rstd(od) * gd_ref[...]).astype(BF16)
        oc_ref[:, W:2 * W] = (om * _rstd(om) * gm_ref[...]).astype(BF16)
        xo_ref[...] = x_ref[...] + _dot(oc_ref[...], w_ref[...])

    row = lambda i: (i, 0)
    fix = lambda i: (0, 0)
    return pl.pallas_call(
        body, name="out_proj", grid=(T // tm,),
        in_specs=[pl.BlockSpec((tm, D), row), pl.BlockSpec((tm, W), row), pl.BlockSpec((tm, W), row),
                  pl.BlockSpec((1, W), fix), pl.BlockSpec((1, W), fix), pl.BlockSpec((2 * W, D), fix)],
        out_specs=[pl.BlockSpec((tm, D), row), pl.BlockSpec((tm, 2 * W), row)],
        out_shape=[jax.ShapeDtypeStruct((T, D), F32), jax.ShapeDtypeStruct((T, 2 * W), BF16)],
        compiler_params=_params(1),
    )(x, o_dil, o_mla, g_dil, g_mla, w)


def _out_proj_bwd(dx, o_dil, o_mla, g_dil, g_mla, w, tm=1024):
    T, D = dx.shape
    W = o_dil.shape[1]

    def body(dx_ref, od_ref, om_ref, gd_ref, gm_ref, w_ref, dod_ref, dom_ref, dgd_ref, dgm_ref):
        @pl.when(pl.program_id(0) == 0)
        def _():
            dgd_ref[...] = jnp.zeros_like(dgd_ref)
            dgm_ref[...] = jnp.zeros_like(dgm_ref)

        doc = _dot_nt(dx_ref[...].astype(BF16), w_ref[...])
        od, om = od_ref[...], om_ref[...]
        dod, dgd = _rms_bwd(doc[:, 0:W], od, gd_ref[...], _rstd(od))
        dom, dgm = _rms_bwd(doc[:, W:2 * W], om, gm_ref[...], _rstd(om))
        dod_ref[...] = dod
        dom_ref[...] = dom
        dgd_ref[...] += jnp.sum(dgd, axis=0, keepdims=True)
        dgm_ref[...] += jnp.sum(dgm, axis=0, keepdims=True)

    row = lambda i: (i, 0)
    fix = lambda i: (0, 0)
    return pl.pallas_call(
        body, name="out_proj_bwd", grid=(T // tm,),
        in_specs=[pl.BlockSpec((tm, D), row), pl.BlockSpec((tm, W), row), pl.BlockSpec((tm, W), row),
                  pl.BlockSpec((1, W), fix), pl.BlockSpec((1, W), fix), pl.BlockSpec((2 * W, D), fix)],
        out_specs=[pl.BlockSpec((tm, W), row), pl.BlockSpec((tm, W), row),
                   pl.BlockSpec((1, W), fix), pl.BlockSpec((1, W), fix)],
        out_shape=[jax.ShapeDtypeStruct((T, W), F32), jax.ShapeDtypeStruct((T, W), F32),
                   jax.ShapeDtypeStruct((1, W), F32), jax.ShapeDtypeStruct((1, W), F32)],
        compiler_params=_params(1),
    )(dx, o_dil, o_mla, g_dil, g_mla, w)


def _pair_rstd(x, lo):
    sq = x * x
    s0 = jnp.sum(jnp.where(lo, sq, 0.0), axis=-1, keepdims=True)
    s1 = jnp.sum(jnp.where(lo, 0.0, sq), axis=-1, keepdims=True)
    return jnp.where(lo, lax.rsqrt(s0 / DIL_HD + EPS), lax.rsqrt(s1 / DIL_HD + EPS))


def _pair_rms_bwd(dn, x, r, g, lo):
    u = dn * g
    t = u * x
    d0 = jnp.sum(jnp.where(lo, t, 0.0), axis=-1, keepdims=True)
    d1 = jnp.sum(jnp.where(lo, 0.0, t), axis=-1, keepdims=True)
    dx = r * u - x * (r * r * r) * (jnp.where(lo, d0, d1) / DIL_HD)
    return dx, jnp.sum(dn * x * r, axis=0, keepdims=True)


def _pair_col(x, lo, e):
    sel = lo if e == 0 else jnp.logical_not(lo)
    return jnp.max(jnp.where(sel, x, NEG), axis=-1, keepdims=True)


def _first_head_lanes():
    return lax.broadcasted_iota(jnp.int32, (DIL_BLOCK, DIL_BLOCK), 1) < DIL_HD


def _window_masks():
    i = np.arange(DIL_BLOCK)[:, None]
    j = np.arange(DIL_BLOCK)[None, :]
    cur = j <= i
    both = np.concatenate([j >= i, cur], axis=1)
    first = np.concatenate([np.zeros_like(cur), cur], axis=1)
    return jnp.asarray(np.where(np.stack([both, first]), 0.0, NEG).reshape(2, -1), F32)


def _stack_heads(x, lo):
    return jnp.concatenate([jnp.where(lo, x, 0.0), jnp.where(lo, 0.0, x)], axis=0)


def _unstack_heads(x2, lo):
    return jnp.where(lo, x2[:DIL_BLOCK], x2[DIL_BLOCK:])


def _dil_pairs(d):
    return 4 if d == 1 else 1


def _sub_rows(r, d):
    return pl.ds(r, DIL_BLOCK, stride=d) if d > 1 else pl.ds(0, DIL_BLOCK)


def _store_piece(scratch, i, part, piece):
    if part is None:
        scratch[i] = piece
    else:
        scratch[i, pl.ds(DIL_BLOCK * part, DIL_BLOCK), :] = piece


def _split_subsequences(loads, d, P, stage=None):
    if d == 16:
        group = 4 * DIL_BLOCK
        for block, scratch, part in loads:
            for a in range(4):
                stage[pl.ds(a * group, group), :] = block[pl.ds(a, group, stride=4), :]
            for a in range(4):
                for b in range(4):
                    _store_piece(scratch, a + 4 * b, part, stage[pl.ds(a * group + b, DIL_BLOCK, stride=4), :])
        return
    for r in range(d):
        for p in range(P):
            for block, scratch, part in loads:
                _store_piece(scratch, r * P + p, part, block[_sub_rows(r, d), pl.ds(128 * p, 128)])


def _keep_previous_block(scratches, n):
    for scratch in scratches:
        @pl.when(n == 0)
        def _():
            scratch[:, pl.ds(0, DIL_BLOCK), :] = jnp.zeros((scratch.shape[0], DIL_BLOCK, 128), F32)

        @pl.when(n > 0)
        def _():
            scratch[:, pl.ds(0, DIL_BLOCK), :] = scratch[:, pl.ds(DIL_BLOCK, DIL_BLOCK), :]


def _merge_subsequences(stores, d, P, stage=None):
    if d == 16:
        group = 4 * DIL_BLOCK
        for block, scratch, plus in stores:
            for a in range(4):
                for b in range(4):
                    stage[pl.ds(a * group + b, DIL_BLOCK, stride=4), :] = scratch[a + 4 * b]
            for a in range(4):
                rows = pl.ds(a, group, stride=4)
                val = stage[pl.ds(a * group, group), :]
                block[rows, :] = val if plus is None else val + plus[rows, :]
        return
    for r in range(d):
        for p in range(P):
            for block, scratch, plus in stores:
                part = _sub_rows(r, d), pl.ds(128 * p, 128)
                block[part] = scratch[r * P + p] if plus is None else scratch[r * P + p] + plus[part]


def _dil_fwd(qh, kh, proj, bias, d, prev):
    T = proj.shape[0]
    P = _dil_pairs(d)
    rows, cw, n_it = DIL_BLOCK * d, 128 * P, d * P
    nblk = T // rows
    has_prev = prev is not None

    def body(*refs):
        q_ref, kc_ref, vc_ref, bias_ref = refs[:4]
        refs = refs[4:]
        if has_prev:
            oin_ref, lin_ref = refs[:2]
            refs = refs[2:]
        o_ref, l_ref, stage, qs, ks, vs, os_, ls_ = refs[:8]
        pb, n = pl.program_id(0), pl.program_id(1)
        lo = _first_head_lanes()
        first = (n == 0).astype(jnp.int32)
        _keep_previous_block((ks, vs), n)
        loads = [(q_ref, qs, None), (kc_ref, ks, 1), (vc_ref, vs, 1)]
        if has_prev:
            ois, lis = refs[8:]
            loads += [(oin_ref, ois, None), (lin_ref, lis, None)]
        _split_subsequences(loads, d, P, stage)

        def step(i, carry):
            q2 = _stack_heads(qs[i], lo).astype(BF16)
            s = _dot_nt(q2, ks[i].astype(BF16)) + bias_ref[first, pb * P + i % P]
            m = jnp.max(s, axis=-1, keepdims=True)
            p = jnp.exp(s - m)
            l = jnp.sum(p, axis=-1, keepdims=True)
            o = _unstack_heads(_dot(p.astype(BF16), vs[i].astype(BF16)) / l, lo)
            lse = _unstack_heads(jnp.broadcast_to(m + jnp.log(l), (2 * DIL_BLOCK, 128)), lo)
            if has_prev:
                lin = lis[i]
                mx = jnp.maximum(lin, lse)
                lnew = mx + jnp.log(jnp.exp(lin - mx) + jnp.exp(lse - mx))
                o = ois[i] * jnp.exp(lin - lnew) + o * jnp.exp(lse - lnew)
                lse = lnew
            os_[i] = o
            ls_[i] = lse
            return carry

        lax.fori_loop(0, n_it, step, 0, unroll=min(n_it, 8))
        _merge_subsequences([(o_ref, os_, None), (l_ref, ls_, None)], d, P, stage)

    blk = (rows, cw)
    vcol = 2 * DIL_WIDTH // cw
    tok = pl.BlockSpec(blk, lambda pb, n: (n, pb))
    in_specs = [tok, tok, pl.BlockSpec(blk, lambda pb, n: (n, vcol + pb)),
                pl.BlockSpec(bias.shape, lambda pb, n: (0, 0, 0, 0))]
    args = [qh, kh, proj, bias]
    one, two = pltpu.VMEM((n_it, DIL_BLOCK, 128), F32), pltpu.VMEM((n_it, 2 * DIL_BLOCK, 128), F32)
    scratch = [pltpu.VMEM((rows, 128), F32), one, two, two, one, one]
    if has_prev:
        in_specs += [tok, tok]
        args += list(prev)
        scratch += [one, one]
    out = jax.ShapeDtypeStruct((T, DIL_WIDTH), F32)
    return pl.pallas_call(
        body, name=f"dil_fwd_d{d}", grid=(DIL_HEADS // 2 // P, nblk), in_specs=in_specs, out_specs=[tok, tok],
        out_shape=[out, out], scratch_shapes=scratch, compiler_params=_params(2),
    )(*args)


def _dil_bwd(qh, kh, proj, o, lse, do, bias, d, prev):
    T = proj.shape[0]
    P = _dil_pairs(d)
    rows, cw, n_it = DIL_BLOCK * d, 128 * P, d * P
    nblk = T // rows
    has_prev = prev is not None

    def body(*refs):
        q_ref, kc_ref, vc_ref, o_ref, l_ref, do_ref, bias_ref = refs[:7]
        dqi_ref, dki_ref, dvi_ref = refs[7:10] if has_prev else (None, None, None)
        dq_ref, dk_ref, dv_ref, db_ref, stage, qs, ks, vs, os_, ls_, dos, dqs, dks, dvs, ck, cv = refs[7 + 3 * has_prev:]
        pb, n = pl.program_id(0), pl.program_id(1)
        lo = _first_head_lanes()
        first = (n == 0).astype(jnp.int32)

        @pl.when((pb == 0) & (n == 0))
        def _():
            db_ref[...] = jnp.zeros_like(db_ref)

        @pl.when(n == 0)
        def _():
            ck[...] = jnp.zeros_like(ck)
            cv[...] = jnp.zeros_like(cv)

        _keep_previous_block((ks, vs), n)
        _split_subsequences([(q_ref, qs, None), (kc_ref, ks, 1), (vc_ref, vs, 1),
                             (o_ref, os_, None), (l_ref, ls_, None), (do_ref, dos, None)], d, P, stage)

        def step(i, carry):
            pair = pb * P + i % P
            q2 = _stack_heads(qs[i], lo).astype(BF16)
            kcat, vcat = ks[i].astype(BF16), vs[i].astype(BF16)
            dov = dos[i]
            do2 = _stack_heads(dov, lo).astype(BF16)
            delta = jnp.sum(_stack_heads(dov * os_[i], lo), axis=-1, keepdims=True)
            lse_pair = ls_[i]
            lse2 = jnp.concatenate([_pair_col(lse_pair, lo, 0), _pair_col(lse_pair, lo, 1)], axis=0)
            s = _dot_nt(q2, kcat) + bias_ref[first, pair]
            p = jnp.exp(s - lse2)
            ds = p * (_dot_nt(do2, vcat) - delta)
            db_ref[pair] += ds
            dsb = ds.astype(BF16)
            dqs[i] = _unstack_heads(_dot(dsb, kcat), lo)
            dk2 = _dot_tn(dsb, q2)
            dv2 = _dot_tn(p.astype(BF16), do2)
            dks[i] = ck[i] + dk2[:DIL_BLOCK]
            dvs[i] = cv[i] + dv2[:DIL_BLOCK]
            ck[i] = dk2[DIL_BLOCK:]
            cv[i] = dv2[DIL_BLOCK:]
            return carry

        @pl.when(n < nblk)
        def _():
            lax.fori_loop(0, n_it, step, 0, unroll=min(n_it, 8))
            _merge_subsequences([(dq_ref, dqs, dqi_ref), (dk_ref, dks, dki_ref), (dv_ref, dvs, dvi_ref)], d, P, stage)

        @pl.when(n == nblk)
        def _():
            _merge_subsequences([(dk_ref, ck, dki_ref), (dv_ref, cv, dvi_ref)], d, P, stage)

    blk = (rows, cw)
    vcol = 2 * DIL_WIDTH // cw
    qn_ = lambda n: jnp.minimum(n, nblk - 1)
    pn_ = lambda n: jnp.maximum(n - 1, 0)
    fix3 = lambda pb, n: (0, 0, 0)
    tok_q = pl.BlockSpec(blk, lambda pb, n: (qn_(n), pb))
    tok_p = pl.BlockSpec(blk, lambda pb, n: (pn_(n), pb))
    in_specs = [tok_q, tok_q, pl.BlockSpec(blk, lambda pb, n: (qn_(n), vcol + pb)), tok_q, tok_q, tok_q,
                pl.BlockSpec(bias.shape, lambda pb, n: (0, 0, 0, 0))]
    in_specs += [tok_q, tok_p, tok_p] if has_prev else []
    tok_shape = jax.ShapeDtypeStruct((T, DIL_WIDTH), F32)
    one, two = pltpu.VMEM((n_it, DIL_BLOCK, 128), F32), pltpu.VMEM((n_it, 2 * DIL_BLOCK, 128), F32)
    dq, dk, dv, db = pl.pallas_call(
        body, name=f"dil_bwd_d{d}", grid=(DIL_HEADS // 2 // P, nblk + 1), in_specs=in_specs,
        out_specs=[tok_q, tok_p, tok_p, pl.BlockSpec(bias.shape[1:], fix3)],
        out_shape=[tok_shape, tok_shape, tok_shape, jax.ShapeDtypeStruct(bias.shape[1:], F32)],
        scratch_shapes=[pltpu.VMEM((rows, 128), F32), one, two, two] + [one] * 8,
        compiler_params=_params(2),
    )(qh, kh, proj, o, lse, do, bias, *(prev or ()))
    return (dq, dk, dv), db


def _t5_bucket(dist):
    max_exact = REL_BUCKETS // 2
    dd = np.maximum(dist, 1).astype(np.float32)
    large = max_exact + (np.log(dd / max_exact) / np.log(REL_MAX_DIST / max_exact)
                         * (REL_BUCKETS - max_exact)).astype(np.int32)
    large = np.minimum(large, REL_BUCKETS - 1)
    return np.where(dist < max_exact, dist, large).astype(np.int32)


def _bucket_onehots():
    i = np.arange(DIL_BLOCK)[:, None]
    j = np.arange(DIL_BLOCK)[None, :]
    out = []
    for _, d in DIL_BRANCHES:
        dist = np.concatenate([DIL_BLOCK + i - j, i - j], axis=1)
        bucket = _t5_bucket(np.clip(dist, 0, None) * d).reshape(-1)
        out.append(jnp.asarray(np.eye(REL_BUCKETS, dtype=np.float32)[:, bucket], BF16))
    return out


def _bias_tables(rel_bias, onehots):
    n = len(onehots)

    def body(rb_ref, mask_ref, *refs):
        parts = _split3(rb_ref[...])
        for k in range(n):
            oh = refs[k][...]
            bias = _dot(parts[0], oh) + _dot(parts[1], oh) + _dot(parts[2], oh)
            refs[n + k][0] = bias + mask_ref[0:1, :]
            refs[n + k][1] = bias + mask_ref[1:2, :]

    flat = pl.pallas_call(
        body, name="bias_tables",
        out_shape=[jax.ShapeDtypeStruct((2, DIL_HEADS, 2 * DIL_BLOCK * DIL_BLOCK), F32)] * n,
        compiler_params=pltpu.CompilerParams(vmem_limit_bytes=VMEM_LIMIT),
    )(rel_bias, _window_masks(), *onehots)
    return [t.reshape(2, DIL_HEADS // 2, 2 * DIL_BLOCK, 2 * DIL_BLOCK) for t in flat]


def _bias_grad(dbs, onehots):
    n = len(dbs)
    dbs = [t.reshape(DIL_HEADS, 2 * DIL_BLOCK * DIL_BLOCK) for t in dbs]

    def body(*refs):
        acc = jnp.zeros((DIL_HEADS, REL_BUCKETS), F32)
        for k in range(n):
            oh = refs[n + k][...]
            for part in _split3(refs[k][...]):
                acc = acc + _dot_nt(part, oh)
        refs[-1][...] = acc

    return pl.pallas_call(
        body, name="bias_grad",
        out_shape=jax.ShapeDtypeStruct((DIL_HEADS, REL_BUCKETS), F32),
        compiler_params=pltpu.CompilerParams(vmem_limit_bytes=VMEM_LIMIT),
    )(*dbs, *onehots)


def _swap_halves(x):
    lane = lax.broadcasted_iota(jnp.int32, x.shape, 1)
    first = (lane % 64) < 32
    return jnp.where(first, pltpu.roll(x, 96, 1), pltpu.roll(x, 32, 1))


def _rope_tables(T):
    pos = jnp.arange(T, dtype=F32)
    inv_freq = ROPE_BASE ** (-jnp.arange(0, MLA_ROPE, 2, dtype=F32) / MLA_ROPE)
    ang = pos[:, None] * inv_freq[None, :]
    z = jnp.zeros((T, 128 - MLA_ROPE), F32)
    cos = jnp.concatenate([jnp.cos(ang), jnp.cos(ang), z], axis=-1)
    sin = jnp.concatenate([-jnp.sin(ang), jnp.sin(ang), z], axis=-1)
    return cos, sin


def _mla_prep(proj, cos, sin, g_qa, g_kva, g_q, g_k, wq, wkv, tm=1024):
    T = proj.shape[0]
    H = MLA_HEADS
    scale = MLA_QK ** -0.5

    def body(cq_ref, ckv_ref, kpe_ref, cos_ref, sin_ref, gqa_ref, gkva_ref, gq_ref, gk_ref, wq_ref, wkv_ref,
             q_ref, k_ref, v_ref):
        cosv, sinv = cos_ref[...], sin_ref[...]

        def rope(x):
            return x * cosv + _swap_halves(x) * sinv

        cq = cq_ref[...]
        qp = _dot_nt((cq * _rstd(cq) * gqa_ref[...]).astype(BF16), wq_ref[...])
        ckv = ckv_ref[...]
        kvp = _dot((ckv * _rstd(ckv) * gkva_ref[...]).astype(BF16), wkv_ref[...])
        kpe = kpe_ref[...]
        one_hot_lane = (lax.broadcasted_iota(jnp.int32, (tm, 128), 1) == 0).astype(BF16)
        for h in range(H):
            a = qp[:, MLA_PAD * h:MLA_PAD * (h + 1)]
            qn = a * _rstd(a, MLA_QK) * gq_ref[...]
            q_ref[h, :, 0:128] = (qn[:, 0:128] * scale).astype(BF16)
            q_ref[h, :, 128:256] = (rope(qn[:, 128:256]) * scale).astype(BF16)
            kn = kvp[:, MLA_PAD * h:MLA_PAD * h + 128]
            r = lax.rsqrt((jnp.sum(kn * kn, axis=-1, keepdims=True)
                           + jnp.sum(kpe * kpe, axis=-1, keepdims=True)) / MLA_QK + EPS)
            k_ref[h, :, 0:128] = (kn * r * gk_ref[:, 0:128]).astype(BF16)
            k_ref[h, :, 128:256] = rope(kpe * r * gk_ref[:, 128:256]).astype(BF16)
            v_ref[h, :, 0:128] = kvp[:, MLA_PAD * h + 128:MLA_PAD * (h + 1)].astype(BF16)
            v_ref[h, :, 128:256] = one_hot_lane

    fix = lambda i: (0, 0)
    return pl.pallas_call(
        body, name="mla_prep", grid=(T // tm,),
        in_specs=[pl.BlockSpec((tm, MLA_Q_RANK), lambda i: (i, CQ_COL // MLA_Q_RANK)),
                  pl.BlockSpec((tm, MLA_KV_RANK), lambda i: (i, CKV_COL // MLA_KV_RANK)),
                  pl.BlockSpec((tm, 128), lambda i: (i, KPE_COL // 128)),
                  pl.BlockSpec((tm, 128), lambda i: (i, 0)), pl.BlockSpec((tm, 128), lambda i: (i, 0)),
                  pl.BlockSpec((1, 256), fix), pl.BlockSpec((1, 128), fix),
                  pl.BlockSpec((1, 256), fix), pl.BlockSpec((1, 256), fix),
                  pl.BlockSpec((H * MLA_PAD, 256), fix), pl.BlockSpec((128, H * MLA_PAD), fix)],
        out_specs=[pl.BlockSpec((H, tm, MLA_PAD), lambda i: (0, i, 0)), pl.BlockSpec((H, tm, MLA_PAD), lambda i: (0, i, 0)),
                   pl.BlockSpec((H, tm, 2 * MLA_V), lambda i: (0, i, 0))],
        out_shape=[jax.ShapeDtypeStruct((H, T, MLA_PAD), BF16), jax.ShapeDtypeStruct((H, T, MLA_PAD), BF16),
                   jax.ShapeDtypeStruct((H, T, 2 * MLA_V), BF16)],
        compiler_params=_params(1),
    )(proj, proj, proj, cos, sin, g_qa, g_kva, g_q, g_k, wq, wkv)


def _mla_prep_bwd(proj, cos, sin, g_qa, g_kva, g_q, g_k, wq, wkv, dq, dk, dv, tm=1024):
    T = proj.shape[0]
    H = MLA_HEADS
    scale = MLA_QK ** -0.5

    def body(cq_ref, ckv_ref, kpe_ref, cos_ref, sin_ref, gqa_ref, gkva_ref, gq_ref, gk_ref, wq_ref, wkv_ref,
             dq_ref, dk_ref, dv_ref,
             dcq_ref, dckv_ref, dkpe_ref, cqn_ref, ckvn_ref, dqp_ref, dkvp_ref,
             dgqa_ref, dgkva_ref, dgq_ref, dgk_ref):
        @pl.when(pl.program_id(0) == 0)
        def _():
            for ref in (dgqa_ref, dgkva_ref, dgq_ref, dgk_ref):
                ref[...] = jnp.zeros_like(ref)

        cosv, sinv = cos_ref[...], sin_ref[...]

        def rope_bwd(dy):
            return dy * cosv + _swap_halves(dy * sinv)

        cq = cq_ref[...]
        rcq = _rstd(cq)
        cqn = (cq * rcq * gqa_ref[...]).astype(BF16)
        cqn_ref[...] = cqn
        qp = _dot_nt(cqn, wq_ref[...])
        ckv = ckv_ref[...]
        rckv = _rstd(ckv)
        ckvn = (ckv * rckv * gkva_ref[...]).astype(BF16)
        ckvn_ref[...] = ckvn
        kvp = _dot(ckvn, wkv_ref[...])
        kpe = kpe_ref[...]
        dkpe = jnp.zeros_like(kpe)
        dgq = jnp.zeros((1, MLA_PAD), F32)
        dgk = jnp.zeros((1, MLA_PAD), F32)
        for h in range(H):
            a = qp[:, MLA_PAD * h:MLA_PAD * (h + 1)]
            dqh = dq_ref[h]
            dn = jnp.concatenate([dqh[:, 0:128], rope_bwd(dqh[:, 128:256])], axis=-1) * scale
            da, dg = _rms_bwd(dn, a, gq_ref[...], _rstd(a, MLA_QK), MLA_QK)
            dgq = dgq + jnp.sum(dg, axis=0, keepdims=True)
            dqp_ref[:, MLA_PAD * h:MLA_PAD * (h + 1)] = da.astype(BF16)

            ak = jnp.concatenate([kvp[:, MLA_PAD * h:MLA_PAD * h + 128], kpe], axis=-1)
            dkh = dk_ref[h]
            dnk = jnp.concatenate([dkh[:, 0:128], rope_bwd(dkh[:, 128:256])], axis=-1)
            dak, dg = _rms_bwd(dnk, ak, gk_ref[...], _rstd(ak, MLA_QK), MLA_QK)
            dgk = dgk + jnp.sum(dg, axis=0, keepdims=True)
            dkpe = dkpe + dak[:, 128:256]
            dkvp_ref[:, MLA_PAD * h:MLA_PAD * h + 128] = dak[:, 0:128].astype(BF16)
            dkvp_ref[:, MLA_PAD * h + 128:MLA_PAD * (h + 1)] = dv_ref[h].astype(BF16)
        dkpe_ref[...] = dkpe
        dgq_ref[...] += dgq
        dgk_ref[...] += dgk
        dcq, dg = _rms_bwd(_dot(dqp_ref[...], wq_ref[...]), cq, gqa_ref[...], rcq)
        dcq_ref[...] = dcq
        dgqa_ref[...] += jnp.sum(dg, axis=0, keepdims=True)
        dckv, dg = _rms_bwd(_dot_nt(dkvp_ref[...], wkv_ref[...]), ckv, gkva_ref[...], rckv)
        dckv_ref[...] = dckv
        dgkva_ref[...] += jnp.sum(dg, axis=0, keepdims=True)

    fix = lambda i: (0, 0)
    row = lambda i: (i, 0)
    head = lambda i: (0, i, 0)
    return pl.pallas_call(
        body, name="mla_prep_bwd", grid=(T // tm,),
        in_specs=[pl.BlockSpec((tm, MLA_Q_RANK), lambda i: (i, CQ_COL // MLA_Q_RANK)),
                  pl.BlockSpec((tm, MLA_KV_RANK), lambda i: (i, CKV_COL // MLA_KV_RANK)),
                  pl.BlockSpec((tm, 128), lambda i: (i, KPE_COL // 128)),
                  pl.BlockSpec((tm, 128), row), pl.BlockSpec((tm, 128), row),
                  pl.BlockSpec((1, 256), fix), pl.BlockSpec((1, 128), fix),
                  pl.BlockSpec((1, 256), fix), pl.BlockSpec((1, 256), fix),
                  pl.BlockSpec((H * MLA_PAD, 256), fix), pl.BlockSpec((128, H * MLA_PAD), fix),
                  pl.BlockSpec((H, tm, MLA_PAD), head), pl.BlockSpec((H, tm, MLA_PAD), head),
                  pl.BlockSpec((H, tm, MLA_V), head)],
        out_specs=[pl.BlockSpec((tm, 256), row), pl.BlockSpec((tm, 128), row), pl.BlockSpec((tm, 128), row),
                   pl.BlockSpec((tm, 256), row), pl.BlockSpec((tm, 128), row),
                   pl.BlockSpec((tm, H * MLA_PAD), row), pl.BlockSpec((tm, H * MLA_PAD), row),
                   pl.BlockSpec((1, 256), fix), pl.BlockSpec((1, 128), fix),
                   pl.BlockSpec((1, 256), fix), pl.BlockSpec((1, 256), fix)],
        out_shape=[jax.ShapeDtypeStruct((T, 256), F32), jax.ShapeDtypeStruct((T, 128), F32),
                   jax.ShapeDtypeStruct((T, 128), F32),
                   jax.ShapeDtypeStruct((T, 256), BF16), jax.ShapeDtypeStruct((T, 128), BF16),
                   jax.ShapeDtypeStruct((T, H * MLA_PAD), BF16), jax.ShapeDtypeStruct((T, H * MLA_PAD), BF16),
                   jax.ShapeDtypeStruct((1, 256), F32), jax.ShapeDtypeStruct((1, 128), F32),
                   jax.ShapeDtypeStruct((1, 256), F32), jax.ShapeDtypeStruct((1, 256), F32)],
        compiler_params=_params(1),
    )(proj, proj, proj, cos, sin, g_qa, g_kva, g_q, g_k, wq, wkv, dq, dk, dv)


def _causal_pairs(T, tq, tk, key_major):
    pairs = [(i, j) for i in range(T // tq) for j in range(T // tk) if j * tk <= i * tq + tq - 1]
    if key_major:
        pairs.sort(key=lambda p: (p[1], p[0]))
    outer = [p[1] if key_major else p[0] for p in pairs]
    first = [int(t == 0 or outer[t] != outer[t - 1]) for t in range(len(pairs))]
    last = [int(t == len(pairs) - 1 or outer[t] != outer[t + 1]) for t in range(len(pairs))]
    tab = lambda v: jnp.asarray(np.array(v, np.int32))
    return tab([p[0] for p in pairs]), tab([p[1] for p in pairs]), tab(first), tab(last)


def _causal_scores(qv, kv, row0):
    s = _dot_nt(qv, kv)
    if row0 is not None:
        row = lax.broadcasted_iota(jnp.int32, s.shape, 0) + row0
        col = lax.broadcasted_iota(jnp.int32, s.shape, 1)
        s = jnp.where(col <= row, s, NEG)
    return s


def _causal_variants(qi, ki, tq, tk, update):
    assert tk % tq == 0
    diag = qi * tq - ki * tk
    for off in range(0, tk, tq):
        pl.when(diag == off)(lambda off=off: update(off))
    pl.when(diag >= tk)(lambda: update(None))


def _visible_keys(off, row0, rows, tk):
    return tk if off is None else min(tk, off + row0 + rows)


def _mla_attn(q, k, v, ride=None, tq=2048, tk=4096, rc=256):
    H, T, _ = q.shape
    tq, tk = min(tq, T), min(tk, T)
    tables = _causal_pairs(T, tq, tk, key_major=False)
    n_pairs = int(tables[0].shape[0])
    r_args, r_in, r_shape, r_out, r_scratch = _ride_parts(ride)

    def body(qt, kt, ft, lt, q_ref, k_ref, v_ref, o_ref, lse_ref, m_s, acc):
        t = pl.program_id(1)
        qi, ki = qt[t], kt[t]

        @pl.when(ft[t] == 1)
        def _():
            m_s[...] = jnp.full_like(m_s, NEG)
            acc[...] = jnp.zeros_like(acc)

        def update(off):
            for c in range(tq // rc):
                rows = pl.ds(c * rc, rc)
                keys = pl.ds(0, _visible_keys(off, c * rc, rc, tk))
                s = _causal_scores(q_ref[rows, :], k_ref[keys, :], None if off is None else off + c * rc)
                m_old = m_s[rows, :]
                m_new = jnp.maximum(m_old, jnp.max(s, axis=-1, keepdims=True))
                p = jnp.exp(s - m_new).astype(BF16)
                acc[rows, :] = jnp.exp(m_old - m_new) * acc[rows, :] + _dot(p, v_ref[keys, :])
                m_s[rows, :] = m_new

        _causal_variants(qi, ki, tq, tk, update)

        @pl.when(lt[t] == 1)
        def _():
            l = jnp.max(acc[:, MLA_V:], axis=-1, keepdims=True)
            o_ref[...] = acc[:, :MLA_V] / l
            lse_ref[...] = jnp.broadcast_to(m_s[...] + jnp.log(l), lse_ref.shape)

    qrow = lambda h, t, qt, kt, ft, lt: (h, qt[t], 0)
    krow = lambda h, t, qt, kt, ft, lt: (h, kt[t], 0)
    first = lambda: (pl.program_id(0) == 0) & (pl.program_id(1) == 0)
    last = lambda: (pl.program_id(0) == H - 1) & (pl.program_id(1) == n_pairs - 1)
    outs = pl.pallas_call(
        _riding(body, 7, 2, 2, ride, first, last), name="mla_attn",
        grid_spec=pltpu.PrefetchScalarGridSpec(
            num_scalar_prefetch=4, grid=(H, n_pairs),
            in_specs=[pl.BlockSpec((None, tq, MLA_PAD), qrow), pl.BlockSpec((None, tk, MLA_PAD), krow),
                      pl.BlockSpec((None, tk, 2 * MLA_V), krow)] + r_in,
            out_specs=[pl.BlockSpec((tq, MLA_V), lambda h, t, qt, kt, ft, lt: (qt[t], h)),
                       pl.BlockSpec((None, tq, 128), qrow)] + r_out,
            scratch_shapes=[pltpu.VMEM((tq, 1), F32), pltpu.VMEM((tq, 2 * MLA_V), F32)] + r_scratch),
        out_shape=[jax.ShapeDtypeStruct((T, H * MLA_V), F32), jax.ShapeDtypeStruct((H, T, 128), F32)] + r_shape,
        compiler_params=_params(2),
    )(*tables, q, k, v, *r_args)
    return outs[:2], outs[2:]


def _mla_attn_bwd(q, k, v, o, lse, do, ride=None, tq=2048, tk=2048, rc=512, rc_diagonal=256):
    H, T, _ = q.shape
    tq, tk = min(tq, T), min(tk, T)
    tables = _causal_pairs(T, tq, tk, key_major=True)
    n_pairs = int(tables[0].shape[0])
    r_args, r_in, r_shape, r_out, r_scratch = _ride_parts(ride)

    def body(qt, kt, ft, lt, q_ref, k_ref, v_ref, o_ref, lse_ref, do_ref, dq_ref, dk_ref, dv_ref, dk_s, dv_s):
        t = pl.program_id(1)
        qi, ki = qt[t], kt[t]

        @pl.when(t == 0)
        def _():
            dq_ref[...] = jnp.zeros_like(dq_ref)

        @pl.when(ft[t] == 1)
        def _():
            dk_s[...] = jnp.zeros_like(dk_s)
            dv_s[...] = jnp.zeros_like(dv_s)

        def update(off):
            rows_per = rc if off is None else rc_diagonal
            for c in range(tq // rows_per):
                rows = pl.ds(c * rows_per, rows_per)
                keys = pl.ds(0, _visible_keys(off, c * rows_per, rows_per, tk))
                kk, vv = k_ref[keys, :], v_ref[keys, :]
                qv, dov = q_ref[rows, :], do_ref[rows, :]
                delta = jnp.sum(dov * o_ref[rows, :], axis=-1, keepdims=True)
                lse_v = jnp.max(lse_ref[rows, :], axis=-1, keepdims=True)
                p = jnp.exp(_causal_scores(qv, kk, None if off is None else off + c * rows_per) - lse_v)
                dob = dov.astype(BF16)
                dv_s[keys, :] += _dot_tn(p.astype(BF16), dob)
                ds = (p * (_dot_nt(dob, vv) - delta)).astype(BF16)
                dk_s[keys, :] += _dot_tn(ds, qv)
                out_rows = pl.ds(pl.multiple_of(qi * tq + c * rows_per, rows_per), rows_per)
                dq_ref[out_rows, :] += _dot(ds, kk)

        _causal_variants(qi, ki, tq, tk, update)

        @pl.when(lt[t] == 1)
        def _():
            dk_ref[...] = dk_s[...]
            dv_ref[...] = dv_s[...]

    qrow = lambda h, t, qt, kt, ft, lt: (h, qt[t], 0)
    krow = lambda h, t, qt, kt, ft, lt: (h, kt[t], 0)
    qcol = lambda h, t, qt, kt, ft, lt: (qt[t], h)
    first = lambda: (pl.program_id(0) == 0) & (pl.program_id(1) == 0)
    last = lambda: (pl.program_id(0) == H - 1) & (pl.program_id(1) == n_pairs - 1)
    outs = pl.pallas_call(
        _riding(body, 10, 3, 2, ride, first, last), name="mla_attn_bwd",
        grid_spec=pltpu.PrefetchScalarGridSpec(
            num_scalar_prefetch=4, grid=(H, n_pairs),
            in_specs=[pl.BlockSpec((None, tq, MLA_PAD), qrow), pl.BlockSpec((None, tk, MLA_PAD), krow),
                      pl.BlockSpec((None, tk, MLA_V), krow), pl.BlockSpec((tq, MLA_V), qcol),
                      pl.BlockSpec((None, tq, 128), qrow), pl.BlockSpec((tq, MLA_V), qcol)] + r_in,
            out_specs=[pl.BlockSpec((None, T, MLA_PAD), lambda h, t, qt, kt, ft, lt: (h, 0, 0)),
                       pl.BlockSpec((None, tk, MLA_PAD), krow), pl.BlockSpec((None, tk, MLA_V), krow)] + r_out,
            scratch_shapes=[pltpu.VMEM((tk, MLA_PAD), F32), pltpu.VMEM((tk, MLA_V), F32)] + r_scratch),
        out_shape=[jax.ShapeDtypeStruct((H, T, MLA_PAD), F32), jax.ShapeDtypeStruct((H, T, MLA_PAD), F32),
                   jax.ShapeDtypeStruct((H, T, MLA_V), F32)] + r_shape,
        compiler_params=_params(2),
    )(*tables, q, k, v, o, lse, do, *r_args)
    return outs[:3], outs[3:]


def _pair_gain(g):
    return jnp.tile(g.reshape(1, DIL_HD), (1, 2))


def _pad_gain(g):
    return jnp.pad(g.reshape(1, MLA_QK), ((0, 0), (0, MLA_PAD - MLA_QK)))


def _local_step(x, target, s, comm):
    T = x.shape[0]
    w = comm.w
    gq, gk = _pair_gain(s["dil_q_norm"]) * DIL_HD ** -0.5, _pair_gain(s["dil_k_norm"])
    g_q, g_k = _pad_gain(s["mla_q_norm"]), _pad_gain(s["mla_k_norm"])
    cos, sin = _rope_tables(T)
    onehots = _bucket_onehots()
    biases = _bias_tables(s["rel_bias"], onehots)

    (x1, h1, gate1, up1), got = _ffn_fwd(x, s["ffn1_norm"], w["ffn1_w_gate"], w["ffn1_w_up"], w["ffn1_w_down"],
                                         ride=comm.gather(_GROUPS["attn"]))
    comm.weights_landed(_GROUPS["attn"], got)
    hm, proj, qh, kh = _in_proj(x1, s["mix_norm"], w["w_in"], gq, gk)
    dil = None
    for (_, d), bias in zip(DIL_BRANCHES, biases):
        dil = _dil_fwd(qh, kh, proj, bias, d, dil)
    o_dil, lse_dil = dil
    q, k, v = _mla_prep(proj, cos, sin, s["mla_q_a_norm"], s["mla_kv_a_norm"], g_q, g_k, w["mla_w_q_b"], w["mla_w_kv_b"])
    (o_mla, lse_mla), got = _mla_attn(q, k, v, ride=comm.gather(_GROUPS["ffn2"]))
    comm.weights_landed(_GROUPS["ffn2"], got)
    x2, oc = _out_proj(x1, o_dil, o_mla, s["out_norm_dil"], s["out_norm_mla"], w["w_out"])
    (dy, h2, gate2, up2, loss), _ = _ffn_fwd(x2, s["ffn2_norm"], w["ffn2_w_gate"], w["ffn2_w_up"], w["ffn2_w_down"],
                                             target=target)

    gw, gs = {}, {}

    def ffn_grads(name, dy_in, x_in, h, gate, up, early=None):
        dx, a, dg, du, dyh, dgain = _ffn_bwd(dy_in, x_in, s[name + "_norm"], gate, up,
                                             w[name + "_w_gate"], w[name + "_w_up"], w[name + "_w_down"])
        gs[name + "_norm"] = dgain
        down, gate_n, up_n = (name + "_w_down",), (name + "_w_gate",), (name + "_w_up",)
        ride = lambda names: comm.scatter(names, gw) if early is not None else None
        gw[down[0]], landed = _matmul_tn(a, dyh, 1408, 1024, ride=ride(early))
        comm.grads_landed(early or (), landed)
        gw[gate_n[0]], landed = _matmul_tn(dg, h, 1408, 1024, ride=ride(down))
        comm.grads_landed(down, landed)
        gw[up_n[0]], landed = _matmul_tn(du, h, 1408, 1024, ride=ride(gate_n))
        comm.grads_landed(gate_n, landed)
        return dx

    dx2 = ffn_grads("ffn2", dy, x2, h2, gate2, up2)
    gw["w_out"], _ = _matmul_tn(oc, dx2, 1024, 1024)
    do_dil, do_mla, gs["out_norm_dil"], gs["out_norm_mla"] = _out_proj_bwd(
        dx2, o_dil, o_mla, s["out_norm_dil"], s["out_norm_mla"], w["w_out"])

    (dq, dk, dv), got = _mla_attn_bwd(q, k, v, o_mla, lse_mla, do_mla, ride=comm.scatter(_GROUPS["ffn2"], gw))
    comm.grads_landed(_GROUPS["ffn2"], got)
    (dcq, dckv, dkpe, cqn, ckvn, dqp, dkvp, gs["mla_q_a_norm"], gs["mla_kv_a_norm"], dg_q, dg_k) = _mla_prep_bwd(
        proj, cos, sin, s["mla_q_a_norm"], s["mla_kv_a_norm"], g_q, g_k, w["mla_w_q_b"], w["mla_w_kv_b"], dq, dk, dv)
    gs["mla_q_norm"], gs["mla_k_norm"] = dg_q[:, :MLA_QK], dg_k[:, :MLA_QK]
    gw["mla_w_q_b"], _ = _matmul_tn(dqp, cqn, 1024, 256)
    gw["mla_w_kv_b"], _ = _matmul_tn(ckvn, dkvp, 128, 1024)

    dqkv, dbs = None, []
    for (_, d), bias in reversed(list(zip(DIL_BRANCHES, biases))):
        dqkv, db = _dil_bwd(qh, kh, proj, o_dil, lse_dil, do_dil, bias, d, dqkv)
        dbs.insert(0, db)
    dqkv = [dqkv]
    gs["rel_bias"] = _bias_grad(dbs, onehots)

    ready = tuple(n for n in _GROUPS["attn"] if n != "w_in")
    (dx1, dproj, gs["mix_norm"], dgq, dgk), got = _in_proj_bwd(dx2, x1, s["mix_norm"], w["w_in"], proj, gq, gk,
                                                               dqkv, dcq, dckv, dkpe, ride=comm.scatter(ready, gw))
    comm.grads_landed(ready, got)
    gs["dil_q_norm"] = (dgq[:, :DIL_HD] + dgq[:, DIL_HD:]) * DIL_HD ** -0.5
    gs["dil_k_norm"] = dgk[:, :DIL_HD] + dgk[:, DIL_HD:]
    gw["w_in"], _ = _matmul_tn(dproj, hm, 1024, 1024)
    grad_x = ffn_grads("ffn1", dx1, x, h1, gate1, up1, early=("w_in",))
    return loss, grad_x, gw, gs


def _position():
    x, y, c = lax.axis_index("x"), lax.axis_index("y"), lax.axis_index("c")
    return x, y, c, 4 * x + 2 * y + c


def _peer(x, y, c, k):
    px = 1 - x if k & 4 else x
    py = 1 - y if k & 2 else y
    pc = 1 - c if k & 1 else c
    return (px, py, pc), 4 * px + 2 * py + pc


class _Ride:
    def __init__(self, arrays, scatter):
        self.arrays, self.scatter = list(arrays), list(scatter)
        self.n = n = len(self.arrays)
        self.specs = [pl.BlockSpec(memory_space=pl.ANY)] * n
        self.out_shape = [jax.ShapeDtypeStruct(a.shape if sc else (N_DEV,) + a.shape, a.dtype)
                          for a, sc in zip(self.arrays, self.scatter)]
        self.scratch = [pltpu.SemaphoreType.DMA((n, N_DEV - 1)), pltpu.SemaphoreType.DMA((n, N_DEV - 1)),
                        pltpu.SemaphoreType.DMA((n,))]

    def _copies(self, ins, outs, sems):
        send_sems, recv_sems, local_sems = sems
        x, y, c, me = _position()
        copies = []
        for a in range(self.n):
            src = ins[a].at[me] if self.scatter[a] else ins[a]
            copies.append(pltpu.make_async_copy(src, outs[a].at[me], local_sems.at[a]))
        for k in range(1, N_DEV):
            peer, peer_idx = _peer(x, y, c, k)
            for a in range(self.n):
                src = ins[a].at[peer_idx] if self.scatter[a] else ins[a]
                copies.append(pltpu.make_async_remote_copy(
                    src_ref=src, dst_ref=outs[a].at[me], send_sem=send_sems.at[a, k - 1], recv_sem=recv_sems.at[a, k - 1],
                    device_id=peer, device_id_type=pl.DeviceIdType.MESH))
        return copies

    def start(self, ins, outs, sems):
        for cp in self._copies(ins, outs, sems):
            cp.start()

    def wait(self, ins, outs, sems):
        for cp in self._copies(ins, outs, sems):
            cp.wait()


def _ride_parts(ride):
    if ride is None:
        return [], [], [], [], []
    return ride.arrays, ride.specs, ride.out_shape, ride.specs, ride.scratch


def _riding(body, n_in, n_out, n_scratch, ride, first, last):
    if ride is None:
        return body
    n = ride.n
    i1, i2 = n_in + n, n_in + n + n_out
    i3, i4 = i2 + n, i2 + n + n_scratch

    def wrapped(*refs):
        ins, outs, sems = refs[n_in:i1], refs[i2:i3], refs[i4:]

        @pl.when(first())
        def _():
            ride.start(ins, outs, sems)

        body(*refs[:n_in], *refs[i1:i2], *refs[i3:i4])

        @pl.when(last())
        def _():
            ride.wait(ins, outs, sems)

    return wrapped


def _gather_two_level(arrays, name):
    n = len(arrays)
    out_shape = [jax.ShapeDtypeStruct((N_DEV,) + a.shape, a.dtype) for a in arrays]

    def body(*refs):
        ins, outs = refs[:n], refs[n:2 * n]
        send_sems, recv_sems, local_sems = refs[2 * n:]
        x, y, c, me = _position()
        sibling = (x, y, 1 - c)
        chips = [(1 - x, y), (x, 1 - y), (1 - x, 1 - y)]
        block = lambda px, py, pc: 4 * px + 2 * py + pc

        def copy(a, k, blk, to, src=None):
            dst = outs[a].at[blk]
            return pltpu.make_async_remote_copy(
                src_ref=dst if src is None else src, dst_ref=dst, send_sem=send_sems.at[a, k], recv_sem=recv_sems.at[a, k],
                device_id=to, device_id_type=pl.DeviceIdType.MESH)

        local = [pltpu.make_async_copy(ins[a], outs[a].at[me], local_sems.at[a]) for a in range(n)]
        first = []
        for a in range(n):
            first.append(copy(a, 0, me, sibling, src=ins[a]))
            first += [copy(a, 1 + j, me, (*chip, c), src=ins[a]) for j, chip in enumerate(chips)]
        for cp in local + first:
            cp.start()
        passed = []
        for j, chip in enumerate(chips):
            for a in range(n):
                copy(a, 1 + j, block(*chip, c), sibling).wait_recv()
                passed.append(copy(a, 4 + j, block(*chip, c), sibling))
                passed[-1].start()
        for a in range(n):
            copy(a, 0, block(x, y, 1 - c), sibling).wait_recv()
            for j, chip in enumerate(chips):
                copy(a, 4 + j, block(*chip, 1 - c), sibling).wait_recv()
        for cp in first + passed:
            cp.wait_send()
        for cp in local:
            cp.wait()

    any_spec = [pl.BlockSpec(memory_space=pl.ANY)] * n
    return pl.pallas_call(
        body, name=name, in_specs=any_spec, out_specs=any_spec, out_shape=out_shape,
        scratch_shapes=[pltpu.SemaphoreType.DMA((n, N_DEV - 1)), pltpu.SemaphoreType.DMA((n, N_DEV - 1)),
                        pltpu.SemaphoreType.DMA((n,))],
    )(*arrays)


def _adamw_math(wv, g, m, v):
    m = ADAM_B1 * m + (1.0 - ADAM_B1) * g
    v = ADAM_B2 * v + (1.0 - ADAM_B2) * (g * g)
    m_hat = m / (1.0 - ADAM_B1 ** ADAM_STEP)
    v_hat = v / (1.0 - ADAM_B2 ** ADAM_STEP)
    delta = -ADAM_LR * (m_hat / (jnp.sqrt(v_hat) + ADAM_EPS) + ADAM_WD * wv)
    return delta, m, v


def _adamw(items, ride=None, max_rows=256):
    K = len(items)
    tiles, spans, start = [], [], 0
    for _, wv, _, _ in items:
        R = wv.shape[1]
        tr = max([t for t in range(16, max_rows + 1, 16) if R % t == 0] or [R])
        tiles.append(tr)
        spans.append((start, R // tr))
        start += R // tr
    total = start
    r_args, r_in, r_shape, r_out, r_scratch = _ride_parts(ride)

    def body(*refs):
        i = pl.program_id(0)
        for k, (first_step, n_steps) in enumerate(spans):
            def update(k=k):
                p_ref, w_ref, m_ref, v_ref = refs[4 * k:4 * k + 4]
                g_ref, d_ref, mo_ref, vo_ref = refs[4 * K + 4 * k:4 * K + 4 * k + 4]
                g = p_ref[0].astype(F32)
                for j in range(1, N_DEV):
                    g = g + p_ref[j].astype(F32)
                d, mn, vn = _adamw_math(w_ref[0], g, m_ref[0], v_ref[0])
                g_ref[0] = g
                d_ref[0] = d
                mo_ref[0] = mn
                vo_ref[0] = vn

            pl.when((i >= first_step) & (i < first_step + n_steps))(update)

    in_specs, out_specs, out_shape, args = [], [], [], []
    for (parts, wv, m, v), tr, (first_step, n_steps) in zip(items, tiles, spans):
        C = wv.shape[2]
        tile = lambda i, s=first_step, n=n_steps: (0, jnp.clip(i - s, 0, n - 1), 0)
        blk = pl.BlockSpec((1, tr, C), tile)
        in_specs += [pl.BlockSpec((N_DEV, tr, C), tile), blk, blk, blk]
        out_specs += [blk] * 4
        out_shape += [jax.ShapeDtypeStruct(wv.shape, F32)] * 4
        args += [parts, wv, m, v]
    outs = pl.pallas_call(
        _riding(body, 4 * K, 4 * K, 0, ride, lambda: pl.program_id(0) == 0, lambda: pl.program_id(0) == total - 1),
        name="adamw", grid=(total,),
        in_specs=in_specs + r_in, out_specs=out_specs + r_out, out_shape=out_shape + r_shape,
        scratch_shapes=r_scratch, compiler_params=_params(1),
    )(*args, *r_args)
    return [outs[4 * k:4 * k + 4] for k in range(K)], outs[4 * K:]


_TRANSPOSED = ("ffn1_w_gate", "ffn1_w_up", "ffn2_w_gate", "ffn2_w_up", "w_in", "mla_w_q_b")
_GROUPS = {"ffn1": ("ffn1_w_gate", "ffn1_w_up", "ffn1_w_down"),
           "ffn2": ("ffn2_w_gate", "ffn2_w_up", "ffn2_w_down"),
           "attn": ("w_in", "mla_w_q_b", "mla_w_kv_b", "w_out")}
_SMALL = ("ffn1_norm", "mix_norm", "ffn2_norm", "out_norm_dil", "out_norm_mla", "mla_q_a_norm", "rel_bias",
          "mla_q_norm", "mla_k_norm", "mla_kv_a_norm", "dil_q_norm", "dil_k_norm")
_SMALL_ROWS = 48


def _cols_to_full(g):
    return g.transpose(1, 0, 2).reshape(g.shape[1], N_DEV * g.shape[2])


def _full_to_cols(f):
    return f.reshape(f.shape[0], N_DEV, f.shape[1] // N_DEV).transpose(1, 0, 2)


def _shard_view(name, a):
    return jnp.swapaxes(a, 1, 2) if name in _TRANSPOSED else a


def _to_full(name, g):
    if name == "mla_w_kv_b":
        return _cols_to_full(g)
    f = g.reshape(-1, g.shape[-1])
    if name == "w_in":
        f = jnp.pad(f, ((0, PROJ_PAD - PROJ_COLS), (0, 0)))
    if name == "mla_w_q_b":
        f = jnp.pad(f.reshape(MLA_HEADS, MLA_QK, -1), ((0, 0), (0, MLA_PAD - MLA_QK), (0, 0)))
        f = f.reshape(MLA_HEADS * MLA_PAD, -1)
    return f


def _to_parts(name, f):
    if name == "mla_w_kv_b":
        return _full_to_cols(f).astype(BF16)
    if name == "w_in":
        f = f[:PROJ_COLS]
    if name == "mla_w_q_b":
        f = f.reshape(MLA_HEADS, MLA_PAD, -1)[:, :MLA_QK].reshape(MLA_HEADS * MLA_QK, -1)
    return f.reshape(N_DEV, -1, f.shape[-1]).astype(BF16)


class _Comm:
    def __init__(self, shards):
        self.shards, self.w, self.recv = shards, {}, {}

    def gather(self, names):
        return _Ride([self.shards[n] for n in names], [False] * len(names))

    def scatter(self, names, grads):
        return _Ride([_to_parts(n, grads[n]) for n in names], [True] * len(names))

    def weights_landed(self, names, got):
        self.w.update({n: _to_full(n, g) for n, g in zip(names, got)})

    def grads_landed(self, names, got):
        self.recv.update(zip(names, got))


def _pack_small(parts, extra):
    flat = jnp.concatenate([parts[n].reshape(-1) for n in _SMALL] + [extra.reshape(-1)])
    return jnp.pad(flat, (0, _SMALL_ROWS * 128 - flat.shape[0])).reshape(_SMALL_ROWS, 128)


def _unpack_small(packed, shapes):
    flat, out, off = packed.reshape(-1), {}, 0
    for n in _SMALL:
        size = math.prod(shapes[n])
        out[n] = flat[off:off + size].reshape(shapes[n])
        off += size
    return out, flat[off]


_NAMES = ("ffn1_norm", "ffn1_w_gate", "ffn1_w_up", "ffn1_w_down", "mix_norm", "w_in", "dil_q_norm", "dil_k_norm",
          "rel_bias", "mla_q_a_norm", "mla_w_q_b", "mla_kv_a_norm", "mla_w_kv_b", "mla_q_norm", "mla_k_norm",
          "out_norm_dil", "out_norm_mla", "w_out", "ffn2_norm", "ffn2_w_gate", "ffn2_w_up", "ffn2_w_down")


def kernel(x, ffn1_norm, ffn1_w_gate, ffn1_w_up, ffn1_w_down, mix_norm, w_in, dil_q_norm, dil_k_norm, rel_bias, mla_q_a_norm, mla_w_q_b, mla_kv_a_norm, mla_w_kv_b, mla_q_norm, mla_k_norm, out_norm_dil, out_norm_mla, w_out, ffn2_norm, ffn2_w_gate, ffn2_w_up, ffn2_w_down, loss_target, m_ffn1_norm, m_ffn1_w_gate, m_ffn1_w_up, m_ffn1_w_down, m_mix_norm, m_w_in, m_dil_q_norm, m_dil_k_norm, m_rel_bias, m_mla_q_a_norm, m_mla_w_q_b, m_mla_kv_a_norm, m_mla_w_kv_b, m_mla_q_norm, m_mla_k_norm, m_out_norm_dil, m_out_norm_mla, m_w_out, m_ffn2_norm, m_ffn2_w_gate, m_ffn2_w_up, m_ffn2_w_down, v_ffn1_norm, v_ffn1_w_gate, v_ffn1_w_up, v_ffn1_w_down, v_mix_norm, v_w_in, v_dil_q_norm, v_dil_k_norm, v_rel_bias, v_mla_q_a_norm, v_mla_w_q_b, v_mla_kv_a_norm, v_mla_w_kv_b, v_mla_q_norm, v_mla_k_norm, v_out_norm_dil, v_out_norm_mla, v_w_out, v_ffn2_norm, v_ffn2_w_gate, v_ffn2_w_up, v_ffn2_w_down):
    args = locals()
    wts = {n: args[n] for n in _NAMES}
    mom = {n: args["m_" + n] for n in _NAMES}
    var = {n: args["v_" + n] for n in _NAMES}

    matrices = [n for group in _GROUPS.values() for n in group]
    comm = _Comm({n: _shard_view(n, wts[n])[0].astype(BF16) for n in matrices})
    comm.weights_landed(_GROUPS["ffn1"], _gather_two_level(comm.gather(_GROUPS["ffn1"]).arrays, "gather_first"))
    small = {n: wts[n].reshape(1, -1) if n != "rel_bias" else wts[n] for n in _SMALL}

    loss, grad_x, gw, gs = _local_step(x[0], loss_target[0], small, comm)

    item = lambda n: (comm.recv[n],) + tuple(_shard_view(n, a[n]) for a in (wts, mom, var))
    landed = [n for n in matrices if n != "ffn1_w_up"]
    last = comm.scatter(("ffn1_w_up",), gw)
    updates, got = _adamw([item(n) for n in landed], max_rows=32,
                          ride=_Ride(last.arrays + [_pack_small(gs, loss[0, 0])], last.scatter + [False]))
    comm.grads_landed(("ffn1_w_up",), got[:-1])

    zero = jnp.zeros((), F32)
    small_item = (got[-1],) + tuple(_pack_small(a, zero)[None] for a in (wts, mom, var))
    (up_update, packed), _ = _adamw([item("ffn1_w_up"), small_item])
    res = {n: [_shard_view(n, r) for r in u] for n, u in zip(landed + ["ffn1_w_up"], updates + [up_update])}
    shapes = {n: wts[n].shape for n in _SMALL}
    loss_total = None
    for slot, q in enumerate(packed):
        vals, extra = _unpack_small(q, shapes)
        if slot == 0:
            loss_total = extra
        for n in _SMALL:
            res.setdefault(n, [None] * 4)[slot] = vals[n]
    outs = [loss_total, grad_x[None]]
    for slot in range(4):
        outs += [res[n][slot].reshape(wts[n].shape) for n in _NAMES]
    return tuple(outs)
```

```python
import math

import numpy as np
import jax
import jax.numpy as jnp
from jax import lax
from jax.experimental import pallas as pl
from jax.experimental.pallas import tpu as pltpu

F32, BF16 = jnp.float32, jnp.bfloat16
EPS = 1e-6
NEG = -1e30
N_DEV = 8

DIL_HEADS, DIL_HD = 8, 64
DIL_WIDTH = DIL_HEADS * DIL_HD
DIL_BRANCHES = ((128, 1), (512, 4), (2048, 16))
DIL_BLOCK = 128
MLA_HEADS, MLA_NOPE, MLA_ROPE, MLA_V = 4, 128, 64, 128
MLA_QK = MLA_NOPE + MLA_ROPE
MLA_PAD = 256
ROPE_BASE = 10000.0
REL_BUCKETS, REL_MAX_DIST = 32, 2048
MLA_Q_RANK, MLA_KV_RANK = 256, 128
PROJ_COLS, PROJ_PAD = 1984, 2048
CQ_COL = 3 * DIL_WIDTH
CKV_COL, KPE_COL = CQ_COL + MLA_Q_RANK, CQ_COL + MLA_Q_RANK + MLA_KV_RANK
FFN_RESID = 0.5
ADAM_LR, ADAM_B1, ADAM_B2, ADAM_EPS, ADAM_WD, ADAM_STEP = 0.001, 0.9, 0.999, 1e-08, 0.01, 10
VMEM_LIMIT = 62 * 1024 * 1024

_NT = (((1,), (1,)), ((), ()))
_TN = (((0,), (0,)), ((), ()))


def _dot(a, b):
    return jnp.dot(a, b, preferred_element_type=F32)


def _dot_nt(a, b):
    return lax.dot_general(a, b, _NT, preferred_element_type=F32)


def _dot_tn(a, b):
    return lax.dot_general(a, b, _TN, preferred_element_type=F32)


def _params(n_axes):
    return pltpu.CompilerParams(dimension_semantics=("arbitrary",) * n_axes, vmem_limit_bytes=VMEM_LIMIT)


def _rstd(x, n=None):
    n = x.shape[-1] if n is None else n
    return lax.rsqrt(jnp.sum(x * x, axis=-1, keepdims=True) / n + EPS)


def _rms_bwd(dy, x, g, r, n=None):
    n = x.shape[-1] if n is None else n
    u = dy * g
    dx = r * u - x * (r * r * r) * (jnp.sum(u * x, axis=-1, keepdims=True) / n)
    return dx, dy * x * r


def _sigmoid(x):
    return 1.0 / (1.0 + jnp.exp(-x))


def _split3(x):
    parts = []
    for _ in range(3):
        xb = x.astype(BF16)
        parts.append(xb)
        x = x - xb.astype(F32)
    return parts


def _ffn_fwd(x, gain, wg, wu, wd, ride=None, target=None, tm=512, tf=2816):
    T, D = x.shape
    F = wg.shape[0]
    ni, nj = T // tm, F // tf
    with_loss = target is not None
    r_args, r_in, r_shape, r_out, r_scratch = _ride_parts(ride)

    def body(*refs):
        x_ref, g_ref, wg_ref, wu_ref, wd_ref = refs[:5]
        t_ref = refs[5] if with_loss else None
        xo_ref, h_ref, gate_ref, up_ref = refs[5 + with_loss:9 + with_loss]
        loss_ref = refs[-2] if with_loss else None
        acc = refs[-1]
        i, j = pl.program_id(0), pl.program_id(1)

        @pl.when(j == 0)
        def _():
            xv = x_ref[...]
            h_ref[...] = (xv * _rstd(xv) * g_ref[...]).astype(BF16)
            if nj > 1:
                acc[...] = jnp.zeros_like(acc)

        h = h_ref[...]
        g = _dot_nt(h, wg_ref[...])
        u = _dot_nt(h, wu_ref[...])
        gate_ref[...] = g.astype(BF16)
        up_ref[...] = u.astype(BF16)
        a = (g * _sigmoid(g) * u).astype(BF16)
        part = _dot(a, wd_ref[...])
        if nj > 1:
            acc[...] += part

        @pl.when(j == nj - 1)
        def _():
            y = x_ref[...] + FFN_RESID * (acc[...] if nj > 1 else part)
            if with_loss:
                @pl.when(i == 0)
                def _():
                    loss_ref[...] = jnp.zeros_like(loss_ref)

                e = y - t_ref[...]
                xo_ref[...] = e * (1.0 / D)
                loss_ref[...] += (0.5 / D) * jnp.sum(e * e)
            else:
                xo_ref[...] = y

    row = lambda i, j: (i, 0)
    tile = lambda i, j: (i, j)
    n_in, n_out = 5 + with_loss, 4 + with_loss
    first = lambda: (pl.program_id(0) == 0) & (pl.program_id(1) == 0)
    last = lambda: (pl.program_id(0) == ni - 1) & (pl.program_id(1) == nj - 1)
    outs = pl.pallas_call(
        _riding(body, n_in, n_out, 1, ride, first, last), name="ffn_fwd", grid=(ni, nj),
        in_specs=[pl.BlockSpec((tm, D), row), pl.BlockSpec((1, D), lambda i, j: (0, 0)),
                  pl.BlockSpec((tf, D), lambda i, j: (j, 0)), pl.BlockSpec((tf, D), lambda i, j: (j, 0)),
                  pl.BlockSpec((tf, D), lambda i, j: (j, 0))] + [pl.BlockSpec((tm, D), row)] * with_loss + r_in,
        out_specs=[pl.BlockSpec((tm, D), row), pl.BlockSpec((tm, D), row), pl.BlockSpec((tm, tf), tile),
                   pl.BlockSpec((tm, tf), tile)] + [pl.BlockSpec((1, 128), lambda i, j: (0, 0))] * with_loss + r_out,
        out_shape=[jax.ShapeDtypeStruct((T, D), F32), jax.ShapeDtypeStruct((T, D), BF16),
                   jax.ShapeDtypeStruct((T, F), BF16), jax.ShapeDtypeStruct((T, F), BF16)]
        + [jax.ShapeDtypeStruct((1, 128), F32)] * with_loss + r_shape,
        scratch_shapes=[pltpu.VMEM((tm, D), F32)] + r_scratch,
        compiler_params=_params(2),
    )(x, gain, wg, wu, wd, *([target] if with_loss else []), *r_args)
    return outs[:n_out], outs[n_out:]


def _ffn_bwd(dy, x, gain, gate, up, wg, wu, wd, tm=256, tf=2816):
    T, D = x.shape
    F = wg.shape[0]
    ni, nj = T // tm, F // tf

    def body(dy_ref, x_ref, g_ref, gate_ref, up_ref, wg_ref, wu_ref, wd_ref,
             dx_ref, a_ref, dg_ref, du_ref, dyh_ref, dgain_ref, acc):
        i, j = pl.program_id(0), pl.program_id(1)

        @pl.when((i == 0) & (j == 0))
        def _():
            dgain_ref[...] = jnp.zeros_like(dgain_ref)

        @pl.when(j == 0)
        def _():
            dyh_ref[...] = (FFN_RESID * dy_ref[...]).astype(BF16)
            if nj > 1:
                acc[...] = jnp.zeros_like(acc)

        da = _dot_nt(dyh_ref[...], wd_ref[...])
        g = gate_ref[...].astype(F32)
        u = up_ref[...].astype(F32)
        sig = _sigmoid(g)
        s = g * sig
        a_ref[...] = (s * u).astype(BF16)
        dg = (da * u * (sig * (1.0 + g * (1.0 - sig)))).astype(BF16)
        du = (da * s).astype(BF16)
        dg_ref[...] = dg
        du_ref[...] = du
        part = _dot(dg, wg_ref[...]) + _dot(du, wu_ref[...])
        if nj > 1:
            acc[...] += part

        @pl.when(j == nj - 1)
        def _():
            xv = x_ref[...]
            dxn, dgc = _rms_bwd(acc[...] if nj > 1 else part, xv, g_ref[...], _rstd(xv))
            dx_ref[...] = dy_ref[...] + dxn
            dgain_ref[...] += jnp.sum(dgc, axis=0, keepdims=True)

    return pl.pallas_call(
        body, name="ffn_bwd", grid=(ni, nj),
        in_specs=[pl.BlockSpec((tm, D), lambda i, j: (i, 0)), pl.BlockSpec((tm, D), lambda i, j: (i, 0)),
                  pl.BlockSpec((1, D), lambda i, j: (0, 0)),
                  pl.BlockSpec((tm, tf), lambda i, j: (i, j)), pl.BlockSpec((tm, tf), lambda i, j: (i, j)),
                  pl.BlockSpec((tf, D), lambda i, j: (j, 0)), pl.BlockSpec((tf, D), lambda i, j: (j, 0)),
                  pl.BlockSpec((tf, D), lambda i, j: (j, 0))],
        out_specs=[pl.BlockSpec((tm, D), lambda i, j: (i, 0)),
                   pl.BlockSpec((tm, tf), lambda i, j: (i, j)), pl.BlockSpec((tm, tf), lambda i, j: (i, j)),
                   pl.BlockSpec((tm, tf), lambda i, j: (i, j)),
                   pl.BlockSpec((tm, D), lambda i, j: (i, 0)), pl.BlockSpec((1, D), lambda i, j: (0, 0))],
        out_shape=[jax.ShapeDtypeStruct((T, D), F32), jax.ShapeDtypeStruct((T, F), BF16),
                   jax.ShapeDtypeStruct((T, F), BF16), jax.ShapeDtypeStruct((T, F), BF16),
                   jax.ShapeDtypeStruct((T, D), BF16), jax.ShapeDtypeStruct((1, D), F32)],
        scratch_shapes=[pltpu.VMEM((tm, D), F32)],
        compiler_params=_params(2),
    )(dy, x, gain, gate, up, wg, wu, wd)


def _matmul_tn(a, b, tk, tn, ride=None, tt=2048):
    T, K = a.shape
    N = b.shape[1]
    tk, tn = min(tk, K), min(tn, N)
    grid = (K // tk, N // tn, T // tt)
    r_args, r_in, r_shape, r_out, r_scratch = _ride_parts(ride)

    def body(a_ref, b_ref, o_ref, acc):
        t = pl.program_id(2)

        @pl.when(t == 0)
        def _():
            acc[...] = jnp.zeros_like(acc)

        acc[...] += _dot_tn(a_ref[...].astype(BF16), b_ref[...].astype(BF16))

        @pl.when(t == grid[2] - 1)
        def _():
            o_ref[...] = acc[...].astype(BF16)

    first = lambda: (pl.program_id(0) == 0) & (pl.program_id(1) == 0) & (pl.program_id(2) == 0)
    last = lambda: ((pl.program_id(0) == grid[0] - 1) & (pl.program_id(1) == grid[1] - 1)
                    & (pl.program_id(2) == grid[2] - 1))
    outs = pl.pallas_call(
        _riding(body, 2, 1, 1, ride, first, last), name="matmul_tn", grid=grid,
        in_specs=[pl.BlockSpec((tt, tk), lambda k, n, t: (t, k)), pl.BlockSpec((tt, tn), lambda k, n, t: (t, n))] + r_in,
        out_specs=[pl.BlockSpec((tk, tn), lambda k, n, t: (k, n))] + r_out,
        out_shape=[jax.ShapeDtypeStruct((K, N), BF16)] + r_shape,
        scratch_shapes=[pltpu.VMEM((tk, tn), F32)] + r_scratch,
        compiler_params=_params(3),
    )(a, b, *r_args)
    return outs[0], outs[1:]


def _in_proj(x, gain, w, gq, gk, tm=1024):
    T, D = x.shape
    N = w.shape[0]
    W = DIL_WIDTH

    def body(x_ref, g_ref, w_ref, gq_ref, gk_ref, h_ref, p_ref, qh_ref, kh_ref):
        xv = x_ref[...]
        h = (xv * _rstd(xv) * g_ref[...]).astype(BF16)
        h_ref[...] = h
        p_ref[...] = _dot_nt(h, w_ref[...])
        lo = lax.broadcasted_iota(jnp.int32, (tm, 128), 1) < DIL_HD
        for hp in range(DIL_HEADS // 2):
            q = p_ref[:, 128 * hp:128 * (hp + 1)]
            k = p_ref[:, W + 128 * hp:W + 128 * (hp + 1)]
            qh_ref[:, 128 * hp:128 * (hp + 1)] = (q * _pair_rstd(q, lo) * gq_ref[...]).astype(BF16).astype(F32)
            kh_ref[:, 128 * hp:128 * (hp + 1)] = (k * _pair_rstd(k, lo) * gk_ref[...]).astype(BF16).astype(F32)

    row = lambda i: (i, 0)
    fix = lambda i: (0, 0)
    return pl.pallas_call(
        body, name="in_proj", grid=(T // tm,),
        in_specs=[pl.BlockSpec((tm, D), row), pl.BlockSpec((1, D), fix), pl.BlockSpec((N, D), fix),
                  pl.BlockSpec((1, 128), fix), pl.BlockSpec((1, 128), fix)],
        out_specs=[pl.BlockSpec((tm, D), row), pl.BlockSpec((tm, N), row), pl.BlockSpec((tm, W), row),
                   pl.BlockSpec((tm, W), row)],
        out_shape=[jax.ShapeDtypeStruct((T, D), BF16), jax.ShapeDtypeStruct((T, N), F32),
                   jax.ShapeDtypeStruct((T, W), F32), jax.ShapeDtypeStruct((T, W), F32)],
        compiler_params=_params(1),
    )(x, gain, w, gq, gk)


def _in_proj_bwd(dx_up, x, gain, w, proj, gq, gk, dqkv, dcq, dckv, dkpe, ride=None, tm=512):
    T, D = x.shape
    N = w.shape[0]
    W = DIL_WIDTH
    nb = len(dqkv)

    def body(*refs):
        dxu_ref, x_ref, g_ref, w_ref, q_ref, k_ref, gq_ref, gk_ref = refs[:8]
        dil_refs = refs[8:8 + 3 * nb]
        dcq_ref, dckv_ref, dkpe_ref, dx_ref, dp_ref, dgain_ref, dgq_ref, dgk_ref = refs[8 + 3 * nb:]

        @pl.when(pl.program_id(0) == 0)
        def _():
            for ref in (dgain_ref, dgq_ref, dgk_ref):
                ref[...] = jnp.zeros_like(ref)

        lo = lax.broadcasted_iota(jnp.int32, (tm, 128), 1) < DIL_HD
        norms = ((q_ref, gq_ref, dgq_ref), (k_ref, gk_ref, dgk_ref))
        for part in range(3):
            acc = dil_refs[part][...]
            for b in range(1, nb):
                acc = acc + dil_refs[3 * b + part][...]
            if part == 2:
                dp_ref[:, 2 * W:3 * W] = acc.astype(BF16)
                continue
            raw_ref, gn_ref, dgn_ref = norms[part]
            for hp in range(DIL_HEADS // 2):
                raw = raw_ref[:, 128 * hp:128 * (hp + 1)]
                d_raw, dgn = _pair_rms_bwd(acc[:, 128 * hp:128 * (hp + 1)], raw, _pair_rstd(raw, lo), gn_ref[...], lo)
                dp_ref[:, part * W + 128 * hp:part * W + 128 * (hp + 1)] = d_raw.astype(BF16)
                dgn_ref[...] += dgn
        dp_ref[:, 3 * W:3 * W + 256] = dcq_ref[...].astype(BF16)
        dp_ref[:, 3 * W + 256:3 * W + 384] = dckv_ref[...].astype(BF16)
        dp_ref[:, 3 * W + 384:N] = dkpe_ref[...].astype(BF16)
        dh = _dot(dp_ref[...], w_ref[...])
        xv = x_ref[...]
        dxn, dgc = _rms_bwd(dh, xv, g_ref[...], _rstd(xv))
        dx_ref[...] = dxu_ref[...] + dxn
        dgain_ref[...] += jnp.sum(dgc, axis=0, keepdims=True)

    row = lambda i: (i, 0)
    fix = lambda i: (0, 0)
    r_args, r_in, r_shape, r_out, r_scratch = _ride_parts(ride)
    first = lambda: pl.program_id(0) == 0
    last = lambda: pl.program_id(0) == T // tm - 1
    outs = pl.pallas_call(
        _riding(body, 11 + 3 * nb, 5, 0, ride, first, last), name="in_proj_bwd", grid=(T // tm,),
        in_specs=[pl.BlockSpec((tm, D), row), pl.BlockSpec((tm, D), row), pl.BlockSpec((1, D), fix),
                  pl.BlockSpec((N, D), fix), pl.BlockSpec((tm, W), row), pl.BlockSpec((tm, W), lambda i: (i, 1)),
                  pl.BlockSpec((1, 128), fix), pl.BlockSpec((1, 128), fix)] + [pl.BlockSpec((tm, W), row)] * (3 * nb)
                 + [pl.BlockSpec((tm, 256), row), pl.BlockSpec((tm, 128), row), pl.BlockSpec((tm, 128), row)] + r_in,
        out_specs=[pl.BlockSpec((tm, D), row), pl.BlockSpec((tm, N), row), pl.BlockSpec((1, D), fix),
                   pl.BlockSpec((1, 128), fix), pl.BlockSpec((1, 128), fix)] + r_out,
        out_shape=[jax.ShapeDtypeStruct((T, D), F32), jax.ShapeDtypeStruct((T, N), BF16),
                   jax.ShapeDtypeStruct((1, D), F32), jax.ShapeDtypeStruct((1, 128), F32),
                   jax.ShapeDtypeStruct((1, 128), F32)] + r_shape,
        scratch_shapes=r_scratch,
        compiler_params=_params(1),
    )(dx_up, x, gain, w, proj, proj, gq, gk, *[a for triple in dqkv for a in triple], dcq, dckv, dkpe, *r_args)
    return outs[:5], outs[5:]


def _out_proj(x, o_dil, o_mla, g_dil, g_mla, w, tm=1024):
    T, D = x.shape
    W = o_dil.shape[1]

    def body(x_ref, od_ref, om_ref, gd_ref, gm_ref, w_ref, xo_ref, oc_ref):
        od, om = od_ref[...], om_ref[...]
        oc_ref[:, 0:W] = (od * _rstd(od) * gd_ref[...]).astype(BF16)
        oc_ref[:, W:2 * W] = (om * _rstd(om) * gm_ref[...]).astype(BF16)
        xo_ref[...] = x_ref[...] + _dot(oc_ref[...], w_ref[...])

    row = lambda i: (i, 0)
    fix = lambda i: (0, 0)
    return pl.pallas_call(
        body, name="out_proj", grid=(T // tm,),
        in_specs=[pl.BlockSpec((tm, D), row), pl.BlockSpec((tm, W), row), pl.BlockSpec((tm, W), row),
                  pl.BlockSpec((1, W), fix), pl.BlockSpec((1, W), fix), pl.BlockSpec((2 * W, D), fix)],
        out_specs=[pl.BlockSpec((tm, D), row), pl.BlockSpec((tm, 2 * W), row)],
        out_shape=[jax.ShapeDtypeStruct((T, D), F32), jax.ShapeDtypeStruct((T, 2 * W), BF16)],
        compiler_params=_params(1),
    )(x, o_dil, o_mla, g_dil, g_mla, w)


def _out_proj_bwd(dx, o_dil, o_mla, g_dil, g_mla, w, tm=1024):
    T, D = dx.shape
    W = o_dil.shape[1]

    def body(dx_ref, od_ref, om_ref, gd_ref, gm_ref, w_ref, dod_ref, dom_ref, dgd_ref, dgm_ref):
        @pl.when(pl.program_id(0) == 0)
        def _():
            dgd_ref[...] = jnp.zeros_like(dgd_ref)
            dgm_ref[...] = jnp.zeros_like(dgm_ref)

        doc = _dot_nt(dx_ref[...].astype(BF16), w_ref[...])
        od, om = od_ref[...], om_ref[...]
        dod, dgd = _rms_bwd(doc[:, 0:W], od, gd_ref[...], _rstd(od))
        dom, dgm = _rms_bwd(doc[:, W:2 * W], om, gm_ref[...], _rstd(om))
        dod_ref[...] = dod
        dom_ref[...] = dom
        dgd_ref[...] += jnp.sum(dgd, axis=0, keepdims=True)
        dgm_ref[...] += jnp.sum(dgm, axis=0, keepdims=True)

    row = lambda i: (i, 0)
    fix = lambda i: (0, 0)
    return pl.pallas_call(
        body, name="out_proj_bwd", grid=(T // tm,),
        in_specs=[pl.BlockSpec((tm, D), row), pl.BlockSpec((tm, W), row), pl.BlockSpec((tm, W), row),
                  pl.BlockSpec((1, W), fix), pl.BlockSpec((1, W), fix), pl.BlockSpec((2 * W, D), fix)],
        out_specs=[pl.BlockSpec((tm, W), row), pl.BlockSpec((tm, W), row),
                   pl.BlockSpec((1, W), fix), pl.BlockSpec((1, W), fix)],
        out_shape=[jax.ShapeDtypeStruct((T, W), F32), jax.ShapeDtypeStruct((T, W), F32),
                   jax.ShapeDtypeStruct((1, W), F32), jax.ShapeDtypeStruct((1, W), F32)],
        compiler_params=_params(1),
    )(dx, o_dil, o_mla, g_dil, g_mla, w)


def _pair_rstd(x, lo):
    sq = x * x
    s0 = jnp.sum(jnp.where(lo, sq, 0.0), axis=-1, keepdims=True)
    s1 = jnp.sum(jnp.where(lo, 0.0, sq), axis=-1, keepdims=True)
    return jnp.where(lo, lax.rsqrt(s0 / DIL_HD + EPS), lax.rsqrt(s1 / DIL_HD + EPS))


def _pair_rms_bwd(dn, x, r, g, lo):
    u = dn * g
    t = u * x
    d0 = jnp.sum(jnp.where(lo, t, 0.0), axis=-1, keepdims=True)
    d1 = jnp.sum(jnp.where(lo, 0.0, t), axis=-1, keepdims=True)
    dx = r * u - x * (r * r * r) * (jnp.where(lo, d0, d1) / DIL_HD)
    return dx, jnp.sum(dn * x * r, axis=0, keepdims=True)


def _pair_col(x, lo, e):
    sel = lo if e == 0 else jnp.logical_not(lo)
    return jnp.max(jnp.where(sel, x, NEG), axis=-1, keepdims=True)


def _first_head_lanes():
    return lax.broadcasted_iota(jnp.int32, (DIL_BLOCK, DIL_BLOCK), 1) < DIL_HD


def _window_masks():
    i = np.arange(DIL_BLOCK)[:, None]
    j = np.arange(DIL_BLOCK)[None, :]
    cur = j <= i
    both = np.concatenate([j >= i, cur], axis=1)
    first = np.concatenate([np.zeros_like(cur), cur], axis=1)
    return jnp.asarray(np.where(np.stack([both, first]), 0.0, NEG).reshape(2, -1), F32)


def _stack_heads(x, lo):
    return jnp.concatenate([jnp.where(lo, x, 0.0), jnp.where(lo, 0.0, x)], axis=0)


def _unstack_heads(x2, lo):
    return jnp.where(lo, x2[:DIL_BLOCK], x2[DIL_BLOCK:])


def _dil_pairs(d):
    return 4 if d == 1 else 1


def _sub_rows(r, d):
    return pl.ds(r, DIL_BLOCK, stride=d) if d > 1 else pl.ds(0, DIL_BLOCK)


def _store_piece(scratch, i, part, piece):
    if part is None:
        scratch[i] = piece
    else:
        scratch[i, pl.ds(DIL_BLOCK * part, DIL_BLOCK), :] = piece


def _split_subsequences(loads, d, P, stage=None):
    if d == 16:
        group = 4 * DIL_BLOCK
        for block, scratch, part in loads:
            for a in range(4):
                stage[pl.ds(a * group, group), :] = block[pl.ds(a, group, stride=4), :]
            for a in range(4):
                for b in range(4):
                    _store_piece(scratch, a + 4 * b, part, stage[pl.ds(a * group + b, DIL_BLOCK, stride=4), :])
        return
    for r in range(d):
        for p in range(P):
            for block, scratch, part in loads:
                _store_piece(scratch, r * P + p, part, block[_sub_rows(r, d), pl.ds(128 * p, 128)])


def _keep_previous_block(scratches, n):
    for scratch in scratches:
        @pl.when(n == 0)
        def _():
            scratch[:, pl.ds(0, DIL_BLOCK), :] = jnp.zeros((scratch.shape[0], DIL_BLOCK, 128), F32)

        @pl.when(n > 0)
        def _():
            scratch[:, pl.ds(0, DIL_BLOCK), :] = scratch[:, pl.ds(DIL_BLOCK, DIL_BLOCK), :]


def _merge_subsequences(stores, d, P, stage=None):
    if d == 16:
        group = 4 * DIL_BLOCK
        for block, scratch, plus in stores:
            for a in range(4):
                for b in range(4):
                    stage[pl.ds(a * group + b, DIL_BLOCK, stride=4), :] = scratch[a + 4 * b]
            for a in range(4):
                rows = pl.ds(a, group, stride=4)
                val = stage[pl.ds(a * group, group), :]
                block[rows, :] = val if plus is None else val + plus[rows, :]
        return
    for r in range(d):
        for p in range(P):
            for block, scratch, plus in stores:
                part = _sub_rows(r, d), pl.ds(128 * p, 128)
                block[part] = scratch[r * P + p] if plus is None else scratch[r * P + p] + plus[part]


def _dil_fwd(qh, kh, proj, bias, d, prev):
    T = proj.shape[0]
    P = _dil_pairs(d)
    rows, cw, n_it = DIL_BLOCK * d, 128 * P, d * P
    nblk = T // rows
    has_prev = prev is not None

    def body(*refs):
        q_ref, kc_ref, vc_ref, bias_ref = refs[:4]
        refs = refs[4:]
        if has_prev:
            oin_ref, lin_ref = refs[:2]
            refs = refs[2:]
        o_ref, l_ref, stage, qs, ks, vs, os_, ls_ = refs[:8]
        pb, n = pl.program_id(0), pl.program_id(1)
        lo = _first_head_lanes()
        first = (n == 0).astype(jnp.int32)
        _keep_previous_block((ks, vs), n)
        loads = [(q_ref, qs, None), (kc_ref, ks, 1), (vc_ref, vs, 1)]
        if has_prev:
            ois, lis = refs[8:]
            loads += [(oin_ref, ois, None), (lin_ref, lis, None)]
        _split_subsequences(loads, d, P, stage)

        def step(i, carry):
            q2 = _stack_heads(qs[i], lo).astype(BF16)
            s = _dot_nt(q2, ks[i].astype(BF16)) + bias_ref[first, pb * P + i % P]
            m = jnp.max(s, axis=-1, keepdims=True)
            p = jnp.exp(s - m)
            l = jnp.sum(p, axis=-1, keepdims=True)
            o = _unstack_heads(_dot(p.astype(BF16), vs[i].astype(BF16)) / l, lo)
            lse = _unstack_heads(jnp.broadcast_to(m + jnp.log(l), (2 * DIL_BLOCK, 128)), lo)
            if has_prev:
                lin = lis[i]
                mx = jnp.maximum(lin, lse)
                lnew = mx + jnp.log(jnp.exp(lin - mx) + jnp.exp(lse - mx))
                o = ois[i] * jnp.exp(lin - lnew) + o * jnp.exp(lse - lnew)
                lse = lnew
            os_[i] = o
            ls_[i] = lse
            return carry

        lax.fori_loop(0, n_it, step, 0, unroll=min(n_it, 8))
        _merge_subsequences([(o_ref, os_, None), (l_ref, ls_, None)], d, P, stage)

    blk = (rows, cw)
    vcol = 2 * DIL_WIDTH // cw
    tok = pl.BlockSpec(blk, lambda pb, n: (n, pb))
    in_specs = [tok, tok, pl.BlockSpec(blk, lambda pb, n: (n, vcol + pb)),
                pl.BlockSpec(bias.shape, lambda pb, n: (0, 0, 0, 0))]
    args = [qh, kh, proj, bias]
    one, two = pltpu.VMEM((n_it, DIL_BLOCK, 128), F32), pltpu.VMEM((n_it, 2 * DIL_BLOCK, 128), F32)
    scratch = [pltpu.VMEM((rows, 128), F32), one, two, two, one, one]
    if has_prev:
        in_specs += [tok, tok]
        args += list(prev)
        scratch += [one, one]
    out = jax.ShapeDtypeStruct((T, DIL_WIDTH), F32)
    return pl.pallas_call(
        body, name=f"dil_fwd_d{d}", grid=(DIL_HEADS // 2 // P, nblk), in_specs=in_specs, out_specs=[tok, tok],
        out_shape=[out, out], scratch_shapes=scratch, compiler_params=_params(2),
    )(*args)


def _dil_bwd(qh, kh, proj, o, lse, do, bias, d, prev):
    T = proj.shape[0]
    P = _dil_pairs(d)
    rows, cw, n_it = DIL_BLOCK * d, 128 * P, d * P
    nblk = T // rows
    has_prev = prev is not None

    def body(*refs):
        q_ref, kc_ref, vc_ref, o_ref, l_ref, do_ref, bias_ref = refs[:7]
        dqi_ref, dki_ref, dvi_ref = refs[7:10] if has_prev else (None, None, None)
        dq_ref, dk_ref, dv_ref, db_ref, stage, qs, ks, vs, os_, ls_, dos, dqs, dks, dvs, ck, cv = refs[7 + 3 * has_prev:]
        pb, n = pl.program_id(0), pl.program_id(1)
        lo = _first_head_lanes()
        first = (n == 0).astype(jnp.int32)

        @pl.when((pb == 0) & (n == 0))
        def _():
            db_ref[...] = jnp.zeros_like(db_ref)

        @pl.when(n == 0)
        def _():
            ck[...] = jnp.zeros_like(ck)
            cv[...] = jnp.zeros_like(cv)

        _keep_previous_block((ks, vs), n)
        _split_subsequences([(q_ref, qs, None), (kc_ref, ks, 1), (vc_ref, vs, 1),
                             (o_ref, os_, None), (l_ref, ls_, None), (do_ref, dos, None)], d, P, stage)

        def step(i, carry):
            pair = pb * P + i % P
            q2 = _stack_heads(qs[i], lo).astype(BF16)
            kcat, vcat = ks[i].astype(BF16), vs[i].astype(BF16)
            dov = dos[i]
            do2 = _stack_heads(dov, lo).astype(BF16)
            delta = jnp.sum(_stack_heads(dov * os_[i], lo), axis=-1, keepdims=True)
            lse_pair = ls_[i]
            lse2 = jnp.concatenate([_pair_col(lse_pair, lo, 0), _pair_col(lse_pair, lo, 1)], axis=0)
            s = _dot_nt(q2, kcat) + bias_ref[first, pair]
            p = jnp.exp(s - lse2)
            ds = p * (_dot_nt(do2, vcat) - delta)
            db_ref[pair] += ds
            dsb = ds.astype(BF16)
            dqs[i] = _unstack_heads(_dot(dsb, kcat), lo)
            dk2 = _dot_tn(dsb, q2)
            dv2 = _dot_tn(p.astype(BF16), do2)
            dks[i] = ck[i] + dk2[:DIL_BLOCK]
            dvs[i] = cv[i] + dv2[:DIL_BLOCK]
            ck[i] = dk2[DIL_BLOCK:]
            cv[i] = dv2[DIL_BLOCK:]
            return carry

        @pl.when(n < nblk)
        def _():
            lax.fori_loop(0, n_it, step, 0, unroll=min(n_it, 8))
            _merge_subsequences([(dq_ref, dqs, dqi_ref), (dk_ref, dks, dki_ref), (dv_ref, dvs, dvi_ref)], d, P, stage)

        @pl.when(n == nblk)
        def _():
            _merge_subsequences([(dk_ref, ck, dki_ref), (dv_ref, cv, dvi_ref)], d, P, stage)

    blk = (rows, cw)
    vcol = 2 * DIL_WIDTH // cw
    qn_ = lambda n: jnp.minimum(n, nblk - 1)
    pn_ = lambda n: jnp.maximum(n - 1, 0)
    fix3 = lambda pb, n: (0, 0, 0)
    tok_q = pl.BlockSpec(blk, lambda pb, n: (qn_(n), pb))
    tok_p = pl.BlockSpec(blk, lambda pb, n: (pn_(n), pb))
    in_specs = [tok_q, tok_q, pl.BlockSpec(blk, lambda pb, n: (qn_(n), vcol + pb)), tok_q, tok_q, tok_q,
                pl.BlockSpec(bias.shape, lambda pb, n: (0, 0, 0, 0))]
    in_specs += [tok_q, tok_p, tok_p] if has_prev else []
    tok_shape = jax.ShapeDtypeStruct((T, DIL_WIDTH), F32)
    one, two = pltpu.VMEM((n_it, DIL_BLOCK, 128), F32), pltpu.VMEM((n_it, 2 * DIL_BLOCK, 128), F32)
    dq, dk, dv, db = pl.pallas_call(
        body, name=f"dil_bwd_d{d}", grid=(DIL_HEADS // 2 // P, nblk + 1), in_specs=in_specs,
        out_specs=[tok_q, tok_p, tok_p, pl.BlockSpec(bias.shape[1:], fix3)],
        out_shape=[tok_shape, tok_shape, tok_shape, jax.ShapeDtypeStruct(bias.shape[1:], F32)],
        scratch_shapes=[pltpu.VMEM((rows, 128), F32), one, two, two] + [one] * 8,
        compiler_params=_params(2),
    )(qh, kh, proj, o, lse, do, bias, *(prev or ()))
    return (dq, dk, dv), db


def _t5_bucket(dist):
    max_exact = REL_BUCKETS // 2
    dd = np.maximum(dist, 1).astype(np.float32)
    large = max_exact + (np.log(dd / max_exact) / np.log(REL_MAX_DIST / max_exact)
                         * (REL_BUCKETS - max_exact)).astype(np.int32)
    large = np.minimum(large, REL_BUCKETS - 1)
    return np.where(dist < max_exact, dist, large).astype(np.int32)


def _bucket_onehots():
    i = np.arange(DIL_BLOCK)[:, None]
    j = np.arange(DIL_BLOCK)[None, :]
    out = []
    for _, d in DIL_BRANCHES:
        dist = np.concatenate([DIL_BLOCK + i - j, i - j], axis=1)
        bucket = _t5_bucket(np.clip(dist, 0, None) * d).reshape(-1)
        out.append(jnp.asarray(np.eye(REL_BUCKETS, dtype=np.float32)[:, bucket], BF16))
    return out


def _bias_tables(rel_bias, onehots):
    n = len(onehots)

    def body(rb_ref, mask_ref, *refs):
        parts = _split3(rb_ref[...])
        for k in range(n):
            oh = refs[k][...]
            bias = _dot(parts[0], oh) + _dot(parts[1], oh) + _dot(parts[2], oh)
            refs[n + k][0] = bias + mask_ref[0:1, :]
            refs[n + k][1] = bias + mask_ref[1:2, :]

    flat = pl.pallas_call(
        body, name="bias_tables",
        out_shape=[jax.ShapeDtypeStruct((2, DIL_HEADS, 2 * DIL_BLOCK * DIL_BLOCK), F32)] * n,
        compiler_params=pltpu.CompilerParams(vmem_limit_bytes=VMEM_LIMIT),
    )(rel_bias, _window_masks(), *onehots)
    return [t.reshape(2, DIL_HEADS // 2, 2 * DIL_BLOCK, 2 * DIL_BLOCK) for t in flat]


def _bias_grad(dbs, onehots):
    n = len(dbs)
    dbs = [t.reshape(DIL_HEADS, 2 * DIL_BLOCK * DIL_BLOCK) for t in dbs]

    def body(*refs):
        acc = jnp.zeros((DIL_HEADS, REL_BUCKETS), F32)
        for k in range(n):
            oh = refs[n + k][...]
            for part in _split3(refs[k][...]):
                acc = acc + _dot_nt(part, oh)
        refs[-1][...] = acc

    return pl.pallas_call(
        body, name="bias_grad",
        out_shape=jax.ShapeDtypeStruct((DIL_HEADS, REL_BUCKETS), F32),
        compiler_params=pltpu.CompilerParams(vmem_limit_bytes=VMEM_LIMIT),
    )(*dbs, *onehots)


def _swap_halves(x):
    lane = lax.broadcasted_iota(jnp.int32, x.shape, 1)
    first = (lane % 64) < 32
    return jnp.where(first, pltpu.roll(x, 96, 1), pltpu.roll(x, 32, 1))


def _rope_tables(T):
    pos = jnp.arange(T, dtype=F32)
    inv_freq = ROPE_BASE ** (-jnp.arange(0, MLA_ROPE, 2, dtype=F32) / MLA_ROPE)
    ang = pos[:, None] * inv_freq[None, :]
    z = jnp.zeros((T, 128 - MLA_ROPE), F32)
    cos = jnp.concatenate([jnp.cos(ang), jnp.cos(ang), z], axis=-1)
    sin = jnp.concatenate([-jnp.sin(ang), jnp.sin(ang), z], axis=-1)
    return cos, sin


def _mla_prep(proj, cos, sin, g_qa, g_kva, g_q, g_k, wq, wkv, tm=1024):
    T = proj.shape[0]
    H = MLA_HEADS
    scale = MLA_QK ** -0.5

    def body(cq_ref, ckv_ref, kpe_ref, cos_ref, sin_ref, gqa_ref, gkva_ref, gq_ref, gk_ref, wq_ref, wkv_ref,
             q_ref, k_ref, v_ref):
        cosv, sinv = cos_ref[...], sin_ref[...]

        def rope(x):
            return x * cosv + _swap_halves(x) * sinv

        cq = cq_ref[...]
        qp = _dot_nt((cq * _rstd(cq) * gqa_ref[...]).astype(BF16), wq_ref[...])
        ckv = ckv_ref[...]
        kvp = _dot((ckv * _rstd(ckv) * gkva_ref[...]).astype(BF16), wkv_ref[...])
        kpe = kpe_ref[...]
        one_hot_lane = (lax.broadcasted_iota(jnp.int32, (tm, 128), 1) == 0).astype(BF16)
        for h in range(H):
            a = qp[:, MLA_PAD * h:MLA_PAD * (h + 1)]
            qn = a * _rstd(a, MLA_QK) * gq_ref[...]
            q_ref[h, :, 0:128] = (qn[:, 0:128] * scale).astype(BF16)
            q_ref[h, :, 128:256] = (rope(qn[:, 128:256]) * scale).astype(BF16)
            kn = kvp[:, MLA_PAD * h:MLA_PAD * h + 128]
            r = lax.rsqrt((jnp.sum(kn * kn, axis=-1, keepdims=True)
                           + jnp.sum(kpe * kpe, axis=-1, keepdims=True)) / MLA_QK + EPS)
            k_ref[h, :, 0:128] = (kn * r * gk_ref[:, 0:128]).astype(BF16)
            k_ref[h, :, 128:256] = rope(kpe * r * gk_ref[:, 128:256]).astype(BF16)
            v_ref[h, :, 0:128] = kvp[:, MLA_PAD * h + 128:MLA_PAD * (h + 1)].astype(BF16)
            v_ref[h, :, 128:256] = one_hot_lane

    fix = lambda i: (0, 0)
    return pl.pallas_call(
        body, name="mla_prep", grid=(T // tm,),
        in_specs=[pl.BlockSpec((tm, MLA_Q_RANK), lambda i: (i, CQ_COL // MLA_Q_RANK)),
                  pl.BlockSpec((tm, MLA_KV_RANK), lambda i: (i, CKV_COL // MLA_KV_RANK)),
                  pl.BlockSpec((tm, 128), lambda i: (i, KPE_COL // 128)),
                  pl.BlockSpec((tm, 128), lambda i: (i, 0)), pl.BlockSpec((tm, 128), lambda i: (i, 0)),
                  pl.BlockSpec((1, 256), fix), pl.BlockSpec((1, 128), fix),
                  pl.BlockSpec((1, 256), fix), pl.BlockSpec((1, 256), fix),
                  pl.BlockSpec((H * MLA_PAD, 256), fix), pl.BlockSpec((128, H * MLA_PAD), fix)],
        out_specs=[pl.BlockSpec((H, tm, MLA_PAD), lambda i: (0, i, 0)), pl.BlockSpec((H, tm, MLA_PAD), lambda i: (0, i, 0)),
                   pl.BlockSpec((H, tm, 2 * MLA_V), lambda i: (0, i, 0))],
        out_shape=[jax.ShapeDtypeStruct((H, T, MLA_PAD), BF16), jax.ShapeDtypeStruct((H, T, MLA_PAD), BF16),
                   jax.ShapeDtypeStruct((H, T, 2 * MLA_V), BF16)],
        compiler_params=_params(1),
    )(proj, proj, proj, cos, sin, g_qa, g_kva, g_q, g_k, wq, wkv)


def _mla_prep_bwd(proj, cos, sin, g_qa, g_kva, g_q, g_k, wq, wkv, dq, dk, dv, tm=1024):
    T = proj.shape[0]
    H = MLA_HEADS
    scale = MLA_QK ** -0.5

    def body(cq_ref, ckv_ref, kpe_ref, cos_ref, sin_ref, gqa_ref, gkva_ref, gq_ref, gk_ref, wq_ref, wkv_ref,
             dq_ref, dk_ref, dv_ref,
             dcq_ref, dckv_ref, dkpe_ref, cqn_ref, ckvn_ref, dqp_ref, dkvp_ref,
             dgqa_ref, dgkva_ref, dgq_ref, dgk_ref):
        @pl.when(pl.program_id(0) == 0)
        def _():
            for ref in (dgqa_ref, dgkva_ref, dgq_ref, dgk_ref):
                ref[...] = jnp.zeros_like(ref)

        cosv, sinv = cos_ref[...], sin_ref[...]

        def rope_bwd(dy):
            return dy * cosv + _swap_halves(dy * sinv)

        cq = cq_ref[...]
        rcq = _rstd(cq)
        cqn = (cq * rcq * gqa_ref[...]).astype(BF16)
        cqn_ref[...] = cqn
        qp = _dot_nt(cqn, wq_ref[...])
        ckv = ckv_ref[...]
        rckv = _rstd(ckv)
        ckvn = (ckv * rckv * gkva_ref[...]).astype(BF16)
        ckvn_ref[...] = ckvn
        kvp = _dot(ckvn, wkv_ref[...])
        kpe = kpe_ref[...]
        dkpe = jnp.zeros_like(kpe)
        dgq = jnp.zeros((1, MLA_PAD), F32)
        dgk = jnp.zeros((1, MLA_PAD), F32)
        for h in range(H):
            a = qp[:, MLA_PAD * h:MLA_PAD * (h + 1)]
            dqh = dq_ref[h]
            dn = jnp.concatenate([dqh[:, 0:128], rope_bwd(dqh[:, 128:256])], axis=-1) * scale
            da, dg = _rms_bwd(dn, a, gq_ref[...], _rstd(a, MLA_QK), MLA_QK)
            dgq = dgq + jnp.sum(dg, axis=0, keepdims=True)
            dqp_ref[:, MLA_PAD * h:MLA_PAD * (h + 1)] = da.astype(BF16)

            ak = jnp.concatenate([kvp[:, MLA_PAD * h:MLA_PAD * h + 128], kpe], axis=-1)
            dkh = dk_ref[h]
            dnk = jnp.concatenate([dkh[:, 0:128], rope_bwd(dkh[:, 128:256])], axis=-1)
            dak, dg = _rms_bwd(dnk, ak, gk_ref[...], _rstd(ak, MLA_QK), MLA_QK)
            dgk = dgk + jnp.sum(dg, axis=0, keepdims=True)
            dkpe = dkpe + dak[:, 128:256]
            dkvp_ref[:, MLA_PAD * h:MLA_PAD * h + 128] = dak[:, 0:128].astype(BF16)
            dkvp_ref[:, MLA_PAD * h + 128:MLA_PAD * (h + 1)] = dv_ref[h].astype(BF16)
        dkpe_ref[...] = dkpe
        dgq_ref[...] += dgq
        dgk_ref[...] += dgk
        dcq, dg = _rms_bwd(_dot(dqp_ref[...], wq_ref[...]), cq, gqa_ref[...], rcq)
        dcq_ref[...] = dcq
        dgqa_ref[...] += jnp.sum(dg, axis=0, keepdims=True)
        dckv, dg = _rms_bwd(_dot_nt(dkvp_ref[...], wkv_ref[...]), ckv, gkva_ref[...], rckv)
        dckv_ref[...] = dckv
        dgkva_ref[...] += jnp.sum(dg, axis=0, keepdims=True)

    fix = lambda i: (0, 0)
    row = lambda i: (i, 0)
    head = lambda i: (0, i, 0)
    return pl.pallas_call(
        body, name="mla_prep_bwd", grid=(T // tm,),
        in_specs=[pl.BlockSpec((tm, MLA_Q_RANK), lambda i: (i, CQ_COL // MLA_Q_RANK)),
                  pl.BlockSpec((tm, MLA_KV_RANK), lambda i: (i, CKV_COL // MLA_KV_RANK)),
                  pl.BlockSpec((tm, 128), lambda i: (i, KPE_COL // 128)),
                  pl.BlockSpec((tm, 128), row), pl.BlockSpec((tm, 128), row),
                  pl.BlockSpec((1, 256), fix), pl.BlockSpec((1, 128), fix),
                  pl.BlockSpec((1, 256), fix), pl.BlockSpec((1, 256), fix),
                  pl.BlockSpec((H * MLA_PAD, 256), fix), pl.BlockSpec((128, H * MLA_PAD), fix),
                  pl.BlockSpec((H, tm, MLA_PAD), head), pl.BlockSpec((H, tm, MLA_PAD), head),
                  pl.BlockSpec((H, tm, MLA_V), head)],
        out_specs=[pl.BlockSpec((tm, 256), row), pl.BlockSpec((tm, 128), row), pl.BlockSpec((tm, 128), row),
                   pl.BlockSpec((tm, 256), row), pl.BlockSpec((tm, 128), row),
                   pl.BlockSpec((tm, H * MLA_PAD), row), pl.BlockSpec((tm, H * MLA_PAD), row),
                   pl.BlockSpec((1, 256), fix), pl.BlockSpec((1, 128), fix),
                   pl.BlockSpec((1, 256), fix), pl.BlockSpec((1, 256), fix)],
        out_shape=[jax.ShapeDtypeStruct((T, 256), F32), jax.ShapeDtypeStruct((T, 128), F32),
                   jax.ShapeDtypeStruct((T, 128), F32),
                   jax.ShapeDtypeStruct((T, 256), BF16), jax.ShapeDtypeStruct((T, 128), BF16),
                   jax.ShapeDtypeStruct((T, H * MLA_PAD), BF16), jax.ShapeDtypeStruct((T, H * MLA_PAD), BF16),
                   jax.ShapeDtypeStruct((1, 256), F32), jax.ShapeDtypeStruct((1, 128), F32),
                   jax.ShapeDtypeStruct((1, 256), F32), jax.ShapeDtypeStruct((1, 256), F32)],
        compiler_params=_params(1),
    )(proj, proj, proj, cos, sin, g_qa, g_kva, g_q, g_k, wq, wkv, dq, dk, dv)


def _causal_pairs(T, tq, tk, key_major):
    pairs = [(i, j) for i in range(T // tq) for j in range(T // tk) if j * tk <= i * tq + tq - 1]
    if key_major:
        pairs.sort(key=lambda p: (p[1], p[0]))
    outer = [p[1] if key_major else p[0] for p in pairs]
    first = [int(t == 0 or outer[t] != outer[t - 1]) for t in range(len(pairs))]
    last = [int(t == len(pairs) - 1 or outer[t] != outer[t + 1]) for t in range(len(pairs))]
    tab = lambda v: jnp.asarray(np.array(v, np.int32))
    return tab([p[0] for p in pairs]), tab([p[1] for p in pairs]), tab(first), tab(last)


def _causal_scores(qv, kv, row0):
    s = _dot_nt(qv, kv)
    if row0 is not None:
        row = lax.broadcasted_iota(jnp.int32, s.shape, 0) + row0
        col = lax.broadcasted_iota(jnp.int32, s.shape, 1)
        s = jnp.where(col <= row, s, NEG)
    return s


def _causal_variants(qi, ki, tq, tk, update):
    assert tk % tq == 0
    diag = qi * tq - ki * tk
    for off in range(0, tk, tq):
        pl.when(diag == off)(lambda off=off: update(off))
    pl.when(diag >= tk)(lambda: update(None))


def _visible_keys(off, row0, rows, tk):
    return tk if off is None else min(tk, off + row0 + rows)


def _mla_attn(q, k, v, ride=None, tq=2048, tk=4096, rc=256):
    H, T, _ = q.shape
    tq, tk = min(tq, T), min(tk, T)
    tables = _causal_pairs(T, tq, tk, key_major=False)
    n_pairs = int(tables[0].shape[0])
    r_args, r_in, r_shape, r_out, r_scratch = _ride_parts(ride)

    def body(qt, kt, ft, lt, q_ref, k_ref, v_ref, o_ref, lse_ref, m_s, acc):
        t = pl.program_id(1)
        qi, ki = qt[t], kt[t]

        @pl.when(ft[t] == 1)
        def _():
            m_s[...] = jnp.full_like(m_s, NEG)
            acc[...] = jnp.zeros_like(acc)

        def update(off):
            for c in range(tq // rc):
                rows = pl.ds(c * rc, rc)
                keys = pl.ds(0, _visible_keys(off, c * rc, rc, tk))
                s = _causal_scores(q_ref[rows, :], k_ref[keys, :], None if off is None else off + c * rc)
                m_old = m_s[rows, :]
                m_new = jnp.maximum(m_old, jnp.max(s, axis=-1, keepdims=True))
                p = jnp.exp(s - m_new).astype(BF16)
                acc[rows, :] = jnp.exp(m_old - m_new) * acc[rows, :] + _dot(p, v_ref[keys, :])
                m_s[rows, :] = m_new

        _causal_variants(qi, ki, tq, tk, update)

        @pl.when(lt[t] == 1)
        def _():
            l = jnp.max(acc[:, MLA_V:], axis=-1, keepdims=True)
            o_ref[...] = acc[:, :MLA_V] / l
            lse_ref[...] = jnp.broadcast_to(m_s[...] + jnp.log(l), lse_ref.shape)

    qrow = lambda h, t, qt, kt, ft, lt: (h, qt[t], 0)
    krow = lambda h, t, qt, kt, ft, lt: (h, kt[t], 0)
    first = lambda: (pl.program_id(0) == 0) & (pl.program_id(1) == 0)
    last = lambda: (pl.program_id(0) == H - 1) & (pl.program_id(1) == n_pairs - 1)
    outs = pl.pallas_call(
        _riding(body, 7, 2, 2, ride, first, last), name="mla_attn",
        grid_spec=pltpu.PrefetchScalarGridSpec(
            num_scalar_prefetch=4, grid=(H, n_pairs),
            in_specs=[pl.BlockSpec((None, tq, MLA_PAD), qrow), pl.BlockSpec((None, tk, MLA_PAD), krow),
                      pl.BlockSpec((None, tk, 2 * MLA_V), krow)] + r_in,
            out_specs=[pl.BlockSpec((tq, MLA_V), lambda h, t, qt, kt, ft, lt: (qt[t], h)),
                       pl.BlockSpec((None, tq, 128), qrow)] + r_out,
            scratch_shapes=[pltpu.VMEM((tq, 1), F32), pltpu.VMEM((tq, 2 * MLA_V), F32)] + r_scratch),
        out_shape=[jax.ShapeDtypeStruct((T, H * MLA_V), F32), jax.ShapeDtypeStruct((H, T, 128), F32)] + r_shape,
        compiler_params=_params(2),
    )(*tables, q, k, v, *r_args)
    return outs[:2], outs[2:]


def _mla_attn_bwd(q, k, v, o, lse, do, ride=None, tq=2048, tk=2048, rc=512, rc_diagonal=256):
    H, T, _ = q.shape
    tq, tk = min(tq, T), min(tk, T)
    tables = _causal_pairs(T, tq, tk, key_major=True)
    n_pairs = int(tables[0].shape[0])
    r_args, r_in, r_shape, r_out, r_scratch = _ride_parts(ride)

    def body(qt, kt, ft, lt, q_ref, k_ref, v_ref, o_ref, lse_ref, do_ref, dq_ref, dk_ref, dv_ref, dk_s, dv_s):
        t = pl.program_id(1)
        qi, ki = qt[t], kt[t]

        @pl.when(t == 0)
        def _():
            dq_ref[...] = jnp.zeros_like(dq_ref)

        @pl.when(ft[t] == 1)
        def _():
            dk_s[...] = jnp.zeros_like(dk_s)
            dv_s[...] = jnp.zeros_like(dv_s)

        def update(off):
            rows_per = rc if off is None else rc_diagonal
            for c in range(tq // rows_per):
                rows = pl.ds(c * rows_per, rows_per)
                keys = pl.ds(0, _visible_keys(off, c * rows_per, rows_per, tk))
                kk, vv = k_ref[keys, :], v_ref[keys, :]
                qv, dov = q_ref[rows, :], do_ref[rows, :]
                delta = jnp.sum(dov * o_ref[rows, :], axis=-1, keepdims=True)
                lse_v = jnp.max(lse_ref[rows, :], axis=-1, keepdims=True)
                p = jnp.exp(_causal_scores(qv, kk, None if off is None else off + c * rows_per) - lse_v)
                dob = dov.astype(BF16)
                dv_s[keys, :] += _dot_tn(p.astype(BF16), dob)
                ds = (p * (_dot_nt(dob, vv) - delta)).astype(BF16)
                dk_s[keys, :] += _dot_tn(ds, qv)
                out_rows = pl.ds(pl.multiple_of(qi * tq + c * rows_per, rows_per), rows_per)
                dq_ref[out_rows, :] += _dot(ds, kk)

        _causal_variants(qi, ki, tq, tk, update)

        @pl.when(lt[t] == 1)
        def _():
            dk_ref[...] = dk_s[...]
            dv_ref[...] = dv_s[...]

    qrow = lambda h, t, qt, kt, ft, lt: (h, qt[t], 0)
    krow = lambda h, t, qt, kt, ft, lt: (h, kt[t], 0)
    qcol = lambda h, t, qt, kt, ft, lt: (qt[t], h)
    first = lambda: (pl.program_id(0) == 0) & (pl.program_id(1) == 0)
    last = lambda: (pl.program_id(0) == H - 1) & (pl.program_id(1) == n_pairs - 1)
    outs = pl.pallas_call(
        _riding(body, 10, 3, 2, ride, first, last), name="mla_attn_bwd",
        grid_spec=pltpu.PrefetchScalarGridSpec(
            num_scalar_prefetch=4, grid=(H, n_pairs),
            in_specs=[pl.BlockSpec((None, tq, MLA_PAD), qrow), pl.BlockSpec((None, tk, MLA_PAD), krow),
                      pl.BlockSpec((None, tk, MLA_V), krow), pl.BlockSpec((tq, MLA_V), qcol),
                      pl.BlockSpec((None, tq, 128), qrow), pl.BlockSpec((tq, MLA_V), qcol)] + r_in,
            out_specs=[pl.BlockSpec((None, T, MLA_PAD), lambda h, t, qt, kt, ft, lt: (h, 0, 0)),
                       pl.BlockSpec((None, tk, MLA_PAD), krow), pl.BlockSpec((None, tk, MLA_V), krow)] + r_out,
            scratch_shapes=[pltpu.VMEM((tk, MLA_PAD), F32), pltpu.VMEM((tk, MLA_V), F32)] + r_scratch),
        out_shape=[jax.ShapeDtypeStruct((H, T, MLA_PAD), F32), jax.ShapeDtypeStruct((H, T, MLA_PAD), F32),
                   jax.ShapeDtypeStruct((H, T, MLA_V), F32)] + r_shape,
        compiler_params=_params(2),
    )(*tables, q, k, v, o, lse, do, *r_args)
    return outs[:3], outs[3:]


def _pair_gain(g):
    return jnp.tile(g.reshape(1, DIL_HD), (1, 2))


def _pad_gain(g):
    return jnp.pad(g.reshape(1, MLA_QK), ((0, 0), (0, MLA_PAD - MLA_QK)))


def _local_step(x, target, s, comm):
    T = x.shape[0]
    w = comm.w
    gq, gk = _pair_gain(s["dil_q_norm"]) * DIL_HD ** -0.5, _pair_gain(s["dil_k_norm"])
    g_q, g_k = _pad_gain(s["mla_q_norm"]), _pad_gain(s["mla_k_norm"])
    cos, sin = _rope_tables(T)
    onehots = _bucket_onehots()
    biases = _bias_tables(s["rel_bias"], onehots)

    (x1, h1, gate1, up1), got = _ffn_fwd(x, s["ffn1_norm"], w["ffn1_w_gate"], w["ffn1_w_up"], w["ffn1_w_down"],
                                         ride=comm.gather(_GROUPS["attn"]))
    comm.weights_landed(_GROUPS["attn"], got)
    hm, proj, qh, kh = _in_proj(x1, s["mix_norm"], w["w_in"], gq, gk)
    dil = None
    for (_, d), bias in zip(DIL_BRANCHES, biases):
        dil = _dil_fwd(qh, kh, proj, bias, d, dil)
    o_dil, lse_dil = dil
    q, k, v = _mla_prep(proj, cos, sin, s["mla_q_a_norm"], s["mla_kv_a_norm"], g_q, g_k, w["mla_w_q_b"], w["mla_w_kv_b"])
    (o_mla, lse_mla), got = _mla_attn(q, k, v, ride=comm.gather(_GROUPS["ffn2"]))
    comm.weights_landed(_GROUPS["ffn2"], got)
    x2, oc = _out_proj(x1, o_dil, o_mla, s["out_norm_dil"], s["out_norm_mla"], w["w_out"])
    (dy, h2, gate2, up2, loss), _ = _ffn_fwd(x2, s["ffn2_norm"], w["ffn2_w_gate"], w["ffn2_w_up"], w["ffn2_w_down"],
                                             target=target)

    gw, gs = {}, {}

    def ffn_grads(name, dy_in, x_in, h, gate, up, early=None):
        dx, a, dg, du, dyh, dgain = _ffn_bwd(dy_in, x_in, s[name + "_norm"], gate, up,
                                             w[name + "_w_gate"], w[name + "_w_up"], w[name + "_w_down"])
        gs[name + "_norm"] = dgain
        down, gate_n, up_n = (name + "_w_down",), (name + "_w_gate",), (name + "_w_up",)
        ride = lambda names: comm.scatter(names, gw) if early is not None else None
        gw[down[0]], landed = _matmul_tn(a, dyh, 1408, 1024, ride=ride(early))
        comm.grads_landed(early or (), landed)
        gw[gate_n[0]], landed = _matmul_tn(dg, h, 1408, 1024, ride=ride(down))
        comm.grads_landed(down, landed)
        gw[up_n[0]], landed = _matmul_tn(du, h, 1408, 1024, ride=ride(gate_n))
        comm.grads_landed(gate_n, landed)
        return dx

    dx2 = ffn_grads("ffn2", dy, x2, h2, gate2, up2)
    gw["w_out"], _ = _matmul_tn(oc, dx2, 1024, 1024)
    do_dil, do_mla, gs["out_norm_dil"], gs["out_norm_mla"] = _out_proj_bwd(
        dx2, o_dil, o_mla, s["out_norm_dil"], s["out_norm_mla"], w["w_out"])

    (dq, dk, dv), got = _mla_attn_bwd(q, k, v, o_mla, lse_mla, do_mla, ride=comm.scatter(_GROUPS["ffn2"], gw))
    comm.grads_landed(_GROUPS["ffn2"], got)
    (dcq, dckv, dkpe, cqn, ckvn, dqp, dkvp, gs["mla_q_a_norm"], gs["mla_kv_a_norm"], dg_q, dg_k) = _mla_prep_bwd(
        proj, cos, sin, s["mla_q_a_norm"], s["mla_kv_a_norm"], g_q, g_k, w["mla_w_q_b"], w["mla_w_kv_b"], dq, dk, dv)
    gs["mla_q_norm"], gs["mla_k_norm"] = dg_q[:, :MLA_QK], dg_k[:, :MLA_QK]
    gw["mla_w_q_b"], _ = _matmul_tn(dqp, cqn, 1024, 256)
    gw["mla_w_kv_b"], _ = _matmul_tn(ckvn, dkvp, 128, 1024)

    dqkv, dbs = None, []
    for (_, d), bias in reversed(list(zip(DIL_BRANCHES, biases))):
        dqkv, db = _dil_bwd(qh, kh, proj, o_dil, lse_dil, do_dil, bias, d, dqkv)
        dbs.insert(0, db)
    dqkv = [dqkv]
    gs["rel_bias"] = _bias_grad(dbs, onehots)

    ready = tuple(n for n in _GROUPS["attn"] if n != "w_in")
    (dx1, dproj, gs["mix_norm"], dgq, dgk), got = _in_proj_bwd(dx2, x1, s["mix_norm"], w["w_in"], proj, gq, gk,
                                                               dqkv, dcq, dckv, dkpe, ride=comm.scatter(ready, gw))
    comm.grads_landed(ready, got)
    gs["dil_q_norm"] = (dgq[:, :DIL_HD] + dgq[:, DIL_HD:]) * DIL_HD ** -0.5
    gs["dil_k_norm"] = dgk[:, :DIL_HD] + dgk[:, DIL_HD:]
    gw["w_in"], _ = _matmul_tn(dproj, hm, 1024, 1024)
    grad_x = ffn_grads("ffn1", dx1, x, h1, gate1, up1, early=("w_in",))
    return loss, grad_x, gw, gs


def _position():
    x, y, c = lax.axis_index("x"), lax.axis_index("y"), lax.axis_index("c")
    return x, y, c, 4 * x + 2 * y + c


def _peer(x, y, c, k):
    px = 1 - x if k & 4 else x
    py = 1 - y if k & 2 else y
    pc = 1 - c if k & 1 else c
    return (px, py, pc), 4 * px + 2 * py + pc


class _Ride:
    def __init__(self, arrays, scatter):
        self.arrays, self.scatter = list(arrays), list(scatter)
        self.n = n = len(self.arrays)
        self.specs = [pl.BlockSpec(memory_space=pl.ANY)] * n
        self.out_shape = [jax.ShapeDtypeStruct(a.shape if sc else (N_DEV,) + a.shape, a.dtype)
                          for a, sc in zip(self.arrays, self.scatter)]
        self.scratch = [pltpu.SemaphoreType.DMA((n, N_DEV - 1)), pltpu.SemaphoreType.DMA((n, N_DEV - 1)),
                        pltpu.SemaphoreType.DMA((n,))]

    def _copies(self, ins, outs, sems):
        send_sems, recv_sems, local_sems = sems
        x, y, c, me = _position()
        copies = []
        for a in range(self.n):
            src = ins[a].at[me] if self.scatter[a] else ins[a]
            copies.append(pltpu.make_async_copy(src, outs[a].at[me], local_sems.at[a]))
        for k in range(1, N_DEV):
            peer, peer_idx = _peer(x, y, c, k)
            for a in range(self.n):
                src = ins[a].at[peer_idx] if self.scatter[a] else ins[a]
                copies.append(pltpu.make_async_remote_copy(
                    src_ref=src, dst_ref=outs[a].at[me], send_sem=send_sems.at[a, k - 1], recv_sem=recv_sems.at[a, k - 1],
                    device_id=peer, device_id_type=pl.DeviceIdType.MESH))
        return copies

    def start(self, ins, outs, sems):
        for cp in self._copies(ins, outs, sems):
            cp.start()

    def wait(self, ins, outs, sems):
        for cp in self._copies(ins, outs, sems):
            cp.wait()


def _ride_parts(ride):
    if ride is None:
        return [], [], [], [], []
    return ride.arrays, ride.specs, ride.out_shape, ride.specs, ride.scratch


def _riding(body, n_in, n_out, n_scratch, ride, first, last):
    if ride is None:
        return body
    n = ride.n
    i1, i2 = n_in + n, n_in + n + n_out
    i3, i4 = i2 + n, i2 + n + n_scratch

    def wrapped(*refs):
        ins, outs, sems = refs[n_in:i1], refs[i2:i3], refs[i4:]

        @pl.when(first())
        def _():
            ride.start(ins, outs, sems)

        body(*refs[:n_in], *refs[i1:i2], *refs[i3:i4])

        @pl.when(last())
        def _():
            ride.wait(ins, outs, sems)

    return wrapped


def _gather_two_level(arrays, name):
    n = len(arrays)
    out_shape = [jax.ShapeDtypeStruct((N_DEV,) + a.shape, a.dtype) for a in arrays]

    def body(*refs):
        ins, outs = refs[:n], refs[n:2 * n]
        send_sems, recv_sems, local_sems = refs[2 * n:]
        x, y, c, me = _position()
        sibling = (x, y, 1 - c)
        chips = [(1 - x, y), (x, 1 - y), (1 - x, 1 - y)]
        block = lambda px, py, pc: 4 * px + 2 * py + pc

        def copy(a, k, blk, to, src=None):
            dst = outs[a].at[blk]
            return pltpu.make_async_remote_copy(
                src_ref=dst if src is None else src, dst_ref=dst, send_sem=send_sems.at[a, k], recv_sem=recv_sems.at[a, k],
                device_id=to, device_id_type=pl.DeviceIdType.MESH)

        local = [pltpu.make_async_copy(ins[a], outs[a].at[me], local_sems.at[a]) for a in range(n)]
        first = []
        for a in range(n):
            first.append(copy(a, 0, me, sibling, src=ins[a]))
            first += [copy(a, 1 + j, me, (*chip, c), src=ins[a]) for j, chip in enumerate(chips)]
        for cp in local + first:
            cp.start()
        passed = []
        for j, chip in enumerate(chips):
            for a in range(n):
                copy(a, 1 + j, block(*chip, c), sibling).wait_recv()
                passed.append(copy(a, 4 + j, block(*chip, c), sibling))
                passed[-1].start()
        for a in range(n):
            copy(a, 0, block(x, y, 1 - c), sibling).wait_recv()
            for j, chip in enumerate(chips):
                copy(a, 4 + j, block(*chip, 1 - c), sibling).wait_recv()
        for cp in first + passed:
            cp.wait_send()
        for cp in local:
            cp.wait()

    any_spec = [pl.BlockSpec(memory_space=pl.ANY)] * n
    return pl.pallas_call(
        body, name=name, in_specs=any_spec, out_specs=any_spec, out_shape=out_shape,
        scratch_shapes=[pltpu.SemaphoreType.DMA((n, N_DEV - 1)), pltpu.SemaphoreType.DMA((n, N_DEV - 1)),
                        pltpu.SemaphoreType.DMA((n,))],
    )(*arrays)


def _adamw_math(wv, g, m, v):
    m = ADAM_B1 * m + (1.0 - ADAM_B1) * g
    v = ADAM_B2 * v + (1.0 - ADAM_B2) * (g * g)
    m_hat = m / (1.0 - ADAM_B1 ** ADAM_STEP)
    v_hat = v / (1.0 - ADAM_B2 ** ADAM_STEP)
    delta = -ADAM_LR * (m_hat / (jnp.sqrt(v_hat) + ADAM_EPS) + ADAM_WD * wv)
    return delta, m, v


def _adamw(items, ride=None, max_rows=256):
    K = len(items)
    tiles, spans, start = [], [], 0
    for _, wv, _, _ in items:
        R = wv.shape[1]
        tr = max([t for t in range(16, max_rows + 1, 16) if R % t == 0] or [R])
        tiles.append(tr)
        spans.append((start, R // tr))
        start += R // tr
    total = start
    r_args, r_in, r_shape, r_out, r_scratch = _ride_parts(ride)

    def body(*refs):
        i = pl.program_id(0)
        for k, (first_step, n_steps) in enumerate(spans):
            def update(k=k):
                p_ref, w_ref, m_ref, v_ref = refs[4 * k:4 * k + 4]
                g_ref, d_ref, mo_ref, vo_ref = refs[4 * K + 4 * k:4 * K + 4 * k + 4]
                g = p_ref[0].astype(F32)
                for j in range(1, N_DEV):
                    g = g + p_ref[j].astype(F32)
                d, mn, vn = _adamw_math(w_ref[0], g, m_ref[0], v_ref[0])
                g_ref[0] = g
                d_ref[0] = d
                mo_ref[0] = mn
                vo_ref[0] = vn

            pl.when((i >= first_step) & (i < first_step + n_steps))(update)

    in_specs, out_specs, out_shape, args = [], [], [], []
    for (parts, wv, m, v), tr, (first_step, n_steps) in zip(items, tiles, spans):
        C = wv.shape[2]
        tile = lambda i, s=first_step, n=n_steps: (0, jnp.clip(i - s, 0, n - 1), 0)
        blk = pl.BlockSpec((1, tr, C), tile)
        in_specs += [pl.BlockSpec((N_DEV, tr, C), tile), blk, blk, blk]
        out_specs += [blk] * 4
        out_shape += [jax.ShapeDtypeStruct(wv.shape, F32)] * 4
        args += [parts, wv, m, v]
    outs = pl.pallas_call(
        _riding(body, 4 * K, 4 * K, 0, ride, lambda: pl.program_id(0) == 0, lambda: pl.program_id(0) == total - 1),
        name="adamw", grid=(total,),
        in_specs=in_specs + r_in, out_specs=out_specs + r_out, out_shape=out_shape + r_shape,
        scratch_shapes=r_scratch, compiler_params=_params(1),
    )(*args, *r_args)
    return [outs[4 * k:4 * k + 4] for k in range(K)], outs[4 * K:]


_TRANSPOSED = ("ffn1_w_gate", "ffn1_w_up", "ffn2_w_gate", "ffn2_w_up", "w_in", "mla_w_q_b")
_GROUPS = {"ffn1": ("ffn1_w_gate", "ffn1_w_up", "ffn1_w_down"),
           "ffn2": ("ffn2_w_gate", "ffn2_w_up", "ffn2_w_down"),
           "attn": ("w_in", "mla_w_q_b", "mla_w_kv_b", "w_out")}
_SMALL = ("ffn1_norm", "mix_norm", "ffn2_norm", "out_norm_dil", "out_norm_mla", "mla_q_a_norm", "rel_bias",
          "mla_q_norm", "mla_k_norm", "mla_kv_a_norm", "dil_q_norm", "dil_k_norm")
_SMALL_ROWS = 48


def _cols_to_full(g):
    return g.transpose(1, 0, 2).reshape(g.shape[1], N_DEV * g.shape[2])


def _full_to_cols(f):
    return f.reshape(f.shape[0], N_DEV, f.shape[1] // N_DEV).transpose(1, 0, 2)


def _shard_view(name, a):
    return jnp.swapaxes(a, 1, 2) if name in _TRANSPOSED else a


def _to_full(name, g):
    if name == "mla_w_kv_b":
        return _cols_to_full(g)
    f = g.reshape(-1, g.shape[-1])
    if name == "w_in":
        f = jnp.pad(f, ((0, PROJ_PAD - PROJ_COLS), (0, 0)))
    if name == "mla_w_q_b":
        f = jnp.pad(f.reshape(MLA_HEADS, MLA_QK, -1), ((0, 0), (0, MLA_PAD - MLA_QK), (0, 0)))
        f = f.reshape(MLA_HEADS * MLA_PAD, -1)
    return f


def _to_parts(name, f):
    if name == "mla_w_kv_b":
        return _full_to_cols(f).astype(BF16)
    if name == "w_in":
        f = f[:PROJ_COLS]
    if name == "mla_w_q_b":
        f = f.reshape(MLA_HEADS, MLA_PAD, -1)[:, :MLA_QK].reshape(MLA_HEADS * MLA_QK, -1)
    return f.reshape(N_DEV, -1, f.shape[-1]).astype(BF16)


class _Comm:
    def __init__(self, shards):
        self.shards, self.w, self.recv = shards, {}, {}

    def gather(self, names):
        return _Ride([self.shards[n] for n in names], [False] * len(names))

    def scatter(self, names, grads):
        return _Ride([_to_parts(n, grads[n]) for n in names], [True] * len(names))

    def weights_landed(self, names, got):
        self.w.update({n: _to_full(n, g) for n, g in zip(names, got)})

    def grads_landed(self, names, got):
        self.recv.update(zip(names, got))


def _pack_small(parts, extra):
    flat = jnp.concatenate([parts[n].reshape(-1) for n in _SMALL] + [extra.reshape(-1)])
    return jnp.pad(flat, (0, _SMALL_ROWS * 128 - flat.shape[0])).reshape(_SMALL_ROWS, 128)


def _unpack_small(packed, shapes):
    flat, out, off = packed.reshape(-1), {}, 0
    for n in _SMALL:
        size = math.prod(shapes[n])
        out[n] = flat[off:off + size].reshape(shapes[n])
        off += size
    return out, flat[off]


_NAMES = ("ffn1_norm", "ffn1_w_gate", "ffn1_w_up", "ffn1_w_down", "mix_norm", "w_in", "dil_q_norm", "dil_k_norm",
          "rel_bias", "mla_q_a_norm", "mla_w_q_b", "mla_kv_a_norm", "mla_w_kv_b", "mla_q_norm", "mla_k_norm",
          "out_norm_dil", "out_norm_mla", "w_out", "ffn2_norm", "ffn2_w_gate", "ffn2_w_up", "ffn2_w_down")


def kernel(x, ffn1_norm, ffn1_w_gate, ffn1_w_up, ffn1_w_down, mix_norm, w_in, dil_q_norm, dil_k_norm, rel_bias, mla_q_a_norm, mla_w_q_b, mla_kv_a_norm, mla_w_kv_b, mla_q_norm, mla_k_norm, out_norm_dil, out_norm_mla, w_out, ffn2_norm, ffn2_w_gate, ffn2_w_up, ffn2_w_down, loss_target, m_ffn1_norm, m_ffn1_w_gate, m_ffn1_w_up, m_ffn1_w_down, m_mix_norm, m_w_in, m_dil_q_norm, m_dil_k_norm, m_rel_bias, m_mla_q_a_norm, m_mla_w_q_b, m_mla_kv_a_norm, m_mla_w_kv_b, m_mla_q_norm, m_mla_k_norm, m_out_norm_dil, m_out_norm_mla, m_w_out, m_ffn2_norm, m_ffn2_w_gate, m_ffn2_w_up, m_ffn2_w_down, v_ffn1_norm, v_ffn1_w_gate, v_ffn1_w_up, v_ffn1_w_down, v_mix_norm, v_w_in, v_dil_q_norm, v_dil_k_norm, v_rel_bias, v_mla_q_a_norm, v_mla_w_q_b, v_mla_kv_a_norm, v_mla_w_kv_b, v_mla_q_norm, v_mla_k_norm, v_out_norm_dil, v_out_norm_mla, v_w_out, v_ffn2_norm, v_ffn2_w_gate, v_ffn2_w_up, v_ffn2_w_down):
    args = locals()
    wts = {n: args[n] for n in _NAMES}
    mom = {n: args["m_" + n] for n in _NAMES}
    var = {n: args["v_" + n] for n in _NAMES}

    matrices = [n for group in _GROUPS.values() for n in group]
    comm = _Comm({n: _shard_view(n, wts[n])[0].astype(BF16) for n in matrices})
    comm.weights_landed(_GROUPS["ffn1"], _gather_two_level(comm.gather(_GROUPS["ffn1"]).arrays, "gather_first"))
    small = {n: wts[n].reshape(1, -1) if n != "rel_bias" else wts[n] for n in _SMALL}

    loss, grad_x, gw, gs = _local_step(x[0], loss_target[0], small, comm)

    item = lambda n: (comm.recv[n],) + tuple(_shard_view(n, a[n]) for a in (wts, mom, var))
    landed = [n for n in matrices if n != "ffn1_w_up"]
    last = comm.scatter(("ffn1_w_up",), gw)
    updates, got = _adamw([item(n) for n in landed], max_rows=32,
                          ride=_Ride(last.arrays + [_pack_small(gs, loss[0, 0])], last.scatter + [False]))
    comm.grads_landed(("ffn1_w_up",), got[:-1])

    zero = jnp.zeros((), F32)
    small_item = (got[-1],) + tuple(_pack_small(a, zero)[None] for a in (wts, mom, var))
    (up_update, packed), _ = _adamw([item("ffn1_w_up"), small_item])
    res = {n: [_shard_view(n, r) for r in u] for n, u in zip(landed + ["ffn1_w_up"], updates + [up_update])}
    shapes = {n: wts[n].shape for n in _SMALL}
    loss_total = None
    for slot, q in enumerate(packed):
        vals, extra = _unpack_small(q, shapes)
        if slot == 0:
            loss_total = extra
        for n in _SMALL:
            res.setdefault(n, [None] * 4)[slot] = vals[n]
    outs = [loss_total, grad_x[None]]
    for slot in range(4):
        outs += [res[n][slot].reshape(wts[n].shape) for n in _NAMES]
    return tuple(outs)
```

```python
import math

import numpy as np
import jax
import jax.numpy as jnp
from jax import lax
from jax.experimental import pallas as pl
from jax.experimental.pallas import tpu as pltpu

F32, BF16 = jnp.float32, jnp.bfloat16
EPS = 1e-6
NEG = -1e30
N_DEV = 8

DIL_HEADS, DIL_HD = 8, 64
DIL_WIDTH = DIL_HEADS * DIL_HD
DIL_BRANCHES = ((128, 1), (512, 4), (2048, 16))
DIL_BLOCK = 128
MLA_HEADS, MLA_NOPE, MLA_ROPE, MLA_V = 4, 128, 64, 128
MLA_QK = MLA_NOPE + MLA_ROPE
MLA_PAD = 256
ROPE_BASE = 10000.0
REL_BUCKETS, REL_MAX_DIST = 32, 2048
MLA_Q_RANK, MLA_KV_RANK = 256, 128
PROJ_COLS, PROJ_PAD = 1984, 2048
CQ_COL = 3 * DIL_WIDTH
CKV_COL, KPE_COL = CQ_COL + MLA_Q_RANK, CQ_COL + MLA_Q_RANK + MLA_KV_RANK
FFN_RESID = 0.5
ADAM_LR, ADAM_B1, ADAM_B2, ADAM_EPS, ADAM_WD, ADAM_STEP = 0.001, 0.9, 0.999, 1e-08, 0.01, 10
VMEM_LIMIT = 62 * 1024 * 1024

_NT = (((1,), (1,)), ((), ()))
_TN = (((0,), (0,)), ((), ()))


def _dot(a, b):
    return jnp.dot(a, b, preferred_element_type=F32)


def _dot_nt(a, b):
    return lax.dot_general(a, b, _NT, preferred_element_type=F32)


def _dot_tn(a, b):
    return lax.dot_general(a, b, _TN, preferred_element_type=F32)


def _params(n_axes):
    return pltpu.CompilerParams(dimension_semantics=("arbitrary",) * n_axes, vmem_limit_bytes=VMEM_LIMIT)


def _rstd(x, n=None):
    n = x.shape[-1] if n is None else n
    return lax.rsqrt(jnp.sum(x * x, axis=-1, keepdims=True) / n + EPS)


def _rms_bwd(dy, x, g, r, n=None):
    n = x.shape[-1] if n is None else n
    u = dy * g
    dx = r * u - x * (r * r * r) * (jnp.sum(u * x, axis=-1, keepdims=True) / n)
    return dx, dy * x * r


def _sigmoid(x):
    return 1.0 / (1.0 + jnp.exp(-x))


def _split3(x):
    parts = []
    for _ in range(3):
        xb = x.astype(BF16)
        parts.append(xb)
        x = x - xb.astype(F32)
    return parts


def _ffn_fwd(x, gain, wg, wu, wd, ride=None, target=None, tm=512, tf=2816):
    T, D = x.shape
    F = wg.shape[0]
    ni, nj = T // tm, F // tf
    with_loss = target is not None
    r_args, r_in, r_shape, r_out, r_scratch = _ride_parts(ride)

    def body(*refs):
        x_ref, g_ref, wg_ref, wu_ref, wd_ref = refs[:5]
        t_ref = refs[5] if with_loss else None
        xo_ref, h_ref, gate_ref, up_ref = refs[5 + with_loss:9 + with_loss]
        loss_ref = refs[-2] if with_loss else None
        acc = refs[-1]
        i, j = pl.program_id(0), pl.program_id(1)

        @pl.when(j == 0)
        def _():
            xv = x_ref[...]
            h_ref[...] = (xv * _rstd(xv) * g_ref[...]).astype(BF16)
            if nj > 1:
                acc[...] = jnp.zeros_like(acc)

        h = h_ref[...]
        g = _dot_nt(h, wg_ref[...])
        u = _dot_nt(h, wu_ref[...])
        gate_ref[...] = g.astype(BF16)
        up_ref[...] = u.astype(BF16)
        a = (g * _sigmoid(g) * u).astype(BF16)
        part = _dot(a, wd_ref[...])
        if nj > 1:
            acc[...] += part

        @pl.when(j == nj - 1)
        def _():
            y = x_ref[...] + FFN_RESID * (acc[...] if nj > 1 else part)
            if with_loss:
                @pl.when(i == 0)
                def _():
                    loss_ref[...] = jnp.zeros_like(loss_ref)

                e = y - t_ref[...]
                xo_ref[...] = e * (1.0 / D)
                loss_ref[...] += (0.5 / D) * jnp.sum(e * e)
            else:
                xo_ref[...] = y

    row = lambda i, j: (i, 0)
    tile = lambda i, j: (i, j)
    n_in, n_out = 5 + with_loss, 4 + with_loss
    first = lambda: (pl.program_id(0) == 0) & (pl.program_id(1) == 0)
    last = lambda: (pl.program_id(0) == ni - 1) & (pl.program_id(1) == nj - 1)
    outs = pl.pallas_call(
        _riding(body, n_in, n_out, 1, ride, first, last), name="ffn_fwd", grid=(ni, nj),
        in_specs=[pl.BlockSpec((tm, D), row), pl.BlockSpec((1, D), lambda i, j: (0, 0)),
                  pl.BlockSpec((tf, D), lambda i, j: (j, 0)), pl.BlockSpec((tf, D), lambda i, j: (j, 0)),
                  pl.BlockSpec((tf, D), lambda i, j: (j, 0))] + [pl.BlockSpec((tm, D), row)] * with_loss + r_in,
        out_specs=[pl.BlockSpec((tm, D), row), pl.BlockSpec((tm, D), row), pl.BlockSpec((tm, tf), tile),
                   pl.BlockSpec((tm, tf), tile)] + [pl.BlockSpec((1, 128), lambda i, j: (0, 0))] * with_loss + r_out,
        out_shape=[jax.ShapeDtypeStruct((T, D), F32), jax.ShapeDtypeStruct((T, D), BF16),
                   jax.ShapeDtypeStruct((T, F), BF16), jax.ShapeDtypeStruct((T, F), BF16)]
        + [jax.ShapeDtypeStruct((1, 128), F32)] * with_loss + r_shape,
        scratch_shapes=[pltpu.VMEM((tm, D), F32)] + r_scratch,
        compiler_params=_params(2),
    )(x, gain, wg, wu, wd, *([target] if with_loss else []), *r_args)
    return outs[:n_out], outs[n_out:]


def _ffn_bwd(dy, x, gain, gate, up, wg, wu, wd, tm=256, tf=2816):
    T, D = x.shape
    F = wg.shape[0]
    ni, nj = T // tm, F // tf

    def body(dy_ref, x_ref, g_ref, gate_ref, up_ref, wg_ref, wu_ref, wd_ref,
             dx_ref, a_ref, dg_ref, du_ref, dyh_ref, dgain_ref, acc):
        i, j = pl.program_id(0), pl.program_id(1)

        @pl.when((i == 0) & (j == 0))
        def _():
            dgain_ref[...] = jnp.zeros_like(dgain_ref)

        @pl.when(j == 0)
        def _():
            dyh_ref[...] = (FFN_RESID * dy_ref[...]).astype(BF16)
            if nj > 1:
                acc[...] = jnp.zeros_like(acc)

        da = _dot_nt(dyh_ref[...], wd_ref[...])
        g = gate_ref[...].astype(F32)
        u = up_ref[...].astype(F32)
        sig = _sigmoid(g)
        s = g * sig
        a_ref[...] = (s * u).astype(BF16)
        dg = (da * u * (sig * (1.0 + g * (1.0 - sig)))).astype(BF16)
        du = (da * s).astype(BF16)
        dg_ref[...] = dg
        du_ref[...] = du
        part = _dot(dg, wg_ref[...]) + _dot(du, wu_ref[...])
        if nj > 1:
            acc[...] += part

        @pl.when(j == nj - 1)
        def _():
            xv = x_ref[...]
            dxn, dgc = _rms_bwd(acc[...] if nj > 1 else part, xv, g_ref[...], _rstd(xv))
            dx_ref[...] = dy_ref[...] + dxn
            dgain_ref[...] += jnp.sum(dgc, axis=0, keepdims=True)

    return pl.pallas_call(
        body, name="ffn_bwd", grid=(ni, nj),
        in_specs=[pl.BlockSpec((tm, D), lambda i, j: (i, 0)), pl.BlockSpec((tm, D), lambda i, j: (i, 0)),
                  pl.BlockSpec((1, D), lambda i, j: (0, 0)),
                  pl.BlockSpec((tm, tf), lambda i, j: (i, j)), pl.BlockSpec((tm, tf), lambda i, j: (i, j)),
                  pl.BlockSpec((tf, D), lambda i, j: (j, 0)), pl.BlockSpec((tf, D), lambda i, j: (j, 0)),
                  pl.BlockSpec((tf, D), lambda i, j: (j, 0))],
        out_specs=[pl.BlockSpec((tm, D), lambda i, j: (i, 0)),
                   pl.BlockSpec((tm, tf), lambda i, j: (i, j)), pl.BlockSpec((tm, tf), lambda i, j: (i, j)),
                   pl.BlockSpec((tm, tf), lambda i, j: (i, j)),
                   pl.BlockSpec((tm, D), lambda i, j: (i, 0)), pl.BlockSpec((1, D), lambda i, j: (0, 0))],
        out_shape=[jax.ShapeDtypeStruct((T, D), F32), jax.ShapeDtypeStruct((T, F), BF16),
                   jax.ShapeDtypeStruct((T, F), BF16), jax.ShapeDtypeStruct((T, F), BF16),
                   jax.ShapeDtypeStruct((T, D), BF16), jax.ShapeDtypeStruct((1, D), F32)],
        scratch_shapes=[pltpu.VMEM((tm, D), F32)],
        compiler_params=_params(2),
    )(dy, x, gain, gate, up, wg, wu, wd)


def _matmul_tn(a, b, tk, tn, ride=None, tt=2048):
    T, K = a.shape
    N = b.shape[1]
    tk, tn = min(tk, K), min(tn, N)
    grid = (K // tk, N // tn, T // tt)
    r_args, r_in, r_shape, r_out, r_scratch = _ride_parts(ride)

    def body(a_ref, b_ref, o_ref, acc):
        t = pl.program_id(2)

        @pl.when(t == 0)
        def _():
            acc[...] = jnp.zeros_like(acc)

        acc[...] += _dot_tn(a_ref[...].astype(BF16), b_ref[...].astype(BF16))

        @pl.when(t == grid[2] - 1)
        def _():
            o_ref[...] = acc[...].astype(BF16)

    first = lambda: (pl.program_id(0) == 0) & (pl.program_id(1) == 0) & (pl.program_id(2) == 0)
    last = lambda: ((pl.program_id(0) == grid[0] - 1) & (pl.program_id(1) == grid[1] - 1)
                    & (pl.program_id(2) == grid[2] - 1))
    outs = pl.pallas_call(
        _riding(body, 2, 1, 1, ride, first, last), name="matmul_tn", grid=grid,
        in_specs=[pl.BlockSpec((tt, tk), lambda k, n, t: (t, k)), pl.BlockSpec((tt, tn), lambda k, n, t: (t, n))] + r_in,
        out_specs=[pl.BlockSpec((tk, tn), lambda k, n, t: (k, n))] + r_out,
        out_shape=[jax.ShapeDtypeStruct((K, N), BF16)] + r_shape,
        scratch_shapes=[pltpu.VMEM((tk, tn), F32)] + r_scratch,
        compiler_params=_params(3),
    )(a, b, *r_args)
    return outs[0], outs[1:]


def _in_proj(x, gain, w, gq, gk, ride=None, tm=1024):
    T, D = x.shape
    N = w.shape[0]
    W = DIL_WIDTH

    def body(x_ref, g_ref, w_ref, gq_ref, gk_ref, h_ref, p_ref, qh_ref, kh_ref):
        xv = x_ref[...]
        h = (xv * _rstd(xv) * g_ref[...]).astype(BF16)
        h_ref[...] = h
        p_ref[...] = _dot_nt(h, w_ref[...])
        lo = lax.broadcasted_iota(jnp.int32, (tm, 128), 1) < DIL_HD
        for hp in range(DIL_HEADS // 2):
            q = p_ref[:, 128 * hp:128 * (hp + 1)]
            k = p_ref[:, W + 128 * hp:W + 128 * (hp + 1)]
            qh_ref[:, 128 * hp:128 * (hp + 1)] = (q * _pair_rstd(q, lo) * gq_ref[...]).astype(BF16).astype(F32)
            kh_ref[:, 128 * hp:128 * (hp + 1)] = (k * _pair_rstd(k, lo) * gk_ref[...]).astype(BF16).astype(F32)

    row = lambda i: (i, 0)
    fix = lambda i: (0, 0)
    r_args, r_in, r_shape, r_out, r_scratch = _ride_parts(ride)
    outs = pl.pallas_call(
        _riding(body, 5, 4, 0, ride, lambda: pl.program_id(0) == 0, lambda: pl.program_id(0) == T // tm - 1),
        name="in_proj", grid=(T // tm,),
        in_specs=[pl.BlockSpec((tm, D), row), pl.BlockSpec((1, D), fix), pl.BlockSpec((N, D), fix),
                  pl.BlockSpec((1, 128), fix), pl.BlockSpec((1, 128), fix)] + r_in,
        out_specs=[pl.BlockSpec((tm, D), row), pl.BlockSpec((tm, N), row), pl.BlockSpec((tm, W), row),
                   pl.BlockSpec((tm, W), row)] + r_out,
        out_shape=[jax.ShapeDtypeStruct((T, D), BF16), jax.ShapeDtypeStruct((T, N), F32),
                   jax.ShapeDtypeStruct((T, W), F32), jax.ShapeDtypeStruct((T, W), F32)] + r_shape,
        scratch_shapes=r_scratch,
        compiler_params=_params(1),
    )(x, gain, w, gq, gk, *r_args)
    return outs[:4], outs[4:]


def _in_proj_bwd(dx_up, x, gain, w, proj, gq, gk, dqkv, dcq, dckv, dkpe, ride=None, tm=512):
    T, D = x.shape
    N = w.shape[0]
    W = DIL_WIDTH
    nb = len(dqkv)

    def body(*refs):
        dxu_ref, x_ref, g_ref, w_ref, q_ref, k_ref, gq_ref, gk_ref = refs[:8]
        dil_refs = refs[8:8 + 3 * nb]
        dcq_ref, dckv_ref, dkpe_ref, dx_ref, dp_ref, dgain_ref, dgq_ref, dgk_ref = refs[8 + 3 * nb:]

        @pl.when(pl.program_id(0) == 0)
        def _():
            for ref in (dgain_ref, dgq_ref, dgk_ref):
                ref[...] = jnp.zeros_like(ref)

        lo = lax.broadcasted_iota(jnp.int32, (tm, 128), 1) < DIL_HD
        norms = ((q_ref, gq_ref, dgq_ref), (k_ref, gk_ref, dgk_ref))
        for part in range(3):
            acc = dil_refs[part][...]
            for b in range(1, nb):
                acc = acc + dil_refs[3 * b + part][...]
            if part == 2:
                dp_ref[:, 2 * W:3 * W] = acc.astype(BF16)
                continue
            raw_ref, gn_ref, dgn_ref = norms[part]
            for hp in range(DIL_HEADS // 2):
                raw = raw_ref[:, 128 * hp:128 * (hp + 1)]
                d_raw, dgn = _pair_rms_bwd(acc[:, 128 * hp:128 * (hp + 1)], raw, _pair_rstd(raw, lo), gn_ref[...], lo)
                dp_ref[:, part * W + 128 * hp:part * W + 128 * (hp + 1)] = d_raw.astype(BF16)
                dgn_ref[...] += dgn
        dp_ref[:, 3 * W:3 * W + 256] = dcq_ref[...].astype(BF16)
        dp_ref[:, 3 * W + 256:3 * W + 384] = dckv_ref[...].astype(BF16)
        dp_ref[:, 3 * W + 384:N] = dkpe_ref[...].astype(BF16)
        dh = _dot(dp_ref[...], w_ref[...])
        xv = x_ref[...]
        dxn, dgc = _rms_bwd(dh, xv, g_ref[...], _rstd(xv))
        dx_ref[...] = dxu_ref[...] + dxn
        dgain_ref[...] += jnp.sum(dgc, axis=0, keepdims=True)

    row = lambda i: (i, 0)
    fix = lambda i: (0, 0)
    r_args, r_in, r_shape, r_out, r_scratch = _ride_parts(ride)
    first = lambda: pl.program_id(0) == 0
    last = lambda: pl.program_id(0) == T // tm - 1
    outs = pl.pallas_call(
        _riding(body, 11 + 3 * nb, 5, 0, ride, first, last), name="in_proj_bwd", grid=(T // tm,),
        in_specs=[pl.BlockSpec((tm, D), row), pl.BlockSpec((tm, D), row), pl.BlockSpec((1, D), fix),
                  pl.BlockSpec((N, D), fix), pl.BlockSpec((tm, W), row), pl.BlockSpec((tm, W), lambda i: (i, 1)),
                  pl.BlockSpec((1, 128), fix), pl.BlockSpec((1, 128), fix)] + [pl.BlockSpec((tm, W), row)] * (3 * nb)
                 + [pl.BlockSpec((tm, 256), row), pl.BlockSpec((tm, 128), row), pl.BlockSpec((tm, 128), row)] + r_in,
        out_specs=[pl.BlockSpec((tm, D), row), pl.BlockSpec((tm, N), row), pl.BlockSpec((1, D), fix),
                   pl.BlockSpec((1, 128), fix), pl.BlockSpec((1, 128), fix)] + r_out,
        out_shape=[jax.ShapeDtypeStruct((T, D), F32), jax.ShapeDtypeStruct((T, N), BF16),
                   jax.ShapeDtypeStruct((1, D), F32), jax.ShapeDtypeStruct((1, 128), F32),
                   jax.ShapeDtypeStruct((1, 128), F32)] + r_shape,
        scratch_shapes=r_scratch,
        compiler_params=_params(1),
    )(dx_up, x, gain, w, proj, proj, gq, gk, *[a for triple in dqkv for a in triple], dcq, dckv, dkpe, *r_args)
    return outs[:5], outs[5:]


def _out_proj(x, o_dil, o_mla, g_dil, g_mla, w, tm=1024):
    T, D = x.shape
    W = o_dil.shape[1]

    def body(x_ref, od_ref, om_ref, gd_ref, gm_ref, w_ref, xo_ref, oc_ref):
        od, om = od_ref[...], om_ref[...]
        oc_ref[:, 0:W] = (od * _rstd(od) * gd_ref[...]).astype(BF16)
        oc_ref[:, W:2 * W] = (om * _rstd(om) * gm_ref[...]).astype(BF16)
        xo_ref[...] = x_ref[...] + _dot(oc_ref[...], w_ref[...])

    row = lambda i: (i, 0)
    fix = lambda i: (0, 0)
    return pl.pallas_call(
        body, name="out_proj", grid=(T // tm,),
        in_specs=[pl.BlockSpec((tm, D), row), pl.BlockSpec((tm, W), row), pl.BlockSpec((tm, W), row),
                  pl.BlockSpec((1, W), fix), pl.BlockSpec((1, W), fix), pl.BlockSpec((2 * W, D), fix)],
        out_specs=[pl.BlockSpec((tm, D), row), pl.BlockSpec((tm, 2 * W), row)],
        out_shape=[jax.ShapeDtypeStruct((T, D), F32), jax.ShapeDtypeStruct((T, 2 * W), BF16)],
        compiler_params=_params(1),
    )(x, o_dil, o_mla, g_dil, g_mla, w)


def _out_proj_bwd(dx, o_dil, o_mla, g_dil, g_mla, w, tm=1024):
    T, D = dx.shape
    W = o_dil.shape[1]

    def body(dx_ref, od_ref, om_ref, gd_ref, gm_ref, w_ref, dod_ref, dom_ref, dgd_ref, dgm_ref):
        @pl.when(pl.program_id(0) == 0)
        def _():
            dgd_ref[...] = jnp.zeros_like(dgd_ref)
            dgm_ref[...] = jnp.zeros_like(dgm_ref)

        doc = _dot_nt(dx_ref[...].astype(BF16), w_ref[...])
        od, om = od_ref[...], om_ref[...]
        dod, dgd = _rms_bwd(doc[:, 0:W], od, gd_ref[...], _rstd(od))
        dom, dgm = _rms_bwd(doc[:, W:2 * W], om, gm_ref[...], _rstd(om))
        dod_ref[...] = dod
        dom_ref[...] = dom
        dgd_ref[...] += jnp.sum(dgd, axis=0, keepdims=True)
        dgm_ref[...] += jnp.sum(dgm, axis=0, keepdims=True)

    row = lambda i: (i, 0)
    fix = lambda i: (0, 0)
    return pl.pallas_call(
        body, name="out_proj_bwd", grid=(T // tm,),
        in_specs=[pl.BlockSpec((tm, D), row), pl.BlockSpec((tm, W), row), pl.BlockSpec((tm, W), row),
                  pl.BlockSpec((1, W), fix), pl.BlockSpec((1, W), fix), pl.BlockSpec((2 * W, D), fix)],
        out_specs=[pl.BlockSpec((tm, W), row), pl.BlockSpec((tm, W), row),
                   pl.BlockSpec((1, W), fix), pl.BlockSpec((1, W), fix)],
        out_shape=[jax.ShapeDtypeStruct((T, W), F32), jax.ShapeDtypeStruct((T, W), F32),
                   jax.ShapeDtypeStruct((1, W), F32), jax.ShapeDtypeStruct((1, W), F32)],
        compiler_params=_params(1),
    )(dx, o_dil, o_mla, g_dil, g_mla, w)


def _pair_rstd(x, lo):
    sq = x * x
    s0 = jnp.sum(jnp.where(lo, sq, 0.0), axis=-1, keepdims=True)
    s1 = jnp.sum(jnp.where(lo, 0.0, sq), axis=-1, keepdims=True)
    return jnp.where(lo, lax.rsqrt(s0 / DIL_HD + EPS), lax.rsqrt(s1 / DIL_HD + EPS))


def _pair_rms_bwd(dn, x, r, g, lo):
    u = dn * g
    t = u * x
    d0 = jnp.sum(jnp.where(lo, t, 0.0), axis=-1, keepdims=True)
    d1 = jnp.sum(jnp.where(lo, 0.0, t), axis=-1, keepdims=True)
    dx = r * u - x * (r * r * r) * (jnp.where(lo, d0, d1) / DIL_HD)
    return dx, jnp.sum(dn * x * r, axis=0, keepdims=True)


def _pair_col(x, lo, e):
    sel = lo if e == 0 else jnp.logical_not(lo)
    return jnp.max(jnp.where(sel, x, NEG), axis=-1, keepdims=True)


def _first_head_lanes():
    return lax.broadcasted_iota(jnp.int32, (DIL_BLOCK, DIL_BLOCK), 1) < DIL_HD


def _window_masks():
    i = np.arange(DIL_BLOCK)[:, None]
    j = np.arange(DIL_BLOCK)[None, :]
    cur = j <= i
    both = np.concatenate([j >= i, cur], axis=1)
    first = np.concatenate([np.zeros_like(cur), cur], axis=1)
    return jnp.asarray(np.where(np.stack([both, first]), 0.0, NEG).reshape(2, -1), F32)


def _stack_heads(x, lo):
    return jnp.concatenate([jnp.where(lo, x, 0.0), jnp.where(lo, 0.0, x)], axis=0)


def _unstack_heads(x2, lo):
    return jnp.where(lo, x2[:DIL_BLOCK], x2[DIL_BLOCK:])


def _dil_pairs(d):
    return 4 if d == 1 else 1


def _sub_rows(r, d):
    return pl.ds(r, DIL_BLOCK, stride=d) if d > 1 else pl.ds(0, DIL_BLOCK)


def _store_piece(scratch, i, part, piece):
    if part is None:
        scratch[i] = piece
    else:
        scratch[i, pl.ds(DIL_BLOCK * part, DIL_BLOCK), :] = piece


def _split_subsequences(loads, d, P, stage=None):
    if d == 16:
        group = 4 * DIL_BLOCK
        for block, scratch, part in loads:
            for a in range(4):
                stage[pl.ds(a * group, group), :] = block[pl.ds(a, group, stride=4), :]
            for a in range(4):
                for b in range(4):
                    _store_piece(scratch, a + 4 * b, part, stage[pl.ds(a * group + b, DIL_BLOCK, stride=4), :])
        return
    for r in range(d):
        for p in range(P):
            for block, scratch, part in loads:
                _store_piece(scratch, r * P + p, part, block[_sub_rows(r, d), pl.ds(128 * p, 128)])


def _keep_previous_block(scratches, n):
    for scratch in scratches:
        @pl.when(n == 0)
        def _():
            scratch[:, pl.ds(0, DIL_BLOCK), :] = jnp.zeros((scratch.shape[0], DIL_BLOCK, 128), F32)

        @pl.when(n > 0)
        def _():
            scratch[:, pl.ds(0, DIL_BLOCK), :] = scratch[:, pl.ds(DIL_BLOCK, DIL_BLOCK), :]


def _merge_subsequences(stores, d, P, stage=None):
    if d == 16:
        group = 4 * DIL_BLOCK
        for block, scratch, plus in stores:
            for a in range(4):
                for b in range(4):
                    stage[pl.ds(a * group + b, DIL_BLOCK, stride=4), :] = scratch[a + 4 * b]
            for a in range(4):
                rows = pl.ds(a, group, stride=4)
                val = stage[pl.ds(a * group, group), :]
                block[rows, :] = val if plus is None else val + plus[rows, :]
        return
    for r in range(d):
        for p in range(P):
            for block, scratch, plus in stores:
                part = _sub_rows(r, d), pl.ds(128 * p, 128)
                block[part] = scratch[r * P + p] if plus is None else scratch[r * P + p] + plus[part]


def _dil_fwd(qh, kh, proj, bias, d, prev):
    T = proj.shape[0]
    P = _dil_pairs(d)
    rows, cw, n_it = DIL_BLOCK * d, 128 * P, d * P
    nblk = T // rows
    has_prev = prev is not None

    def body(*refs):
        q_ref, kc_ref, vc_ref, bias_ref = refs[:4]
        refs = refs[4:]
        if has_prev:
            oin_ref, lin_ref = refs[:2]
            refs = refs[2:]
        o_ref, l_ref, stage, qs, ks, vs, os_, ls_ = refs[:8]
        pb, n = pl.program_id(0), pl.program_id(1)
        lo = _first_head_lanes()
        first = (n == 0).astype(jnp.int32)
        _keep_previous_block((ks, vs), n)
        loads = [(q_ref, qs, None), (kc_ref, ks, 1), (vc_ref, vs, 1)]
        if has_prev:
            ois, lis = refs[8:]
            loads += [(oin_ref, ois, None), (lin_ref, lis, None)]
        _split_subsequences(loads, d, P, stage)

        def step(i, carry):
            q2 = _stack_heads(qs[i], lo).astype(BF16)
            s = _dot_nt(q2, ks[i].astype(BF16)) + bias_ref[first, pb * P + i % P]
            m = jnp.max(s, axis=-1, keepdims=True)
            p = jnp.exp(s - m)
            l = jnp.sum(p, axis=-1, keepdims=True)
            o = _unstack_heads(_dot(p.astype(BF16), vs[i].astype(BF16)) / l, lo)
            lse = _unstack_heads(jnp.broadcast_to(m + jnp.log(l), (2 * DIL_BLOCK, 128)), lo)
            if has_prev:
                lin = lis[i]
                mx = jnp.maximum(lin, lse)
                lnew = mx + jnp.log(jnp.exp(lin - mx) + jnp.exp(lse - mx))
                o = ois[i] * jnp.exp(lin - lnew) + o * jnp.exp(lse - lnew)
                lse = lnew
            os_[i] = o
            ls_[i] = lse
            return carry

        lax.fori_loop(0, n_it, step, 0, unroll=min(n_it, 8))
        _merge_subsequences([(o_ref, os_, None), (l_ref, ls_, None)], d, P, stage)

    blk = (rows, cw)
    vcol = 2 * DIL_WIDTH // cw
    tok = pl.BlockSpec(blk, lambda pb, n: (n, pb))
    in_specs = [tok, tok, pl.BlockSpec(blk, lambda pb, n: (n, vcol + pb)),
                pl.BlockSpec(bias.shape, lambda pb, n: (0, 0, 0, 0))]
    args = [qh, kh, proj, bias]
    one, two = pltpu.VMEM((n_it, DIL_BLOCK, 128), F32), pltpu.VMEM((n_it, 2 * DIL_BLOCK, 128), F32)
    scratch = [pltpu.VMEM((rows, 128), F32), one, two, two, one, one]
    if has_prev:
        in_specs += [tok, tok]
        args += list(prev)
        scratch += [one, one]
    out = jax.ShapeDtypeStruct((T, DIL_WIDTH), F32)
    return pl.pallas_call(
        body, name=f"dil_fwd_d{d}", grid=(DIL_HEADS // 2 // P, nblk), in_specs=in_specs, out_specs=[tok, tok],
        out_shape=[out, out], scratch_shapes=scratch, compiler_params=_params(2),
    )(*args)


def _dil_bwd(qh, kh, proj, o, lse, do, bias, d, prev):
    T = proj.shape[0]
    P = _dil_pairs(d)
    rows, cw, n_it = DIL_BLOCK * d, 128 * P, d * P
    nblk = T // rows
    has_prev = prev is not None

    def body(*refs):
        q_ref, kc_ref, vc_ref, o_ref, l_ref, do_ref, bias_ref = refs[:7]
        dqi_ref, dki_ref, dvi_ref = refs[7:10] if has_prev else (None, None, None)
        dq_ref, dk_ref, dv_ref, db_ref, stage, qs, ks, vs, os_, ls_, dos, dqs, dks, dvs, ck, cv = refs[7 + 3 * has_prev:]
        pb, n = pl.program_id(0), pl.program_id(1)
        lo = _first_head_lanes()
        first = (n == 0).astype(jnp.int32)

        @pl.when((pb == 0) & (n == 0))
        def _():
            db_ref[...] = jnp.zeros_like(db_ref)

        @pl.when(n == 0)
        def _():
            ck[...] = jnp.zeros_like(ck)
            cv[...] = jnp.zeros_like(cv)

        _keep_previous_block((ks, vs), n)
        _split_subsequences([(q_ref, qs, None), (kc_ref, ks, 1), (vc_ref, vs, 1),
                             (o_ref, os_, None), (l_ref, ls_, None), (do_ref, dos, None)], d, P, stage)

        def step(i, carry):
            pair = pb * P + i % P
            q2 = _stack_heads(qs[i], lo).astype(BF16)
            kcat, vcat = ks[i].astype(BF16), vs[i].astype(BF16)
            dov = dos[i]
            do2 = _stack_heads(dov, lo).astype(BF16)
            delta = jnp.sum(_stack_heads(dov * os_[i], lo), axis=-1, keepdims=True)
            lse_pair = ls_[i]
            lse2 = jnp.concatenate([_pair_col(lse_pair, lo, 0), _pair_col(lse_pair, lo, 1)], axis=0)
            s = _dot_nt(q2, kcat) + bias_ref[first, pair]
            p = jnp.exp(s - lse2)
            ds = p * (_dot_nt(do2, vcat) - delta)
            db_ref[pair] += ds
            dsb = ds.astype(BF16)
            dqs[i] = _unstack_heads(_dot(dsb, kcat), lo)
            dk2 = _dot_tn(dsb, q2)
            dv2 = _dot_tn(p.astype(BF16), do2)
            dks[i] = ck[i] + dk2[:DIL_BLOCK]
            dvs[i] = cv[i] + dv2[:DIL_BLOCK]
            ck[i] = dk2[DIL_BLOCK:]
            cv[i] = dv2[DIL_BLOCK:]
            return carry

        @pl.when(n < nblk)
        def _():
            lax.fori_loop(0, n_it, step, 0, unroll=min(n_it, 8))
            _merge_subsequences([(dq_ref, dqs, dqi_ref), (dk_ref, dks, dki_ref), (dv_ref, dvs, dvi_ref)], d, P, stage)

        @pl.when(n == nblk)
        def _():
            _merge_subsequences([(dk_ref, ck, dki_ref), (dv_ref, cv, dvi_ref)], d, P, stage)

    blk = (rows, cw)
    vcol = 2 * DIL_WIDTH // cw
    qn_ = lambda n: jnp.minimum(n, nblk - 1)
    pn_ = lambda n: jnp.maximum(n - 1, 0)
    fix3 = lambda pb, n: (0, 0, 0)
    tok_q = pl.BlockSpec(blk, lambda pb, n: (qn_(n), pb))
    tok_p = pl.BlockSpec(blk, lambda pb, n: (pn_(n), pb))
    in_specs = [tok_q, tok_q, pl.BlockSpec(blk, lambda pb, n: (qn_(n), vcol + pb)), tok_q, tok_q, tok_q,
                pl.BlockSpec(bias.shape, lambda pb, n: (0, 0, 0, 0))]
    in_specs += [tok_q, tok_p, tok_p] if has_prev else []
    tok_shape = jax.ShapeDtypeStruct((T, DIL_WIDTH), F32)
    one, two = pltpu.VMEM((n_it, DIL_BLOCK, 128), F32), pltpu.VMEM((n_it, 2 * DIL_BLOCK, 128), F32)
    dq, dk, dv, db = pl.pallas_call(
        body, name=f"dil_bwd_d{d}", grid=(DIL_HEADS // 2 // P, nblk + 1), in_specs=in_specs,
        out_specs=[tok_q, tok_p, tok_p, pl.BlockSpec(bias.shape[1:], fix3)],
        out_shape=[tok_shape, tok_shape, tok_shape, jax.ShapeDtypeStruct(bias.shape[1:], F32)],
        scratch_shapes=[pltpu.VMEM((rows, 128), F32), one, two, two] + [one] * 8,
        compiler_params=_params(2),
    )(qh, kh, proj, o, lse, do, bias, *(prev or ()))
    return (dq, dk, dv), db


def _t5_bucket(dist):
    max_exact = REL_BUCKETS // 2
    dd = np.maximum(dist, 1).astype(np.float32)
    large = max_exact + (np.log(dd / max_exact) / np.log(REL_MAX_DIST / max_exact)
                         * (REL_BUCKETS - max_exact)).astype(np.int32)
    large = np.minimum(large, REL_BUCKETS - 1)
    return np.where(dist < max_exact, dist, large).astype(np.int32)


def _bucket_onehots():
    i = np.arange(DIL_BLOCK)[:, None]
    j = np.arange(DIL_BLOCK)[None, :]
    out = []
    for _, d in DIL_BRANCHES:
        dist = np.concatenate([DIL_BLOCK + i - j, i - j], axis=1)
        bucket = _t5_bucket(np.clip(dist, 0, None) * d).reshape(-1)
        out.append(jnp.asarray(np.eye(REL_BUCKETS, dtype=np.float32)[:, bucket], BF16))
    return out


def _bias_tables(rel_bias, onehots):
    n = len(onehots)

    def body(rb_ref, mask_ref, *refs):
        parts = _split3(rb_ref[...])
        for k in range(n):
            oh = refs[k][...]
            bias = _dot(parts[0], oh) + _dot(parts[1], oh) + _dot(parts[2], oh)
            refs[n + k][0] = bias + mask_ref[0:1, :]
            refs[n + k][1] = bias + mask_ref[1:2, :]

    flat = pl.pallas_call(
        body, name="bias_tables",
        out_shape=[jax.ShapeDtypeStruct((2, DIL_HEADS, 2 * DIL_BLOCK * DIL_BLOCK), F32)] * n,
        compiler_params=pltpu.CompilerParams(vmem_limit_bytes=VMEM_LIMIT),
    )(rel_bias, _window_masks(), *onehots)
    return [t.reshape(2, DIL_HEADS // 2, 2 * DIL_BLOCK, 2 * DIL_BLOCK) for t in flat]


def _bias_grad(dbs, onehots):
    n = len(dbs)
    dbs = [t.reshape(DIL_HEADS, 2 * DIL_BLOCK * DIL_BLOCK) for t in dbs]

    def body(*refs):
        acc = jnp.zeros((DIL_HEADS, REL_BUCKETS), F32)
        for k in range(n):
            oh = refs[n + k][...]
            for part in _split3(refs[k][...]):
                acc = acc + _dot_nt(part, oh)
        refs[-1][...] = acc

    return pl.pallas_call(
        body, name="bias_grad",
        out_shape=jax.ShapeDtypeStruct((DIL_HEADS, REL_BUCKETS), F32),
        compiler_params=pltpu.CompilerParams(vmem_limit_bytes=VMEM_LIMIT),
    )(*dbs, *onehots)


def _swap_halves(x):
    lane = lax.broadcasted_iota(jnp.int32, x.shape, 1)
    first = (lane % 64) < 32
    return jnp.where(first, pltpu.roll(x, 96, 1), pltpu.roll(x, 32, 1))


def _rope_tables(T):
    pos = jnp.arange(T, dtype=F32)
    inv_freq = ROPE_BASE ** (-jnp.arange(0, MLA_ROPE, 2, dtype=F32) / MLA_ROPE)
    ang = pos[:, None] * inv_freq[None, :]
    z = jnp.zeros((T, 128 - MLA_ROPE), F32)
    cos = jnp.concatenate([jnp.cos(ang), jnp.cos(ang), z], axis=-1)
    sin = jnp.concatenate([-jnp.sin(ang), jnp.sin(ang), z], axis=-1)
    return cos, sin


def _mla_prep(proj, cos, sin, g_qa, g_kva, g_q, g_k, wq, wkv, tm=1024):
    T = proj.shape[0]
    H = MLA_HEADS
    scale = MLA_QK ** -0.5

    def body(cq_ref, ckv_ref, kpe_ref, cos_ref, sin_ref, gqa_ref, gkva_ref, gq_ref, gk_ref, wq_ref, wkv_ref,
             q_ref, k_ref, v_ref):
        cosv, sinv = cos_ref[...], sin_ref[...]

        def rope(x):
            return x * cosv + _swap_halves(x) * sinv

        cq = cq_ref[...]
        qp = _dot_nt((cq * _rstd(cq) * gqa_ref[...]).astype(BF16), wq_ref[...])
        ckv = ckv_ref[...]
        kvp = _dot((ckv * _rstd(ckv) * gkva_ref[...]).astype(BF16), wkv_ref[...])
        kpe = kpe_ref[...]
        one_hot_lane = (lax.broadcasted_iota(jnp.int32, (tm, 128), 1) == 0).astype(BF16)
        for h in range(H):
            a = qp[:, MLA_PAD * h:MLA_PAD * (h + 1)]
            qn = a * _rstd(a, MLA_QK) * gq_ref[...]
            q_ref[h, :, 0:128] = (qn[:, 0:128] * scale).astype(BF16)
            q_ref[h, :, 128:256] = (rope(qn[:, 128:256]) * scale).astype(BF16)
            kn = kvp[:, MLA_PAD * h:MLA_PAD * h + 128]
            r = lax.rsqrt((jnp.sum(kn * kn, axis=-1, keepdims=True)
                           + jnp.sum(kpe * kpe, axis=-1, keepdims=True)) / MLA_QK + EPS)
            k_ref[h, :, 0:128] = (kn * r * gk_ref[:, 0:128]).astype(BF16)
            k_ref[h, :, 128:256] = rope(kpe * r * gk_ref[:, 128:256]).astype(BF16)
            v_ref[h, :, 0:128] = kvp[:, MLA_PAD * h + 128:MLA_PAD * (h + 1)].astype(BF16)
            v_ref[h, :, 128:256] = one_hot_lane

    fix = lambda i: (0, 0)
    return pl.pallas_call(
        body, name="mla_prep", grid=(T // tm,),
        in_specs=[pl.BlockSpec((tm, MLA_Q_RANK), lambda i: (i, CQ_COL // MLA_Q_RANK)),
                  pl.BlockSpec((tm, MLA_KV_RANK), lambda i: (i, CKV_COL // MLA_KV_RANK)),
                  pl.BlockSpec((tm, 128), lambda i: (i, KPE_COL // 128)),
                  pl.BlockSpec((tm, 128), lambda i: (i, 0)), pl.BlockSpec((tm, 128), lambda i: (i, 0)),
                  pl.BlockSpec((1, 256), fix), pl.BlockSpec((1, 128), fix),
                  pl.BlockSpec((1, 256), fix), pl.BlockSpec((1, 256), fix),
                  pl.BlockSpec((H * MLA_PAD, 256), fix), pl.BlockSpec((128, H * MLA_PAD), fix)],
        out_specs=[pl.BlockSpec((H, tm, MLA_PAD), lambda i: (0, i, 0)), pl.BlockSpec((H, tm, MLA_PAD), lambda i: (0, i, 0)),
                   pl.BlockSpec((H, tm, 2 * MLA_V), lambda i: (0, i, 0))],
        out_shape=[jax.ShapeDtypeStruct((H, T, MLA_PAD), BF16), jax.ShapeDtypeStruct((H, T, MLA_PAD), BF16),
                   jax.ShapeDtypeStruct((H, T, 2 * MLA_V), BF16)],
        compiler_params=_params(1),
    )(proj, proj, proj, cos, sin, g_qa, g_kva, g_q, g_k, wq, wkv)


def _mla_prep_bwd(proj, cos, sin, g_qa, g_kva, g_q, g_k, wq, wkv, dq, dk, dv, tm=1024):
    T = proj.shape[0]
    H = MLA_HEADS
    scale = MLA_QK ** -0.5

    def body(cq_ref, ckv_ref, kpe_ref, cos_ref, sin_ref, gqa_ref, gkva_ref, gq_ref, gk_ref, wq_ref, wkv_ref,
             dq_ref, dk_ref, dv_ref,
             dcq_ref, dckv_ref, dkpe_ref, cqn_ref, ckvn_ref, dqp_ref, dkvp_ref,
             dgqa_ref, dgkva_ref, dgq_ref, dgk_ref):
        @pl.when(pl.program_id(0) == 0)
        def _():
            for ref in (dgqa_ref, dgkva_ref, dgq_ref, dgk_ref):
                ref[...] = jnp.zeros_like(ref)

        cosv, sinv = cos_ref[...], sin_ref[...]

        def rope_bwd(dy):
            return dy * cosv + _swap_halves(dy * sinv)

        cq = cq_ref[...]
        rcq = _rstd(cq)
        cqn = (cq * rcq * gqa_ref[...]).astype(BF16)
        cqn_ref[...] = cqn
        qp = _dot_nt(cqn, wq_ref[...])
        ckv = ckv_ref[...]
        rckv = _rstd(ckv)
        ckvn = (ckv * rckv * gkva_ref[...]).astype(BF16)
        ckvn_ref[...] = ckvn
        kvp = _dot(ckvn, wkv_ref[...])
        kpe = kpe_ref[...]
        dkpe = jnp.zeros_like(kpe)
        dgq = jnp.zeros((1, MLA_PAD), F32)
        dgk = jnp.zeros((1, MLA_PAD), F32)
        for h in range(H):
            a = qp[:, MLA_PAD * h:MLA_PAD * (h + 1)]
            dqh = dq_ref[h]
            dn = jnp.concatenate([dqh[:, 0:128], rope_bwd(dqh[:, 128:256])], axis=-1) * scale
            da, dg = _rms_bwd(dn, a, gq_ref[...], _rstd(a, MLA_QK), MLA_QK)
            dgq = dgq + jnp.sum(dg, axis=0, keepdims=True)
            dqp_ref[:, MLA_PAD * h:MLA_PAD * (h + 1)] = da.astype(BF16)

            ak = jnp.concatenate([kvp[:, MLA_PAD * h:MLA_PAD * h + 128], kpe], axis=-1)
            dkh = dk_ref[h]
            dnk = jnp.concatenate([dkh[:, 0:128], rope_bwd(dkh[:, 128:256])], axis=-1)
            dak, dg = _rms_bwd(dnk, ak, gk_ref[...], _rstd(ak, MLA_QK), MLA_QK)
            dgk = dgk + jnp.sum(dg, axis=0, keepdims=True)
            dkpe = dkpe + dak[:, 128:256]
            dkvp_ref[:, MLA_PAD * h:MLA_PAD * h + 128] = dak[:, 0:128].astype(BF16)
            dkvp_ref[:, MLA_PAD * h + 128:MLA_PAD * (h + 1)] = dv_ref[h].astype(BF16)
        dkpe_ref[...] = dkpe
        dgq_ref[...] += dgq
        dgk_ref[...] += dgk
        dcq, dg = _rms_bwd(_dot(dqp_ref[...], wq_ref[...]), cq, gqa_ref[...], rcq)
        dcq_ref[...] = dcq
        dgqa_ref[...] += jnp.sum(dg, axis=0, keepdims=True)
        dckv, dg = _rms_bwd(_dot_nt(dkvp_ref[...], wkv_ref[...]), ckv, gkva_ref[...], rckv)
        dckv_ref[...] = dckv
        dgkva_ref[...] += jnp.sum(dg, axis=0, keepdims=True)

    fix = lambda i: (0, 0)
    row = lambda i: (i, 0)
    head = lambda i: (0, i, 0)
    return pl.pallas_call(
        body, name="mla_prep_bwd", grid=(T // tm,),
        in_specs=[pl.BlockSpec((tm, MLA_Q_RANK), lambda i: (i, CQ_COL // MLA_Q_RANK)),
                  pl.BlockSpec((tm, MLA_KV_RANK), lambda i: (i, CKV_COL // MLA_KV_RANK)),
                  pl.BlockSpec((tm, 128), lambda i: (i, KPE_COL // 128)),
                  pl.BlockSpec((tm, 128), row), pl.BlockSpec((tm, 128), row),
                  pl.BlockSpec((1, 256), fix), pl.BlockSpec((1, 128), fix),
                  pl.BlockSpec((1, 256), fix), pl.BlockSpec((1, 256), fix),
                  pl.BlockSpec((H * MLA_PAD, 256), fix), pl.BlockSpec((128, H * MLA_PAD), fix),
                  pl.BlockSpec((H, tm, MLA_PAD), head), pl.BlockSpec((H, tm, MLA_PAD), head),
                  pl.BlockSpec((H, tm, MLA_V), head)],
        out_specs=[pl.BlockSpec((tm, 256), row), pl.BlockSpec((tm, 128), row), pl.BlockSpec((tm, 128), row),
                   pl.BlockSpec((tm, 256), row), pl.BlockSpec((tm, 128), row),
                   pl.BlockSpec((tm, H * MLA_PAD), row), pl.BlockSpec((tm, H * MLA_PAD), row),
                   pl.BlockSpec((1, 256), fix), pl.BlockSpec((1, 128), fix),
                   pl.BlockSpec((1, 256), fix), pl.BlockSpec((1, 256), fix)],
        out_shape=[jax.ShapeDtypeStruct((T, 256), F32), jax.ShapeDtypeStruct((T, 128), F32),
                   jax.ShapeDtypeStruct((T, 128), F32),
                   jax.ShapeDtypeStruct((T, 256), BF16), jax.ShapeDtypeStruct((T, 128), BF16),
                   jax.ShapeDtypeStruct((T, H * MLA_PAD), BF16), jax.ShapeDtypeStruct((T, H * MLA_PAD), BF16),
                   jax.ShapeDtypeStruct((1, 256), F32), jax.ShapeDtypeStruct((1, 128), F32),
                   jax.ShapeDtypeStruct((1, 256), F32), jax.ShapeDtypeStruct((1, 256), F32)],
        compiler_params=_params(1),
    )(proj, proj, proj, cos, sin, g_qa, g_kva, g_q, g_k, wq, wkv, dq, dk, dv)


def _causal_pairs(T, tq, tk, key_major):
    pairs = [(i, j) for i in range(T // tq) for j in range(T // tk) if j * tk <= i * tq + tq - 1]
    if key_major:
        pairs.sort(key=lambda p: (p[1], p[0]))
    outer = [p[1] if key_major else p[0] for p in pairs]
    first = [int(t == 0 or outer[t] != outer[t - 1]) for t in range(len(pairs))]
    last = [int(t == len(pairs) - 1 or outer[t] != outer[t + 1]) for t in range(len(pairs))]
    tab = lambda v: jnp.asarray(np.array(v, np.int32))
    return tab([p[0] for p in pairs]), tab([p[1] for p in pairs]), tab(first), tab(last)


def _causal_scores(qv, kv, row0):
    s = _dot_nt(qv, kv)
    if row0 is not None:
        row = lax.broadcasted_iota(jnp.int32, s.shape, 0) + row0
        col = lax.broadcasted_iota(jnp.int32, s.shape, 1)
        s = jnp.where(col <= row, s, NEG)
    return s


def _causal_variants(qi, ki, tq, tk, update):
    assert tk % tq == 0
    diag = qi * tq - ki * tk
    for off in range(0, tk, tq):
        pl.when(diag == off)(lambda off=off: update(off))
    pl.when(diag >= tk)(lambda: update(None))


def _visible_keys(off, row0, rows, tk):
    return tk if off is None else min(tk, off + row0 + rows)


def _mla_attn(q, k, v, ride=None, tq=2048, tk=4096, rc=256):
    H, T, _ = q.shape
    tq, tk = min(tq, T), min(tk, T)
    tables = _causal_pairs(T, tq, tk, key_major=False)
    n_pairs = int(tables[0].shape[0])
    r_args, r_in, r_shape, r_out, r_scratch = _ride_parts(ride)

    def body(qt, kt, ft, lt, q_ref, k_ref, v_ref, o_ref, lse_ref, m_s, acc):
        t = pl.program_id(1)
        qi, ki = qt[t], kt[t]

        @pl.when(ft[t] == 1)
        def _():
            m_s[...] = jnp.full_like(m_s, NEG)
            acc[...] = jnp.zeros_like(acc)

        def update(off):
            for c in range(tq // rc):
                rows = pl.ds(c * rc, rc)
                keys = pl.ds(0, _visible_keys(off, c * rc, rc, tk))
                s = _causal_scores(q_ref[rows, :], k_ref[keys, :], None if off is None else off + c * rc)
                m_old = m_s[rows, :]
                m_new = jnp.maximum(m_old, jnp.max(s, axis=-1, keepdims=True))
                p = jnp.exp(s - m_new).astype(BF16)
                acc[rows, :] = jnp.exp(m_old - m_new) * acc[rows, :] + _dot(p, v_ref[keys, :])
                m_s[rows, :] = m_new

        _causal_variants(qi, ki, tq, tk, update)

        @pl.when(lt[t] == 1)
        def _():
            l = jnp.max(acc[:, MLA_V:], axis=-1, keepdims=True)
            o_ref[...] = acc[:, :MLA_V] / l
            lse_ref[...] = jnp.broadcast_to(m_s[...] + jnp.log(l), lse_ref.shape)

    qrow = lambda h, t, qt, kt, ft, lt: (h, qt[t], 0)
    krow = lambda h, t, qt, kt, ft, lt: (h, kt[t], 0)
    first = lambda: (pl.program_id(0) == 0) & (pl.program_id(1) == 0)
    last = lambda: (pl.program_id(0) == H - 1) & (pl.program_id(1) == n_pairs - 1)
    outs = pl.pallas_call(
        _riding(body, 7, 2, 2, ride, first, last), name="mla_attn",
        grid_spec=pltpu.PrefetchScalarGridSpec(
            num_scalar_prefetch=4, grid=(H, n_pairs),
            in_specs=[pl.BlockSpec((None, tq, MLA_PAD), qrow), pl.BlockSpec((None, tk, MLA_PAD), krow),
                      pl.BlockSpec((None, tk, 2 * MLA_V), krow)] + r_in,
            out_specs=[pl.BlockSpec((tq, MLA_V), lambda h, t, qt, kt, ft, lt: (qt[t], h)),
                       pl.BlockSpec((None, tq, 128), qrow)] + r_out,
            scratch_shapes=[pltpu.VMEM((tq, 1), F32), pltpu.VMEM((tq, 2 * MLA_V), F32)] + r_scratch),
        out_shape=[jax.ShapeDtypeStruct((T, H * MLA_V), F32), jax.ShapeDtypeStruct((H, T, 128), F32)] + r_shape,
        compiler_params=_params(2),
    )(*tables, q, k, v, *r_args)
    return outs[:2], outs[2:]


def _mla_attn_bwd(q, k, v, o, lse, do, ride=None, tq=2048, tk=2048, rc=512, rc_diagonal=256):
    H, T, _ = q.shape
    tq, tk = min(tq, T), min(tk, T)
    tables = _causal_pairs(T, tq, tk, key_major=True)
    n_pairs = int(tables[0].shape[0])
    r_args, r_in, r_shape, r_out, r_scratch = _ride_parts(ride)

    def body(qt, kt, ft, lt, q_ref, k_ref, v_ref, o_ref, lse_ref, do_ref, dq_ref, dk_ref, dv_ref, dk_s, dv_s):
        t = pl.program_id(1)
        qi, ki = qt[t], kt[t]

        @pl.when(t == 0)
        def _():
            dq_ref[...] = jnp.zeros_like(dq_ref)

        @pl.when(ft[t] == 1)
        def _():
            dk_s[...] = jnp.zeros_like(dk_s)
            dv_s[...] = jnp.zeros_like(dv_s)

        def update(off):
            rows_per = rc if off is None else rc_diagonal
            for c in range(tq // rows_per):
                rows = pl.ds(c * rows_per, rows_per)
                keys = pl.ds(0, _visible_keys(off, c * rows_per, rows_per, tk))
                kk, vv = k_ref[keys, :], v_ref[keys, :]
                qv, dov = q_ref[rows, :], do_ref[rows, :]
                delta = jnp.sum(dov * o_ref[rows, :], axis=-1, keepdims=True)
                lse_v = jnp.max(lse_ref[rows, :], axis=-1, keepdims=True)
                p = jnp.exp(_causal_scores(qv, kk, None if off is None else off + c * rows_per) - lse_v)
                dob = dov.astype(BF16)
                dv_s[keys, :] += _dot_tn(p.astype(BF16), dob)
                ds = (p * (_dot_nt(dob, vv) - delta)).astype(BF16)
                dk_s[keys, :] += _dot_tn(ds, qv)
                out_rows = pl.ds(pl.multiple_of(qi * tq + c * rows_per, rows_per), rows_per)
                dq_ref[out_rows, :] += _dot(ds, kk)

        _causal_variants(qi, ki, tq, tk, update)

        @pl.when(lt[t] == 1)
        def _():
            dk_ref[...] = dk_s[...]
            dv_ref[...] = dv_s[...]

    qrow = lambda h, t, qt, kt, ft, lt: (h, qt[t], 0)
    krow = lambda h, t, qt, kt, ft, lt: (h, kt[t], 0)
    qcol = lambda h, t, qt, kt, ft, lt: (qt[t], h)
    first = lambda: (pl.program_id(0) == 0) & (pl.program_id(1) == 0)
    last = lambda: (pl.program_id(0) == H - 1) & (pl.program_id(1) == n_pairs - 1)
    outs = pl.pallas_call(
        _riding(body, 10, 3, 2, ride, first, last), name="mla_attn_bwd",
        grid_spec=pltpu.PrefetchScalarGridSpec(
            num_scalar_prefetch=4, grid=(H, n_pairs),
            in_specs=[pl.BlockSpec((None, tq, MLA_PAD), qrow), pl.BlockSpec((None, tk, MLA_PAD), krow),
                      pl.BlockSpec((None, tk, MLA_V), krow), pl.BlockSpec((tq, MLA_V), qcol),
                      pl.BlockSpec((None, tq, 128), qrow), pl.BlockSpec((tq, MLA_V), qcol)] + r_in,
            out_specs=[pl.BlockSpec((None, T, MLA_PAD), lambda h, t, qt, kt, ft, lt: (h, 0, 0)),
                       pl.BlockSpec((None, tk, MLA_PAD), krow), pl.BlockSpec((None, tk, MLA_V), krow)] + r_out,
            scratch_shapes=[pltpu.VMEM((tk, MLA_PAD), F32), pltpu.VMEM((tk, MLA_V), F32)] + r_scratch),
        out_shape=[jax.ShapeDtypeStruct((H, T, MLA_PAD), F32), jax.ShapeDtypeStruct((H, T, MLA_PAD), F32),
                   jax.ShapeDtypeStruct((H, T, MLA_V), F32)] + r_shape,
        compiler_params=_params(2),
    )(*tables, q, k, v, o, lse, do, *r_args)
    return outs[:3], outs[3:]


def _pair_gain(g):
    return jnp.tile(g.reshape(1, DIL_HD), (1, 2))


def _pad_gain(g):
    return jnp.pad(g.reshape(1, MLA_QK), ((0, 0), (0, MLA_PAD - MLA_QK)))


def _local_step(x, target, s, comm):
    T = x.shape[0]
    w = comm.w
    gq, gk = _pair_gain(s["dil_q_norm"]) * DIL_HD ** -0.5, _pair_gain(s["dil_k_norm"])
    g_q, g_k = _pad_gain(s["mla_q_norm"]), _pad_gain(s["mla_k_norm"])
    cos, sin = _rope_tables(T)
    onehots = _bucket_onehots()
    biases = _bias_tables(s["rel_bias"], onehots)

    (x1, h1, gate1, up1), got = _ffn_fwd(x, s["ffn1_norm"], w["ffn1_w_gate"], w["ffn1_w_up"], w["ffn1_w_down"],
                                         ride=comm.gather(("w_in",)))
    comm.weights_landed(("w_in",), got)
    later = tuple(n for n in _GROUPS["attn"] if n != "w_in")
    (hm, proj, qh, kh), got = _in_proj(x1, s["mix_norm"], w["w_in"], gq, gk, ride=comm.gather(later))
    comm.weights_landed(later, got)
    dil = None
    for (_, d), bias in zip(DIL_BRANCHES, biases):
        dil = _dil_fwd(qh, kh, proj, bias, d, dil)
    o_dil, lse_dil = dil
    q, k, v = _mla_prep(proj, cos, sin, s["mla_q_a_norm"], s["mla_kv_a_norm"], g_q, g_k, w["mla_w_q_b"], w["mla_w_kv_b"])
    (o_mla, lse_mla), got = _mla_attn(q, k, v, ride=comm.gather(_GROUPS["ffn2"]))
    comm.weights_landed(_GROUPS["ffn2"], got)
    x2, oc = _out_proj(x1, o_dil, o_mla, s["out_norm_dil"], s["out_norm_mla"], w["w_out"])
    (dy, h2, gate2, up2, loss), _ = _ffn_fwd(x2, s["ffn2_norm"], w["ffn2_w_gate"], w["ffn2_w_up"], w["ffn2_w_down"],
                                             target=target)

    gw, gs = {}, {}

    def ffn_grads(name, dy_in, x_in, h, gate, up, early=None):
        dx, a, dg, du, dyh, dgain = _ffn_bwd(dy_in, x_in, s[name + "_norm"], gate, up,
                                             w[name + "_w_gate"], w[name + "_w_up"], w[name + "_w_down"])
        gs[name + "_norm"] = dgain
        down, gate_n, up_n = (name + "_w_down",), (name + "_w_gate",), (name + "_w_up",)
        ride = lambda names: comm.scatter(names, gw) if early is not None else None
        gw[down[0]], landed = _matmul_tn(a, dyh, 1408, 1024, ride=ride(early))
        comm.grads_landed(early or (), landed)
        gw[gate_n[0]], landed = _matmul_tn(dg, h, 1408, 1024, ride=ride(down))
        comm.grads_landed(down, landed)
        gw[up_n[0]], landed = _matmul_tn(du, h, 1408, 1024, ride=ride(gate_n))
        comm.grads_landed(gate_n, landed)
        return dx

    dx2 = ffn_grads("ffn2", dy, x2, h2, gate2, up2)
    gw["w_out"], _ = _matmul_tn(oc, dx2, 1024, 1024)
    do_dil, do_mla, gs["out_norm_dil"], gs["out_norm_mla"] = _out_proj_bwd(
        dx2, o_dil, o_mla, s["out_norm_dil"], s["out_norm_mla"], w["w_out"])

    (dq, dk, dv), got = _mla_attn_bwd(q, k, v, o_mla, lse_mla, do_mla, ride=comm.scatter(_GROUPS["ffn2"], gw))
    comm.grads_landed(_GROUPS["ffn2"], got)
    (dcq, dckv, dkpe, cqn, ckvn, dqp, dkvp, gs["mla_q_a_norm"], gs["mla_kv_a_norm"], dg_q, dg_k) = _mla_prep_bwd(
        proj, cos, sin, s["mla_q_a_norm"], s["mla_kv_a_norm"], g_q, g_k, w["mla_w_q_b"], w["mla_w_kv_b"], dq, dk, dv)
    gs["mla_q_norm"], gs["mla_k_norm"] = dg_q[:, :MLA_QK], dg_k[:, :MLA_QK]
    gw["mla_w_q_b"], _ = _matmul_tn(dqp, cqn, 1024, 256)
    gw["mla_w_kv_b"], _ = _matmul_tn(ckvn, dkvp, 128, 1024)

    dqkv, dbs = None, []
    for (_, d), bias in reversed(list(zip(DIL_BRANCHES, biases))):
        dqkv, db = _dil_bwd(qh, kh, proj, o_dil, lse_dil, do_dil, bias, d, dqkv)
        dbs.insert(0, db)
    dqkv = [dqkv]
    gs["rel_bias"] = _bias_grad(dbs, onehots)

    ready = tuple(n for n in _GROUPS["attn"] if n != "w_in")
    (dx1, dproj, gs["mix_norm"], dgq, dgk), got = _in_proj_bwd(dx2, x1, s["mix_norm"], w["w_in"], proj, gq, gk,
                                                               dqkv, dcq, dckv, dkpe, ride=comm.scatter(ready, gw))
    comm.grads_landed(ready, got)
    gs["dil_q_norm"] = (dgq[:, :DIL_HD] + dgq[:, DIL_HD:]) * DIL_HD ** -0.5
    gs["dil_k_norm"] = dgk[:, :DIL_HD] + dgk[:, DIL_HD:]
    gw["w_in"], _ = _matmul_tn(dproj, hm, 1024, 1024)
    grad_x = ffn_grads("ffn1", dx1, x, h1, gate1, up1, early=("w_in",))
    return loss, grad_x, gw, gs


def _position():
    x, y, c = lax.axis_index("x"), lax.axis_index("y"), lax.axis_index("c")
    return x, y, c, 4 * x + 2 * y + c


def _peer(x, y, c, k):
    px = 1 - x if k & 4 else x
    py = 1 - y if k & 2 else y
    pc = 1 - c if k & 1 else c
    return (px, py, pc), 4 * px + 2 * py + pc


class _Ride:
    def __init__(self, arrays, scatter):
        self.arrays, self.scatter = list(arrays), list(scatter)
        self.n = n = len(self.arrays)
        self.specs = [pl.BlockSpec(memory_space=pl.ANY)] * n
        self.out_shape = [jax.ShapeDtypeStruct(a.shape if sc else (N_DEV,) + a.shape, a.dtype)
                          for a, sc in zip(self.arrays, self.scatter)]
        self.scratch = [pltpu.SemaphoreType.DMA((n, N_DEV - 1)), pltpu.SemaphoreType.DMA((n, N_DEV - 1)),
                        pltpu.SemaphoreType.DMA((n,))]

    def _copies(self, ins, outs, sems):
        send_sems, recv_sems, local_sems = sems
        x, y, c, me = _position()
        copies = []
        for a in range(self.n):
            src = ins[a].at[me] if self.scatter[a] else ins[a]
            copies.append(pltpu.make_async_copy(src, outs[a].at[me], local_sems.at[a]))
        for k in range(1, N_DEV):
            peer, peer_idx = _peer(x, y, c, k)
            for a in range(self.n):
                src = ins[a].at[peer_idx] if self.scatter[a] else ins[a]
                copies.append(pltpu.make_async_remote_copy(
                    src_ref=src, dst_ref=outs[a].at[me], send_sem=send_sems.at[a, k - 1], recv_sem=recv_sems.at[a, k - 1],
                    device_id=peer, device_id_type=pl.DeviceIdType.MESH))
        return copies

    def start(self, ins, outs, sems):
        for cp in self._copies(ins, outs, sems):
            cp.start()

    def wait(self, ins, outs, sems):
        for cp in self._copies(ins, outs, sems):
            cp.wait()


def _ride_parts(ride):
    if ride is None:
        return [], [], [], [], []
    return ride.arrays, ride.specs, ride.out_shape, ride.specs, ride.scratch


def _riding(body, n_in, n_out, n_scratch, ride, first, last):
    if ride is None:
        return body
    n = ride.n
    i1, i2 = n_in + n, n_in + n + n_out
    i3, i4 = i2 + n, i2 + n + n_scratch

    def wrapped(*refs):
        ins, outs, sems = refs[n_in:i1], refs[i2:i3], refs[i4:]

        @pl.when(first())
        def _():
            ride.start(ins, outs, sems)

        body(*refs[:n_in], *refs[i1:i2], *refs[i3:i4])

        @pl.when(last())
        def _():
            ride.wait(ins, outs, sems)

    return wrapped


def _gather_two_level(arrays, name):
    n = len(arrays)
    out_shape = [jax.ShapeDtypeStruct((N_DEV,) + a.shape, a.dtype) for a in arrays]

    def body(*refs):
        ins, outs = refs[:n], refs[n:2 * n]
        send_sems, recv_sems, local_sems = refs[2 * n:]
        x, y, c, me = _position()
        sibling = (x, y, 1 - c)
        chips = [(1 - x, y), (x, 1 - y), (1 - x, 1 - y)]
        block = lambda px, py, pc: 4 * px + 2 * py + pc

        def copy(a, k, blk, to, src=None):
            dst = outs[a].at[blk]
            return pltpu.make_async_remote_copy(
                src_ref=dst if src is None else src, dst_ref=dst, send_sem=send_sems.at[a, k], recv_sem=recv_sems.at[a, k],
                device_id=to, device_id_type=pl.DeviceIdType.MESH)

        local = [pltpu.make_async_copy(ins[a], outs[a].at[me], local_sems.at[a]) for a in range(n)]
        first = []
        for a in range(n):
            first.append(copy(a, 0, me, sibling, src=ins[a]))
            first += [copy(a, 1 + j, me, (*chip, c), src=ins[a]) for j, chip in enumerate(chips)]
        for cp in local + first:
            cp.start()
        passed = []
        for j, chip in enumerate(chips):
            for a in range(n):
                copy(a, 1 + j, block(*chip, c), sibling).wait_recv()
                passed.append(copy(a, 4 + j, block(*chip, c), sibling))
                passed[-1].start()
        for a in range(n):
            copy(a, 0, block(x, y, 1 - c), sibling).wait_recv()
            for j, chip in enumerate(chips):
                copy(a, 4 + j, block(*chip, 1 - c), sibling).wait_recv()
        for cp in first + passed:
            cp.wait_send()
        for cp in local:
            cp.wait()

    any_spec = [pl.BlockSpec(memory_space=pl.ANY)] * n
    return pl.pallas_call(
        body, name=name, in_specs=any_spec, out_specs=any_spec, out_shape=out_shape,
        scratch_shapes=[pltpu.SemaphoreType.DMA((n, N_DEV - 1)), pltpu.SemaphoreType.DMA((n, N_DEV - 1)),
                        pltpu.SemaphoreType.DMA((n,))],
    )(*arrays)


def _adamw_math(wv, g, m, v):
    m = ADAM_B1 * m + (1.0 - ADAM_B1) * g
    v = ADAM_B2 * v + (1.0 - ADAM_B2) * (g * g)
    m_hat = m / (1.0 - ADAM_B1 ** ADAM_STEP)
    v_hat = v / (1.0 - ADAM_B2 ** ADAM_STEP)
    delta = -ADAM_LR * (m_hat / (jnp.sqrt(v_hat) + ADAM_EPS) + ADAM_WD * wv)
    return delta, m, v


def _adamw(items, ride=None, max_rows=256):
    K = len(items)
    tiles, spans, start = [], [], 0
    for _, wv, _, _ in items:
        R = wv.shape[1]
        tr = max([t for t in range(16, max_rows + 1, 16) if R % t == 0] or [R])
        tiles.append(tr)
        spans.append((start, R // tr))
        start += R // tr
    total = start
    r_args, r_in, r_shape, r_out, r_scratch = _ride_parts(ride)

    def body(*refs):
        i = pl.program_id(0)
        for k, (first_step, n_steps) in enumerate(spans):
            def update(k=k):
                p_ref, w_ref, m_ref, v_ref = refs[4 * k:4 * k + 4]
                g_ref, d_ref, mo_ref, vo_ref = refs[4 * K + 4 * k:4 * K + 4 * k + 4]
                g = p_ref[0].astype(F32)
                for j in range(1, N_DEV):
                    g = g + p_ref[j].astype(F32)
                d, mn, vn = _adamw_math(w_ref[0], g, m_ref[0], v_ref[0])
                g_ref[0] = g
                d_ref[0] = d
                mo_ref[0] = mn
                vo_ref[0] = vn

            pl.when((i >= first_step) & (i < first_step + n_steps))(update)

    in_specs, out_specs, out_shape, args = [], [], [], []
    for (parts, wv, m, v), tr, (first_step, n_steps) in zip(items, tiles, spans):
        C = wv.shape[2]
        tile = lambda i, s=first_step, n=n_steps: (0, jnp.clip(i - s, 0, n - 1), 0)
        blk = pl.BlockSpec((1, tr, C), tile)
        in_specs += [pl.BlockSpec((N_DEV, tr, C), tile), blk, blk, blk]
        out_specs += [blk] * 4
        out_shape += [jax.ShapeDtypeStruct(wv.shape, F32)] * 4
        args += [parts, wv, m, v]
    outs = pl.pallas_call(
        _riding(body, 4 * K, 4 * K, 0, ride, lambda: pl.program_id(0) == 0, lambda: pl.program_id(0) == total - 1),
        name="adamw", grid=(total,),
        in_specs=in_specs + r_in, out_specs=out_specs + r_out, out_shape=out_shape + r_shape,
        scratch_shapes=r_scratch, compiler_params=_params(1),
    )(*args, *r_args)
    return [outs[4 * k:4 * k + 4] for k in range(K)], outs[4 * K:]


_TRANSPOSED = ("ffn1_w_gate", "ffn1_w_up", "ffn2_w_gate", "ffn2_w_up", "w_in", "mla_w_q_b")
_GROUPS = {"ffn1": ("ffn1_w_gate", "ffn1_w_up", "ffn1_w_down"),
           "ffn2": ("ffn2_w_gate", "ffn2_w_up", "ffn2_w_down"),
           "attn": ("w_in", "mla_w_q_b", "mla_w_kv_b", "w_out")}
_SMALL = ("ffn1_norm", "mix_norm", "ffn2_norm", "out_norm_dil", "out_norm_mla", "mla_q_a_norm", "rel_bias",
          "mla_q_norm", "mla_k_norm", "mla_kv_a_norm", "dil_q_norm", "dil_k_norm")
_SMALL_ROWS = 48


def _cols_to_full(g):
    return g.transpose(1, 0, 2).reshape(g.shape[1], N_DEV * g.shape[2])


def _full_to_cols(f):
    return f.reshape(f.shape[0], N_DEV, f.shape[1] // N_DEV).transpose(1, 0, 2)


def _shard_view(name, a):
    return jnp.swapaxes(a, 1, 2) if name in _TRANSPOSED else a


def _to_full(name, g):
    if name == "mla_w_kv_b":
        return _cols_to_full(g)
    f = g.reshape(-1, g.shape[-1])
    if name == "w_in":
        f = jnp.pad(f, ((0, PROJ_PAD - PROJ_COLS), (0, 0)))
    if name == "mla_w_q_b":
        f = jnp.pad(f.reshape(MLA_HEADS, MLA_QK, -1), ((0, 0), (0, MLA_PAD - MLA_QK), (0, 0)))
        f = f.reshape(MLA_HEADS * MLA_PAD, -1)
    return f


def _to_parts(name, f):
    if name == "mla_w_kv_b":
        return _full_to_cols(f).astype(BF16)
    if name == "w_in":
        f = f[:PROJ_COLS]
    if name == "mla_w_q_b":
        f = f.reshape(MLA_HEADS, MLA_PAD, -1)[:, :MLA_QK].reshape(MLA_HEADS * MLA_QK, -1)
    return f.reshape(N_DEV, -1, f.shape[-1]).astype(BF16)


class _Comm:
    def __init__(self, shards):
        self.shards, self.w, self.recv = shards, {}, {}

    def gather(self, names):
        return _Ride([self.shards[n] for n in names], [False] * len(names))

    def scatter(self, names, grads):
        return _Ride([_to_parts(n, grads[n]) for n in names], [True] * len(names))

    def weights_landed(self, names, got):
        self.w.update({n: _to_full(n, g) for n, g in zip(names, got)})

    def grads_landed(self, names, got):
        self.recv.update(zip(names, got))


def _pack_small(parts, extra):
    flat = jnp.concatenate([parts[n].reshape(-1) for n in _SMALL] + [extra.reshape(-1)])
    return jnp.pad(flat, (0, _SMALL_ROWS * 128 - flat.shape[0])).reshape(_SMALL_ROWS, 128)


def _unpack_small(packed, shapes):
    flat, out, off = packed.reshape(-1), {}, 0
    for n in _SMALL:
        size = math.prod(shapes[n])
        out[n] = flat[off:off + size].reshape(shapes[n])
        off += size
    return out, flat[off]


_NAMES = ("ffn1_norm", "ffn1_w_gate", "ffn1_w_up", "ffn1_w_down", "mix_norm", "w_in", "dil_q_norm", "dil_k_norm",
          "rel_bias", "mla_q_a_norm", "mla_w_q_b", "mla_kv_a_norm", "mla_w_kv_b", "mla_q_norm", "mla_k_norm",
          "out_norm_dil", "out_norm_mla", "w_out", "ffn2_norm", "ffn2_w_gate", "ffn2_w_up", "ffn2_w_down")


def kernel(x, ffn1_norm, ffn1_w_gate, ffn1_w_up, ffn1_w_down, mix_norm, w_in, dil_q_norm, dil_k_norm, rel_bias, mla_q_a_norm, mla_w_q_b, mla_kv_a_norm, mla_w_kv_b, mla_q_norm, mla_k_norm, out_norm_dil, out_norm_mla, w_out, ffn2_norm, ffn2_w_gate, ffn2_w_up, ffn2_w_down, loss_target, m_ffn1_norm, m_ffn1_w_gate, m_ffn1_w_up, m_ffn1_w_down, m_mix_norm, m_w_in, m_dil_q_norm, m_dil_k_norm, m_rel_bias, m_mla_q_a_norm, m_mla_w_q_b, m_mla_kv_a_norm, m_mla_w_kv_b, m_mla_q_norm, m_mla_k_norm, m_out_norm_dil, m_out_norm_mla, m_w_out, m_ffn2_norm, m_ffn2_w_gate, m_ffn2_w_up, m_ffn2_w_down, v_ffn1_norm, v_ffn1_w_gate, v_ffn1_w_up, v_ffn1_w_down, v_mix_norm, v_w_in, v_dil_q_norm, v_dil_k_norm, v_rel_bias, v_mla_q_a_norm, v_mla_w_q_b, v_mla_kv_a_norm, v_mla_w_kv_b, v_mla_q_norm, v_mla_k_norm, v_out_norm_dil, v_out_norm_mla, v_w_out, v_ffn2_norm, v_ffn2_w_gate, v_ffn2_w_up, v_ffn2_w_down):
    args = locals()
    wts = {n: args[n] for n in _NAMES}
    mom = {n: args["m_" + n] for n in _NAMES}
    var = {n: args["v_" + n] for n in _NAMES}

    matrices = [n for group in _GROUPS.values() for n in group]
    comm = _Comm({n: _shard_view(n, wts[n])[0].astype(BF16) for n in matrices})
    comm.weights_landed(_GROUPS["ffn1"], _gather_two_level(comm.gather(_GROUPS["ffn1"]).arrays, "gather_first"))
    small = {n: wts[n].reshape(1, -1) if n != "rel_bias" else wts[n] for n in _SMALL}

    loss, grad_x, gw, gs = _local_step(x[0], loss_target[0], small, comm)

    item = lambda n: (comm.recv[n],) + tuple(_shard_view(n, a[n]) for a in (wts, mom, var))
    landed = [n for n in matrices if n != "ffn1_w_up"]
    last = comm.scatter(("ffn1_w_up",), gw)
    updates, got = _adamw([item(n) for n in landed], max_rows=32,
                          ride=_Ride(last.arrays + [_pack_small(gs, loss[0, 0])], last.scatter + [False]))
    comm.grads_landed(("ffn1_w_up",), got[:-1])

    zero = jnp.zeros((), F32)
    small_item = (got[-1],) + tuple(_pack_small(a, zero)[None] for a in (wts, mom, var))
    (up_update, packed), _ = _adamw([item("ffn1_w_up"), small_item])
    res = {n: [_shard_view(n, r) for r in u] for n, u in zip(landed + ["ffn1_w_up"], updates + [up_update])}
    shapes = {n: wts[n].shape for n in _SMALL}
    loss_total = None
    for slot, q in enumerate(packed):
        vals, extra = _unpack_small(q, shapes)
        if slot == 0:
            loss_total = extra
        for n in _SMALL:
            res.setdefault(n, [None] * 4)[slot] = vals[n]
    outs = [loss_total, grad_x[None]]
    for slot in range(4):
        outs += [res[n][slot].reshape(wts[n].shape) for n in _NAMES]
    return tuple(outs)
```
